```python
import math
import jax, jax.numpy as jnp
from jax import lax
import numpy as np

D_MODEL = 1024
BATCH = 4
SEQ = 8192
DEPTH = 1
DEC_BATCH = 8
DEC_SEQ = 16
PAST_LEN = 4096

CHUNK = 64
Q_BLOCK = 128
MIX_DIM = D_MODEL
SSD_WIDTH = MIX_DIM // 2
SSD_HEADDIM = 64
SSD_HEADS = SSD_WIDTH // SSD_HEADDIM
SSD_GROUPS = 2
SSD_HPG = SSD_HEADS // SSD_GROUPS
SSD_STATE = 128
SSD_CONV = 4
SSD_CHUNK = CHUNK
SSD_CONV_DIM = SSD_WIDTH + 2 * SSD_GROUPS * SSD_STATE
DA_WIDTH = MIX_DIM - SSD_WIDTH
DA_DK = 64
DA_DV = 2 * DA_DK
DA_HEADS = DA_WIDTH // DA_DV
REL_BUCKETS = 32
REL_MAX_DIST = 128
D_FF = ((8 * D_MODEL // 3 + 127) // 128) * 128
FFN_CONV = 3
EPS = 1e-6
IN_DIM = SSD_WIDTH + SSD_CONV_DIM + SSD_HEADS + 3 * DA_WIDTH
SPLITS = [SSD_WIDTH,
          SSD_WIDTH + SSD_CONV_DIM,
          SSD_WIDTH + SSD_CONV_DIM + SSD_HEADS,
          SSD_WIDTH + SSD_CONV_DIM + SSD_HEADS + DA_WIDTH,
          SSD_WIDTH + SSD_CONV_DIM + SSD_HEADS + 2 * DA_WIDTH]

kernel_name = 'hybrid_ssd_diffattn_streaming_encoder_step'


def lambda_init(layer):
    return 0.8 - 0.6 * math.exp(-0.3 * layer)


def rmsnorm(x, w):
    xf = x.astype(jnp.float32)
    y = xf * lax.rsqrt(jnp.mean(xf * xf, axis=-1, keepdims=True) + EPS)
    return (y * w.astype(jnp.float32)).astype(x.dtype)


def causal_dwconv(u, hist, w, b):
    full = jnp.concatenate([hist.astype(u.dtype), u], axis=1)
    out = lax.conv_general_dilated(full, w[:, None, :].astype(u.dtype), window_strides=(1,),
                                   padding='VALID', dimension_numbers=('NWC', 'WIO', 'NWC'),
                                   feature_group_count=u.shape[-1])
    return out + b, full[:, full.shape[1] - (w.shape[0] - 1):]


def rel_bucket(rel):
    nb = REL_BUCKETS // 2
    max_exact = nb // 2
    n = jnp.abs(rel)
    nf = jnp.maximum(n, 1).astype(jnp.float32)
    large = max_exact + (jnp.log(nf / max_exact) / math.log(REL_MAX_DIST / max_exact)
                         * (nb - max_exact)).astype(jnp.int32)
    large = jnp.minimum(large, nb - 1)
    return jnp.where(rel > 0, nb, 0) + jnp.where(n < max_exact, n, large)


def ssd_scan(x, dt, a, bm, cm, h0, chunk):
    f32 = jnp.float32
    b, t, g, r, p = x.shape
    n = bm.shape[-1]
    c = t // chunk
    xd = (x.astype(f32) * dt[..., None]).reshape(b, c, chunk, g, r, p)
    acs = jnp.cumsum((dt * a).reshape(b, c, chunk, g, r), axis=2)
    bm = bm.astype(f32).reshape(b, c, chunk, g, n)
    cm = cm.astype(f32).reshape(b, c, chunk, g, n)
    causal = jnp.tril(jnp.ones((chunk, chunk), bool))[None, None, :, :, None, None]
    seg = acs[:, :, :, None] - acs[:, :, None, :]
    decay = jnp.exp(jnp.where(causal, seg, -jnp.inf))
    cb = jnp.einsum('bclgn,bcsgn->bclsg', cm, bm)
    y_diag = jnp.einsum('bclsg,bclsgr,bcsgrp->bclgrp', cb, decay, xd)
    decay_to_end = jnp.exp(acs[:, :, -1:] - acs)
    chunk_states = jnp.einsum('bclgn,bclgr,bclgrp->bcgrpn', bm, decay_to_end, xd)
    chunk_decay = jnp.exp(acs[:, :, -1])

    def step(h, inp):
        s, d = inp
        return h * d[..., None, None] + s, h

    h_last, h_in = lax.scan(step, h0.astype(f32),
                            (jnp.moveaxis(chunk_states, 1, 0), jnp.moveaxis(chunk_decay, 1, 0)))
    h_in = jnp.moveaxis(h_in, 0, 1)
    y_off = jnp.einsum('bclgn,bcgrpn,bclgr->bclgrp', cm, h_in, jnp.exp(acs))
    return (y_diag + y_off).reshape(b, t, g, r, p), h_last


def diff_attention(q, k, v, qpos, kpos, lam, rel_bias):
    s = jnp.einsum('bqhmd,bkhmd->bhmqk', q, k, preferred_element_type=jnp.float32) * (DA_DK ** -0.5)
    bias = rel_bias[rel_bucket(kpos[None, :] - qpos[:, None])].astype(jnp.float32)
    bias = jnp.transpose(bias, (2, 0, 1))[None, :, None]
    visible = (kpos[None, :] // CHUNK) <= (qpos[:, None] // CHUNK)
    p = jax.nn.softmax(jnp.where(visible, s + bias, -jnp.inf), axis=-1)
    attn = p[:, :, 0] - lam * p[:, :, 1]
    return jnp.einsum('bhqk,bkhd->bqhd', attn.astype(v.dtype), v)


def parallel_mixer(h, past_k, past_v, ssm_h0, conv_hist, w_in, conv_w, conv_b, dt_bias, a_log, d_skip,
                   ssm_norm_w, lq1, lk1, lq2, lk2, subln_w, lam_init, rel_bias, w_out):
    f32 = jnp.float32
    b, t, _ = h.shape
    past = past_k.shape[1]
    z, xbc, dt, q, k, v = jnp.split(h @ w_in, SPLITS, axis=-1)
    xbc, new_conv = causal_dwconv(xbc, conv_hist, conv_w, conv_b)
    xbc = jax.nn.silu(xbc)
    xs, bm, cm = jnp.split(xbc, [SSD_WIDTH, SSD_WIDTH + SSD_GROUPS * SSD_STATE], axis=-1)
    xs = xs.reshape(b, t, SSD_GROUPS, SSD_HPG, SSD_HEADDIM)
    bm = bm.reshape(b, t, SSD_GROUPS, SSD_STATE)
    cm = cm.reshape(b, t, SSD_GROUPS, SSD_STATE)
    dtp = jax.nn.softplus(dt.astype(f32) + dt_bias.astype(f32)).reshape(b, t, SSD_GROUPS, SSD_HPG)
    a = -jnp.exp(a_log.astype(f32)).reshape(SSD_GROUPS, SSD_HPG)
    h0 = ssm_h0.reshape(b, SSD_GROUPS, SSD_HPG, SSD_HEADDIM, SSD_STATE)
    y, h_last = ssd_scan(xs, dtp, a, bm, cm, h0, min(SSD_CHUNK, t))
    y = y + d_skip.astype(f32).reshape(SSD_GROUPS, SSD_HPG)[:, :, None] * xs.astype(f32)
    y = (y.reshape(b, t, SSD_WIDTH) * jax.nn.silu(z.astype(f32))).reshape(b, t, SSD_GROUPS, -1)
    y = y * lax.rsqrt(jnp.mean(y * y, axis=-1, keepdims=True) + EPS)
    y_ssd = (y.reshape(b, t, SSD_WIDTH) * ssm_norm_w.astype(f32)).astype(h.dtype)
    q = q.reshape(b, t, DA_HEADS, 2, DA_DK)
    k_rows = k.reshape(b, t, DA_HEADS, 2 * DA_DK)
    v_rows = v.reshape(b, t, DA_HEADS, DA_DV)
    k_all = jnp.concatenate([past_k.astype(h.dtype), k_rows], axis=1).reshape(b, past + t, DA_HEADS, 2, DA_DK)
    v_all = jnp.concatenate([past_v.astype(h.dtype), v_rows], axis=1)
    lam = (jnp.exp(jnp.sum(lq1.astype(f32) * lk1.astype(f32)))
           - jnp.exp(jnp.sum(lq2.astype(f32) * lk2.astype(f32))) + lam_init)
    kpos = jnp.arange(past + t)
    qpos = past + jnp.arange(t)
    if t > Q_BLOCK:
        nblk = t // Q_BLOCK
        qb = jnp.moveaxis(q.reshape(b, nblk, Q_BLOCK, DA_HEADS, 2, DA_DK), 1, 0)
        pb = qpos.reshape(nblk, Q_BLOCK)
        o = lax.map(lambda qp: diff_attention(qp[0], k_all, v_all, qp[1], kpos, lam, rel_bias), (qb, pb))
        o = jnp.moveaxis(o, 0, 1).reshape(b, t, DA_HEADS, DA_DV)
    else:
        o = diff_attention(q, k_all, v_all, qpos, kpos, lam, rel_bias)
    y_att = (rmsnorm(o, subln_w) * (1.0 - lam_init)).reshape(b, t, DA_WIDTH)
    out = jnp.concatenate([y_ssd, y_att.astype(h.dtype)], axis=-1) @ w_out
    return out, k_rows, v_rows, h_last.reshape(b, SSD_HEADS, SSD_HEADDIM, SSD_STATE).astype(h.dtype), new_conv


def conv_ffn(h, hist, w_up, conv_w, conv_b, w_down):
    u, new_hist = causal_dwconv(h @ w_up, hist, conv_w, conv_b)
    val, gate = jnp.split(u, 2, axis=-1)
    return (jax.nn.silu(gate) * val) @ w_down, new_hist


def run_group(x, c, past_k, past_v, ssm_h0, ssm_conv_hist, ffn_conv_hist, params, rel_bias, final_norm_w):
    (w_ada, b_ada, norm_mix_w, w_in, ssm_conv_w, ssm_conv_b, ssm_dt_bias, ssm_a_log, ssm_d, ssm_norm_w,
     lambda_q1, lambda_k1, lambda_q2, lambda_k2, attn_subln_w, w_out,
     norm_ffn_w, w_up, ffn_conv_w, ffn_conv_b, w_down) = params
    ks, vs, hs, cs, fs = [], [], [], [], []
    for l in range(DEPTH):
        mod = (jax.nn.silu(c) @ w_ada[l] + b_ada[l])[:, None, :]
        sh_m, sc_m, g_m, sh_f, sc_f, g_f = jnp.split(mod, 6, axis=-1)
        h = rmsnorm(x, norm_mix_w[l]) * (1.0 + sc_m) + sh_m
        mix, k_rows, v_rows, h_last, conv_new = parallel_mixer(
            h, past_k[l], past_v[l], ssm_h0[l], ssm_conv_hist[l], w_in[l], ssm_conv_w[l], ssm_conv_b[l],
            ssm_dt_bias[l], ssm_a_log[l], ssm_d[l], ssm_norm_w[l], lambda_q1[l], lambda_k1[l],
            lambda_q2[l], lambda_k2[l], attn_subln_w[l], lambda_init(l), rel_bias, w_out[l])
        x = x + g_m * mix
        h = rmsnorm(x, norm_ffn_w[l]) * (1.0 + sc_f) + sh_f
        f, ffn_new = conv_ffn(h, ffn_conv_hist[l], w_up[l], ffn_conv_w[l], ffn_conv_b[l], w_down[l])
        x = x + g_f * f
        ks.append(k_rows)
        vs.append(v_rows)
        hs.append(h_last)
        cs.append(conv_new)
        fs.append(ffn_new)
    y = rmsnorm(x, final_norm_w)
    return y, jnp.stack(ks), jnp.stack(vs), jnp.stack(hs), jnp.stack(cs), jnp.stack(fs)


def setup_inputs(seed: int = 0) -> dict:
    key = jax.random.key(seed)
    ks = iter(jax.random.split(key, 48))
    f32 = jnp.float32

    def nrm(shape, scale):
        return jax.random.normal(next(ks), shape, f32) * scale

    def gain(shape):
        return 1.0 + nrm(shape, 0.05)

    dt0 = jnp.exp(jax.random.uniform(next(ks), (DEPTH, SSD_HEADS), f32, math.log(1e-3), math.log(1e-1)))
    a0 = jax.random.uniform(next(ks), (DEPTH, SSD_HEADS), f32, 1.0, 16.0)
    return {
        'x_prompt': nrm((BATCH, SEQ, D_MODEL), 1.0),
        'x_sample': nrm((DEC_BATCH, DEC_SEQ, D_MODEL), 1.0),
        'c_prompt': nrm((BATCH, D_MODEL), 1.0),
        'c_sample': nrm((DEC_BATCH, D_MODEL), 1.0),
        'cache_k': nrm((DEPTH, DEC_BATCH, PAST_LEN, DA_HEADS, 2 * DA_DK), 1.0),
        'cache_v': nrm((DEPTH, DEC_BATCH, PAST_LEN, DA_HEADS, DA_DV), 1.0),
        'state_ssm': nrm((DEPTH, DEC_BATCH, SSD_HEADS, SSD_HEADDIM, SSD_STATE), 0.5),
        'state_ssm_conv': nrm((DEPTH, DEC_BATCH, SSD_CONV - 1, SSD_CONV_DIM), 1.0),
        'state_ffn_conv': nrm((DEPTH, DEC_BATCH, FFN_CONV - 1, 2 * D_FF), 1.0),
        'w_ada': nrm((DEPTH, D_MODEL, 6 * D_MODEL), 0.5 * D_MODEL ** -0.5),
        'b_ada': nrm((DEPTH, 6 * D_MODEL), 0.02),
        'norm_mix_w': gain((DEPTH, D_MODEL)),
        'w_in': nrm((DEPTH, D_MODEL, IN_DIM), D_MODEL ** -0.5),
        'ssm_conv_w': nrm((DEPTH, SSD_CONV, SSD_CONV_DIM), SSD_CONV ** -0.5),
        'ssm_conv_b': nrm((DEPTH, SSD_CONV_DIM), 0.02),
        'ssm_dt_bias': dt0 + jnp.log(-jnp.expm1(-dt0)),
        'ssm_a_log': jnp.log(a0),
        'ssm_d': gain((DEPTH, SSD_HEADS)),
        'ssm_norm_w': gain((DEPTH, SSD_WIDTH)),
        'lambda_q1': nrm((DEPTH, DA_DK), 0.1),
        'lambda_k1': nrm((DEPTH, DA_DK), 0.1),
        'lambda_q2': nrm((DEPTH, DA_DK), 0.1),
        'lambda_k2': nrm((DEPTH, DA_DK), 0.1),
        'attn_subln_w': gain((DEPTH, DA_DV)),
        'rel_bias': nrm((REL_BUCKETS, DA_HEADS), 0.5),
        'w_out': nrm((DEPTH, MIX_DIM, D_MODEL), MIX_DIM ** -0.5),
        'norm_ffn_w': gain((DEPTH, D_MODEL)),
        'w_up': nrm((DEPTH, D_MODEL, 2 * D_FF), D_MODEL ** -0.5),
        'ffn_conv_w': nrm((DEPTH, FFN_CONV, 2 * D_FF), FFN_CONV ** -0.5),
        'ffn_conv_b': nrm((DEPTH, 2 * D_FF), 0.02),
        'w_down': nrm((DEPTH, D_FF, D_MODEL), D_FF ** -0.5),
        'final_norm_w': gain((D_MODEL,)),
    }


def reference(x_prompt, x_sample, c_prompt, c_sample, cache_k, cache_v, state_ssm, state_ssm_conv,
              state_ffn_conv, w_ada, b_ada, norm_mix_w, w_in, ssm_conv_w, ssm_conv_b, ssm_dt_bias, ssm_a_log,
              ssm_d, ssm_norm_w, lambda_q1, lambda_k1, lambda_q2, lambda_k2, attn_subln_w, rel_bias, w_out,
              norm_ffn_w, w_up, ffn_conv_w, ffn_conv_b, w_down, final_norm_w):
    params = (w_ada, b_ada, norm_mix_w, w_in, ssm_conv_w, ssm_conv_b, ssm_dt_bias, ssm_a_log, ssm_d,
              ssm_norm_w, lambda_q1, lambda_k1, lambda_q2, lambda_k2, attn_subln_w, w_out,
              norm_ffn_w, w_up, ffn_conv_w, ffn_conv_b, w_down)
    bp = x_prompt.shape[0]
    dtp = x_prompt.dtype
    empty_k = jnp.zeros((DEPTH, bp, 0, DA_HEADS, 2 * DA_DK), dtp)
    empty_v = jnp.zeros((DEPTH, bp, 0, DA_HEADS, DA_DV), dtp)
    zero_ssm = jnp.zeros((DEPTH, bp, SSD_HEADS, SSD_HEADDIM, SSD_STATE), dtp)
    zero_conv = jnp.zeros((DEPTH, bp, SSD_CONV - 1, SSD_CONV_DIM), dtp)
    zero_fconv = jnp.zeros((DEPTH, bp, FFN_CONV - 1, 2 * D_FF), dtp)
    y_prompt, k_p, v_p, ssm_p, conv_p, fconv_p = run_group(
        x_prompt, c_prompt, empty_k, empty_v, zero_ssm, zero_conv, zero_fconv, params, rel_bias, final_norm_w)
    y_sample, k_s, v_s, ssm_s, conv_s, fconv_s = run_group(
        x_sample, c_sample, cache_k, cache_v, state_ssm, state_ssm_conv, state_ffn_conv, params, rel_bias,
        final_norm_w)
    return (y_prompt, y_sample, k_p, v_p, ssm_p, conv_p, fconv_p, k_s, v_s, ssm_s, conv_s, fconv_s)
```

```python
import functools
import math

import numpy as np
import jax
import jax.numpy as jnp
from jax import lax
from jax.experimental import pallas as pl
from jax.experimental.pallas import tpu as pltpu

F32 = jnp.float32
BF16 = jnp.bfloat16
HIGHEST = lax.Precision.HIGHEST

D_MODEL = 1024
CHUNK = 64
SSD_WIDTH = 512
SSD_HEADDIM = 64
SSD_HEADS = 8
SSD_GROUPS = 2
SSD_HPG = 4
SSD_STATE = 128
SSD_CONV = 4
SSD_CONV_DIM = SSD_WIDTH + 2 * SSD_GROUPS * SSD_STATE
GROUP_W = SSD_HPG * SSD_HEADDIM
DA_WIDTH = 512
DA_DK = 64
DA_DV = 128
DA_HEADS = 4
REL_BUCKETS = 32
REL_MAX_DIST = 128
D_FF = 2816
FFN_CONV = 3
EPS = 1e-6
IN_SPLITS = (512, 1536, 1544, 2056, 2568)
LANES = 128
SUBLANES = 8
SUPER = 128
FFN_CN = 256
FFN_NC = D_FF // FFN_CN
NEG = -1e30
VMEM_LIMIT = 56 * 1024 * 1024

PZ, PX, PDT, PQ, PK, PV, PEND = 0, 512, 1536, 1664, 2176, 2688, 3200


def _silu(x):
    return x / (1.0 + jnp.exp(-x))


def _softplus(x):
    return jnp.maximum(x, 0.0) + jnp.log1p(jnp.exp(-jnp.abs(x)))


def _rms(x, w):
    return x * lax.rsqrt(jnp.mean(x * x, axis=-1, keepdims=True) + EPS) * w


def _const_spec(shape):
    nd = len(shape)
    return pl.BlockSpec(shape, lambda *_: (0,) * nd)


def _mod_kernel(c_ref, w_ref, b_ref, o_ref):
    a = _silu(c_ref[...]).astype(BF16)
    o_ref[...] = jnp.dot(a, w_ref[...].astype(BF16), preferred_element_type=F32) + b_ref[...]


def _mod_call(c, w_ada, b_ada):
    n, d = c.shape
    nout = w_ada.shape[1]
    tn = 1024
    return pl.pallas_call(
        _mod_kernel,
        grid=(nout // tn,),
        in_specs=[pl.BlockSpec((n, d), lambda j: (0, 0)),
                  pl.BlockSpec((d, tn), lambda j: (0, j)),
                  pl.BlockSpec((1, tn), lambda j: (0, j))],
        out_specs=pl.BlockSpec((n, tn), lambda j: (0, j)),
        out_shape=jax.ShapeDtypeStruct((n, nout), F32),
        name="mod",
    )(c, w_ada, b_ada)


def _inproj_kernel(x_ref, mod_ref, nw_ref, w_ref, z_ref, xbc_ref, dt_ref, q_ref, k_ref, v_ref, kb_ref, vb_ref):
    h = _rms(x_ref[...], nw_ref[...]) * (1.0 + mod_ref[1:2, :]) + mod_ref[0:1, :]
    hb = h.astype(BF16)

    def proj(a, b):
        return jnp.dot(hb, w_ref[:, a:b], preferred_element_type=F32)

    z_ref[...] = proj(PZ, PX).astype(BF16)
    xbc_ref[...] = proj(PX, PDT).astype(BF16)
    dt_ref[...] = proj(PDT, PQ)
    q_ref[...] = proj(PQ, PK).astype(BF16)
    k = proj(PK, PV)
    k_ref[...] = k
    kb_ref[...] = k.astype(BF16)
    v = proj(PV, PEND)
    v_ref[...] = v
    vb_ref[...] = v.astype(BF16)


def _inproj_call(x, mod3, norm_w, w_cat, tm):
    b, t, d = x.shape

    def row(width):
        return pl.BlockSpec((None, tm, width), lambda i, j: (i, j, 0))

    def out(width, dtype):
        return jax.ShapeDtypeStruct((b, t, width), dtype)

    return pl.pallas_call(
        _inproj_kernel,
        grid=(b, t // tm),
        in_specs=[row(d),
                  pl.BlockSpec((None, 6, d), lambda i, j: (i, 0, 0)),
                  _const_spec((1, d)),
                  _const_spec((d, PEND))],
        out_specs=[row(512), row(1024), row(LANES), row(512), row(512), row(512), row(512), row(512)],
        out_shape=[out(512, BF16), out(1024, BF16), out(LANES, F32), out(512, BF16),
                   out(512, F32), out(512, F32), out(512, BF16), out(512, BF16)],
        compiler_params=pltpu.CompilerParams(dimension_semantics=("parallel", "parallel"),
                                             vmem_limit_bytes=VMEM_LIMIT),
        name="inproj",
    )(x, mod3, norm_w, w_cat)


def _ssd_kernel(z_ref, xbc_ref, dt_ref, h0_ref, hist_ref, cw_ref, cbias_ref, dtb_ref, alog_ref, dsk_ref, nw_ref,
                tri_ref, e_ref, y_ref, hout_ref, cout_ref, h_scr, cbuf, ybuf, *, chunk, rows, real):
    t = pl.program_id(1)
    nconv = SSD_CONV - 1

    @pl.when(t == 0)
    def _init():
        h_scr[...] = h0_ref[...]
        cbuf[0:SUBLANES, :] = jnp.zeros((SUBLANES, SSD_CONV_DIM), F32)
        cbuf[SUBLANES - nconv:SUBLANES, :] = hist_ref[...]

    cbuf[SUBLANES:SUBLANES + rows, :] = xbc_ref[...].astype(F32)
    conv = cbias_ref[...]
    for j in range(SSD_CONV):
        off = SUBLANES - nconv + j
        conv = conv + cw_ref[j:j + 1, :] * cbuf[off:off + rows, :]
    tail = cbuf[SUBLANES - nconv + real:SUBLANES + real, :]
    cout_ref[...] = tail
    cbuf[SUBLANES - nconv:SUBLANES, :] = tail
    xc = _silu(conv)
    xs = xc[:, 0:SSD_WIDTH]

    dtv = _softplus(dt_ref[...] + dtb_ref[...])
    da = dtv * (-jnp.exp(alog_ref[...]))
    acs = jnp.dot(tri_ref[...], da, precision=HIGHEST, preferred_element_type=F32)
    dt_x = jnp.dot(dtv, e_ref[...], precision=HIGHEST, preferred_element_type=F32)
    acs_x = jnp.dot(acs, e_ref[...], precision=HIGHEST, preferred_element_type=F32)
    eacs_x = jnp.exp(acs_x)
    xd = xs * dt_x

    li = lax.broadcasted_iota(jnp.int32, (SUPER, SUPER), 0)
    si = lax.broadcasted_iota(jnp.int32, (SUPER, SUPER), 1)
    cshift = chunk.bit_length() - 1
    mask2 = ((li >> cshift) == (si >> cshift)) & (si <= li)
    lane_g = lax.broadcasted_iota(jnp.int32, (SUPER, GROUP_W), 1) >> (SSD_HEADDIM.bit_length() - 1)

    for sb in range(rows // SUPER):
        o = sb * SUPER
        nreal = (min(real, o + SUPER) - o) // chunk
        acs2 = acs[o:o + SUPER, :]
        acs_t = acs2.T
        for g in range(SSD_GROUPS):
            gs = slice(g * GROUP_W, (g + 1) * GROUP_W)
            bm2 = xc[o:o + SUPER, SSD_WIDTH + g * SSD_STATE:SSD_WIDTH + (g + 1) * SSD_STATE]
            cm2 = xc[o:o + SUPER, SSD_WIDTH + (SSD_GROUPS + g) * SSD_STATE:SSD_WIDTH + (SSD_GROUPS + g + 1) * SSD_STATE]
            bmb = bm2.astype(BF16)
            cmb = cm2.astype(BF16)
            cb2 = lax.dot_general(cmb, bmb, (((1,), (1,)), ((), ())), preferred_element_type=F32)
            bm_t = bm2.T.astype(BF16)
            xd_g = xd[o:o + SUPER, gs]
            ms, xm = [], []
            for rr in range(SSD_HPG):
                r = g * SSD_HPG + rr
                seg = acs2[:, r:r + 1] - acs_t[r:r + 1, :]
                dec = jnp.where(mask2, jnp.exp(jnp.where(mask2, seg, 0.0)), 0.0)
                ms.append((cb2 * dec).astype(BF16))
                xm.append(jnp.where(lane_g == rr, xd_g, 0.0).astype(BF16))
            ybuf[o:o + SUPER, gs] = jnp.dot(jnp.concatenate(ms, axis=1), jnp.concatenate(xm, axis=0),
                                            preferred_element_type=F32)
            for j in range(nreal):
                a0, a1 = o + j * chunk, o + (j + 1) * chunk
                h_t = h_scr[g]
                yoff = jnp.dot(cmb[j * chunk:(j + 1) * chunk, :], h_t.astype(BF16), preferred_element_type=F32)
                ybuf[a0:a1, gs] = ybuf[a0:a1, gs] + yoff * eacs_x[a0:a1, gs]
                dte = jnp.exp(acs_x[a1 - 1:a1, gs] - acs_x[a0:a1, gs])
                xw = (xd[a0:a1, gs] * dte).astype(BF16)
                pieces = []
                if j > 0:
                    pieces.append(jnp.zeros((j * chunk, GROUP_W), BF16))
                pieces.append(xw)
                if (j + 1) * chunk < SUPER:
                    pieces.append(jnp.zeros((SUPER - (j + 1) * chunk, GROUP_W), BF16))
                xw2 = jnp.concatenate(pieces, axis=0) if len(pieces) > 1 else xw
                st = jnp.dot(bm_t, xw2, preferred_element_type=F32)
                h_scr[g] = h_t * eacs_x[a1 - 1:a1, gs] + st

    y = ybuf[...] + dsk_ref[...] * xs
    y = y * _silu(z_ref[...].astype(F32))
    for g in range(SSD_GROUPS):
        gs = slice(g * GROUP_W, (g + 1) * GROUP_W)
        y_ref[:, gs] = _rms(y[:, gs], nw_ref[:, gs]).astype(BF16)

    @pl.when(t == pl.num_programs(1) - 1)
    def _fin():
        hout_ref[...] = h_scr[...]


def _ssd_call(z, xbc, dt, h0_t, hist, cw, cbias, dtb, alog, dsk, nw, *, chunk, rows, real):
    b, t, _ = z.shape
    ii = np.arange(rows)
    tri = ((ii[:, None] // chunk == ii[None, :] // chunk) & (ii[None, :] <= ii[:, None])).astype(np.float32)
    e = np.zeros((LANES, SSD_WIDTH), np.float32)
    for r in range(SSD_HEADS):
        e[r, r * SSD_HEADDIM:(r + 1) * SSD_HEADDIM] = 1.0

    def row(width):
        return pl.BlockSpec((None, rows, width), lambda i, j: (i, j, 0))

    kern = functools.partial(_ssd_kernel, chunk=chunk, rows=rows, real=real)
    return pl.pallas_call(
        kern,
        grid=(b, t // rows),
        in_specs=[row(SSD_WIDTH), row(SSD_CONV_DIM), row(LANES),
                  pl.BlockSpec((None, SSD_GROUPS, SSD_STATE, GROUP_W), lambda i, j: (i, 0, 0, 0)),
                  pl.BlockSpec((None, SSD_CONV - 1, SSD_CONV_DIM), lambda i, j: (i, 0, 0)),
                  _const_spec((SSD_CONV, SSD_CONV_DIM)), _const_spec((1, SSD_CONV_DIM)),
                  _const_spec((1, LANES)), _const_spec((1, LANES)),
                  _const_spec((1, SSD_WIDTH)), _const_spec((1, SSD_WIDTH)),
                  _const_spec((rows, rows)), _const_spec((LANES, SSD_WIDTH))],
        out_specs=[row(SSD_WIDTH),
                   pl.BlockSpec((None, SSD_GROUPS, SSD_STATE, GROUP_W), lambda i, j: (i, 0, 0, 0)),
                   pl.BlockSpec((None, SSD_CONV - 1, SSD_CONV_DIM), lambda i, j: (i, 0, 0))],
        out_shape=[jax.ShapeDtypeStruct((b, t, SSD_WIDTH), BF16),
                   jax.ShapeDtypeStruct((b, SSD_GROUPS, SSD_STATE, GROUP_W), F32),
                   jax.ShapeDtypeStruct((b, SSD_CONV - 1, SSD_CONV_DIM), F32)],
        scratch_shapes=[pltpu.VMEM((SSD_GROUPS, SSD_STATE, GROUP_W), F32),
                        pltpu.VMEM((rows + SUBLANES, SSD_CONV_DIM), F32),
                        pltpu.VMEM((rows, SSD_WIDTH), F32)],
        compiler_params=pltpu.CompilerParams(dimension_semantics=("parallel", "arbitrary"),
                                             vmem_limit_bytes=VMEM_LIMIT),
        name="ssd",
    )(z, xbc, dt, h0_t, hist, cw, cbias, dtb, alog, dsk, nw, jnp.asarray(tri), jnp.asarray(e))


def _attn_kernel(scal_ref, q_ref, k_ref, v_ref, bias_ref, subw_ref, o_ref, m_scr, l_scr, acc_scr,
                 *, bq, bk, noff, out_scale):
    h = pl.program_id(1)
    qi = pl.program_id(2)
    kn0 = qi + (noff - 1)
    lam = scal_ref[0]
    cfar = scal_ref[1 + h]

    q = q_ref[...]
    lane = lax.broadcasted_iota(jnp.int32, (bq, DA_DV), 1)
    zero = jnp.zeros_like(q)
    qz = (jnp.where(lane < DA_DK, q, zero), jnp.where(lane >= DA_DK, q, zero))

    m_scr[...] = jnp.full(m_scr.shape, NEG, F32)
    l_scr[...] = jnp.zeros(l_scr.shape, F32)
    acc_scr[...] = jnp.zeros(acc_scr.shape, F32)

    def block(ki, bias, shift):
        start = pl.multiple_of(ki * bk, bk)
        k = k_ref[pl.ds(start, bk), :]
        v = v_ref[pl.ds(start, bk), :]
        for mm in range(2):
            s = lax.dot_general(qz[mm], k, (((1,), (1,)), ((), ())), preferred_element_type=F32)
            if bias is not None:
                s = s + bias
            m_old = m_scr[mm]
            m_new = jnp.maximum(m_old, jnp.max(s, axis=-1, keepdims=True) + shift)
            p = jnp.exp(s - (m_new - shift))
            alpha = jnp.exp(m_old - m_new)
            l_scr[mm] = alpha * l_scr[mm] + jnp.sum(p, axis=-1, keepdims=True)
            acc_scr[mm] = alpha * acc_scr[mm] + jnp.dot(p.astype(BF16), v, preferred_element_type=F32)
            m_scr[mm] = m_new

    def far_body(ki, carry):
        block(ki, None, cfar)
        return carry

    lax.fori_loop(0, jnp.maximum(kn0, 0), far_body, 0)

    @pl.when(kn0 >= 0)
    def _near0():
        block(kn0, bias_ref[0], 0.0)

    block(kn0 + 1, bias_ref[1], 0.0)

    o = acc_scr[0] / l_scr[0] - lam * (acc_scr[1] / l_scr[1])
    o_ref[...] = (_rms(o, subw_ref[...]) * out_scale).astype(BF16)


def _attn_call(scal, q, kb, vb, bias, subw, *, bq, bk, noff, out_scale):
    b, tq, _ = q.shape
    tk = kb.shape[1]
    kern = functools.partial(_attn_kernel, bq=bq, bk=bk, noff=noff, out_scale=out_scale)
    return pl.pallas_call(
        kern,
        grid=(b, DA_HEADS, tq // bq),
        in_specs=[pl.BlockSpec(memory_space=pltpu.SMEM),
                  pl.BlockSpec((None, bq, DA_DV), lambda i, h, j: (i, j, h)),
                  pl.BlockSpec((None, tk, DA_DV), lambda i, h, j: (i, 0, h)),
                  pl.BlockSpec((None, tk, DA_DV), lambda i, h, j: (i, 0, h)),
                  pl.BlockSpec((None, 2, bq, bk), lambda i, h, j: (h, 0, 0, 0)),
                  pl.BlockSpec((1, DA_DV), lambda i, h, j: (0, 0))],
        out_specs=pl.BlockSpec((None, bq, DA_DV), lambda i, h, j: (i, j, h)),
        out_shape=jax.ShapeDtypeStruct((b, tq, DA_WIDTH), BF16),
        scratch_shapes=[pltpu.VMEM((2, bq, 1), F32), pltpu.VMEM((2, bq, 1), F32),
                        pltpu.VMEM((2, bq, DA_DV), F32)],
        compiler_params=pltpu.CompilerParams(dimension_semantics=("parallel", "parallel", "arbitrary"),
                                             vmem_limit_bytes=VMEM_LIMIT),
        name="attn",
    )(scal, q, kb, vb, bias, subw)


def _rel_bucket(rel):
    nb = REL_BUCKETS // 2
    max_exact = nb // 2
    n = jnp.abs(rel)
    nf = jnp.maximum(n, 1).astype(jnp.float32)
    large = max_exact + (jnp.log(nf / max_exact) / math.log(REL_MAX_DIST / max_exact)
                         * (nb - max_exact)).astype(jnp.int32)
    large = jnp.minimum(large, nb - 1)
    return jnp.where(rel > 0, nb, 0) + jnp.where(n < max_exact, n, large)


def _bias_tiles(rel_bias, qpos0, kpos0, bq, bk, tk_real):
    qpos = qpos0 + np.arange(bq)
    tiles = []
    for d in range(2):
        kpos = kpos0[d] + np.arange(bk)
        rel = kpos[None, :] - qpos[:, None]
        vis = (kpos[None, :] // CHUNK <= qpos[:, None] // CHUNK) & (kpos[None, :] < tk_real)
        bias = rel_bias[_rel_bucket(jnp.asarray(rel, jnp.int32))].astype(F32)
        tiles.append(jnp.where(jnp.asarray(vis)[:, :, None], bias, NEG))
    return jnp.transpose(jnp.stack(tiles), (3, 0, 1, 2))


def _ffn_kernel(x_ref, ys_ref, ya_ref, mod_ref, wo1_ref, wo2_ref, nfw_ref, wuv_ref, wug_ref, cwv_ref, cwg_ref,
                cbv_ref, cbg_ref, wd_ref, fw_ref, hv_ref, hg_ref, y_ref, tv_ref, tg_ref,
                tailv, tailg, bufv, bufg, x1_scr, h2_scr, acc_scr, *, tm, real):
    t = pl.program_id(1)
    nh = FFN_CONV - 1
    lo = SUBLANES - nh

    @pl.when(t == 0)
    def _init():
        tailv[:, lo:SUBLANES, :] = hv_ref[...]
        tailg[:, lo:SUBLANES, :] = hg_ref[...]

    mix = (jnp.dot(ys_ref[...], wo1_ref[...], preferred_element_type=F32)
           + jnp.dot(ya_ref[...], wo2_ref[...], preferred_element_type=F32))
    x1 = x_ref[...] + mod_ref[2:3, :] * mix
    x1_scr[...] = x1
    h2 = _rms(x1, nfw_ref[...]) * (1.0 + mod_ref[4:5, :]) + mod_ref[3:4, :]
    h2_scr[...] = h2.astype(BF16)
    acc_scr[...] = jnp.zeros(acc_scr.shape, F32)

    def conv(u, buf, tail, cw_ref, cb_ref, j):
        buf[lo:SUBLANES, :] = tail[j, lo:SUBLANES, :]
        buf[SUBLANES:SUBLANES + tm, :] = u
        cw = cw_ref[j]
        c = cb_ref[j]
        for i in range(FFN_CONV):
            c = c + cw[i:i + 1, :] * buf[lo + i:lo + i + tm, :]
        tail[j, lo:SUBLANES, :] = buf[lo + real:SUBLANES + real, :]
        return c

    def body(j, carry):
        hb = h2_scr[...]
        uv = jnp.dot(hb, wuv_ref[j], preferred_element_type=F32)
        ug = jnp.dot(hb, wug_ref[j], preferred_element_type=F32)
        cv = conv(uv, bufv, tailv, cwv_ref, cbv_ref, j)
        cg = conv(ug, bufg, tailg, cwg_ref, cbg_ref, j)
        act = (_silu(cg) * cv).astype(BF16)
        acc_scr[...] += jnp.dot(act, wd_ref[j], preferred_element_type=F32)
        return carry

    lax.fori_loop(0, FFN_NC, body, 0)
    x2 = x1_scr[...] + mod_ref[5:6, :] * acc_scr[...]
    y_ref[...] = _rms(x2, fw_ref[...])
    tv_ref[...] = tailv[:, lo:SUBLANES, :]
    tg_ref[...] = tailg[:, lo:SUBLANES, :]


def _ffn_call(x, ys, ya, mod3, wo1, wo2, nfw, wuv, wug, cwv, cwg, cbv, cbg, wd, fw, hv, hg, *, tm, real):
    b, t, d = x.shape
    nh = FFN_CONV - 1

    def row(width):
        return pl.BlockSpec((None, tm, width), lambda i, j: (i, j, 0))

    hist_spec = pl.BlockSpec((None, FFN_NC, nh, FFN_CN), lambda i, j: (i, 0, 0, 0))
    kern = functools.partial(_ffn_kernel, tm=tm, real=real)
    return pl.pallas_call(
        kern,
        grid=(b, t // tm),
        in_specs=[row(d), row(SSD_WIDTH), row(DA_WIDTH),
                  pl.BlockSpec((None, 6, d), lambda i, j: (i, 0, 0)),
                  _const_spec((SSD_WIDTH, d)), _const_spec((DA_WIDTH, d)), _const_spec((1, d)),
                  _const_spec((FFN_NC, d, FFN_CN)), _const_spec((FFN_NC, d, FFN_CN)),
                  _const_spec((FFN_NC, FFN_CONV, FFN_CN)), _const_spec((FFN_NC, FFN_CONV, FFN_CN)),
                  _const_spec((FFN_NC, 1, FFN_CN)), _const_spec((FFN_NC, 1, FFN_CN)),
                  _const_spec((FFN_NC, FFN_CN, d)), _const_spec((1, d)),
                  hist_spec, hist_spec],
        out_specs=[row(d), hist_spec, hist_spec],
        out_shape=[jax.ShapeDtypeStruct((b, t, d), F32),
                   jax.ShapeDtypeStruct((b, FFN_NC, nh, FFN_CN), F32),
                   jax.ShapeDtypeStruct((b, FFN_NC, nh, FFN_CN), F32)],
        scratch_shapes=[pltpu.VMEM((FFN_NC, SUBLANES, FFN_CN), F32), pltpu.VMEM((FFN_NC, SUBLANES, FFN_CN), F32),
                        pltpu.VMEM((tm + SUBLANES, FFN_CN), F32), pltpu.VMEM((tm + SUBLANES, FFN_CN), F32),
                        pltpu.VMEM((tm, d), F32), pltpu.VMEM((tm, d), BF16), pltpu.VMEM((tm, d), F32)],
        compiler_params=pltpu.CompilerParams(dimension_semantics=("parallel", "arbitrary"),
                                             vmem_limit_bytes=VMEM_LIMIT),
        name="ffn",
    )(x, ys, ya, mod3, wo1, wo2, nfw, wuv, wug, cwv, cwg, cbv, cbg, wd, fw, hv, hg)


def _pack_params(norm_mix_w, w_in, ssm_conv_w, ssm_conv_b, ssm_dt_bias, ssm_a_log, ssm_d, ssm_norm_w,
                 lambda_q1, lambda_k1, lambda_q2, lambda_k2, attn_subln_w, rel_bias, w_out,
                 norm_ffn_w, w_up, ffn_conv_w, ffn_conv_b, w_down, final_norm_w, layer):
    l = layer
    wz, wx, wdt, wq, wk, wv = jnp.split(w_in[l], IN_SPLITS, axis=-1)
    wdt = jnp.pad(wdt, ((0, 0), (0, LANES - SSD_HEADS)))
    w_cat = jnp.concatenate([wz, wx, wdt, wq * (DA_DK ** -0.5), wk, wv], axis=-1).astype(BF16)

    def pad_heads(v):
        return jnp.pad(v.astype(F32), (0, LANES - SSD_HEADS)).reshape(1, LANES)

    def chunks(v):
        return jnp.moveaxis(v.reshape(v.shape[:-1] + (FFN_NC, FFN_CN)), -2, 0)

    lam_init = 0.8 - 0.6 * math.exp(-0.3 * l)
    lam = (jnp.exp(jnp.sum(lambda_q1[l].astype(F32) * lambda_k1[l].astype(F32)))
           - jnp.exp(jnp.sum(lambda_q2[l].astype(F32) * lambda_k2[l].astype(F32))) + lam_init)
    far_bias = rel_bias[REL_BUCKETS // 2 - 1].astype(F32)
    return dict(
        norm_mix_w=norm_mix_w[l].reshape(1, D_MODEL), w_cat=w_cat,
        cw=ssm_conv_w[l], cbias=ssm_conv_b[l].reshape(1, SSD_CONV_DIM),
        dtb=pad_heads(ssm_dt_bias[l]), alog=pad_heads(ssm_a_log[l]),
        dsk=jnp.repeat(ssm_d[l].astype(F32), SSD_HEADDIM).reshape(1, SSD_WIDTH),
        ssm_nw=ssm_norm_w[l].reshape(1, SSD_WIDTH),
        scal=jnp.concatenate([lam.reshape(1), far_bias]).astype(F32), lam_init=lam_init,
        subw=attn_subln_w[l].reshape(1, DA_DV), rel_bias=rel_bias,
        wo1=w_out[l][:SSD_WIDTH].astype(BF16), wo2=w_out[l][SSD_WIDTH:].astype(BF16),
        nfw=norm_ffn_w[l].reshape(1, D_MODEL),
        wuv=chunks(w_up[l][:, :D_FF]).astype(BF16), wug=chunks(w_up[l][:, D_FF:]).astype(BF16),
        cwv=chunks(ffn_conv_w[l][:, :D_FF]), cwg=chunks(ffn_conv_w[l][:, D_FF:]),
        cbv=chunks(ffn_conv_b[l][:D_FF].reshape(1, D_FF)), cbg=chunks(ffn_conv_b[l][D_FF:].reshape(1, D_FF)),
        wd=w_down[l].reshape(FFN_NC, FFN_CN, D_MODEL).astype(BF16),
        fw=final_norm_w.reshape(1, D_MODEL),
    )


def _state_to_kernel(h):
    b = h.shape[0]
    h = h.reshape(b, SSD_GROUPS, SSD_HPG, SSD_HEADDIM, SSD_STATE)
    return jnp.transpose(h, (0, 1, 4, 2, 3)).reshape(b, SSD_GROUPS, SSD_STATE, GROUP_W)


def _state_from_kernel(h):
    b = h.shape[0]
    h = h.reshape(b, SSD_GROUPS, SSD_STATE, SSD_HPG, SSD_HEADDIM)
    return jnp.transpose(h, (0, 1, 3, 4, 2)).reshape(b, SSD_HEADS, SSD_HEADDIM, SSD_STATE)


def _ffn_hist_to_kernel(hist):
    b, nh, _ = hist.shape
    hv = hist[:, :, :D_FF].reshape(b, nh, FFN_NC, FFN_CN)
    hg = hist[:, :, D_FF:].reshape(b, nh, FFN_NC, FFN_CN)
    return jnp.transpose(hv, (0, 2, 1, 3)), jnp.transpose(hg, (0, 2, 1, 3))


def _ffn_hist_from_kernel(tv, tg):
    b, _, nh, _ = tv.shape
    tv = jnp.transpose(tv, (0, 2, 1, 3)).reshape(b, nh, D_FF)
    tg = jnp.transpose(tg, (0, 2, 1, 3)).reshape(b, nh, D_FF)
    return jnp.concatenate([tv, tg], axis=-1)


def _run_group(x, mod, past_k, past_v, ssm_h0, ssm_conv_hist, ffn_conv_hist, p, *, tm, ssd_rows, bq, bk):
    b, t, d = x.shape
    past = 0 if past_k is None else past_k.shape[1]
    chunk = min(CHUNK, t)
    tp = max(t, SUPER)
    if tp != t:
        x = jnp.pad(x, ((0, 0), (0, tp - t), (0, 0)))
        tm = ssd_rows = bq = tp
    mod3 = mod.reshape(b, 6, d)

    z, xbc, dt, q, k, v, kb, vb = _inproj_call(x, mod3, p["norm_mix_w"], p["w_cat"], tm)

    y_ssd, h_t, conv_new = _ssd_call(z, xbc, dt, _state_to_kernel(ssm_h0.astype(F32)), ssm_conv_hist.astype(F32),
                                     p["cw"], p["cbias"], p["dtb"], p["alog"], p["dsk"], p["ssm_nw"],
                                     chunk=chunk, rows=ssd_rows, real=min(t, ssd_rows))

    if past == 0:
        assert bq == bk and t % bq == 0
        k_all, v_all, noff = kb, vb, 0
        bias = _bias_tiles(p["rel_bias"], bq, (0, bq), bq, bk, 2 * bq)
    else:
        assert past % bk == 0 and t <= bk
        pad = bk - t
        k_all = jnp.concatenate([past_k.reshape(b, past, DA_WIDTH).astype(BF16), kb[:, :t],
                                 jnp.zeros((b, pad, DA_WIDTH), BF16)], axis=1)
        v_all = jnp.concatenate([past_v.reshape(b, past, DA_WIDTH).astype(BF16), vb[:, :t],
                                 jnp.zeros((b, pad, DA_WIDTH), BF16)], axis=1)
        noff = past // bk
        bias = _bias_tiles(p["rel_bias"], past, (past - bk, past), bq, bk, past + t)
    y_att = _attn_call(p["scal"], q, k_all, v_all, bias, p["subw"], bq=bq, bk=bk, noff=noff,
                       out_scale=1.0 - p["lam_init"])

    hv, hg = _ffn_hist_to_kernel(ffn_conv_hist.astype(F32))
    y, tv, tg = _ffn_call(x, y_ssd, y_att, mod3, p["wo1"], p["wo2"], p["nfw"], p["wuv"], p["wug"],
                          p["cwv"], p["cwg"], p["cbv"], p["cbg"], p["wd"], p["fw"], hv, hg,
                          tm=tm, real=min(t, tm))
    return (y[:, :t], k[:, :t].reshape(b, t, DA_HEADS, 2 * DA_DK), v[:, :t].reshape(b, t, DA_HEADS, DA_DV),
            _state_from_kernel(h_t), conv_new, _ffn_hist_from_kernel(tv, tg))


def kernel(x_prompt, x_sample, c_prompt, c_sample, cache_k, cache_v, state_ssm, state_ssm_conv, state_ffn_conv, w_ada, b_ada, norm_mix_w, w_in, ssm_conv_w, ssm_conv_b, ssm_dt_bias, ssm_a_log, ssm_d, ssm_norm_w, lambda_q1, lambda_k1, lambda_q2, lambda_k2, attn_subln_w, rel_bias, w_out, norm_ffn_w, w_up, ffn_conv_w, ffn_conv_b, w_down, final_norm_w):
    bp, bs = x_prompt.shape[0], x_sample.shape[0]
    dt = x_prompt.dtype
    p = _pack_params(norm_mix_w, w_in, ssm_conv_w, ssm_conv_b, ssm_dt_bias, ssm_a_log, ssm_d, ssm_norm_w,
                     lambda_q1, lambda_k1, lambda_q2, lambda_k2, attn_subln_w, rel_bias, w_out,
                     norm_ffn_w, w_up, ffn_conv_w, ffn_conv_b, w_down, final_norm_w, 0)
    c_all = jnp.concatenate([c_prompt, c_sample], axis=0)
    npad = -c_all.shape[0] % SUBLANES
    c_all = jnp.pad(c_all, ((0, npad), (0, 0)))
    mod = _mod_call(c_all, w_ada[0], b_ada[0].reshape(1, -1))

    zeros = lambda *s: jnp.zeros(s, dt)
    out_p = _run_group(x_prompt, mod[:bp], None, None,
                       zeros(bp, SSD_HEADS, SSD_HEADDIM, SSD_STATE), zeros(bp, SSD_CONV - 1, SSD_CONV_DIM),
                       zeros(bp, FFN_CONV - 1, 2 * D_FF), p, tm=512, ssd_rows=256, bq=512, bk=512)
    out_s = _run_group(x_sample, mod[bp:bp + bs], cache_k[0], cache_v[0], state_ssm[0], state_ssm_conv[0],
                       state_ffn_conv[0], p, tm=SUPER, ssd_rows=SUPER, bq=SUPER, bk=512)
    y_p, k_p, v_p, h_p, c_p, f_p = out_p
    y_s, k_s, v_s, h_s, c_s, f_s = out_s
    return (y_p, y_s, k_p[None], v_p[None], h_p[None], c_p[None], f_p[None],
            k_s[None], v_s[None], h_s[None], c_s[None], f_s[None])
```

```python
import functools
import math

import numpy as np
import jax
import jax.numpy as jnp
from jax import lax
from jax.experimental import pallas as pl
from jax.experimental.pallas import tpu as pltpu

F32 = jnp.float32
BF16 = jnp.bfloat16
HIGHEST = lax.Precision.HIGHEST

D_MODEL = 1024
CHUNK = 64
SSD_WIDTH = 512
SSD_HEADDIM = 64
SSD_HEADS = 8
SSD_GROUPS = 2
SSD_HPG = 4
SSD_STATE = 128
SSD_CONV = 4
SSD_CONV_DIM = SSD_WIDTH + 2 * SSD_GROUPS * SSD_STATE
GROUP_W = SSD_HPG * SSD_HEADDIM
DA_WIDTH = 512
DA_DK = 64
DA_DV = 128
DA_HEADS = 4
REL_BUCKETS = 32
REL_MAX_DIST = 128
D_FF = 2816
FFN_CONV = 3
EPS = 1e-6
IN_SPLITS = (512, 1536, 1544, 2056, 2568)
LANES = 128
SUBLANES = 8
SUPER = 128
FFN_CN = 256
FFN_NC = D_FF // FFN_CN
NEG = -1e30
VMEM_LIMIT = 56 * 1024 * 1024

PZ, PX, PDT, PK, PV, PEND = 0, 512, 1536, 1664, 2176, 2688
LOG2E = math.log2(math.e)


def _silu(x):
    return x / (1.0 + jnp.exp(-x))


def _softplus(x):
    return jnp.maximum(x, 0.0) + jnp.log1p(jnp.exp(-jnp.abs(x)))


def _rms(x, w):
    return x * lax.rsqrt(jnp.mean(x * x, axis=-1, keepdims=True) + EPS) * w


def _const_spec(shape):
    nd = len(shape)
    return pl.BlockSpec(shape, lambda *_: (0,) * nd)


def _mod_kernel(c_ref, w_ref, b_ref, o_ref):
    a = _silu(c_ref[...]).astype(BF16)
    o_ref[...] = jnp.dot(a, w_ref[...].astype(BF16), preferred_element_type=F32) + b_ref[...]


def _mod_call(c, w_ada, b_ada):
    n, d = c.shape
    nout = w_ada.shape[1]
    tn = 1024
    return pl.pallas_call(
        _mod_kernel,
        grid=(nout // tn,),
        in_specs=[pl.BlockSpec((n, d), lambda j: (0, 0)),
                  pl.BlockSpec((d, tn), lambda j: (0, j)),
                  pl.BlockSpec((1, tn), lambda j: (0, j))],
        out_specs=pl.BlockSpec((n, tn), lambda j: (0, j)),
        out_shape=jax.ShapeDtypeStruct((n, nout), F32),
        name="mod",
    )(c, w_ada, b_ada)


def _inproj_kernel(x_ref, mod_ref, nw_ref, w_ref, wt_ref, z_ref, xbc_ref, dt_ref, k_ref, v_ref, kb_ref, qt_ref, vt_ref):
    h = _rms(x_ref[...], nw_ref[...]) * (1.0 + mod_ref[1:2, :]) + mod_ref[0:1, :]
    hb = h.astype(BF16)

    def proj(a, b):
        return jnp.dot(hb, w_ref[:, a:b], preferred_element_type=F32)

    def proj_t(a, b):
        return lax.dot_general(wt_ref[a:b, :], hb, (((1,), (1,)), ((), ())), preferred_element_type=F32)

    z_ref[...] = proj(PZ, PX).astype(BF16)
    xbc_ref[...] = proj(PX, PDT).astype(BF16)
    dt_ref[...] = proj(PDT, PK)
    k = proj(PK, PV)
    k_ref[...] = k
    kb_ref[...] = k.astype(BF16)
    v_ref[...] = proj(PV, PEND)
    qt_ref[...] = proj_t(0, DA_WIDTH).astype(BF16)
    vt_ref[...] = proj_t(DA_WIDTH, 2 * DA_WIDTH).astype(BF16)


def _inproj_call(x, mod3, norm_w, w_cat, w_t, tm):
    b, t, d = x.shape
    nt = t // tm

    def row(width):
        return pl.BlockSpec((None, tm, width), lambda i, j: (i, j, 0))

    def out(width, dtype):
        return jax.ShapeDtypeStruct((b, t, width), dtype)

    tspec = pl.BlockSpec((None, None, DA_WIDTH, tm), lambda i, j: (i, j, 0, 0))
    tshape = jax.ShapeDtypeStruct((b, nt, DA_WIDTH, tm), BF16)
    return pl.pallas_call(
        _inproj_kernel,
        grid=(b, nt),
        in_specs=[row(d),
                  pl.BlockSpec((None, 6, d), lambda i, j: (i, 0, 0)),
                  _const_spec((1, d)),
                  _const_spec((d, PEND)),
                  _const_spec((2 * DA_WIDTH, d))],
        out_specs=[row(512), row(1024), row(LANES), row(512), row(512), row(512), tspec, tspec],
        out_shape=[out(512, BF16), out(1024, BF16), out(LANES, F32),
                   out(512, F32), out(512, F32), out(512, BF16), tshape, tshape],
        compiler_params=pltpu.CompilerParams(dimension_semantics=("parallel", "parallel"),
                                             vmem_limit_bytes=VMEM_LIMIT),
        name="inproj",
    )(x, mod3, norm_w, w_cat, w_t)


def _ssd_kernel(z_ref, xbc_ref, dt_ref, h0_ref, hist_ref, cw_ref, cbias_ref, dtb_ref, alog_ref, dsk_ref, nw_ref,
                tri_ref, e_ref, y_ref, hout_ref, cout_ref, h_scr, cbuf, ybuf, *, chunk, rows, real):
    t = pl.program_id(1)
    nconv = SSD_CONV - 1

    @pl.when(t == 0)
    def _init():
        h_scr[...] = h0_ref[...]
        cbuf[0:SUBLANES, :] = jnp.zeros((SUBLANES, SSD_CONV_DIM), F32)
        cbuf[SUBLANES - nconv:SUBLANES, :] = hist_ref[...]

    cbuf[SUBLANES:SUBLANES + rows, :] = xbc_ref[...].astype(F32)
    conv = cbias_ref[...]
    for j in range(SSD_CONV):
        off = SUBLANES - nconv + j
        conv = conv + cw_ref[j:j + 1, :] * cbuf[off:off + rows, :]
    tail = cbuf[SUBLANES - nconv + real:SUBLANES + real, :]
    cout_ref[...] = tail
    cbuf[SUBLANES - nconv:SUBLANES, :] = tail
    xc = _silu(conv)
    xs = xc[:, 0:SSD_WIDTH]

    dtv = _softplus(dt_ref[...] + dtb_ref[...])
    da = dtv * (-jnp.exp(alog_ref[...]))
    acs = jnp.dot(tri_ref[...], da, precision=HIGHEST, preferred_element_type=F32)
    dt_x = jnp.dot(dtv, e_ref[...], precision=HIGHEST, preferred_element_type=F32)
    acs_x = jnp.dot(acs, e_ref[...], precision=HIGHEST, preferred_element_type=F32)
    eacs_x = jnp.exp(acs_x)
    xd = xs * dt_x

    li = lax.broadcasted_iota(jnp.int32, (SUPER, SUPER), 0)
    si = lax.broadcasted_iota(jnp.int32, (SUPER, SUPER), 1)
    cshift = chunk.bit_length() - 1
    mask2 = ((li >> cshift) == (si >> cshift)) & (si <= li)
    lane_g = lax.broadcasted_iota(jnp.int32, (SUPER, GROUP_W), 1) >> (SSD_HEADDIM.bit_length() - 1)

    for sb in range(rows // SUPER):
        o = sb * SUPER
        nreal = (min(real, o + SUPER) - o) // chunk
        acs2 = acs[o:o + SUPER, :]
        acs_t = acs2.T
        for g in range(SSD_GROUPS):
            gs = slice(g * GROUP_W, (g + 1) * GROUP_W)
            bm2 = xc[o:o + SUPER, SSD_WIDTH + g * SSD_STATE:SSD_WIDTH + (g + 1) * SSD_STATE]
            cm2 = xc[o:o + SUPER, SSD_WIDTH + (SSD_GROUPS + g) * SSD_STATE:SSD_WIDTH + (SSD_GROUPS + g + 1) * SSD_STATE]
            bmb = bm2.astype(BF16)
            cmb = cm2.astype(BF16)
            cb2 = lax.dot_general(cmb, bmb, (((1,), (1,)), ((), ())), preferred_element_type=F32)
            bm_t = bm2.T.astype(BF16)
            xd_g = xd[o:o + SUPER, gs]
            ms, xm = [], []
            for rr in range(SSD_HPG):
                r = g * SSD_HPG + rr
                seg = acs2[:, r:r + 1] - acs_t[r:r + 1, :]
                dec = jnp.where(mask2, jnp.exp(jnp.where(mask2, seg, 0.0)), 0.0)
                ms.append((cb2 * dec).astype(BF16))
                xm.append(jnp.where(lane_g == rr, xd_g, 0.0).astype(BF16))
            ybuf[o:o + SUPER, gs] = jnp.dot(jnp.concatenate(ms, axis=1), jnp.concatenate(xm, axis=0),
                                            preferred_element_type=F32)
            for j in range(nreal):
                a0, a1 = o + j * chunk, o + (j + 1) * chunk
                h_t = h_scr[g]
                yoff = jnp.dot(cmb[j * chunk:(j + 1) * chunk, :], h_t.astype(BF16), preferred_element_type=F32)
                ybuf[a0:a1, gs] = ybuf[a0:a1, gs] + yoff * eacs_x[a0:a1, gs]
                dte = jnp.exp(acs_x[a1 - 1:a1, gs] - acs_x[a0:a1, gs])
                xw = (xd[a0:a1, gs] * dte).astype(BF16)
                pieces = []
                if j > 0:
                    pieces.append(jnp.zeros((j * chunk, GROUP_W), BF16))
                pieces.append(xw)
                if (j + 1) * chunk < SUPER:
                    pieces.append(jnp.zeros((SUPER - (j + 1) * chunk, GROUP_W), BF16))
                xw2 = jnp.concatenate(pieces, axis=0) if len(pieces) > 1 else xw
                st = jnp.dot(bm_t, xw2, preferred_element_type=F32)
                h_scr[g] = h_t * eacs_x[a1 - 1:a1, gs] + st

    y = ybuf[...] + dsk_ref[...] * xs
    y = y * _silu(z_ref[...].astype(F32))
    for g in range(SSD_GROUPS):
        gs = slice(g * GROUP_W, (g + 1) * GROUP_W)
        y_ref[:, gs] = _rms(y[:, gs], nw_ref[:, gs]).astype(BF16)

    @pl.when(t == pl.num_programs(1) - 1)
    def _fin():
        hout_ref[...] = h_scr[...]


def _ssd_call(z, xbc, dt, h0_t, hist, cw, cbias, dtb, alog, dsk, nw, *, chunk, rows, real):
    b, t, _ = z.shape
    ii = np.arange(rows)
    tri = ((ii[:, None] // chunk == ii[None, :] // chunk) & (ii[None, :] <= ii[:, None])).astype(np.float32)
    e = np.zeros((LANES, SSD_WIDTH), np.float32)
    for r in range(SSD_HEADS):
        e[r, r * SSD_HEADDIM:(r + 1) * SSD_HEADDIM] = 1.0

    def row(width):
        return pl.BlockSpec((None, rows, width), lambda i, j: (i, j, 0))

    kern = functools.partial(_ssd_kernel, chunk=chunk, rows=rows, real=real)
    return pl.pallas_call(
        kern,
        grid=(b, t // rows),
        in_specs=[row(SSD_WIDTH), row(SSD_CONV_DIM), row(LANES),
                  pl.BlockSpec((None, SSD_GROUPS, SSD_STATE, GROUP_W), lambda i, j: (i, 0, 0, 0)),
                  pl.BlockSpec((None, SSD_CONV - 1, SSD_CONV_DIM), lambda i, j: (i, 0, 0)),
                  _const_spec((SSD_CONV, SSD_CONV_DIM)), _const_spec((1, SSD_CONV_DIM)),
                  _const_spec((1, LANES)), _const_spec((1, LANES)),
                  _const_spec((1, SSD_WIDTH)), _const_spec((1, SSD_WIDTH)),
                  _const_spec((rows, rows)), _const_spec((LANES, SSD_WIDTH))],
        out_specs=[row(SSD_WIDTH),
                   pl.BlockSpec((None, SSD_GROUPS, SSD_STATE, GROUP_W), lambda i, j: (i, 0, 0, 0)),
                   pl.BlockSpec((None, SSD_CONV - 1, SSD_CONV_DIM), lambda i, j: (i, 0, 0))],
        out_shape=[jax.ShapeDtypeStruct((b, t, SSD_WIDTH), BF16),
                   jax.ShapeDtypeStruct((b, SSD_GROUPS, SSD_STATE, GROUP_W), F32),
                   jax.ShapeDtypeStruct((b, SSD_CONV - 1, SSD_CONV_DIM), F32)],
        scratch_shapes=[pltpu.VMEM((SSD_GROUPS, SSD_STATE, GROUP_W), F32),
                        pltpu.VMEM((rows + SUBLANES, SSD_CONV_DIM), F32),
                        pltpu.VMEM((rows, SSD_WIDTH), F32)],
        compiler_params=pltpu.CompilerParams(dimension_semantics=("parallel", "arbitrary"),
                                             vmem_limit_bytes=VMEM_LIMIT),
        name="ssd",
    )(z, xbc, dt, h0_t, hist, cw, cbias, dtb, alog, dsk, nw, jnp.asarray(tri), jnp.asarray(e))


def _attn_kernel(scal_ref, qt_ref, k_ref, vt_ref, bias_ref, subw_ref, o_ref, m_scr, l_scr, acc_scr,
                 *, bq, bk, noff, out_scale):
    h = pl.program_id(1)
    qi = pl.program_id(2)
    kn0 = qi + (noff - 1)
    lam = scal_ref[0]
    cfar = scal_ref[1 + h]

    qt = qt_ref[...]
    feat = lax.broadcasted_iota(jnp.int32, (DA_DV, bq), 0)
    zero = jnp.zeros_like(qt)
    qz = (jnp.where(feat < DA_DK, qt, zero), jnp.where(feat >= DA_DK, qt, zero))

    m_scr[...] = jnp.full(m_scr.shape, NEG, F32)
    l_scr[...] = jnp.zeros(l_scr.shape, F32)
    acc_scr[...] = jnp.zeros(acc_scr.shape, F32)

    def block(ki, bias, shift):
        start = pl.multiple_of(ki * bk, bk)
        k = k_ref[pl.ds(start, bk), :]
        vt = vt_ref[ki]
        for mm in range(2):
            s = jnp.dot(k, qz[mm], preferred_element_type=F32)
            if bias is not None:
                s = s + bias
            m_old = m_scr[mm]
            m_new = jnp.maximum(m_old, jnp.max(s, axis=0, keepdims=True) + shift)
            p = jnp.exp2(s - (m_new - shift))
            alpha = jnp.exp2(m_old - m_new)
            l_scr[mm] = alpha * l_scr[mm] + jnp.sum(p, axis=0, keepdims=True)
            acc_scr[mm] = alpha * acc_scr[mm] + jnp.dot(vt, p.astype(BF16), preferred_element_type=F32)
            m_scr[mm] = m_new

    def far_body(ki, carry):
        block(ki, None, cfar)
        return carry

    lax.fori_loop(0, jnp.maximum(kn0, 0), far_body, 0)

    @pl.when(kn0 >= 0)
    def _near0():
        block(kn0, bias_ref[0], 0.0)

    block(kn0 + 1, bias_ref[1], 0.0)

    o = acc_scr[0] * (1.0 / l_scr[0]) - acc_scr[1] * (lam / l_scr[1])
    o = o * lax.rsqrt(jnp.mean(o * o, axis=0, keepdims=True) + EPS) * (subw_ref[...] * out_scale)
    o_ref[...] = o.T.astype(BF16)


def _attn_call(scal, qt, kb, vt, bias, subw, *, bq, bk, noff, out_scale):
    b, nq = qt.shape[:2]
    tk = kb.shape[1]
    nkb = vt.shape[1]
    kern = functools.partial(_attn_kernel, bq=bq, bk=bk, noff=noff, out_scale=out_scale)
    return pl.pallas_call(
        kern,
        grid=(b, DA_HEADS, nq),
        in_specs=[pl.BlockSpec(memory_space=pltpu.SMEM),
                  pl.BlockSpec((None, None, DA_DV, bq), lambda i, h, j: (i, j, h, 0)),
                  pl.BlockSpec((None, tk, DA_DV), lambda i, h, j: (i, 0, h)),
                  pl.BlockSpec((None, nkb, DA_DV, bk), lambda i, h, j: (i, 0, h, 0)),
                  pl.BlockSpec((None, 2, bk, bq), lambda i, h, j: (h, 0, 0, 0)),
                  pl.BlockSpec((DA_DV, 1), lambda i, h, j: (0, 0))],
        out_specs=pl.BlockSpec((None, bq, DA_DV), lambda i, h, j: (i, j, h)),
        out_shape=jax.ShapeDtypeStruct((b, nq * bq, DA_WIDTH), BF16),
        scratch_shapes=[pltpu.VMEM((2, 1, bq), F32), pltpu.VMEM((2, 1, bq), F32),
                        pltpu.VMEM((2, DA_DV, bq), F32)],
        compiler_params=pltpu.CompilerParams(dimension_semantics=("parallel", "parallel", "arbitrary"),
                                             vmem_limit_bytes=VMEM_LIMIT),
        name="attn",
    )(scal, qt, kb, vt, bias, subw)


def _rel_bucket(rel):
    nb = REL_BUCKETS // 2
    max_exact = nb // 2
    n = jnp.abs(rel)
    nf = jnp.maximum(n, 1).astype(jnp.float32)
    large = max_exact + (jnp.log(nf / max_exact) / math.log(REL_MAX_DIST / max_exact)
                         * (nb - max_exact)).astype(jnp.int32)
    large = jnp.minimum(large, nb - 1)
    return jnp.where(rel > 0, nb, 0) + jnp.where(n < max_exact, n, large)


def _bias_tiles(rel_bias, qpos0, kpos0, bq, bk, tk_real):
    qpos = qpos0 + np.arange(bq)
    span = bq + bk - 1
    tiles = []
    for d in range(2):
        kpos = kpos0[d] + np.arange(bk)
        offs = (kpos0[d] - qpos0) + np.arange(-(bq - 1), bk)
        table = rel_bias[_rel_bucket(jnp.asarray(offs, jnp.int32))].astype(F32).T * LOG2E
        rev = jnp.pad(table[:, ::-1], ((0, 0), (0, 1)))
        skew = jnp.tile(rev, (1, bk))[:, :bk * span].reshape(DA_HEADS, bk, span)
        toep = skew[:, :, bk - 1:bk - 1 + bq]
        vis = (kpos[:, None] // CHUNK <= qpos[None, :] // CHUNK) & (kpos[:, None] < tk_real)
        tiles.append(jnp.where(jnp.asarray(vis)[None], toep, NEG))
    return jnp.stack(tiles, axis=1)


def _ffn_kernel(x_ref, ys_ref, ya_ref, mod_ref, wo1_ref, wo2_ref, nfw_ref, wuv_ref, wug_ref, cwv_ref, cwg_ref,
                cbv_ref, cbg_ref, wd_ref, fw_ref, hv_ref, hg_ref, y_ref, tv_ref, tg_ref,
                tailv, tailg, bufv, bufg, x1_scr, h2_scr, acc_scr, *, tm, real):
    t = pl.program_id(1)
    nh = FFN_CONV - 1
    lo = SUBLANES - nh

    @pl.when(t == 0)
    def _init():
        tailv[:, lo:SUBLANES, :] = hv_ref[...]
        tailg[:, lo:SUBLANES, :] = hg_ref[...]

    mix = (jnp.dot(ys_ref[...], wo1_ref[...], preferred_element_type=F32)
           + jnp.dot(ya_ref[...], wo2_ref[...], preferred_element_type=F32))
    x1 = x_ref[...] + mod_ref[2:3, :] * mix
    x1_scr[...] = x1
    h2 = _rms(x1, nfw_ref[...]) * (1.0 + mod_ref[4:5, :]) + mod_ref[3:4, :]
    h2_scr[...] = h2.astype(BF16)
    acc_scr[...] = jnp.zeros(acc_scr.shape, F32)

    def conv(u, buf, tail, cw_ref, cb_ref, j):
        buf[lo:SUBLANES, :] = tail[j, lo:SUBLANES, :]
        buf[SUBLANES:SUBLANES + tm, :] = u
        cw = cw_ref[j]
        c = cb_ref[j]
        for i in range(FFN_CONV):
            c = c + cw[i:i + 1, :] * buf[lo + i:lo + i + tm, :]
        tail[j, lo:SUBLANES, :] = buf[lo + real:SUBLANES + real, :]
        return c

    def body(j, carry):
        hb = h2_scr[...]
        uv = jnp.dot(hb, wuv_ref[j], preferred_element_type=F32)
        ug = jnp.dot(hb, wug_ref[j], preferred_element_type=F32)
        cv = conv(uv, bufv, tailv, cwv_ref, cbv_ref, j)
        cg = conv(ug, bufg, tailg, cwg_ref, cbg_ref, j)
        act = (_silu(cg) * cv).astype(BF16)
        acc_scr[...] += jnp.dot(act, wd_ref[j], preferred_element_type=F32)
        return carry

    lax.fori_loop(0, FFN_NC, body, 0)
    x2 = x1_scr[...] + mod_ref[5:6, :] * acc_scr[...]
    y_ref[...] = _rms(x2, fw_ref[...])
    tv_ref[...] = tailv[:, lo:SUBLANES, :]
    tg_ref[...] = tailg[:, lo:SUBLANES, :]


def _ffn_call(x, ys, ya, mod3, wo1, wo2, nfw, wuv, wug, cwv, cwg, cbv, cbg, wd, fw, hv, hg, *, tm, real):
    b, t, d = x.shape
    nh = FFN_CONV - 1

    def row(width):
        return pl.BlockSpec((None, tm, width), lambda i, j: (i, j, 0))

    hist_spec = pl.BlockSpec((None, FFN_NC, nh, FFN_CN), lambda i, j: (i, 0, 0, 0))
    kern = functools.partial(_ffn_kernel, tm=tm, real=real)
    return pl.pallas_call(
        kern,
        grid=(b, t // tm),
        in_specs=[row(d), row(SSD_WIDTH), row(DA_WIDTH),
                  pl.BlockSpec((None, 6, d), lambda i, j: (i, 0, 0)),
                  _const_spec((SSD_WIDTH, d)), _const_spec((DA_WIDTH, d)), _const_spec((1, d)),
                  _const_spec((FFN_NC, d, FFN_CN)), _const_spec((FFN_NC, d, FFN_CN)),
                  _const_spec((FFN_NC, FFN_CONV, FFN_CN)), _const_spec((FFN_NC, FFN_CONV, FFN_CN)),
                  _const_spec((FFN_NC, 1, FFN_CN)), _const_spec((FFN_NC, 1, FFN_CN)),
                  _const_spec((FFN_NC, FFN_CN, d)), _const_spec((1, d)),
                  hist_spec, hist_spec],
        out_specs=[row(d), hist_spec, hist_spec],
        out_shape=[jax.ShapeDtypeStruct((b, t, d), F32),
                   jax.ShapeDtypeStruct((b, FFN_NC, nh, FFN_CN), F32),
                   jax.ShapeDtypeStruct((b, FFN_NC, nh, FFN_CN), F32)],
        scratch_shapes=[pltpu.VMEM((FFN_NC, SUBLANES, FFN_CN), F32), pltpu.VMEM((FFN_NC, SUBLANES, FFN_CN), F32),
                        pltpu.VMEM((tm + SUBLANES, FFN_CN), F32), pltpu.VMEM((tm + SUBLANES, FFN_CN), F32),
                        pltpu.VMEM((tm, d), F32), pltpu.VMEM((tm, d), BF16), pltpu.VMEM((tm, d), F32)],
        compiler_params=pltpu.CompilerParams(dimension_semantics=("parallel", "arbitrary"),
                                             vmem_limit_bytes=VMEM_LIMIT),
        name="ffn",
    )(x, ys, ya, mod3, wo1, wo2, nfw, wuv, wug, cwv, cwg, cbv, cbg, wd, fw, hv, hg)


def _pack_params(norm_mix_w, w_in, ssm_conv_w, ssm_conv_b, ssm_dt_bias, ssm_a_log, ssm_d, ssm_norm_w,
                 lambda_q1, lambda_k1, lambda_q2, lambda_k2, attn_subln_w, rel_bias, w_out,
                 norm_ffn_w, w_up, ffn_conv_w, ffn_conv_b, w_down, final_norm_w, layer):
    l = layer
    wz, wx, wdt, wq, wk, wv = jnp.split(w_in[l], IN_SPLITS, axis=-1)
    wdt = jnp.pad(wdt, ((0, 0), (0, LANES - SSD_HEADS)))
    w_cat = jnp.concatenate([wz, wx, wdt, wk, wv], axis=-1).astype(BF16)
    w_t = jnp.concatenate([wq * (DA_DK ** -0.5 * LOG2E), wv], axis=-1).T.astype(BF16)

    def pad_heads(v):
        return jnp.pad(v.astype(F32), (0, LANES - SSD_HEADS)).reshape(1, LANES)

    def chunks(v):
        return jnp.moveaxis(v.reshape(v.shape[:-1] + (FFN_NC, FFN_CN)), -2, 0)

    lam_init = 0.8 - 0.6 * math.exp(-0.3 * l)
    lam = (jnp.exp(jnp.sum(lambda_q1[l].astype(F32) * lambda_k1[l].astype(F32)))
           - jnp.exp(jnp.sum(lambda_q2[l].astype(F32) * lambda_k2[l].astype(F32))) + lam_init)
    far_bias = rel_bias[REL_BUCKETS // 2 - 1].astype(F32)
    return dict(
        norm_mix_w=norm_mix_w[l].reshape(1, D_MODEL), w_cat=w_cat, w_t=w_t,
        cw=ssm_conv_w[l], cbias=ssm_conv_b[l].reshape(1, SSD_CONV_DIM),
        dtb=pad_heads(ssm_dt_bias[l]), alog=pad_heads(ssm_a_log[l]),
        dsk=jnp.repeat(ssm_d[l].astype(F32), SSD_HEADDIM).reshape(1, SSD_WIDTH),
        ssm_nw=ssm_norm_w[l].reshape(1, SSD_WIDTH),
        scal=jnp.concatenate([lam.reshape(1), far_bias * LOG2E]).astype(F32), lam_init=lam_init,
        subw=attn_subln_w[l].reshape(DA_DV, 1), rel_bias=rel_bias,
        wo1=w_out[l][:SSD_WIDTH].astype(BF16), wo2=w_out[l][SSD_WIDTH:].astype(BF16),
        nfw=norm_ffn_w[l].reshape(1, D_MODEL),
        wuv=chunks(w_up[l][:, :D_FF]).astype(BF16), wug=chunks(w_up[l][:, D_FF:]).astype(BF16),
        cwv=chunks(ffn_conv_w[l][:, :D_FF]), cwg=chunks(ffn_conv_w[l][:, D_FF:]),
        cbv=chunks(ffn_conv_b[l][:D_FF].reshape(1, D_FF)), cbg=chunks(ffn_conv_b[l][D_FF:].reshape(1, D_FF)),
        wd=w_down[l].reshape(FFN_NC, FFN_CN, D_MODEL).astype(BF16),
        fw=final_norm_w.reshape(1, D_MODEL),
    )


def _state_to_kernel(h):
    b = h.shape[0]
    h = h.reshape(b, SSD_GROUPS, SSD_HPG, SSD_HEADDIM, SSD_STATE)
    return jnp.transpose(h, (0, 1, 4, 2, 3)).reshape(b, SSD_GROUPS, SSD_STATE, GROUP_W)


def _state_from_kernel(h):
    b = h.shape[0]
    h = h.reshape(b, SSD_GROUPS, SSD_STATE, SSD_HPG, SSD_HEADDIM)
    return jnp.transpose(h, (0, 1, 3, 4, 2)).reshape(b, SSD_HEADS, SSD_HEADDIM, SSD_STATE)


def _ffn_hist_to_kernel(hist):
    b, nh, _ = hist.shape
    hv = hist[:, :, :D_FF].reshape(b, nh, FFN_NC, FFN_CN)
    hg = hist[:, :, D_FF:].reshape(b, nh, FFN_NC, FFN_CN)
    return jnp.transpose(hv, (0, 2, 1, 3)), jnp.transpose(hg, (0, 2, 1, 3))


def _ffn_hist_from_kernel(tv, tg):
    b, _, nh, _ = tv.shape
    tv = jnp.transpose(tv, (0, 2, 1, 3)).reshape(b, nh, D_FF)
    tg = jnp.transpose(tg, (0, 2, 1, 3)).reshape(b, nh, D_FF)
    return jnp.concatenate([tv, tg], axis=-1)


def _run_group(x, mod, past_k, past_v, ssm_h0, ssm_conv_hist, ffn_conv_hist, p, *, tm, ssd_rows, bq, bk):
    b, t, d = x.shape
    past = 0 if past_k is None else past_k.shape[1]
    chunk = min(CHUNK, t)
    tp = max(t, SUPER)
    if tp != t:
        x = jnp.pad(x, ((0, 0), (0, tp - t), (0, 0)))
        tm = ssd_rows = bq = tp
    mod3 = mod.reshape(b, 6, d)

    assert tm == bq
    z, xbc, dt, k, v, kb, qt, vt = _inproj_call(x, mod3, p["norm_mix_w"], p["w_cat"], p["w_t"], tm)

    y_ssd, h_t, conv_new = _ssd_call(z, xbc, dt, _state_to_kernel(ssm_h0.astype(F32)), ssm_conv_hist.astype(F32),
                                     p["cw"], p["cbias"], p["dtb"], p["alog"], p["dsk"], p["ssm_nw"],
                                     chunk=chunk, rows=ssd_rows, real=min(t, ssd_rows))

    if past == 0:
        assert bq == bk and t % bq == 0
        k_all, vt_all, noff = kb, vt, 0
        bias = _bias_tiles(p["rel_bias"], bq, (0, bq), bq, bk, 2 * bq)
    else:
        assert past % bk == 0 and t <= bk and tp <= bk
        k_all = jnp.concatenate([past_k.reshape(b, past, DA_WIDTH).astype(BF16), kb[:, :t],
                                 jnp.zeros((b, bk - t, DA_WIDTH), BF16)], axis=1)
        past_vt = jnp.swapaxes(past_v.reshape(b, past // bk, bk, DA_WIDTH), 2, 3).astype(BF16)
        vt_all = jnp.concatenate([past_vt, jnp.pad(vt, ((0, 0), (0, 0), (0, 0), (0, bk - tp)))], axis=1)
        noff = past // bk
        bias = _bias_tiles(p["rel_bias"], past, (past - bk, past), bq, bk, past + t)
    y_att = _attn_call(p["scal"], qt, k_all, vt_all, bias, p["subw"], bq=bq, bk=bk, noff=noff,
                       out_scale=1.0 - p["lam_init"])

    hv, hg = _ffn_hist_to_kernel(ffn_conv_hist.astype(F32))
    y, tv, tg = _ffn_call(x, y_ssd, y_att, mod3, p["wo1"], p["wo2"], p["nfw"], p["wuv"], p["wug"],
                          p["cwv"], p["cwg"], p["cbv"], p["cbg"], p["wd"], p["fw"], hv, hg,
                          tm=tm, real=min(t, tm))
    return (y[:, :t], k[:, :t].reshape(b, t, DA_HEADS, 2 * DA_DK), v[:, :t].reshape(b, t, DA_HEADS, DA_DV),
            _state_from_kernel(h_t), conv_new, _ffn_hist_from_kernel(tv, tg))


def kernel(x_prompt, x_sample, c_prompt, c_sample, cache_k, cache_v, state_ssm, state_ssm_conv, state_ffn_conv, w_ada, b_ada, norm_mix_w, w_in, ssm_conv_w, ssm_conv_b, ssm_dt_bias, ssm_a_log, ssm_d, ssm_norm_w, lambda_q1, lambda_k1, lambda_q2, lambda_k2, attn_subln_w, rel_bias, w_out, norm_ffn_w, w_up, ffn_conv_w, ffn_conv_b, w_down, final_norm_w):
    bp, bs = x_prompt.shape[0], x_sample.shape[0]
    dt = x_prompt.dtype
    p = _pack_params(norm_mix_w, w_in, ssm_conv_w, ssm_conv_b, ssm_dt_bias, ssm_a_log, ssm_d, ssm_norm_w,
                     lambda_q1, lambda_k1, lambda_q2, lambda_k2, attn_subln_w, rel_bias, w_out,
                     norm_ffn_w, w_up, ffn_conv_w, ffn_conv_b, w_down, final_norm_w, 0)
    c_all = jnp.concatenate([c_prompt, c_sample], axis=0)
    npad = -c_all.shape[0] % SUBLANES
    c_all = jnp.pad(c_all, ((0, npad), (0, 0)))
    mod = _mod_call(c_all, w_ada[0], b_ada[0].reshape(1, -1))

    zeros = lambda *s: jnp.zeros(s, dt)
    out_p = _run_group(x_prompt, mod[:bp], None, None,
                       zeros(bp, SSD_HEADS, SSD_HEADDIM, SSD_STATE), zeros(bp, SSD_CONV - 1, SSD_CONV_DIM),
                       zeros(bp, FFN_CONV - 1, 2 * D_FF), p, tm=512, ssd_rows=256, bq=512, bk=512)
    out_s = _run_group(x_sample, mod[bp:bp + bs], cache_k[0], cache_v[0], state_ssm[0], state_ssm_conv[0],
                       state_ffn_conv[0], p, tm=SUPER, ssd_rows=SUPER, bq=SUPER, bk=512)
    y_p, k_p, v_p, h_p, c_p, f_p = out_p
    y_s, k_s, v_s, h_s, c_s, f_s = out_s
    return (y_p, y_s, k_p[None], v_p[None], h_p[None], c_p[None], f_p[None],
            k_s[None], v_s[None], h_s[None], c_s[None], f_s[None])
```

```python
import functools
import math

import numpy as np
import jax
import jax.numpy as jnp
from jax import lax
from jax.experimental import pallas as pl
from jax.experimental.pallas import tpu as pltpu

F32 = jnp.float32
BF16 = jnp.bfloat16
HIGHEST = lax.Precision.HIGHEST

D_MODEL = 1024
CHUNK = 64
SSD_WIDTH = 512
SSD_HEADDIM = 64
SSD_HEADS = 8
SSD_GROUPS = 2
SSD_HPG = 4
SSD_STATE = 128
SSD_CONV = 4
SSD_CONV_DIM = SSD_WIDTH + 2 * SSD_GROUPS * SSD_STATE
GROUP_W = SSD_HPG * SSD_HEADDIM
DA_WIDTH = 512
DA_DK = 64
DA_DV = 128
DA_HEADS = 4
REL_BUCKETS = 32
REL_MAX_DIST = 128
D_FF = 2816
FFN_CONV = 3
EPS = 1e-6
IN_SPLITS = (512, 1536, 1544, 2056, 2568)
LANES = 128
SUBLANES = 8
SUPER = 128
ATT_KT = 64
FFN_CN = 256
FFN_NC = D_FF // FFN_CN
NEG = -1e30
VMEM_LIMIT = 56 * 1024 * 1024

PZ, PX, PDT, PK, PV, PEND = 0, 512, 1536, 1664, 2176, 2688
LOG2E = math.log2(math.e)


def _silu(x):
    return x / (1.0 + jnp.exp(-x))


def _softplus(x):
    return jnp.maximum(x, 0.0) + jnp.log1p(jnp.exp(-jnp.abs(x)))


def _rms(x, w):
    return x * lax.rsqrt(jnp.mean(x * x, axis=-1, keepdims=True) + EPS) * w


def _const_spec(shape):
    nd = len(shape)
    return pl.BlockSpec(shape, lambda *_: (0,) * nd)


def _mod_kernel(c_ref, w_ref, b_ref, o_ref):
    a = _silu(c_ref[...]).astype(BF16)
    o_ref[...] = jnp.dot(a, w_ref[...].astype(BF16), preferred_element_type=F32) + b_ref[...]


def _mod_call(c, w_ada, b_ada):
    n, d = c.shape
    nout = w_ada.shape[1]
    tn = 1024
    return pl.pallas_call(
        _mod_kernel,
        grid=(nout // tn,),
        in_specs=[pl.BlockSpec((n, d), lambda j: (0, 0)),
                  pl.BlockSpec((d, tn), lambda j: (0, j)),
                  pl.BlockSpec((1, tn), lambda j: (0, j))],
        out_specs=pl.BlockSpec((n, tn), lambda j: (0, j)),
        out_shape=jax.ShapeDtypeStruct((n, nout), F32),
        name="mod",
    )(c, w_ada, b_ada)


def _inproj_kernel(x_ref, mod_ref, nw_ref, w_ref, wt_ref, z_ref, xbc_ref, dt_ref, k_ref, v_ref, kb_ref, qt_ref, vt_ref):
    h = _rms(x_ref[...], nw_ref[...]) * (1.0 + mod_ref[1:2, :]) + mod_ref[0:1, :]
    hb = h.astype(BF16)

    def proj(a, b):
        return jnp.dot(hb, w_ref[:, a:b], preferred_element_type=F32)

    def proj_t(a, b):
        return lax.dot_general(wt_ref[a:b, :], hb, (((1,), (1,)), ((), ())), preferred_element_type=F32)

    z_ref[...] = proj(PZ, PX).astype(BF16)
    xbc_ref[...] = proj(PX, PDT).astype(BF16)
    dt_ref[...] = proj(PDT, PK)
    k = proj(PK, PV)
    k_ref[...] = k
    kb_ref[...] = k.astype(BF16)
    v_ref[...] = proj(PV, PEND)
    qt_ref[...] = proj_t(0, DA_WIDTH).astype(BF16)
    vt_ref[...] = proj_t(DA_WIDTH, 2 * DA_WIDTH).astype(BF16)


def _inproj_call(x, mod3, norm_w, w_cat, w_t, tm):
    b, t, d = x.shape
    nt = t // tm

    def row(width):
        return pl.BlockSpec((None, tm, width), lambda i, j: (i, j, 0))

    def out(width, dtype):
        return jax.ShapeDtypeStruct((b, t, width), dtype)

    tspec = pl.BlockSpec((None, None, DA_WIDTH, tm), lambda i, j: (i, j, 0, 0))
    tshape = jax.ShapeDtypeStruct((b, nt, DA_WIDTH, tm), BF16)
    return pl.pallas_call(
        _inproj_kernel,
        grid=(b, nt),
        in_specs=[row(d),
                  pl.BlockSpec((None, 6, d), lambda i, j: (i, 0, 0)),
                  _const_spec((1, d)),
                  _const_spec((d, PEND)),
                  _const_spec((2 * DA_WIDTH, d))],
        out_specs=[row(512), row(1024), row(LANES), row(512), row(512), row(512), tspec, tspec],
        out_shape=[out(512, BF16), out(1024, BF16), out(LANES, F32),
                   out(512, F32), out(512, F32), out(512, BF16), tshape, tshape],
        compiler_params=pltpu.CompilerParams(dimension_semantics=("parallel", "parallel"),
                                             vmem_limit_bytes=VMEM_LIMIT),
        name="inproj",
    )(x, mod3, norm_w, w_cat, w_t)


def _ssd_kernel(z_ref, xbc_ref, dt_ref, h0_ref, hist_ref, cw_ref, cbias_ref, dtb_ref, alog_ref, dsk_ref, nw_ref,
                tri_ref, e_ref, y_ref, hout_ref, cout_ref, h_scr, cbuf, ybuf, *, chunk, rows, real):
    t = pl.program_id(1)
    nconv = SSD_CONV - 1

    @pl.when(t == 0)
    def _init():
        h_scr[...] = h0_ref[...]
        cbuf[0:SUBLANES, :] = jnp.zeros((SUBLANES, SSD_CONV_DIM), F32)
        cbuf[SUBLANES - nconv:SUBLANES, :] = hist_ref[...]

    cbuf[SUBLANES:SUBLANES + rows, :] = xbc_ref[...].astype(F32)
    conv = cbias_ref[...]
    for j in range(SSD_CONV):
        off = SUBLANES - nconv + j
        conv = conv + cw_ref[j:j + 1, :] * cbuf[off:off + rows, :]
    tail = cbuf[SUBLANES - nconv + real:SUBLANES + real, :]
    cout_ref[...] = tail
    cbuf[SUBLANES - nconv:SUBLANES, :] = tail
    xc = _silu(conv)
    xs = xc[:, 0:SSD_WIDTH]

    dtv = _softplus(dt_ref[...] + dtb_ref[...])
    da = dtv * (-jnp.exp(alog_ref[...]))
    acs = jnp.dot(tri_ref[...], da, precision=HIGHEST, preferred_element_type=F32)
    dt_x = jnp.dot(dtv, e_ref[...], precision=HIGHEST, preferred_element_type=F32)
    acs_x = jnp.dot(acs, e_ref[...], precision=HIGHEST, preferred_element_type=F32)
    eacs_x = jnp.exp(acs_x)
    xd = xs * dt_x

    li = lax.broadcasted_iota(jnp.int32, (SUPER, SUPER), 0)
    si = lax.broadcasted_iota(jnp.int32, (SUPER, SUPER), 1)
    cshift = chunk.bit_length() - 1
    mask2 = ((li >> cshift) == (si >> cshift)) & (si <= li)
    lane_g = lax.broadcasted_iota(jnp.int32, (SUPER, GROUP_W), 1) >> (SSD_HEADDIM.bit_length() - 1)

    for sb in range(rows // SUPER):
        o = sb * SUPER
        nreal = (min(real, o + SUPER) - o) // chunk
        acs2 = acs[o:o + SUPER, :]
        acs_t = acs2.T
        for g in range(SSD_GROUPS):
            gs = slice(g * GROUP_W, (g + 1) * GROUP_W)
            bm2 = xc[o:o + SUPER, SSD_WIDTH + g * SSD_STATE:SSD_WIDTH + (g + 1) * SSD_STATE]
            cm2 = xc[o:o + SUPER, SSD_WIDTH + (SSD_GROUPS + g) * SSD_STATE:SSD_WIDTH + (SSD_GROUPS + g + 1) * SSD_STATE]
            bmb = bm2.astype(BF16)
            cmb = cm2.astype(BF16)
            cb2 = lax.dot_general(cmb, bmb, (((1,), (1,)), ((), ())), preferred_element_type=F32)
            bm_t = bm2.T.astype(BF16)
            xd_g = xd[o:o + SUPER, gs]
            ms, xm = [], []
            for rr in range(SSD_HPG):
                r = g * SSD_HPG + rr
                seg = acs2[:, r:r + 1] - acs_t[r:r + 1, :]
                dec = jnp.where(mask2, jnp.exp(jnp.where(mask2, seg, 0.0)), 0.0)
                ms.append((cb2 * dec).astype(BF16))
                xm.append(jnp.where(lane_g == rr, xd_g, 0.0).astype(BF16))
            ybuf[o:o + SUPER, gs] = jnp.dot(jnp.concatenate(ms, axis=1), jnp.concatenate(xm, axis=0),
                                            preferred_element_type=F32)
            for j in range(nreal):
                a0, a1 = o + j * chunk, o + (j + 1) * chunk
                h_t = h_scr[g]
                yoff = jnp.dot(cmb[j * chunk:(j + 1) * chunk, :], h_t.astype(BF16), preferred_element_type=F32)
                ybuf[a0:a1, gs] = ybuf[a0:a1, gs] + yoff * eacs_x[a0:a1, gs]
                dte = jnp.exp(acs_x[a1 - 1:a1, gs] - acs_x[a0:a1, gs])
                xw = (xd[a0:a1, gs] * dte).astype(BF16)
                pieces = []
                if j > 0:
                    pieces.append(jnp.zeros((j * chunk, GROUP_W), BF16))
                pieces.append(xw)
                if (j + 1) * chunk < SUPER:
                    pieces.append(jnp.zeros((SUPER - (j + 1) * chunk, GROUP_W), BF16))
                xw2 = jnp.concatenate(pieces, axis=0) if len(pieces) > 1 else xw
                st = jnp.dot(bm_t, xw2, preferred_element_type=F32)
                h_scr[g] = h_t * eacs_x[a1 - 1:a1, gs] + st

    y = ybuf[...] + dsk_ref[...] * xs
    y = y * _silu(z_ref[...].astype(F32))
    for g in range(SSD_GROUPS):
        gs = slice(g * GROUP_W, (g + 1) * GROUP_W)
        y_ref[:, gs] = _rms(y[:, gs], nw_ref[:, gs]).astype(BF16)

    @pl.when(t == pl.num_programs(1) - 1)
    def _fin():
        hout_ref[...] = h_scr[...]


def _ssd_call(z, xbc, dt, h0_t, hist, cw, cbias, dtb, alog, dsk, nw, *, chunk, rows, real):
    b, t, _ = z.shape
    ii = np.arange(rows)
    tri = ((ii[:, None] // chunk == ii[None, :] // chunk) & (ii[None, :] <= ii[:, None])).astype(np.float32)
    e = np.zeros((LANES, SSD_WIDTH), np.float32)
    for r in range(SSD_HEADS):
        e[r, r * SSD_HEADDIM:(r + 1) * SSD_HEADDIM] = 1.0

    def row(width):
        return pl.BlockSpec((None, rows, width), lambda i, j: (i, j, 0))

    kern = functools.partial(_ssd_kernel, chunk=chunk, rows=rows, real=real)
    return pl.pallas_call(
        kern,
        grid=(b, t // rows),
        in_specs=[row(SSD_WIDTH), row(SSD_CONV_DIM), row(LANES),
                  pl.BlockSpec((None, SSD_GROUPS, SSD_STATE, GROUP_W), lambda i, j: (i, 0, 0, 0)),
                  pl.BlockSpec((None, SSD_CONV - 1, SSD_CONV_DIM), lambda i, j: (i, 0, 0)),
                  _const_spec((SSD_CONV, SSD_CONV_DIM)), _const_spec((1, SSD_CONV_DIM)),
                  _const_spec((1, LANES)), _const_spec((1, LANES)),
                  _const_spec((1, SSD_WIDTH)), _const_spec((1, SSD_WIDTH)),
                  _const_spec((rows, rows)), _const_spec((LANES, SSD_WIDTH))],
        out_specs=[row(SSD_WIDTH),
                   pl.BlockSpec((None, SSD_GROUPS, SSD_STATE, GROUP_W), lambda i, j: (i, 0, 0, 0)),
                   pl.BlockSpec((None, SSD_CONV - 1, SSD_CONV_DIM), lambda i, j: (i, 0, 0))],
        out_shape=[jax.ShapeDtypeStruct((b, t, SSD_WIDTH), BF16),
                   jax.ShapeDtypeStruct((b, SSD_GROUPS, SSD_STATE, GROUP_W), F32),
                   jax.ShapeDtypeStruct((b, SSD_CONV - 1, SSD_CONV_DIM), F32)],
        scratch_shapes=[pltpu.VMEM((SSD_GROUPS, SSD_STATE, GROUP_W), F32),
                        pltpu.VMEM((rows + SUBLANES, SSD_CONV_DIM), F32),
                        pltpu.VMEM((rows, SSD_WIDTH), F32)],
        compiler_params=pltpu.CompilerParams(dimension_semantics=("parallel", "arbitrary"),
                                             vmem_limit_bytes=VMEM_LIMIT),
        name="ssd",
    )(z, xbc, dt, h0_t, hist, cw, cbias, dtb, alog, dsk, nw, jnp.asarray(tri), jnp.asarray(e))


def _attn_kernel(scal_ref, qt_ref, k_ref, vt_ref, bias_ref, subw_ref, o_ref,
                 m_scr, l_scr, acc_scr, sa_scr, sb_scr, sc_scr, p_scr, qz_scr,
                 *, bq, bk, noff, out_scale):
    h = pl.program_id(1)
    qi = pl.program_id(2)
    kn0 = qi + (noff - 1)
    lam = scal_ref[0]
    cfar = scal_ref[1 + h]

    zero = jnp.zeros((DA_DK, bq), BF16)
    qz_scr[0, 0:DA_DK, :] = qt_ref[0:DA_DK, :]
    qz_scr[0, DA_DK:DA_DV, :] = zero
    qz_scr[1, 0:DA_DK, :] = zero
    qz_scr[1, DA_DK:DA_DV, :] = qt_ref[DA_DK:DA_DV, :]

    m_scr[...] = jnp.full(m_scr.shape, NEG, F32)
    l_scr[...] = jnp.zeros(l_scr.shape, F32)
    acc_scr[...] = jnp.zeros(acc_scr.shape, F32)

    nsub = bk // ATT_KT

    def fold(x):
        return x.reshape(ATT_KT // SUBLANES, SUBLANES, bq)

    def scores(s_buf, first, count):
        for mm in range(2):
            for e in range(count):
                start = pl.multiple_of((first + e) * bk, bk)
                s_buf[mm, e] = jnp.dot(k_ref[pl.ds(start, bk), :], qz_scr[mm], preferred_element_type=F32)

    def softmax_pv(s_buf, first, entries):
        n = len(entries)
        for mm in range(2):
            cand = None
            for e, (near, shift) in enumerate(entries):
                mx = None
                for t in range(nsub):
                    rows = slice(t * ATT_KT, (t + 1) * ATT_KT)
                    s = s_buf[mm, e, rows, :]
                    if near is not None:
                        s = s + bias_ref[near, rows, :]
                    pm = jnp.max(fold(s), axis=0)
                    mx = pm if mx is None else jnp.maximum(mx, pm)
                mx = jnp.max(mx, axis=0, keepdims=True) + shift
                cand = mx if cand is None else jnp.maximum(cand, mx)
            m_old = m_scr[mm]
            m_new = jnp.maximum(m_old, cand)
            ls = None
            for e, (near, shift) in enumerate(entries):
                off = m_new - shift
                for t in range(nsub):
                    rows = slice(t * ATT_KT, (t + 1) * ATT_KT)
                    s = s_buf[mm, e, rows, :]
                    if near is not None:
                        s = s + bias_ref[near, rows, :]
                    p = jnp.exp2(s - off)
                    p_scr[mm, e * bk + t * ATT_KT:e * bk + (t + 1) * ATT_KT, :] = p.astype(BF16)
                    ps = jnp.sum(fold(p), axis=0)
                    ls = ps if ls is None else ls + ps
            vts = [vt_ref[first + e] for e in range(n)]
            vt = jnp.concatenate(vts, axis=1) if n > 1 else vts[0]
            pv = jnp.dot(vt, p_scr[mm, 0:n * bk, :], preferred_element_type=F32)
            alpha = jnp.exp2(m_old - m_new)
            l_scr[mm] = alpha * l_scr[mm] + jnp.sum(ls, axis=0, keepdims=True)
            acc_scr[mm] = alpha * acc_scr[mm] + pv
            m_scr[mm] = m_new

    nfar = jnp.maximum(kn0, 0)
    odd = nfar % 2
    far = (None, cfar)

    near_pair = [(0, 0.0), (1, 0.0)]

    @pl.when(kn0 < 0)
    def _only_first():
        scores(sc_scr, 0, 1)
        softmax_pv(sc_scr, 0, [(1, 0.0)])

    @pl.when(kn0 >= 0)
    def _groups():
        @pl.when(odd == 1)
        def _single():
            scores(sc_scr, 0, 1)
            scores(sa_scr, 1, 2)
            softmax_pv(sc_scr, 0, [far])

        @pl.when(odd == 0)
        def _first_pair():
            scores(sa_scr, 0, 2)

        def far_pair(s_cur, s_next, cur):
            scores(s_next, cur + 2, 2)
            softmax_pv(s_cur, cur, [far, far])

        def far_body(j, carry):
            cur = odd + 4 * j
            far_pair(sa_scr, sb_scr, cur)
            far_pair(sb_scr, sa_scr, cur + 2)
            return carry

        npairs = nfar // 2
        lax.fori_loop(0, npairs // 2, far_body, 0)

        @pl.when(npairs % 2 == 1)
        def _tail_b():
            far_pair(sa_scr, sb_scr, kn0 - 2)
            softmax_pv(sb_scr, kn0, near_pair)

        @pl.when(npairs % 2 == 0)
        def _tail_a():
            softmax_pv(sa_scr, kn0, near_pair)

    o = acc_scr[0] * (1.0 / l_scr[0]) - acc_scr[1] * (lam / l_scr[1])
    o = o * lax.rsqrt(jnp.mean(o * o, axis=0, keepdims=True) + EPS) * (subw_ref[...] * out_scale)
    o_ref[...] = o.T.astype(BF16)


def _attn_call(scal, qt, kb, vt, bias, subw, *, bq, bk, noff, out_scale):
    b, nq = qt.shape[:2]
    tk = kb.shape[1]
    nkb = vt.shape[1]
    kern = functools.partial(_attn_kernel, bq=bq, bk=bk, noff=noff, out_scale=out_scale)
    return pl.pallas_call(
        kern,
        grid=(b, DA_HEADS, nq),
        in_specs=[pl.BlockSpec(memory_space=pltpu.SMEM),
                  pl.BlockSpec((None, None, DA_DV, bq), lambda i, h, j: (i, j, h, 0)),
                  pl.BlockSpec((None, tk, DA_DV), lambda i, h, j: (i, 0, h)),
                  pl.BlockSpec((None, nkb, DA_DV, bk), lambda i, h, j: (i, 0, h, 0)),
                  pl.BlockSpec((None, 2, bk, bq), lambda i, h, j: (h, 0, 0, 0)),
                  pl.BlockSpec((DA_DV, 1), lambda i, h, j: (0, 0))],
        out_specs=pl.BlockSpec((None, bq, DA_DV), lambda i, h, j: (i, j, h)),
        out_shape=jax.ShapeDtypeStruct((b, nq * bq, DA_WIDTH), BF16),
        scratch_shapes=[pltpu.VMEM((2, 1, bq), F32), pltpu.VMEM((2, 1, bq), F32),
                        pltpu.VMEM((2, DA_DV, bq), F32),
                        pltpu.VMEM((2, 2, bk, bq), F32), pltpu.VMEM((2, 2, bk, bq), F32),
                        pltpu.VMEM((2, 1, bk, bq), F32), pltpu.VMEM((2, 2 * bk, bq), BF16),
                        pltpu.VMEM((2, DA_DV, bq), BF16)],
        compiler_params=pltpu.CompilerParams(dimension_semantics=("parallel", "parallel", "arbitrary"),
                                             vmem_limit_bytes=VMEM_LIMIT),
        name="attn",
    )(scal, qt, kb, vt, bias, subw)


def _rel_bucket(rel):
    nb = REL_BUCKETS // 2
    max_exact = nb // 2
    n = jnp.abs(rel)
    nf = jnp.maximum(n, 1).astype(jnp.float32)
    large = max_exact + (jnp.log(nf / max_exact) / math.log(REL_MAX_DIST / max_exact)
                         * (nb - max_exact)).astype(jnp.int32)
    large = jnp.minimum(large, nb - 1)
    return jnp.where(rel > 0, nb, 0) + jnp.where(n < max_exact, n, large)


def _bias_tiles(rel_bias, qpos0, kpos0, bq, bk, tk_real):
    qpos = qpos0 + np.arange(bq)
    span = bq + bk - 1
    tiles = []
    for d in range(2):
        kpos = kpos0[d] + np.arange(bk)
        offs = (kpos0[d] - qpos0) + np.arange(-(bq - 1), bk)
        table = rel_bias[_rel_bucket(jnp.asarray(offs, jnp.int32))].astype(F32).T * LOG2E
        rev = jnp.pad(table[:, ::-1], ((0, 0), (0, 1)))
        skew = jnp.tile(rev, (1, bk))[:, :bk * span].reshape(DA_HEADS, bk, span)
        toep = skew[:, :, bk - 1:bk - 1 + bq]
        vis = (kpos[:, None] // CHUNK <= qpos[None, :] // CHUNK) & (kpos[:, None] < tk_real)
        tiles.append(jnp.where(jnp.asarray(vis)[None], toep, NEG))
    return jnp.stack(tiles, axis=1)


def _ffn_kernel(x_ref, ys_ref, ya_ref, mod_ref, wo1_ref, wo2_ref, nfw_ref, wu_ref, cw_ref, cb_ref, wd_ref, fw_ref,
                hist_ref, y_ref, tail_ref, tail_scr, buf_a, buf_b, x1_scr, h2_scr, act_scr, *, tm, real):
    t = pl.program_id(1)
    nh = FFN_CONV - 1
    lo = SUBLANES - nh

    @pl.when(t == 0)
    def _init():
        tail_scr[lo:SUBLANES, :] = hist_ref[...]

    mix = (jnp.dot(ys_ref[...], wo1_ref[...], preferred_element_type=F32)
           + jnp.dot(ya_ref[...], wo2_ref[...], preferred_element_type=F32))
    x1 = x_ref[...] + mod_ref[2:3, :] * mix
    x1_scr[...] = x1
    h2 = _rms(x1, nfw_ref[...]) * (1.0 + mod_ref[4:5, :]) + mod_ref[3:4, :]
    h2_scr[...] = h2.astype(BF16)

    def cols(j, half):
        return slice(half * D_FF + j * FFN_CN, half * D_FF + (j + 1) * FFN_CN)

    def up(j, buf):
        for half in range(2):
            buf[half, SUBLANES:SUBLANES + tm, :] = jnp.dot(h2_scr[...], wu_ref[:, cols(j, half)],
                                                            preferred_element_type=F32)

    def conv(j, buf, half):
        cs = cols(j, half)
        buf[half, lo:SUBLANES, :] = tail_scr[lo:SUBLANES, cs]
        c = cb_ref[:, cs]
        for i in range(FFN_CONV):
            c = c + cw_ref[i:i + 1, cs] * buf[half, lo + i:lo + i + tm, :]
        tail_scr[lo:SUBLANES, cs] = buf[half, lo + real:SUBLANES + real, :]
        return c

    bufs = (buf_a, buf_b)
    up(0, bufs[0])
    for j in range(FFN_NC):
        if j + 1 < FFN_NC:
            up(j + 1, bufs[(j + 1) % 2])
        cv = conv(j, bufs[j % 2], 0)
        cg = conv(j, bufs[j % 2], 1)
        act_scr[:, j * FFN_CN:(j + 1) * FFN_CN] = (_silu(cg) * cv).astype(BF16)

    f = jnp.dot(act_scr[...], wd_ref[...], preferred_element_type=F32)
    x2 = x1_scr[...] + mod_ref[5:6, :] * f
    y_ref[...] = _rms(x2, fw_ref[...])
    tail_ref[...] = tail_scr[lo:SUBLANES, :]


def _ffn_call(x, ys, ya, mod3, wo1, wo2, nfw, wu, cw, cb, wd, fw, hist, *, tm, real):
    b, t, d = x.shape
    nh = FFN_CONV - 1

    def row(width):
        return pl.BlockSpec((None, tm, width), lambda i, j: (i, j, 0))

    hist_spec = pl.BlockSpec((None, nh, 2 * D_FF), lambda i, j: (i, 0, 0))
    kern = functools.partial(_ffn_kernel, tm=tm, real=real)
    return pl.pallas_call(
        kern,
        grid=(b, t // tm),
        in_specs=[row(d), row(SSD_WIDTH), row(DA_WIDTH),
                  pl.BlockSpec((None, 6, d), lambda i, j: (i, 0, 0)),
                  _const_spec((SSD_WIDTH, d)), _const_spec((DA_WIDTH, d)), _const_spec((1, d)),
                  _const_spec((d, 2 * D_FF)), _const_spec((FFN_CONV, 2 * D_FF)), _const_spec((1, 2 * D_FF)),
                  _const_spec((D_FF, d)), _const_spec((1, d)),
                  hist_spec],
        out_specs=[row(d), hist_spec],
        out_shape=[jax.ShapeDtypeStruct((b, t, d), F32),
                   jax.ShapeDtypeStruct((b, nh, 2 * D_FF), F32)],
        scratch_shapes=[pltpu.VMEM((SUBLANES, 2 * D_FF), F32),
                        pltpu.VMEM((2, tm + SUBLANES, FFN_CN), F32), pltpu.VMEM((2, tm + SUBLANES, FFN_CN), F32),
                        pltpu.VMEM((tm, d), F32), pltpu.VMEM((tm, d), BF16), pltpu.VMEM((tm, D_FF), BF16)],
        compiler_params=pltpu.CompilerParams(dimension_semantics=("parallel", "arbitrary"),
                                             vmem_limit_bytes=VMEM_LIMIT),
        name="ffn",
    )(x, ys, ya, mod3, wo1, wo2, nfw, wu, cw, cb, wd, fw, hist)


def _pack_params(norm_mix_w, w_in, ssm_conv_w, ssm_conv_b, ssm_dt_bias, ssm_a_log, ssm_d, ssm_norm_w,
                 lambda_q1, lambda_k1, lambda_q2, lambda_k2, attn_subln_w, rel_bias, w_out,
                 norm_ffn_w, w_up, ffn_conv_w, ffn_conv_b, w_down, final_norm_w, layer):
    l = layer
    wz, wx, wdt, wq, wk, wv = jnp.split(w_in[l], IN_SPLITS, axis=-1)
    wdt = jnp.pad(wdt, ((0, 0), (0, LANES - SSD_HEADS)))
    w_cat = jnp.concatenate([wz, wx, wdt, wk, wv], axis=-1).astype(BF16)
    w_t = jnp.concatenate([wq * (DA_DK ** -0.5 * LOG2E), wv], axis=-1).T.astype(BF16)

    def pad_heads(v):
        return jnp.pad(v.astype(F32), (0, LANES - SSD_HEADS)).reshape(1, LANES)

    lam_init = 0.8 - 0.6 * math.exp(-0.3 * l)
    lam = (jnp.exp(jnp.sum(lambda_q1[l].astype(F32) * lambda_k1[l].astype(F32)))
           - jnp.exp(jnp.sum(lambda_q2[l].astype(F32) * lambda_k2[l].astype(F32))) + lam_init)
    far_bias = rel_bias[REL_BUCKETS // 2 - 1].astype(F32)
    return dict(
        norm_mix_w=norm_mix_w[l].reshape(1, D_MODEL), w_cat=w_cat, w_t=w_t,
        cw=ssm_conv_w[l], cbias=ssm_conv_b[l].reshape(1, SSD_CONV_DIM),
        dtb=pad_heads(ssm_dt_bias[l]), alog=pad_heads(ssm_a_log[l]),
        dsk=jnp.repeat(ssm_d[l].astype(F32), SSD_HEADDIM).reshape(1, SSD_WIDTH),
        ssm_nw=ssm_norm_w[l].reshape(1, SSD_WIDTH),
        scal=jnp.concatenate([lam.reshape(1), far_bias * LOG2E]).astype(F32), lam_init=lam_init,
        subw=attn_subln_w[l].reshape(DA_DV, 1), rel_bias=rel_bias,
        wo1=w_out[l][:SSD_WIDTH].astype(BF16), wo2=w_out[l][SSD_WIDTH:].astype(BF16),
        nfw=norm_ffn_w[l].reshape(1, D_MODEL),
        wu=w_up[l].astype(BF16), ffn_cw=ffn_conv_w[l], ffn_cb=ffn_conv_b[l].reshape(1, 2 * D_FF),
        wd=w_down[l].astype(BF16), fw=final_norm_w.reshape(1, D_MODEL),
    )


def _state_to_kernel(h):
    b = h.shape[0]
    h = h.reshape(b, SSD_GROUPS, SSD_HPG, SSD_HEADDIM, SSD_STATE)
    return jnp.transpose(h, (0, 1, 4, 2, 3)).reshape(b, SSD_GROUPS, SSD_STATE, GROUP_W)


def _state_from_kernel(h):
    b = h.shape[0]
    h = h.reshape(b, SSD_GROUPS, SSD_STATE, SSD_HPG, SSD_HEADDIM)
    return jnp.transpose(h, (0, 1, 3, 4, 2)).reshape(b, SSD_HEADS, SSD_HEADDIM, SSD_STATE)


def _run_group(x, mod, past_k, past_v, ssm_h0, ssm_conv_hist, ffn_conv_hist, p, *, tm, ssd_rows, bq, bk):
    b, t, d = x.shape
    past = 0 if past_k is None else past_k.shape[1]
    chunk = min(CHUNK, t)
    tp = max(t, SUPER)
    if tp != t:
        x = jnp.pad(x, ((0, 0), (0, tp - t), (0, 0)))
        tm = ssd_rows = bq = tp
    mod3 = mod.reshape(b, 6, d)

    assert tm == bq
    z, xbc, dt, k, v, kb, qt, vt = _inproj_call(x, mod3, p["norm_mix_w"], p["w_cat"], p["w_t"], tm)

    y_ssd, h_t, conv_new = _ssd_call(z, xbc, dt, _state_to_kernel(ssm_h0.astype(F32)), ssm_conv_hist.astype(F32),
                                     p["cw"], p["cbias"], p["dtb"], p["alog"], p["dsk"], p["ssm_nw"],
                                     chunk=chunk, rows=ssd_rows, real=min(t, ssd_rows))

    if past == 0:
        assert bq == bk and t % bq == 0
        k_all, vt_all, noff = kb, vt, 0
        bias = _bias_tiles(p["rel_bias"], bq, (0, bq), bq, bk, 2 * bq)
    else:
        assert past % bk == 0 and t <= bk and tp <= bk
        k_all = jnp.concatenate([past_k.reshape(b, past, DA_WIDTH).astype(BF16), kb[:, :t],
                                 jnp.zeros((b, bk - t, DA_WIDTH), BF16)], axis=1)
        past_vt = jnp.swapaxes(past_v.reshape(b, past // bk, bk, DA_WIDTH), 2, 3).astype(BF16)
        vt_all = jnp.concatenate([past_vt, jnp.pad(vt, ((0, 0), (0, 0), (0, 0), (0, bk - tp)))], axis=1)
        noff = past // bk
        bias = _bias_tiles(p["rel_bias"], past, (past - bk, past), bq, bk, past + t)
    y_att = _attn_call(p["scal"], qt, k_all, vt_all, bias, p["subw"], bq=bq, bk=bk, noff=noff,
                       out_scale=1.0 - p["lam_init"])

    y, ffn_new = _ffn_call(x, y_ssd, y_att, mod3, p["wo1"], p["wo2"], p["nfw"], p["wu"], p["ffn_cw"], p["ffn_cb"],
                           p["wd"], p["fw"], ffn_conv_hist.astype(F32), tm=tm, real=min(t, tm))
    return (y[:, :t], k[:, :t].reshape(b, t, DA_HEADS, 2 * DA_DK), v[:, :t].reshape(b, t, DA_HEADS, DA_DV),
            _state_from_kernel(h_t), conv_new, ffn_new)


def kernel(x_prompt, x_sample, c_prompt, c_sample, cache_k, cache_v, state_ssm, state_ssm_conv, state_ffn_conv, w_ada, b_ada, norm_mix_w, w_in, ssm_conv_w, ssm_conv_b, ssm_dt_bias, ssm_a_log, ssm_d, ssm_norm_w, lambda_q1, lambda_k1, lambda_q2, lambda_k2, attn_subln_w, rel_bias, w_out, norm_ffn_w, w_up, ffn_conv_w, ffn_conv_b, w_down, final_norm_w):
    bp, bs = x_prompt.shape[0], x_sample.shape[0]
    dt = x_prompt.dtype
    p = _pack_params(norm_mix_w, w_in, ssm_conv_w, ssm_conv_b, ssm_dt_bias, ssm_a_log, ssm_d, ssm_norm_w,
                     lambda_q1, lambda_k1, lambda_q2, lambda_k2, attn_subln_w, rel_bias, w_out,
                     norm_ffn_w, w_up, ffn_conv_w, ffn_conv_b, w_down, final_norm_w, 0)
    c_all = jnp.concatenate([c_prompt, c_sample], axis=0)
    npad = -c_all.shape[0] % SUBLANES
    c_all = jnp.pad(c_all, ((0, npad), (0, 0)))
    mod = _mod_call(c_all, w_ada[0], b_ada[0].reshape(1, -1))

    zeros = lambda *s: jnp.zeros(s, dt)
    out_p = _run_group(x_prompt, mod[:bp], None, None,
                       zeros(bp, SSD_HEADS, SSD_HEADDIM, SSD_STATE), zeros(bp, SSD_CONV - 1, SSD_CONV_DIM),
                       zeros(bp, FFN_CONV - 1, 2 * D_FF), p, tm=512, ssd_rows=256, bq=512, bk=512)
    out_s = _run_group(x_sample, mod[bp:bp + bs], cache_k[0], cache_v[0], state_ssm[0], state_ssm_conv[0],
                       state_ffn_conv[0], p, tm=SUPER, ssd_rows=SUPER, bq=SUPER, bk=512)
    y_p, k_p, v_p, h_p, c_p, f_p = out_p
    y_s, k_s, v_s, h_s, c_s, f_s = out_s
    return (y_p, y_s, k_p[None], v_p[None], h_p[None], c_p[None], f_p[None],
            k_s[None], v_s[None], h_s[None], c_s[None], f_s[None])
```

```python
import functools
import math

import numpy as np
import jax
import jax.numpy as jnp
from jax import lax
from jax.experimental import pallas as pl
from jax.experimental.pallas import tpu as pltpu

F32 = jnp.float32
BF16 = jnp.bfloat16
HIGHEST = lax.Precision.HIGHEST

D_MODEL = 1024
CHUNK = 64
SSD_WIDTH = 512
SSD_HEADDIM = 64
SSD_HEADS = 8
SSD_GROUPS = 2
SSD_HPG = 4
SSD_STATE = 128
SSD_CONV = 4
SSD_CONV_DIM = SSD_WIDTH + 2 * SSD_GROUPS * SSD_STATE
GROUP_W = SSD_HPG * SSD_HEADDIM
DA_WIDTH = 512
DA_DK = 64
DA_DV = 128
DA_HEADS = 4
REL_BUCKETS = 32
REL_MAX_DIST = 128
D_FF = 2816
FFN_CONV = 3
EPS = 1e-6
IN_SPLITS = (512, 1536, 1544, 2056, 2568)
LANES = 128
SUBLANES = 8
SUPER = 128
ATT_KT = 64
FFN_CN = 256
FFN_NC = D_FF // FFN_CN
NEG = -1e30
VMEM_LIMIT = 56 * 1024 * 1024

PZ, PX, PDT, PK, PV, PEND = 0, 512, 1536, 1664, 2176, 2688
LOG2E = math.log2(math.e)


def _silu(x):
    return x / (1.0 + jnp.exp(-x))


def _softplus(x):
    return jnp.maximum(x, 0.0) + jnp.log1p(jnp.exp(-jnp.abs(x)))


def _split3(x):
    hi = x.astype(BF16)
    r1 = x - hi.astype(F32)
    mid = r1.astype(BF16)
    lo = (r1 - mid.astype(F32)).astype(BF16)
    return hi, mid, lo


def _rms(x, w):
    return x * lax.rsqrt(jnp.mean(x * x, axis=-1, keepdims=True) + EPS) * w


def _const_spec(shape):
    nd = len(shape)
    return pl.BlockSpec(shape, lambda *_: (0,) * nd)


def _mod_kernel(c_ref, w_ref, b_ref, o_ref):
    a = _silu(c_ref[...]).astype(BF16)
    o_ref[...] = jnp.dot(a, w_ref[...].astype(BF16), preferred_element_type=F32) + b_ref[...]


def _mod_call(c, w_ada, b_ada):
    n, d = c.shape
    nout = w_ada.shape[1]
    tn = 1024
    return pl.pallas_call(
        _mod_kernel,
        grid=(nout // tn,),
        in_specs=[pl.BlockSpec((n, d), lambda j: (0, 0)),
                  pl.BlockSpec((d, tn), lambda j: (0, j)),
                  pl.BlockSpec((1, tn), lambda j: (0, j))],
        out_specs=pl.BlockSpec((n, tn), lambda j: (0, j)),
        out_shape=jax.ShapeDtypeStruct((n, nout), F32),
        name="mod",
    )(c, w_ada, b_ada)


def _inproj_kernel(x_ref, mod_ref, nw_ref, w_ref, wt_ref, hist_ref, cw_ref, cbias_ref,
                   zs_ref, xc_ref, dt_ref, k_ref, v_ref, kb_ref, qt_ref, vt_ref, cout_ref, cbuf, hb_scr, *, tm, real):
    t = pl.program_id(1)
    nconv = SSD_CONV - 1

    @pl.when(t == 0)
    def _init():
        cbuf[0:SUBLANES, :] = jnp.zeros((SUBLANES, SSD_CONV_DIM), F32)
        cbuf[SUBLANES - nconv:SUBLANES, :] = hist_ref[...]

    h = _rms(x_ref[...], nw_ref[...]) * (1.0 + mod_ref[1:2, :]) + mod_ref[0:1, :]
    hb_scr[...] = h.astype(BF16)

    def proj(a, b):
        return jnp.dot(hb_scr[...], w_ref[:, a:b], preferred_element_type=F32)

    def proj_t(a, b):
        return lax.dot_general(wt_ref[a:b, :], hb_scr[...], (((1,), (1,)), ((), ())), preferred_element_type=F32)

    zs_ref[...] = _silu(proj(PZ, PX)).astype(BF16)
    cbuf[SUBLANES:SUBLANES + tm, :] = proj(PX, PDT)
    conv = cbias_ref[...]
    for j in range(SSD_CONV):
        off = SUBLANES - nconv + j
        conv = conv + cw_ref[j:j + 1, :] * cbuf[off:off + tm, :]
    tail = cbuf[SUBLANES - nconv + real:SUBLANES + real, :]
    cout_ref[...] = tail
    cbuf[SUBLANES - nconv:SUBLANES, :] = tail
    xc_ref[...] = _silu(conv).astype(BF16)
    dt_ref[...] = proj(PDT, PK)
    k = proj(PK, PV)
    k_ref[...] = k
    kb_ref[...] = k.astype(BF16)
    v_ref[...] = proj(PV, PEND)
    qt_ref[...] = proj_t(0, DA_WIDTH).astype(BF16)
    vt_ref[...] = proj_t(DA_WIDTH, 2 * DA_WIDTH).astype(BF16)


def _inproj_call(x, mod3, norm_w, w_cat, w_t, hist, cw, cbias, *, tm, real):
    b, t, d = x.shape
    nt = t // tm

    def row(width):
        return pl.BlockSpec((None, tm, width), lambda i, j: (i, j, 0))

    def out(width, dtype):
        return jax.ShapeDtypeStruct((b, t, width), dtype)

    tspec = pl.BlockSpec((None, None, DA_WIDTH, tm), lambda i, j: (i, j, 0, 0))
    tshape = jax.ShapeDtypeStruct((b, nt, DA_WIDTH, tm), BF16)
    hist_spec = pl.BlockSpec((None, SSD_CONV - 1, SSD_CONV_DIM), lambda i, j: (i, 0, 0))
    return pl.pallas_call(
        functools.partial(_inproj_kernel, tm=tm, real=real),
        grid=(b, nt),
        in_specs=[row(d),
                  pl.BlockSpec((None, 6, d), lambda i, j: (i, 0, 0)),
                  _const_spec((1, d)),
                  _const_spec((d, PEND)),
                  _const_spec((2 * DA_WIDTH, d)),
                  hist_spec, _const_spec((SSD_CONV, SSD_CONV_DIM)), _const_spec((1, SSD_CONV_DIM))],
        out_specs=[row(512), row(1024), row(LANES), row(512), row(512), row(512), tspec, tspec, hist_spec],
        out_shape=[out(512, BF16), out(1024, BF16), out(LANES, F32),
                   out(512, F32), out(512, F32), out(512, BF16), tshape, tshape,
                   jax.ShapeDtypeStruct((b, SSD_CONV - 1, SSD_CONV_DIM), F32)],
        scratch_shapes=[pltpu.VMEM((tm + SUBLANES, SSD_CONV_DIM), F32), pltpu.VMEM((tm, d), BF16)],
        compiler_params=pltpu.CompilerParams(dimension_semantics=("parallel", "arbitrary"),
                                             vmem_limit_bytes=VMEM_LIMIT),
        name="inproj",
    )(x, mod3, norm_w, w_cat, w_t, hist, cw, cbias)


def _ssd_kernel(zs_ref, xc_ref, dt_ref, h0_ref, dtb_ref, alog_ref, dsk_ref, nw_ref, tri_ref, e_ref,
                y_ref, hout_ref, h_scr, ybuf, *, chunk, rows, real):
    t = pl.program_id(1)

    @pl.when(t == 0)
    def _init():
        h_scr[...] = h0_ref[...]

    xs = xc_ref[:, 0:SSD_WIDTH].astype(F32)
    dtv = _softplus(dt_ref[...] + dtb_ref[...])
    da = dtv * (-jnp.exp(alog_ref[...]))
    acs = jnp.dot(tri_ref[...], jnp.concatenate(_split3(da), axis=0), preferred_element_type=F32)
    dt_x = jnp.dot(jnp.concatenate(_split3(dtv), axis=1), e_ref[...], preferred_element_type=F32)
    acs_x = jnp.dot(jnp.concatenate(_split3(acs), axis=1), e_ref[...], preferred_element_type=F32)
    eacs_x = jnp.exp(acs_x)
    xd = xs * dt_x

    li = lax.broadcasted_iota(jnp.int32, (SUPER, SUPER), 0)
    si = lax.broadcasted_iota(jnp.int32, (SUPER, SUPER), 1)
    cshift = chunk.bit_length() - 1
    mask2 = ((li >> cshift) == (si >> cshift)) & (si <= li)
    lane_g = lax.broadcasted_iota(jnp.int32, (SUPER, GROUP_W), 1) >> (SSD_HEADDIM.bit_length() - 1)

    for sb in range(rows // SUPER):
        o = sb * SUPER
        nreal = (min(real, o + SUPER) - o) // chunk
        acs2 = acs[o:o + SUPER, :]
        acs_t = acs2.T
        for g in range(SSD_GROUPS):
            gs = slice(g * GROUP_W, (g + 1) * GROUP_W)
            bcol = SSD_WIDTH + g * SSD_STATE
            ccol = SSD_WIDTH + (SSD_GROUPS + g) * SSD_STATE
            bmb = xc_ref[o:o + SUPER, bcol:bcol + SSD_STATE]
            cmb = xc_ref[o:o + SUPER, ccol:ccol + SSD_STATE]
            cb2 = lax.dot_general(cmb, bmb, (((1,), (1,)), ((), ())), preferred_element_type=F32)
            bm_t = bmb.astype(F32).T.astype(BF16)
            ms = []
            for rr in range(SSD_HPG):
                r = g * SSD_HPG + rr
                seg = acs2[:, r:r + 1] - acs_t[r:r + 1, :]
                dec = jnp.where(mask2, jnp.exp(jnp.where(mask2, seg, 0.0)), 0.0)
                ms.append((cb2 * dec).astype(BF16))
            full = jnp.dot(jnp.concatenate(ms, axis=0), xd[o:o + SUPER, gs].astype(BF16),
                           preferred_element_type=F32)
            ydiag = full[0:SUPER]
            for rr in range(1, SSD_HPG):
                ydiag = jnp.where(lane_g == rr, full[rr * SUPER:(rr + 1) * SUPER], ydiag)
            ybuf[o:o + SUPER, gs] = ydiag
            for j in range(nreal):
                a0, a1 = o + j * chunk, o + (j + 1) * chunk
                h_t = h_scr[g]
                yoff = jnp.dot(cmb[j * chunk:(j + 1) * chunk, :], h_t.astype(BF16), preferred_element_type=F32)
                ybuf[a0:a1, gs] = ybuf[a0:a1, gs] + yoff * eacs_x[a0:a1, gs]
                dte = jnp.exp(acs_x[a1 - 1:a1, gs] - acs_x[a0:a1, gs])
                xw = (xd[a0:a1, gs] * dte).astype(BF16)
                pieces = []
                if j > 0:
                    pieces.append(jnp.zeros((j * chunk, GROUP_W), BF16))
                pieces.append(xw)
                if (j + 1) * chunk < SUPER:
                    pieces.append(jnp.zeros((SUPER - (j + 1) * chunk, GROUP_W), BF16))
                xw2 = jnp.concatenate(pieces, axis=0) if len(pieces) > 1 else xw
                st = jnp.dot(bm_t, xw2, preferred_element_type=F32)
                h_scr[g] = h_t * eacs_x[a1 - 1:a1, gs] + st

    y = (ybuf[...] + dsk_ref[...] * xs) * zs_ref[...].astype(F32)
    for g in range(SSD_GROUPS):
        gs = slice(g * GROUP_W, (g + 1) * GROUP_W)
        y_ref[:, gs] = _rms(y[:, gs], nw_ref[:, gs]).astype(BF16)

    @pl.when(t == pl.num_programs(1) - 1)
    def _fin():
        hout_ref[...] = h_scr[...]


def _ssd_call(zs, xc, dt, h0_t, dtb, alog, dsk, nw, *, chunk, rows, real):
    b, t, _ = zs.shape
    ii = np.arange(rows)
    tri = ((ii[:, None] // chunk == ii[None, :] // chunk) & (ii[None, :] <= ii[:, None])).astype(np.float32)
    e = np.zeros((LANES, SSD_WIDTH), np.float32)
    for r in range(SSD_HEADS):
        e[r, r * SSD_HEADDIM:(r + 1) * SSD_HEADDIM] = 1.0

    def row(width):
        return pl.BlockSpec((None, rows, width), lambda i, j: (i, j, 0))

    kern = functools.partial(_ssd_kernel, chunk=chunk, rows=rows, real=real)
    return pl.pallas_call(
        kern,
        grid=(b, t // rows),
        in_specs=[row(SSD_WIDTH), row(SSD_CONV_DIM), row(LANES),
                  pl.BlockSpec((None, SSD_GROUPS, SSD_STATE, GROUP_W), lambda i, j: (i, 0, 0, 0)),
                  _const_spec((1, LANES)), _const_spec((1, LANES)),
                  _const_spec((1, SSD_WIDTH)), _const_spec((1, SSD_WIDTH)),
                  _const_spec((rows, 3 * rows)), _const_spec((3 * LANES, SSD_WIDTH))],
        out_specs=[row(SSD_WIDTH),
                   pl.BlockSpec((None, SSD_GROUPS, SSD_STATE, GROUP_W), lambda i, j: (i, 0, 0, 0))],
        out_shape=[jax.ShapeDtypeStruct((b, t, SSD_WIDTH), BF16),
                   jax.ShapeDtypeStruct((b, SSD_GROUPS, SSD_STATE, GROUP_W), F32)],
        scratch_shapes=[pltpu.VMEM((SSD_GROUPS, SSD_STATE, GROUP_W), F32),
                        pltpu.VMEM((rows, SSD_WIDTH), F32)],
        compiler_params=pltpu.CompilerParams(dimension_semantics=("parallel", "arbitrary"),
                                             vmem_limit_bytes=VMEM_LIMIT),
        name="ssd",
    )(zs, xc, dt, h0_t, dtb, alog, dsk, nw,
      jnp.asarray(np.tile(tri, (1, 3)), BF16), jnp.asarray(np.tile(e, (3, 1)), BF16))


def _attn_kernel(scal_ref, qt_ref, k_ref, vt_ref, bias_ref, subw_ref, o_ref,
                 m_scr, l_scr, acc_scr, sa_scr, sb_scr, sc_scr, p_scr, qz_scr,
                 *, bq, bk, noff, out_scale):
    h = pl.program_id(1)
    qi = pl.program_id(2)
    kn0 = qi + (noff - 1)
    lam = scal_ref[0]
    cfar = scal_ref[1 + h]

    zero = jnp.zeros((DA_DK, bq), BF16)
    qz_scr[0, 0:DA_DK, :] = qt_ref[0:DA_DK, :]
    qz_scr[0, DA_DK:DA_DV, :] = zero
    qz_scr[1, 0:DA_DK, :] = zero
    qz_scr[1, DA_DK:DA_DV, :] = qt_ref[DA_DK:DA_DV, :]

    m_scr[...] = jnp.full(m_scr.shape, NEG, F32)
    l_scr[...] = jnp.zeros(l_scr.shape, F32)
    acc_scr[...] = jnp.zeros(acc_scr.shape, F32)

    nsub = bk // ATT_KT

    def fold(x):
        return x.reshape(ATT_KT // SUBLANES, SUBLANES, bq)

    def scores(s_buf, first, count):
        for mm in range(2):
            for e in range(count):
                start = pl.multiple_of((first + e) * bk, bk)
                s_buf[mm, e] = jnp.dot(k_ref[pl.ds(start, bk), :], qz_scr[mm], preferred_element_type=F32)

    def softmax_pv(s_buf, first, entries):
        n = len(entries)
        for mm in range(2):
            cand = None
            for e, (near, shift) in enumerate(entries):
                mx = None
                for t in range(nsub):
                    rows = slice(t * ATT_KT, (t + 1) * ATT_KT)
                    s = s_buf[mm, e, rows, :]
                    if near is not None:
                        s = s + bias_ref[near, rows, :]
                    pm = jnp.max(fold(s), axis=0)
                    mx = pm if mx is None else jnp.maximum(mx, pm)
                mx = jnp.max(mx, axis=0, keepdims=True) + shift
                cand = mx if cand is None else jnp.maximum(cand, mx)
            m_old = m_scr[mm]
            m_new = jnp.maximum(m_old, cand)
            ls = None
            for e, (near, shift) in enumerate(entries):
                off = m_new - shift
                for t in range(nsub):
                    rows = slice(t * ATT_KT, (t + 1) * ATT_KT)
                    s = s_buf[mm, e, rows, :]
                    if near is not None:
                        s = s + bias_ref[near, rows, :]
                    p = jnp.exp2(s - off)
                    p_scr[mm, e * bk + t * ATT_KT:e * bk + (t + 1) * ATT_KT, :] = p.astype(BF16)
                    ps = jnp.sum(fold(p), axis=0)
                    ls = ps if ls is None else ls + ps
            vts = [vt_ref[first + e] for e in range(n)]
            vt = jnp.concatenate(vts, axis=1) if n > 1 else vts[0]
            pv = jnp.dot(vt, p_scr[mm, 0:n * bk, :], preferred_element_type=F32)
            alpha = jnp.exp2(m_old - m_new)
            l_scr[mm] = alpha * l_scr[mm] + jnp.sum(ls, axis=0, keepdims=True)
            acc_scr[mm] = alpha * acc_scr[mm] + pv
            m_scr[mm] = m_new

    nfar = jnp.maximum(kn0, 0)
    odd = nfar % 2
    far = (None, cfar)

    near_pair = [(0, 0.0), (1, 0.0)]

    @pl.when(kn0 < 0)
    def _only_first():
        scores(sc_scr, 0, 1)
        softmax_pv(sc_scr, 0, [(1, 0.0)])

    @pl.when(kn0 >= 0)
    def _groups():
        @pl.when(odd == 1)
        def _single():
            scores(sc_scr, 0, 1)
            scores(sa_scr, 1, 2)
            softmax_pv(sc_scr, 0, [far])

        @pl.when(odd == 0)
        def _first_pair():
            scores(sa_scr, 0, 2)

        def far_pair(s_cur, s_next, cur):
            scores(s_next, cur + 2, 2)
            softmax_pv(s_cur, cur, [far, far])

        def far_body(j, carry):
            cur = odd + 4 * j
            far_pair(sa_scr, sb_scr, cur)
            far_pair(sb_scr, sa_scr, cur + 2)
            return carry

        npairs = nfar // 2
        lax.fori_loop(0, npairs // 2, far_body, 0)

        @pl.when(npairs % 2 == 1)
        def _tail_b():
            far_pair(sa_scr, sb_scr, kn0 - 2)
            softmax_pv(sb_scr, kn0, near_pair)

        @pl.when(npairs % 2 == 0)
        def _tail_a():
            softmax_pv(sa_scr, kn0, near_pair)

    o = acc_scr[0] * (1.0 / l_scr[0]) - acc_scr[1] * (lam / l_scr[1])
    o = o * lax.rsqrt(jnp.mean(o * o, axis=0, keepdims=True) + EPS) * (subw_ref[...] * out_scale)
    o_ref[...] = o.T.astype(BF16)


def _attn_call(scal, qt, kb, vt, bias, subw, *, bq, bk, noff, out_scale):
    b, nq = qt.shape[:2]
    tk = kb.shape[1]
    nkb = vt.shape[1]
    kern = functools.partial(_attn_kernel, bq=bq, bk=bk, noff=noff, out_scale=out_scale)
    return pl.pallas_call(
        kern,
        grid=(b, DA_HEADS, nq),
        in_specs=[pl.BlockSpec(memory_space=pltpu.SMEM),
                  pl.BlockSpec((None, None, DA_DV, bq), lambda i, h, j: (i, j, h, 0)),
                  pl.BlockSpec((None, tk, DA_DV), lambda i, h, j: (i, 0, h)),
                  pl.BlockSpec((None, nkb, DA_DV, bk), lambda i, h, j: (i, 0, h, 0)),
                  pl.BlockSpec((None, 2, bk, bq), lambda i, h, j: (h, 0, 0, 0)),
                  pl.BlockSpec((DA_DV, 1), lambda i, h, j: (0, 0))],
        out_specs=pl.BlockSpec((None, bq, DA_DV), lambda i, h, j: (i, j, h)),
        out_shape=jax.ShapeDtypeStruct((b, nq * bq, DA_WIDTH), BF16),
        scratch_shapes=[pltpu.VMEM((2, 1, bq), F32), pltpu.VMEM((2, 1, bq), F32),
                        pltpu.VMEM((2, DA_DV, bq), F32),
                        pltpu.VMEM((2, 2, bk, bq), F32), pltpu.VMEM((2, 2, bk, bq), F32),
                        pltpu.VMEM((2, 1, bk, bq), F32), pltpu.VMEM((2, 2 * bk, bq), BF16),
                        pltpu.VMEM((2, DA_DV, bq), BF16)],
        compiler_params=pltpu.CompilerParams(dimension_semantics=("parallel", "parallel", "arbitrary"),
                                             vmem_limit_bytes=VMEM_LIMIT),
        name="attn",
    )(scal, qt, kb, vt, bias, subw)


def _attn_cached_kernel(scal_ref, qt_ref, kc_ref, vc_ref, kn_ref, vtn_ref, bias0_ref, bias1_ref, subw_ref, o_ref,
                        m_scr, l_scr, acc_scr, qz_scr, *, bq, bk, out_scale):
    lam = scal_ref[0]
    nfar = kc_ref.shape[0] // (bk * DA_HEADS) - 1
    zero = jnp.zeros((DA_DK, bq), BF16)

    for h in range(DA_HEADS):
        hs = slice(h * DA_DV, (h + 1) * DA_DV)
        cfar = scal_ref[1 + h]
        qz_scr[0, 0:DA_DK, :] = qt_ref[h * DA_DV:h * DA_DV + DA_DK, :]
        qz_scr[0, DA_DK:DA_DV, :] = zero
        qz_scr[1, 0:DA_DK, :] = zero
        qz_scr[1, DA_DK:DA_DV, :] = qt_ref[h * DA_DV + DA_DK:(h + 1) * DA_DV, :]
        m_scr[...] = jnp.full(m_scr.shape, NEG, F32)
        l_scr[...] = jnp.zeros(l_scr.shape, F32)
        acc_scr[...] = jnp.zeros(acc_scr.shape, F32)

        def block(k, vt, bias, shift):
            for mm in range(2):
                s = jnp.dot(k, qz_scr[mm], preferred_element_type=F32)
                if bias is not None:
                    s = s + bias
                m_old = m_scr[mm]
                m_new = jnp.maximum(m_old, jnp.max(s, axis=0, keepdims=True) + shift)
                p = jnp.exp2(s - (m_new - shift))
                alpha = jnp.exp2(m_old - m_new)
                l_scr[mm] = alpha * l_scr[mm] + jnp.sum(p, axis=0, keepdims=True)
                acc_scr[mm] = alpha * acc_scr[mm] + jnp.dot(vt, p.astype(BF16), preferred_element_type=F32)
                m_scr[mm] = m_new

        def cached(ki, bias, shift, h=h):
            rows = pl.ds(ki * (bk * DA_HEADS) + h, bk, stride=DA_HEADS)
            block(kc_ref[rows, :].astype(BF16), vc_ref[rows, :].T.astype(BF16), bias, shift)

        def far_body(ki, carry, cached=cached, cfar=cfar):
            cached(ki, None, cfar)
            return carry

        lax.fori_loop(0, nfar, far_body, 0)
        cached(nfar, bias0_ref[h], 0.0)
        block(kn_ref[:, hs], vtn_ref[hs, :], bias1_ref[h], 0.0)

        o = acc_scr[0] * (1.0 / l_scr[0]) - acc_scr[1] * (lam / l_scr[1])
        o = o * lax.rsqrt(jnp.mean(o * o, axis=0, keepdims=True) + EPS) * (subw_ref[...] * out_scale)
        o_ref[:, hs] = o.T.astype(BF16)


def _attn_cached_call(scal, qt, cache_k, cache_v, kb, vt, bias0, bias1, subw, *, bk, out_scale):
    b, past = cache_k.shape[:2]
    bq = qt.shape[-1]
    rows = past * DA_HEADS
    kern = functools.partial(_attn_cached_kernel, bq=bq, bk=bk, out_scale=out_scale)
    cache_spec = pl.BlockSpec((None, rows, DA_DV), lambda i: (i, 0, 0))
    return pl.pallas_call(
        kern,
        grid=(b,),
        in_specs=[pl.BlockSpec(memory_space=pltpu.SMEM),
                  pl.BlockSpec((None, None, DA_WIDTH, bq), lambda i: (i, 0, 0, 0)),
                  cache_spec, cache_spec,
                  pl.BlockSpec((None, bq, DA_WIDTH), lambda i: (i, 0, 0)),
                  pl.BlockSpec((None, None, DA_WIDTH, bq), lambda i: (i, 0, 0, 0)),
                  _const_spec((DA_HEADS, bk, bq)), _const_spec((DA_HEADS, bq, bq)),
                  _const_spec((DA_DV, 1))],
        out_specs=pl.BlockSpec((None, bq, DA_WIDTH), lambda i: (i, 0, 0)),
        out_shape=jax.ShapeDtypeStruct((b, bq, DA_WIDTH), BF16),
        scratch_shapes=[pltpu.VMEM((2, 1, bq), F32), pltpu.VMEM((2, 1, bq), F32),
                        pltpu.VMEM((2, DA_DV, bq), F32), pltpu.VMEM((2, DA_DV, bq), BF16)],
        compiler_params=pltpu.CompilerParams(dimension_semantics=("parallel",), vmem_limit_bytes=VMEM_LIMIT),
        name="attn_cached",
    )(scal, qt, cache_k.reshape(b, rows, DA_DV), cache_v.reshape(b, rows, DA_DV), kb, vt, bias0, bias1, subw)


def _rel_bucket(rel):
    nb = REL_BUCKETS // 2
    max_exact = nb // 2
    n = jnp.abs(rel)
    nf = jnp.maximum(n, 1).astype(jnp.float32)
    large = max_exact + (jnp.log(nf / max_exact) / math.log(REL_MAX_DIST / max_exact)
                         * (nb - max_exact)).astype(jnp.int32)
    large = jnp.minimum(large, nb - 1)
    return jnp.where(rel > 0, nb, 0) + jnp.where(n < max_exact, n, large)


def _bias_tiles(rel_bias, qpos0, kpos0, bq, bk, tk_real):
    qpos = qpos0 + np.arange(bq)
    span = bq + bk - 1
    tiles = []
    for d in range(2):
        kpos = kpos0[d] + np.arange(bk)
        offs = (kpos0[d] - qpos0) + np.arange(-(bq - 1), bk)
        table = rel_bias[_rel_bucket(jnp.asarray(offs, jnp.int32))].astype(F32).T * LOG2E
        rev = jnp.pad(table[:, ::-1], ((0, 0), (0, 1)))
        skew = jnp.tile(rev, (1, bk))[:, :bk * span].reshape(DA_HEADS, bk, span)
        toep = skew[:, :, bk - 1:bk - 1 + bq]
        vis = (kpos[:, None] // CHUNK <= qpos[None, :] // CHUNK) & (kpos[:, None] < tk_real)
        tiles.append(jnp.where(jnp.asarray(vis)[None], toep, NEG))
    return jnp.stack(tiles, axis=1)


def _ffn_kernel(x_ref, ys_ref, ya_ref, mod_ref, wo1_ref, wo2_ref, nfw_ref, wu_ref, cw_ref, cb_ref, wd_ref, fw_ref,
                hist_ref, y_ref, tail_ref, tail_scr, buf_a, buf_b, x1_scr, h2_scr, act_scr, *, tm, real):
    t = pl.program_id(1)
    nh = FFN_CONV - 1
    lo = SUBLANES - nh

    @pl.when(t == 0)
    def _init():
        tail_scr[lo:SUBLANES, :] = hist_ref[...]

    mix = (jnp.dot(ys_ref[...], wo1_ref[...], preferred_element_type=F32)
           + jnp.dot(ya_ref[...], wo2_ref[...], preferred_element_type=F32))
    x1 = x_ref[...] + mod_ref[2:3, :] * mix
    x1_scr[...] = x1
    h2 = _rms(x1, nfw_ref[...]) * (1.0 + mod_ref[4:5, :]) + mod_ref[3:4, :]
    h2_scr[...] = h2.astype(BF16)

    def cols(j, half):
        return slice(half * D_FF + j * FFN_CN, half * D_FF + (j + 1) * FFN_CN)

    def up(j, buf):
        for half in range(2):
            buf[half, SUBLANES:SUBLANES + tm, :] = jnp.dot(h2_scr[...], wu_ref[:, cols(j, half)],
                                                            preferred_element_type=F32)

    def conv(j, buf, half):
        cs = cols(j, half)
        buf[half, lo:SUBLANES, :] = tail_scr[lo:SUBLANES, cs]
        c = cb_ref[:, cs]
        for i in range(FFN_CONV):
            c = c + cw_ref[i:i + 1, cs] * buf[half, lo + i:lo + i + tm, :]
        tail_scr[lo:SUBLANES, cs] = buf[half, lo + real:SUBLANES + real, :]
        return c

    bufs = (buf_a, buf_b)
    up(0, bufs[0])
    for j in range(FFN_NC):
        if j + 1 < FFN_NC:
            up(j + 1, bufs[(j + 1) % 2])
        cv = conv(j, bufs[j % 2], 0)
        cg = conv(j, bufs[j % 2], 1)
        act_scr[:, j * FFN_CN:(j + 1) * FFN_CN] = (_silu(cg) * cv).astype(BF16)

    f = jnp.dot(act_scr[...], wd_ref[...], preferred_element_type=F32)
    x2 = x1_scr[...] + mod_ref[5:6, :] * f
    y_ref[...] = _rms(x2, fw_ref[...])
    tail_ref[...] = tail_scr[lo:SUBLANES, :]


def _ffn_call(x, ys, ya, mod3, wo1, wo2, nfw, wu, cw, cb, wd, fw, hist, *, tm, real):
    b, t, d = x.shape
    nh = FFN_CONV - 1

    def row(width):
        return pl.BlockSpec((None, tm, width), lambda i, j: (i, j, 0))

    hist_spec = pl.BlockSpec((None, nh, 2 * D_FF), lambda i, j: (i, 0, 0))
    kern = functools.partial(_ffn_kernel, tm=tm, real=real)
    return pl.pallas_call(
        kern,
        grid=(b, t // tm),
        in_specs=[row(d), row(SSD_WIDTH), row(DA_WIDTH),
                  pl.BlockSpec((None, 6, d), lambda i, j: (i, 0, 0)),
                  _const_spec((SSD_WIDTH, d)), _const_spec((DA_WIDTH, d)), _const_spec((1, d)),
                  _const_spec((d, 2 * D_FF)), _const_spec((FFN_CONV, 2 * D_FF)), _const_spec((1, 2 * D_FF)),
                  _const_spec((D_FF, d)), _const_spec((1, d)),
                  hist_spec],
        out_specs=[row(d), hist_spec],
        out_shape=[jax.ShapeDtypeStruct((b, t, d), F32),
                   jax.ShapeDtypeStruct((b, nh, 2 * D_FF), F32)],
        scratch_shapes=[pltpu.VMEM((SUBLANES, 2 * D_FF), F32),
                        pltpu.VMEM((2, tm + SUBLANES, FFN_CN), F32), pltpu.VMEM((2, tm + SUBLANES, FFN_CN), F32),
                        pltpu.VMEM((tm, d), F32), pltpu.VMEM((tm, d), BF16), pltpu.VMEM((tm, D_FF), BF16)],
        compiler_params=pltpu.CompilerParams(dimension_semantics=("parallel", "arbitrary"),
                                             vmem_limit_bytes=VMEM_LIMIT),
        name="ffn",
    )(x, ys, ya, mod3, wo1, wo2, nfw, wu, cw, cb, wd, fw, hist)


def _pack_params(norm_mix_w, w_in, ssm_conv_w, ssm_conv_b, ssm_dt_bias, ssm_a_log, ssm_d, ssm_norm_w,
                 lambda_q1, lambda_k1, lambda_q2, lambda_k2, attn_subln_w, rel_bias, w_out,
                 norm_ffn_w, w_up, ffn_conv_w, ffn_conv_b, w_down, final_norm_w, layer):
    l = layer
    wz, wx, wdt, wq, wk, wv = jnp.split(w_in[l], IN_SPLITS, axis=-1)
    wdt = jnp.pad(wdt, ((0, 0), (0, LANES - SSD_HEADS)))
    w_cat = jnp.concatenate([wz, wx, wdt, wk, wv], axis=-1).astype(BF16)
    w_t = jnp.concatenate([wq * (DA_DK ** -0.5 * LOG2E), wv], axis=-1).T.astype(BF16)

    def pad_heads(v):
        return jnp.pad(v.astype(F32), (0, LANES - SSD_HEADS)).reshape(1, LANES)

    lam_init = 0.8 - 0.6 * math.exp(-0.3 * l)
    lam = (jnp.exp(jnp.sum(lambda_q1[l].astype(F32) * lambda_k1[l].astype(F32)))
           - jnp.exp(jnp.sum(lambda_q2[l].astype(F32) * lambda_k2[l].astype(F32))) + lam_init)
    far_bias = rel_bias[REL_BUCKETS // 2 - 1].astype(F32)
    return dict(
        norm_mix_w=norm_mix_w[l].reshape(1, D_MODEL), w_cat=w_cat, w_t=w_t,
        cw=ssm_conv_w[l], cbias=ssm_conv_b[l].reshape(1, SSD_CONV_DIM),
        dtb=pad_heads(ssm_dt_bias[l]), alog=pad_heads(ssm_a_log[l]),
        dsk=jnp.repeat(ssm_d[l].astype(F32), SSD_HEADDIM).reshape(1, SSD_WIDTH),
        ssm_nw=ssm_norm_w[l].reshape(1, SSD_WIDTH),
        scal=jnp.concatenate([lam.reshape(1), far_bias * LOG2E]).astype(F32), lam_init=lam_init,
        subw=attn_subln_w[l].reshape(DA_DV, 1), rel_bias=rel_bias,
        wo1=w_out[l][:SSD_WIDTH].astype(BF16), wo2=w_out[l][SSD_WIDTH:].astype(BF16),
        nfw=norm_ffn_w[l].reshape(1, D_MODEL),
        wu=w_up[l].astype(BF16), ffn_cw=ffn_conv_w[l], ffn_cb=ffn_conv_b[l].reshape(1, 2 * D_FF),
        wd=w_down[l].astype(BF16), fw=final_norm_w.reshape(1, D_MODEL),
    )


def _state_to_kernel(h):
    b = h.shape[0]
    h = h.reshape(b, SSD_GROUPS, SSD_HPG, SSD_HEADDIM, SSD_STATE)
    return jnp.transpose(h, (0, 1, 4, 2, 3)).reshape(b, SSD_GROUPS, SSD_STATE, GROUP_W)


def _state_from_kernel(h):
    b = h.shape[0]
    h = h.reshape(b, SSD_GROUPS, SSD_STATE, SSD_HPG, SSD_HEADDIM)
    return jnp.transpose(h, (0, 1, 3, 4, 2)).reshape(b, SSD_HEADS, SSD_HEADDIM, SSD_STATE)


def _run_group(x, mod, past_k, past_v, ssm_h0, ssm_conv_hist, ffn_conv_hist, p, *, tm, ssd_rows, bq, bk):
    b, t, d = x.shape
    past = 0 if past_k is None else past_k.shape[1]
    chunk = min(CHUNK, t)
    tp = max(t, SUPER)
    if tp != t:
        x = jnp.pad(x, ((0, 0), (0, tp - t), (0, 0)))
        tm = ssd_rows = bq = tp
    mod3 = mod.reshape(b, 6, d)

    assert tm == bq
    zs, xc, dt, k, v, kb, qt, vt, conv_new = _inproj_call(
        x, mod3, p["norm_mix_w"], p["w_cat"], p["w_t"], ssm_conv_hist.astype(F32), p["cw"], p["cbias"],
        tm=tm, real=min(t, tm))

    y_ssd, h_t = _ssd_call(zs, xc, dt, _state_to_kernel(ssm_h0.astype(F32)),
                           p["dtb"], p["alog"], p["dsk"], p["ssm_nw"],
                           chunk=chunk, rows=ssd_rows, real=min(t, ssd_rows))

    if past == 0:
        assert bq == bk and t % bq == 0
        bias = _bias_tiles(p["rel_bias"], bq, (0, bq), bq, bk, 2 * bq)
        y_att = _attn_call(p["scal"], qt, kb, vt, bias, p["subw"], bq=bq, bk=bk, noff=0,
                           out_scale=1.0 - p["lam_init"])
    else:
        assert past % bk == 0 and past >= bk and bq == tp <= bk
        bias = _bias_tiles(p["rel_bias"], past, (past - bk, past), bq, bk, past + t)
        y_att = _attn_cached_call(p["scal"], qt, past_k, past_v, kb, vt, bias[:, 0], bias[:, 1, :bq], p["subw"],
                                  bk=bk, out_scale=1.0 - p["lam_init"])

    y, ffn_new = _ffn_call(x, y_ssd, y_att, mod3, p["wo1"], p["wo2"], p["nfw"], p["wu"], p["ffn_cw"], p["ffn_cb"],
                           p["wd"], p["fw"], ffn_conv_hist.astype(F32), tm=tm, real=min(t, tm))
    return (y[:, :t], k[:, :t].reshape(b, t, DA_HEADS, 2 * DA_DK), v[:, :t].reshape(b, t, DA_HEADS, DA_DV),
            _state_from_kernel(h_t), conv_new, ffn_new)


def kernel(x_prompt, x_sample, c_prompt, c_sample, cache_k, cache_v, state_ssm, state_ssm_conv, state_ffn_conv, w_ada, b_ada, norm_mix_w, w_in, ssm_conv_w, ssm_conv_b, ssm_dt_bias, ssm_a_log, ssm_d, ssm_norm_w, lambda_q1, lambda_k1, lambda_q2, lambda_k2, attn_subln_w, rel_bias, w_out, norm_ffn_w, w_up, ffn_conv_w, ffn_conv_b, w_down, final_norm_w):
    bp, bs = x_prompt.shape[0], x_sample.shape[0]
    dt = x_prompt.dtype
    p = _pack_params(norm_mix_w, w_in, ssm_conv_w, ssm_conv_b, ssm_dt_bias, ssm_a_log, ssm_d, ssm_norm_w,
                     lambda_q1, lambda_k1, lambda_q2, lambda_k2, attn_subln_w, rel_bias, w_out,
                     norm_ffn_w, w_up, ffn_conv_w, ffn_conv_b, w_down, final_norm_w, 0)
    c_all = jnp.concatenate([c_prompt, c_sample], axis=0)
    npad = -c_all.shape[0] % SUBLANES
    c_all = jnp.pad(c_all, ((0, npad), (0, 0)))
    mod = _mod_call(c_all, w_ada[0], b_ada[0].reshape(1, -1))

    zeros = lambda *s: jnp.zeros(s, dt)
    out_p = _run_group(x_prompt, mod[:bp], None, None,
                       zeros(bp, SSD_HEADS, SSD_HEADDIM, SSD_STATE), zeros(bp, SSD_CONV - 1, SSD_CONV_DIM),
                       zeros(bp, FFN_CONV - 1, 2 * D_FF), p, tm=512, ssd_rows=256, bq=512, bk=512)
    out_s = _run_group(x_sample, mod[bp:bp + bs], cache_k[0], cache_v[0], state_ssm[0], state_ssm_conv[0],
                       state_ffn_conv[0], p, tm=SUPER, ssd_rows=SUPER, bq=SUPER, bk=512)
    y_p, k_p, v_p, h_p, c_p, f_p = out_p
    y_s, k_s, v_s, h_s, c_s, f_s = out_s
    return (y_p, y_s, k_p[None], v_p[None], h_p[None], c_p[None], f_p[None],
            k_s[None], v_s[None], h_s[None], c_s[None], f_s[None])
```

```python
import functools
import math

import numpy as np
import jax
import jax.numpy as jnp
from jax import lax
from jax.experimental import pallas as pl
from jax.experimental.pallas import tpu as pltpu

F32 = jnp.float32
BF16 = jnp.bfloat16
HIGHEST = lax.Precision.HIGHEST

D_MODEL = 1024
CHUNK = 64
SSD_WIDTH = 512
SSD_HEADDIM = 64
SSD_HEADS = 8
SSD_GROUPS = 2
SSD_HPG = 4
SSD_STATE = 128
SSD_CONV = 4
SSD_CONV_DIM = SSD_WIDTH + 2 * SSD_GROUPS * SSD_STATE
GROUP_W = SSD_HPG * SSD_HEADDIM
DA_WIDTH = 512
DA_DK = 64
DA_DV = 128
DA_HEADS = 4
REL_BUCKETS = 32
REL_MAX_DIST = 128
D_FF = 2816
FFN_CONV = 3
EPS = 1e-6
IN_SPLITS = (512, 1536, 1544, 2056, 2568)
LANES = 128
SUBLANES = 8
SUPER = 128
ATT_KT = 64
FFN_CN = 256
FFN_NC = D_FF // FFN_CN
NEG = -1e30
VMEM_LIMIT = 56 * 1024 * 1024

PZ, PX, PDT, PK, PV, PEND = 0, 512, 1536, 1664, 2176, 2688
LOG2E = math.log2(math.e)


def _silu(x):
    return x / (1.0 + jnp.exp(-x))


def _softplus(x):
    return jnp.maximum(x, 0.0) + jnp.log1p(jnp.exp(-jnp.abs(x)))


def _split3(x):
    hi = x.astype(BF16)
    r1 = x - hi.astype(F32)
    mid = r1.astype(BF16)
    lo = (r1 - mid.astype(F32)).astype(BF16)
    return hi, mid, lo


def _rms(x, w):
    return x * lax.rsqrt(jnp.mean(x * x, axis=-1, keepdims=True) + EPS) * w


def _const_spec(shape):
    nd = len(shape)
    return pl.BlockSpec(shape, lambda *_: (0,) * nd)


def _mod_kernel(c_ref, w_ref, b_ref, o_ref):
    a = _silu(c_ref[...]).astype(BF16)
    o_ref[...] = jnp.dot(a, w_ref[...].astype(BF16), preferred_element_type=F32) + b_ref[...]


def _mod_call(c, w_ada, b_ada):
    n, d = c.shape
    nout = w_ada.shape[1]
    tn = 1024
    return pl.pallas_call(
        _mod_kernel,
        grid=(nout // tn,),
        in_specs=[pl.BlockSpec((n, d), lambda j: (0, 0)),
                  pl.BlockSpec((d, tn), lambda j: (0, j)),
                  pl.BlockSpec((1, tn), lambda j: (0, j))],
        out_specs=pl.BlockSpec((n, tn), lambda j: (0, j)),
        out_shape=jax.ShapeDtypeStruct((n, nout), F32),
        name="mod",
    )(c, w_ada, b_ada)


def _inproj_kernel(x_ref, mod_ref, nw_ref, w_ref, wt_ref, hist_ref, cw_ref, cbias_ref,
                   zs_ref, xc_ref, dt_ref, k_ref, v_ref, kb_ref, qt_ref, vt_ref, cout_ref, cbuf, hb_scr, zbuf,
                   *, tm, real):
    t = pl.program_id(1)
    nconv = SSD_CONV - 1

    @pl.when(t == 0)
    def _init():
        cbuf[0:SUBLANES, :] = jnp.zeros((SUBLANES, SSD_CONV_DIM), F32)
        cbuf[SUBLANES - nconv:SUBLANES, :] = hist_ref[...]

    h = _rms(x_ref[...], nw_ref[...]) * (1.0 + mod_ref[1:2, :]) + mod_ref[0:1, :]
    hb_scr[...] = h.astype(BF16)

    def proj(a, b):
        return jnp.dot(hb_scr[...], w_ref[:, a:b], preferred_element_type=F32)

    def proj_t(a, b):
        return lax.dot_general(wt_ref[a:b, :], hb_scr[...], (((1,), (1,)), ((), ())), preferred_element_type=F32)

    cbuf[SUBLANES:SUBLANES + tm, :] = proj(PX, PDT)
    zbuf[...] = proj(PZ, PX)
    dt_ref[...] = proj(PDT, PK)
    k_ref[...] = proj(PK, PV)
    v_ref[...] = proj(PV, PEND)
    qt_ref[...] = proj_t(0, DA_WIDTH).astype(BF16)
    vt_ref[...] = proj_t(DA_WIDTH, 2 * DA_WIDTH).astype(BF16)
    kb_ref[...] = k_ref[...].astype(BF16)
    conv = cbias_ref[...]
    for j in range(SSD_CONV):
        off = SUBLANES - nconv + j
        conv = conv + cw_ref[j:j + 1, :] * cbuf[off:off + tm, :]
    tail = cbuf[SUBLANES - nconv + real:SUBLANES + real, :]
    cout_ref[...] = tail
    cbuf[SUBLANES - nconv:SUBLANES, :] = tail
    xc_ref[...] = _silu(conv).astype(BF16)
    zs_ref[...] = _silu(zbuf[...]).astype(BF16)


def _inproj_call(x, mod3, norm_w, w_cat, w_t, hist, cw, cbias, *, tm, real):
    b, t, d = x.shape
    nt = t // tm

    def row(width):
        return pl.BlockSpec((None, tm, width), lambda i, j: (i, j, 0))

    def out(width, dtype):
        return jax.ShapeDtypeStruct((b, t, width), dtype)

    tspec = pl.BlockSpec((None, None, DA_WIDTH, tm), lambda i, j: (i, j, 0, 0))
    tshape = jax.ShapeDtypeStruct((b, nt, DA_WIDTH, tm), BF16)
    hist_spec = pl.BlockSpec((None, SSD_CONV - 1, SSD_CONV_DIM), lambda i, j: (i, 0, 0))
    return pl.pallas_call(
        functools.partial(_inproj_kernel, tm=tm, real=real),
        grid=(b, nt),
        in_specs=[row(d),
                  pl.BlockSpec((None, 6, d), lambda i, j: (i, 0, 0)),
                  _const_spec((1, d)),
                  _const_spec((d, PEND)),
                  _const_spec((2 * DA_WIDTH, d)),
                  hist_spec, _const_spec((SSD_CONV, SSD_CONV_DIM)), _const_spec((1, SSD_CONV_DIM))],
        out_specs=[row(512), row(1024), row(LANES), row(512), row(512), row(512), tspec, tspec, hist_spec],
        out_shape=[out(512, BF16), out(1024, BF16), out(LANES, F32),
                   out(512, F32), out(512, F32), out(512, BF16), tshape, tshape,
                   jax.ShapeDtypeStruct((b, SSD_CONV - 1, SSD_CONV_DIM), F32)],
        scratch_shapes=[pltpu.VMEM((tm + SUBLANES, SSD_CONV_DIM), F32), pltpu.VMEM((tm, d), BF16),
                        pltpu.VMEM((tm, SSD_WIDTH), F32)],
        compiler_params=pltpu.CompilerParams(dimension_semantics=("parallel", "arbitrary"),
                                             vmem_limit_bytes=VMEM_LIMIT),
        name="inproj",
    )(x, mod3, norm_w, w_cat, w_t, hist, cw, cbias)


def _ssd_kernel(zs_ref, xc_ref, dt_ref, h0_ref, dtb_ref, alog_ref, dsk_ref, nw_ref, tri_ref, e_ref,
                y_ref, hout_ref, h_scr, ybuf, *, chunk, rows, real):
    t = pl.program_id(1)

    @pl.when(t == 0)
    def _init():
        h_scr[...] = h0_ref[...]

    xs = xc_ref[:, 0:SSD_WIDTH].astype(F32)
    dtv = _softplus(dt_ref[...] + dtb_ref[...])
    da = dtv * (-jnp.exp(alog_ref[...]))
    acs = jnp.dot(tri_ref[...], jnp.concatenate(_split3(da), axis=0), preferred_element_type=F32)
    dt_x = jnp.dot(jnp.concatenate(_split3(dtv), axis=1), e_ref[...], preferred_element_type=F32)
    acs_x = jnp.dot(jnp.concatenate(_split3(acs), axis=1), e_ref[...], preferred_element_type=F32)
    eacs_x = jnp.exp(acs_x)
    xd = xs * dt_x

    li = lax.broadcasted_iota(jnp.int32, (SUPER, SUPER), 0)
    si = lax.broadcasted_iota(jnp.int32, (SUPER, SUPER), 1)
    cshift = chunk.bit_length() - 1
    mask2 = ((li >> cshift) == (si >> cshift)) & (si <= li)
    lane_g = lax.broadcasted_iota(jnp.int32, (SUPER, GROUP_W), 1) >> (SSD_HEADDIM.bit_length() - 1)

    for sb in range(rows // SUPER):
        o = sb * SUPER
        nreal = (min(real, o + SUPER) - o) // chunk
        acs2 = acs[o:o + SUPER, :]
        acs_t = acs2.T
        for g in range(SSD_GROUPS):
            gs = slice(g * GROUP_W, (g + 1) * GROUP_W)
            bcol = SSD_WIDTH + g * SSD_STATE
            ccol = SSD_WIDTH + (SSD_GROUPS + g) * SSD_STATE
            bmb = xc_ref[o:o + SUPER, bcol:bcol + SSD_STATE]
            cmb = xc_ref[o:o + SUPER, ccol:ccol + SSD_STATE]
            cb2 = lax.dot_general(cmb, bmb, (((1,), (1,)), ((), ())), preferred_element_type=F32)
            bm_t = bmb.astype(F32).T.astype(BF16)
            ms = []
            for rr in range(SSD_HPG):
                r = g * SSD_HPG + rr
                seg = acs2[:, r:r + 1] - acs_t[r:r + 1, :]
                dec = jnp.where(mask2, jnp.exp(jnp.where(mask2, seg, 0.0)), 0.0)
                ms.append((cb2 * dec).astype(BF16))
            full = jnp.dot(jnp.concatenate(ms, axis=0), xd[o:o + SUPER, gs].astype(BF16),
                           preferred_element_type=F32)
            ydiag = full[0:SUPER]
            for rr in range(1, SSD_HPG):
                ydiag = jnp.where(lane_g == rr, full[rr * SUPER:(rr + 1) * SUPER], ydiag)
            ybuf[o:o + SUPER, gs] = ydiag
            for j in range(nreal):
                a0, a1 = o + j * chunk, o + (j + 1) * chunk
                h_t = h_scr[g]
                yoff = jnp.dot(cmb[j * chunk:(j + 1) * chunk, :], h_t.astype(BF16), preferred_element_type=F32)
                ybuf[a0:a1, gs] = ybuf[a0:a1, gs] + yoff * eacs_x[a0:a1, gs]
                dte = jnp.exp(acs_x[a1 - 1:a1, gs] - acs_x[a0:a1, gs])
                xw = (xd[a0:a1, gs] * dte).astype(BF16)
                pieces = []
                if j > 0:
                    pieces.append(jnp.zeros((j * chunk, GROUP_W), BF16))
                pieces.append(xw)
                if (j + 1) * chunk < SUPER:
                    pieces.append(jnp.zeros((SUPER - (j + 1) * chunk, GROUP_W), BF16))
                xw2 = jnp.concatenate(pieces, axis=0) if len(pieces) > 1 else xw
                st = jnp.dot(bm_t, xw2, preferred_element_type=F32)
                h_scr[g] = h_t * eacs_x[a1 - 1:a1, gs] + st

    y = (ybuf[...] + dsk_ref[...] * xs) * zs_ref[...].astype(F32)
    for g in range(SSD_GROUPS):
        gs = slice(g * GROUP_W, (g + 1) * GROUP_W)
        y_ref[:, gs] = _rms(y[:, gs], nw_ref[:, gs]).astype(BF16)

    @pl.when(t == pl.num_programs(1) - 1)
    def _fin():
        hout_ref[...] = h_scr[...]


def _ssd_call(zs, xc, dt, h0_t, dtb, alog, dsk, nw, *, chunk, rows, real):
    b, t, _ = zs.shape
    ii = np.arange(rows)
    tri = ((ii[:, None] // chunk == ii[None, :] // chunk) & (ii[None, :] <= ii[:, None])).astype(np.float32)
    e = np.zeros((LANES, SSD_WIDTH), np.float32)
    for r in range(SSD_HEADS):
        e[r, r * SSD_HEADDIM:(r + 1) * SSD_HEADDIM] = 1.0

    def row(width):
        return pl.BlockSpec((None, rows, width), lambda i, j: (i, j, 0))

    kern = functools.partial(_ssd_kernel, chunk=chunk, rows=rows, real=real)
    return pl.pallas_call(
        kern,
        grid=(b, t // rows),
        in_specs=[row(SSD_WIDTH), row(SSD_CONV_DIM), row(LANES),
                  pl.BlockSpec((None, SSD_GROUPS, SSD_STATE, GROUP_W), lambda i, j: (i, 0, 0, 0)),
                  _const_spec((1, LANES)), _const_spec((1, LANES)),
                  _const_spec((1, SSD_WIDTH)), _const_spec((1, SSD_WIDTH)),
                  _const_spec((rows, 3 * rows)), _const_spec((3 * LANES, SSD_WIDTH))],
        out_specs=[row(SSD_WIDTH),
                   pl.BlockSpec((None, SSD_GROUPS, SSD_STATE, GROUP_W), lambda i, j: (i, 0, 0, 0))],
        out_shape=[jax.ShapeDtypeStruct((b, t, SSD_WIDTH), BF16),
                   jax.ShapeDtypeStruct((b, SSD_GROUPS, SSD_STATE, GROUP_W), F32)],
        scratch_shapes=[pltpu.VMEM((SSD_GROUPS, SSD_STATE, GROUP_W), F32),
                        pltpu.VMEM((rows, SSD_WIDTH), F32)],
        compiler_params=pltpu.CompilerParams(dimension_semantics=("parallel", "arbitrary"),
                                             vmem_limit_bytes=VMEM_LIMIT),
        name="ssd",
    )(zs, xc, dt, h0_t, dtb, alog, dsk, nw,
      jnp.asarray(np.tile(tri, (1, 3)), BF16), jnp.asarray(np.tile(e, (3, 1)), BF16))


def _attn_kernel(scal_ref, qt_ref, k_ref, vt_ref, bias_ref, subw_ref, o_ref,
                 m_scr, l_scr, acc_scr, sa_scr, sb_scr, sc_scr, p_scr, qz_scr,
                 *, bq, bk, noff, out_scale):
    h = pl.program_id(1)
    qi = pl.program_id(2)
    kn0 = qi + (noff - 1)
    lam = scal_ref[0]
    cfar = scal_ref[1 + h]

    zero = jnp.zeros((DA_DK, bq), BF16)
    qz_scr[0, 0:DA_DK, :] = qt_ref[0:DA_DK, :]
    qz_scr[0, DA_DK:DA_DV, :] = zero
    qz_scr[1, 0:DA_DK, :] = zero
    qz_scr[1, DA_DK:DA_DV, :] = qt_ref[DA_DK:DA_DV, :]

    m_scr[...] = jnp.full(m_scr.shape, NEG, F32)
    l_scr[...] = jnp.zeros(l_scr.shape, F32)
    acc_scr[...] = jnp.zeros(acc_scr.shape, F32)

    nsub = bk // ATT_KT

    def fold(x):
        return x.reshape(ATT_KT // SUBLANES, SUBLANES, bq)

    def scores(s_buf, first, count):
        for mm in range(2):
            for e in range(count):
                start = pl.multiple_of((first + e) * bk, bk)
                s_buf[mm, e] = jnp.dot(k_ref[pl.ds(start, bk), :], qz_scr[mm], preferred_element_type=F32)

    def softmax_pv(s_buf, first, entries):
        n = len(entries)
        for mm in range(2):
            cand = None
            for e, (near, shift) in enumerate(entries):
                mx = None
                for t in range(nsub):
                    rows = slice(t * ATT_KT, (t + 1) * ATT_KT)
                    s = s_buf[mm, e, rows, :]
                    if near is not None:
                        s = s + bias_ref[near, rows, :]
                    pm = jnp.max(fold(s), axis=0)
                    mx = pm if mx is None else jnp.maximum(mx, pm)
                mx = jnp.max(mx, axis=0, keepdims=True) + shift
                cand = mx if cand is None else jnp.maximum(cand, mx)
            m_old = m_scr[mm]
            m_new = jnp.maximum(m_old, cand)
            ls = None
            for e, (near, shift) in enumerate(entries):
                off = m_new - shift
                for t in range(nsub):
                    rows = slice(t * ATT_KT, (t + 1) * ATT_KT)
                    s = s_buf[mm, e, rows, :]
                    if near is not None:
                        s = s + bias_ref[near, rows, :]
                    p = jnp.exp2(s - off)
                    p_scr[mm, e * bk + t * ATT_KT:e * bk + (t + 1) * ATT_KT, :] = p.astype(BF16)
                    ps = jnp.sum(fold(p), axis=0)
                    ls = ps if ls is None else ls + ps
            vts = [vt_ref[first + e] for e in range(n)]
            vt = jnp.concatenate(vts, axis=1) if n > 1 else vts[0]
            pv = jnp.dot(vt, p_scr[mm, 0:n * bk, :], preferred_element_type=F32)
            alpha = jnp.exp2(m_old - m_new)
            l_scr[mm] = alpha * l_scr[mm] + jnp.sum(ls, axis=0, keepdims=True)
            acc_scr[mm] = alpha * acc_scr[mm] + pv
            m_scr[mm] = m_new

    nfar = jnp.maximum(kn0, 0)
    odd = nfar % 2
    far = (None, cfar)

    near0, near1 = (0, 0.0), (1, 0.0)

    def step(s_cur, s_next, cur, entry, more=True):
        if more:
            scores(s_next, cur + 1, 1)
        softmax_pv(s_cur, cur, [entry])

    scores(sa_scr, 0, 1)

    @pl.when(kn0 < 0)
    def _only_first():
        softmax_pv(sa_scr, 0, [near1])

    @pl.when(kn0 >= 0)
    def _groups():
        def far_body(j, carry):
            step(sa_scr, sb_scr, 2 * j, far)
            step(sb_scr, sa_scr, 2 * j + 1, far)
            return carry

        lax.fori_loop(0, nfar // 2, far_body, 0)

        @pl.when(odd == 1)
        def _tail_odd():
            step(sa_scr, sb_scr, kn0 - 1, far)
            step(sb_scr, sa_scr, kn0, near0)
            step(sa_scr, sb_scr, kn0 + 1, near1, more=False)

        @pl.when(odd == 0)
        def _tail_even():
            step(sa_scr, sb_scr, kn0, near0)
            step(sb_scr, sa_scr, kn0 + 1, near1, more=False)

    o = acc_scr[0] * (1.0 / l_scr[0]) - acc_scr[1] * (lam / l_scr[1])
    o = o * lax.rsqrt(jnp.mean(o * o, axis=0, keepdims=True) + EPS) * (subw_ref[...] * out_scale)
    o_ref[...] = o.T.astype(BF16)


def _attn_call(scal, qt, kb, vt, bias, subw, *, bq, bk, noff, out_scale):
    b, nq = qt.shape[:2]
    tk = kb.shape[1]
    nkb = vt.shape[1]
    kern = functools.partial(_attn_kernel, bq=bq, bk=bk, noff=noff, out_scale=out_scale)
    return pl.pallas_call(
        kern,
        grid=(b, DA_HEADS, nq),
        in_specs=[pl.BlockSpec(memory_space=pltpu.SMEM),
                  pl.BlockSpec((None, None, DA_DV, bq), lambda i, h, j: (i, j, h, 0)),
                  pl.BlockSpec((None, tk, DA_DV), lambda i, h, j: (i, 0, h)),
                  pl.BlockSpec((None, nkb, DA_DV, bk), lambda i, h, j: (i, 0, h, 0)),
                  pl.BlockSpec((None, 2, bk, bq), lambda i, h, j: (h, 0, 0, 0)),
                  pl.BlockSpec((DA_DV, 1), lambda i, h, j: (0, 0))],
        out_specs=pl.BlockSpec((None, bq, DA_DV), lambda i, h, j: (i, j, h)),
        out_shape=jax.ShapeDtypeStruct((b, nq * bq, DA_WIDTH), BF16),
        scratch_shapes=[pltpu.VMEM((2, 1, bq), F32), pltpu.VMEM((2, 1, bq), F32),
                        pltpu.VMEM((2, DA_DV, bq), F32),
                        pltpu.VMEM((2, 2, bk, bq), F32), pltpu.VMEM((2, 2, bk, bq), F32),
                        pltpu.VMEM((2, 1, bk, bq), F32), pltpu.VMEM((2, 2 * bk, bq), BF16),
                        pltpu.VMEM((2, DA_DV, bq), BF16)],
        compiler_params=pltpu.CompilerParams(dimension_semantics=("parallel", "parallel", "arbitrary"),
                                             vmem_limit_bytes=VMEM_LIMIT),
        name="attn",
    )(scal, qt, kb, vt, bias, subw)


def _attn_cached_kernel(scal_ref, qt_ref, kc_ref, vc_ref, kn_ref, vtn_ref, bias0_ref, bias1_ref, subw_ref, o_ref,
                        m_scr, l_scr, acc_scr, qz_scr, *, bq, bk, out_scale):
    lam = scal_ref[0]
    nfar = kc_ref.shape[0] // (bk * DA_HEADS) - 1
    zero = jnp.zeros((DA_DK, bq), BF16)

    for h in range(DA_HEADS):
        hs = slice(h * DA_DV, (h + 1) * DA_DV)
        cfar = scal_ref[1 + h]
        qz_scr[0, 0:DA_DK, :] = qt_ref[h * DA_DV:h * DA_DV + DA_DK, :]
        qz_scr[0, DA_DK:DA_DV, :] = zero
        qz_scr[1, 0:DA_DK, :] = zero
        qz_scr[1, DA_DK:DA_DV, :] = qt_ref[h * DA_DV + DA_DK:(h + 1) * DA_DV, :]
        m_scr[...] = jnp.full(m_scr.shape, NEG, F32)
        l_scr[...] = jnp.zeros(l_scr.shape, F32)
        acc_scr[...] = jnp.zeros(acc_scr.shape, F32)

        def block(k, vt, bias, shift):
            for mm in range(2):
                s = jnp.dot(k, qz_scr[mm], preferred_element_type=F32)
                if bias is not None:
                    s = s + bias
                m_old = m_scr[mm]
                m_new = jnp.maximum(m_old, jnp.max(s, axis=0, keepdims=True) + shift)
                p = jnp.exp2(s - (m_new - shift))
                alpha = jnp.exp2(m_old - m_new)
                l_scr[mm] = alpha * l_scr[mm] + jnp.sum(p, axis=0, keepdims=True)
                acc_scr[mm] = alpha * acc_scr[mm] + jnp.dot(vt, p.astype(BF16), preferred_element_type=F32)
                m_scr[mm] = m_new

        def cached(ki, bias, shift, h=h):
            rows = pl.ds(ki * (bk * DA_HEADS) + h, bk, stride=DA_HEADS)
            block(kc_ref[rows, :].astype(BF16), vc_ref[rows, :].T.astype(BF16), bias, shift)

        def far_body(ki, carry, cached=cached, cfar=cfar):
            cached(ki, None, cfar)
            return carry

        lax.fori_loop(0, nfar, far_body, 0)
        cached(nfar, bias0_ref[h], 0.0)
        block(kn_ref[:, hs], vtn_ref[hs, :], bias1_ref[h], 0.0)

        o = acc_scr[0] * (1.0 / l_scr[0]) - acc_scr[1] * (lam / l_scr[1])
        o = o * lax.rsqrt(jnp.mean(o * o, axis=0, keepdims=True) + EPS) * (subw_ref[...] * out_scale)
        o_ref[:, hs] = o.T.astype(BF16)


def _attn_cached_call(scal, qt, cache_k, cache_v, kb, vt, bias0, bias1, subw, *, bk, out_scale):
    b, past = cache_k.shape[:2]
    bq = qt.shape[-1]
    rows = past * DA_HEADS
    kern = functools.partial(_attn_cached_kernel, bq=bq, bk=bk, out_scale=out_scale)
    cache_spec = pl.BlockSpec((None, rows, DA_DV), lambda i: (i, 0, 0))
    return pl.pallas_call(
        kern,
        grid=(b,),
        in_specs=[pl.BlockSpec(memory_space=pltpu.SMEM),
                  pl.BlockSpec((None, None, DA_WIDTH, bq), lambda i: (i, 0, 0, 0)),
                  cache_spec, cache_spec,
                  pl.BlockSpec((None, bq, DA_WIDTH), lambda i: (i, 0, 0)),
                  pl.BlockSpec((None, None, DA_WIDTH, bq), lambda i: (i, 0, 0, 0)),
                  _const_spec((DA_HEADS, bk, bq)), _const_spec((DA_HEADS, bq, bq)),
                  _const_spec((DA_DV, 1))],
        out_specs=pl.BlockSpec((None, bq, DA_WIDTH), lambda i: (i, 0, 0)),
        out_shape=jax.ShapeDtypeStruct((b, bq, DA_WIDTH), BF16),
        scratch_shapes=[pltpu.VMEM((2, 1, bq), F32), pltpu.VMEM((2, 1, bq), F32),
                        pltpu.VMEM((2, DA_DV, bq), F32), pltpu.VMEM((2, DA_DV, bq), BF16)],
        compiler_params=pltpu.CompilerParams(dimension_semantics=("parallel",), vmem_limit_bytes=VMEM_LIMIT),
        name="attn_cached",
    )(scal, qt, cache_k.reshape(b, rows, DA_DV), cache_v.reshape(b, rows, DA_DV), kb, vt, bias0, bias1, subw)


def _rel_bucket(rel):
    nb = REL_BUCKETS // 2
    max_exact = nb // 2
    n = jnp.abs(rel)
    nf = jnp.maximum(n, 1).astype(jnp.float32)
    large = max_exact + (jnp.log(nf / max_exact) / math.log(REL_MAX_DIST / max_exact)
                         * (nb - max_exact)).astype(jnp.int32)
    large = jnp.minimum(large, nb - 1)
    return jnp.where(rel > 0, nb, 0) + jnp.where(n < max_exact, n, large)


def _bias_tiles(rel_bias, qpos0, kpos0, bq, bk, tk_real):
    qpos = qpos0 + np.arange(bq)
    span = bq + bk - 1
    tiles = []
    for d in range(2):
        kpos = kpos0[d] + np.arange(bk)
        offs = (kpos0[d] - qpos0) + np.arange(-(bq - 1), bk)
        table = rel_bias[_rel_bucket(jnp.asarray(offs, jnp.int32))].astype(F32).T * LOG2E
        rev = jnp.pad(table[:, ::-1], ((0, 0), (0, 1)))
        skew = jnp.tile(rev, (1, bk))[:, :bk * span].reshape(DA_HEADS, bk, span)
        toep = skew[:, :, bk - 1:bk - 1 + bq]
        vis = (kpos[:, None] // CHUNK <= qpos[None, :] // CHUNK) & (kpos[:, None] < tk_real)
        tiles.append(jnp.where(jnp.asarray(vis)[None], toep, NEG))
    return jnp.stack(tiles, axis=1)


def _ffn_kernel(x_ref, ys_ref, ya_ref, mod_ref, wo1_ref, wo2_ref, nfw_ref, wu_ref, cw_ref, cb_ref, wd_ref, fw_ref,
                hist_ref, y_ref, tail_ref, tail_scr, buf_a, buf_b, x1_scr, h2_scr, act_scr, *, tm, real):
    t = pl.program_id(1)
    nh = FFN_CONV - 1
    lo = SUBLANES - nh

    @pl.when(t == 0)
    def _init():
        tail_scr[lo:SUBLANES, :] = hist_ref[...]

    mix = (jnp.dot(ys_ref[...], wo1_ref[...], preferred_element_type=F32)
           + jnp.dot(ya_ref[...], wo2_ref[...], preferred_element_type=F32))
    x1 = x_ref[...] + mod_ref[2:3, :] * mix
    x1_scr[...] = x1
    h2 = _rms(x1, nfw_ref[...]) * (1.0 + mod_ref[4:5, :]) + mod_ref[3:4, :]
    h2_scr[...] = h2.astype(BF16)

    def cols(j, half):
        return slice(half * D_FF + j * FFN_CN, half * D_FF + (j + 1) * FFN_CN)

    def up(j, buf):
        for half in range(2):
            buf[half, SUBLANES:SUBLANES + tm, :] = jnp.dot(h2_scr[...], wu_ref[:, cols(j, half)],
                                                            preferred_element_type=F32)

    def conv(j, buf, half):
        cs = cols(j, half)
        buf[half, lo:SUBLANES, :] = tail_scr[lo:SUBLANES, cs]
        c = cb_ref[:, cs]
        for i in range(FFN_CONV):
            c = c + cw_ref[i:i + 1, cs] * buf[half, lo + i:lo + i + tm, :]
        tail_scr[lo:SUBLANES, cs] = buf[half, lo + real:SUBLANES + real, :]
        return c

    bufs = (buf_a, buf_b)
    up(0, bufs[0])
    for j in range(FFN_NC):
        if j + 1 < FFN_NC:
            up(j + 1, bufs[(j + 1) % 2])
        cv = conv(j, bufs[j % 2], 0)
        cg = conv(j, bufs[j % 2], 1)
        act_scr[:, j * FFN_CN:(j + 1) * FFN_CN] = (_silu(cg) * cv).astype(BF16)

    f = jnp.dot(act_scr[...], wd_ref[...], preferred_element_type=F32)
    x2 = x1_scr[...] + mod_ref[5:6, :] * f
    y_ref[...] = _rms(x2, fw_ref[...])
    tail_ref[...] = tail_scr[lo:SUBLANES, :]


def _ffn_call(x, ys, ya, mod3, wo1, wo2, nfw, wu, cw, cb, wd, fw, hist, *, tm, real):
    b, t, d = x.shape
    nh = FFN_CONV - 1

    def row(width):
        return pl.BlockSpec((None, tm, width), lambda i, j: (i, j, 0))

    hist_spec = pl.BlockSpec((None, nh, 2 * D_FF), lambda i, j: (i, 0, 0))
    kern = functools.partial(_ffn_kernel, tm=tm, real=real)
    return pl.pallas_call(
        kern,
        grid=(b, t // tm),
        in_specs=[row(d), row(SSD_WIDTH), row(DA_WIDTH),
                  pl.BlockSpec((None, 6, d), lambda i, j: (i, 0, 0)),
                  _const_spec((SSD_WIDTH, d)), _const_spec((DA_WIDTH, d)), _const_spec((1, d)),
                  _const_spec((d, 2 * D_FF)), _const_spec((FFN_CONV, 2 * D_FF)), _const_spec((1, 2 * D_FF)),
                  _const_spec((D_FF, d)), _const_spec((1, d)),
                  hist_spec],
        out_specs=[row(d), hist_spec],
        out_shape=[jax.ShapeDtypeStruct((b, t, d), F32),
                   jax.ShapeDtypeStruct((b, nh, 2 * D_FF), F32)],
        scratch_shapes=[pltpu.VMEM((SUBLANES, 2 * D_FF), F32),
                        pltpu.VMEM((2, tm + SUBLANES, FFN_CN), F32), pltpu.VMEM((2, tm + SUBLANES, FFN_CN), F32),
                        pltpu.VMEM((tm, d), F32), pltpu.VMEM((tm, d), BF16), pltpu.VMEM((tm, D_FF), BF16)],
        compiler_params=pltpu.CompilerParams(dimension_semantics=("parallel", "arbitrary"),
                                             vmem_limit_bytes=VMEM_LIMIT),
        name="ffn",
    )(x, ys, ya, mod3, wo1, wo2, nfw, wu, cw, cb, wd, fw, hist)


def _pack_params(norm_mix_w, w_in, ssm_conv_w, ssm_conv_b, ssm_dt_bias, ssm_a_log, ssm_d, ssm_norm_w,
                 lambda_q1, lambda_k1, lambda_q2, lambda_k2, attn_subln_w, rel_bias, w_out,
                 norm_ffn_w, w_up, ffn_conv_w, ffn_conv_b, w_down, final_norm_w, layer):
    l = layer
    wz, wx, wdt, wq, wk, wv = jnp.split(w_in[l], IN_SPLITS, axis=-1)
    wdt = jnp.pad(wdt, ((0, 0), (0, LANES - SSD_HEADS)))
    w_cat = jnp.concatenate([wz, wx, wdt, wk, wv], axis=-1).astype(BF16)
    w_t = jnp.concatenate([wq * (DA_DK ** -0.5 * LOG2E), wv], axis=-1).T.astype(BF16)

    def pad_heads(v):
        return jnp.pad(v.astype(F32), (0, LANES - SSD_HEADS)).reshape(1, LANES)

    lam_init = 0.8 - 0.6 * math.exp(-0.3 * l)
    lam = (jnp.exp(jnp.sum(lambda_q1[l].astype(F32) * lambda_k1[l].astype(F32)))
           - jnp.exp(jnp.sum(lambda_q2[l].astype(F32) * lambda_k2[l].astype(F32))) + lam_init)
    far_bias = rel_bias[REL_BUCKETS // 2 - 1].astype(F32)
    return dict(
        norm_mix_w=norm_mix_w[l].reshape(1, D_MODEL), w_cat=w_cat, w_t=w_t,
        cw=ssm_conv_w[l], cbias=ssm_conv_b[l].reshape(1, SSD_CONV_DIM),
        dtb=pad_heads(ssm_dt_bias[l]), alog=pad_heads(ssm_a_log[l]),
        dsk=jnp.repeat(ssm_d[l].astype(F32), SSD_HEADDIM).reshape(1, SSD_WIDTH),
        ssm_nw=ssm_norm_w[l].reshape(1, SSD_WIDTH),
        scal=jnp.concatenate([lam.reshape(1), far_bias * LOG2E]).astype(F32), lam_init=lam_init,
        subw=attn_subln_w[l].reshape(DA_DV, 1), rel_bias=rel_bias,
        wo1=w_out[l][:SSD_WIDTH].astype(BF16), wo2=w_out[l][SSD_WIDTH:].astype(BF16),
        nfw=norm_ffn_w[l].reshape(1, D_MODEL),
        wu=w_up[l].astype(BF16), ffn_cw=ffn_conv_w[l], ffn_cb=ffn_conv_b[l].reshape(1, 2 * D_FF),
        wd=w_down[l].astype(BF16), fw=final_norm_w.reshape(1, D_MODEL),
    )


def _state_to_kernel(h):
    b = h.shape[0]
    h = h.reshape(b, SSD_GROUPS, SSD_HPG, SSD_HEADDIM, SSD_STATE)
    return jnp.transpose(h, (0, 1, 4, 2, 3)).reshape(b, SSD_GROUPS, SSD_STATE, GROUP_W)


def _state_from_kernel(h):
    b = h.shape[0]
    h = h.reshape(b, SSD_GROUPS, SSD_STATE, SSD_HPG, SSD_HEADDIM)
    return jnp.transpose(h, (0, 1, 3, 4, 2)).reshape(b, SSD_HEADS, SSD_HEADDIM, SSD_STATE)


def _run_group(x, mod, past_k, past_v, ssm_h0, ssm_conv_hist, ffn_conv_hist, p, *, tm, ssd_rows, bq, bk):
    b, t, d = x.shape
    past = 0 if past_k is None else past_k.shape[1]
    chunk = min(CHUNK, t)
    tp = max(t, SUPER)
    if tp != t:
        x = jnp.pad(x, ((0, 0), (0, tp - t), (0, 0)))
        tm = ssd_rows = bq = tp
    mod3 = mod.reshape(b, 6, d)

    assert tm == bq
    zs, xc, dt, k, v, kb, qt, vt, conv_new = _inproj_call(
        x, mod3, p["norm_mix_w"], p["w_cat"], p["w_t"], ssm_conv_hist.astype(F32), p["cw"], p["cbias"],
        tm=tm, real=min(t, tm))

    y_ssd, h_t = _ssd_call(zs, xc, dt, _state_to_kernel(ssm_h0.astype(F32)),
                           p["dtb"], p["alog"], p["dsk"], p["ssm_nw"],
                           chunk=chunk, rows=ssd_rows, real=min(t, ssd_rows))

    if past == 0:
        assert bq == bk and t % bq == 0
        bias = _bias_tiles(p["rel_bias"], bq, (0, bq), bq, bk, 2 * bq)
        y_att = _attn_call(p["scal"], qt, kb, vt, bias, p["subw"], bq=bq, bk=bk, noff=0,
                           out_scale=1.0 - p["lam_init"])
    else:
        assert past % bk == 0 and past >= bk and bq == tp <= bk
        bias = _bias_tiles(p["rel_bias"], past, (past - bk, past), bq, bk, past + t)
        y_att = _attn_cached_call(p["scal"], qt, past_k, past_v, kb, vt, bias[:, 0], bias[:, 1, :bq], p["subw"],
                                  bk=bk, out_scale=1.0 - p["lam_init"])

    y, ffn_new = _ffn_call(x, y_ssd, y_att, mod3, p["wo1"], p["wo2"], p["nfw"], p["wu"], p["ffn_cw"], p["ffn_cb"],
                           p["wd"], p["fw"], ffn_conv_hist.astype(F32), tm=tm, real=min(t, tm))
    return (y[:, :t], k[:, :t].reshape(b, t, DA_HEADS, 2 * DA_DK), v[:, :t].reshape(b, t, DA_HEADS, DA_DV),
            _state_from_kernel(h_t), conv_new, ffn_new)


def kernel(x_prompt, x_sample, c_prompt, c_sample, cache_k, cache_v, state_ssm, state_ssm_conv, state_ffn_conv, w_ada, b_ada, norm_mix_w, w_in, ssm_conv_w, ssm_conv_b, ssm_dt_bias, ssm_a_log, ssm_d, ssm_norm_w, lambda_q1, lambda_k1, lambda_q2, lambda_k2, attn_subln_w, rel_bias, w_out, norm_ffn_w, w_up, ffn_conv_w, ffn_conv_b, w_down, final_norm_w):
    bp, bs = x_prompt.shape[0], x_sample.shape[0]
    dt = x_prompt.dtype
    p = _pack_params(norm_mix_w, w_in, ssm_conv_w, ssm_conv_b, ssm_dt_bias, ssm_a_log, ssm_d, ssm_norm_w,
                     lambda_q1, lambda_k1, lambda_q2, lambda_k2, attn_subln_w, rel_bias, w_out,
                     norm_ffn_w, w_up, ffn_conv_w, ffn_conv_b, w_down, final_norm_w, 0)
    c_all = jnp.concatenate([c_prompt, c_sample], axis=0)
    npad = -c_all.shape[0] % SUBLANES
    c_all = jnp.pad(c_all, ((0, npad), (0, 0)))
    mod = _mod_call(c_all, w_ada[0], b_ada[0].reshape(1, -1))

    zeros = lambda *s: jnp.zeros(s, dt)
    out_p = _run_group(x_prompt, mod[:bp], None, None,
                       zeros(bp, SSD_HEADS, SSD_HEADDIM, SSD_STATE), zeros(bp, SSD_CONV - 1, SSD_CONV_DIM),
                       zeros(bp, FFN_CONV - 1, 2 * D_FF), p, tm=512, ssd_rows=256, bq=512, bk=512)
    out_s = _run_group(x_sample, mod[bp:bp + bs], cache_k[0], cache_v[0], state_ssm[0], state_ssm_conv[0],
                       state_ffn_conv[0], p, tm=SUPER, ssd_rows=SUPER, bq=SUPER, bk=512)
    y_p, k_p, v_p, h_p, c_p, f_p = out_p
    y_s, k_s, v_s, h_s, c_s, f_s = out_s
    return (y_p, y_s, k_p[None], v_p[None], h_p[None], c_p[None], f_p[None],
            k_s[None], v_s[None], h_s[None], c_s[None], f_s[None])
```

```python
import functools
import math

import numpy as np
import jax
import jax.numpy as jnp
from jax import lax
from jax.experimental import pallas as pl
from jax.experimental.pallas import tpu as pltpu

F32 = jnp.float32
BF16 = jnp.bfloat16
HIGHEST = lax.Precision.HIGHEST

D_MODEL = 1024
CHUNK = 64
SSD_WIDTH = 512
SSD_HEADDIM = 64
SSD_HEADS = 8
SSD_GROUPS = 2
SSD_HPG = 4
SSD_STATE = 128
SSD_CONV = 4
SSD_CONV_DIM = SSD_WIDTH + 2 * SSD_GROUPS * SSD_STATE
GROUP_W = SSD_HPG * SSD_HEADDIM
DA_WIDTH = 512
DA_DK = 64
DA_DV = 128
DA_HEADS = 4
REL_BUCKETS = 32
REL_MAX_DIST = 128
D_FF = 2816
FFN_CONV = 3
EPS = 1e-6
IN_SPLITS = (512, 1536, 1544, 2056, 2568)
LANES = 128
SUBLANES = 8
SUPER = 128
ATT_KT = 64
FFN_CN = 256
FFN_NC = D_FF // FFN_CN
NEG = -1e30
VMEM_LIMIT = 56 * 1024 * 1024

PZ, PX, PDT, PK, PV, PEND = 0, 512, 1536, 1664, 2176, 2688
LOG2E = math.log2(math.e)


def _silu(x):
    return x / (1.0 + jnp.exp(-x))


def _softplus(x):
    return jnp.maximum(x, 0.0) + jnp.log1p(jnp.exp(-jnp.abs(x)))


def _split3(x):
    hi = x.astype(BF16)
    r1 = x - hi.astype(F32)
    mid = r1.astype(BF16)
    lo = (r1 - mid.astype(F32)).astype(BF16)
    return hi, mid, lo


def _rms(x, w):
    return x * lax.rsqrt(jnp.mean(x * x, axis=-1, keepdims=True) + EPS) * w


def _const_spec(shape):
    nd = len(shape)
    return pl.BlockSpec(shape, lambda *_: (0,) * nd)


def _mod_kernel(c_ref, w_ref, b_ref, o_ref):
    a = _silu(c_ref[...]).astype(BF16)
    o_ref[...] = jnp.dot(a, w_ref[...].astype(BF16), preferred_element_type=F32) + b_ref[...]


def _mod_call(c, w_ada, b_ada):
    n, d = c.shape
    nout = w_ada.shape[1]
    tn = 1024
    return pl.pallas_call(
        _mod_kernel,
        grid=(nout // tn,),
        in_specs=[pl.BlockSpec((n, d), lambda j: (0, 0)),
                  pl.BlockSpec((d, tn), lambda j: (0, j)),
                  pl.BlockSpec((1, tn), lambda j: (0, j))],
        out_specs=pl.BlockSpec((n, tn), lambda j: (0, j)),
        out_shape=jax.ShapeDtypeStruct((n, nout), F32),
        name="mod",
    )(c, w_ada, b_ada)


def _inproj_kernel(x_ref, mod_ref, nw_ref, w_ref, wt_ref, hist_ref, cw_ref, cbias_ref,
                   zs_ref, xc_ref, dt_ref, k_ref, v_ref, kb_ref, qt_ref, vt_ref, cout_ref, cbuf, hb_scr, zbuf,
                   *, tm, real):
    t = pl.program_id(1)
    nconv = SSD_CONV - 1

    @pl.when(t == 0)
    def _init():
        cbuf[0:SUBLANES, :] = jnp.zeros((SUBLANES, SSD_CONV_DIM), F32)
        cbuf[SUBLANES - nconv:SUBLANES, :] = hist_ref[...]

    h = _rms(x_ref[...], nw_ref[...]) * (1.0 + mod_ref[1:2, :]) + mod_ref[0:1, :]
    hb_scr[...] = h.astype(BF16)

    def proj(a, b):
        return jnp.dot(hb_scr[...], w_ref[:, a:b], preferred_element_type=F32)

    def proj_t(a, b):
        return lax.dot_general(wt_ref[a:b, :], hb_scr[...], (((1,), (1,)), ((), ())), preferred_element_type=F32)

    cbuf[SUBLANES:SUBLANES + tm, :] = proj(PX, PDT)
    zbuf[...] = proj(PZ, PX)
    dt_ref[...] = proj(PDT, PK)
    k = proj(PK, PV)
    v = proj(PV, PEND)
    for hd in range(DA_HEADS):
        k_ref[:, hd, :] = k[:, hd * DA_DV:(hd + 1) * DA_DV]
        v_ref[:, hd, :] = v[:, hd * DA_DV:(hd + 1) * DA_DV]
    kb_ref[...] = k.astype(BF16)
    qt_ref[...] = proj_t(0, DA_WIDTH).astype(BF16)
    vt_ref[...] = proj_t(DA_WIDTH, 2 * DA_WIDTH).astype(BF16)
    conv = cbias_ref[...]
    for j in range(SSD_CONV):
        off = SUBLANES - nconv + j
        conv = conv + cw_ref[j:j + 1, :] * cbuf[off:off + tm, :]
    tail = cbuf[SUBLANES - nconv + real:SUBLANES + real, :]
    cout_ref[...] = tail
    cbuf[SUBLANES - nconv:SUBLANES, :] = tail
    xc_ref[...] = _silu(conv).astype(BF16)
    zs_ref[...] = _silu(zbuf[...]).astype(BF16)


def _inproj_call(x, mod3, norm_w, w_cat, w_t, hist, cw, cbias, *, tm, real):
    b, t, d = x.shape
    nt = t // tm

    def row(width):
        return pl.BlockSpec((None, tm, width), lambda i, j: (i, j, 0))

    def out(width, dtype):
        return jax.ShapeDtypeStruct((b, t, width), dtype)

    tspec = pl.BlockSpec((None, None, DA_WIDTH, tm), lambda i, j: (i, j, 0, 0))
    tshape = jax.ShapeDtypeStruct((b, nt, DA_WIDTH, tm), BF16)
    hist_spec = pl.BlockSpec((None, SSD_CONV - 1, SSD_CONV_DIM), lambda i, j: (i, 0, 0))
    hspec = pl.BlockSpec((None, tm, DA_HEADS, DA_DV), lambda i, j: (i, j, 0, 0))
    hshape = jax.ShapeDtypeStruct((b, t, DA_HEADS, DA_DV), F32)
    return pl.pallas_call(
        functools.partial(_inproj_kernel, tm=tm, real=real),
        grid=(b, nt),
        in_specs=[row(d),
                  pl.BlockSpec((None, 6, d), lambda i, j: (i, 0, 0)),
                  _const_spec((1, d)),
                  _const_spec((d, PEND)),
                  _const_spec((2 * DA_WIDTH, d)),
                  hist_spec, _const_spec((SSD_CONV, SSD_CONV_DIM)), _const_spec((1, SSD_CONV_DIM))],
        out_specs=[row(512), row(1024), row(LANES), hspec, hspec, row(512), tspec, tspec, hist_spec],
        out_shape=[out(512, BF16), out(1024, BF16), out(LANES, F32),
                   hshape, hshape, out(512, BF16), tshape, tshape,
                   jax.ShapeDtypeStruct((b, SSD_CONV - 1, SSD_CONV_DIM), F32)],
        scratch_shapes=[pltpu.VMEM((tm + SUBLANES, SSD_CONV_DIM), F32), pltpu.VMEM((tm, d), BF16),
                        pltpu.VMEM((tm, SSD_WIDTH), F32)],
        compiler_params=pltpu.CompilerParams(dimension_semantics=("parallel", "arbitrary"),
                                             vmem_limit_bytes=VMEM_LIMIT),
        name="inproj",
    )(x, mod3, norm_w, w_cat, w_t, hist, cw, cbias)


def _ssd_kernel(zs_ref, xc_ref, dt_ref, h0_ref, dtb_ref, alog_ref, dsk_ref, nw_ref, tri_ref, e_ref,
                y_ref, hout_ref, h_scr, ybuf, *, chunk, rows, real):
    t = pl.program_id(1)

    @pl.when(t == 0)
    def _init():
        h_scr[...] = h0_ref[...]

    xs = xc_ref[:, 0:SSD_WIDTH].astype(F32)
    dtv = _softplus(dt_ref[...] + dtb_ref[...])
    da = dtv * (-jnp.exp(alog_ref[...]))
    acs = jnp.dot(tri_ref[...], jnp.concatenate(_split3(da), axis=0), preferred_element_type=F32)
    dt_x = jnp.dot(jnp.concatenate(_split3(dtv), axis=1), e_ref[...], preferred_element_type=F32)
    acs_x = jnp.dot(jnp.concatenate(_split3(acs), axis=1), e_ref[...], preferred_element_type=F32)
    eacs_x = jnp.exp(acs_x)
    xd = xs * dt_x

    li = lax.broadcasted_iota(jnp.int32, (SUPER, SUPER), 0)
    si = lax.broadcasted_iota(jnp.int32, (SUPER, SUPER), 1)
    cshift = chunk.bit_length() - 1
    mask2 = ((li >> cshift) == (si >> cshift)) & (si <= li)
    lane_g = lax.broadcasted_iota(jnp.int32, (SUPER, GROUP_W), 1) >> (SSD_HEADDIM.bit_length() - 1)

    for sb in range(rows // SUPER):
        o = sb * SUPER
        nreal = (min(real, o + SUPER) - o) // chunk
        acs2 = acs[o:o + SUPER, :]
        acs_t = acs2.T
        for g in range(SSD_GROUPS):
            gs = slice(g * GROUP_W, (g + 1) * GROUP_W)
            bcol = SSD_WIDTH + g * SSD_STATE
            ccol = SSD_WIDTH + (SSD_GROUPS + g) * SSD_STATE
            bmb = xc_ref[o:o + SUPER, bcol:bcol + SSD_STATE]
            cmb = xc_ref[o:o + SUPER, ccol:ccol + SSD_STATE]
            cb2 = lax.dot_general(cmb, bmb, (((1,), (1,)), ((), ())), preferred_element_type=F32)
            bm_t = bmb.astype(F32).T.astype(BF16)
            ms = []
            for rr in range(SSD_HPG):
                r = g * SSD_HPG + rr
                seg = acs2[:, r:r + 1] - acs_t[r:r + 1, :]
                dec = jnp.where(mask2, jnp.exp(jnp.where(mask2, seg, 0.0)), 0.0)
                ms.append((cb2 * dec).astype(BF16))
            full = jnp.dot(jnp.concatenate(ms, axis=0), xd[o:o + SUPER, gs].astype(BF16),
                           preferred_element_type=F32)
            ydiag = full[0:SUPER]
            for rr in range(1, SSD_HPG):
                ydiag = jnp.where(lane_g == rr, full[rr * SUPER:(rr + 1) * SUPER], ydiag)
            ybuf[o:o + SUPER, gs] = ydiag
            for j in range(nreal):
                a0, a1 = o + j * chunk, o + (j + 1) * chunk
                h_t = h_scr[g]
                yoff = jnp.dot(cmb[j * chunk:(j + 1) * chunk, :], h_t.astype(BF16), preferred_element_type=F32)
                ybuf[a0:a1, gs] = ybuf[a0:a1, gs] + yoff * eacs_x[a0:a1, gs]
                dte = jnp.exp(acs_x[a1 - 1:a1, gs] - acs_x[a0:a1, gs])
                xw = (xd[a0:a1, gs] * dte).astype(BF16)
                pieces = []
                if j > 0:
                    pieces.append(jnp.zeros((j * chunk, GROUP_W), BF16))
                pieces.append(xw)
                if (j + 1) * chunk < SUPER:
                    pieces.append(jnp.zeros((SUPER - (j + 1) * chunk, GROUP_W), BF16))
                xw2 = jnp.concatenate(pieces, axis=0) if len(pieces) > 1 else xw
                st = jnp.dot(bm_t, xw2, preferred_element_type=F32)
                h_scr[g] = h_t * eacs_x[a1 - 1:a1, gs] + st

    y = (ybuf[...] + dsk_ref[...] * xs) * zs_ref[...].astype(F32)
    for g in range(SSD_GROUPS):
        gs = slice(g * GROUP_W, (g + 1) * GROUP_W)
        y_ref[:, gs] = _rms(y[:, gs], nw_ref[:, gs]).astype(BF16)

    @pl.when(t == pl.num_programs(1) - 1)
    def _fin():
        hout_ref[...] = h_scr[...]


def _ssd_call(zs, xc, dt, h0_t, dtb, alog, dsk, nw, *, chunk, rows, real):
    b, t, _ = zs.shape
    ii = np.arange(rows)
    tri = ((ii[:, None] // chunk == ii[None, :] // chunk) & (ii[None, :] <= ii[:, None])).astype(np.float32)
    e = np.zeros((LANES, SSD_WIDTH), np.float32)
    for r in range(SSD_HEADS):
        e[r, r * SSD_HEADDIM:(r + 1) * SSD_HEADDIM] = 1.0

    def row(width):
        return pl.BlockSpec((None, rows, width), lambda i, j: (i, j, 0))

    kern = functools.partial(_ssd_kernel, chunk=chunk, rows=rows, real=real)
    return pl.pallas_call(
        kern,
        grid=(b, t // rows),
        in_specs=[row(SSD_WIDTH), row(SSD_CONV_DIM), row(LANES),
                  pl.BlockSpec((None, SSD_GROUPS, SSD_STATE, GROUP_W), lambda i, j: (i, 0, 0, 0)),
                  _const_spec((1, LANES)), _const_spec((1, LANES)),
                  _const_spec((1, SSD_WIDTH)), _const_spec((1, SSD_WIDTH)),
                  _const_spec((rows, 3 * rows)), _const_spec((3 * LANES, SSD_WIDTH))],
        out_specs=[row(SSD_WIDTH),
                   pl.BlockSpec((None, SSD_GROUPS, SSD_STATE, GROUP_W), lambda i, j: (i, 0, 0, 0))],
        out_shape=[jax.ShapeDtypeStruct((b, t, SSD_WIDTH), BF16),
                   jax.ShapeDtypeStruct((b, SSD_GROUPS, SSD_STATE, GROUP_W), F32)],
        scratch_shapes=[pltpu.VMEM((SSD_GROUPS, SSD_STATE, GROUP_W), F32),
                        pltpu.VMEM((rows, SSD_WIDTH), F32)],
        compiler_params=pltpu.CompilerParams(dimension_semantics=("parallel", "arbitrary"),
                                             vmem_limit_bytes=VMEM_LIMIT),
        name="ssd",
    )(zs, xc, dt, h0_t, dtb, alog, dsk, nw,
      jnp.asarray(np.tile(tri, (1, 3)), BF16), jnp.asarray(np.tile(e, (3, 1)), BF16))


def _attn_kernel(scal_ref, qt_ref, k_ref, vt_ref, bias_ref, subw_ref, o_ref,
                 m_scr, l_scr, acc_scr, sa_scr, sb_scr, sc_scr, p_scr, qz_scr,
                 *, bq, bk, noff, out_scale):
    h = pl.program_id(1)
    qi = pl.program_id(2)
    kn0 = qi + (noff - 1)
    lam = scal_ref[0]
    cfar = scal_ref[1 + h]

    zero = jnp.zeros((DA_DK, bq), BF16)
    qz_scr[0, 0:DA_DK, :] = qt_ref[0:DA_DK, :]
    qz_scr[0, DA_DK:DA_DV, :] = zero
    qz_scr[1, 0:DA_DK, :] = zero
    qz_scr[1, DA_DK:DA_DV, :] = qt_ref[DA_DK:DA_DV, :]

    m_scr[...] = jnp.full(m_scr.shape, NEG, F32)
    l_scr[...] = jnp.zeros(l_scr.shape, F32)
    acc_scr[...] = jnp.zeros(acc_scr.shape, F32)

    nsub = bk // ATT_KT

    def fold(x):
        return x.reshape(ATT_KT // SUBLANES, SUBLANES, bq)

    def scores(s_buf, first, count):
        for mm in range(2):
            for e in range(count):
                start = pl.multiple_of((first + e) * bk, bk)
                s_buf[mm, e] = jnp.dot(k_ref[pl.ds(start, bk), :], qz_scr[mm], preferred_element_type=F32)

    def softmax_pv(s_buf, first, entries):
        n = len(entries)
        for mm in range(2):
            cand = None
            for e, (near, shift) in enumerate(entries):
                mx = None
                for t in range(nsub):
                    rows = slice(t * ATT_KT, (t + 1) * ATT_KT)
                    s = s_buf[mm, e, rows, :]
                    if near is not None:
                        s = s + bias_ref[near, rows, :]
                    pm = jnp.max(fold(s), axis=0)
                    mx = pm if mx is None else jnp.maximum(mx, pm)
                mx = jnp.max(mx, axis=0, keepdims=True) + shift
                cand = mx if cand is None else jnp.maximum(cand, mx)
            m_old = m_scr[mm]
            m_new = jnp.maximum(m_old, cand)
            ls = None
            for e, (near, shift) in enumerate(entries):
                off = m_new - shift
                for t in range(nsub):
                    rows = slice(t * ATT_KT, (t + 1) * ATT_KT)
                    s = s_buf[mm, e, rows, :]
                    if near is not None:
                        s = s + bias_ref[near, rows, :]
                    p = jnp.exp2(s - off)
                    p_scr[mm, e * bk + t * ATT_KT:e * bk + (t + 1) * ATT_KT, :] = p.astype(BF16)
                    ps = jnp.sum(fold(p), axis=0)
                    ls = ps if ls is None else ls + ps
            vts = [vt_ref[first + e] for e in range(n)]
            vt = jnp.concatenate(vts, axis=1) if n > 1 else vts[0]
            pv = jnp.dot(vt, p_scr[mm, 0:n * bk, :], preferred_element_type=F32)
            alpha = jnp.exp2(m_old - m_new)
            l_scr[mm] = alpha * l_scr[mm] + jnp.sum(ls, axis=0, keepdims=True)
            acc_scr[mm] = alpha * acc_scr[mm] + pv
            m_scr[mm] = m_new

    nfar = jnp.maximum(kn0, 0)
    odd = nfar % 2
    far = (None, cfar)

    near_pair = [(0, 0.0), (1, 0.0)]

    @pl.when(kn0 < 0)
    def _only_first():
        scores(sc_scr, 0, 1)
        softmax_pv(sc_scr, 0, [(1, 0.0)])

    @pl.when(kn0 >= 0)
    def _groups():
        @pl.when(odd == 1)
        def _single():
            scores(sc_scr, 0, 1)
            scores(sa_scr, 1, 2)
            softmax_pv(sc_scr, 0, [far])

        @pl.when(odd == 0)
        def _first_pair():
            scores(sa_scr, 0, 2)

        def far_pair(s_cur, s_next, cur):
            scores(s_next, cur + 2, 2)
            softmax_pv(s_cur, cur, [far, far])

        def far_body(j, carry):
            cur = odd + 4 * j
            far_pair(sa_scr, sb_scr, cur)
            far_pair(sb_scr, sa_scr, cur + 2)
            return carry

        npairs = nfar // 2
        lax.fori_loop(0, npairs // 2, far_body, 0)

        @pl.when(npairs % 2 == 1)
        def _tail_b():
            far_pair(sa_scr, sb_scr, kn0 - 2)
            softmax_pv(sb_scr, kn0, near_pair)

        @pl.when(npairs % 2 == 0)
        def _tail_a():
            softmax_pv(sa_scr, kn0, near_pair)

    o = acc_scr[0] * (1.0 / l_scr[0]) - acc_scr[1] * (lam / l_scr[1])
    o = o * lax.rsqrt(jnp.mean(o * o, axis=0, keepdims=True) + EPS) * (subw_ref[...] * out_scale)
    o_ref[...] = o.T.astype(BF16)


def _attn_call(scal, qt, kb, vt, bias, subw, *, bq, bk, noff, out_scale):
    b, nq = qt.shape[:2]
    tk = kb.shape[1]
    nkb = vt.shape[1]
    kern = functools.partial(_attn_kernel, bq=bq, bk=bk, noff=noff, out_scale=out_scale)
    return pl.pallas_call(
        kern,
        grid=(b, DA_HEADS, nq),
        in_specs=[pl.BlockSpec(memory_space=pltpu.SMEM),
                  pl.BlockSpec((None, None, DA_DV, bq), lambda i, h, j: (i, j, h, 0)),
                  pl.BlockSpec((None, tk, DA_DV), lambda i, h, j: (i, 0, h)),
                  pl.BlockSpec((None, nkb, DA_DV, bk), lambda i, h, j: (i, 0, h, 0)),
                  pl.BlockSpec((None, 2, bk, bq), lambda i, h, j: (h, 0, 0, 0)),
                  pl.BlockSpec((DA_DV, 1), lambda i, h, j: (0, 0))],
        out_specs=pl.BlockSpec((None, bq, DA_DV), lambda i, h, j: (i, j, h)),
        out_shape=jax.ShapeDtypeStruct((b, nq * bq, DA_WIDTH), BF16),
        scratch_shapes=[pltpu.VMEM((2, 1, bq), F32), pltpu.VMEM((2, 1, bq), F32),
                        pltpu.VMEM((2, DA_DV, bq), F32),
                        pltpu.VMEM((2, 2, bk, bq), F32), pltpu.VMEM((2, 2, bk, bq), F32),
                        pltpu.VMEM((2, 1, bk, bq), F32), pltpu.VMEM((2, 2 * bk, bq), BF16),
                        pltpu.VMEM((2, DA_DV, bq), BF16)],
        compiler_params=pltpu.CompilerParams(dimension_semantics=("parallel", "parallel", "arbitrary"),
                                             vmem_limit_bytes=VMEM_LIMIT),
        name="attn",
    )(scal, qt, kb, vt, bias, subw)


def _attn_cached_kernel(scal_ref, qt_ref, kc_ref, vc_ref, kn_ref, vtn_ref, bias0_ref, bias1_ref, subw_ref, o_ref,
                        m_scr, l_scr, acc_scr, qz_scr, *, bq, bk, out_scale):
    lam = scal_ref[0]
    nfar = kc_ref.shape[0] // (bk * DA_HEADS) - 1
    zero = jnp.zeros((DA_DK, bq), BF16)

    for h in range(DA_HEADS):
        hs = slice(h * DA_DV, (h + 1) * DA_DV)
        cfar = scal_ref[1 + h]
        qz_scr[0, 0:DA_DK, :] = qt_ref[h * DA_DV:h * DA_DV + DA_DK, :]
        qz_scr[0, DA_DK:DA_DV, :] = zero
        qz_scr[1, 0:DA_DK, :] = zero
        qz_scr[1, DA_DK:DA_DV, :] = qt_ref[h * DA_DV + DA_DK:(h + 1) * DA_DV, :]
        m_scr[...] = jnp.full(m_scr.shape, NEG, F32)
        l_scr[...] = jnp.zeros(l_scr.shape, F32)
        acc_scr[...] = jnp.zeros(acc_scr.shape, F32)

        def block(k, vt, bias, shift):
            for mm in range(2):
                s = jnp.dot(k, qz_scr[mm], preferred_element_type=F32)
                if bias is not None:
                    s = s + bias
                m_old = m_scr[mm]
                m_new = jnp.maximum(m_old, jnp.max(s, axis=0, keepdims=True) + shift)
                p = jnp.exp2(s - (m_new - shift))
                alpha = jnp.exp2(m_old - m_new)
                l_scr[mm] = alpha * l_scr[mm] + jnp.sum(p, axis=0, keepdims=True)
                acc_scr[mm] = alpha * acc_scr[mm] + jnp.dot(vt, p.astype(BF16), preferred_element_type=F32)
                m_scr[mm] = m_new

        def cached(ki, bias, shift, h=h):
            rows = pl.ds(ki * (bk * DA_HEADS) + h, bk, stride=DA_HEADS)
            block(kc_ref[rows, :].astype(BF16), vc_ref[rows, :].T.astype(BF16), bias, shift)

        def far_body(ki, carry, cached=cached, cfar=cfar):
            cached(ki, None, cfar)
            return carry

        lax.fori_loop(0, nfar, far_body, 0)
        cached(nfar, bias0_ref[h], 0.0)
        block(kn_ref[:, hs], vtn_ref[hs, :], bias1_ref[h], 0.0)

        o = acc_scr[0] * (1.0 / l_scr[0]) - acc_scr[1] * (lam / l_scr[1])
        o = o * lax.rsqrt(jnp.mean(o * o, axis=0, keepdims=True) + EPS) * (subw_ref[...] * out_scale)
        o_ref[:, hs] = o.T.astype(BF16)


def _attn_cached_call(scal, qt, cache_k, cache_v, kb, vt, bias0, bias1, subw, *, bk, out_scale):
    b, past = cache_k.shape[:2]
    bq = qt.shape[-1]
    rows = past * DA_HEADS
    kern = functools.partial(_attn_cached_kernel, bq=bq, bk=bk, out_scale=out_scale)
    cache_spec = pl.BlockSpec((None, rows, DA_DV), lambda i: (i, 0, 0))
    return pl.pallas_call(
        kern,
        grid=(b,),
        in_specs=[pl.BlockSpec(memory_space=pltpu.SMEM),
                  pl.BlockSpec((None, None, DA_WIDTH, bq), lambda i: (i, 0, 0, 0)),
                  cache_spec, cache_spec,
                  pl.BlockSpec((None, bq, DA_WIDTH), lambda i: (i, 0, 0)),
                  pl.BlockSpec((None, None, DA_WIDTH, bq), lambda i: (i, 0, 0, 0)),
                  _const_spec((DA_HEADS, bk, bq)), _const_spec((DA_HEADS, bq, bq)),
                  _const_spec((DA_DV, 1))],
        out_specs=pl.BlockSpec((None, bq, DA_WIDTH), lambda i: (i, 0, 0)),
        out_shape=jax.ShapeDtypeStruct((b, bq, DA_WIDTH), BF16),
        scratch_shapes=[pltpu.VMEM((2, 1, bq), F32), pltpu.VMEM((2, 1, bq), F32),
                        pltpu.VMEM((2, DA_DV, bq), F32), pltpu.VMEM((2, DA_DV, bq), BF16)],
        compiler_params=pltpu.CompilerParams(dimension_semantics=("parallel",), vmem_limit_bytes=VMEM_LIMIT),
        name="attn_cached",
    )(scal, qt, cache_k.reshape(b, rows, DA_DV), cache_v.reshape(b, rows, DA_DV), kb, vt, bias0, bias1, subw)


def _rel_bucket(rel):
    nb = REL_BUCKETS // 2
    max_exact = nb // 2
    n = jnp.abs(rel)
    nf = jnp.maximum(n, 1).astype(jnp.float32)
    large = max_exact + (jnp.log(nf / max_exact) / math.log(REL_MAX_DIST / max_exact)
                         * (nb - max_exact)).astype(jnp.int32)
    large = jnp.minimum(large, nb - 1)
    return jnp.where(rel > 0, nb, 0) + jnp.where(n < max_exact, n, large)


def _bias_tiles(rel_bias, qpos0, kpos0, bq, bk, tk_real):
    qpos = qpos0 + np.arange(bq)
    span = bq + bk - 1
    tiles = []
    for d in range(2):
        kpos = kpos0[d] + np.arange(bk)
        offs = (kpos0[d] - qpos0) + np.arange(-(bq - 1), bk)
        table = rel_bias[_rel_bucket(jnp.asarray(offs, jnp.int32))].astype(F32).T * LOG2E
        rev = jnp.pad(table[:, ::-1], ((0, 0), (0, 1)))
        skew = jnp.tile(rev, (1, bk))[:, :bk * span].reshape(DA_HEADS, bk, span)
        toep = skew[:, :, bk - 1:bk - 1 + bq]
        vis = (kpos[:, None] // CHUNK <= qpos[None, :] // CHUNK) & (kpos[:, None] < tk_real)
        tiles.append(jnp.where(jnp.asarray(vis)[None], toep, NEG))
    return jnp.stack(tiles, axis=1)


def _ffn_kernel(x_ref, ys_ref, ya_ref, mod_ref, wo1_ref, wo2_ref, nfw_ref, wu_ref, cw_ref, cb_ref, wd_ref, fw_ref,
                hist_ref, y_ref, tail_ref, tail_scr, buf_a, buf_b, x1_scr, h2_scr, act_scr, *, tm, real):
    t = pl.program_id(1)
    nh = FFN_CONV - 1
    lo = SUBLANES - nh

    @pl.when(t == 0)
    def _init():
        tail_scr[lo:SUBLANES, :] = hist_ref[...]

    mix = (jnp.dot(ys_ref[...], wo1_ref[...], preferred_element_type=F32)
           + jnp.dot(ya_ref[...], wo2_ref[...], preferred_element_type=F32))
    x1 = x_ref[...] + mod_ref[2:3, :] * mix
    x1_scr[...] = x1
    h2 = _rms(x1, nfw_ref[...]) * (1.0 + mod_ref[4:5, :]) + mod_ref[3:4, :]
    h2_scr[...] = h2.astype(BF16)

    def cols(j, half):
        return slice(half * D_FF + j * FFN_CN, half * D_FF + (j + 1) * FFN_CN)

    def up(j, buf):
        for half in range(2):
            buf[half, SUBLANES:SUBLANES + tm, :] = jnp.dot(h2_scr[...], wu_ref[:, cols(j, half)],
                                                            preferred_element_type=F32)

    def conv(j, buf, half):
        cs = cols(j, half)
        buf[half, lo:SUBLANES, :] = tail_scr[lo:SUBLANES, cs]
        c = cb_ref[:, cs]
        for i in range(FFN_CONV):
            c = c + cw_ref[i:i + 1, cs] * buf[half, lo + i:lo + i + tm, :]
        tail_scr[lo:SUBLANES, cs] = buf[half, lo + real:SUBLANES + real, :]
        return c

    bufs = (buf_a, buf_b)
    up(0, bufs[0])
    for j in range(FFN_NC):
        if j + 1 < FFN_NC:
            up(j + 1, bufs[(j + 1) % 2])
        cv = conv(j, bufs[j % 2], 0)
        cg = conv(j, bufs[j % 2], 1)
        act_scr[:, j * FFN_CN:(j + 1) * FFN_CN] = (_silu(cg) * cv).astype(BF16)

    f = jnp.dot(act_scr[...], wd_ref[...], preferred_element_type=F32)
    x2 = x1_scr[...] + mod_ref[5:6, :] * f
    y_ref[...] = _rms(x2, fw_ref[...])
    tail_ref[...] = tail_scr[lo:SUBLANES, :]


def _ffn_call(x, ys, ya, mod3, wo1, wo2, nfw, wu, cw, cb, wd, fw, hist, *, tm, real):
    b, t, d = x.shape
    nh = FFN_CONV - 1

    def row(width):
        return pl.BlockSpec((None, tm, width), lambda i, j: (i, j, 0))

    hist_spec = pl.BlockSpec((None, nh, 2 * D_FF), lambda i, j: (i, 0, 0))
    kern = functools.partial(_ffn_kernel, tm=tm, real=real)
    return pl.pallas_call(
        kern,
        grid=(b, t // tm),
        in_specs=[row(d), row(SSD_WIDTH), row(DA_WIDTH),
                  pl.BlockSpec((None, 6, d), lambda i, j: (i, 0, 0)),
                  _const_spec((SSD_WIDTH, d)), _const_spec((DA_WIDTH, d)), _const_spec((1, d)),
                  _const_spec((d, 2 * D_FF)), _const_spec((FFN_CONV, 2 * D_FF)), _const_spec((1, 2 * D_FF)),
                  _const_spec((D_FF, d)), _const_spec((1, d)),
                  hist_spec],
        out_specs=[row(d), hist_spec],
        out_shape=[jax.ShapeDtypeStruct((b, t, d), F32),
                   jax.ShapeDtypeStruct((b, nh, 2 * D_FF), F32)],
        scratch_shapes=[pltpu.VMEM((SUBLANES, 2 * D_FF), F32),
                        pltpu.VMEM((2, tm + SUBLANES, FFN_CN), F32), pltpu.VMEM((2, tm + SUBLANES, FFN_CN), F32),
                        pltpu.VMEM((tm, d), F32), pltpu.VMEM((tm, d), BF16), pltpu.VMEM((tm, D_FF), BF16)],
        compiler_params=pltpu.CompilerParams(dimension_semantics=("parallel", "arbitrary"),
                                             vmem_limit_bytes=VMEM_LIMIT),
        name="ffn",
    )(x, ys, ya, mod3, wo1, wo2, nfw, wu, cw, cb, wd, fw, hist)


def _pack_params(norm_mix_w, w_in, ssm_conv_w, ssm_conv_b, ssm_dt_bias, ssm_a_log, ssm_d, ssm_norm_w,
                 lambda_q1, lambda_k1, lambda_q2, lambda_k2, attn_subln_w, rel_bias, w_out,
                 norm_ffn_w, w_up, ffn_conv_w, ffn_conv_b, w_down, final_norm_w, layer):
    l = layer
    wz, wx, wdt, wq, wk, wv = jnp.split(w_in[l], IN_SPLITS, axis=-1)
    wdt = jnp.pad(wdt, ((0, 0), (0, LANES - SSD_HEADS)))
    w_cat = jnp.concatenate([wz, wx, wdt, wk, wv], axis=-1).astype(BF16)
    w_t = jnp.concatenate([wq * (DA_DK ** -0.5 * LOG2E), wv], axis=-1).T.astype(BF16)

    def pad_heads(v):
        return jnp.pad(v.astype(F32), (0, LANES - SSD_HEADS)).reshape(1, LANES)

    lam_init = 0.8 - 0.6 * math.exp(-0.3 * l)
    lam = (jnp.exp(jnp.sum(lambda_q1[l].astype(F32) * lambda_k1[l].astype(F32)))
           - jnp.exp(jnp.sum(lambda_q2[l].astype(F32) * lambda_k2[l].astype(F32))) + lam_init)
    far_bias = rel_bias[REL_BUCKETS // 2 - 1].astype(F32)
    return dict(
        norm_mix_w=norm_mix_w[l].reshape(1, D_MODEL), w_cat=w_cat, w_t=w_t,
        cw=ssm_conv_w[l], cbias=ssm_conv_b[l].reshape(1, SSD_CONV_DIM),
        dtb=pad_heads(ssm_dt_bias[l]), alog=pad_heads(ssm_a_log[l]),
        dsk=jnp.repeat(ssm_d[l].astype(F32), SSD_HEADDIM).reshape(1, SSD_WIDTH),
        ssm_nw=ssm_norm_w[l].reshape(1, SSD_WIDTH),
        scal=jnp.concatenate([lam.reshape(1), far_bias * LOG2E]).astype(F32), lam_init=lam_init,
        subw=attn_subln_w[l].reshape(DA_DV, 1), rel_bias=rel_bias,
        wo1=w_out[l][:SSD_WIDTH].astype(BF16), wo2=w_out[l][SSD_WIDTH:].astype(BF16),
        nfw=norm_ffn_w[l].reshape(1, D_MODEL),
        wu=w_up[l].astype(BF16), ffn_cw=ffn_conv_w[l], ffn_cb=ffn_conv_b[l].reshape(1, 2 * D_FF),
        wd=w_down[l].astype(BF16), fw=final_norm_w.reshape(1, D_MODEL),
    )


def _state_to_kernel(h):
    b = h.shape[0]
    h = h.reshape(b, SSD_GROUPS, SSD_HPG, SSD_HEADDIM, SSD_STATE)
    return jnp.transpose(h, (0, 1, 4, 2, 3)).reshape(b, SSD_GROUPS, SSD_STATE, GROUP_W)


def _state_from_kernel(h):
    b = h.shape[0]
    h = h.reshape(b, SSD_GROUPS, SSD_STATE, SSD_HPG, SSD_HEADDIM)
    return jnp.transpose(h, (0, 1, 3, 4, 2)).reshape(b, SSD_HEADS, SSD_HEADDIM, SSD_STATE)


def _run_group(x, mod, past_k, past_v, ssm_h0, ssm_conv_hist, ffn_conv_hist, p, *, tm, ssd_rows, bq, bk):
    b, t, d = x.shape
    past = 0 if past_k is None else past_k.shape[1]
    chunk = min(CHUNK, t)
    tp = max(t, SUPER)
    if tp != t:
        x = jnp.pad(x, ((0, 0), (0, tp - t), (0, 0)))
        tm = ssd_rows = bq = tp
    mod3 = mod.reshape(b, 6, d)

    assert tm == bq
    zs, xc, dt, k, v, kb, qt, vt, conv_new = _inproj_call(
        x, mod3, p["norm_mix_w"], p["w_cat"], p["w_t"], ssm_conv_hist.astype(F32), p["cw"], p["cbias"],
        tm=tm, real=min(t, tm))

    y_ssd, h_t = _ssd_call(zs, xc, dt, _state_to_kernel(ssm_h0.astype(F32)),
                           p["dtb"], p["alog"], p["dsk"], p["ssm_nw"],
                           chunk=chunk, rows=ssd_rows, real=min(t, ssd_rows))

    if past == 0:
        assert bq == bk and t % bq == 0
        bias = _bias_tiles(p["rel_bias"], bq, (0, bq), bq, bk, 2 * bq)
        y_att = _attn_call(p["scal"], qt, kb, vt, bias, p["subw"], bq=bq, bk=bk, noff=0,
                           out_scale=1.0 - p["lam_init"])
    else:
        assert past % bk == 0 and past >= bk and bq == tp <= bk
        bias = _bias_tiles(p["rel_bias"], past, (past - bk, past), bq, bk, past + t)
        y_att = _attn_cached_call(p["scal"], qt, past_k, past_v, kb, vt, bias[:, 0], bias[:, 1, :bq], p["subw"],
                                  bk=bk, out_scale=1.0 - p["lam_init"])

    y, ffn_new = _ffn_call(x, y_ssd, y_att, mod3, p["wo1"], p["wo2"], p["nfw"], p["wu"], p["ffn_cw"], p["ffn_cb"],
                           p["wd"], p["fw"], ffn_conv_hist.astype(F32), tm=tm, real=min(t, tm))
    return (y[:, :t], k[:, :t], v[:, :t],
            _state_from_kernel(h_t), conv_new, ffn_new)


def kernel(x_prompt, x_sample, c_prompt, c_sample, cache_k, cache_v, state_ssm, state_ssm_conv, state_ffn_conv, w_ada, b_ada, norm_mix_w, w_in, ssm_conv_w, ssm_conv_b, ssm_dt_bias, ssm_a_log, ssm_d, ssm_norm_w, lambda_q1, lambda_k1, lambda_q2, lambda_k2, attn_subln_w, rel_bias, w_out, norm_ffn_w, w_up, ffn_conv_w, ffn_conv_b, w_down, final_norm_w):
    bp, bs = x_prompt.shape[0], x_sample.shape[0]
    dt = x_prompt.dtype
    p = _pack_params(norm_mix_w, w_in, ssm_conv_w, ssm_conv_b, ssm_dt_bias, ssm_a_log, ssm_d, ssm_norm_w,
                     lambda_q1, lambda_k1, lambda_q2, lambda_k2, attn_subln_w, rel_bias, w_out,
                     norm_ffn_w, w_up, ffn_conv_w, ffn_conv_b, w_down, final_norm_w, 0)
    c_all = jnp.concatenate([c_prompt, c_sample], axis=0)
    npad = -c_all.shape[0] % SUBLANES
    c_all = jnp.pad(c_all, ((0, npad), (0, 0)))
    mod = _mod_call(c_all, w_ada[0], b_ada[0].reshape(1, -1))

    zeros = lambda *s: jnp.zeros(s, dt)
    out_p = _run_group(x_prompt, mod[:bp], None, None,
                       zeros(bp, SSD_HEADS, SSD_HEADDIM, SSD_STATE), zeros(bp, SSD_CONV - 1, SSD_CONV_DIM),
                       zeros(bp, FFN_CONV - 1, 2 * D_FF), p, tm=512, ssd_rows=256, bq=512, bk=512)
    out_s = _run_group(x_sample, mod[bp:bp + bs], cache_k[0], cache_v[0], state_ssm[0], state_ssm_conv[0],
                       state_ffn_conv[0], p, tm=SUPER, ssd_rows=SUPER, bq=SUPER, bk=512)
    y_p, k_p, v_p, h_p, c_p, f_p = out_p
    y_s, k_s, v_s, h_s, c_s, f_s = out_s
    return (y_p, y_s, k_p[None], v_p[None], h_p[None], c_p[None], f_p[None],
            k_s[None], v_s[None], h_s[None], c_s[None], f_s[None])
```

```python
import functools
import math

import numpy as np
import jax
import jax.numpy as jnp
from jax import lax
from jax.experimental import pallas as pl
from jax.experimental.pallas import tpu as pltpu

F32 = jnp.float32
BF16 = jnp.bfloat16
HIGHEST = lax.Precision.HIGHEST

D_MODEL = 1024
CHUNK = 64
SSD_WIDTH = 512
SSD_HEADDIM = 64
SSD_HEADS = 8
SSD_GROUPS = 2
SSD_HPG = 4
SSD_STATE = 128
SSD_CONV = 4
SSD_CONV_DIM = SSD_WIDTH + 2 * SSD_GROUPS * SSD_STATE
GROUP_W = SSD_HPG * SSD_HEADDIM
DA_WIDTH = 512
DA_DK = 64
DA_DV = 128
DA_HEADS = 4
REL_BUCKETS = 32
REL_MAX_DIST = 128
D_FF = 2816
FFN_CONV = 3
EPS = 1e-6
IN_SPLITS = (512, 1536, 1544, 2056, 2568)
LANES = 128
SUBLANES = 8
SUPER = 128
ATT_KT = 64
FFN_CN = 256
FFN_NC = D_FF // FFN_CN
NEG = -1e30
VMEM_LIMIT = 56 * 1024 * 1024

PZ, PX, PDT, PK, PV, PEND = 0, 512, 1536, 1664, 2176, 2688
LOG2E = math.log2(math.e)


def _silu(x):
    return x / (1.0 + jnp.exp(-x))


def _softplus(x):
    return jnp.maximum(x, 0.0) + jnp.log1p(jnp.exp(-jnp.abs(x)))


def _split3(x):
    hi = x.astype(BF16)
    r1 = x - hi.astype(F32)
    mid = r1.astype(BF16)
    lo = (r1 - mid.astype(F32)).astype(BF16)
    return hi, mid, lo


def _rms(x, w):
    return x * lax.rsqrt(jnp.mean(x * x, axis=-1, keepdims=True) + EPS) * w


def _const_spec(shape):
    nd = len(shape)
    return pl.BlockSpec(shape, lambda *_: (0,) * nd)


def _mod_kernel(c_ref, w_ref, b_ref, o_ref):
    a = _silu(c_ref[...]).astype(BF16)
    o_ref[...] = jnp.dot(a, w_ref[...].astype(BF16), preferred_element_type=F32) + b_ref[...]


def _mod_call(c, w_ada, b_ada):
    n, d = c.shape
    nout = w_ada.shape[1]
    tn = 1024
    return pl.pallas_call(
        _mod_kernel,
        grid=(nout // tn,),
        in_specs=[pl.BlockSpec((n, d), lambda j: (0, 0)),
                  pl.BlockSpec((d, tn), lambda j: (0, j)),
                  pl.BlockSpec((1, tn), lambda j: (0, j))],
        out_specs=pl.BlockSpec((n, tn), lambda j: (0, j)),
        out_shape=jax.ShapeDtypeStruct((n, nout), F32),
        name="mod",
    )(c, w_ada, b_ada)


def _inproj_kernel(x_ref, mod_ref, nw_ref, w_ref, wt_ref, hist_ref, cw_ref, cbias_ref,
                   zs_ref, xc_ref, dt_ref, k_ref, v_ref, kb_ref, qt_ref, vt_ref, cout_ref, cbuf, hb_scr, zbuf,
                   *, tm, real):
    t = pl.program_id(1)
    nconv = SSD_CONV - 1

    @pl.when(t == 0)
    def _init():
        cbuf[0:SUBLANES, :] = jnp.zeros((SUBLANES, SSD_CONV_DIM), F32)
        cbuf[SUBLANES - nconv:SUBLANES, :] = hist_ref[...]

    h = _rms(x_ref[...], nw_ref[...]) * (1.0 + mod_ref[1:2, :]) + mod_ref[0:1, :]
    hb_scr[...] = h.astype(BF16)

    def proj(a, b):
        return jnp.dot(hb_scr[...], w_ref[:, a:b], preferred_element_type=F32)

    def proj_t(a, b):
        return lax.dot_general(wt_ref[a:b, :], hb_scr[...], (((1,), (1,)), ((), ())), preferred_element_type=F32)

    cbuf[SUBLANES:SUBLANES + tm, :] = proj(PX, PDT)
    zbuf[...] = proj(PZ, PX)
    dt_ref[...] = proj(PDT, PK)
    k = proj(PK, PV)
    v = proj(PV, PEND)
    for hd in range(DA_HEADS):
        k_ref[:, hd, :] = k[:, hd * DA_DV:(hd + 1) * DA_DV]
        v_ref[:, hd, :] = v[:, hd * DA_DV:(hd + 1) * DA_DV]
    kb_ref[...] = k.astype(BF16)
    qt_ref[...] = proj_t(0, DA_WIDTH).astype(BF16)
    vt_ref[...] = proj_t(DA_WIDTH, 2 * DA_WIDTH).astype(BF16)
    conv = cbias_ref[...]
    for j in range(SSD_CONV):
        off = SUBLANES - nconv + j
        conv = conv + cw_ref[j:j + 1, :] * cbuf[off:off + tm, :]
    tail = cbuf[SUBLANES - nconv + real:SUBLANES + real, :]
    cout_ref[...] = tail
    cbuf[SUBLANES - nconv:SUBLANES, :] = tail
    xc_ref[...] = _silu(conv).astype(BF16)
    zs_ref[...] = _silu(zbuf[...]).astype(BF16)


def _inproj_call(x, mod3, norm_w, w_cat, w_t, hist, cw, cbias, *, tm, real):
    b, t, d = x.shape
    nt = t // tm

    def row(width):
        return pl.BlockSpec((None, tm, width), lambda i, j: (i, j, 0))

    def out(width, dtype):
        return jax.ShapeDtypeStruct((b, t, width), dtype)

    tspec = pl.BlockSpec((None, None, DA_WIDTH, tm), lambda i, j: (i, j, 0, 0))
    tshape = jax.ShapeDtypeStruct((b, nt, DA_WIDTH, tm), BF16)
    hist_spec = pl.BlockSpec((None, SSD_CONV - 1, SSD_CONV_DIM), lambda i, j: (i, 0, 0))
    hspec = pl.BlockSpec((None, tm, DA_HEADS, DA_DV), lambda i, j: (i, j, 0, 0))
    hshape = jax.ShapeDtypeStruct((b, t, DA_HEADS, DA_DV), F32)
    return pl.pallas_call(
        functools.partial(_inproj_kernel, tm=tm, real=real),
        grid=(b, nt),
        in_specs=[row(d),
                  pl.BlockSpec((None, 6, d), lambda i, j: (i, 0, 0)),
                  _const_spec((1, d)),
                  _const_spec((d, PEND)),
                  _const_spec((2 * DA_WIDTH, d)),
                  hist_spec, _const_spec((SSD_CONV, SSD_CONV_DIM)), _const_spec((1, SSD_CONV_DIM))],
        out_specs=[row(512), row(1024), row(LANES), hspec, hspec, row(512), tspec, tspec, hist_spec],
        out_shape=[out(512, BF16), out(1024, BF16), out(LANES, F32),
                   hshape, hshape, out(512, BF16), tshape, tshape,
                   jax.ShapeDtypeStruct((b, SSD_CONV - 1, SSD_CONV_DIM), F32)],
        scratch_shapes=[pltpu.VMEM((tm + SUBLANES, SSD_CONV_DIM), F32), pltpu.VMEM((tm, d), BF16),
                        pltpu.VMEM((tm, SSD_WIDTH), F32)],
        compiler_params=pltpu.CompilerParams(dimension_semantics=("parallel", "arbitrary"),
                                             vmem_limit_bytes=VMEM_LIMIT),
        name="inproj",
    )(x, mod3, norm_w, w_cat, w_t, hist, cw, cbias)


def _ssd_kernel(zs_ref, xc_ref, dt_ref, h0_ref, dtb_ref, alog_ref, dsk_ref, nw_ref, tri_ref, e_ref,
                y_ref, hout_ref, h_scr, ybuf, *, chunk, rows, real):
    t = pl.program_id(1)

    @pl.when(t == 0)
    def _init():
        for g in range(SSD_GROUPS):
            h_scr[g] = h0_ref[g].T

    xs = xc_ref[:, 0:SSD_WIDTH].astype(F32)
    dtv = _softplus(dt_ref[...] + dtb_ref[...])
    da = dtv * (-jnp.exp(alog_ref[...]))
    acs = jnp.dot(tri_ref[...], jnp.concatenate(_split3(da), axis=0), preferred_element_type=F32)
    dt_x = jnp.dot(jnp.concatenate(_split3(dtv), axis=1), e_ref[...], preferred_element_type=F32)
    acs_x = jnp.dot(jnp.concatenate(_split3(acs), axis=1), e_ref[...], preferred_element_type=F32)
    eacs_x = jnp.exp(acs_x)
    xd = xs * dt_x

    li = lax.broadcasted_iota(jnp.int32, (SUPER, SUPER), 0)
    si = lax.broadcasted_iota(jnp.int32, (SUPER, SUPER), 1)
    cshift = chunk.bit_length() - 1
    mask2 = ((li >> cshift) == (si >> cshift)) & (si <= li)
    lane_g = lax.broadcasted_iota(jnp.int32, (SUPER, GROUP_W), 1) >> (SSD_HEADDIM.bit_length() - 1)

    for sb in range(rows // SUPER):
        o = sb * SUPER
        nreal = (min(real, o + SUPER) - o) // chunk
        acs2 = acs[o:o + SUPER, :]
        acs_t = acs2.T
        for g in range(SSD_GROUPS):
            gs = slice(g * GROUP_W, (g + 1) * GROUP_W)
            bcol = SSD_WIDTH + g * SSD_STATE
            ccol = SSD_WIDTH + (SSD_GROUPS + g) * SSD_STATE
            bmb = xc_ref[o:o + SUPER, bcol:bcol + SSD_STATE]
            cmb = xc_ref[o:o + SUPER, ccol:ccol + SSD_STATE]
            cb2 = lax.dot_general(cmb, bmb, (((1,), (1,)), ((), ())), preferred_element_type=F32)
            bm_t = bmb.astype(F32).T.astype(BF16)
            ms = []
            for rr in range(SSD_HPG):
                r = g * SSD_HPG + rr
                seg = acs2[:, r:r + 1] - acs_t[r:r + 1, :]
                dec = jnp.where(mask2, jnp.exp(jnp.where(mask2, seg, 0.0)), 0.0)
                ms.append((cb2 * dec).astype(BF16))
            full = jnp.dot(jnp.concatenate(ms, axis=0), xd[o:o + SUPER, gs].astype(BF16),
                           preferred_element_type=F32)
            ydiag = full[0:SUPER]
            for rr in range(1, SSD_HPG):
                ydiag = jnp.where(lane_g == rr, full[rr * SUPER:(rr + 1) * SUPER], ydiag)
            ybuf[o:o + SUPER, gs] = ydiag
            for j in range(nreal):
                a0, a1 = o + j * chunk, o + (j + 1) * chunk
                h_t = h_scr[g]
                yoff = jnp.dot(cmb[j * chunk:(j + 1) * chunk, :], h_t.astype(BF16), preferred_element_type=F32)
                ybuf[a0:a1, gs] = ybuf[a0:a1, gs] + yoff * eacs_x[a0:a1, gs]
                dte = jnp.exp(acs_x[a1 - 1:a1, gs] - acs_x[a0:a1, gs])
                xw = (xd[a0:a1, gs] * dte).astype(BF16)
                pieces = []
                if j > 0:
                    pieces.append(jnp.zeros((j * chunk, GROUP_W), BF16))
                pieces.append(xw)
                if (j + 1) * chunk < SUPER:
                    pieces.append(jnp.zeros((SUPER - (j + 1) * chunk, GROUP_W), BF16))
                xw2 = jnp.concatenate(pieces, axis=0) if len(pieces) > 1 else xw
                st = jnp.dot(bm_t, xw2, preferred_element_type=F32)
                h_scr[g] = h_t * eacs_x[a1 - 1:a1, gs] + st

    y = (ybuf[...] + dsk_ref[...] * xs) * zs_ref[...].astype(F32)
    for g in range(SSD_GROUPS):
        gs = slice(g * GROUP_W, (g + 1) * GROUP_W)
        y_ref[:, gs] = _rms(y[:, gs], nw_ref[:, gs]).astype(BF16)

    @pl.when(t == pl.num_programs(1) - 1)
    def _fin():
        for g in range(SSD_GROUPS):
            hout_ref[g] = h_scr[g].T


def _ssd_call(zs, xc, dt, h0_t, dtb, alog, dsk, nw, *, chunk, rows, real):
    b, t, _ = zs.shape
    ii = np.arange(rows)
    tri = ((ii[:, None] // chunk == ii[None, :] // chunk) & (ii[None, :] <= ii[:, None])).astype(np.float32)
    e = np.zeros((LANES, SSD_WIDTH), np.float32)
    for r in range(SSD_HEADS):
        e[r, r * SSD_HEADDIM:(r + 1) * SSD_HEADDIM] = 1.0

    def row(width):
        return pl.BlockSpec((None, rows, width), lambda i, j: (i, j, 0))

    kern = functools.partial(_ssd_kernel, chunk=chunk, rows=rows, real=real)
    return pl.pallas_call(
        kern,
        grid=(b, t // rows),
        in_specs=[row(SSD_WIDTH), row(SSD_CONV_DIM), row(LANES),
                  pl.BlockSpec((None, SSD_GROUPS, GROUP_W, SSD_STATE), lambda i, j: (i, 0, 0, 0)),
                  _const_spec((1, LANES)), _const_spec((1, LANES)),
                  _const_spec((1, SSD_WIDTH)), _const_spec((1, SSD_WIDTH)),
                  _const_spec((rows, 3 * rows)), _const_spec((3 * LANES, SSD_WIDTH))],
        out_specs=[row(SSD_WIDTH),
                   pl.BlockSpec((None, SSD_GROUPS, GROUP_W, SSD_STATE), lambda i, j: (i, 0, 0, 0))],
        out_shape=[jax.ShapeDtypeStruct((b, t, SSD_WIDTH), BF16),
                   jax.ShapeDtypeStruct((b, SSD_GROUPS, GROUP_W, SSD_STATE), F32)],
        scratch_shapes=[pltpu.VMEM((SSD_GROUPS, SSD_STATE, GROUP_W), F32),
                        pltpu.VMEM((rows, SSD_WIDTH), F32)],
        compiler_params=pltpu.CompilerParams(dimension_semantics=("parallel", "arbitrary"),
                                             vmem_limit_bytes=VMEM_LIMIT),
        name="ssd",
    )(zs, xc, dt, h0_t, dtb, alog, dsk, nw,
      jnp.asarray(np.tile(tri, (1, 3)), BF16), jnp.asarray(np.tile(e, (3, 1)), BF16))


def _attn_kernel(scal_ref, qt_ref, k_ref, vt_ref, bias_ref, subw_ref, o_ref,
                 m_scr, l_scr, acc_scr, sa_scr, sb_scr, sc_scr, p_scr, qz_scr,
                 *, bq, bk, noff, out_scale):
    h = pl.program_id(1)
    qi = pl.program_id(2)
    kn0 = qi + (noff - 1)
    lam = scal_ref[0]
    cfar = scal_ref[1 + h]

    zero = jnp.zeros((DA_DK, bq), BF16)
    qz_scr[0, 0:DA_DK, :] = qt_ref[0:DA_DK, :]
    qz_scr[0, DA_DK:DA_DV, :] = zero
    qz_scr[1, 0:DA_DK, :] = zero
    qz_scr[1, DA_DK:DA_DV, :] = qt_ref[DA_DK:DA_DV, :]

    m_scr[...] = jnp.full(m_scr.shape, NEG, F32)
    l_scr[...] = jnp.zeros(l_scr.shape, F32)
    acc_scr[...] = jnp.zeros(acc_scr.shape, F32)

    nsub = bk // ATT_KT

    def fold(x):
        return x.reshape(ATT_KT // SUBLANES, SUBLANES, bq)

    def scores(s_buf, first, count):
        for mm in range(2):
            for e in range(count):
                start = pl.multiple_of((first + e) * bk, bk)
                s_buf[mm, e] = jnp.dot(k_ref[pl.ds(start, bk), :], qz_scr[mm], preferred_element_type=F32)

    def softmax_pv(s_buf, first, entries):
        n = len(entries)
        for mm in range(2):
            cand = None
            for e, (near, shift) in enumerate(entries):
                mx = None
                for t in range(nsub):
                    rows = slice(t * ATT_KT, (t + 1) * ATT_KT)
                    s = s_buf[mm, e, rows, :]
                    if near is not None:
                        s = s + bias_ref[near, rows, :]
                    pm = jnp.max(fold(s), axis=0)
                    mx = pm if mx is None else jnp.maximum(mx, pm)
                mx = jnp.max(mx, axis=0, keepdims=True) + shift
                cand = mx if cand is None else jnp.maximum(cand, mx)
            m_old = m_scr[mm]
            m_new = jnp.maximum(m_old, cand)
            ls = None
            for e, (near, shift) in enumerate(entries):
                off = m_new - shift
                for t in range(nsub):
                    rows = slice(t * ATT_KT, (t + 1) * ATT_KT)
                    s = s_buf[mm, e, rows, :]
                    if near is not None:
                        s = s + bias_ref[near, rows, :]
                    p = jnp.exp2(s - off)
                    p_scr[mm, e * bk + t * ATT_KT:e * bk + (t + 1) * ATT_KT, :] = p.astype(BF16)
                    ps = jnp.sum(fold(p), axis=0)
                    ls = ps if ls is None else ls + ps
            vts = [vt_ref[first + e] for e in range(n)]
            vt = jnp.concatenate(vts, axis=1) if n > 1 else vts[0]
            pv = jnp.dot(vt, p_scr[mm, 0:n * bk, :], preferred_element_type=F32)
            alpha = jnp.exp2(m_old - m_new)
            l_scr[mm] = alpha * l_scr[mm] + jnp.sum(ls, axis=0, keepdims=True)
            acc_scr[mm] = alpha * acc_scr[mm] + pv
            m_scr[mm] = m_new

    nfar = jnp.maximum(kn0, 0)
    odd = nfar % 2
    far = (None, cfar)

    near_pair = [(0, 0.0), (1, 0.0)]

    @pl.when(kn0 < 0)
    def _only_first():
        scores(sc_scr, 0, 1)
        softmax_pv(sc_scr, 0, [(1, 0.0)])

    @pl.when(kn0 >= 0)
    def _groups():
        @pl.when(odd == 1)
        def _single():
            scores(sc_scr, 0, 1)
            scores(sa_scr, 1, 2)
            softmax_pv(sc_scr, 0, [far])

        @pl.when(odd == 0)
        def _first_pair():
            scores(sa_scr, 0, 2)

        def far_pair(s_cur, s_next, cur):
            scores(s_next, cur + 2, 2)
            softmax_pv(s_cur, cur, [far, far])

        def far_body(j, carry):
            cur = odd + 4 * j
            far_pair(sa_scr, sb_scr, cur)
            far_pair(sb_scr, sa_scr, cur + 2)
            return carry

        npairs = nfar // 2
        lax.fori_loop(0, npairs // 2, far_body, 0)

        @pl.when(npairs % 2 == 1)
        def _tail_b():
            far_pair(sa_scr, sb_scr, kn0 - 2)
            softmax_pv(sb_scr, kn0, near_pair)

        @pl.when(npairs % 2 == 0)
        def _tail_a():
            softmax_pv(sa_scr, kn0, near_pair)

    o = acc_scr[0] * (1.0 / l_scr[0]) - acc_scr[1] * (lam / l_scr[1])
    o = o * lax.rsqrt(jnp.mean(o * o, axis=0, keepdims=True) + EPS) * (subw_ref[...] * out_scale)
    o_ref[...] = o.T.astype(BF16)


def _attn_call(scal, qt, kb, vt, bias, subw, *, bq, bk, noff, out_scale):
    b, nq = qt.shape[:2]
    tk = kb.shape[1]
    nkb = vt.shape[1]
    kern = functools.partial(_attn_kernel, bq=bq, bk=bk, noff=noff, out_scale=out_scale)
    return pl.pallas_call(
        kern,
        grid=(b, DA_HEADS, nq),
        in_specs=[pl.BlockSpec(memory_space=pltpu.SMEM),
                  pl.BlockSpec((None, None, DA_DV, bq), lambda i, h, j: (i, j, h, 0)),
                  pl.BlockSpec((None, tk, DA_DV), lambda i, h, j: (i, 0, h)),
                  pl.BlockSpec((None, nkb, DA_DV, bk), lambda i, h, j: (i, 0, h, 0)),
                  pl.BlockSpec((None, 2, bk, bq), lambda i, h, j: (h, 0, 0, 0)),
                  pl.BlockSpec((DA_DV, 1), lambda i, h, j: (0, 0))],
        out_specs=pl.BlockSpec((None, bq, DA_DV), lambda i, h, j: (i, j, h)),
        out_shape=jax.ShapeDtypeStruct((b, nq * bq, DA_WIDTH), BF16),
        scratch_shapes=[pltpu.VMEM((2, 1, bq), F32), pltpu.VMEM((2, 1, bq), F32),
                        pltpu.VMEM((2, DA_DV, bq), F32),
                        pltpu.VMEM((2, 2, bk, bq), F32), pltpu.VMEM((2, 2, bk, bq), F32),
                        pltpu.VMEM((2, 1, bk, bq), F32), pltpu.VMEM((2, 2 * bk, bq), BF16),
                        pltpu.VMEM((2, DA_DV, bq), BF16)],
        compiler_params=pltpu.CompilerParams(dimension_semantics=("parallel", "parallel", "arbitrary"),
                                             vmem_limit_bytes=VMEM_LIMIT),
        name="attn",
    )(scal, qt, kb, vt, bias, subw)


def _attn_cached_kernel(scal_ref, qt_ref, kc_ref, vc_ref, kn_ref, vtn_ref, bias0_ref, bias1_ref, subw_ref, o_ref,
                        *, bq, tq, bk, out_scale):
    lam = scal_ref[0]
    past = kc_ref.shape[0] // DA_HEADS
    lane = lax.broadcasted_iota(jnp.int32, (tq, DA_DV), 1)
    nt = (((1,), (1,)), ((), ()))

    for h in range(DA_HEADS):
        hs = slice(h * DA_DV, (h + 1) * DA_DV)
        cfar = scal_ref[1 + h]
        qn = qt_ref[hs, :].astype(F32).T[0:tq, :]
        q2 = jnp.concatenate([jnp.where(lane < DA_DK, qn, 0.0), jnp.where(lane >= DA_DK, qn, 0.0)],
                             axis=0).astype(BF16)
        head_rows = pl.ds(h, past, stride=DA_HEADS)
        s_c = lax.dot_general(q2, kc_ref[head_rows, :].astype(BF16), nt, preferred_element_type=F32)
        s_n = lax.dot_general(q2, kn_ref[:, hs], nt, preferred_element_type=F32)
        b0 = bias0_ref[h]
        b1 = bias1_ref[h]
        s = jnp.concatenate([s_c[:, 0:past - bk] + cfar,
                             s_c[:, past - bk:past] + jnp.concatenate([b0, b0], axis=0),
                             s_n + jnp.concatenate([b1, b1], axis=0)], axis=1)
        p = jnp.exp2(s - jnp.max(s, axis=1, keepdims=True))
        inv = 1.0 / jnp.sum(p, axis=1, keepdims=True)
        pb = p.astype(BF16)
        acc = (jnp.dot(pb[:, 0:past], vc_ref[head_rows, :].astype(BF16), preferred_element_type=F32)
               + jnp.dot(pb[:, past:past + bq], vtn_ref[hs, :].astype(F32).T.astype(BF16),
                         preferred_element_type=F32))
        o = acc[0:tq, :] * inv[0:tq] - acc[tq:2 * tq, :] * (lam * inv[tq:2 * tq])
        o_ref[0:tq, hs] = (_rms(o, subw_ref[...]) * out_scale).astype(BF16)
        o_ref[tq:bq, hs] = jnp.zeros((bq - tq, DA_DV), BF16)


def _attn_cached_call(scal, qt, cache_k, cache_v, kb, vt, bias0, bias1, subw, *, tq, bk, out_scale):
    b, past = cache_k.shape[:2]
    bq = qt.shape[-1]
    rows = past * DA_HEADS
    kern = functools.partial(_attn_cached_kernel, bq=bq, tq=tq, bk=bk, out_scale=out_scale)
    cache_spec = pl.BlockSpec((None, rows, DA_DV), lambda i: (i, 0, 0))
    return pl.pallas_call(
        kern,
        grid=(b,),
        in_specs=[pl.BlockSpec(memory_space=pltpu.SMEM),
                  pl.BlockSpec((None, None, DA_WIDTH, bq), lambda i: (i, 0, 0, 0)),
                  cache_spec, cache_spec,
                  pl.BlockSpec((None, bq, DA_WIDTH), lambda i: (i, 0, 0)),
                  pl.BlockSpec((None, None, DA_WIDTH, bq), lambda i: (i, 0, 0, 0)),
                  _const_spec((DA_HEADS, tq, bk)), _const_spec((DA_HEADS, tq, bq)),
                  _const_spec((1, DA_DV))],
        out_specs=pl.BlockSpec((None, bq, DA_WIDTH), lambda i: (i, 0, 0)),
        out_shape=jax.ShapeDtypeStruct((b, bq, DA_WIDTH), BF16),
        compiler_params=pltpu.CompilerParams(dimension_semantics=("parallel",), vmem_limit_bytes=VMEM_LIMIT),
        name="attn_cached",
    )(scal, qt, cache_k.reshape(b, rows, DA_DV), cache_v.reshape(b, rows, DA_DV), kb, vt, bias0, bias1, subw)


def _rel_bucket(rel):
    nb = REL_BUCKETS // 2
    max_exact = nb // 2
    n = jnp.abs(rel)
    nf = jnp.maximum(n, 1).astype(jnp.float32)
    large = max_exact + (jnp.log(nf / max_exact) / math.log(REL_MAX_DIST / max_exact)
                         * (nb - max_exact)).astype(jnp.int32)
    large = jnp.minimum(large, nb - 1)
    return jnp.where(rel > 0, nb, 0) + jnp.where(n < max_exact, n, large)


def _bias_tiles(rel_bias, qpos0, kpos0, bq, bk, tk_real):
    qpos = qpos0 + np.arange(bq)
    span = bq + bk - 1
    tiles = []
    for d in range(2):
        kpos = kpos0[d] + np.arange(bk)
        offs = (kpos0[d] - qpos0) + np.arange(-(bq - 1), bk)
        table = rel_bias[_rel_bucket(jnp.asarray(offs, jnp.int32))].astype(F32).T * LOG2E
        rev = jnp.pad(table[:, ::-1], ((0, 0), (0, 1)))
        skew = jnp.tile(rev, (1, bk))[:, :bk * span].reshape(DA_HEADS, bk, span)
        toep = skew[:, :, bk - 1:bk - 1 + bq]
        vis = (kpos[:, None] // CHUNK <= qpos[None, :] // CHUNK) & (kpos[:, None] < tk_real)
        tiles.append(jnp.where(jnp.asarray(vis)[None], toep, NEG))
    return jnp.stack(tiles, axis=1)


def _ffn_kernel(x_ref, ys_ref, ya_ref, mod_ref, wo1_ref, wo2_ref, nfw_ref, wu_ref, cw_ref, cb_ref, wd_ref, fw_ref,
                hist_ref, y_ref, tail_ref, tail_scr, buf_a, buf_b, x1_scr, h2_scr, act_scr, *, tm, real):
    t = pl.program_id(1)
    nh = FFN_CONV - 1
    lo = SUBLANES - nh

    @pl.when(t == 0)
    def _init():
        tail_scr[lo:SUBLANES, :] = hist_ref[...]

    mix = (jnp.dot(ys_ref[...], wo1_ref[...], preferred_element_type=F32)
           + jnp.dot(ya_ref[...], wo2_ref[...], preferred_element_type=F32))
    x1 = x_ref[...] + mod_ref[2:3, :] * mix
    x1_scr[...] = x1
    h2 = _rms(x1, nfw_ref[...]) * (1.0 + mod_ref[4:5, :]) + mod_ref[3:4, :]
    h2_scr[...] = h2.astype(BF16)

    def cols(j, half):
        return slice(half * D_FF + j * FFN_CN, half * D_FF + (j + 1) * FFN_CN)

    def up(j, buf):
        for half in range(2):
            buf[half, SUBLANES:SUBLANES + tm, :] = jnp.dot(h2_scr[...], wu_ref[:, cols(j, half)],
                                                            preferred_element_type=F32)

    def conv(j, buf, half):
        cs = cols(j, half)
        buf[half, lo:SUBLANES, :] = tail_scr[lo:SUBLANES, cs]
        c = cb_ref[:, cs]
        for i in range(FFN_CONV):
            c = c + cw_ref[i:i + 1, cs] * buf[half, lo + i:lo + i + tm, :]
        tail_scr[lo:SUBLANES, cs] = buf[half, lo + real:SUBLANES + real, :]
        return c

    bufs = (buf_a, buf_b)
    up(0, bufs[0])
    for j in range(FFN_NC):
        if j + 1 < FFN_NC:
            up(j + 1, bufs[(j + 1) % 2])
        cv = conv(j, bufs[j % 2], 0)
        cg = conv(j, bufs[j % 2], 1)
        act_scr[:, j * FFN_CN:(j + 1) * FFN_CN] = (_silu(cg) * cv).astype(BF16)

    f = jnp.dot(act_scr[...], wd_ref[...], preferred_element_type=F32)
    x2 = x1_scr[...] + mod_ref[5:6, :] * f
    y_ref[...] = _rms(x2, fw_ref[...])
    tail_ref[...] = tail_scr[lo:SUBLANES, :]


def _ffn_call(x, ys, ya, mod3, wo1, wo2, nfw, wu, cw, cb, wd, fw, hist, *, tm, real):
    b, t, d = x.shape
    nh = FFN_CONV - 1

    def row(width):
        return pl.BlockSpec((None, tm, width), lambda i, j: (i, j, 0))

    hist_spec = pl.BlockSpec((None, nh, 2 * D_FF), lambda i, j: (i, 0, 0))
    kern = functools.partial(_ffn_kernel, tm=tm, real=real)
    return pl.pallas_call(
        kern,
        grid=(b, t // tm),
        in_specs=[row(d), row(SSD_WIDTH), row(DA_WIDTH),
                  pl.BlockSpec((None, 6, d), lambda i, j: (i, 0, 0)),
                  _const_spec((SSD_WIDTH, d)), _const_spec((DA_WIDTH, d)), _const_spec((1, d)),
                  _const_spec((d, 2 * D_FF)), _const_spec((FFN_CONV, 2 * D_FF)), _const_spec((1, 2 * D_FF)),
                  _const_spec((D_FF, d)), _const_spec((1, d)),
                  hist_spec],
        out_specs=[row(d), hist_spec],
        out_shape=[jax.ShapeDtypeStruct((b, t, d), F32),
                   jax.ShapeDtypeStruct((b, nh, 2 * D_FF), F32)],
        scratch_shapes=[pltpu.VMEM((SUBLANES, 2 * D_FF), F32),
                        pltpu.VMEM((2, tm + SUBLANES, FFN_CN), F32), pltpu.VMEM((2, tm + SUBLANES, FFN_CN), F32),
                        pltpu.VMEM((tm, d), F32), pltpu.VMEM((tm, d), BF16), pltpu.VMEM((tm, D_FF), BF16)],
        compiler_params=pltpu.CompilerParams(dimension_semantics=("parallel", "arbitrary"),
                                             vmem_limit_bytes=VMEM_LIMIT),
        name="ffn",
    )(x, ys, ya, mod3, wo1, wo2, nfw, wu, cw, cb, wd, fw, hist)


def _pack_params(norm_mix_w, w_in, ssm_conv_w, ssm_conv_b, ssm_dt_bias, ssm_a_log, ssm_d, ssm_norm_w,
                 lambda_q1, lambda_k1, lambda_q2, lambda_k2, attn_subln_w, rel_bias, w_out,
                 norm_ffn_w, w_up, ffn_conv_w, ffn_conv_b, w_down, final_norm_w, layer):
    l = layer
    wz, wx, wdt, wq, wk, wv = jnp.split(w_in[l], IN_SPLITS, axis=-1)
    wdt = jnp.pad(wdt, ((0, 0), (0, LANES - SSD_HEADS)))
    w_cat = jnp.concatenate([wz, wx, wdt, wk, wv], axis=-1).astype(BF16)
    w_t = jnp.concatenate([wq * (DA_DK ** -0.5 * LOG2E), wv], axis=-1).T.astype(BF16)

    def pad_heads(v):
        return jnp.pad(v.astype(F32), (0, LANES - SSD_HEADS)).reshape(1, LANES)

    lam_init = 0.8 - 0.6 * math.exp(-0.3 * l)
    lam = (jnp.exp(jnp.sum(lambda_q1[l].astype(F32) * lambda_k1[l].astype(F32)))
           - jnp.exp(jnp.sum(lambda_q2[l].astype(F32) * lambda_k2[l].astype(F32))) + lam_init)
    far_bias = rel_bias[REL_BUCKETS // 2 - 1].astype(F32)
    return dict(
        norm_mix_w=norm_mix_w[l].reshape(1, D_MODEL), w_cat=w_cat, w_t=w_t,
        cw=ssm_conv_w[l], cbias=ssm_conv_b[l].reshape(1, SSD_CONV_DIM),
        dtb=pad_heads(ssm_dt_bias[l]), alog=pad_heads(ssm_a_log[l]),
        dsk=jnp.repeat(ssm_d[l].astype(F32), SSD_HEADDIM).reshape(1, SSD_WIDTH),
        ssm_nw=ssm_norm_w[l].reshape(1, SSD_WIDTH),
        scal=jnp.concatenate([lam.reshape(1), far_bias * LOG2E]).astype(F32), lam_init=lam_init,
        subw=attn_subln_w[l].reshape(DA_DV, 1), rel_bias=rel_bias,
        wo1=w_out[l][:SSD_WIDTH].astype(BF16), wo2=w_out[l][SSD_WIDTH:].astype(BF16),
        nfw=norm_ffn_w[l].reshape(1, D_MODEL),
        wu=w_up[l].astype(BF16), ffn_cw=ffn_conv_w[l], ffn_cb=ffn_conv_b[l].reshape(1, 2 * D_FF),
        wd=w_down[l].astype(BF16), fw=final_norm_w.reshape(1, D_MODEL),
    )


def _state_to_kernel(h):
    return h.reshape(h.shape[0], SSD_GROUPS, GROUP_W, SSD_STATE)


def _state_from_kernel(h):
    return h.reshape(h.shape[0], SSD_HEADS, SSD_HEADDIM, SSD_STATE)


def _run_group(x, mod, past_k, past_v, ssm_h0, ssm_conv_hist, ffn_conv_hist, p, *, tm, ssd_rows, bq, bk):
    b, t, d = x.shape
    past = 0 if past_k is None else past_k.shape[1]
    chunk = min(CHUNK, t)
    tp = max(t, SUPER)
    if tp != t:
        x = jnp.pad(x, ((0, 0), (0, tp - t), (0, 0)))
        tm = ssd_rows = bq = tp
    mod3 = mod.reshape(b, 6, d)

    assert tm == bq
    zs, xc, dt, k, v, kb, qt, vt, conv_new = _inproj_call(
        x, mod3, p["norm_mix_w"], p["w_cat"], p["w_t"], ssm_conv_hist.astype(F32), p["cw"], p["cbias"],
        tm=tm, real=min(t, tm))

    y_ssd, h_t = _ssd_call(zs, xc, dt, _state_to_kernel(ssm_h0.astype(F32)),
                           p["dtb"], p["alog"], p["dsk"], p["ssm_nw"],
                           chunk=chunk, rows=ssd_rows, real=min(t, ssd_rows))

    if past == 0:
        assert bq == bk and t % bq == 0
        bias = _bias_tiles(p["rel_bias"], bq, (0, bq), bq, bk, 2 * bq)
        y_att = _attn_call(p["scal"], qt, kb, vt, bias, p["subw"], bq=bq, bk=bk, noff=0,
                           out_scale=1.0 - p["lam_init"])
    else:
        assert past % bk == 0 and past >= bk and bq == tp <= bk
        bias = _bias_tiles(p["rel_bias"], past, (past - bk, past), bq, bk, past + t)
        bias0 = jnp.swapaxes(bias[:, 0], 1, 2)[:, :t]
        bias1 = jnp.swapaxes(bias[:, 1, :bq], 1, 2)[:, :t]
        y_att = _attn_cached_call(p["scal"], qt, past_k, past_v, kb, vt, bias0, bias1, p["subw"].reshape(1, DA_DV),
                                  tq=t, bk=bk, out_scale=1.0 - p["lam_init"])

    y, ffn_new = _ffn_call(x, y_ssd, y_att, mod3, p["wo1"], p["wo2"], p["nfw"], p["wu"], p["ffn_cw"], p["ffn_cb"],
                           p["wd"], p["fw"], ffn_conv_hist.astype(F32), tm=tm, real=min(t, tm))
    return (y[:, :t], k[:, :t], v[:, :t],
            _state_from_kernel(h_t), conv_new, ffn_new)


def kernel(x_prompt, x_sample, c_prompt, c_sample, cache_k, cache_v, state_ssm, state_ssm_conv, state_ffn_conv, w_ada, b_ada, norm_mix_w, w_in, ssm_conv_w, ssm_conv_b, ssm_dt_bias, ssm_a_log, ssm_d, ssm_norm_w, lambda_q1, lambda_k1, lambda_q2, lambda_k2, attn_subln_w, rel_bias, w_out, norm_ffn_w, w_up, ffn_conv_w, ffn_conv_b, w_down, final_norm_w):
    bp, bs = x_prompt.shape[0], x_sample.shape[0]
    dt = x_prompt.dtype
    p = _pack_params(norm_mix_w, w_in, ssm_conv_w, ssm_conv_b, ssm_dt_bias, ssm_a_log, ssm_d, ssm_norm_w,
                     lambda_q1, lambda_k1, lambda_q2, lambda_k2, attn_subln_w, rel_bias, w_out,
                     norm_ffn_w, w_up, ffn_conv_w, ffn_conv_b, w_down, final_norm_w, 0)
    c_all = jnp.concatenate([c_prompt, c_sample], axis=0)
    npad = -c_all.shape[0] % SUBLANES
    c_all = jnp.pad(c_all, ((0, npad), (0, 0)))
    mod = _mod_call(c_all, w_ada[0], b_ada[0].reshape(1, -1))

    zeros = lambda *s: jnp.zeros(s, dt)
    out_p = _run_group(x_prompt, mod[:bp], None, None,
                       zeros(bp, SSD_HEADS, SSD_HEADDIM, SSD_STATE), zeros(bp, SSD_CONV - 1, SSD_CONV_DIM),
                       zeros(bp, FFN_CONV - 1, 2 * D_FF), p, tm=512, ssd_rows=256, bq=512, bk=512)
    out_s = _run_group(x_sample, mod[bp:bp + bs], cache_k[0], cache_v[0], state_ssm[0], state_ssm_conv[0],
                       state_ffn_conv[0], p, tm=SUPER, ssd_rows=SUPER, bq=SUPER, bk=512)
    y_p, k_p, v_p, h_p, c_p, f_p = out_p
    y_s, k_s, v_s, h_s, c_s, f_s = out_s
    return (y_p, y_s, k_p[None], v_p[None], h_p[None], c_p[None], f_p[None],
            k_s[None], v_s[None], h_s[None], c_s[None], f_s[None])
```

```python
import functools
import math

import numpy as np
import jax
import jax.numpy as jnp
from jax import lax
from jax.experimental import pallas as pl
from jax.experimental.pallas import tpu as pltpu

F32 = jnp.float32
BF16 = jnp.bfloat16
HIGHEST = lax.Precision.HIGHEST

D_MODEL = 1024
CHUNK = 64
SSD_WIDTH = 512
SSD_HEADDIM = 64
SSD_HEADS = 8
SSD_GROUPS = 2
SSD_HPG = 4
SSD_STATE = 128
SSD_CONV = 4
SSD_CONV_DIM = SSD_WIDTH + 2 * SSD_GROUPS * SSD_STATE
GROUP_W = SSD_HPG * SSD_HEADDIM
DA_WIDTH = 512
DA_DK = 64
DA_DV = 128
DA_HEADS = 4
REL_BUCKETS = 32
REL_MAX_DIST = 128
D_FF = 2816
FFN_CONV = 3
EPS = 1e-6
IN_SPLITS = (512, 1536, 1544, 2056, 2568)
LANES = 128
SUBLANES = 8
SUPER = 128
ATT_KT = 64
FFN_CN = 256
FFN_NC = D_FF // FFN_CN
NEG = -1e30
VMEM_LIMIT = 56 * 1024 * 1024

PZ, PX, PDT, PK, PV, PEND = 0, 512, 1536, 1664, 2176, 2688
LOG2E = math.log2(math.e)


def _silu(x):
    return x / (1.0 + jnp.exp(-x))


def _softplus(x):
    return jnp.maximum(x, 0.0) + jnp.log1p(jnp.exp(-jnp.abs(x)))


def _split3(x):
    hi = x.astype(BF16)
    r1 = x - hi.astype(F32)
    mid = r1.astype(BF16)
    lo = (r1 - mid.astype(F32)).astype(BF16)
    return hi, mid, lo


def _rms(x, w):
    return x * lax.rsqrt(jnp.mean(x * x, axis=-1, keepdims=True) + EPS) * w


def _const_spec(shape):
    nd = len(shape)
    return pl.BlockSpec(shape, lambda *_: (0,) * nd)


def _mod_kernel(c_ref, w_ref, b_ref, o_ref):
    a = _silu(c_ref[...]).astype(BF16)
    o_ref[...] = jnp.dot(a, w_ref[...].astype(BF16), preferred_element_type=F32) + b_ref[...]


def _mod_call(c, w_ada, b_ada):
    n, d = c.shape
    nout = w_ada.shape[1]
    tn = 1024
    return pl.pallas_call(
        _mod_kernel,
        grid=(nout // tn,),
        in_specs=[pl.BlockSpec((n, d), lambda j: (0, 0)),
                  pl.BlockSpec((d, tn), lambda j: (0, j)),
                  pl.BlockSpec((1, tn), lambda j: (0, j))],
        out_specs=pl.BlockSpec((n, tn), lambda j: (0, j)),
        out_shape=jax.ShapeDtypeStruct((n, nout), F32),
        name="mod",
    )(c, w_ada, b_ada)


def _inproj_kernel(x_ref, mod_ref, nw_ref, w_ref, wt_ref, hist_ref, cw_ref, cbias_ref,
                   zs_ref, xc_ref, dt_ref, k_ref, v_ref, kb_ref, qt_ref, vt_ref, cout_ref, cbuf, hb_scr, zbuf,
                   *, tm, real):
    t = pl.program_id(1)
    nconv = SSD_CONV - 1

    @pl.when(t == 0)
    def _init():
        cbuf[0:SUBLANES, :] = jnp.zeros((SUBLANES, SSD_CONV_DIM), F32)
        cbuf[SUBLANES - nconv:SUBLANES, :] = hist_ref[...]

    h = _rms(x_ref[...], nw_ref[...]) * (1.0 + mod_ref[1:2, :]) + mod_ref[0:1, :]
    hb_scr[...] = h.astype(BF16)

    def proj(a, b):
        return jnp.dot(hb_scr[...], w_ref[:, a:b], preferred_element_type=F32)

    def proj_t(a, b):
        return lax.dot_general(wt_ref[a:b, :], hb_scr[...], (((1,), (1,)), ((), ())), preferred_element_type=F32)

    cbuf[SUBLANES:SUBLANES + tm, :] = proj(PX, PDT)
    zbuf[...] = proj(PZ, PX)
    dt_ref[...] = proj(PDT, PK)
    k = proj(PK, PV)
    v = proj(PV, PEND)
    for hd in range(DA_HEADS):
        k_ref[:, hd, :] = k[:, hd * DA_DV:(hd + 1) * DA_DV]
        v_ref[:, hd, :] = v[:, hd * DA_DV:(hd + 1) * DA_DV]
    kb_ref[...] = k.astype(BF16)
    qt_ref[...] = proj_t(0, DA_WIDTH).astype(BF16)
    vt_ref[...] = proj_t(DA_WIDTH, 2 * DA_WIDTH).astype(BF16)
    conv = cbias_ref[...]
    for j in range(SSD_CONV):
        off = SUBLANES - nconv + j
        conv = conv + cw_ref[j:j + 1, :] * cbuf[off:off + tm, :]
    tail = cbuf[SUBLANES - nconv + real:SUBLANES + real, :]
    cout_ref[...] = tail
    cbuf[SUBLANES - nconv:SUBLANES, :] = tail
    xc_ref[...] = _silu(conv).astype(BF16)
    zs_ref[...] = _silu(zbuf[...]).astype(BF16)


def _inproj_call(x, mod3, norm_w, w_cat, w_t, hist, cw, cbias, *, tm, real):
    b, t, d = x.shape
    nt = t // tm

    def row(width):
        return pl.BlockSpec((None, tm, width), lambda i, j: (i, j, 0))

    def out(width, dtype):
        return jax.ShapeDtypeStruct((b, t, width), dtype)

    tspec = pl.BlockSpec((None, None, DA_WIDTH, tm), lambda i, j: (i, j, 0, 0))
    tshape = jax.ShapeDtypeStruct((b, nt, DA_WIDTH, tm), BF16)
    hist_spec = pl.BlockSpec((None, SSD_CONV - 1, SSD_CONV_DIM), lambda i, j: (i, 0, 0))
    hspec = pl.BlockSpec((None, tm, DA_HEADS, DA_DV), lambda i, j: (i, j, 0, 0))
    hshape = jax.ShapeDtypeStruct((b, t, DA_HEADS, DA_DV), F32)
    return pl.pallas_call(
        functools.partial(_inproj_kernel, tm=tm, real=real),
        grid=(b, nt),
        in_specs=[row(d),
                  pl.BlockSpec((None, 6, d), lambda i, j: (i, 0, 0)),
                  _const_spec((1, d)),
                  _const_spec((d, PEND)),
                  _const_spec((2 * DA_WIDTH, d)),
                  hist_spec, _const_spec((SSD_CONV, SSD_CONV_DIM)), _const_spec((1, SSD_CONV_DIM))],
        out_specs=[row(512), row(1024), row(LANES), hspec, hspec, row(512), tspec, tspec, hist_spec],
        out_shape=[out(512, BF16), out(1024, BF16), out(LANES, F32),
                   hshape, hshape, out(512, BF16), tshape, tshape,
                   jax.ShapeDtypeStruct((b, SSD_CONV - 1, SSD_CONV_DIM), F32)],
        scratch_shapes=[pltpu.VMEM((tm + SUBLANES, SSD_CONV_DIM), F32), pltpu.VMEM((tm, d), BF16),
                        pltpu.VMEM((tm, SSD_WIDTH), F32)],
        compiler_params=pltpu.CompilerParams(dimension_semantics=("parallel", "arbitrary"),
                                             vmem_limit_bytes=VMEM_LIMIT),
        name="inproj",
    )(x, mod3, norm_w, w_cat, w_t, hist, cw, cbias)


def _ssd_kernel(zs_ref, xc_ref, dt_ref, h0_ref, dtb_ref, alog_ref, dsk_ref, nw_ref, tri_ref, e_ref,
                y_ref, hout_ref, h_scr, ybuf, *, chunk, rows, real):
    t = pl.program_id(1)

    @pl.when(t == 0)
    def _init():
        for g in range(SSD_GROUPS):
            h_scr[g] = h0_ref[g].T

    xs = xc_ref[:, 0:SSD_WIDTH].astype(F32)
    dtv = _softplus(dt_ref[...] + dtb_ref[...])
    da = dtv * (-jnp.exp(alog_ref[...]))
    acs = jnp.dot(tri_ref[...], jnp.concatenate(_split3(da), axis=0), preferred_element_type=F32)
    dt_x = jnp.dot(jnp.concatenate(_split3(dtv), axis=1), e_ref[...], preferred_element_type=F32)
    acs_x = jnp.dot(jnp.concatenate(_split3(acs), axis=1), e_ref[...], preferred_element_type=F32)
    eacs_x = jnp.exp(acs_x)
    xd = xs * dt_x

    li = lax.broadcasted_iota(jnp.int32, (SUPER, SUPER), 0)
    si = lax.broadcasted_iota(jnp.int32, (SUPER, SUPER), 1)
    cshift = chunk.bit_length() - 1
    mask2 = ((li >> cshift) == (si >> cshift)) & (si <= li)
    lane_g = lax.broadcasted_iota(jnp.int32, (SUPER, GROUP_W), 1) >> (SSD_HEADDIM.bit_length() - 1)

    for sb in range(rows // SUPER):
        o = sb * SUPER
        nreal = (min(real, o + SUPER) - o) // chunk
        acs2 = acs[o:o + SUPER, :]
        acs_t = acs2.T
        for g in range(SSD_GROUPS):
            gs = slice(g * GROUP_W, (g + 1) * GROUP_W)
            bcol = SSD_WIDTH + g * SSD_STATE
            ccol = SSD_WIDTH + (SSD_GROUPS + g) * SSD_STATE
            bmb = xc_ref[o:o + SUPER, bcol:bcol + SSD_STATE]
            cmb = xc_ref[o:o + SUPER, ccol:ccol + SSD_STATE]
            cb2 = lax.dot_general(cmb, bmb, (((1,), (1,)), ((), ())), preferred_element_type=F32)
            bm_t = bmb.astype(F32).T.astype(BF16)
            ms = []
            for rr in range(SSD_HPG):
                r = g * SSD_HPG + rr
                seg = acs2[:, r:r + 1] - acs_t[r:r + 1, :]
                dec = jnp.where(mask2, jnp.exp(jnp.where(mask2, seg, 0.0)), 0.0)
                ms.append((cb2 * dec).astype(BF16))
            full = jnp.dot(jnp.concatenate(ms, axis=0), xd[o:o + SUPER, gs].astype(BF16),
                           preferred_element_type=F32)
            ydiag = full[0:SUPER]
            for rr in range(1, SSD_HPG):
                ydiag = jnp.where(lane_g == rr, full[rr * SUPER:(rr + 1) * SUPER], ydiag)
            ybuf[o:o + SUPER, gs] = ydiag
            for j in range(nreal):
                a0, a1 = o + j * chunk, o + (j + 1) * chunk
                h_t = h_scr[g]
                yoff = jnp.dot(cmb[j * chunk:(j + 1) * chunk, :], h_t.astype(BF16), preferred_element_type=F32)
                ybuf[a0:a1, gs] = ybuf[a0:a1, gs] + yoff * eacs_x[a0:a1, gs]
                dte = jnp.exp(acs_x[a1 - 1:a1, gs] - acs_x[a0:a1, gs])
                xw = (xd[a0:a1, gs] * dte).astype(BF16)
                pieces = []
                if j > 0:
                    pieces.append(jnp.zeros((j * chunk, GROUP_W), BF16))
                pieces.append(xw)
                if (j + 1) * chunk < SUPER:
                    pieces.append(jnp.zeros((SUPER - (j + 1) * chunk, GROUP_W), BF16))
                xw2 = jnp.concatenate(pieces, axis=0) if len(pieces) > 1 else xw
                st = jnp.dot(bm_t, xw2, preferred_element_type=F32)
                h_scr[g] = h_t * eacs_x[a1 - 1:a1, gs] + st

    y = (ybuf[...] + dsk_ref[...] * xs) * zs_ref[...].astype(F32)
    for g in range(SSD_GROUPS):
        gs = slice(g * GROUP_W, (g + 1) * GROUP_W)
        y_ref[:, gs] = _rms(y[:, gs], nw_ref[:, gs]).astype(BF16)

    @pl.when(t == pl.num_programs(1) - 1)
    def _fin():
        for g in range(SSD_GROUPS):
            hout_ref[g] = h_scr[g].T


def _ssd_call(zs, xc, dt, h0_t, dtb, alog, dsk, nw, *, chunk, rows, real):
    b, t, _ = zs.shape
    ii = np.arange(rows)
    tri = ((ii[:, None] // chunk == ii[None, :] // chunk) & (ii[None, :] <= ii[:, None])).astype(np.float32)
    e = np.zeros((LANES, SSD_WIDTH), np.float32)
    for r in range(SSD_HEADS):
        e[r, r * SSD_HEADDIM:(r + 1) * SSD_HEADDIM] = 1.0

    def row(width):
        return pl.BlockSpec((None, rows, width), lambda i, j: (i, j, 0))

    kern = functools.partial(_ssd_kernel, chunk=chunk, rows=rows, real=real)
    return pl.pallas_call(
        kern,
        grid=(b, t // rows),
        in_specs=[row(SSD_WIDTH), row(SSD_CONV_DIM), row(LANES),
                  pl.BlockSpec((None, SSD_GROUPS, GROUP_W, SSD_STATE), lambda i, j: (i, 0, 0, 0)),
                  _const_spec((1, LANES)), _const_spec((1, LANES)),
                  _const_spec((1, SSD_WIDTH)), _const_spec((1, SSD_WIDTH)),
                  _const_spec((rows, 3 * rows)), _const_spec((3 * LANES, SSD_WIDTH))],
        out_specs=[row(SSD_WIDTH),
                   pl.BlockSpec((None, SSD_GROUPS, GROUP_W, SSD_STATE), lambda i, j: (i, 0, 0, 0))],
        out_shape=[jax.ShapeDtypeStruct((b, t, SSD_WIDTH), BF16),
                   jax.ShapeDtypeStruct((b, SSD_GROUPS, GROUP_W, SSD_STATE), F32)],
        scratch_shapes=[pltpu.VMEM((SSD_GROUPS, SSD_STATE, GROUP_W), F32),
                        pltpu.VMEM((rows, SSD_WIDTH), F32)],
        compiler_params=pltpu.CompilerParams(dimension_semantics=("parallel", "arbitrary"),
                                             vmem_limit_bytes=VMEM_LIMIT),
        name="ssd",
    )(zs, xc, dt, h0_t, dtb, alog, dsk, nw,
      jnp.asarray(np.tile(tri, (1, 3)), BF16), jnp.asarray(np.tile(e, (3, 1)), BF16))


def _attn_kernel(scal_ref, qt_ref, k_ref, vt_ref, bias_ref, subw_ref, o_ref,
                 m_scr, l_scr, acc_scr, sa_scr, sb_scr, sc_scr, pa_scr, pb_scr, alpha_scr, qz_scr,
                 *, bq, bk, noff, out_scale):
    h = pl.program_id(1)
    qi = pl.program_id(2)
    kn0 = qi + (noff - 1)
    lam = scal_ref[0]
    cfar = scal_ref[1 + h]

    zero = jnp.zeros((DA_DK, bq), BF16)
    qz_scr[0, 0:DA_DK, :] = qt_ref[0:DA_DK, :]
    qz_scr[0, DA_DK:DA_DV, :] = zero
    qz_scr[1, 0:DA_DK, :] = zero
    qz_scr[1, DA_DK:DA_DV, :] = qt_ref[DA_DK:DA_DV, :]

    m_scr[...] = jnp.full(m_scr.shape, NEG, F32)
    l_scr[...] = jnp.zeros(l_scr.shape, F32)
    acc_scr[...] = jnp.zeros(acc_scr.shape, F32)

    nsub = bk // ATT_KT

    def fold(x):
        return x.reshape(ATT_KT // SUBLANES, SUBLANES, bq)

    def scores(s_buf, first, count):
        for mm in range(2):
            for e in range(count):
                start = pl.multiple_of((first + e) * bk, bk)
                s_buf[mm, e] = jnp.dot(k_ref[pl.ds(start, bk), :], qz_scr[mm], preferred_element_type=F32)

    def softmax(s_buf, p_buf, entries):
        alphas = []
        for mm in range(2):
            cand = None
            for e, (near, shift) in enumerate(entries):
                mx = None
                for t in range(nsub):
                    rows = slice(t * ATT_KT, (t + 1) * ATT_KT)
                    s = s_buf[mm, e, rows, :]
                    if near is not None:
                        s = s + bias_ref[near, rows, :]
                    pm = jnp.max(fold(s), axis=0)
                    mx = pm if mx is None else jnp.maximum(mx, pm)
                mx = jnp.max(mx, axis=0, keepdims=True) + shift
                cand = mx if cand is None else jnp.maximum(cand, mx)
            m_old = m_scr[mm]
            m_new = jnp.maximum(m_old, cand)
            ls = None
            for e, (near, shift) in enumerate(entries):
                off = m_new - shift
                for t in range(nsub):
                    rows = slice(t * ATT_KT, (t + 1) * ATT_KT)
                    s = s_buf[mm, e, rows, :]
                    if near is not None:
                        s = s + bias_ref[near, rows, :]
                    p = jnp.exp2(s - off)
                    p_buf[mm, e * bk + t * ATT_KT:e * bk + (t + 1) * ATT_KT, :] = p.astype(BF16)
                    ps = jnp.sum(fold(p), axis=0)
                    ls = ps if ls is None else ls + ps
            alpha = jnp.exp2(m_old - m_new)
            l_scr[mm] = alpha * l_scr[mm] + jnp.sum(ls, axis=0, keepdims=True)
            m_scr[mm] = m_new
            alphas.append(alpha)
        return alphas

    def pv(p_buf, first, n):
        vts = [vt_ref[first + e] for e in range(n)]
        vt = jnp.concatenate(vts, axis=1) if n > 1 else vts[0]
        return [jnp.dot(vt, p_buf[mm, 0:n * bk, :], preferred_element_type=F32) for mm in range(2)]

    def accumulate(alphas, pvs):
        for mm in range(2):
            acc_scr[mm] = alphas[mm] * acc_scr[mm] + pvs[mm]

    def softmax_pv(s_buf, first, entries):
        alphas = softmax(s_buf, pa_scr, entries)
        accumulate(alphas, pv(pa_scr, first, len(entries)))

    nfar = jnp.maximum(kn0, 0)
    odd = nfar % 2
    far = (None, cfar)

    near_pair = [(0, 0.0), (1, 0.0)]

    @pl.when(kn0 < 0)
    def _only_first():
        scores(sc_scr, 0, 1)
        softmax_pv(sc_scr, 0, [(1, 0.0)])

    @pl.when(kn0 >= 0)
    def _groups():
        @pl.when(odd == 1)
        def _single():
            scores(sc_scr, 0, 1)
            scores(sa_scr, 1, 2)
            softmax_pv(sc_scr, 0, [far])

        @pl.when(odd == 0)
        def _first_pair():
            scores(sa_scr, 0, 2)

        def far_pair(s_cur, s_next, cur):
            scores(s_next, cur + 2, 2)
            softmax_pv(s_cur, cur, [far, far])

        def far_step(s_cur, p_cur, s_next, p_prev, cur):
            scores(s_next, cur + 2, 2)
            pending = None if p_prev is None else pv(p_prev, cur - 2, 2)
            alphas = softmax(s_cur, p_cur, [far, far])
            if pending is not None:
                accumulate([alpha_scr[0], alpha_scr[1]], pending)
            for mm in range(2):
                alpha_scr[mm] = alphas[mm]

        npairs = nfar // 2
        niter = npairs // 2

        @pl.when(niter >= 1)
        def _far_loop():
            far_step(sa_scr, pa_scr, sb_scr, None, odd)
            far_step(sb_scr, pb_scr, sa_scr, pa_scr, odd + 2)

            def far_body(j, carry):
                cur = odd + 4 * j
                far_step(sa_scr, pa_scr, sb_scr, pb_scr, cur)
                far_step(sb_scr, pb_scr, sa_scr, pa_scr, cur + 2)
                return carry

            lax.fori_loop(1, niter, far_body, 0)
            accumulate([alpha_scr[0], alpha_scr[1]], pv(pb_scr, odd + 4 * niter - 2, 2))

        @pl.when(npairs % 2 == 1)
        def _tail_b():
            far_pair(sa_scr, sb_scr, kn0 - 2)
            softmax_pv(sb_scr, kn0, near_pair)

        @pl.when(npairs % 2 == 0)
        def _tail_a():
            softmax_pv(sa_scr, kn0, near_pair)

    o = acc_scr[0] * (1.0 / l_scr[0]) - acc_scr[1] * (lam / l_scr[1])
    o = o * lax.rsqrt(jnp.mean(o * o, axis=0, keepdims=True) + EPS) * (subw_ref[...] * out_scale)
    o_ref[...] = o.T.astype(BF16)


def _attn_call(scal, qt, kb, vt, bias, subw, *, bq, bk, noff, out_scale):
    b, nq = qt.shape[:2]
    tk = kb.shape[1]
    nkb = vt.shape[1]
    kern = functools.partial(_attn_kernel, bq=bq, bk=bk, noff=noff, out_scale=out_scale)
    return pl.pallas_call(
        kern,
        grid=(b, DA_HEADS, nq),
        in_specs=[pl.BlockSpec(memory_space=pltpu.SMEM),
                  pl.BlockSpec((None, None, DA_DV, bq), lambda i, h, j: (i, j, h, 0)),
                  pl.BlockSpec((None, tk, DA_DV), lambda i, h, j: (i, 0, h)),
                  pl.BlockSpec((None, nkb, DA_DV, bk), lambda i, h, j: (i, 0, h, 0)),
                  pl.BlockSpec((None, 2, bk, bq), lambda i, h, j: (h, 0, 0, 0)),
                  pl.BlockSpec((DA_DV, 1), lambda i, h, j: (0, 0))],
        out_specs=pl.BlockSpec((None, bq, DA_DV), lambda i, h, j: (i, j, h)),
        out_shape=jax.ShapeDtypeStruct((b, nq * bq, DA_WIDTH), BF16),
        scratch_shapes=[pltpu.VMEM((2, 1, bq), F32), pltpu.VMEM((2, 1, bq), F32),
                        pltpu.VMEM((2, DA_DV, bq), F32),
                        pltpu.VMEM((2, 2, bk, bq), F32), pltpu.VMEM((2, 2, bk, bq), F32),
                        pltpu.VMEM((2, 1, bk, bq), F32),
                        pltpu.VMEM((2, 2 * bk, bq), BF16), pltpu.VMEM((2, 2 * bk, bq), BF16),
                        pltpu.VMEM((2, 1, bq), F32),
                        pltpu.VMEM((2, DA_DV, bq), BF16)],
        compiler_params=pltpu.CompilerParams(dimension_semantics=("parallel", "parallel", "arbitrary"),
                                             vmem_limit_bytes=VMEM_LIMIT),
        name="attn",
    )(scal, qt, kb, vt, bias, subw)


def _attn_cached_kernel(scal_ref, qt_ref, kc_ref, vc_ref, kn_ref, vtn_ref, bias0_ref, bias1_ref, subw_ref, o_ref,
                        *, bq, tq, bk, out_scale):
    lam = scal_ref[0]
    past = kc_ref.shape[0] // DA_HEADS
    lane = lax.broadcasted_iota(jnp.int32, (tq, DA_DV), 1)
    nt = (((1,), (1,)), ((), ()))

    for h in range(DA_HEADS):
        hs = slice(h * DA_DV, (h + 1) * DA_DV)
        cfar = scal_ref[1 + h]
        qn = qt_ref[hs, :].astype(F32).T[0:tq, :]
        q2 = jnp.concatenate([jnp.where(lane < DA_DK, qn, 0.0), jnp.where(lane >= DA_DK, qn, 0.0)],
                             axis=0).astype(BF16)
        head_rows = pl.ds(h, past, stride=DA_HEADS)
        s_c = lax.dot_general(q2, kc_ref[head_rows, :].astype(BF16), nt, preferred_element_type=F32)
        s_n = lax.dot_general(q2, kn_ref[:, hs], nt, preferred_element_type=F32)
        b0 = bias0_ref[h]
        b1 = bias1_ref[h]
        s = jnp.concatenate([s_c[:, 0:past - bk] + cfar,
                             s_c[:, past - bk:past] + jnp.concatenate([b0, b0], axis=0),
                             s_n + jnp.concatenate([b1, b1], axis=0)], axis=1)
        p = jnp.exp2(s - jnp.max(s, axis=1, keepdims=True))
        inv = 1.0 / jnp.sum(p, axis=1, keepdims=True)
        pb = p.astype(BF16)
        acc = (jnp.dot(pb[:, 0:past], vc_ref[head_rows, :].astype(BF16), preferred_element_type=F32)
               + jnp.dot(pb[:, past:past + bq], vtn_ref[hs, :].astype(F32).T.astype(BF16),
                         preferred_element_type=F32))
        o = acc[0:tq, :] * inv[0:tq] - acc[tq:2 * tq, :] * (lam * inv[tq:2 * tq])
        o_ref[0:tq, hs] = (_rms(o, subw_ref[...]) * out_scale).astype(BF16)
        o_ref[tq:bq, hs] = jnp.zeros((bq - tq, DA_DV), BF16)


def _attn_cached_call(scal, qt, cache_k, cache_v, kb, vt, bias0, bias1, subw, *, tq, bk, out_scale):
    b, past = cache_k.shape[:2]
    bq = qt.shape[-1]
    rows = past * DA_HEADS
    kern = functools.partial(_attn_cached_kernel, bq=bq, tq=tq, bk=bk, out_scale=out_scale)
    cache_spec = pl.BlockSpec((None, rows, DA_DV), lambda i: (i, 0, 0))
    return pl.pallas_call(
        kern,
        grid=(b,),
        in_specs=[pl.BlockSpec(memory_space=pltpu.SMEM),
                  pl.BlockSpec((None, None, DA_WIDTH, bq), lambda i: (i, 0, 0, 0)),
                  cache_spec, cache_spec,
                  pl.BlockSpec((None, bq, DA_WIDTH), lambda i: (i, 0, 0)),
                  pl.BlockSpec((None, None, DA_WIDTH, bq), lambda i: (i, 0, 0, 0)),
                  _const_spec((DA_HEADS, tq, bk)), _const_spec((DA_HEADS, tq, bq)),
                  _const_spec((1, DA_DV))],
        out_specs=pl.BlockSpec((None, bq, DA_WIDTH), lambda i: (i, 0, 0)),
        out_shape=jax.ShapeDtypeStruct((b, bq, DA_WIDTH), BF16),
        compiler_params=pltpu.CompilerParams(dimension_semantics=("parallel",), vmem_limit_bytes=VMEM_LIMIT),
        name="attn_cached",
    )(scal, qt, cache_k.reshape(b, rows, DA_DV), cache_v.reshape(b, rows, DA_DV), kb, vt, bias0, bias1, subw)


def _rel_bucket(rel):
    nb = REL_BUCKETS // 2
    max_exact = nb // 2
    n = jnp.abs(rel)
    nf = jnp.maximum(n, 1).astype(jnp.float32)
    large = max_exact + (jnp.log(nf / max_exact) / math.log(REL_MAX_DIST / max_exact)
                         * (nb - max_exact)).astype(jnp.int32)
    large = jnp.minimum(large, nb - 1)
    return jnp.where(rel > 0, nb, 0) + jnp.where(n < max_exact, n, large)


def _bias_tiles(rel_bias, qpos0, kpos0, bq, bk, tk_real):
    qpos = qpos0 + np.arange(bq)
    span = bq + bk - 1
    tiles = []
    for d in range(2):
        kpos = kpos0[d] + np.arange(bk)
        offs = (kpos0[d] - qpos0) + np.arange(-(bq - 1), bk)
        table = rel_bias[_rel_bucket(jnp.asarray(offs, jnp.int32))].astype(F32).T * LOG2E
        rev = jnp.pad(table[:, ::-1], ((0, 0), (0, 1)))
        skew = jnp.tile(rev, (1, bk))[:, :bk * span].reshape(DA_HEADS, bk, span)
        toep = skew[:, :, bk - 1:bk - 1 + bq]
        vis = (kpos[:, None] // CHUNK <= qpos[None, :] // CHUNK) & (kpos[:, None] < tk_real)
        tiles.append(jnp.where(jnp.asarray(vis)[None], toep, NEG))
    return jnp.stack(tiles, axis=1)


def _ffn_kernel(x_ref, ys_ref, ya_ref, mod_ref, wo1_ref, wo2_ref, nfw_ref, wu_ref, cw_ref, cb_ref, wd_ref, fw_ref,
                hist_ref, y_ref, tail_ref, tail_scr, buf_a, buf_b, x1_scr, h2_scr, act_scr, *, tm, real):
    t = pl.program_id(1)
    nh = FFN_CONV - 1
    lo = SUBLANES - nh

    @pl.when(t == 0)
    def _init():
        tail_scr[lo:SUBLANES, :] = hist_ref[...]

    mix = (jnp.dot(ys_ref[...], wo1_ref[...], preferred_element_type=F32)
           + jnp.dot(ya_ref[...], wo2_ref[...], preferred_element_type=F32))
    x1 = x_ref[...] + mod_ref[2:3, :] * mix
    x1_scr[...] = x1
    h2 = _rms(x1, nfw_ref[...]) * (1.0 + mod_ref[4:5, :]) + mod_ref[3:4, :]
    h2_scr[...] = h2.astype(BF16)

    def cols(j, half):
        return slice(half * D_FF + j * FFN_CN, half * D_FF + (j + 1) * FFN_CN)

    def up(j, buf):
        for half in range(2):
            buf[half, SUBLANES:SUBLANES + tm, :] = jnp.dot(h2_scr[...], wu_ref[:, cols(j, half)],
                                                            preferred_element_type=F32)

    def conv(j, buf, half):
        cs = cols(j, half)
        buf[half, lo:SUBLANES, :] = tail_scr[lo:SUBLANES, cs]
        c = cb_ref[:, cs]
        for i in range(FFN_CONV):
            c = c + cw_ref[i:i + 1, cs] * buf[half, lo + i:lo + i + tm, :]
        tail_scr[lo:SUBLANES, cs] = buf[half, lo + real:SUBLANES + real, :]
        return c

    bufs = (buf_a, buf_b)
    up(0, bufs[0])
    for j in range(FFN_NC):
        if j + 1 < FFN_NC:
            up(j + 1, bufs[(j + 1) % 2])
        cv = conv(j, bufs[j % 2], 0)
        cg = conv(j, bufs[j % 2], 1)
        act_scr[:, j * FFN_CN:(j + 1) * FFN_CN] = (_silu(cg) * cv).astype(BF16)

    f = jnp.dot(act_scr[...], wd_ref[...], preferred_element_type=F32)
    x2 = x1_scr[...] + mod_ref[5:6, :] * f
    y_ref[...] = _rms(x2, fw_ref[...])
    tail_ref[...] = tail_scr[lo:SUBLANES, :]


def _ffn_call(x, ys, ya, mod3, wo1, wo2, nfw, wu, cw, cb, wd, fw, hist, *, tm, real):
    b, t, d = x.shape
    nh = FFN_CONV - 1

    def row(width):
        return pl.BlockSpec((None, tm, width), lambda i, j: (i, j, 0))

    hist_spec = pl.BlockSpec((None, nh, 2 * D_FF), lambda i, j: (i, 0, 0))
    kern = functools.partial(_ffn_kernel, tm=tm, real=real)
    return pl.pallas_call(
        kern,
        grid=(b, t // tm),
        in_specs=[row(d), row(SSD_WIDTH), row(DA_WIDTH),
                  pl.BlockSpec((None, 6, d), lambda i, j: (i, 0, 0)),
                  _const_spec((SSD_WIDTH, d)), _const_spec((DA_WIDTH, d)), _const_spec((1, d)),
                  _const_spec((d, 2 * D_FF)), _const_spec((FFN_CONV, 2 * D_FF)), _const_spec((1, 2 * D_FF)),
                  _const_spec((D_FF, d)), _const_spec((1, d)),
                  hist_spec],
        out_specs=[row(d), hist_spec],
        out_shape=[jax.ShapeDtypeStruct((b, t, d), F32),
                   jax.ShapeDtypeStruct((b, nh, 2 * D_FF), F32)],
        scratch_shapes=[pltpu.VMEM((SUBLANES, 2 * D_FF), F32),
                        pltpu.VMEM((2, tm + SUBLANES, FFN_CN), F32), pltpu.VMEM((2, tm + SUBLANES, FFN_CN), F32),
                        pltpu.VMEM((tm, d), F32), pltpu.VMEM((tm, d), BF16), pltpu.VMEM((tm, D_FF), BF16)],
        compiler_params=pltpu.CompilerParams(dimension_semantics=("parallel", "arbitrary"),
                                             vmem_limit_bytes=VMEM_LIMIT),
        name="ffn",
    )(x, ys, ya, mod3, wo1, wo2, nfw, wu, cw, cb, wd, fw, hist)


def _pack_params(norm_mix_w, w_in, ssm_conv_w, ssm_conv_b, ssm_dt_bias, ssm_a_log, ssm_d, ssm_norm_w,
                 lambda_q1, lambda_k1, lambda_q2, lambda_k2, attn_subln_w, rel_bias, w_out,
                 norm_ffn_w, w_up, ffn_conv_w, ffn_conv_b, w_down, final_norm_w, layer):
    l = layer
    wz, wx, wdt, wq, wk, wv = jnp.split(w_in[l], IN_SPLITS, axis=-1)
    wdt = jnp.pad(wdt, ((0, 0), (0, LANES - SSD_HEADS)))
    w_cat = jnp.concatenate([wz, wx, wdt, wk, wv], axis=-1).astype(BF16)
    w_t = jnp.concatenate([wq * (DA_DK ** -0.5 * LOG2E), wv], axis=-1).T.astype(BF16)

    def pad_heads(v):
        return jnp.pad(v.astype(F32), (0, LANES - SSD_HEADS)).reshape(1, LANES)

    lam_init = 0.8 - 0.6 * math.exp(-0.3 * l)
    lam = (jnp.exp(jnp.sum(lambda_q1[l].astype(F32) * lambda_k1[l].astype(F32)))
           - jnp.exp(jnp.sum(lambda_q2[l].astype(F32) * lambda_k2[l].astype(F32))) + lam_init)
    far_bias = rel_bias[REL_BUCKETS // 2 - 1].astype(F32)
    return dict(
        norm_mix_w=norm_mix_w[l].reshape(1, D_MODEL), w_cat=w_cat, w_t=w_t,
        cw=ssm_conv_w[l], cbias=ssm_conv_b[l].reshape(1, SSD_CONV_DIM),
        dtb=pad_heads(ssm_dt_bias[l]), alog=pad_heads(ssm_a_log[l]),
        dsk=jnp.repeat(ssm_d[l].astype(F32), SSD_HEADDIM).reshape(1, SSD_WIDTH),
        ssm_nw=ssm_norm_w[l].reshape(1, SSD_WIDTH),
        scal=jnp.concatenate([lam.reshape(1), far_bias * LOG2E]).astype(F32), lam_init=lam_init,
        subw=attn_subln_w[l].reshape(DA_DV, 1), rel_bias=rel_bias,
        wo1=w_out[l][:SSD_WIDTH].astype(BF16), wo2=w_out[l][SSD_WIDTH:].astype(BF16),
        nfw=norm_ffn_w[l].reshape(1, D_MODEL),
        wu=w_up[l].astype(BF16), ffn_cw=ffn_conv_w[l], ffn_cb=ffn_conv_b[l].reshape(1, 2 * D_FF),
        wd=w_down[l].astype(BF16), fw=final_norm_w.reshape(1, D_MODEL),
    )


def _state_to_kernel(h):
    return h.reshape(h.shape[0], SSD_GROUPS, GROUP_W, SSD_STATE)


def _state_from_kernel(h):
    return h.reshape(h.shape[0], SSD_HEADS, SSD_HEADDIM, SSD_STATE)


def _run_group(x, mod, past_k, past_v, ssm_h0, ssm_conv_hist, ffn_conv_hist, p, *, tm, ssd_rows, bq, bk):
    b, t, d = x.shape
    past = 0 if past_k is None else past_k.shape[1]
    chunk = min(CHUNK, t)
    tp = max(t, SUPER)
    if tp != t:
        x = jnp.pad(x, ((0, 0), (0, tp - t), (0, 0)))
        tm = ssd_rows = bq = tp
    mod3 = mod.reshape(b, 6, d)

    assert tm == bq
    zs, xc, dt, k, v, kb, qt, vt, conv_new = _inproj_call(
        x, mod3, p["norm_mix_w"], p["w_cat"], p["w_t"], ssm_conv_hist.astype(F32), p["cw"], p["cbias"],
        tm=tm, real=min(t, tm))

    y_ssd, h_t = _ssd_call(zs, xc, dt, _state_to_kernel(ssm_h0.astype(F32)),
                           p["dtb"], p["alog"], p["dsk"], p["ssm_nw"],
                           chunk=chunk, rows=ssd_rows, real=min(t, ssd_rows))

    if past == 0:
        assert bq == bk and t % bq == 0
        bias = _bias_tiles(p["rel_bias"], bq, (0, bq), bq, bk, 2 * bq)
        y_att = _attn_call(p["scal"], qt, kb, vt, bias, p["subw"], bq=bq, bk=bk, noff=0,
                           out_scale=1.0 - p["lam_init"])
    else:
        assert past % bk == 0 and past >= bk and bq == tp <= bk
        bias = _bias_tiles(p["rel_bias"], past, (past - bk, past), bq, bk, past + t)
        bias0 = jnp.swapaxes(bias[:, 0], 1, 2)[:, :t]
        bias1 = jnp.swapaxes(bias[:, 1, :bq], 1, 2)[:, :t]
        y_att = _attn_cached_call(p["scal"], qt, past_k, past_v, kb, vt, bias0, bias1, p["subw"].reshape(1, DA_DV),
                                  tq=t, bk=bk, out_scale=1.0 - p["lam_init"])

    y, ffn_new = _ffn_call(x, y_ssd, y_att, mod3, p["wo1"], p["wo2"], p["nfw"], p["wu"], p["ffn_cw"], p["ffn_cb"],
                           p["wd"], p["fw"], ffn_conv_hist.astype(F32), tm=tm, real=min(t, tm))
    return (y[:, :t], k[:, :t], v[:, :t],
            _state_from_kernel(h_t), conv_new, ffn_new)


def kernel(x_prompt, x_sample, c_prompt, c_sample, cache_k, cache_v, state_ssm, state_ssm_conv, state_ffn_conv, w_ada, b_ada, norm_mix_w, w_in, ssm_conv_w, ssm_conv_b, ssm_dt_bias, ssm_a_log, ssm_d, ssm_norm_w, lambda_q1, lambda_k1, lambda_q2, lambda_k2, attn_subln_w, rel_bias, w_out, norm_ffn_w, w_up, ffn_conv_w, ffn_conv_b, w_down, final_norm_w):
    bp, bs = x_prompt.shape[0], x_sample.shape[0]
    dt = x_prompt.dtype
    p = _pack_params(norm_mix_w, w_in, ssm_conv_w, ssm_conv_b, ssm_dt_bias, ssm_a_log, ssm_d, ssm_norm_w,
                     lambda_q1, lambda_k1, lambda_q2, lambda_k2, attn_subln_w, rel_bias, w_out,
                     norm_ffn_w, w_up, ffn_conv_w, ffn_conv_b, w_down, final_norm_w, 0)
    c_all = jnp.concatenate([c_prompt, c_sample], axis=0)
    npad = -c_all.shape[0] % SUBLANES
    c_all = jnp.pad(c_all, ((0, npad), (0, 0)))
    mod = _mod_call(c_all, w_ada[0], b_ada[0].reshape(1, -1))

    zeros = lambda *s: jnp.zeros(s, dt)
    out_p = _run_group(x_prompt, mod[:bp], None, None,
                       zeros(bp, SSD_HEADS, SSD_HEADDIM, SSD_STATE), zeros(bp, SSD_CONV - 1, SSD_CONV_DIM),
                       zeros(bp, FFN_CONV - 1, 2 * D_FF), p, tm=512, ssd_rows=256, bq=512, bk=512)
    out_s = _run_group(x_sample, mod[bp:bp + bs], cache_k[0], cache_v[0], state_ssm[0], state_ssm_conv[0],
                       state_ffn_conv[0], p, tm=SUPER, ssd_rows=SUPER, bq=SUPER, bk=512)
    y_p, k_p, v_p, h_p, c_p, f_p = out_p
    y_s, k_s, v_s, h_s, c_s, f_s = out_s
    return (y_p, y_s, k_p[None], v_p[None], h_p[None], c_p[None], f_p[None],
            k_s[None], v_s[None], h_s[None], c_s[None], f_s[None])
```

```python
import functools
import math

import numpy as np
import jax
import jax.numpy as jnp
from jax import lax
from jax.experimental import pallas as pl
from jax.experimental.pallas import tpu as pltpu

F32 = jnp.float32
BF16 = jnp.bfloat16
HIGHEST = lax.Precision.HIGHEST

D_MODEL = 1024
CHUNK = 64
SSD_WIDTH = 512
SSD_HEADDIM = 64
SSD_HEADS = 8
SSD_GROUPS = 2
SSD_HPG = 4
SSD_STATE = 128
SSD_CONV = 4
SSD_CONV_DIM = SSD_WIDTH + 2 * SSD_GROUPS * SSD_STATE
GROUP_W = SSD_HPG * SSD_HEADDIM
DA_WIDTH = 512
DA_DK = 64
DA_DV = 128
DA_HEADS = 4
REL_BUCKETS = 32
REL_MAX_DIST = 128
D_FF = 2816
FFN_CONV = 3
EPS = 1e-6
IN_SPLITS = (512, 1536, 1544, 2056, 2568)
LANES = 128
SUBLANES = 8
SUPER = 128
SSD_NSEQ = 4
ATT_KT = 64
FFN_CN = 256
FFN_NC = D_FF // FFN_CN
NEG = -1e30
VMEM_LIMIT = 56 * 1024 * 1024

PZ, PX, PDT, PK, PV, PEND = 0, 512, 1536, 1664, 2176, 2688
LOG2E = math.log2(math.e)


def _silu(x):
    return x / (1.0 + jnp.exp(-x))


def _softplus(x):
    return jnp.maximum(x, 0.0) + jnp.log1p(jnp.exp(-jnp.abs(x)))


def _split3(x):
    hi = x.astype(BF16)
    r1 = x - hi.astype(F32)
    mid = r1.astype(BF16)
    lo = (r1 - mid.astype(F32)).astype(BF16)
    return hi, mid, lo


def _rms(x, w):
    return x * lax.rsqrt(jnp.mean(x * x, axis=-1, keepdims=True) + EPS) * w


def _const_spec(shape):
    nd = len(shape)
    return pl.BlockSpec(shape, lambda *_: (0,) * nd)


def _mod_kernel(c_ref, w_ref, b_ref, o_ref):
    a = _silu(c_ref[...]).astype(BF16)
    o_ref[...] = jnp.dot(a, w_ref[...].astype(BF16), preferred_element_type=F32) + b_ref[...]


def _mod_call(c, w_ada, b_ada):
    n, d = c.shape
    nout = w_ada.shape[1]
    tn = 1024
    return pl.pallas_call(
        _mod_kernel,
        grid=(nout // tn,),
        in_specs=[pl.BlockSpec((n, d), lambda j: (0, 0)),
                  pl.BlockSpec((d, tn), lambda j: (0, j)),
                  pl.BlockSpec((1, tn), lambda j: (0, j))],
        out_specs=pl.BlockSpec((n, tn), lambda j: (0, j)),
        out_shape=jax.ShapeDtypeStruct((n, nout), F32),
        name="mod",
    )(c, w_ada, b_ada)


def _inproj_kernel(x_ref, mod_ref, nw_ref, w_ref, wt_ref, hist_ref, cw_ref, cbias_ref,
                   zs_ref, xc_ref, dt_ref, k_ref, v_ref, kb_ref, qt_ref, vt_ref, cout_ref, cbuf, hb_scr, zbuf,
                   *, tm, real):
    t = pl.program_id(1)
    nconv = SSD_CONV - 1

    @pl.when(t == 0)
    def _init():
        cbuf[0:SUBLANES, :] = jnp.zeros((SUBLANES, SSD_CONV_DIM), F32)
        cbuf[SUBLANES - nconv:SUBLANES, :] = hist_ref[...]

    h = _rms(x_ref[...], nw_ref[...]) * (1.0 + mod_ref[1:2, :]) + mod_ref[0:1, :]
    hb_scr[...] = h.astype(BF16)

    def proj(a, b):
        return jnp.dot(hb_scr[...], w_ref[:, a:b], preferred_element_type=F32)

    def proj_t(a, b):
        return lax.dot_general(wt_ref[a:b, :], hb_scr[...], (((1,), (1,)), ((), ())), preferred_element_type=F32)

    cbuf[SUBLANES:SUBLANES + tm, :] = proj(PX, PDT)
    zbuf[...] = proj(PZ, PX)
    dt_ref[...] = proj(PDT, PK)
    k = proj(PK, PV)
    v = proj(PV, PEND)
    for hd in range(DA_HEADS):
        k_ref[:, hd, :] = k[:, hd * DA_DV:(hd + 1) * DA_DV]
        v_ref[:, hd, :] = v[:, hd * DA_DV:(hd + 1) * DA_DV]
    kb_ref[...] = k.astype(BF16)
    qt_ref[...] = proj_t(0, DA_WIDTH).astype(BF16)
    vt_ref[...] = proj_t(DA_WIDTH, 2 * DA_WIDTH).astype(BF16)
    conv = cbias_ref[...]
    for j in range(SSD_CONV):
        off = SUBLANES - nconv + j
        conv = conv + cw_ref[j:j + 1, :] * cbuf[off:off + tm, :]
    tail = cbuf[SUBLANES - nconv + real:SUBLANES + real, :]
    cout_ref[...] = tail
    cbuf[SUBLANES - nconv:SUBLANES, :] = tail
    xc_ref[...] = _silu(conv).astype(BF16)
    zs_ref[...] = _silu(zbuf[...]).astype(BF16)


def _inproj_call(x, mod3, norm_w, w_cat, w_t, hist, cw, cbias, *, tm, real):
    b, t, d = x.shape
    nt = t // tm

    def row(width):
        return pl.BlockSpec((None, tm, width), lambda i, j: (i, j, 0))

    def out(width, dtype):
        return jax.ShapeDtypeStruct((b, t, width), dtype)

    tspec = pl.BlockSpec((None, None, DA_WIDTH, tm), lambda i, j: (i, j, 0, 0))
    tshape = jax.ShapeDtypeStruct((b, nt, DA_WIDTH, tm), BF16)
    hist_spec = pl.BlockSpec((None, SSD_CONV - 1, SSD_CONV_DIM), lambda i, j: (i, 0, 0))
    hspec = pl.BlockSpec((None, tm, DA_HEADS, DA_DV), lambda i, j: (i, j, 0, 0))
    hshape = jax.ShapeDtypeStruct((b, t, DA_HEADS, DA_DV), F32)
    return pl.pallas_call(
        functools.partial(_inproj_kernel, tm=tm, real=real),
        grid=(b, nt),
        in_specs=[row(d),
                  pl.BlockSpec((None, 6, d), lambda i, j: (i, 0, 0)),
                  _const_spec((1, d)),
                  _const_spec((d, PEND)),
                  _const_spec((2 * DA_WIDTH, d)),
                  hist_spec, _const_spec((SSD_CONV, SSD_CONV_DIM)), _const_spec((1, SSD_CONV_DIM))],
        out_specs=[row(512), row(1024), row(LANES), hspec, hspec, row(512), tspec, tspec, hist_spec],
        out_shape=[out(512, BF16), out(1024, BF16), out(LANES, F32),
                   hshape, hshape, out(512, BF16), tshape, tshape,
                   jax.ShapeDtypeStruct((b, SSD_CONV - 1, SSD_CONV_DIM), F32)],
        scratch_shapes=[pltpu.VMEM((tm + SUBLANES, SSD_CONV_DIM), F32), pltpu.VMEM((tm, d), BF16),
                        pltpu.VMEM((tm, SSD_WIDTH), F32)],
        compiler_params=pltpu.CompilerParams(dimension_semantics=("parallel", "arbitrary"),
                                             vmem_limit_bytes=VMEM_LIMIT),
        name="inproj",
    )(x, mod3, norm_w, w_cat, w_t, hist, cw, cbias)


def _ssd_kernel(zs_ref, xc_ref, dt_ref, h0_ref, dtb_ref, alog_ref, dsk_ref, nw_ref, tri_ref, e_ref,
                y_ref, hout_ref, h_scr, ybuf, *, nseq, chunk, rows, real):
    t = pl.program_id(1)
    seqs = range(nseq)

    @pl.when(t == 0)
    def _init():
        for i in seqs:
            for g in range(SSD_GROUPS):
                h_scr[i, g] = h0_ref[i, g].T

    li = lax.broadcasted_iota(jnp.int32, (SUPER, SUPER), 0)
    si = lax.broadcasted_iota(jnp.int32, (SUPER, SUPER), 1)
    cshift = chunk.bit_length() - 1
    mask2 = ((li >> cshift) == (si >> cshift)) & (si <= li)
    lane_g = lax.broadcasted_iota(jnp.int32, (SUPER, GROUP_W), 1) >> (SSD_HEADDIM.bit_length() - 1)

    pre = []
    for i in seqs:
        xs = xc_ref[i, :, 0:SSD_WIDTH].astype(F32)
        dtv = _softplus(dt_ref[i] + dtb_ref[...])
        da = dtv * (-jnp.exp(alog_ref[...]))
        acs = jnp.dot(tri_ref[...], jnp.concatenate(_split3(da), axis=0), preferred_element_type=F32)
        dt_x = jnp.dot(jnp.concatenate(_split3(dtv), axis=1), e_ref[...], preferred_element_type=F32)
        acs_x = jnp.dot(jnp.concatenate(_split3(acs), axis=1), e_ref[...], preferred_element_type=F32)
        pre.append((xs, acs, acs_x, jnp.exp(acs_x), xs * dt_x))

    for sb in range(rows // SUPER):
        o = sb * SUPER
        nreal = (min(real, o + SUPER) - o) // chunk
        acs2 = [pre[i][1][o:o + SUPER, :] for i in seqs]
        acs_t = [a.T for a in acs2]
        for g in range(SSD_GROUPS):
            gs = slice(g * GROUP_W, (g + 1) * GROUP_W)
            bcol = SSD_WIDTH + g * SSD_STATE
            ccol = SSD_WIDTH + (SSD_GROUPS + g) * SSD_STATE
            for i in seqs:
                _, _, acs_x, eacs_x, xd = pre[i]
                bmb = xc_ref[i, o:o + SUPER, bcol:bcol + SSD_STATE]
                cmb = xc_ref[i, o:o + SUPER, ccol:ccol + SSD_STATE]
                cb2 = lax.dot_general(cmb, bmb, (((1,), (1,)), ((), ())), preferred_element_type=F32)
                bm_t = bmb.astype(F32).T.astype(BF16)
                ms = []
                for rr in range(SSD_HPG):
                    r = g * SSD_HPG + rr
                    seg = acs2[i][:, r:r + 1] - acs_t[i][r:r + 1, :]
                    dec = jnp.where(mask2, jnp.exp(jnp.where(mask2, seg, 0.0)), 0.0)
                    ms.append((cb2 * dec).astype(BF16))
                full = jnp.dot(jnp.concatenate(ms, axis=0), xd[o:o + SUPER, gs].astype(BF16),
                               preferred_element_type=F32)
                ydiag = full[0:SUPER]
                for rr in range(1, SSD_HPG):
                    ydiag = jnp.where(lane_g == rr, full[rr * SUPER:(rr + 1) * SUPER], ydiag)
                ybuf[i, o:o + SUPER, gs] = ydiag
                for j in range(nreal):
                    a0, a1 = o + j * chunk, o + (j + 1) * chunk
                    h_t = h_scr[i, g]
                    yoff = jnp.dot(cmb[j * chunk:(j + 1) * chunk, :], h_t.astype(BF16), preferred_element_type=F32)
                    ybuf[i, a0:a1, gs] = ybuf[i, a0:a1, gs] + yoff * eacs_x[a0:a1, gs]
                    dte = jnp.exp(acs_x[a1 - 1:a1, gs] - acs_x[a0:a1, gs])
                    xw = (xd[a0:a1, gs] * dte).astype(BF16)
                    pieces = []
                    if j > 0:
                        pieces.append(jnp.zeros((j * chunk, GROUP_W), BF16))
                    pieces.append(xw)
                    if (j + 1) * chunk < SUPER:
                        pieces.append(jnp.zeros((SUPER - (j + 1) * chunk, GROUP_W), BF16))
                    xw2 = jnp.concatenate(pieces, axis=0) if len(pieces) > 1 else xw
                    st = jnp.dot(bm_t, xw2, preferred_element_type=F32)
                    h_scr[i, g] = h_t * eacs_x[a1 - 1:a1, gs] + st

    for i in seqs:
        y = (ybuf[i] + dsk_ref[...] * pre[i][0]) * zs_ref[i].astype(F32)
        for g in range(SSD_GROUPS):
            gs = slice(g * GROUP_W, (g + 1) * GROUP_W)
            y_ref[i, :, gs] = _rms(y[:, gs], nw_ref[:, gs]).astype(BF16)

    @pl.when(t == pl.num_programs(1) - 1)
    def _fin():
        for i in seqs:
            for g in range(SSD_GROUPS):
                hout_ref[i, g] = h_scr[i, g].T


def _ssd_call(zs, xc, dt, h0_t, dtb, alog, dsk, nw, *, chunk, rows, real):
    b, t, _ = zs.shape
    ii = np.arange(rows)
    tri = ((ii[:, None] // chunk == ii[None, :] // chunk) & (ii[None, :] <= ii[:, None])).astype(np.float32)
    e = np.zeros((LANES, SSD_WIDTH), np.float32)
    for r in range(SSD_HEADS):
        e[r, r * SSD_HEADDIM:(r + 1) * SSD_HEADDIM] = 1.0

    nseq = math.gcd(b, SSD_NSEQ)

    def row(width):
        return pl.BlockSpec((nseq, rows, width), lambda i, j: (i, j, 0))

    state_spec = pl.BlockSpec((nseq, SSD_GROUPS, GROUP_W, SSD_STATE), lambda i, j: (i, 0, 0, 0))
    kern = functools.partial(_ssd_kernel, nseq=nseq, chunk=chunk, rows=rows, real=real)
    return pl.pallas_call(
        kern,
        grid=(b // nseq, t // rows),
        in_specs=[row(SSD_WIDTH), row(SSD_CONV_DIM), row(LANES), state_spec,
                  _const_spec((1, LANES)), _const_spec((1, LANES)),
                  _const_spec((1, SSD_WIDTH)), _const_spec((1, SSD_WIDTH)),
                  _const_spec((rows, 3 * rows)), _const_spec((3 * LANES, SSD_WIDTH))],
        out_specs=[row(SSD_WIDTH), state_spec],
        out_shape=[jax.ShapeDtypeStruct((b, t, SSD_WIDTH), BF16),
                   jax.ShapeDtypeStruct((b, SSD_GROUPS, GROUP_W, SSD_STATE), F32)],
        scratch_shapes=[pltpu.VMEM((nseq, SSD_GROUPS, SSD_STATE, GROUP_W), F32),
                        pltpu.VMEM((nseq, rows, SSD_WIDTH), F32)],
        compiler_params=pltpu.CompilerParams(dimension_semantics=("parallel", "arbitrary"),
                                             vmem_limit_bytes=VMEM_LIMIT),
        name="ssd",
    )(zs, xc, dt, h0_t, dtb, alog, dsk, nw,
      jnp.asarray(np.tile(tri, (1, 3)), BF16), jnp.asarray(np.tile(e, (3, 1)), BF16))


def _attn_kernel(scal_ref, qt_ref, k_ref, vt_ref, bias_ref, subw_ref, o_ref,
                 m_scr, l_scr, acc_scr, sa_scr, sb_scr, sc_scr, pa_scr, pb_scr, alpha_scr, qz_scr,
                 *, bq, bk, noff, out_scale):
    h = pl.program_id(1)
    qi = pl.program_id(2)
    kn0 = qi + (noff - 1)
    lam = scal_ref[0]
    cfar = scal_ref[1 + h]

    zero = jnp.zeros((DA_DK, bq), BF16)
    qz_scr[0, 0:DA_DK, :] = qt_ref[0:DA_DK, :]
    qz_scr[0, DA_DK:DA_DV, :] = zero
    qz_scr[1, 0:DA_DK, :] = zero
    qz_scr[1, DA_DK:DA_DV, :] = qt_ref[DA_DK:DA_DV, :]

    m_scr[...] = jnp.full(m_scr.shape, NEG, F32)
    l_scr[...] = jnp.zeros(l_scr.shape, F32)
    acc_scr[...] = jnp.zeros(acc_scr.shape, F32)

    nsub = bk // ATT_KT

    def fold(x):
        return x.reshape(ATT_KT // SUBLANES, SUBLANES, bq)

    def scores(s_buf, first, count):
        for mm in range(2):
            for e in range(count):
                start = pl.multiple_of((first + e) * bk, bk)
                s_buf[mm, e] = jnp.dot(k_ref[pl.ds(start, bk), :], qz_scr[mm], preferred_element_type=F32)

    def softmax(s_buf, p_buf, entries):
        alphas = []
        for mm in range(2):
            cand = None
            for e, (near, shift) in enumerate(entries):
                mx = None
                for t in range(nsub):
                    rows = slice(t * ATT_KT, (t + 1) * ATT_KT)
                    s = s_buf[mm, e, rows, :]
                    if near is not None:
                        s = s + bias_ref[near, rows, :]
                    pm = jnp.max(fold(s), axis=0)
                    mx = pm if mx is None else jnp.maximum(mx, pm)
                mx = jnp.max(mx, axis=0, keepdims=True) + shift
                cand = mx if cand is None else jnp.maximum(cand, mx)
            m_old = m_scr[mm]
            m_new = jnp.maximum(m_old, cand)
            ls = None
            for e, (near, shift) in enumerate(entries):
                off = m_new - shift
                for t in range(nsub):
                    rows = slice(t * ATT_KT, (t + 1) * ATT_KT)
                    s = s_buf[mm, e, rows, :]
                    if near is not None:
                        s = s + bias_ref[near, rows, :]
                    p = jnp.exp2(s - off)
                    p_buf[mm, e * bk + t * ATT_KT:e * bk + (t + 1) * ATT_KT, :] = p.astype(BF16)
                    ps = jnp.sum(fold(p), axis=0)
                    ls = ps if ls is None else ls + ps
            alpha = jnp.exp2(m_old - m_new)
            l_scr[mm] = alpha * l_scr[mm] + jnp.sum(ls, axis=0, keepdims=True)
            m_scr[mm] = m_new
            alphas.append(alpha)
        return alphas

    def pv(p_buf, first, n):
        vts = [vt_ref[first + e] for e in range(n)]
        vt = jnp.concatenate(vts, axis=1) if n > 1 else vts[0]
        return [jnp.dot(vt, p_buf[mm, 0:n * bk, :], preferred_element_type=F32) for mm in range(2)]

    def accumulate(alphas, pvs):
        for mm in range(2):
            acc_scr[mm] = alphas[mm] * acc_scr[mm] + pvs[mm]

    def softmax_pv(s_buf, first, entries):
        alphas = softmax(s_buf, pa_scr, entries)
        accumulate(alphas, pv(pa_scr, first, len(entries)))

    nfar = jnp.maximum(kn0, 0)
    odd = nfar % 2
    far = (None, cfar)

    near_pair = [(0, 0.0), (1, 0.0)]

    @pl.when(kn0 < 0)
    def _only_first():
        scores(sc_scr, 0, 1)
        softmax_pv(sc_scr, 0, [(1, 0.0)])

    @pl.when(kn0 >= 0)
    def _groups():
        @pl.when(odd == 1)
        def _single():
            scores(sc_scr, 0, 1)
            scores(sa_scr, 1, 2)
            softmax_pv(sc_scr, 0, [far])

        @pl.when(odd == 0)
        def _first_pair():
            scores(sa_scr, 0, 2)

        def far_pair(s_cur, s_next, cur):
            scores(s_next, cur + 2, 2)
            softmax_pv(s_cur, cur, [far, far])

        def far_step(s_cur, p_cur, s_next, p_prev, cur):
            scores(s_next, cur + 2, 2)
            pending = None if p_prev is None else pv(p_prev, cur - 2, 2)
            alphas = softmax(s_cur, p_cur, [far, far])
            if pending is not None:
                accumulate([alpha_scr[0], alpha_scr[1]], pending)
            for mm in range(2):
                alpha_scr[mm] = alphas[mm]

        npairs = nfar // 2
        niter = npairs // 2

        @pl.when(niter >= 1)
        def _far_loop():
            far_step(sa_scr, pa_scr, sb_scr, None, odd)
            far_step(sb_scr, pb_scr, sa_scr, pa_scr, odd + 2)

            def far_body(j, carry):
                cur = odd + 4 * j
                far_step(sa_scr, pa_scr, sb_scr, pb_scr, cur)
                far_step(sb_scr, pb_scr, sa_scr, pa_scr, cur + 2)
                return carry

            lax.fori_loop(1, niter, far_body, 0)
            accumulate([alpha_scr[0], alpha_scr[1]], pv(pb_scr, odd + 4 * niter - 2, 2))

        @pl.when(npairs % 2 == 1)
        def _tail_b():
            far_pair(sa_scr, sb_scr, kn0 - 2)
            softmax_pv(sb_scr, kn0, near_pair)

        @pl.when(npairs % 2 == 0)
        def _tail_a():
            softmax_pv(sa_scr, kn0, near_pair)

    o = acc_scr[0] * (1.0 / l_scr[0]) - acc_scr[1] * (lam / l_scr[1])
    o = o * lax.rsqrt(jnp.mean(o * o, axis=0, keepdims=True) + EPS) * (subw_ref[...] * out_scale)
    o_ref[...] = o.T.astype(BF16)


def _attn_call(scal, qt, kb, vt, bias, subw, *, bq, bk, noff, out_scale):
    b, nq = qt.shape[:2]
    tk = kb.shape[1]
    nkb = vt.shape[1]
    kern = functools.partial(_attn_kernel, bq=bq, bk=bk, noff=noff, out_scale=out_scale)
    return pl.pallas_call(
        kern,
        grid=(b, DA_HEADS, nq),
        in_specs=[pl.BlockSpec(memory_space=pltpu.SMEM),
                  pl.BlockSpec((None, None, DA_DV, bq), lambda i, h, j: (i, j, h, 0)),
                  pl.BlockSpec((None, tk, DA_DV), lambda i, h, j: (i, 0, h)),
                  pl.BlockSpec((None, nkb, DA_DV, bk), lambda i, h, j: (i, 0, h, 0)),
                  pl.BlockSpec((None, 2, bk, bq), lambda i, h, j: (h, 0, 0, 0)),
                  pl.BlockSpec((DA_DV, 1), lambda i, h, j: (0, 0))],
        out_specs=pl.BlockSpec((None, bq, DA_DV), lambda i, h, j: (i, j, h)),
        out_shape=jax.ShapeDtypeStruct((b, nq * bq, DA_WIDTH), BF16),
        scratch_shapes=[pltpu.VMEM((2, 1, bq), F32), pltpu.VMEM((2, 1, bq), F32),
                        pltpu.VMEM((2, DA_DV, bq), F32),
                        pltpu.VMEM((2, 2, bk, bq), F32), pltpu.VMEM((2, 2, bk, bq), F32),
                        pltpu.VMEM((2, 1, bk, bq), F32),
                        pltpu.VMEM((2, 2 * bk, bq), BF16), pltpu.VMEM((2, 2 * bk, bq), BF16),
                        pltpu.VMEM((2, 1, bq), F32),
                        pltpu.VMEM((2, DA_DV, bq), BF16)],
        compiler_params=pltpu.CompilerParams(dimension_semantics=("parallel", "parallel", "arbitrary"),
                                             vmem_limit_bytes=VMEM_LIMIT),
        name="attn",
    )(scal, qt, kb, vt, bias, subw)


def _attn_cached_kernel(scal_ref, qt_ref, kc_ref, vc_ref, kn_ref, vtn_ref, bias0_ref, bias1_ref, subw_ref, o_ref,
                        *, bq, tq, bk, out_scale):
    lam = scal_ref[0]
    past = kc_ref.shape[0] // DA_HEADS
    lane = lax.broadcasted_iota(jnp.int32, (tq, DA_DV), 1)
    nt = (((1,), (1,)), ((), ()))

    for h in range(DA_HEADS):
        hs = slice(h * DA_DV, (h + 1) * DA_DV)
        cfar = scal_ref[1 + h]
        qn = qt_ref[hs, :].astype(F32).T[0:tq, :]
        q2 = jnp.concatenate([jnp.where(lane < DA_DK, qn, 0.0), jnp.where(lane >= DA_DK, qn, 0.0)],
                             axis=0).astype(BF16)
        head_rows = pl.ds(h, past, stride=DA_HEADS)
        s_c = lax.dot_general(q2, kc_ref[head_rows, :].astype(BF16), nt, preferred_element_type=F32)
        s_n = lax.dot_general(q2, kn_ref[:, hs], nt, preferred_element_type=F32)
        b0 = bias0_ref[h]
        b1 = bias1_ref[h]
        s = jnp.concatenate([s_c[:, 0:past - bk] + cfar,
                             s_c[:, past - bk:past] + jnp.concatenate([b0, b0], axis=0),
                             s_n + jnp.concatenate([b1, b1], axis=0)], axis=1)
        p = jnp.exp2(s - jnp.max(s, axis=1, keepdims=True))
        inv = 1.0 / jnp.sum(p, axis=1, keepdims=True)
        pb = p.astype(BF16)
        acc = (jnp.dot(pb[:, 0:past], vc_ref[head_rows, :].astype(BF16), preferred_element_type=F32)
               + jnp.dot(pb[:, past:past + bq], vtn_ref[hs, :].astype(F32).T.astype(BF16),
                         preferred_element_type=F32))
        o = acc[0:tq, :] * inv[0:tq] - acc[tq:2 * tq, :] * (lam * inv[tq:2 * tq])
        o_ref[0:tq, hs] = (_rms(o, subw_ref[...]) * out_scale).astype(BF16)
        o_ref[tq:bq, hs] = jnp.zeros((bq - tq, DA_DV), BF16)


def _attn_cached_call(scal, qt, cache_k, cache_v, kb, vt, bias0, bias1, subw, *, tq, bk, out_scale):
    b, past = cache_k.shape[:2]
    bq = qt.shape[-1]
    rows = past * DA_HEADS
    kern = functools.partial(_attn_cached_kernel, bq=bq, tq=tq, bk=bk, out_scale=out_scale)
    cache_spec = pl.BlockSpec((None, rows, DA_DV), lambda i: (i, 0, 0))
    return pl.pallas_call(
        kern,
        grid=(b,),
        in_specs=[pl.BlockSpec(memory_space=pltpu.SMEM),
                  pl.BlockSpec((None, None, DA_WIDTH, bq), lambda i: (i, 0, 0, 0)),
                  cache_spec, cache_spec,
                  pl.BlockSpec((None, bq, DA_WIDTH), lambda i: (i, 0, 0)),
                  pl.BlockSpec((None, None, DA_WIDTH, bq), lambda i: (i, 0, 0, 0)),
                  _const_spec((DA_HEADS, tq, bk)), _const_spec((DA_HEADS, tq, bq)),
                  _const_spec((1, DA_DV))],
        out_specs=pl.BlockSpec((None, bq, DA_WIDTH), lambda i: (i, 0, 0)),
        out_shape=jax.ShapeDtypeStruct((b, bq, DA_WIDTH), BF16),
        compiler_params=pltpu.CompilerParams(dimension_semantics=("parallel",), vmem_limit_bytes=VMEM_LIMIT),
        name="attn_cached",
    )(scal, qt, cache_k.reshape(b, rows, DA_DV), cache_v.reshape(b, rows, DA_DV), kb, vt, bias0, bias1, subw)


def _rel_bucket(rel):
    nb = REL_BUCKETS // 2
    max_exact = nb // 2
    n = jnp.abs(rel)
    nf = jnp.maximum(n, 1).astype(jnp.float32)
    large = max_exact + (jnp.log(nf / max_exact) / math.log(REL_MAX_DIST / max_exact)
                         * (nb - max_exact)).astype(jnp.int32)
    large = jnp.minimum(large, nb - 1)
    return jnp.where(rel > 0, nb, 0) + jnp.where(n < max_exact, n, large)


def _bias_tiles(rel_bias, qpos0, kpos0, bq, bk, tk_real):
    qpos = qpos0 + np.arange(bq)
    span = bq + bk - 1
    tiles = []
    for d in range(2):
        kpos = kpos0[d] + np.arange(bk)
        offs = (kpos0[d] - qpos0) + np.arange(-(bq - 1), bk)
        table = rel_bias[_rel_bucket(jnp.asarray(offs, jnp.int32))].astype(F32).T * LOG2E
        rev = jnp.pad(table[:, ::-1], ((0, 0), (0, 1)))
        skew = jnp.tile(rev, (1, bk))[:, :bk * span].reshape(DA_HEADS, bk, span)
        toep = skew[:, :, bk - 1:bk - 1 + bq]
        vis = (kpos[:, None] // CHUNK <= qpos[None, :] // CHUNK) & (kpos[:, None] < tk_real)
        tiles.append(jnp.where(jnp.asarray(vis)[None], toep, NEG))
    return jnp.stack(tiles, axis=1)


def _ffn_kernel(x_ref, ys_ref, ya_ref, mod_ref, wo1_ref, wo2_ref, nfw_ref, wu_ref, cw_ref, cb_ref, wd_ref, fw_ref,
                hist_ref, y_ref, tail_ref, tail_scr, buf_a, buf_b, x1_scr, h2_scr, act_scr, *, tm, real):
    t = pl.program_id(1)
    nh = FFN_CONV - 1
    lo = SUBLANES - nh

    @pl.when(t == 0)
    def _init():
        tail_scr[lo:SUBLANES, :] = hist_ref[...]

    mix = (jnp.dot(ys_ref[...], wo1_ref[...], preferred_element_type=F32)
           + jnp.dot(ya_ref[...], wo2_ref[...], preferred_element_type=F32))
    x1 = x_ref[...] + mod_ref[2:3, :] * mix
    x1_scr[...] = x1
    h2 = _rms(x1, nfw_ref[...]) * (1.0 + mod_ref[4:5, :]) + mod_ref[3:4, :]
    h2_scr[...] = h2.astype(BF16)

    def cols(j, half):
        return slice(half * D_FF + j * FFN_CN, half * D_FF + (j + 1) * FFN_CN)

    def up(j, buf):
        for half in range(2):
            buf[half, SUBLANES:SUBLANES + tm, :] = jnp.dot(h2_scr[...], wu_ref[:, cols(j, half)],
                                                            preferred_element_type=F32)

    def conv(j, buf, half):
        cs = cols(j, half)
        buf[half, lo:SUBLANES, :] = tail_scr[lo:SUBLANES, cs]
        c = cb_ref[:, cs]
        for i in range(FFN_CONV):
            c = c + cw_ref[i:i + 1, cs] * buf[half, lo + i:lo + i + tm, :]
        tail_scr[lo:SUBLANES, cs] = buf[half, lo + real:SUBLANES + real, :]
        return c

    bufs = (buf_a, buf_b)
    up(0, bufs[0])
    for j in range(FFN_NC):
        if j + 1 < FFN_NC:
            up(j + 1, bufs[(j + 1) % 2])
        cv = conv(j, bufs[j % 2], 0)
        cg = conv(j, bufs[j % 2], 1)
        act_scr[:, j * FFN_CN:(j + 1) * FFN_CN] = (_silu(cg) * cv).astype(BF16)

    f = jnp.dot(act_scr[...], wd_ref[...], preferred_element_type=F32)
    x2 = x1_scr[...] + mod_ref[5:6, :] * f
    y_ref[...] = _rms(x2, fw_ref[...])
    tail_ref[...] = tail_scr[lo:SUBLANES, :]


def _ffn_call(x, ys, ya, mod3, wo1, wo2, nfw, wu, cw, cb, wd, fw, hist, *, tm, real):
    b, t, d = x.shape
    nh = FFN_CONV - 1

    def row(width):
        return pl.BlockSpec((None, tm, width), lambda i, j: (i, j, 0))

    hist_spec = pl.BlockSpec((None, nh, 2 * D_FF), lambda i, j: (i, 0, 0))
    kern = functools.partial(_ffn_kernel, tm=tm, real=real)
    return pl.pallas_call(
        kern,
        grid=(b, t // tm),
        in_specs=[row(d), row(SSD_WIDTH), row(DA_WIDTH),
                  pl.BlockSpec((None, 6, d), lambda i, j: (i, 0, 0)),
                  _const_spec((SSD_WIDTH, d)), _const_spec((DA_WIDTH, d)), _const_spec((1, d)),
                  _const_spec((d, 2 * D_FF)), _const_spec((FFN_CONV, 2 * D_FF)), _const_spec((1, 2 * D_FF)),
                  _const_spec((D_FF, d)), _const_spec((1, d)),
                  hist_spec],
        out_specs=[row(d), hist_spec],
        out_shape=[jax.ShapeDtypeStruct((b, t, d), F32),
                   jax.ShapeDtypeStruct((b, nh, 2 * D_FF), F32)],
        scratch_shapes=[pltpu.VMEM((SUBLANES, 2 * D_FF), F32),
                        pltpu.VMEM((2, tm + SUBLANES, FFN_CN), F32), pltpu.VMEM((2, tm + SUBLANES, FFN_CN), F32),
                        pltpu.VMEM((tm, d), F32), pltpu.VMEM((tm, d), BF16), pltpu.VMEM((tm, D_FF), BF16)],
        compiler_params=pltpu.CompilerParams(dimension_semantics=("parallel", "arbitrary"),
                                             vmem_limit_bytes=VMEM_LIMIT),
        name="ffn",
    )(x, ys, ya, mod3, wo1, wo2, nfw, wu, cw, cb, wd, fw, hist)


def _pack_params(norm_mix_w, w_in, ssm_conv_w, ssm_conv_b, ssm_dt_bias, ssm_a_log, ssm_d, ssm_norm_w,
                 lambda_q1, lambda_k1, lambda_q2, lambda_k2, attn_subln_w, rel_bias, w_out,
                 norm_ffn_w, w_up, ffn_conv_w, ffn_conv_b, w_down, final_norm_w, layer):
    l = layer
    wz, wx, wdt, wq, wk, wv = jnp.split(w_in[l], IN_SPLITS, axis=-1)
    wdt = jnp.pad(wdt, ((0, 0), (0, LANES - SSD_HEADS)))
    w_cat = jnp.concatenate([wz, wx, wdt, wk, wv], axis=-1).astype(BF16)
    w_t = jnp.concatenate([wq * (DA_DK ** -0.5 * LOG2E), wv], axis=-1).T.astype(BF16)

    def pad_heads(v):
        return jnp.pad(v.astype(F32), (0, LANES - SSD_HEADS)).reshape(1, LANES)

    lam_init = 0.8 - 0.6 * math.exp(-0.3 * l)
    lam = (jnp.exp(jnp.sum(lambda_q1[l].astype(F32) * lambda_k1[l].astype(F32)))
           - jnp.exp(jnp.sum(lambda_q2[l].astype(F32) * lambda_k2[l].astype(F32))) + lam_init)
    far_bias = rel_bias[REL_BUCKETS // 2 - 1].astype(F32)
    return dict(
        norm_mix_w=norm_mix_w[l].reshape(1, D_MODEL), w_cat=w_cat, w_t=w_t,
        cw=ssm_conv_w[l], cbias=ssm_conv_b[l].reshape(1, SSD_CONV_DIM),
        dtb=pad_heads(ssm_dt_bias[l]), alog=pad_heads(ssm_a_log[l]),
        dsk=jnp.repeat(ssm_d[l].astype(F32), SSD_HEADDIM).reshape(1, SSD_WIDTH),
        ssm_nw=ssm_norm_w[l].reshape(1, SSD_WIDTH),
        scal=jnp.concatenate([lam.reshape(1), far_bias * LOG2E]).astype(F32), lam_init=lam_init,
        subw=attn_subln_w[l].reshape(DA_DV, 1), rel_bias=rel_bias,
        wo1=w_out[l][:SSD_WIDTH].astype(BF16), wo2=w_out[l][SSD_WIDTH:].astype(BF16),
        nfw=norm_ffn_w[l].reshape(1, D_MODEL),
        wu=w_up[l].astype(BF16), ffn_cw=ffn_conv_w[l], ffn_cb=ffn_conv_b[l].reshape(1, 2 * D_FF),
        wd=w_down[l].astype(BF16), fw=final_norm_w.reshape(1, D_MODEL),
    )


def _state_to_kernel(h):
    return h.reshape(h.shape[0], SSD_GROUPS, GROUP_W, SSD_STATE)


def _state_from_kernel(h):
    return h.reshape(h.shape[0], SSD_HEADS, SSD_HEADDIM, SSD_STATE)


def _run_group(x, mod, past_k, past_v, ssm_h0, ssm_conv_hist, ffn_conv_hist, p, *, tm, ssd_rows, bq, bk):
    b, t, d = x.shape
    past = 0 if past_k is None else past_k.shape[1]
    chunk = min(CHUNK, t)
    tp = max(t, SUPER)
    if tp != t:
        x = jnp.pad(x, ((0, 0), (0, tp - t), (0, 0)))
        tm = ssd_rows = bq = tp
    mod3 = mod.reshape(b, 6, d)

    assert tm == bq
    zs, xc, dt, k, v, kb, qt, vt, conv_new = _inproj_call(
        x, mod3, p["norm_mix_w"], p["w_cat"], p["w_t"], ssm_conv_hist.astype(F32), p["cw"], p["cbias"],
        tm=tm, real=min(t, tm))

    y_ssd, h_t = _ssd_call(zs, xc, dt, _state_to_kernel(ssm_h0.astype(F32)),
                           p["dtb"], p["alog"], p["dsk"], p["ssm_nw"],
                           chunk=chunk, rows=ssd_rows, real=min(t, ssd_rows))

    if past == 0:
        assert bq == bk and t % bq == 0
        bias = _bias_tiles(p["rel_bias"], bq, (0, bq), bq, bk, 2 * bq)
        y_att = _attn_call(p["scal"], qt, kb, vt, bias, p["subw"], bq=bq, bk=bk, noff=0,
                           out_scale=1.0 - p["lam_init"])
    else:
        assert past % bk == 0 and past >= bk and bq == tp <= bk
        bias = _bias_tiles(p["rel_bias"], past, (past - bk, past), bq, bk, past + t)
        bias0 = jnp.swapaxes(bias[:, 0], 1, 2)[:, :t]
        bias1 = jnp.swapaxes(bias[:, 1, :bq], 1, 2)[:, :t]
        y_att = _attn_cached_call(p["scal"], qt, past_k, past_v, kb, vt, bias0, bias1, p["subw"].reshape(1, DA_DV),
                                  tq=t, bk=bk, out_scale=1.0 - p["lam_init"])

    y, ffn_new = _ffn_call(x, y_ssd, y_att, mod3, p["wo1"], p["wo2"], p["nfw"], p["wu"], p["ffn_cw"], p["ffn_cb"],
                           p["wd"], p["fw"], ffn_conv_hist.astype(F32), tm=tm, real=min(t, tm))
    return (y[:, :t], k[:, :t], v[:, :t],
            _state_from_kernel(h_t), conv_new, ffn_new)


def kernel(x_prompt, x_sample, c_prompt, c_sample, cache_k, cache_v, state_ssm, state_ssm_conv, state_ffn_conv, w_ada, b_ada, norm_mix_w, w_in, ssm_conv_w, ssm_conv_b, ssm_dt_bias, ssm_a_log, ssm_d, ssm_norm_w, lambda_q1, lambda_k1, lambda_q2, lambda_k2, attn_subln_w, rel_bias, w_out, norm_ffn_w, w_up, ffn_conv_w, ffn_conv_b, w_down, final_norm_w):
    bp, bs = x_prompt.shape[0], x_sample.shape[0]
    dt = x_prompt.dtype
    p = _pack_params(norm_mix_w, w_in, ssm_conv_w, ssm_conv_b, ssm_dt_bias, ssm_a_log, ssm_d, ssm_norm_w,
                     lambda_q1, lambda_k1, lambda_q2, lambda_k2, attn_subln_w, rel_bias, w_out,
                     norm_ffn_w, w_up, ffn_conv_w, ffn_conv_b, w_down, final_norm_w, 0)
    c_all = jnp.concatenate([c_prompt, c_sample], axis=0)
    npad = -c_all.shape[0] % SUBLANES
    c_all = jnp.pad(c_all, ((0, npad), (0, 0)))
    mod = _mod_call(c_all, w_ada[0], b_ada[0].reshape(1, -1))

    zeros = lambda *s: jnp.zeros(s, dt)
    out_p = _run_group(x_prompt, mod[:bp], None, None,
                       zeros(bp, SSD_HEADS, SSD_HEADDIM, SSD_STATE), zeros(bp, SSD_CONV - 1, SSD_CONV_DIM),
                       zeros(bp, FFN_CONV - 1, 2 * D_FF), p, tm=512, ssd_rows=256, bq=512, bk=512)
    out_s = _run_group(x_sample, mod[bp:bp + bs], cache_k[0], cache_v[0], state_ssm[0], state_ssm_conv[0],
                       state_ffn_conv[0], p, tm=SUPER, ssd_rows=SUPER, bq=SUPER, bk=512)
    y_p, k_p, v_p, h_p, c_p, f_p = out_p
    y_s, k_s, v_s, h_s, c_s, f_s = out_s
    return (y_p, y_s, k_p[None], v_p[None], h_p[None], c_p[None], f_p[None],
            k_s[None], v_s[None], h_s[None], c_s[None], f_s[None])
```

```python
import functools
import math

import numpy as np
import jax
import jax.numpy as jnp
from jax import lax
from jax.experimental import pallas as pl
from jax.experimental.pallas import tpu as pltpu

F32 = jnp.float32
BF16 = jnp.bfloat16
HIGHEST = lax.Precision.HIGHEST

D_MODEL = 1024
CHUNK = 64
SSD_WIDTH = 512
SSD_HEADDIM = 64
SSD_HEADS = 8
SSD_GROUPS = 2
SSD_HPG = 4
SSD_STATE = 128
SSD_CONV = 4
SSD_CONV_DIM = SSD_WIDTH + 2 * SSD_GROUPS * SSD_STATE
GROUP_W = SSD_HPG * SSD_HEADDIM
DA_WIDTH = 512
DA_DK = 64
DA_DV = 128
DA_HEADS = 4
REL_BUCKETS = 32
REL_MAX_DIST = 128
D_FF = 2816
FFN_CONV = 3
EPS = 1e-6
IN_SPLITS = (512, 1536, 1544, 2056, 2568)
LANES = 128
SUBLANES = 8
SUPER = 128
SSD_NSEQ = 4
ATT_KT = 64
FFN_CN = 256
FFN_NC = D_FF // FFN_CN
NEG = -1e30
VMEM_LIMIT = 56 * 1024 * 1024

PZ, PX, PDT, PK, PV, PEND = 0, 512, 1536, 1664, 2176, 2688
LOG2E = math.log2(math.e)


def _silu(x):
    return x / (1.0 + jnp.exp(-x))


def _softplus(x):
    return jnp.maximum(x, 0.0) + jnp.log1p(jnp.exp(-jnp.abs(x)))


def _split3(x):
    hi = x.astype(BF16)
    r1 = x - hi.astype(F32)
    mid = r1.astype(BF16)
    lo = (r1 - mid.astype(F32)).astype(BF16)
    return hi, mid, lo


def _rms(x, w):
    return x * lax.rsqrt(jnp.mean(x * x, axis=-1, keepdims=True) + EPS) * w


def _const_spec(shape):
    nd = len(shape)
    return pl.BlockSpec(shape, lambda *_: (0,) * nd)


def _mod_kernel(c_ref, w_ref, b_ref, o_ref):
    a = _silu(c_ref[...]).astype(BF16)
    o_ref[...] = jnp.dot(a, w_ref[...].astype(BF16), preferred_element_type=F32) + b_ref[...]


def _mod_call(c, w_ada, b_ada):
    n, d = c.shape
    nout = w_ada.shape[1]
    tn = 1024
    return pl.pallas_call(
        _mod_kernel,
        grid=(nout // tn,),
        in_specs=[pl.BlockSpec((n, d), lambda j: (0, 0)),
                  pl.BlockSpec((d, tn), lambda j: (0, j)),
                  pl.BlockSpec((1, tn), lambda j: (0, j))],
        out_specs=pl.BlockSpec((n, tn), lambda j: (0, j)),
        out_shape=jax.ShapeDtypeStruct((n, nout), F32),
        name="mod",
    )(c, w_ada, b_ada)


def _inproj_kernel(x_ref, mod_ref, nw_ref, w_ref, wt_ref, hist_ref, cw_ref, cbias_ref,
                   zs_ref, xc_ref, dt_ref, k_ref, v_ref, kb_ref, qt_ref, vt_ref, cout_ref, cbuf, hb_scr, zbuf,
                   *, tm, real):
    t = pl.program_id(1)
    nconv = SSD_CONV - 1

    @pl.when(t == 0)
    def _init():
        cbuf[0:SUBLANES, :] = jnp.zeros((SUBLANES, SSD_CONV_DIM), F32)
        cbuf[SUBLANES - nconv:SUBLANES, :] = hist_ref[...]

    h = _rms(x_ref[...], nw_ref[...]) * (1.0 + mod_ref[1:2, :]) + mod_ref[0:1, :]
    hb_scr[...] = h.astype(BF16)

    def proj(a, b):
        return jnp.dot(hb_scr[...], w_ref[:, a:b], preferred_element_type=F32)

    def proj_t(a, b):
        return lax.dot_general(wt_ref[a:b, :], hb_scr[...], (((1,), (1,)), ((), ())), preferred_element_type=F32)

    cbuf[SUBLANES:SUBLANES + tm, :] = proj(PX, PDT)
    zbuf[...] = proj(PZ, PX)
    dt_ref[...] = proj(PDT, PK)
    k = proj(PK, PV)
    v = proj(PV, PEND)
    for hd in range(DA_HEADS):
        k_ref[:, hd, :] = k[:, hd * DA_DV:(hd + 1) * DA_DV]
        v_ref[:, hd, :] = v[:, hd * DA_DV:(hd + 1) * DA_DV]
    kb_ref[...] = k.astype(BF16)
    qt_ref[...] = proj_t(0, DA_WIDTH).astype(BF16)
    vt_ref[...] = proj_t(DA_WIDTH, 2 * DA_WIDTH).astype(BF16)
    conv = cbias_ref[...]
    for j in range(SSD_CONV):
        off = SUBLANES - nconv + j
        conv = conv + cw_ref[j:j + 1, :] * cbuf[off:off + tm, :]
    tail = cbuf[SUBLANES - nconv + real:SUBLANES + real, :]
    cout_ref[...] = tail
    cbuf[SUBLANES - nconv:SUBLANES, :] = tail
    xc_ref[...] = _silu(conv).astype(BF16)
    zs_ref[...] = _silu(zbuf[...]).astype(BF16)


def _inproj_call(x, mod3, norm_w, w_cat, w_t, hist, cw, cbias, *, tm, real):
    b, t, d = x.shape
    nt = t // tm

    def row(width):
        return pl.BlockSpec((None, tm, width), lambda i, j: (i, j, 0))

    def out(width, dtype):
        return jax.ShapeDtypeStruct((b, t, width), dtype)

    tspec = pl.BlockSpec((None, None, DA_WIDTH, tm), lambda i, j: (i, j, 0, 0))
    tshape = jax.ShapeDtypeStruct((b, nt, DA_WIDTH, tm), BF16)
    hist_spec = pl.BlockSpec((None, SSD_CONV - 1, SSD_CONV_DIM), lambda i, j: (i, 0, 0))
    hspec = pl.BlockSpec((None, tm, DA_HEADS, DA_DV), lambda i, j: (i, j, 0, 0))
    hshape = jax.ShapeDtypeStruct((b, t, DA_HEADS, DA_DV), F32)
    return pl.pallas_call(
        functools.partial(_inproj_kernel, tm=tm, real=real),
        grid=(b, nt),
        in_specs=[row(d),
                  pl.BlockSpec((None, 6, d), lambda i, j: (i, 0, 0)),
                  _const_spec((1, d)),
                  _const_spec((d, PEND)),
                  _const_spec((2 * DA_WIDTH, d)),
                  hist_spec, _const_spec((SSD_CONV, SSD_CONV_DIM)), _const_spec((1, SSD_CONV_DIM))],
        out_specs=[row(512), row(1024), row(LANES), hspec, hspec, row(512), tspec, tspec, hist_spec],
        out_shape=[out(512, BF16), out(1024, BF16), out(LANES, F32),
                   hshape, hshape, out(512, BF16), tshape, tshape,
                   jax.ShapeDtypeStruct((b, SSD_CONV - 1, SSD_CONV_DIM), F32)],
        scratch_shapes=[pltpu.VMEM((tm + SUBLANES, SSD_CONV_DIM), F32), pltpu.VMEM((tm, d), BF16),
                        pltpu.VMEM((tm, SSD_WIDTH), F32)],
        compiler_params=pltpu.CompilerParams(dimension_semantics=("parallel", "arbitrary"),
                                             vmem_limit_bytes=VMEM_LIMIT),
        name="inproj",
    )(x, mod3, norm_w, w_cat, w_t, hist, cw, cbias)


def _ssd_kernel(zs_ref, xc_ref, dt_ref, h0_ref, dtb_ref, alog_ref, dsk_ref, nw_ref, tri_ref, e_ref,
                y_ref, hout_ref, h_scr, ybuf, *, nseq, chunk, rows, real):
    t = pl.program_id(1)
    seqs = range(nseq)

    @pl.when(t == 0)
    def _init():
        for i in seqs:
            for g in range(SSD_GROUPS):
                h_scr[i, g] = h0_ref[i, g].T

    li = lax.broadcasted_iota(jnp.int32, (SUPER, SUPER), 0)
    si = lax.broadcasted_iota(jnp.int32, (SUPER, SUPER), 1)
    cshift = chunk.bit_length() - 1
    mask2 = ((li >> cshift) == (si >> cshift)) & (si <= li)
    lane_g = lax.broadcasted_iota(jnp.int32, (SUPER, GROUP_W), 1) >> (SSD_HEADDIM.bit_length() - 1)

    pre = []
    for i in seqs:
        xs = xc_ref[i, :, 0:SSD_WIDTH].astype(F32)
        dtv = _softplus(dt_ref[i] + dtb_ref[...])
        da = dtv * (-jnp.exp(alog_ref[...]))
        acs = jnp.dot(tri_ref[...], jnp.concatenate(_split3(da), axis=0), preferred_element_type=F32)
        dt_x = jnp.dot(jnp.concatenate(_split3(dtv), axis=1), e_ref[...], preferred_element_type=F32)
        acs_x = jnp.dot(jnp.concatenate(_split3(acs), axis=1), e_ref[...], preferred_element_type=F32)
        pre.append((xs, acs, acs_x, jnp.exp(acs_x), xs * dt_x))

    for sb in range(rows // SUPER):
        o = sb * SUPER
        nreal = (min(real, o + SUPER) - o) // chunk
        acs2 = [pre[i][1][o:o + SUPER, :] for i in seqs]
        acs_t = [a.T for a in acs2]
        for g in range(SSD_GROUPS):
            gs = slice(g * GROUP_W, (g + 1) * GROUP_W)
            bcol = SSD_WIDTH + g * SSD_STATE
            ccol = SSD_WIDTH + (SSD_GROUPS + g) * SSD_STATE
            for i in seqs:
                _, _, acs_x, eacs_x, xd = pre[i]
                bmb = xc_ref[i, o:o + SUPER, bcol:bcol + SSD_STATE]
                cmb = xc_ref[i, o:o + SUPER, ccol:ccol + SSD_STATE]
                cb2 = lax.dot_general(cmb, bmb, (((1,), (1,)), ((), ())), preferred_element_type=F32)
                bm_t = bmb.astype(F32).T.astype(BF16)
                ms = []
                for rr in range(SSD_HPG):
                    r = g * SSD_HPG + rr
                    seg = acs2[i][:, r:r + 1] - acs_t[i][r:r + 1, :]
                    dec = jnp.where(mask2, jnp.exp(jnp.where(mask2, seg, 0.0)), 0.0)
                    ms.append((cb2 * dec).astype(BF16))
                full = jnp.dot(jnp.concatenate(ms, axis=0), xd[o:o + SUPER, gs].astype(BF16),
                               preferred_element_type=F32)
                ydiag = full[0:SUPER]
                for rr in range(1, SSD_HPG):
                    ydiag = jnp.where(lane_g == rr, full[rr * SUPER:(rr + 1) * SUPER], ydiag)
                ybuf[i, o:o + SUPER, gs] = ydiag
                for j in range(nreal):
                    a0, a1 = o + j * chunk, o + (j + 1) * chunk
                    h_t = h_scr[i, g]
                    yoff = jnp.dot(cmb[j * chunk:(j + 1) * chunk, :], h_t.astype(BF16), preferred_element_type=F32)
                    ybuf[i, a0:a1, gs] = ybuf[i, a0:a1, gs] + yoff * eacs_x[a0:a1, gs]
                    dte = jnp.exp(acs_x[a1 - 1:a1, gs] - acs_x[a0:a1, gs])
                    xw = (xd[a0:a1, gs] * dte).astype(BF16)
                    pieces = []
                    if j > 0:
                        pieces.append(jnp.zeros((j * chunk, GROUP_W), BF16))
                    pieces.append(xw)
                    if (j + 1) * chunk < SUPER:
                        pieces.append(jnp.zeros((SUPER - (j + 1) * chunk, GROUP_W), BF16))
                    xw2 = jnp.concatenate(pieces, axis=0) if len(pieces) > 1 else xw
                    st = jnp.dot(bm_t, xw2, preferred_element_type=F32)
                    h_scr[i, g] = h_t * eacs_x[a1 - 1:a1, gs] + st

    for i in seqs:
        y = (ybuf[i] + dsk_ref[...] * pre[i][0]) * zs_ref[i].astype(F32)
        for g in range(SSD_GROUPS):
            gs = slice(g * GROUP_W, (g + 1) * GROUP_W)
            y_ref[i, :, gs] = _rms(y[:, gs], nw_ref[:, gs]).astype(BF16)

    @pl.when(t == pl.num_programs(1) - 1)
    def _fin():
        for i in seqs:
            for g in range(SSD_GROUPS):
                hout_ref[i, g] = h_scr[i, g].T


def _ssd_call(zs, xc, dt, h0_t, dtb, alog, dsk, nw, *, chunk, rows, real):
    b, t, _ = zs.shape
    ii = np.arange(rows)
    tri = ((ii[:, None] // chunk == ii[None, :] // chunk) & (ii[None, :] <= ii[:, None])).astype(np.float32)
    e = np.zeros((LANES, SSD_WIDTH), np.float32)
    for r in range(SSD_HEADS):
        e[r, r * SSD_HEADDIM:(r + 1) * SSD_HEADDIM] = 1.0

    nseq = math.gcd(b, SSD_NSEQ)

    def row(width):
        return pl.BlockSpec((nseq, rows, width), lambda i, j: (i, j, 0))

    state_spec = pl.BlockSpec((nseq, SSD_GROUPS, GROUP_W, SSD_STATE), lambda i, j: (i, 0, 0, 0))
    kern = functools.partial(_ssd_kernel, nseq=nseq, chunk=chunk, rows=rows, real=real)
    return pl.pallas_call(
        kern,
        grid=(b // nseq, t // rows),
        in_specs=[row(SSD_WIDTH), row(SSD_CONV_DIM), row(LANES), state_spec,
                  _const_spec((1, LANES)), _const_spec((1, LANES)),
                  _const_spec((1, SSD_WIDTH)), _const_spec((1, SSD_WIDTH)),
                  _const_spec((rows, 3 * rows)), _const_spec((3 * LANES, SSD_WIDTH))],
        out_specs=[row(SSD_WIDTH), state_spec],
        out_shape=[jax.ShapeDtypeStruct((b, t, SSD_WIDTH), BF16),
                   jax.ShapeDtypeStruct((b, SSD_GROUPS, GROUP_W, SSD_STATE), F32)],
        scratch_shapes=[pltpu.VMEM((nseq, SSD_GROUPS, SSD_STATE, GROUP_W), F32),
                        pltpu.VMEM((nseq, rows, SSD_WIDTH), F32)],
        compiler_params=pltpu.CompilerParams(dimension_semantics=("parallel", "arbitrary"),
                                             vmem_limit_bytes=VMEM_LIMIT),
        name="ssd",
    )(zs, xc, dt, h0_t, dtb, alog, dsk, nw,
      jnp.asarray(np.tile(tri, (1, 3)), BF16), jnp.asarray(np.tile(e, (3, 1)), BF16))


def _attn_kernel(scal_ref, qt_ref, k_ref, vt_ref, btab_ref, subw_ref, o_ref,
                 m_scr, l_scr, acc_scr, sa_scr, sb_scr, sc_scr, pa_scr, pb_scr, alpha_scr, qz_scr, bias_scr,
                 *, bq, bk, noff, out_scale):
    h = pl.program_id(1)
    qi = pl.program_id(2)
    kn0 = qi + (noff - 1)
    lam = scal_ref[0]
    cfar = scal_ref[1 + h]

    @pl.when(qi == 0)
    def _build_bias_tiles():
        width = bq + bk
        kj = lax.broadcasted_iota(jnp.int32, (bk, bq), 0)
        qj = lax.broadcasted_iota(jnp.int32, (bk, bq), 1)
        cshift = CHUNK.bit_length() - 1
        for d in range(2):
            skew = pltpu.roll(jnp.broadcast_to(btab_ref[d], (bk, width)), 0, 1, stride=1, stride_axis=0)
            tile = skew[:, bk:width]
            if d == 1:
                tile = jnp.where((kj >> cshift) <= (qj >> cshift), tile, NEG)
            bias_scr[d] = tile

    zero = jnp.zeros((DA_DK, bq), BF16)
    qz_scr[0, 0:DA_DK, :] = qt_ref[0:DA_DK, :]
    qz_scr[0, DA_DK:DA_DV, :] = zero
    qz_scr[1, 0:DA_DK, :] = zero
    qz_scr[1, DA_DK:DA_DV, :] = qt_ref[DA_DK:DA_DV, :]

    m_scr[...] = jnp.full(m_scr.shape, NEG, F32)
    l_scr[...] = jnp.zeros(l_scr.shape, F32)
    acc_scr[...] = jnp.zeros(acc_scr.shape, F32)

    nsub = bk // ATT_KT

    def fold(x):
        return x.reshape(ATT_KT // SUBLANES, SUBLANES, bq)

    def scores(s_buf, first, count):
        for mm in range(2):
            for e in range(count):
                start = pl.multiple_of((first + e) * bk, bk)
                s_buf[mm, e] = jnp.dot(k_ref[pl.ds(start, bk), :], qz_scr[mm], preferred_element_type=F32)

    def softmax(s_buf, p_buf, entries):
        alphas = []
        for mm in range(2):
            cand = None
            for e, (near, shift) in enumerate(entries):
                mx = None
                for t in range(nsub):
                    rows = slice(t * ATT_KT, (t + 1) * ATT_KT)
                    s = s_buf[mm, e, rows, :]
                    if near is not None:
                        s = s + bias_scr[near, rows, :]
                    pm = jnp.max(fold(s), axis=0)
                    mx = pm if mx is None else jnp.maximum(mx, pm)
                mx = jnp.max(mx, axis=0, keepdims=True) + shift
                cand = mx if cand is None else jnp.maximum(cand, mx)
            m_old = m_scr[mm]
            m_new = jnp.maximum(m_old, cand)
            ls = None
            for e, (near, shift) in enumerate(entries):
                off = m_new - shift
                for t in range(nsub):
                    rows = slice(t * ATT_KT, (t + 1) * ATT_KT)
                    s = s_buf[mm, e, rows, :]
                    if near is not None:
                        s = s + bias_scr[near, rows, :]
                    p = jnp.exp2(s - off)
                    p_buf[mm, e * bk + t * ATT_KT:e * bk + (t + 1) * ATT_KT, :] = p.astype(BF16)
                    ps = jnp.sum(fold(p), axis=0)
                    ls = ps if ls is None else ls + ps
            alpha = jnp.exp2(m_old - m_new)
            l_scr[mm] = alpha * l_scr[mm] + jnp.sum(ls, axis=0, keepdims=True)
            m_scr[mm] = m_new
            alphas.append(alpha)
        return alphas

    def pv(p_buf, first, n):
        vts = [vt_ref[first + e] for e in range(n)]
        vt = jnp.concatenate(vts, axis=1) if n > 1 else vts[0]
        return [jnp.dot(vt, p_buf[mm, 0:n * bk, :], preferred_element_type=F32) for mm in range(2)]

    def accumulate(alphas, pvs):
        for mm in range(2):
            acc_scr[mm] = alphas[mm] * acc_scr[mm] + pvs[mm]

    def softmax_pv(s_buf, first, entries):
        alphas = softmax(s_buf, pa_scr, entries)
        accumulate(alphas, pv(pa_scr, first, len(entries)))

    nfar = jnp.maximum(kn0, 0)
    odd = nfar % 2
    far = (None, cfar)

    near_pair = [(0, 0.0), (1, 0.0)]

    @pl.when(kn0 < 0)
    def _only_first():
        scores(sc_scr, 0, 1)
        softmax_pv(sc_scr, 0, [(1, 0.0)])

    @pl.when(kn0 >= 0)
    def _groups():
        @pl.when(odd == 1)
        def _single():
            scores(sc_scr, 0, 1)
            scores(sa_scr, 1, 2)
            softmax_pv(sc_scr, 0, [far])

        @pl.when(odd == 0)
        def _first_pair():
            scores(sa_scr, 0, 2)

        def far_pair(s_cur, s_next, cur):
            scores(s_next, cur + 2, 2)
            softmax_pv(s_cur, cur, [far, far])

        def far_step(s_cur, p_cur, s_next, p_prev, cur):
            scores(s_next, cur + 2, 2)
            pending = None if p_prev is None else pv(p_prev, cur - 2, 2)
            alphas = softmax(s_cur, p_cur, [far, far])
            if pending is not None:
                accumulate([alpha_scr[0], alpha_scr[1]], pending)
            for mm in range(2):
                alpha_scr[mm] = alphas[mm]

        npairs = nfar // 2
        niter = npairs // 2

        @pl.when(niter >= 1)
        def _far_loop():
            far_step(sa_scr, pa_scr, sb_scr, None, odd)
            far_step(sb_scr, pb_scr, sa_scr, pa_scr, odd + 2)

            def far_body(j, carry):
                cur = odd + 4 * j
                far_step(sa_scr, pa_scr, sb_scr, pb_scr, cur)
                far_step(sb_scr, pb_scr, sa_scr, pa_scr, cur + 2)
                return carry

            lax.fori_loop(1, niter, far_body, 0)
            accumulate([alpha_scr[0], alpha_scr[1]], pv(pb_scr, odd + 4 * niter - 2, 2))

        @pl.when(npairs % 2 == 1)
        def _tail_b():
            far_pair(sa_scr, sb_scr, kn0 - 2)
            softmax_pv(sb_scr, kn0, near_pair)

        @pl.when(npairs % 2 == 0)
        def _tail_a():
            softmax_pv(sa_scr, kn0, near_pair)

    o = acc_scr[0] * (1.0 / l_scr[0]) - acc_scr[1] * (lam / l_scr[1])
    o = o * lax.rsqrt(jnp.mean(o * o, axis=0, keepdims=True) + EPS) * (subw_ref[...] * out_scale)
    o_ref[...] = o.T.astype(BF16)


def _attn_call(scal, qt, kb, vt, btab, subw, *, bq, bk, noff, out_scale):
    b, nq = qt.shape[:2]
    tk = kb.shape[1]
    nkb = vt.shape[1]
    kern = functools.partial(_attn_kernel, bq=bq, bk=bk, noff=noff, out_scale=out_scale)
    return pl.pallas_call(
        kern,
        grid=(b, DA_HEADS, nq),
        in_specs=[pl.BlockSpec(memory_space=pltpu.SMEM),
                  pl.BlockSpec((None, None, DA_DV, bq), lambda i, h, j: (i, j, h, 0)),
                  pl.BlockSpec((None, tk, DA_DV), lambda i, h, j: (i, 0, h)),
                  pl.BlockSpec((None, nkb, DA_DV, bk), lambda i, h, j: (i, 0, h, 0)),
                  pl.BlockSpec((None, 2, 1, bq + bk), lambda i, h, j: (h, 0, 0, 0)),
                  pl.BlockSpec((DA_DV, 1), lambda i, h, j: (0, 0))],
        out_specs=pl.BlockSpec((None, bq, DA_DV), lambda i, h, j: (i, j, h)),
        out_shape=jax.ShapeDtypeStruct((b, nq * bq, DA_WIDTH), BF16),
        scratch_shapes=[pltpu.VMEM((2, 1, bq), F32), pltpu.VMEM((2, 1, bq), F32),
                        pltpu.VMEM((2, DA_DV, bq), F32),
                        pltpu.VMEM((2, 2, bk, bq), F32), pltpu.VMEM((2, 2, bk, bq), F32),
                        pltpu.VMEM((2, 1, bk, bq), F32),
                        pltpu.VMEM((2, 2 * bk, bq), BF16), pltpu.VMEM((2, 2 * bk, bq), BF16),
                        pltpu.VMEM((2, 1, bq), F32),
                        pltpu.VMEM((2, DA_DV, bq), BF16),
                        pltpu.VMEM((2, bk, bq), F32)],
        compiler_params=pltpu.CompilerParams(dimension_semantics=("parallel", "parallel", "arbitrary"),
                                             vmem_limit_bytes=VMEM_LIMIT),
        name="attn",
    )(scal, qt, kb, vt, btab, subw)


def _attn_cached_kernel(scal_ref, qt_ref, kc_ref, vc_ref, kn_ref, vtn_ref, bias0_ref, bias1_ref, subw_ref, o_ref,
                        *, bq, tq, bk, out_scale):
    lam = scal_ref[0]
    past = kc_ref.shape[0] // DA_HEADS
    lane = lax.broadcasted_iota(jnp.int32, (tq, DA_DV), 1)
    nt = (((1,), (1,)), ((), ()))

    for h in range(DA_HEADS):
        hs = slice(h * DA_DV, (h + 1) * DA_DV)
        cfar = scal_ref[1 + h]
        qn = qt_ref[hs, :].astype(F32).T[0:tq, :]
        q2 = jnp.concatenate([jnp.where(lane < DA_DK, qn, 0.0), jnp.where(lane >= DA_DK, qn, 0.0)],
                             axis=0).astype(BF16)
        head_rows = pl.ds(h, past, stride=DA_HEADS)
        s_c = lax.dot_general(q2, kc_ref[head_rows, :].astype(BF16), nt, preferred_element_type=F32)
        s_n = lax.dot_general(q2, kn_ref[:, hs], nt, preferred_element_type=F32)
        b0 = bias0_ref[h]
        b1 = bias1_ref[h]
        s = jnp.concatenate([s_c[:, 0:past - bk] + cfar,
                             s_c[:, past - bk:past] + jnp.concatenate([b0, b0], axis=0),
                             s_n + jnp.concatenate([b1, b1], axis=0)], axis=1)
        p = jnp.exp2(s - jnp.max(s, axis=1, keepdims=True))
        inv = 1.0 / jnp.sum(p, axis=1, keepdims=True)
        pb = p.astype(BF16)
        acc = (jnp.dot(pb[:, 0:past], vc_ref[head_rows, :].astype(BF16), preferred_element_type=F32)
               + jnp.dot(pb[:, past:past + bq], vtn_ref[hs, :].astype(F32).T.astype(BF16),
                         preferred_element_type=F32))
        o = acc[0:tq, :] * inv[0:tq] - acc[tq:2 * tq, :] * (lam * inv[tq:2 * tq])
        o_ref[0:tq, hs] = (_rms(o, subw_ref[...]) * out_scale).astype(BF16)
        o_ref[tq:bq, hs] = jnp.zeros((bq - tq, DA_DV), BF16)


def _attn_cached_call(scal, qt, cache_k, cache_v, kb, vt, bias0, bias1, subw, *, tq, bk, out_scale):
    b, past = cache_k.shape[:2]
    bq = qt.shape[-1]
    rows = past * DA_HEADS
    kern = functools.partial(_attn_cached_kernel, bq=bq, tq=tq, bk=bk, out_scale=out_scale)
    cache_spec = pl.BlockSpec((None, rows, DA_DV), lambda i: (i, 0, 0))
    return pl.pallas_call(
        kern,
        grid=(b,),
        in_specs=[pl.BlockSpec(memory_space=pltpu.SMEM),
                  pl.BlockSpec((None, None, DA_WIDTH, bq), lambda i: (i, 0, 0, 0)),
                  cache_spec, cache_spec,
                  pl.BlockSpec((None, bq, DA_WIDTH), lambda i: (i, 0, 0)),
                  pl.BlockSpec((None, None, DA_WIDTH, bq), lambda i: (i, 0, 0, 0)),
                  _const_spec((DA_HEADS, tq, bk)), _const_spec((DA_HEADS, tq, bq)),
                  _const_spec((1, DA_DV))],
        out_specs=pl.BlockSpec((None, bq, DA_WIDTH), lambda i: (i, 0, 0)),
        out_shape=jax.ShapeDtypeStruct((b, bq, DA_WIDTH), BF16),
        compiler_params=pltpu.CompilerParams(dimension_semantics=("parallel",), vmem_limit_bytes=VMEM_LIMIT),
        name="attn_cached",
    )(scal, qt, cache_k.reshape(b, rows, DA_DV), cache_v.reshape(b, rows, DA_DV), kb, vt, bias0, bias1, subw)


def _rel_bucket(rel):
    nb = REL_BUCKETS // 2
    max_exact = nb // 2
    n = jnp.abs(rel)
    nf = jnp.maximum(n, 1).astype(jnp.float32)
    large = max_exact + (jnp.log(nf / max_exact) / math.log(REL_MAX_DIST / max_exact)
                         * (nb - max_exact)).astype(jnp.int32)
    large = jnp.minimum(large, nb - 1)
    return jnp.where(rel > 0, nb, 0) + jnp.where(n < max_exact, n, large)


def _bias_tables(rel_bias, bq, bk):
    tabs = []
    for d in range(2):
        offs = (d - 1) * bk + bk - np.arange(bq + bk)
        tabs.append(rel_bias[_rel_bucket(jnp.asarray(offs, jnp.int32))].astype(F32).T * LOG2E)
    return jnp.stack(tabs, axis=1)[:, :, None, :]


def _bias_rows(rel_bias, qpos0, tq, kpos0, nkeys, tk_real):
    offs = (kpos0 - qpos0) + np.arange(-(tq - 1), nkeys)
    table = rel_bias[_rel_bucket(jnp.asarray(offs, jnp.int32))].astype(F32).T * LOG2E
    rows = jnp.stack([table[:, tq - 1 - i:tq - 1 - i + nkeys] for i in range(tq)], axis=1)
    qpos = qpos0 + np.arange(tq)
    kpos = kpos0 + np.arange(nkeys)
    vis = (kpos[None, :] // CHUNK <= qpos[:, None] // CHUNK) & (kpos[None, :] < tk_real)
    return jnp.where(jnp.asarray(vis)[None], rows, NEG)


def _ffn_kernel(x_ref, ys_ref, ya_ref, mod_ref, wo1_ref, wo2_ref, nfw_ref, wu_ref, cw_ref, cb_ref, wd_ref, fw_ref,
                hist_ref, y_ref, tail_ref, tail_scr, buf_a, buf_b, x1_scr, h2_scr, act_scr, *, tm, real):
    t = pl.program_id(1)
    nh = FFN_CONV - 1
    lo = SUBLANES - nh

    @pl.when(t == 0)
    def _init():
        tail_scr[lo:SUBLANES, :] = hist_ref[...]

    mix = (jnp.dot(ys_ref[...], wo1_ref[...], preferred_element_type=F32)
           + jnp.dot(ya_ref[...], wo2_ref[...], preferred_element_type=F32))
    x1 = x_ref[...] + mod_ref[2:3, :] * mix
    x1_scr[...] = x1
    h2 = _rms(x1, nfw_ref[...]) * (1.0 + mod_ref[4:5, :]) + mod_ref[3:4, :]
    h2_scr[...] = h2.astype(BF16)

    def cols(j, half):
        return slice(half * D_FF + j * FFN_CN, half * D_FF + (j + 1) * FFN_CN)

    def up(j, buf):
        for half in range(2):
            buf[half, SUBLANES:SUBLANES + tm, :] = jnp.dot(h2_scr[...], wu_ref[:, cols(j, half)],
                                                            preferred_element_type=F32)

    def conv(j, buf, half):
        cs = cols(j, half)
        buf[half, lo:SUBLANES, :] = tail_scr[lo:SUBLANES, cs]
        c = cb_ref[:, cs]
        for i in range(FFN_CONV):
            c = c + cw_ref[i:i + 1, cs] * buf[half, lo + i:lo + i + tm, :]
        tail_scr[lo:SUBLANES, cs] = buf[half, lo + real:SUBLANES + real, :]
        return c

    bufs = (buf_a, buf_b)
    up(0, bufs[0])
    for j in range(FFN_NC):
        if j + 1 < FFN_NC:
            up(j + 1, bufs[(j + 1) % 2])
        cv = conv(j, bufs[j % 2], 0)
        cg = conv(j, bufs[j % 2], 1)
        act_scr[:, j * FFN_CN:(j + 1) * FFN_CN] = (_silu(cg) * cv).astype(BF16)

    f = jnp.dot(act_scr[...], wd_ref[...], preferred_element_type=F32)
    x2 = x1_scr[...] + mod_ref[5:6, :] * f
    y_ref[...] = _rms(x2, fw_ref[...])
    tail_ref[...] = tail_scr[lo:SUBLANES, :]


def _ffn_call(x, ys, ya, mod3, wo1, wo2, nfw, wu, cw, cb, wd, fw, hist, *, tm, real):
    b, t, d = x.shape
    nh = FFN_CONV - 1

    def row(width):
        return pl.BlockSpec((None, tm, width), lambda i, j: (i, j, 0))

    hist_spec = pl.BlockSpec((None, nh, 2 * D_FF), lambda i, j: (i, 0, 0))
    kern = functools.partial(_ffn_kernel, tm=tm, real=real)
    return pl.pallas_call(
        kern,
        grid=(b, t // tm),
        in_specs=[row(d), row(SSD_WIDTH), row(DA_WIDTH),
                  pl.BlockSpec((None, 6, d), lambda i, j: (i, 0, 0)),
                  _const_spec((SSD_WIDTH, d)), _const_spec((DA_WIDTH, d)), _const_spec((1, d)),
                  _const_spec((d, 2 * D_FF)), _const_spec((FFN_CONV, 2 * D_FF)), _const_spec((1, 2 * D_FF)),
                  _const_spec((D_FF, d)), _const_spec((1, d)),
                  hist_spec],
        out_specs=[row(d), hist_spec],
        out_shape=[jax.ShapeDtypeStruct((b, t, d), F32),
                   jax.ShapeDtypeStruct((b, nh, 2 * D_FF), F32)],
        scratch_shapes=[pltpu.VMEM((SUBLANES, 2 * D_FF), F32),
                        pltpu.VMEM((2, tm + SUBLANES, FFN_CN), F32), pltpu.VMEM((2, tm + SUBLANES, FFN_CN), F32),
                        pltpu.VMEM((tm, d), F32), pltpu.VMEM((tm, d), BF16), pltpu.VMEM((tm, D_FF), BF16)],
        compiler_params=pltpu.CompilerParams(dimension_semantics=("parallel", "arbitrary"),
                                             vmem_limit_bytes=VMEM_LIMIT),
        name="ffn",
    )(x, ys, ya, mod3, wo1, wo2, nfw, wu, cw, cb, wd, fw, hist)


def _pack_params(norm_mix_w, w_in, ssm_conv_w, ssm_conv_b, ssm_dt_bias, ssm_a_log, ssm_d, ssm_norm_w,
                 lambda_q1, lambda_k1, lambda_q2, lambda_k2, attn_subln_w, rel_bias, w_out,
                 norm_ffn_w, w_up, ffn_conv_w, ffn_conv_b, w_down, final_norm_w, layer):
    l = layer
    wz, wx, wdt, wq, wk, wv = jnp.split(w_in[l], IN_SPLITS, axis=-1)
    wdt = jnp.pad(wdt, ((0, 0), (0, LANES - SSD_HEADS)))
    w_cat = jnp.concatenate([wz, wx, wdt, wk, wv], axis=-1).astype(BF16)
    w_t = jnp.concatenate([wq * (DA_DK ** -0.5 * LOG2E), wv], axis=-1).T.astype(BF16)

    def pad_heads(v):
        return jnp.pad(v.astype(F32), (0, LANES - SSD_HEADS)).reshape(1, LANES)

    lam_init = 0.8 - 0.6 * math.exp(-0.3 * l)
    lam = (jnp.exp(jnp.sum(lambda_q1[l].astype(F32) * lambda_k1[l].astype(F32)))
           - jnp.exp(jnp.sum(lambda_q2[l].astype(F32) * lambda_k2[l].astype(F32))) + lam_init)
    far_bias = rel_bias[REL_BUCKETS // 2 - 1].astype(F32)
    return dict(
        norm_mix_w=norm_mix_w[l].reshape(1, D_MODEL), w_cat=w_cat, w_t=w_t,
        cw=ssm_conv_w[l], cbias=ssm_conv_b[l].reshape(1, SSD_CONV_DIM),
        dtb=pad_heads(ssm_dt_bias[l]), alog=pad_heads(ssm_a_log[l]),
        dsk=jnp.repeat(ssm_d[l].astype(F32), SSD_HEADDIM).reshape(1, SSD_WIDTH),
        ssm_nw=ssm_norm_w[l].reshape(1, SSD_WIDTH),
        scal=jnp.concatenate([lam.reshape(1), far_bias * LOG2E]).astype(F32), lam_init=lam_init,
        subw=attn_subln_w[l].reshape(DA_DV, 1), rel_bias=rel_bias,
        wo1=w_out[l][:SSD_WIDTH].astype(BF16), wo2=w_out[l][SSD_WIDTH:].astype(BF16),
        nfw=norm_ffn_w[l].reshape(1, D_MODEL),
        wu=w_up[l].astype(BF16), ffn_cw=ffn_conv_w[l], ffn_cb=ffn_conv_b[l].reshape(1, 2 * D_FF),
        wd=w_down[l].astype(BF16), fw=final_norm_w.reshape(1, D_MODEL),
    )


def _state_to_kernel(h):
    return h.reshape(h.shape[0], SSD_GROUPS, GROUP_W, SSD_STATE)


def _state_from_kernel(h):
    return h.reshape(h.shape[0], SSD_HEADS, SSD_HEADDIM, SSD_STATE)


def _run_group(x, mod, past_k, past_v, ssm_h0, ssm_conv_hist, ffn_conv_hist, p, *, tm, ssd_rows, bq, bk):
    b, t, d = x.shape
    past = 0 if past_k is None else past_k.shape[1]
    chunk = min(CHUNK, t)
    tp = max(t, SUPER)
    if tp != t:
        x = jnp.pad(x, ((0, 0), (0, tp - t), (0, 0)))
        tm = ssd_rows = bq = tp
    mod3 = mod.reshape(b, 6, d)

    assert tm == bq
    zs, xc, dt, k, v, kb, qt, vt, conv_new = _inproj_call(
        x, mod3, p["norm_mix_w"], p["w_cat"], p["w_t"], ssm_conv_hist.astype(F32), p["cw"], p["cbias"],
        tm=tm, real=min(t, tm))

    y_ssd, h_t = _ssd_call(zs, xc, dt, _state_to_kernel(ssm_h0.astype(F32)),
                           p["dtb"], p["alog"], p["dsk"], p["ssm_nw"],
                           chunk=chunk, rows=ssd_rows, real=min(t, ssd_rows))

    if past == 0:
        assert bq == bk and t % bq == 0
        y_att = _attn_call(p["scal"], qt, kb, vt, _bias_tables(p["rel_bias"], bq, bk), p["subw"], bq=bq, bk=bk, noff=0,
                           out_scale=1.0 - p["lam_init"])
    else:
        assert past % bk == 0 and past >= bk and bq == tp <= bk
        bias0 = _bias_rows(p["rel_bias"], past, t, past - bk, bk, past + t)
        bias1 = _bias_rows(p["rel_bias"], past, t, past, bq, past + t)
        y_att = _attn_cached_call(p["scal"], qt, past_k, past_v, kb, vt, bias0, bias1, p["subw"].reshape(1, DA_DV),
                                  tq=t, bk=bk, out_scale=1.0 - p["lam_init"])

    y, ffn_new = _ffn_call(x, y_ssd, y_att, mod3, p["wo1"], p["wo2"], p["nfw"], p["wu"], p["ffn_cw"], p["ffn_cb"],
                           p["wd"], p["fw"], ffn_conv_hist.astype(F32), tm=tm, real=min(t, tm))
    return (y[:, :t], k[:, :t], v[:, :t],
            _state_from_kernel(h_t), conv_new, ffn_new)


def kernel(x_prompt, x_sample, c_prompt, c_sample, cache_k, cache_v, state_ssm, state_ssm_conv, state_ffn_conv, w_ada, b_ada, norm_mix_w, w_in, ssm_conv_w, ssm_conv_b, ssm_dt_bias, ssm_a_log, ssm_d, ssm_norm_w, lambda_q1, lambda_k1, lambda_q2, lambda_k2, attn_subln_w, rel_bias, w_out, norm_ffn_w, w_up, ffn_conv_w, ffn_conv_b, w_down, final_norm_w):
    bp, bs = x_prompt.shape[0], x_sample.shape[0]
    dt = x_prompt.dtype
    p = _pack_params(norm_mix_w, w_in, ssm_conv_w, ssm_conv_b, ssm_dt_bias, ssm_a_log, ssm_d, ssm_norm_w,
                     lambda_q1, lambda_k1, lambda_q2, lambda_k2, attn_subln_w, rel_bias, w_out,
                     norm_ffn_w, w_up, ffn_conv_w, ffn_conv_b, w_down, final_norm_w, 0)
    c_all = jnp.concatenate([c_prompt, c_sample], axis=0)
    npad = -c_all.shape[0] % SUBLANES
    c_all = jnp.pad(c_all, ((0, npad), (0, 0)))
    mod = _mod_call(c_all, w_ada[0], b_ada[0].reshape(1, -1))

    zeros = lambda *s: jnp.zeros(s, dt)
    out_p = _run_group(x_prompt, mod[:bp], None, None,
                       zeros(bp, SSD_HEADS, SSD_HEADDIM, SSD_STATE), zeros(bp, SSD_CONV - 1, SSD_CONV_DIM),
                       zeros(bp, FFN_CONV - 1, 2 * D_FF), p, tm=512, ssd_rows=256, bq=512, bk=512)
    out_s = _run_group(x_sample, mod[bp:bp + bs], cache_k[0], cache_v[0], state_ssm[0], state_ssm_conv[0],
                       state_ffn_conv[0], p, tm=SUPER, ssd_rows=SUPER, bq=SUPER, bk=512)
    y_p, k_p, v_p, h_p, c_p, f_p = out_p
    y_s, k_s, v_s, h_s, c_s, f_s = out_s
    return (y_p, y_s, k_p[None], v_p[None], h_p[None], c_p[None], f_p[None],
            k_s[None], v_s[None], h_s[None], c_s[None], f_s[None])
```

```python
import functools
import math

import numpy as np
import jax
import jax.numpy as jnp
from jax import lax
from jax.experimental import pallas as pl
from jax.experimental.pallas import tpu as pltpu

F32 = jnp.float32
BF16 = jnp.bfloat16
HIGHEST = lax.Precision.HIGHEST

D_MODEL = 1024
CHUNK = 64
SSD_WIDTH = 512
SSD_HEADDIM = 64
SSD_HEADS = 8
SSD_GROUPS = 2
SSD_HPG = 4
SSD_STATE = 128
SSD_CONV = 4
SSD_CONV_DIM = SSD_WIDTH + 2 * SSD_GROUPS * SSD_STATE
GROUP_W = SSD_HPG * SSD_HEADDIM
DA_WIDTH = 512
DA_DK = 64
DA_DV = 128
DA_HEADS = 4
REL_BUCKETS = 32
REL_MAX_DIST = 128
D_FF = 2816
FFN_CONV = 3
EPS = 1e-6
IN_SPLITS = (512, 1536, 1544, 2056, 2568)
LANES = 128
SUBLANES = 8
SUPER = 128
SSD_NSEQ = 4
ATT_KT = 64
FFN_CN = 256
FFN_NC = D_FF // FFN_CN
NEG = -1e30
VMEM_LIMIT = 56 * 1024 * 1024

PZ, PX, PDT, PK, PV, PEND = 0, 512, 1536, 1664, 2176, 2688
LOG2E = math.log2(math.e)


def _silu(x):
    return x / (1.0 + jnp.exp(-x))


def _softplus(x):
    return jnp.maximum(x, 0.0) + jnp.log1p(jnp.exp(-jnp.abs(x)))


def _split3(x):
    hi = x.astype(BF16)
    r1 = x - hi.astype(F32)
    mid = r1.astype(BF16)
    lo = (r1 - mid.astype(F32)).astype(BF16)
    return hi, mid, lo


def _rms(x, w):
    return x * lax.rsqrt(jnp.mean(x * x, axis=-1, keepdims=True) + EPS) * w


def _const_spec(shape):
    nd = len(shape)
    return pl.BlockSpec(shape, lambda *_: (0,) * nd)


def _mod_kernel(c_ref, w_ref, b_ref, o_ref):
    a = _silu(c_ref[...]).astype(BF16)
    o_ref[...] = jnp.dot(a, w_ref[...].astype(BF16), preferred_element_type=F32) + b_ref[...]


def _mod_call(c, w_ada, b_ada):
    n, d = c.shape
    nout = w_ada.shape[1]
    tn = 1024
    return pl.pallas_call(
        _mod_kernel,
        grid=(nout // tn,),
        in_specs=[pl.BlockSpec((n, d), lambda j: (0, 0)),
                  pl.BlockSpec((d, tn), lambda j: (0, j)),
                  pl.BlockSpec((1, tn), lambda j: (0, j))],
        out_specs=pl.BlockSpec((n, tn), lambda j: (0, j)),
        out_shape=jax.ShapeDtypeStruct((n, nout), F32),
        name="mod",
    )(c, w_ada, b_ada)


def _inproj_kernel(x_ref, mod_ref, nw_ref, w_ref, wt_ref, hist_ref, cw_ref, cbias_ref,
                   zs_ref, xc_ref, dt_ref, k_ref, v_ref, kb_ref, qt_ref, vt_ref, cout_ref, cbuf, hb_scr, zbuf,
                   *, tm, real):
    t = pl.program_id(1)
    nconv = SSD_CONV - 1

    @pl.when(t == 0)
    def _init():
        cbuf[0:SUBLANES, :] = jnp.zeros((SUBLANES, SSD_CONV_DIM), F32)
        cbuf[SUBLANES - nconv:SUBLANES, :] = hist_ref[...]

    h = _rms(x_ref[...], nw_ref[...]) * (1.0 + mod_ref[1:2, :]) + mod_ref[0:1, :]
    hb_scr[...] = h.astype(BF16)

    def proj(a, b):
        return jnp.dot(hb_scr[...], w_ref[:, a:b], preferred_element_type=F32)

    def proj_t(a, b):
        return lax.dot_general(wt_ref[a:b, :], hb_scr[...], (((1,), (1,)), ((), ())), preferred_element_type=F32)

    cbuf[SUBLANES:SUBLANES + tm, :] = proj(PX, PDT)
    zbuf[...] = proj(PZ, PX)
    dt_ref[...] = proj(PDT, PK)
    k = proj(PK, PV)
    v = proj(PV, PEND)
    for hd in range(DA_HEADS):
        dst = pl.ds(hd, tm, stride=DA_HEADS)
        k_ref[dst, :] = k[:, hd * DA_DV:(hd + 1) * DA_DV]
        v_ref[dst, :] = v[:, hd * DA_DV:(hd + 1) * DA_DV]
    kb_ref[...] = k.astype(BF16)
    qt_ref[...] = proj_t(0, DA_WIDTH).astype(BF16)
    vt_ref[...] = proj_t(DA_WIDTH, 2 * DA_WIDTH).astype(BF16)
    conv = cbias_ref[...]
    for j in range(SSD_CONV):
        off = SUBLANES - nconv + j
        conv = conv + cw_ref[j:j + 1, :] * cbuf[off:off + tm, :]
    tail = cbuf[SUBLANES - nconv + real:SUBLANES + real, :]
    cout_ref[...] = tail
    cbuf[SUBLANES - nconv:SUBLANES, :] = tail
    xc_ref[...] = _silu(conv).astype(BF16)
    zs_ref[...] = _silu(zbuf[...]).astype(BF16)


def _inproj_call(x, mod3, norm_w, w_cat, w_t, hist, cw, cbias, *, tm, real):
    b, t, d = x.shape
    nt = t // tm

    def row(width):
        return pl.BlockSpec((None, tm, width), lambda i, j: (i, j, 0))

    def out(width, dtype):
        return jax.ShapeDtypeStruct((b, t, width), dtype)

    tspec = pl.BlockSpec((None, None, DA_WIDTH, tm), lambda i, j: (i, j, 0, 0))
    tshape = jax.ShapeDtypeStruct((b, nt, DA_WIDTH, tm), BF16)
    hist_spec = pl.BlockSpec((None, SSD_CONV - 1, SSD_CONV_DIM), lambda i, j: (i, 0, 0))
    hspec = pl.BlockSpec((None, tm * DA_HEADS, DA_DV), lambda i, j: (i, j, 0))
    hshape = jax.ShapeDtypeStruct((b, t * DA_HEADS, DA_DV), F32)
    return pl.pallas_call(
        functools.partial(_inproj_kernel, tm=tm, real=real),
        grid=(b, nt),
        in_specs=[row(d),
                  pl.BlockSpec((None, 6, d), lambda i, j: (i, 0, 0)),
                  _const_spec((1, d)),
                  _const_spec((d, PEND)),
                  _const_spec((2 * DA_WIDTH, d)),
                  hist_spec, _const_spec((SSD_CONV, SSD_CONV_DIM)), _const_spec((1, SSD_CONV_DIM))],
        out_specs=[row(512), row(1024), row(LANES), hspec, hspec, row(512), tspec, tspec, hist_spec],
        out_shape=[out(512, BF16), out(1024, BF16), out(LANES, F32),
                   hshape, hshape, out(512, BF16), tshape, tshape,
                   jax.ShapeDtypeStruct((b, SSD_CONV - 1, SSD_CONV_DIM), F32)],
        scratch_shapes=[pltpu.VMEM((tm + SUBLANES, SSD_CONV_DIM), F32), pltpu.VMEM((tm, d), BF16),
                        pltpu.VMEM((tm, SSD_WIDTH), F32)],
        compiler_params=pltpu.CompilerParams(dimension_semantics=("parallel", "arbitrary"),
                                             vmem_limit_bytes=VMEM_LIMIT),
        name="inproj",
    )(x, mod3, norm_w, w_cat, w_t, hist, cw, cbias)


def _ssd_kernel(zs_ref, xc_ref, dt_ref, h0_ref, dtb_ref, alog_ref, dsk_ref, nw_ref, tri_ref, e_ref,
                y_ref, hout_ref, h_scr, ybuf, *, nseq, chunk, rows, real):
    t = pl.program_id(1)
    seqs = range(nseq)

    @pl.when(t == 0)
    def _init():
        for i in seqs:
            for g in range(SSD_GROUPS):
                h_scr[i, g] = h0_ref[i, g].T

    li = lax.broadcasted_iota(jnp.int32, (SUPER, SUPER), 0)
    si = lax.broadcasted_iota(jnp.int32, (SUPER, SUPER), 1)
    cshift = chunk.bit_length() - 1
    mask2 = ((li >> cshift) == (si >> cshift)) & (si <= li)
    lane_g = lax.broadcasted_iota(jnp.int32, (SUPER, GROUP_W), 1) >> (SSD_HEADDIM.bit_length() - 1)

    pre = []
    for i in seqs:
        xs = xc_ref[i, :, 0:SSD_WIDTH].astype(F32)
        dtv = _softplus(dt_ref[i] + dtb_ref[...])
        da = dtv * (-jnp.exp(alog_ref[...]))
        acs = jnp.dot(tri_ref[...], jnp.concatenate(_split3(da), axis=0), preferred_element_type=F32)
        dt_x = jnp.dot(jnp.concatenate(_split3(dtv), axis=1), e_ref[...], preferred_element_type=F32)
        acs_x = jnp.dot(jnp.concatenate(_split3(acs), axis=1), e_ref[...], preferred_element_type=F32)
        pre.append((xs, acs, acs_x, jnp.exp(acs_x), xs * dt_x))

    for sb in range(rows // SUPER):
        o = sb * SUPER
        nreal = (min(real, o + SUPER) - o) // chunk
        acs2 = [pre[i][1][o:o + SUPER, :] for i in seqs]
        acs_t = [a.T for a in acs2]
        for g in range(SSD_GROUPS):
            gs = slice(g * GROUP_W, (g + 1) * GROUP_W)
            bcol = SSD_WIDTH + g * SSD_STATE
            ccol = SSD_WIDTH + (SSD_GROUPS + g) * SSD_STATE
            for i in seqs:
                _, _, acs_x, eacs_x, xd = pre[i]
                bmb = xc_ref[i, o:o + SUPER, bcol:bcol + SSD_STATE]
                cmb = xc_ref[i, o:o + SUPER, ccol:ccol + SSD_STATE]
                cb2 = lax.dot_general(cmb, bmb, (((1,), (1,)), ((), ())), preferred_element_type=F32)
                bm_t = bmb.astype(F32).T.astype(BF16)
                ms = []
                for rr in range(SSD_HPG):
                    r = g * SSD_HPG + rr
                    seg = acs2[i][:, r:r + 1] - acs_t[i][r:r + 1, :]
                    dec = jnp.where(mask2, jnp.exp(jnp.where(mask2, seg, 0.0)), 0.0)
                    ms.append((cb2 * dec).astype(BF16))
                full = jnp.dot(jnp.concatenate(ms, axis=0), xd[o:o + SUPER, gs].astype(BF16),
                               preferred_element_type=F32)
                ydiag = full[0:SUPER]
                for rr in range(1, SSD_HPG):
                    ydiag = jnp.where(lane_g == rr, full[rr * SUPER:(rr + 1) * SUPER], ydiag)
                ybuf[i, o:o + SUPER, gs] = ydiag
                for j in range(nreal):
                    a0, a1 = o + j * chunk, o + (j + 1) * chunk
                    h_t = h_scr[i, g]
                    yoff = jnp.dot(cmb[j * chunk:(j + 1) * chunk, :], h_t.astype(BF16), preferred_element_type=F32)
                    ybuf[i, a0:a1, gs] = ybuf[i, a0:a1, gs] + yoff * eacs_x[a0:a1, gs]
                    dte = jnp.exp(acs_x[a1 - 1:a1, gs] - acs_x[a0:a1, gs])
                    xw = (xd[a0:a1, gs] * dte).astype(BF16)
                    pieces = []
                    if j > 0:
                        pieces.append(jnp.zeros((j * chunk, GROUP_W), BF16))
                    pieces.append(xw)
                    if (j + 1) * chunk < SUPER:
                        pieces.append(jnp.zeros((SUPER - (j + 1) * chunk, GROUP_W), BF16))
                    xw2 = jnp.concatenate(pieces, axis=0) if len(pieces) > 1 else xw
                    st = jnp.dot(bm_t, xw2, preferred_element_type=F32)
                    h_scr[i, g] = h_t * eacs_x[a1 - 1:a1, gs] + st

    for i in seqs:
        y = (ybuf[i] + dsk_ref[...] * pre[i][0]) * zs_ref[i].astype(F32)
        for g in range(SSD_GROUPS):
            gs = slice(g * GROUP_W, (g + 1) * GROUP_W)
            y_ref[i, :, gs] = _rms(y[:, gs], nw_ref[:, gs]).astype(BF16)

    @pl.when(t == pl.num_programs(1) - 1)
    def _fin():
        for i in seqs:
            for g in range(SSD_GROUPS):
                hout_ref[i, g] = h_scr[i, g].T


def _ssd_call(zs, xc, dt, h0_t, dtb, alog, dsk, nw, *, chunk, rows, real):
    b, t, _ = zs.shape
    ii = np.arange(rows)
    tri = ((ii[:, None] // chunk == ii[None, :] // chunk) & (ii[None, :] <= ii[:, None])).astype(np.float32)
    e = np.zeros((LANES, SSD_WIDTH), np.float32)
    for r in range(SSD_HEADS):
        e[r, r * SSD_HEADDIM:(r + 1) * SSD_HEADDIM] = 1.0

    nseq = math.gcd(b, SSD_NSEQ)

    def row(width):
        return pl.BlockSpec((nseq, rows, width), lambda i, j: (i, j, 0))

    state_spec = pl.BlockSpec((nseq, SSD_GROUPS, GROUP_W, SSD_STATE), lambda i, j: (i, 0, 0, 0))
    kern = functools.partial(_ssd_kernel, nseq=nseq, chunk=chunk, rows=rows, real=real)
    return pl.pallas_call(
        kern,
        grid=(b // nseq, t // rows),
        in_specs=[row(SSD_WIDTH), row(SSD_CONV_DIM), row(LANES), state_spec,
                  _const_spec((1, LANES)), _const_spec((1, LANES)),
                  _const_spec((1, SSD_WIDTH)), _const_spec((1, SSD_WIDTH)),
                  _const_spec((rows, 3 * rows)), _const_spec((3 * LANES, SSD_WIDTH))],
        out_specs=[row(SSD_WIDTH), state_spec],
        out_shape=[jax.ShapeDtypeStruct((b, t, SSD_WIDTH), BF16),
                   jax.ShapeDtypeStruct((b, SSD_GROUPS, GROUP_W, SSD_STATE), F32)],
        scratch_shapes=[pltpu.VMEM((nseq, SSD_GROUPS, SSD_STATE, GROUP_W), F32),
                        pltpu.VMEM((nseq, rows, SSD_WIDTH), F32)],
        compiler_params=pltpu.CompilerParams(dimension_semantics=("parallel", "arbitrary"),
                                             vmem_limit_bytes=VMEM_LIMIT),
        name="ssd",
    )(zs, xc, dt, h0_t, dtb, alog, dsk, nw,
      jnp.asarray(np.tile(tri, (1, 3)), BF16), jnp.asarray(np.tile(e, (3, 1)), BF16))


def _attn_kernel(scal_ref, qt_ref, k_ref, vt_ref, btab_ref, subw_ref, o_ref,
                 m_scr, l_scr, acc_scr, sa_scr, sb_scr, sc_scr, pa_scr, pb_scr, alpha_scr, qz_scr, bias_scr,
                 *, bq, bk, noff, out_scale):
    h = pl.program_id(0)
    qi = pl.program_id(2)
    kn0 = qi + (noff - 1)
    lam = scal_ref[0]
    cfar = scal_ref[1 + h]

    @pl.when((pl.program_id(1) == 0) & (qi == 0))
    def _build_bias_tiles():
        width = bq + bk
        kj = lax.broadcasted_iota(jnp.int32, (bk, bq), 0)
        qj = lax.broadcasted_iota(jnp.int32, (bk, bq), 1)
        cshift = CHUNK.bit_length() - 1
        for d in range(2):
            skew = pltpu.roll(jnp.broadcast_to(btab_ref[d], (bk, width)), 0, 1, stride=1, stride_axis=0)
            tile = skew[:, bk:width]
            if d == 1:
                tile = jnp.where((kj >> cshift) <= (qj >> cshift), tile, NEG)
            bias_scr[d] = tile

    zero = jnp.zeros((DA_DK, bq), BF16)
    qz_scr[0, 0:DA_DK, :] = qt_ref[0:DA_DK, :]
    qz_scr[0, DA_DK:DA_DV, :] = zero
    qz_scr[1, 0:DA_DK, :] = zero
    qz_scr[1, DA_DK:DA_DV, :] = qt_ref[DA_DK:DA_DV, :]

    m_scr[...] = jnp.full(m_scr.shape, NEG, F32)
    l_scr[...] = jnp.zeros(l_scr.shape, F32)
    acc_scr[...] = jnp.zeros(acc_scr.shape, F32)

    nsub = bk // ATT_KT

    def fold(x):
        return x.reshape(ATT_KT // SUBLANES, SUBLANES, bq)

    def scores(s_buf, first, count):
        for mm in range(2):
            for e in range(count):
                start = pl.multiple_of((first + e) * bk, bk)
                s_buf[mm, e] = jnp.dot(k_ref[pl.ds(start, bk), :], qz_scr[mm], preferred_element_type=F32)

    def softmax(s_buf, p_buf, entries):
        alphas = []
        for mm in range(2):
            cand = None
            for e, (near, shift) in enumerate(entries):
                mx = None
                for t in range(nsub):
                    rows = slice(t * ATT_KT, (t + 1) * ATT_KT)
                    s = s_buf[mm, e, rows, :]
                    if near is not None:
                        s = s + bias_scr[near, rows, :]
                    pm = jnp.max(fold(s), axis=0)
                    mx = pm if mx is None else jnp.maximum(mx, pm)
                mx = jnp.max(mx, axis=0, keepdims=True) + shift
                cand = mx if cand is None else jnp.maximum(cand, mx)
            m_old = m_scr[mm]
            m_new = jnp.maximum(m_old, cand)
            ls = None
            for e, (near, shift) in enumerate(entries):
                off = m_new - shift
                for t in range(nsub):
                    rows = slice(t * ATT_KT, (t + 1) * ATT_KT)
                    s = s_buf[mm, e, rows, :]
                    if near is not None:
                        s = s + bias_scr[near, rows, :]
                    p = jnp.exp2(s - off)
                    p_buf[mm, e * bk + t * ATT_KT:e * bk + (t + 1) * ATT_KT, :] = p.astype(BF16)
                    ps = jnp.sum(fold(p), axis=0)
                    ls = ps if ls is None else ls + ps
            alpha = jnp.exp2(m_old - m_new)
            l_scr[mm] = alpha * l_scr[mm] + jnp.sum(ls, axis=0, keepdims=True)
            m_scr[mm] = m_new
            alphas.append(alpha)
        return alphas

    def pv(p_buf, first, n):
        vts = [vt_ref[first + e] for e in range(n)]
        vt = jnp.concatenate(vts, axis=1) if n > 1 else vts[0]
        return [jnp.dot(vt, p_buf[mm, 0:n * bk, :], preferred_element_type=F32) for mm in range(2)]

    def accumulate(alphas, pvs):
        for mm in range(2):
            acc_scr[mm] = alphas[mm] * acc_scr[mm] + pvs[mm]

    def softmax_pv(s_buf, first, entries):
        alphas = softmax(s_buf, pa_scr, entries)
        accumulate(alphas, pv(pa_scr, first, len(entries)))

    nfar = jnp.maximum(kn0, 0)
    odd = nfar % 2
    far = (None, cfar)

    near_pair = [(0, 0.0), (1, 0.0)]

    @pl.when(kn0 < 0)
    def _only_first():
        scores(sc_scr, 0, 1)
        softmax_pv(sc_scr, 0, [(1, 0.0)])

    @pl.when(kn0 >= 0)
    def _groups():
        @pl.when(odd == 1)
        def _single():
            scores(sc_scr, 0, 1)
            scores(sa_scr, 1, 2)
            softmax_pv(sc_scr, 0, [far])

        @pl.when(odd == 0)
        def _first_pair():
            scores(sa_scr, 0, 2)

        def far_pair(s_cur, s_next, cur):
            scores(s_next, cur + 2, 2)
            softmax_pv(s_cur, cur, [far, far])

        def far_step(s_cur, p_cur, s_next, p_prev, cur):
            scores(s_next, cur + 2, 2)
            pending = None if p_prev is None else pv(p_prev, cur - 2, 2)
            alphas = softmax(s_cur, p_cur, [far, far])
            if pending is not None:
                accumulate([alpha_scr[0], alpha_scr[1]], pending)
            for mm in range(2):
                alpha_scr[mm] = alphas[mm]

        npairs = nfar // 2
        niter = npairs // 2

        @pl.when(niter >= 1)
        def _far_loop():
            far_step(sa_scr, pa_scr, sb_scr, None, odd)
            far_step(sb_scr, pb_scr, sa_scr, pa_scr, odd + 2)

            def far_body(j, carry):
                cur = odd + 4 * j
                far_step(sa_scr, pa_scr, sb_scr, pb_scr, cur)
                far_step(sb_scr, pb_scr, sa_scr, pa_scr, cur + 2)
                return carry

            lax.fori_loop(1, niter, far_body, 0)
            accumulate([alpha_scr[0], alpha_scr[1]], pv(pb_scr, odd + 4 * niter - 2, 2))

        @pl.when(npairs % 2 == 1)
        def _tail_b():
            far_pair(sa_scr, sb_scr, kn0 - 2)
            softmax_pv(sb_scr, kn0, near_pair)

        @pl.when(npairs % 2 == 0)
        def _tail_a():
            softmax_pv(sa_scr, kn0, near_pair)

    o = acc_scr[0] * (1.0 / l_scr[0]) - acc_scr[1] * (lam / l_scr[1])
    o = o * lax.rsqrt(jnp.mean(o * o, axis=0, keepdims=True) + EPS) * (subw_ref[...] * out_scale)
    o_ref[...] = o.T.astype(BF16)


def _attn_call(scal, qt, kb, vt, btab, subw, *, bq, bk, noff, out_scale):
    b, nq = qt.shape[:2]
    tk = kb.shape[1]
    nkb = vt.shape[1]
    kern = functools.partial(_attn_kernel, bq=bq, bk=bk, noff=noff, out_scale=out_scale)
    return pl.pallas_call(
        kern,
        grid=(DA_HEADS, b, nq),
        in_specs=[pl.BlockSpec(memory_space=pltpu.SMEM),
                  pl.BlockSpec((None, None, DA_DV, bq), lambda h, i, j: (i, j, h, 0)),
                  pl.BlockSpec((None, tk, DA_DV), lambda h, i, j: (i, 0, h)),
                  pl.BlockSpec((None, nkb, DA_DV, bk), lambda h, i, j: (i, 0, h, 0)),
                  pl.BlockSpec((None, 2, 1, bq + bk), lambda h, i, j: (h, 0, 0, 0)),
                  pl.BlockSpec((DA_DV, 1), lambda h, i, j: (0, 0))],
        out_specs=pl.BlockSpec((None, bq, DA_DV), lambda h, i, j: (i, j, h)),
        out_shape=jax.ShapeDtypeStruct((b, nq * bq, DA_WIDTH), BF16),
        scratch_shapes=[pltpu.VMEM((2, 1, bq), F32), pltpu.VMEM((2, 1, bq), F32),
                        pltpu.VMEM((2, DA_DV, bq), F32),
                        pltpu.VMEM((2, 2, bk, bq), F32), pltpu.VMEM((2, 2, bk, bq), F32),
                        pltpu.VMEM((2, 1, bk, bq), F32),
                        pltpu.VMEM((2, 2 * bk, bq), BF16), pltpu.VMEM((2, 2 * bk, bq), BF16),
                        pltpu.VMEM((2, 1, bq), F32),
                        pltpu.VMEM((2, DA_DV, bq), BF16),
                        pltpu.VMEM((2, bk, bq), F32)],
        compiler_params=pltpu.CompilerParams(dimension_semantics=("arbitrary", "arbitrary", "arbitrary"),
                                             vmem_limit_bytes=VMEM_LIMIT),
        name="attn",
    )(scal, qt, kb, vt, btab, subw)


def _attn_cached_kernel(scal_ref, qt_ref, kc_ref, vc_ref, kn_ref, vtn_ref, bias0_ref, bias1_ref, subw_ref, o_ref,
                        *, bq, tq, bk, out_scale):
    lam = scal_ref[0]
    past = kc_ref.shape[0] // DA_HEADS
    lane = lax.broadcasted_iota(jnp.int32, (tq, DA_DV), 1)
    nt = (((1,), (1,)), ((), ()))

    for h in range(DA_HEADS):
        hs = slice(h * DA_DV, (h + 1) * DA_DV)
        cfar = scal_ref[1 + h]
        qn = qt_ref[hs, :].astype(F32).T[0:tq, :]
        q2 = jnp.concatenate([jnp.where(lane < DA_DK, qn, 0.0), jnp.where(lane >= DA_DK, qn, 0.0)],
                             axis=0).astype(BF16)
        head_rows = pl.ds(h, past, stride=DA_HEADS)
        s_c = lax.dot_general(q2, kc_ref[head_rows, :].astype(BF16), nt, preferred_element_type=F32)
        s_n = lax.dot_general(q2, kn_ref[:, hs], nt, preferred_element_type=F32)
        b0 = bias0_ref[h]
        b1 = bias1_ref[h]
        s = jnp.concatenate([s_c[:, 0:past - bk] + cfar,
                             s_c[:, past - bk:past] + jnp.concatenate([b0, b0], axis=0),
                             s_n + jnp.concatenate([b1, b1], axis=0)], axis=1)
        p = jnp.exp2(s - jnp.max(s, axis=1, keepdims=True))
        inv = 1.0 / jnp.sum(p, axis=1, keepdims=True)
        pb = p.astype(BF16)
        acc = (jnp.dot(pb[:, 0:past], vc_ref[head_rows, :].astype(BF16), preferred_element_type=F32)
               + jnp.dot(pb[:, past:past + bq], vtn_ref[hs, :].astype(F32).T.astype(BF16),
                         preferred_element_type=F32))
        o = acc[0:tq, :] * inv[0:tq] - acc[tq:2 * tq, :] * (lam * inv[tq:2 * tq])
        o_ref[0:tq, hs] = (_rms(o, subw_ref[...]) * out_scale).astype(BF16)
        o_ref[tq:bq, hs] = jnp.zeros((bq - tq, DA_DV), BF16)


def _attn_cached_call(scal, qt, cache_k, cache_v, kb, vt, bias0, bias1, subw, *, tq, bk, out_scale):
    b, past = cache_k.shape[:2]
    bq = qt.shape[-1]
    rows = past * DA_HEADS
    kern = functools.partial(_attn_cached_kernel, bq=bq, tq=tq, bk=bk, out_scale=out_scale)
    cache_spec = pl.BlockSpec((None, rows, DA_DV), lambda i: (i, 0, 0))
    return pl.pallas_call(
        kern,
        grid=(b,),
        in_specs=[pl.BlockSpec(memory_space=pltpu.SMEM),
                  pl.BlockSpec((None, None, DA_WIDTH, bq), lambda i: (i, 0, 0, 0)),
                  cache_spec, cache_spec,
                  pl.BlockSpec((None, bq, DA_WIDTH), lambda i: (i, 0, 0)),
                  pl.BlockSpec((None, None, DA_WIDTH, bq), lambda i: (i, 0, 0, 0)),
                  _const_spec((DA_HEADS, tq, bk)), _const_spec((DA_HEADS, tq, bq)),
                  _const_spec((1, DA_DV))],
        out_specs=pl.BlockSpec((None, bq, DA_WIDTH), lambda i: (i, 0, 0)),
        out_shape=jax.ShapeDtypeStruct((b, bq, DA_WIDTH), BF16),
        compiler_params=pltpu.CompilerParams(dimension_semantics=("parallel",), vmem_limit_bytes=VMEM_LIMIT),
        name="attn_cached",
    )(scal, qt, cache_k.reshape(b, rows, DA_DV), cache_v.reshape(b, rows, DA_DV), kb, vt, bias0, bias1, subw)


def _rel_bucket(rel):
    nb = REL_BUCKETS // 2
    max_exact = nb // 2
    n = jnp.abs(rel)
    nf = jnp.maximum(n, 1).astype(jnp.float32)
    large = max_exact + (jnp.log(nf / max_exact) / math.log(REL_MAX_DIST / max_exact)
                         * (nb - max_exact)).astype(jnp.int32)
    large = jnp.minimum(large, nb - 1)
    return jnp.where(rel > 0, nb, 0) + jnp.where(n < max_exact, n, large)


def _bias_tables(rel_bias, bq, bk):
    tabs = []
    for d in range(2):
        offs = (d - 1) * bk + bk - np.arange(bq + bk)
        tabs.append(rel_bias[_rel_bucket(jnp.asarray(offs, jnp.int32))].astype(F32).T * LOG2E)
    return jnp.stack(tabs, axis=1)[:, :, None, :]


def _bias_rows(rel_bias, qpos0, tq, kpos0, nkeys, tk_real):
    offs = (kpos0 - qpos0) + np.arange(-(tq - 1), nkeys)
    table = rel_bias[_rel_bucket(jnp.asarray(offs, jnp.int32))].astype(F32).T * LOG2E
    rows = jnp.stack([table[:, tq - 1 - i:tq - 1 - i + nkeys] for i in range(tq)], axis=1)
    qpos = qpos0 + np.arange(tq)
    kpos = kpos0 + np.arange(nkeys)
    vis = (kpos[None, :] // CHUNK <= qpos[:, None] // CHUNK) & (kpos[None, :] < tk_real)
    return jnp.where(jnp.asarray(vis)[None], rows, NEG)


def _ffn_kernel(x_ref, ys_ref, ya_ref, mod_ref, wo1_ref, wo2_ref, nfw_ref, wu_ref, cw_ref, cb_ref, wd_ref, fw_ref,
                hist_ref, y_ref, tail_ref, tail_scr, buf_a, buf_b, x1_scr, h2_scr, act_scr, *, tm, real):
    t = pl.program_id(1)
    nh = FFN_CONV - 1
    lo = SUBLANES - nh

    @pl.when(t == 0)
    def _init():
        tail_scr[lo:SUBLANES, :] = hist_ref[...]

    mix = (jnp.dot(ys_ref[...], wo1_ref[...], preferred_element_type=F32)
           + jnp.dot(ya_ref[...], wo2_ref[...], preferred_element_type=F32))
    x1 = x_ref[...] + mod_ref[2:3, :] * mix
    x1_scr[...] = x1
    h2 = _rms(x1, nfw_ref[...]) * (1.0 + mod_ref[4:5, :]) + mod_ref[3:4, :]
    h2_scr[...] = h2.astype(BF16)

    def cols(j, half):
        return slice(half * D_FF + j * FFN_CN, half * D_FF + (j + 1) * FFN_CN)

    def up(j, buf):
        for half in range(2):
            buf[half, SUBLANES:SUBLANES + tm, :] = jnp.dot(h2_scr[...], wu_ref[:, cols(j, half)],
                                                            preferred_element_type=F32)

    def conv(j, buf, half):
        cs = cols(j, half)
        buf[half, lo:SUBLANES, :] = tail_scr[lo:SUBLANES, cs]
        c = cb_ref[:, cs]
        for i in range(FFN_CONV):
            c = c + cw_ref[i:i + 1, cs] * buf[half, lo + i:lo + i + tm, :]
        tail_scr[lo:SUBLANES, cs] = buf[half, lo + real:SUBLANES + real, :]
        return c

    bufs = (buf_a, buf_b)
    up(0, bufs[0])
    for j in range(FFN_NC):
        if j + 1 < FFN_NC:
            up(j + 1, bufs[(j + 1) % 2])
        cv = conv(j, bufs[j % 2], 0)
        cg = conv(j, bufs[j % 2], 1)
        act_scr[:, j * FFN_CN:(j + 1) * FFN_CN] = (_silu(cg) * cv).astype(BF16)

    f = jnp.dot(act_scr[...], wd_ref[...], preferred_element_type=F32)
    x2 = x1_scr[...] + mod_ref[5:6, :] * f
    y_ref[...] = _rms(x2, fw_ref[...])
    tail_ref[...] = tail_scr[lo:SUBLANES, :]


def _ffn_call(x, ys, ya, mod3, wo1, wo2, nfw, wu, cw, cb, wd, fw, hist, *, tm, real):
    b, t, d = x.shape
    nh = FFN_CONV - 1

    def row(width):
        return pl.BlockSpec((None, tm, width), lambda i, j: (i, j, 0))

    hist_spec = pl.BlockSpec((None, nh, 2 * D_FF), lambda i, j: (i, 0, 0))
    kern = functools.partial(_ffn_kernel, tm=tm, real=real)
    return pl.pallas_call(
        kern,
        grid=(b, t // tm),
        in_specs=[row(d), row(SSD_WIDTH), row(DA_WIDTH),
                  pl.BlockSpec((None, 6, d), lambda i, j: (i, 0, 0)),
                  _const_spec((SSD_WIDTH, d)), _const_spec((DA_WIDTH, d)), _const_spec((1, d)),
                  _const_spec((d, 2 * D_FF)), _const_spec((FFN_CONV, 2 * D_FF)), _const_spec((1, 2 * D_FF)),
                  _const_spec((D_FF, d)), _const_spec((1, d)),
                  hist_spec],
        out_specs=[row(d), hist_spec],
        out_shape=[jax.ShapeDtypeStruct((b, t, d), F32),
                   jax.ShapeDtypeStruct((b, nh, 2 * D_FF), F32)],
        scratch_shapes=[pltpu.VMEM((SUBLANES, 2 * D_FF), F32),
                        pltpu.VMEM((2, tm + SUBLANES, FFN_CN), F32), pltpu.VMEM((2, tm + SUBLANES, FFN_CN), F32),
                        pltpu.VMEM((tm, d), F32), pltpu.VMEM((tm, d), BF16), pltpu.VMEM((tm, D_FF), BF16)],
        compiler_params=pltpu.CompilerParams(dimension_semantics=("parallel", "arbitrary"),
                                             vmem_limit_bytes=VMEM_LIMIT),
        name="ffn",
    )(x, ys, ya, mod3, wo1, wo2, nfw, wu, cw, cb, wd, fw, hist)


def _pack_params(norm_mix_w, w_in, ssm_conv_w, ssm_conv_b, ssm_dt_bias, ssm_a_log, ssm_d, ssm_norm_w,
                 lambda_q1, lambda_k1, lambda_q2, lambda_k2, attn_subln_w, rel_bias, w_out,
                 norm_ffn_w, w_up, ffn_conv_w, ffn_conv_b, w_down, final_norm_w, layer):
    l = layer
    wz, wx, wdt, wq, wk, wv = jnp.split(w_in[l], IN_SPLITS, axis=-1)
    wdt = jnp.pad(wdt, ((0, 0), (0, LANES - SSD_HEADS)))
    w_cat = jnp.concatenate([wz, wx, wdt, wk, wv], axis=-1).astype(BF16)
    w_t = jnp.concatenate([wq * (DA_DK ** -0.5 * LOG2E), wv], axis=-1).T.astype(BF16)

    def pad_heads(v):
        return jnp.pad(v.astype(F32), (0, LANES - SSD_HEADS)).reshape(1, LANES)

    lam_init = 0.8 - 0.6 * math.exp(-0.3 * l)
    lam = (jnp.exp(jnp.sum(lambda_q1[l].astype(F32) * lambda_k1[l].astype(F32)))
           - jnp.exp(jnp.sum(lambda_q2[l].astype(F32) * lambda_k2[l].astype(F32))) + lam_init)
    far_bias = rel_bias[REL_BUCKETS // 2 - 1].astype(F32)
    return dict(
        norm_mix_w=norm_mix_w[l].reshape(1, D_MODEL), w_cat=w_cat, w_t=w_t,
        cw=ssm_conv_w[l], cbias=ssm_conv_b[l].reshape(1, SSD_CONV_DIM),
        dtb=pad_heads(ssm_dt_bias[l]), alog=pad_heads(ssm_a_log[l]),
        dsk=jnp.repeat(ssm_d[l].astype(F32), SSD_HEADDIM).reshape(1, SSD_WIDTH),
        ssm_nw=ssm_norm_w[l].reshape(1, SSD_WIDTH),
        scal=jnp.concatenate([lam.reshape(1), far_bias * LOG2E]).astype(F32), lam_init=lam_init,
        subw=attn_subln_w[l].reshape(DA_DV, 1), rel_bias=rel_bias,
        wo1=w_out[l][:SSD_WIDTH].astype(BF16), wo2=w_out[l][SSD_WIDTH:].astype(BF16),
        nfw=norm_ffn_w[l].reshape(1, D_MODEL),
        wu=w_up[l].astype(BF16), ffn_cw=ffn_conv_w[l], ffn_cb=ffn_conv_b[l].reshape(1, 2 * D_FF),
        wd=w_down[l].astype(BF16), fw=final_norm_w.reshape(1, D_MODEL),
    )


def _state_to_kernel(h):
    return h.reshape(h.shape[0], SSD_GROUPS, GROUP_W, SSD_STATE)


def _state_from_kernel(h):
    return h.reshape(h.shape[0], SSD_HEADS, SSD_HEADDIM, SSD_STATE)


def _run_group(x, mod, past_k, past_v, ssm_h0, ssm_conv_hist, ffn_conv_hist, p, *, tm, ssd_rows, bq, bk):
    b, t, d = x.shape
    past = 0 if past_k is None else past_k.shape[1]
    chunk = min(CHUNK, t)
    tp = max(t, SUPER)
    if tp != t:
        x = jnp.pad(x, ((0, 0), (0, tp - t), (0, 0)))
        tm = ssd_rows = bq = tp
    mod3 = mod.reshape(b, 6, d)

    assert tm == bq
    zs, xc, dt, k, v, kb, qt, vt, conv_new = _inproj_call(
        x, mod3, p["norm_mix_w"], p["w_cat"], p["w_t"], ssm_conv_hist.astype(F32), p["cw"], p["cbias"],
        tm=tm, real=min(t, tm))

    y_ssd, h_t = _ssd_call(zs, xc, dt, _state_to_kernel(ssm_h0.astype(F32)),
                           p["dtb"], p["alog"], p["dsk"], p["ssm_nw"],
                           chunk=chunk, rows=ssd_rows, real=min(t, ssd_rows))

    if past == 0:
        assert bq == bk and t % bq == 0
        y_att = _attn_call(p["scal"], qt, kb, vt, _bias_tables(p["rel_bias"], bq, bk), p["subw"], bq=bq, bk=bk, noff=0,
                           out_scale=1.0 - p["lam_init"])
    else:
        assert past % bk == 0 and past >= bk and bq == tp <= bk
        bias0 = _bias_rows(p["rel_bias"], past, t, past - bk, bk, past + t)
        bias1 = _bias_rows(p["rel_bias"], past, t, past, bq, past + t)
        y_att = _attn_cached_call(p["scal"], qt, past_k, past_v, kb, vt, bias0, bias1, p["subw"].reshape(1, DA_DV),
                                  tq=t, bk=bk, out_scale=1.0 - p["lam_init"])

    y, ffn_new = _ffn_call(x, y_ssd, y_att, mod3, p["wo1"], p["wo2"], p["nfw"], p["wu"], p["ffn_cw"], p["ffn_cb"],
                           p["wd"], p["fw"], ffn_conv_hist.astype(F32), tm=tm, real=min(t, tm))
    k = k.reshape(b, tp, DA_HEADS, 2 * DA_DK)
    v = v.reshape(b, tp, DA_HEADS, DA_DV)
    return (y[:, :t], k[:, :t], v[:, :t],
            _state_from_kernel(h_t), conv_new, ffn_new)


def kernel(x_prompt, x_sample, c_prompt, c_sample, cache_k, cache_v, state_ssm, state_ssm_conv, state_ffn_conv, w_ada, b_ada, norm_mix_w, w_in, ssm_conv_w, ssm_conv_b, ssm_dt_bias, ssm_a_log, ssm_d, ssm_norm_w, lambda_q1, lambda_k1, lambda_q2, lambda_k2, attn_subln_w, rel_bias, w_out, norm_ffn_w, w_up, ffn_conv_w, ffn_conv_b, w_down, final_norm_w):
    bp, bs = x_prompt.shape[0], x_sample.shape[0]
    dt = x_prompt.dtype
    p = _pack_params(norm_mix_w, w_in, ssm_conv_w, ssm_conv_b, ssm_dt_bias, ssm_a_log, ssm_d, ssm_norm_w,
                     lambda_q1, lambda_k1, lambda_q2, lambda_k2, attn_subln_w, rel_bias, w_out,
                     norm_ffn_w, w_up, ffn_conv_w, ffn_conv_b, w_down, final_norm_w, 0)
    c_all = jnp.concatenate([c_prompt, c_sample], axis=0)
    npad = -c_all.shape[0] % SUBLANES
    c_all = jnp.pad(c_all, ((0, npad), (0, 0)))
    mod = _mod_call(c_all, w_ada[0], b_ada[0].reshape(1, -1))

    zeros = lambda *s: jnp.zeros(s, dt)
    out_p = _run_group(x_prompt, mod[:bp], None, None,
                       zeros(bp, SSD_HEADS, SSD_HEADDIM, SSD_STATE), zeros(bp, SSD_CONV - 1, SSD_CONV_DIM),
                       zeros(bp, FFN_CONV - 1, 2 * D_FF), p, tm=512, ssd_rows=256, bq=512, bk=512)
    out_s = _run_group(x_sample, mod[bp:bp + bs], cache_k[0], cache_v[0], state_ssm[0], state_ssm_conv[0],
                       state_ffn_conv[0], p, tm=SUPER, ssd_rows=SUPER, bq=SUPER, bk=512)
    y_p, k_p, v_p, h_p, c_p, f_p = out_p
    y_s, k_s, v_s, h_s, c_s, f_s = out_s
    return (y_p, y_s, k_p[None], v_p[None], h_p[None], c_p[None], f_p[None],
            k_s[None], v_s[None], h_s[None], c_s[None], f_s[None])
```

```python
import functools
import math

import numpy as np
import jax
import jax.numpy as jnp
from jax import lax
from jax.experimental import pallas as pl
from jax.experimental.pallas import tpu as pltpu

F32 = jnp.float32
BF16 = jnp.bfloat16
HIGHEST = lax.Precision.HIGHEST

D_MODEL = 1024
CHUNK = 64
SSD_WIDTH = 512
SSD_HEADDIM = 64
SSD_HEADS = 8
SSD_GROUPS = 2
SSD_HPG = 4
SSD_STATE = 128
SSD_CONV = 4
SSD_CONV_DIM = SSD_WIDTH + 2 * SSD_GROUPS * SSD_STATE
GROUP_W = SSD_HPG * SSD_HEADDIM
DA_WIDTH = 512
DA_DK = 64
DA_DV = 128
DA_HEADS = 4
REL_BUCKETS = 32
REL_MAX_DIST = 128
D_FF = 2816
FFN_CONV = 3
EPS = 1e-6
IN_SPLITS = (512, 1536, 1544, 2056, 2568)
LANES = 128
SUBLANES = 8
SUPER = 128
SSD_NSEQ = 4
ATT_KT = 64
FFN_CN = 256
FFN_NC = D_FF // FFN_CN
NEG = -1e30
VMEM_LIMIT = 56 * 1024 * 1024

PZ, PX, PDT, PK, PV, PEND = 0, 512, 1536, 1664, 2176, 2688
LOG2E = math.log2(math.e)


def _silu(x):
    return x / (1.0 + jnp.exp(-x))


def _softplus(x):
    return jnp.maximum(x, 0.0) + jnp.log1p(jnp.exp(-jnp.abs(x)))


def _split3(x):
    hi = x.astype(BF16)
    r1 = x - hi.astype(F32)
    mid = r1.astype(BF16)
    lo = (r1 - mid.astype(F32)).astype(BF16)
    return hi, mid, lo


def _rms(x, w):
    return x * lax.rsqrt(jnp.mean(x * x, axis=-1, keepdims=True) + EPS) * w


def _const_spec(shape):
    nd = len(shape)
    return pl.BlockSpec(shape, lambda *_: (0,) * nd)


def _mod_kernel(c_ref, w_ref, b_ref, o_ref):
    a = _silu(c_ref[...]).astype(BF16)
    o_ref[...] = jnp.dot(a, w_ref[...].astype(BF16), preferred_element_type=F32) + b_ref[...]


def _mod_call(c, w_ada, b_ada):
    n, d = c.shape
    nout = w_ada.shape[1]
    tn = 1024
    return pl.pallas_call(
        _mod_kernel,
        grid=(nout // tn,),
        in_specs=[pl.BlockSpec((n, d), lambda j: (0, 0)),
                  pl.BlockSpec((d, tn), lambda j: (0, j)),
                  pl.BlockSpec((1, tn), lambda j: (0, j))],
        out_specs=pl.BlockSpec((n, tn), lambda j: (0, j)),
        out_shape=jax.ShapeDtypeStruct((n, nout), F32),
        name="mod",
    )(c, w_ada, b_ada)


def _inproj_kernel(x_ref, mod_ref, nw_ref, w_ref, wt_ref, hist_ref, cw_ref, cbias_ref,
                   zs_ref, xc_ref, dt_ref, k_ref, v_ref, kb_ref, qt_ref, vt_ref, cout_ref, cbuf, hb_scr, zbuf,
                   *, tm, real):
    t = pl.program_id(1)
    nconv = SSD_CONV - 1

    @pl.when(t == 0)
    def _init():
        cbuf[0:SUBLANES, :] = jnp.zeros((SUBLANES, SSD_CONV_DIM), F32)
        cbuf[SUBLANES - nconv:SUBLANES, :] = hist_ref[...]

    h = _rms(x_ref[...], nw_ref[...]) * (1.0 + mod_ref[1:2, :]) + mod_ref[0:1, :]
    hb_scr[...] = h.astype(BF16)

    def proj(a, b):
        return jnp.dot(hb_scr[...], w_ref[:, a:b], preferred_element_type=F32)

    def proj_t(a, b):
        return lax.dot_general(wt_ref[a:b, :], hb_scr[...], (((1,), (1,)), ((), ())), preferred_element_type=F32)

    cbuf[SUBLANES:SUBLANES + tm, :] = proj(PX, PDT)
    zbuf[...] = proj(PZ, PX)
    dt_ref[...] = proj(PDT, PK)
    k = proj(PK, PV)
    v = proj(PV, PEND)
    for hd in range(DA_HEADS):
        dst = pl.ds(hd, tm, stride=DA_HEADS)
        k_ref[dst, :] = k[:, hd * DA_DV:(hd + 1) * DA_DV]
        v_ref[dst, :] = v[:, hd * DA_DV:(hd + 1) * DA_DV]
    kb_ref[...] = k.astype(BF16)
    qt_ref[...] = proj_t(0, DA_WIDTH).astype(BF16)
    vt_ref[...] = proj_t(DA_WIDTH, 2 * DA_WIDTH).astype(BF16)
    conv = cbias_ref[...]
    for j in range(SSD_CONV):
        off = SUBLANES - nconv + j
        conv = conv + cw_ref[j:j + 1, :] * cbuf[off:off + tm, :]
    tail = cbuf[SUBLANES - nconv + real:SUBLANES + real, :]
    cout_ref[...] = tail
    cbuf[SUBLANES - nconv:SUBLANES, :] = tail
    xc_ref[...] = _silu(conv).astype(BF16)
    zs_ref[...] = _silu(zbuf[...]).astype(BF16)


def _inproj_call(x, mod3, norm_w, w_cat, w_t, hist, cw, cbias, *, tm, real):
    b, t, d = x.shape
    nt = t // tm

    def row(width):
        return pl.BlockSpec((None, tm, width), lambda i, j: (i, j, 0))

    def out(width, dtype):
        return jax.ShapeDtypeStruct((b, t, width), dtype)

    tspec = pl.BlockSpec((None, None, DA_WIDTH, tm), lambda i, j: (i, j, 0, 0))
    tshape = jax.ShapeDtypeStruct((b, nt, DA_WIDTH, tm), BF16)
    hist_spec = pl.BlockSpec((None, SSD_CONV - 1, SSD_CONV_DIM), lambda i, j: (i, 0, 0))
    hspec = pl.BlockSpec((None, tm * DA_HEADS, DA_DV), lambda i, j: (i, j, 0))
    hshape = jax.ShapeDtypeStruct((b, t * DA_HEADS, DA_DV), F32)
    return pl.pallas_call(
        functools.partial(_inproj_kernel, tm=tm, real=real),
        grid=(b, nt),
        in_specs=[row(d),
                  pl.BlockSpec((None, 6, d), lambda i, j: (i, 0, 0)),
                  _const_spec((1, d)),
                  _const_spec((d, PEND)),
                  _const_spec((2 * DA_WIDTH, d)),
                  hist_spec, _const_spec((SSD_CONV, SSD_CONV_DIM)), _const_spec((1, SSD_CONV_DIM))],
        out_specs=[row(512), row(1024), row(LANES), hspec, hspec, row(512), tspec, tspec, hist_spec],
        out_shape=[out(512, BF16), out(1024, BF16), out(LANES, F32),
                   hshape, hshape, out(512, BF16), tshape, tshape,
                   jax.ShapeDtypeStruct((b, SSD_CONV - 1, SSD_CONV_DIM), F32)],
        scratch_shapes=[pltpu.VMEM((tm + SUBLANES, SSD_CONV_DIM), F32), pltpu.VMEM((tm, d), BF16),
                        pltpu.VMEM((tm, SSD_WIDTH), F32)],
        compiler_params=pltpu.CompilerParams(dimension_semantics=("parallel", "arbitrary"),
                                             vmem_limit_bytes=VMEM_LIMIT),
        name="inproj",
    )(x, mod3, norm_w, w_cat, w_t, hist, cw, cbias)


def _ssd_kernel(zs_ref, xc_ref, dt_ref, h0_ref, dtb_ref, alog_ref, dsk_ref, nw_ref, tri_ref, e_ref,
                y_ref, hout_ref, h_scr, ybuf, *, nseq, chunk, rows, real):
    t = pl.program_id(1)
    seqs = range(nseq)

    @pl.when(t == 0)
    def _init():
        for i in seqs:
            for g in range(SSD_GROUPS):
                h_scr[i, g] = h0_ref[i, g].T

    li = lax.broadcasted_iota(jnp.int32, (SUPER, SUPER), 0)
    si = lax.broadcasted_iota(jnp.int32, (SUPER, SUPER), 1)
    cshift = chunk.bit_length() - 1
    mask2 = ((li >> cshift) == (si >> cshift)) & (si <= li)
    lane_g = lax.broadcasted_iota(jnp.int32, (SUPER, GROUP_W), 1) >> (SSD_HEADDIM.bit_length() - 1)

    pre = []
    for i in seqs:
        xs = xc_ref[i, :, 0:SSD_WIDTH].astype(F32)
        dtv = _softplus(dt_ref[i] + dtb_ref[...])
        da = dtv * (-jnp.exp(alog_ref[...]))
        acs = jnp.dot(tri_ref[...], jnp.concatenate(_split3(da), axis=0), preferred_element_type=F32)
        dt_x = jnp.dot(jnp.concatenate(_split3(dtv), axis=1), e_ref[...], preferred_element_type=F32)
        acs_x = jnp.dot(jnp.concatenate(_split3(acs), axis=1), e_ref[...], preferred_element_type=F32)
        pre.append((xs, acs, acs_x, jnp.exp(acs_x), xs * dt_x))

    for sb in range(rows // SUPER):
        o = sb * SUPER
        nreal = (min(real, o + SUPER) - o) // chunk
        acs2 = [pre[i][1][o:o + SUPER, :] for i in seqs]
        acs_t = [a.T for a in acs2]
        chains = [(g, i) for g in range(SSD_GROUPS) for i in seqs]

        def gsl(g):
            return slice(g * GROUP_W, (g + 1) * GROUP_W)

        cmb, cb2, bm_t = {}, {}, {}
        for g, i in chains:
            bcol = SSD_WIDTH + g * SSD_STATE
            ccol = SSD_WIDTH + (SSD_GROUPS + g) * SSD_STATE
            bmb = xc_ref[i, o:o + SUPER, bcol:bcol + SSD_STATE]
            cmb[g, i] = xc_ref[i, o:o + SUPER, ccol:ccol + SSD_STATE]
            cb2[g, i] = lax.dot_general(cmb[g, i], bmb, (((1,), (1,)), ((), ())), preferred_element_type=F32)
            bm_t[g, i] = bmb.astype(F32).T.astype(BF16)
        for g, i in chains:
            ms = []
            for rr in range(SSD_HPG):
                r = g * SSD_HPG + rr
                seg = acs2[i][:, r:r + 1] - acs_t[i][r:r + 1, :]
                dec = jnp.where(mask2, jnp.exp(jnp.where(mask2, seg, 0.0)), 0.0)
                ms.append((cb2[g, i] * dec).astype(BF16))
            full = jnp.dot(jnp.concatenate(ms, axis=0), pre[i][4][o:o + SUPER, gsl(g)].astype(BF16),
                           preferred_element_type=F32)
            ydiag = full[0:SUPER]
            for rr in range(1, SSD_HPG):
                ydiag = jnp.where(lane_g == rr, full[rr * SUPER:(rr + 1) * SUPER], ydiag)
            ybuf[i, o:o + SUPER, gsl(g)] = ydiag
        for j in range(nreal):
            a0, a1 = o + j * chunk, o + (j + 1) * chunk
            h_t = {c: h_scr[c[1], c[0]] for c in chains}
            yoff = {(g, i): jnp.dot(cmb[g, i][j * chunk:(j + 1) * chunk, :], h_t[g, i].astype(BF16),
                                    preferred_element_type=F32) for g, i in chains}
            st = {}
            for g, i in chains:
                _, _, acs_x, eacs_x, xd = pre[i]
                ybuf[i, a0:a1, gsl(g)] = ybuf[i, a0:a1, gsl(g)] + yoff[g, i] * eacs_x[a0:a1, gsl(g)]
                dte = jnp.exp(acs_x[a1 - 1:a1, gsl(g)] - acs_x[a0:a1, gsl(g)])
                xw = (xd[a0:a1, gsl(g)] * dte).astype(BF16)
                pieces = []
                if j > 0:
                    pieces.append(jnp.zeros((j * chunk, GROUP_W), BF16))
                pieces.append(xw)
                if (j + 1) * chunk < SUPER:
                    pieces.append(jnp.zeros((SUPER - (j + 1) * chunk, GROUP_W), BF16))
                xw2 = jnp.concatenate(pieces, axis=0) if len(pieces) > 1 else xw
                st[g, i] = jnp.dot(bm_t[g, i], xw2, preferred_element_type=F32)
            for g, i in chains:
                h_scr[i, g] = h_t[g, i] * pre[i][3][a1 - 1:a1, gsl(g)] + st[g, i]

    for i in seqs:
        y = (ybuf[i] + dsk_ref[...] * pre[i][0]) * zs_ref[i].astype(F32)
        for g in range(SSD_GROUPS):
            gs = slice(g * GROUP_W, (g + 1) * GROUP_W)
            y_ref[i, :, gs] = _rms(y[:, gs], nw_ref[:, gs]).astype(BF16)

    @pl.when(t == pl.num_programs(1) - 1)
    def _fin():
        for i in seqs:
            for g in range(SSD_GROUPS):
                hout_ref[i, g] = h_scr[i, g].T


def _ssd_call(zs, xc, dt, h0_t, dtb, alog, dsk, nw, *, chunk, rows, real):
    b, t, _ = zs.shape
    ii = np.arange(rows)
    tri = ((ii[:, None] // chunk == ii[None, :] // chunk) & (ii[None, :] <= ii[:, None])).astype(np.float32)
    e = np.zeros((LANES, SSD_WIDTH), np.float32)
    for r in range(SSD_HEADS):
        e[r, r * SSD_HEADDIM:(r + 1) * SSD_HEADDIM] = 1.0

    nseq = math.gcd(b, SSD_NSEQ)

    def row(width):
        return pl.BlockSpec((nseq, rows, width), lambda i, j: (i, j, 0))

    state_spec = pl.BlockSpec((nseq, SSD_GROUPS, GROUP_W, SSD_STATE), lambda i, j: (i, 0, 0, 0))
    kern = functools.partial(_ssd_kernel, nseq=nseq, chunk=chunk, rows=rows, real=real)
    return pl.pallas_call(
        kern,
        grid=(b // nseq, t // rows),
        in_specs=[row(SSD_WIDTH), row(SSD_CONV_DIM), row(LANES), state_spec,
                  _const_spec((1, LANES)), _const_spec((1, LANES)),
                  _const_spec((1, SSD_WIDTH)), _const_spec((1, SSD_WIDTH)),
                  _const_spec((rows, 3 * rows)), _const_spec((3 * LANES, SSD_WIDTH))],
        out_specs=[row(SSD_WIDTH), state_spec],
        out_shape=[jax.ShapeDtypeStruct((b, t, SSD_WIDTH), BF16),
                   jax.ShapeDtypeStruct((b, SSD_GROUPS, GROUP_W, SSD_STATE), F32)],
        scratch_shapes=[pltpu.VMEM((nseq, SSD_GROUPS, SSD_STATE, GROUP_W), F32),
                        pltpu.VMEM((nseq, rows, SSD_WIDTH), F32)],
        compiler_params=pltpu.CompilerParams(dimension_semantics=("parallel", "arbitrary"),
                                             vmem_limit_bytes=VMEM_LIMIT),
        name="ssd",
    )(zs, xc, dt, h0_t, dtb, alog, dsk, nw,
      jnp.asarray(np.tile(tri, (1, 3)), BF16), jnp.asarray(np.tile(e, (3, 1)), BF16))


def _attn_kernel(scal_ref, qt_ref, k_ref, vt_ref, btab_ref, subw_ref, o_ref,
                 m_scr, l_scr, acc_scr, sa_scr, sb_scr, sc_scr, pa_scr, pb_scr, alpha_scr, qz_scr, bias_scr,
                 *, bq, bk, noff, out_scale):
    h = pl.program_id(0)
    qi = pl.program_id(2)
    kn0 = qi + (noff - 1)
    lam = scal_ref[0]
    cfar = scal_ref[1 + h]

    @pl.when((pl.program_id(1) == 0) & (qi == 0))
    def _build_bias_tiles():
        width = bq + bk
        kj = lax.broadcasted_iota(jnp.int32, (bk, bq), 0)
        qj = lax.broadcasted_iota(jnp.int32, (bk, bq), 1)
        cshift = CHUNK.bit_length() - 1
        for d in range(2):
            skew = pltpu.roll(jnp.broadcast_to(btab_ref[d], (bk, width)), 0, 1, stride=1, stride_axis=0)
            tile = skew[:, bk:width]
            if d == 1:
                tile = jnp.where((kj >> cshift) <= (qj >> cshift), tile, NEG)
            bias_scr[d] = tile

    zero = jnp.zeros((DA_DK, bq), BF16)
    qz_scr[0, 0:DA_DK, :] = qt_ref[0:DA_DK, :]
    qz_scr[0, DA_DK:DA_DV, :] = zero
    qz_scr[1, 0:DA_DK, :] = zero
    qz_scr[1, DA_DK:DA_DV, :] = qt_ref[DA_DK:DA_DV, :]

    m_scr[...] = jnp.full(m_scr.shape, NEG, F32)
    l_scr[...] = jnp.zeros(l_scr.shape, F32)
    acc_scr[...] = jnp.zeros(acc_scr.shape, F32)

    nsub = bk // ATT_KT

    def fold(x):
        return x.reshape(ATT_KT // SUBLANES, SUBLANES, bq)

    def scores(s_buf, first, count):
        for mm in range(2):
            for e in range(count):
                start = pl.multiple_of((first + e) * bk, bk)
                s_buf[mm, e] = jnp.dot(k_ref[pl.ds(start, bk), :], qz_scr[mm], preferred_element_type=F32)

    def softmax(s_buf, p_buf, entries):
        alphas = []
        for mm in range(2):
            cand = None
            for e, (near, shift) in enumerate(entries):
                mx = None
                for t in range(nsub):
                    rows = slice(t * ATT_KT, (t + 1) * ATT_KT)
                    s = s_buf[mm, e, rows, :]
                    if near is not None:
                        s = s + bias_scr[near, rows, :]
                    pm = jnp.max(fold(s), axis=0)
                    mx = pm if mx is None else jnp.maximum(mx, pm)
                mx = jnp.max(mx, axis=0, keepdims=True) + shift
                cand = mx if cand is None else jnp.maximum(cand, mx)
            m_old = m_scr[mm]
            m_new = jnp.maximum(m_old, cand)
            ls = None
            for e, (near, shift) in enumerate(entries):
                off = m_new - shift
                for t in range(nsub):
                    rows = slice(t * ATT_KT, (t + 1) * ATT_KT)
                    s = s_buf[mm, e, rows, :]
                    if near is not None:
                        s = s + bias_scr[near, rows, :]
                    p = jnp.exp2(s - off)
                    p_buf[mm, e * bk + t * ATT_KT:e * bk + (t + 1) * ATT_KT, :] = p.astype(BF16)
                    ps = jnp.sum(fold(p), axis=0)
                    ls = ps if ls is None else ls + ps
            alpha = jnp.exp2(m_old - m_new)
            l_scr[mm] = alpha * l_scr[mm] + jnp.sum(ls, axis=0, keepdims=True)
            m_scr[mm] = m_new
            alphas.append(alpha)
        return alphas

    def pv(p_buf, first, n):
        vts = [vt_ref[first + e] for e in range(n)]
        vt = jnp.concatenate(vts, axis=1) if n > 1 else vts[0]
        return [jnp.dot(vt, p_buf[mm, 0:n * bk, :], preferred_element_type=F32) for mm in range(2)]

    def accumulate(alphas, pvs):
        for mm in range(2):
            acc_scr[mm] = alphas[mm] * acc_scr[mm] + pvs[mm]

    def softmax_pv(s_buf, first, entries):
        alphas = softmax(s_buf, pa_scr, entries)
        accumulate(alphas, pv(pa_scr, first, len(entries)))

    nfar = jnp.maximum(kn0, 0)
    odd = nfar % 2
    far = (None, cfar)

    near_pair = [(0, 0.0), (1, 0.0)]

    @pl.when(kn0 < 0)
    def _only_first():
        scores(sc_scr, 0, 1)
        softmax_pv(sc_scr, 0, [(1, 0.0)])

    @pl.when(kn0 >= 0)
    def _groups():
        @pl.when(odd == 1)
        def _single():
            scores(sc_scr, 0, 1)
            scores(sa_scr, 1, 2)
            softmax_pv(sc_scr, 0, [far])

        @pl.when(odd == 0)
        def _first_pair():
            scores(sa_scr, 0, 2)

        def far_pair(s_cur, s_next, cur):
            scores(s_next, cur + 2, 2)
            softmax_pv(s_cur, cur, [far, far])

        def far_step(s_cur, p_cur, s_next, p_prev, cur):
            scores(s_next, cur + 2, 2)
            pending = None if p_prev is None else pv(p_prev, cur - 2, 2)
            alphas = softmax(s_cur, p_cur, [far, far])
            if pending is not None:
                accumulate([alpha_scr[0], alpha_scr[1]], pending)
            for mm in range(2):
                alpha_scr[mm] = alphas[mm]

        npairs = nfar // 2
        niter = npairs // 2

        @pl.when(niter >= 1)
        def _far_loop():
            far_step(sa_scr, pa_scr, sb_scr, None, odd)
            far_step(sb_scr, pb_scr, sa_scr, pa_scr, odd + 2)

            def far_body(j, carry):
                cur = odd + 4 * j
                far_step(sa_scr, pa_scr, sb_scr, pb_scr, cur)
                far_step(sb_scr, pb_scr, sa_scr, pa_scr, cur + 2)
                return carry

            lax.fori_loop(1, niter, far_body, 0)
            accumulate([alpha_scr[0], alpha_scr[1]], pv(pb_scr, odd + 4 * niter - 2, 2))

        @pl.when(npairs % 2 == 1)
        def _tail_b():
            far_pair(sa_scr, sb_scr, kn0 - 2)
            softmax_pv(sb_scr, kn0, near_pair)

        @pl.when(npairs % 2 == 0)
        def _tail_a():
            softmax_pv(sa_scr, kn0, near_pair)

    o = acc_scr[0] * (1.0 / l_scr[0]) - acc_scr[1] * (lam / l_scr[1])
    o = o * lax.rsqrt(jnp.mean(o * o, axis=0, keepdims=True) + EPS) * (subw_ref[...] * out_scale)
    o_ref[...] = o.T.astype(BF16)


def _attn_call(scal, qt, kb, vt, btab, subw, *, bq, bk, noff, out_scale):
    b, nq = qt.shape[:2]
    tk = kb.shape[1]
    nkb = vt.shape[1]
    kern = functools.partial(_attn_kernel, bq=bq, bk=bk, noff=noff, out_scale=out_scale)
    return pl.pallas_call(
        kern,
        grid=(DA_HEADS, b, nq),
        in_specs=[pl.BlockSpec(memory_space=pltpu.SMEM),
                  pl.BlockSpec((None, None, DA_DV, bq), lambda h, i, j: (i, j, h, 0)),
                  pl.BlockSpec((None, tk, DA_DV), lambda h, i, j: (i, 0, h)),
                  pl.BlockSpec((None, nkb, DA_DV, bk), lambda h, i, j: (i, 0, h, 0)),
                  pl.BlockSpec((None, 2, 1, bq + bk), lambda h, i, j: (h, 0, 0, 0)),
                  pl.BlockSpec((DA_DV, 1), lambda h, i, j: (0, 0))],
        out_specs=pl.BlockSpec((None, bq, DA_DV), lambda h, i, j: (i, j, h)),
        out_shape=jax.ShapeDtypeStruct((b, nq * bq, DA_WIDTH), BF16),
        scratch_shapes=[pltpu.VMEM((2, 1, bq), F32), pltpu.VMEM((2, 1, bq), F32),
                        pltpu.VMEM((2, DA_DV, bq), F32),
                        pltpu.VMEM((2, 2, bk, bq), F32), pltpu.VMEM((2, 2, bk, bq), F32),
                        pltpu.VMEM((2, 1, bk, bq), F32),
                        pltpu.VMEM((2, 2 * bk, bq), BF16), pltpu.VMEM((2, 2 * bk, bq), BF16),
                        pltpu.VMEM((2, 1, bq), F32),
                        pltpu.VMEM((2, DA_DV, bq), BF16),
                        pltpu.VMEM((2, bk, bq), F32)],
        compiler_params=pltpu.CompilerParams(dimension_semantics=("arbitrary", "arbitrary", "arbitrary"),
                                             vmem_limit_bytes=VMEM_LIMIT),
        name="attn",
    )(scal, qt, kb, vt, btab, subw)


def _attn_cached_kernel(scal_ref, qt_ref, kc_ref, vc_ref, kn_ref, vtn_ref, bias0_ref, bias1_ref, subw_ref, o_ref,
                        *, bq, tq, bk, out_scale):
    lam = scal_ref[0]
    past = kc_ref.shape[0] // DA_HEADS
    lane = lax.broadcasted_iota(jnp.int32, (tq, DA_DV), 1)
    nt = (((1,), (1,)), ((), ()))

    for h in range(DA_HEADS):
        hs = slice(h * DA_DV, (h + 1) * DA_DV)
        cfar = scal_ref[1 + h]
        qn = qt_ref[hs, :].astype(F32).T[0:tq, :]
        q2 = jnp.concatenate([jnp.where(lane < DA_DK, qn, 0.0), jnp.where(lane >= DA_DK, qn, 0.0)],
                             axis=0).astype(BF16)
        head_rows = pl.ds(h, past, stride=DA_HEADS)
        s_c = lax.dot_general(q2, kc_ref[head_rows, :].astype(BF16), nt, preferred_element_type=F32)
        s_n = lax.dot_general(q2, kn_ref[:, hs], nt, preferred_element_type=F32)
        b0 = bias0_ref[h]
        b1 = bias1_ref[h]
        s = jnp.concatenate([s_c[:, 0:past - bk] + cfar,
                             s_c[:, past - bk:past] + jnp.concatenate([b0, b0], axis=0),
                             s_n + jnp.concatenate([b1, b1], axis=0)], axis=1)
        p = jnp.exp2(s - jnp.max(s, axis=1, keepdims=True))
        inv = 1.0 / jnp.sum(p, axis=1, keepdims=True)
        pb = p.astype(BF16)
        acc = (jnp.dot(pb[:, 0:past], vc_ref[head_rows, :].astype(BF16), preferred_element_type=F32)
               + jnp.dot(pb[:, past:past + bq], vtn_ref[hs, :].astype(F32).T.astype(BF16),
                         preferred_element_type=F32))
        o = acc[0:tq, :] * inv[0:tq] - acc[tq:2 * tq, :] * (lam * inv[tq:2 * tq])
        o_ref[0:tq, hs] = (_rms(o, subw_ref[...]) * out_scale).astype(BF16)
        o_ref[tq:bq, hs] = jnp.zeros((bq - tq, DA_DV), BF16)


def _attn_cached_call(scal, qt, cache_k, cache_v, kb, vt, bias0, bias1, subw, *, tq, bk, out_scale):
    b, past = cache_k.shape[:2]
    bq = qt.shape[-1]
    rows = past * DA_HEADS
    kern = functools.partial(_attn_cached_kernel, bq=bq, tq=tq, bk=bk, out_scale=out_scale)
    cache_spec = pl.BlockSpec((None, rows, DA_DV), lambda i: (i, 0, 0))
    return pl.pallas_call(
        kern,
        grid=(b,),
        in_specs=[pl.BlockSpec(memory_space=pltpu.SMEM),
                  pl.BlockSpec((None, None, DA_WIDTH, bq), lambda i: (i, 0, 0, 0)),
                  cache_spec, cache_spec,
                  pl.BlockSpec((None, bq, DA_WIDTH), lambda i: (i, 0, 0)),
                  pl.BlockSpec((None, None, DA_WIDTH, bq), lambda i: (i, 0, 0, 0)),
                  _const_spec((DA_HEADS, tq, bk)), _const_spec((DA_HEADS, tq, bq)),
                  _const_spec((1, DA_DV))],
        out_specs=pl.BlockSpec((None, bq, DA_WIDTH), lambda i: (i, 0, 0)),
        out_shape=jax.ShapeDtypeStruct((b, bq, DA_WIDTH), BF16),
        compiler_params=pltpu.CompilerParams(dimension_semantics=("parallel",), vmem_limit_bytes=VMEM_LIMIT),
        name="attn_cached",
    )(scal, qt, cache_k.reshape(b, rows, DA_DV), cache_v.reshape(b, rows, DA_DV), kb, vt, bias0, bias1, subw)


def _rel_bucket(rel):
    nb = REL_BUCKETS // 2
    max_exact = nb // 2
    n = jnp.abs(rel)
    nf = jnp.maximum(n, 1).astype(jnp.float32)
    large = max_exact + (jnp.log(nf / max_exact) / math.log(REL_MAX_DIST / max_exact)
                         * (nb - max_exact)).astype(jnp.int32)
    large = jnp.minimum(large, nb - 1)
    return jnp.where(rel > 0, nb, 0) + jnp.where(n < max_exact, n, large)


def _bias_tables(rel_bias, bq, bk):
    tabs = []
    for d in range(2):
        offs = (d - 1) * bk + bk - np.arange(bq + bk)
        tabs.append(rel_bias[_rel_bucket(jnp.asarray(offs, jnp.int32))].astype(F32).T * LOG2E)
    return jnp.stack(tabs, axis=1)[:, :, None, :]


def _bias_rows(rel_bias, qpos0, tq, kpos0, nkeys, tk_real):
    offs = (kpos0 - qpos0) + np.arange(-(tq - 1), nkeys)
    table = rel_bias[_rel_bucket(jnp.asarray(offs, jnp.int32))].astype(F32).T * LOG2E
    rows = jnp.stack([table[:, tq - 1 - i:tq - 1 - i + nkeys] for i in range(tq)], axis=1)
    qpos = qpos0 + np.arange(tq)
    kpos = kpos0 + np.arange(nkeys)
    vis = (kpos[None, :] // CHUNK <= qpos[:, None] // CHUNK) & (kpos[None, :] < tk_real)
    return jnp.where(jnp.asarray(vis)[None], rows, NEG)


def _ffn_kernel(x_ref, ys_ref, ya_ref, mod_ref, wo1_ref, wo2_ref, nfw_ref, wu_ref, cw_ref, cb_ref, wd_ref, fw_ref,
                hist_ref, y_ref, tail_ref, tail_scr, buf_a, buf_b, x1_scr, h2_scr, act_scr, *, tm, real):
    t = pl.program_id(1)
    nh = FFN_CONV - 1
    lo = SUBLANES - nh

    @pl.when(t == 0)
    def _init():
        tail_scr[lo:SUBLANES, :] = hist_ref[...]

    mix = (jnp.dot(ys_ref[...], wo1_ref[...], preferred_element_type=F32)
           + jnp.dot(ya_ref[...], wo2_ref[...], preferred_element_type=F32))
    x1 = x_ref[...] + mod_ref[2:3, :] * mix
    x1_scr[...] = x1
    h2 = _rms(x1, nfw_ref[...]) * (1.0 + mod_ref[4:5, :]) + mod_ref[3:4, :]
    h2_scr[...] = h2.astype(BF16)

    def cols(j, half):
        return slice(half * D_FF + j * FFN_CN, half * D_FF + (j + 1) * FFN_CN)

    def up(j, buf):
        for half in range(2):
            buf[half, SUBLANES:SUBLANES + tm, :] = jnp.dot(h2_scr[...], wu_ref[:, cols(j, half)],
                                                            preferred_element_type=F32)

    def conv(j, buf, half):
        cs = cols(j, half)
        buf[half, lo:SUBLANES, :] = tail_scr[lo:SUBLANES, cs]
        c = cb_ref[:, cs]
        for i in range(FFN_CONV):
            c = c + cw_ref[i:i + 1, cs] * buf[half, lo + i:lo + i + tm, :]
        tail_scr[lo:SUBLANES, cs] = buf[half, lo + real:SUBLANES + real, :]
        return c

    bufs = (buf_a, buf_b)
    up(0, bufs[0])
    for j in range(FFN_NC):
        if j + 1 < FFN_NC:
            up(j + 1, bufs[(j + 1) % 2])
        cv = conv(j, bufs[j % 2], 0)
        cg = conv(j, bufs[j % 2], 1)
        act_scr[:, j * FFN_CN:(j + 1) * FFN_CN] = (_silu(cg) * cv).astype(BF16)

    f = jnp.dot(act_scr[...], wd_ref[...], preferred_element_type=F32)
    x2 = x1_scr[...] + mod_ref[5:6, :] * f
    y_ref[...] = _rms(x2, fw_ref[...])
    tail_ref[...] = tail_scr[lo:SUBLANES, :]


def _ffn_call(x, ys, ya, mod3, wo1, wo2, nfw, wu, cw, cb, wd, fw, hist, *, tm, real):
    b, t, d = x.shape
    nh = FFN_CONV - 1

    def row(width):
        return pl.BlockSpec((None, tm, width), lambda i, j: (i, j, 0))

    hist_spec = pl.BlockSpec((None, nh, 2 * D_FF), lambda i, j: (i, 0, 0))
    kern = functools.partial(_ffn_kernel, tm=tm, real=real)
    return pl.pallas_call(
        kern,
        grid=(b, t // tm),
        in_specs=[row(d), row(SSD_WIDTH), row(DA_WIDTH),
                  pl.BlockSpec((None, 6, d), lambda i, j: (i, 0, 0)),
                  _const_spec((SSD_WIDTH, d)), _const_spec((DA_WIDTH, d)), _const_spec((1, d)),
                  _const_spec((d, 2 * D_FF)), _const_spec((FFN_CONV, 2 * D_FF)), _const_spec((1, 2 * D_FF)),
                  _const_spec((D_FF, d)), _const_spec((1, d)),
                  hist_spec],
        out_specs=[row(d), hist_spec],
        out_shape=[jax.ShapeDtypeStruct((b, t, d), F32),
                   jax.ShapeDtypeStruct((b, nh, 2 * D_FF), F32)],
        scratch_shapes=[pltpu.VMEM((SUBLANES, 2 * D_FF), F32),
                        pltpu.VMEM((2, tm + SUBLANES, FFN_CN), F32), pltpu.VMEM((2, tm + SUBLANES, FFN_CN), F32),
                        pltpu.VMEM((tm, d), F32), pltpu.VMEM((tm, d), BF16), pltpu.VMEM((tm, D_FF), BF16)],
        compiler_params=pltpu.CompilerParams(dimension_semantics=("parallel", "arbitrary"),
                                             vmem_limit_bytes=VMEM_LIMIT),
        name="ffn",
    )(x, ys, ya, mod3, wo1, wo2, nfw, wu, cw, cb, wd, fw, hist)


def _pack_params(norm_mix_w, w_in, ssm_conv_w, ssm_conv_b, ssm_dt_bias, ssm_a_log, ssm_d, ssm_norm_w,
                 lambda_q1, lambda_k1, lambda_q2, lambda_k2, attn_subln_w, rel_bias, w_out,
                 norm_ffn_w, w_up, ffn_conv_w, ffn_conv_b, w_down, final_norm_w, layer):
    l = layer
    wz, wx, wdt, wq, wk, wv = jnp.split(w_in[l], IN_SPLITS, axis=-1)
    wdt = jnp.pad(wdt, ((0, 0), (0, LANES - SSD_HEADS)))
    w_cat = jnp.concatenate([wz, wx, wdt, wk, wv], axis=-1).astype(BF16)
    w_t = jnp.concatenate([wq * (DA_DK ** -0.5 * LOG2E), wv], axis=-1).T.astype(BF16)

    def pad_heads(v):
        return jnp.pad(v.astype(F32), (0, LANES - SSD_HEADS)).reshape(1, LANES)

    lam_init = 0.8 - 0.6 * math.exp(-0.3 * l)
    lam = (jnp.exp(jnp.sum(lambda_q1[l].astype(F32) * lambda_k1[l].astype(F32)))
           - jnp.exp(jnp.sum(lambda_q2[l].astype(F32) * lambda_k2[l].astype(F32))) + lam_init)
    far_bias = rel_bias[REL_BUCKETS // 2 - 1].astype(F32)
    return dict(
        norm_mix_w=norm_mix_w[l].reshape(1, D_MODEL), w_cat=w_cat, w_t=w_t,
        cw=ssm_conv_w[l], cbias=ssm_conv_b[l].reshape(1, SSD_CONV_DIM),
        dtb=pad_heads(ssm_dt_bias[l]), alog=pad_heads(ssm_a_log[l]),
        dsk=jnp.repeat(ssm_d[l].astype(F32), SSD_HEADDIM).reshape(1, SSD_WIDTH),
        ssm_nw=ssm_norm_w[l].reshape(1, SSD_WIDTH),
        scal=jnp.concatenate([lam.reshape(1), far_bias * LOG2E]).astype(F32), lam_init=lam_init,
        subw=attn_subln_w[l].reshape(DA_DV, 1), rel_bias=rel_bias,
        wo1=w_out[l][:SSD_WIDTH].astype(BF16), wo2=w_out[l][SSD_WIDTH:].astype(BF16),
        nfw=norm_ffn_w[l].reshape(1, D_MODEL),
        wu=w_up[l].astype(BF16), ffn_cw=ffn_conv_w[l], ffn_cb=ffn_conv_b[l].reshape(1, 2 * D_FF),
        wd=w_down[l].astype(BF16), fw=final_norm_w.reshape(1, D_MODEL),
    )


def _state_to_kernel(h):
    return h.reshape(h.shape[0], SSD_GROUPS, GROUP_W, SSD_STATE)


def _state_from_kernel(h):
    return h.reshape(h.shape[0], SSD_HEADS, SSD_HEADDIM, SSD_STATE)


def _run_group(x, mod, past_k, past_v, ssm_h0, ssm_conv_hist, ffn_conv_hist, p, *, tm, ssd_rows, bq, bk):
    b, t, d = x.shape
    past = 0 if past_k is None else past_k.shape[1]
    chunk = min(CHUNK, t)
    tp = max(t, SUPER)
    if tp != t:
        x = jnp.pad(x, ((0, 0), (0, tp - t), (0, 0)))
        tm = ssd_rows = bq = tp
    mod3 = mod.reshape(b, 6, d)

    assert tm == bq
    zs, xc, dt, k, v, kb, qt, vt, conv_new = _inproj_call(
        x, mod3, p["norm_mix_w"], p["w_cat"], p["w_t"], ssm_conv_hist.astype(F32), p["cw"], p["cbias"],
        tm=tm, real=min(t, tm))

    y_ssd, h_t = _ssd_call(zs, xc, dt, _state_to_kernel(ssm_h0.astype(F32)),
                           p["dtb"], p["alog"], p["dsk"], p["ssm_nw"],
                           chunk=chunk, rows=ssd_rows, real=min(t, ssd_rows))

    if past == 0:
        assert bq == bk and t % bq == 0
        y_att = _attn_call(p["scal"], qt, kb, vt, _bias_tables(p["rel_bias"], bq, bk), p["subw"], bq=bq, bk=bk, noff=0,
                           out_scale=1.0 - p["lam_init"])
    else:
        assert past % bk == 0 and past >= bk and bq == tp <= bk
        bias0 = _bias_rows(p["rel_bias"], past, t, past - bk, bk, past + t)
        bias1 = _bias_rows(p["rel_bias"], past, t, past, bq, past + t)
        y_att = _attn_cached_call(p["scal"], qt, past_k, past_v, kb, vt, bias0, bias1, p["subw"].reshape(1, DA_DV),
                                  tq=t, bk=bk, out_scale=1.0 - p["lam_init"])

    y, ffn_new = _ffn_call(x, y_ssd, y_att, mod3, p["wo1"], p["wo2"], p["nfw"], p["wu"], p["ffn_cw"], p["ffn_cb"],
                           p["wd"], p["fw"], ffn_conv_hist.astype(F32), tm=tm, real=min(t, tm))
    k = k.reshape(b, tp, DA_HEADS, 2 * DA_DK)
    v = v.reshape(b, tp, DA_HEADS, DA_DV)
    return (y[:, :t], k[:, :t], v[:, :t],
            _state_from_kernel(h_t), conv_new, ffn_new)


def kernel(x_prompt, x_sample, c_prompt, c_sample, cache_k, cache_v, state_ssm, state_ssm_conv, state_ffn_conv, w_ada, b_ada, norm_mix_w, w_in, ssm_conv_w, ssm_conv_b, ssm_dt_bias, ssm_a_log, ssm_d, ssm_norm_w, lambda_q1, lambda_k1, lambda_q2, lambda_k2, attn_subln_w, rel_bias, w_out, norm_ffn_w, w_up, ffn_conv_w, ffn_conv_b, w_down, final_norm_w):
    bp, bs = x_prompt.shape[0], x_sample.shape[0]
    dt = x_prompt.dtype
    p = _pack_params(norm_mix_w, w_in, ssm_conv_w, ssm_conv_b, ssm_dt_bias, ssm_a_log, ssm_d, ssm_norm_w,
                     lambda_q1, lambda_k1, lambda_q2, lambda_k2, attn_subln_w, rel_bias, w_out,
                     norm_ffn_w, w_up, ffn_conv_w, ffn_conv_b, w_down, final_norm_w, 0)
    c_all = jnp.concatenate([c_prompt, c_sample], axis=0)
    npad = -c_all.shape[0] % SUBLANES
    c_all = jnp.pad(c_all, ((0, npad), (0, 0)))
    mod = _mod_call(c_all, w_ada[0], b_ada[0].reshape(1, -1))

    zeros = lambda *s: jnp.zeros(s, dt)
    out_p = _run_group(x_prompt, mod[:bp], None, None,
                       zeros(bp, SSD_HEADS, SSD_HEADDIM, SSD_STATE), zeros(bp, SSD_CONV - 1, SSD_CONV_DIM),
                       zeros(bp, FFN_CONV - 1, 2 * D_FF), p, tm=512, ssd_rows=256, bq=512, bk=512)
    out_s = _run_group(x_sample, mod[bp:bp + bs], cache_k[0], cache_v[0], state_ssm[0], state_ssm_conv[0],
                       state_ffn_conv[0], p, tm=SUPER, ssd_rows=SUPER, bq=SUPER, bk=512)
    y_p, k_p, v_p, h_p, c_p, f_p = out_p
    y_s, k_s, v_s, h_s, c_s, f_s = out_s
    return (y_p, y_s, k_p[None], v_p[None], h_p[None], c_p[None], f_p[None],
            k_s[None], v_s[None], h_s[None], c_s[None], f_s[None])
```

```python
import functools
import math

import numpy as np
import jax
import jax.numpy as jnp
from jax import lax
from jax.experimental import pallas as pl
from jax.experimental.pallas import tpu as pltpu

F32 = jnp.float32
BF16 = jnp.bfloat16
HIGHEST = lax.Precision.HIGHEST

D_MODEL = 1024
CHUNK = 64
SSD_WIDTH = 512
SSD_HEADDIM = 64
SSD_HEADS = 8
SSD_GROUPS = 2
SSD_HPG = 4
SSD_STATE = 128
SSD_CONV = 4
SSD_CONV_DIM = SSD_WIDTH + 2 * SSD_GROUPS * SSD_STATE
GROUP_W = SSD_HPG * SSD_HEADDIM
DA_WIDTH = 512
DA_DK = 64
DA_DV = 128
DA_HEADS = 4
REL_BUCKETS = 32
REL_MAX_DIST = 128
D_FF = 2816
FFN_CONV = 3
EPS = 1e-6
IN_SPLITS = (512, 1536, 1544, 2056, 2568)
LANES = 128
SUBLANES = 8
SUPER = 128
SSD_NSEQ = 4
ATT_KT = 64
ATT_TILES = 16
FFN_CN = 256
FFN_NC = D_FF // FFN_CN
NEG = -1e30
VMEM_LIMIT = 56 * 1024 * 1024

PZ, PX, PDT, PK, PV, PEND = 0, 512, 1536, 1664, 2176, 2688
LOG2E = math.log2(math.e)


def _silu(x):
    return x / (1.0 + jnp.exp(-x))


def _softplus(x):
    return jnp.maximum(x, 0.0) + jnp.log1p(jnp.exp(-jnp.abs(x)))


def _split3(x):
    hi = x.astype(BF16)
    r1 = x - hi.astype(F32)
    mid = r1.astype(BF16)
    lo = (r1 - mid.astype(F32)).astype(BF16)
    return hi, mid, lo


def _rms(x, w):
    return x * lax.rsqrt(jnp.mean(x * x, axis=-1, keepdims=True) + EPS) * w


def _const_spec(shape):
    nd = len(shape)
    return pl.BlockSpec(shape, lambda *_: (0,) * nd)


def _mod_kernel(c_ref, w_ref, b_ref, o_ref):
    a = _silu(c_ref[...]).astype(BF16)
    o_ref[...] = jnp.dot(a, w_ref[...].astype(BF16), preferred_element_type=F32) + b_ref[...]


def _mod_call(c, w_ada, b_ada):
    n, d = c.shape
    nout = w_ada.shape[1]
    tn = 1024
    return pl.pallas_call(
        _mod_kernel,
        grid=(nout // tn,),
        in_specs=[pl.BlockSpec((n, d), lambda j: (0, 0)),
                  pl.BlockSpec((d, tn), lambda j: (0, j)),
                  pl.BlockSpec((1, tn), lambda j: (0, j))],
        out_specs=pl.BlockSpec((n, tn), lambda j: (0, j)),
        out_shape=jax.ShapeDtypeStruct((n, nout), F32),
        name="mod",
    )(c, w_ada, b_ada)


def _inproj_kernel(x_ref, mod_ref, nw_ref, w_ref, wt_ref, hist_ref, cw_ref, cbias_ref,
                   zs_ref, xc_ref, dt_ref, k_ref, v_ref, kb_ref, qt_ref, vt_ref, cout_ref, cbuf, hb_scr, zbuf,
                   *, tm, real):
    t = pl.program_id(1)
    nconv = SSD_CONV - 1

    @pl.when(t == 0)
    def _init():
        cbuf[0:SUBLANES, :] = jnp.zeros((SUBLANES, SSD_CONV_DIM), F32)
        cbuf[SUBLANES - nconv:SUBLANES, :] = hist_ref[...]

    h = _rms(x_ref[...], nw_ref[...]) * (1.0 + mod_ref[1:2, :]) + mod_ref[0:1, :]
    hb_scr[...] = h.astype(BF16)

    def proj(a, b):
        return jnp.dot(hb_scr[...], w_ref[:, a:b], preferred_element_type=F32)

    def proj_t(a, b):
        return lax.dot_general(wt_ref[a:b, :], hb_scr[...], (((1,), (1,)), ((), ())), preferred_element_type=F32)

    cbuf[SUBLANES:SUBLANES + tm, :] = proj(PX, PDT)
    zbuf[...] = proj(PZ, PX)
    dt_ref[...] = proj(PDT, PK)
    k = proj(PK, PV)
    v = proj(PV, PEND)
    for hd in range(DA_HEADS):
        dst = pl.ds(hd, tm, stride=DA_HEADS)
        k_ref[dst, :] = k[:, hd * DA_DV:(hd + 1) * DA_DV]
        v_ref[dst, :] = v[:, hd * DA_DV:(hd + 1) * DA_DV]
    kb_ref[...] = k.astype(BF16)
    qt_ref[...] = proj_t(0, DA_WIDTH).astype(BF16)
    vt_ref[...] = proj_t(DA_WIDTH, 2 * DA_WIDTH).astype(BF16)
    conv = cbias_ref[...]
    for j in range(SSD_CONV):
        off = SUBLANES - nconv + j
        conv = conv + cw_ref[j:j + 1, :] * cbuf[off:off + tm, :]
    tail = cbuf[SUBLANES - nconv + real:SUBLANES + real, :]
    cout_ref[...] = tail
    cbuf[SUBLANES - nconv:SUBLANES, :] = tail
    xc_ref[...] = _silu(conv).astype(BF16)
    zs_ref[...] = _silu(zbuf[...]).astype(BF16)


def _inproj_call(x, mod3, norm_w, w_cat, w_t, hist, cw, cbias, *, tm, real):
    b, t, d = x.shape
    nt = t // tm

    def row(width):
        return pl.BlockSpec((None, tm, width), lambda i, j: (i, j, 0))

    def out(width, dtype):
        return jax.ShapeDtypeStruct((b, t, width), dtype)

    tspec = pl.BlockSpec((None, None, DA_WIDTH, tm), lambda i, j: (i, j, 0, 0))
    tshape = jax.ShapeDtypeStruct((b, nt, DA_WIDTH, tm), BF16)
    hist_spec = pl.BlockSpec((None, SSD_CONV - 1, SSD_CONV_DIM), lambda i, j: (i, 0, 0))
    hspec = pl.BlockSpec((None, tm * DA_HEADS, DA_DV), lambda i, j: (i, j, 0))
    hshape = jax.ShapeDtypeStruct((b, t * DA_HEADS, DA_DV), F32)
    return pl.pallas_call(
        functools.partial(_inproj_kernel, tm=tm, real=real),
        grid=(b, nt),
        in_specs=[row(d),
                  pl.BlockSpec((None, 6, d), lambda i, j: (i, 0, 0)),
                  _const_spec((1, d)),
                  _const_spec((d, PEND)),
                  _const_spec((2 * DA_WIDTH, d)),
                  hist_spec, _const_spec((SSD_CONV, SSD_CONV_DIM)), _const_spec((1, SSD_CONV_DIM))],
        out_specs=[row(512), row(1024), row(LANES), hspec, hspec, row(512), tspec, tspec, hist_spec],
        out_shape=[out(512, BF16), out(1024, BF16), out(LANES, F32),
                   hshape, hshape, out(512, BF16), tshape, tshape,
                   jax.ShapeDtypeStruct((b, SSD_CONV - 1, SSD_CONV_DIM), F32)],
        scratch_shapes=[pltpu.VMEM((tm + SUBLANES, SSD_CONV_DIM), F32), pltpu.VMEM((tm, d), BF16),
                        pltpu.VMEM((tm, SSD_WIDTH), F32)],
        compiler_params=pltpu.CompilerParams(dimension_semantics=("parallel", "arbitrary"),
                                             vmem_limit_bytes=VMEM_LIMIT),
        name="inproj",
    )(x, mod3, norm_w, w_cat, w_t, hist, cw, cbias)


def _ssd_kernel(zs_ref, xc_ref, dt_ref, h0_ref, dtb_ref, alog_ref, dsk_ref, nw_ref, tri_ref, e_ref,
                y_ref, hout_ref, h_scr, ybuf, *, nseq, chunk, rows, real):
    t = pl.program_id(1)
    seqs = range(nseq)

    @pl.when(t == 0)
    def _init():
        for i in seqs:
            for g in range(SSD_GROUPS):
                h_scr[i, g] = h0_ref[i, g].T

    li = lax.broadcasted_iota(jnp.int32, (SUPER, SUPER), 0)
    si = lax.broadcasted_iota(jnp.int32, (SUPER, SUPER), 1)
    cshift = chunk.bit_length() - 1
    mask2 = ((li >> cshift) == (si >> cshift)) & (si <= li)
    lane_g = lax.broadcasted_iota(jnp.int32, (SUPER, GROUP_W), 1) >> (SSD_HEADDIM.bit_length() - 1)

    pre = []
    for i in seqs:
        xs = xc_ref[i, :, 0:SSD_WIDTH].astype(F32)
        dtv = _softplus(dt_ref[i] + dtb_ref[...])
        da = dtv * (-jnp.exp(alog_ref[...]))
        acs = jnp.dot(tri_ref[...], jnp.concatenate(_split3(da), axis=0), preferred_element_type=F32)
        dt_x = jnp.dot(jnp.concatenate(_split3(dtv), axis=1), e_ref[...], preferred_element_type=F32)
        acs_x = jnp.dot(jnp.concatenate(_split3(acs), axis=1), e_ref[...], preferred_element_type=F32)
        pre.append((xs, acs, acs_x, jnp.exp(acs_x), xs * dt_x))

    for sb in range(rows // SUPER):
        o = sb * SUPER
        nreal = (min(real, o + SUPER) - o) // chunk
        acs2 = [pre[i][1][o:o + SUPER, :] for i in seqs]
        acs_t = [a.T for a in acs2]
        chains = [(g, i) for g in range(SSD_GROUPS) for i in seqs]

        def gsl(g):
            return slice(g * GROUP_W, (g + 1) * GROUP_W)

        cmb, cb2, bm_t = {}, {}, {}
        for g, i in chains:
            bcol = SSD_WIDTH + g * SSD_STATE
            ccol = SSD_WIDTH + (SSD_GROUPS + g) * SSD_STATE
            bmb = xc_ref[i, o:o + SUPER, bcol:bcol + SSD_STATE]
            cmb[g, i] = xc_ref[i, o:o + SUPER, ccol:ccol + SSD_STATE]
            cb2[g, i] = lax.dot_general(cmb[g, i], bmb, (((1,), (1,)), ((), ())), preferred_element_type=F32)
            bm_t[g, i] = bmb.astype(F32).T.astype(BF16)
        for g, i in chains:
            ms = []
            for rr in range(SSD_HPG):
                r = g * SSD_HPG + rr
                seg = acs2[i][:, r:r + 1] - acs_t[i][r:r + 1, :]
                dec = jnp.where(mask2, jnp.exp(jnp.where(mask2, seg, 0.0)), 0.0)
                ms.append((cb2[g, i] * dec).astype(BF16))
            full = jnp.dot(jnp.concatenate(ms, axis=0), pre[i][4][o:o + SUPER, gsl(g)].astype(BF16),
                           preferred_element_type=F32)
            ydiag = full[0:SUPER]
            for rr in range(1, SSD_HPG):
                ydiag = jnp.where(lane_g == rr, full[rr * SUPER:(rr + 1) * SUPER], ydiag)
            ybuf[i, o:o + SUPER, gsl(g)] = ydiag
        for j in range(nreal):
            a0, a1 = o + j * chunk, o + (j + 1) * chunk
            h_t = {c: h_scr[c[1], c[0]] for c in chains}
            yoff = {(g, i): jnp.dot(cmb[g, i][j * chunk:(j + 1) * chunk, :], h_t[g, i].astype(BF16),
                                    preferred_element_type=F32) for g, i in chains}
            st = {}
            for g, i in chains:
                _, _, acs_x, eacs_x, xd = pre[i]
                ybuf[i, a0:a1, gsl(g)] = ybuf[i, a0:a1, gsl(g)] + yoff[g, i] * eacs_x[a0:a1, gsl(g)]
                dte = jnp.exp(acs_x[a1 - 1:a1, gsl(g)] - acs_x[a0:a1, gsl(g)])
                xw = (xd[a0:a1, gsl(g)] * dte).astype(BF16)
                pieces = []
                if j > 0:
                    pieces.append(jnp.zeros((j * chunk, GROUP_W), BF16))
                pieces.append(xw)
                if (j + 1) * chunk < SUPER:
                    pieces.append(jnp.zeros((SUPER - (j + 1) * chunk, GROUP_W), BF16))
                xw2 = jnp.concatenate(pieces, axis=0) if len(pieces) > 1 else xw
                st[g, i] = jnp.dot(bm_t[g, i], xw2, preferred_element_type=F32)
            for g, i in chains:
                h_scr[i, g] = h_t[g, i] * pre[i][3][a1 - 1:a1, gsl(g)] + st[g, i]

    for i in seqs:
        y = (ybuf[i] + dsk_ref[...] * pre[i][0]) * zs_ref[i].astype(F32)
        for g in range(SSD_GROUPS):
            gs = slice(g * GROUP_W, (g + 1) * GROUP_W)
            y_ref[i, :, gs] = _rms(y[:, gs], nw_ref[:, gs]).astype(BF16)

    @pl.when(t == pl.num_programs(1) - 1)
    def _fin():
        for i in seqs:
            for g in range(SSD_GROUPS):
                hout_ref[i, g] = h_scr[i, g].T


def _ssd_call(zs, xc, dt, h0_t, dtb, alog, dsk, nw, *, chunk, rows, real):
    b, t, _ = zs.shape
    ii = np.arange(rows)
    tri = ((ii[:, None] // chunk == ii[None, :] // chunk) & (ii[None, :] <= ii[:, None])).astype(np.float32)
    e = np.zeros((LANES, SSD_WIDTH), np.float32)
    for r in range(SSD_HEADS):
        e[r, r * SSD_HEADDIM:(r + 1) * SSD_HEADDIM] = 1.0

    nseq = math.gcd(b, SSD_NSEQ)

    def row(width):
        return pl.BlockSpec((nseq, rows, width), lambda i, j: (i, j, 0))

    state_spec = pl.BlockSpec((nseq, SSD_GROUPS, GROUP_W, SSD_STATE), lambda i, j: (i, 0, 0, 0))
    kern = functools.partial(_ssd_kernel, nseq=nseq, chunk=chunk, rows=rows, real=real)
    return pl.pallas_call(
        kern,
        grid=(b // nseq, t // rows),
        in_specs=[row(SSD_WIDTH), row(SSD_CONV_DIM), row(LANES), state_spec,
                  _const_spec((1, LANES)), _const_spec((1, LANES)),
                  _const_spec((1, SSD_WIDTH)), _const_spec((1, SSD_WIDTH)),
                  _const_spec((rows, 3 * rows)), _const_spec((3 * LANES, SSD_WIDTH))],
        out_specs=[row(SSD_WIDTH), state_spec],
        out_shape=[jax.ShapeDtypeStruct((b, t, SSD_WIDTH), BF16),
                   jax.ShapeDtypeStruct((b, SSD_GROUPS, GROUP_W, SSD_STATE), F32)],
        scratch_shapes=[pltpu.VMEM((nseq, SSD_GROUPS, SSD_STATE, GROUP_W), F32),
                        pltpu.VMEM((nseq, rows, SSD_WIDTH), F32)],
        compiler_params=pltpu.CompilerParams(dimension_semantics=("parallel", "arbitrary"),
                                             vmem_limit_bytes=VMEM_LIMIT),
        name="ssd",
    )(zs, xc, dt, h0_t, dtb, alog, dsk, nw,
      jnp.asarray(np.tile(tri, (1, 3)), BF16), jnp.asarray(np.tile(e, (3, 1)), BF16))


def _attn_kernel(scal_ref, qt_ref, k_ref, vt_ref, btab_ref, subw_ref, o_ref, *scratch, tiles, bq, bk, noff, out_scale):
    first = pl.program_id(2) * tiles

    def tile(sub, carry):
        rows = pl.ds(pl.multiple_of(sub * bq, bq), bq)
        _attn_tile(first + sub, scal_ref, qt_ref.at[sub], k_ref, vt_ref, btab_ref, subw_ref, o_ref.at[rows], *scratch,
                   bq=bq, bk=bk, noff=noff, out_scale=out_scale)
        return carry

    lax.fori_loop(0, tiles, tile, 0)


def _attn_tile(qi, scal_ref, qt_ref, k_ref, vt_ref, btab_ref, subw_ref, o_ref,
               m_scr, l_scr, acc_scr, sa_scr, sb_scr, sc_scr, pa_scr, pb_scr, alpha_scr, qz_scr, bias_scr,
               *, bq, bk, noff, out_scale):
    h = pl.program_id(0)
    kn0 = qi + (noff - 1)
    lam = scal_ref[0]
    cfar = scal_ref[1 + h]

    @pl.when((pl.program_id(1) == 0) & (qi == 0))
    def _build_bias_tiles():
        width = bq + bk
        kj = lax.broadcasted_iota(jnp.int32, (bk, bq), 0)
        qj = lax.broadcasted_iota(jnp.int32, (bk, bq), 1)
        cshift = CHUNK.bit_length() - 1
        for d in range(2):
            skew = pltpu.roll(jnp.broadcast_to(btab_ref[d], (bk, width)), 0, 1, stride=1, stride_axis=0)
            tile = skew[:, bk:width]
            if d == 1:
                tile = jnp.where((kj >> cshift) <= (qj >> cshift), tile, NEG)
            bias_scr[d] = tile

    zero = jnp.zeros((DA_DK, bq), BF16)
    qz_scr[0, 0:DA_DK, :] = qt_ref[0:DA_DK, :]
    qz_scr[0, DA_DK:DA_DV, :] = zero
    qz_scr[1, 0:DA_DK, :] = zero
    qz_scr[1, DA_DK:DA_DV, :] = qt_ref[DA_DK:DA_DV, :]

    m_scr[...] = jnp.full(m_scr.shape, NEG, F32)
    l_scr[...] = jnp.zeros(l_scr.shape, F32)
    acc_scr[...] = jnp.zeros(acc_scr.shape, F32)

    nsub = bk // ATT_KT

    def fold(x):
        return x.reshape(ATT_KT // SUBLANES, SUBLANES, bq)

    def scores(s_buf, first, count):
        for mm in range(2):
            for e in range(count):
                start = pl.multiple_of((first + e) * bk, bk)
                s_buf[mm, e] = jnp.dot(k_ref[pl.ds(start, bk), :], qz_scr[mm], preferred_element_type=F32)

    def softmax(s_buf, p_buf, entries):
        alphas = []
        for mm in range(2):
            cand = None
            for e, (near, shift) in enumerate(entries):
                mx = None
                for t in range(nsub):
                    rows = slice(t * ATT_KT, (t + 1) * ATT_KT)
                    s = s_buf[mm, e, rows, :]
                    if near is not None:
                        s = s + bias_scr[near, rows, :]
                    pm = jnp.max(fold(s), axis=0)
                    mx = pm if mx is None else jnp.maximum(mx, pm)
                mx = jnp.max(mx, axis=0, keepdims=True) + shift
                cand = mx if cand is None else jnp.maximum(cand, mx)
            m_old = m_scr[mm]
            m_new = jnp.maximum(m_old, cand)
            ls = None
            for e, (near, shift) in enumerate(entries):
                off = m_new - shift
                for t in range(nsub):
                    rows = slice(t * ATT_KT, (t + 1) * ATT_KT)
                    s = s_buf[mm, e, rows, :]
                    if near is not None:
                        s = s + bias_scr[near, rows, :]
                    p = jnp.exp2(s - off)
                    p_buf[mm, e * bk + t * ATT_KT:e * bk + (t + 1) * ATT_KT, :] = p.astype(BF16)
                    ps = jnp.sum(fold(p), axis=0)
                    ls = ps if ls is None else ls + ps
            alpha = jnp.exp2(m_old - m_new)
            l_scr[mm] = alpha * l_scr[mm] + jnp.sum(ls, axis=0, keepdims=True)
            m_scr[mm] = m_new
            alphas.append(alpha)
        return alphas

    def pv(p_buf, first, n):
        vts = [vt_ref[first + e] for e in range(n)]
        vt = jnp.concatenate(vts, axis=1) if n > 1 else vts[0]
        return [jnp.dot(vt, p_buf[mm, 0:n * bk, :], preferred_element_type=F32) for mm in range(2)]

    def accumulate(alphas, pvs):
        for mm in range(2):
            acc_scr[mm] = alphas[mm] * acc_scr[mm] + pvs[mm]

    def softmax_pv(s_buf, first, entries):
        alphas = softmax(s_buf, pa_scr, entries)
        accumulate(alphas, pv(pa_scr, first, len(entries)))

    nfar = jnp.maximum(kn0, 0)
    odd = nfar % 2
    far = (None, cfar)

    near_pair = [(0, 0.0), (1, 0.0)]

    @pl.when(kn0 < 0)
    def _only_first():
        scores(sc_scr, 0, 1)
        softmax_pv(sc_scr, 0, [(1, 0.0)])

    @pl.when(kn0 >= 0)
    def _groups():
        @pl.when(odd == 1)
        def _single():
            scores(sc_scr, 0, 1)
            scores(sa_scr, 1, 2)
            softmax_pv(sc_scr, 0, [far])

        @pl.when(odd == 0)
        def _first_pair():
            scores(sa_scr, 0, 2)

        def far_pair(s_cur, s_next, cur):
            scores(s_next, cur + 2, 2)
            softmax_pv(s_cur, cur, [far, far])

        def far_step(s_cur, p_cur, s_next, p_prev, cur):
            scores(s_next, cur + 2, 2)
            pending = None if p_prev is None else pv(p_prev, cur - 2, 2)
            alphas = softmax(s_cur, p_cur, [far, far])
            if pending is not None:
                accumulate([alpha_scr[0], alpha_scr[1]], pending)
            for mm in range(2):
                alpha_scr[mm] = alphas[mm]

        npairs = nfar // 2
        niter = npairs // 2

        @pl.when(niter >= 1)
        def _far_loop():
            far_step(sa_scr, pa_scr, sb_scr, None, odd)
            far_step(sb_scr, pb_scr, sa_scr, pa_scr, odd + 2)

            def far_body(j, carry):
                cur = odd + 4 * j
                far_step(sa_scr, pa_scr, sb_scr, pb_scr, cur)
                far_step(sb_scr, pb_scr, sa_scr, pa_scr, cur + 2)
                return carry

            lax.fori_loop(1, niter, far_body, 0)
            accumulate([alpha_scr[0], alpha_scr[1]], pv(pb_scr, odd + 4 * niter - 2, 2))

        @pl.when(npairs % 2 == 1)
        def _tail_b():
            far_pair(sa_scr, sb_scr, kn0 - 2)
            softmax_pv(sb_scr, kn0, near_pair)

        @pl.when(npairs % 2 == 0)
        def _tail_a():
            softmax_pv(sa_scr, kn0, near_pair)

    o = acc_scr[0] * (1.0 / l_scr[0]) - acc_scr[1] * (lam / l_scr[1])
    o = o * lax.rsqrt(jnp.mean(o * o, axis=0, keepdims=True) + EPS) * (subw_ref[...] * out_scale)
    o_ref[...] = o.T.astype(BF16)


def _attn_call(scal, qt, kb, vt, btab, subw, *, bq, bk, noff, out_scale):
    b, nq = qt.shape[:2]
    tk = kb.shape[1]
    nkb = vt.shape[1]
    tiles = math.gcd(nq, ATT_TILES)
    kern = functools.partial(_attn_kernel, tiles=tiles, bq=bq, bk=bk, noff=noff, out_scale=out_scale)
    return pl.pallas_call(
        kern,
        grid=(DA_HEADS, b, nq // tiles),
        in_specs=[pl.BlockSpec(memory_space=pltpu.SMEM),
                  pl.BlockSpec((None, tiles, DA_DV, bq), lambda h, i, j: (i, j, h, 0)),
                  pl.BlockSpec((None, tk, DA_DV), lambda h, i, j: (i, 0, h)),
                  pl.BlockSpec((None, nkb, DA_DV, bk), lambda h, i, j: (i, 0, h, 0)),
                  pl.BlockSpec((None, 2, 1, bq + bk), lambda h, i, j: (h, 0, 0, 0)),
                  pl.BlockSpec((DA_DV, 1), lambda h, i, j: (0, 0))],
        out_specs=pl.BlockSpec((None, tiles * bq, DA_DV), lambda h, i, j: (i, j, h)),
        out_shape=jax.ShapeDtypeStruct((b, nq * bq, DA_WIDTH), BF16),
        scratch_shapes=[pltpu.VMEM((2, 1, bq), F32), pltpu.VMEM((2, 1, bq), F32),
                        pltpu.VMEM((2, DA_DV, bq), F32),
                        pltpu.VMEM((2, 2, bk, bq), F32), pltpu.VMEM((2, 2, bk, bq), F32),
                        pltpu.VMEM((2, 1, bk, bq), F32),
                        pltpu.VMEM((2, 2 * bk, bq), BF16), pltpu.VMEM((2, 2 * bk, bq), BF16),
                        pltpu.VMEM((2, 1, bq), F32),
                        pltpu.VMEM((2, DA_DV, bq), BF16),
                        pltpu.VMEM((2, bk, bq), F32)],
        compiler_params=pltpu.CompilerParams(dimension_semantics=("arbitrary", "arbitrary", "arbitrary"),
                                             vmem_limit_bytes=VMEM_LIMIT),
        name="attn",
    )(scal, qt, kb, vt, btab, subw)


def _attn_cached_kernel(scal_ref, qt_ref, kc_ref, vc_ref, kn_ref, vtn_ref, bias0_ref, bias1_ref, subw_ref, o_ref,
                        *, bq, tq, bk, out_scale):
    lam = scal_ref[0]
    past = kc_ref.shape[0] // DA_HEADS
    lane = lax.broadcasted_iota(jnp.int32, (tq, DA_DV), 1)
    nt = (((1,), (1,)), ((), ()))

    for h in range(DA_HEADS):
        hs = slice(h * DA_DV, (h + 1) * DA_DV)
        cfar = scal_ref[1 + h]
        qn = qt_ref[hs, :].astype(F32).T[0:tq, :]
        q2 = jnp.concatenate([jnp.where(lane < DA_DK, qn, 0.0), jnp.where(lane >= DA_DK, qn, 0.0)],
                             axis=0).astype(BF16)
        head_rows = pl.ds(h, past, stride=DA_HEADS)
        s_c = lax.dot_general(q2, kc_ref[head_rows, :].astype(BF16), nt, preferred_element_type=F32)
        s_n = lax.dot_general(q2, kn_ref[:, hs], nt, preferred_element_type=F32)
        b0 = bias0_ref[h]
        b1 = bias1_ref[h]
        s = jnp.concatenate([s_c[:, 0:past - bk] + cfar,
                             s_c[:, past - bk:past] + jnp.concatenate([b0, b0], axis=0),
                             s_n + jnp.concatenate([b1, b1], axis=0)], axis=1)
        p = jnp.exp2(s - jnp.max(s, axis=1, keepdims=True))
        inv = 1.0 / jnp.sum(p, axis=1, keepdims=True)
        pb = p.astype(BF16)
        acc = (jnp.dot(pb[:, 0:past], vc_ref[head_rows, :].astype(BF16), preferred_element_type=F32)
               + jnp.dot(pb[:, past:past + bq], vtn_ref[hs, :].astype(F32).T.astype(BF16),
                         preferred_element_type=F32))
        o = acc[0:tq, :] * inv[0:tq] - acc[tq:2 * tq, :] * (lam * inv[tq:2 * tq])
        o_ref[0:tq, hs] = (_rms(o, subw_ref[...]) * out_scale).astype(BF16)
        o_ref[tq:bq, hs] = jnp.zeros((bq - tq, DA_DV), BF16)


def _attn_cached_call(scal, qt, cache_k, cache_v, kb, vt, bias0, bias1, subw, *, tq, bk, out_scale):
    b, past = cache_k.shape[:2]
    bq = qt.shape[-1]
    rows = past * DA_HEADS
    kern = functools.partial(_attn_cached_kernel, bq=bq, tq=tq, bk=bk, out_scale=out_scale)
    cache_spec = pl.BlockSpec((None, rows, DA_DV), lambda i: (i, 0, 0))
    return pl.pallas_call(
        kern,
        grid=(b,),
        in_specs=[pl.BlockSpec(memory_space=pltpu.SMEM),
                  pl.BlockSpec((None, None, DA_WIDTH, bq), lambda i: (i, 0, 0, 0)),
                  cache_spec, cache_spec,
                  pl.BlockSpec((None, bq, DA_WIDTH), lambda i: (i, 0, 0)),
                  pl.BlockSpec((None, None, DA_WIDTH, bq), lambda i: (i, 0, 0, 0)),
                  _const_spec((DA_HEADS, tq, bk)), _const_spec((DA_HEADS, tq, bq)),
                  _const_spec((1, DA_DV))],
        out_specs=pl.BlockSpec((None, bq, DA_WIDTH), lambda i: (i, 0, 0)),
        out_shape=jax.ShapeDtypeStruct((b, bq, DA_WIDTH), BF16),
        compiler_params=pltpu.CompilerParams(dimension_semantics=("parallel",), vmem_limit_bytes=VMEM_LIMIT),
        name="attn_cached",
    )(scal, qt, cache_k.reshape(b, rows, DA_DV), cache_v.reshape(b, rows, DA_DV), kb, vt, bias0, bias1, subw)


def _rel_bucket(rel):
    nb = REL_BUCKETS // 2
    max_exact = nb // 2
    n = jnp.abs(rel)
    nf = jnp.maximum(n, 1).astype(jnp.float32)
    large = max_exact + (jnp.log(nf / max_exact) / math.log(REL_MAX_DIST / max_exact)
                         * (nb - max_exact)).astype(jnp.int32)
    large = jnp.minimum(large, nb - 1)
    return jnp.where(rel > 0, nb, 0) + jnp.where(n < max_exact, n, large)


def _bias_tables(rel_bias, bq, bk):
    tabs = []
    for d in range(2):
        offs = (d - 1) * bk + bk - np.arange(bq + bk)
        tabs.append(rel_bias[_rel_bucket(jnp.asarray(offs, jnp.int32))].astype(F32).T * LOG2E)
    return jnp.stack(tabs, axis=1)[:, :, None, :]


def _bias_rows(rel_bias, qpos0, tq, kpos0, nkeys, tk_real):
    offs = (kpos0 - qpos0) + np.arange(-(tq - 1), nkeys)
    table = rel_bias[_rel_bucket(jnp.asarray(offs, jnp.int32))].astype(F32).T * LOG2E
    rows = jnp.stack([table[:, tq - 1 - i:tq - 1 - i + nkeys] for i in range(tq)], axis=1)
    qpos = qpos0 + np.arange(tq)
    kpos = kpos0 + np.arange(nkeys)
    vis = (kpos[None, :] // CHUNK <= qpos[:, None] // CHUNK) & (kpos[None, :] < tk_real)
    return jnp.where(jnp.asarray(vis)[None], rows, NEG)


def _ffn_kernel(x_ref, ys_ref, ya_ref, mod_ref, wo1_ref, wo2_ref, nfw_ref, wu_ref, cw_ref, cb_ref, wd_ref, fw_ref,
                hist_ref, y_ref, tail_ref, tail_scr, buf_a, buf_b, x1_scr, h2_scr, act_scr, *, tm, real):
    t = pl.program_id(1)
    nh = FFN_CONV - 1
    lo = SUBLANES - nh

    @pl.when(t == 0)
    def _init():
        tail_scr[lo:SUBLANES, :] = hist_ref[...]

    mix = (jnp.dot(ys_ref[...], wo1_ref[...], preferred_element_type=F32)
           + jnp.dot(ya_ref[...], wo2_ref[...], preferred_element_type=F32))
    x1 = x_ref[...] + mod_ref[2:3, :] * mix
    x1_scr[...] = x1
    h2 = _rms(x1, nfw_ref[...]) * (1.0 + mod_ref[4:5, :]) + mod_ref[3:4, :]
    h2_scr[...] = h2.astype(BF16)

    def cols(j, half):
        return slice(half * D_FF + j * FFN_CN, half * D_FF + (j + 1) * FFN_CN)

    def up(j, buf):
        for half in range(2):
            buf[half, SUBLANES:SUBLANES + tm, :] = jnp.dot(h2_scr[...], wu_ref[:, cols(j, half)],
                                                            preferred_element_type=F32)

    def conv(j, buf, half):
        cs = cols(j, half)
        buf[half, lo:SUBLANES, :] = tail_scr[lo:SUBLANES, cs]
        c = cb_ref[:, cs]
        for i in range(FFN_CONV):
            c = c + cw_ref[i:i + 1, cs] * buf[half, lo + i:lo + i + tm, :]
        tail_scr[lo:SUBLANES, cs] = buf[half, lo + real:SUBLANES + real, :]
        return c

    bufs = (buf_a, buf_b)
    up(0, bufs[0])
    for j in range(FFN_NC):
        if j + 1 < FFN_NC:
            up(j + 1, bufs[(j + 1) % 2])
        cv = conv(j, bufs[j % 2], 0)
        cg = conv(j, bufs[j % 2], 1)
        act_scr[:, j * FFN_CN:(j + 1) * FFN_CN] = (_silu(cg) * cv).astype(BF16)

    f = jnp.dot(act_scr[...], wd_ref[...], preferred_element_type=F32)
    x2 = x1_scr[...] + mod_ref[5:6, :] * f
    y_ref[...] = _rms(x2, fw_ref[...])
    tail_ref[...] = tail_scr[lo:SUBLANES, :]


def _ffn_call(x, ys, ya, mod3, wo1, wo2, nfw, wu, cw, cb, wd, fw, hist, *, tm, real):
    b, t, d = x.shape
    nh = FFN_CONV - 1

    def row(width):
        return pl.BlockSpec((None, tm, width), lambda i, j: (i, j, 0))

    hist_spec = pl.BlockSpec((None, nh, 2 * D_FF), lambda i, j: (i, 0, 0))
    kern = functools.partial(_ffn_kernel, tm=tm, real=real)
    return pl.pallas_call(
        kern,
        grid=(b, t // tm),
        in_specs=[row(d), row(SSD_WIDTH), row(DA_WIDTH),
                  pl.BlockSpec((None, 6, d), lambda i, j: (i, 0, 0)),
                  _const_spec((SSD_WIDTH, d)), _const_spec((DA_WIDTH, d)), _const_spec((1, d)),
                  _const_spec((d, 2 * D_FF)), _const_spec((FFN_CONV, 2 * D_FF)), _const_spec((1, 2 * D_FF)),
                  _const_spec((D_FF, d)), _const_spec((1, d)),
                  hist_spec],
        out_specs=[row(d), hist_spec],
        out_shape=[jax.ShapeDtypeStruct((b, t, d), F32),
                   jax.ShapeDtypeStruct((b, nh, 2 * D_FF), F32)],
        scratch_shapes=[pltpu.VMEM((SUBLANES, 2 * D_FF), F32),
                        pltpu.VMEM((2, tm + SUBLANES, FFN_CN), F32), pltpu.VMEM((2, tm + SUBLANES, FFN_CN), F32),
                        pltpu.VMEM((tm, d), F32), pltpu.VMEM((tm, d), BF16), pltpu.VMEM((tm, D_FF), BF16)],
        compiler_params=pltpu.CompilerParams(dimension_semantics=("parallel", "arbitrary"),
                                             vmem_limit_bytes=VMEM_LIMIT),
        name="ffn",
    )(x, ys, ya, mod3, wo1, wo2, nfw, wu, cw, cb, wd, fw, hist)


def _pack_params(norm_mix_w, w_in, ssm_conv_w, ssm_conv_b, ssm_dt_bias, ssm_a_log, ssm_d, ssm_norm_w,
                 lambda_q1, lambda_k1, lambda_q2, lambda_k2, attn_subln_w, rel_bias, w_out,
                 norm_ffn_w, w_up, ffn_conv_w, ffn_conv_b, w_down, final_norm_w, layer):
    l = layer
    wz, wx, wdt, wq, wk, wv = jnp.split(w_in[l], IN_SPLITS, axis=-1)
    wdt = jnp.pad(wdt, ((0, 0), (0, LANES - SSD_HEADS)))
    w_cat = jnp.concatenate([wz, wx, wdt, wk, wv], axis=-1).astype(BF16)
    w_t = jnp.concatenate([wq * (DA_DK ** -0.5 * LOG2E), wv], axis=-1).T.astype(BF16)

    def pad_heads(v):
        return jnp.pad(v.astype(F32), (0, LANES - SSD_HEADS)).reshape(1, LANES)

    lam_init = 0.8 - 0.6 * math.exp(-0.3 * l)
    lam = (jnp.exp(jnp.sum(lambda_q1[l].astype(F32) * lambda_k1[l].astype(F32)))
           - jnp.exp(jnp.sum(lambda_q2[l].astype(F32) * lambda_k2[l].astype(F32))) + lam_init)
    far_bias = rel_bias[REL_BUCKETS // 2 - 1].astype(F32)
    return dict(
        norm_mix_w=norm_mix_w[l].reshape(1, D_MODEL), w_cat=w_cat, w_t=w_t,
        cw=ssm_conv_w[l], cbias=ssm_conv_b[l].reshape(1, SSD_CONV_DIM),
        dtb=pad_heads(ssm_dt_bias[l]), alog=pad_heads(ssm_a_log[l]),
        dsk=jnp.repeat(ssm_d[l].astype(F32), SSD_HEADDIM).reshape(1, SSD_WIDTH),
        ssm_nw=ssm_norm_w[l].reshape(1, SSD_WIDTH),
        scal=jnp.concatenate([lam.reshape(1), far_bias * LOG2E]).astype(F32), lam_init=lam_init,
        subw=attn_subln_w[l].reshape(DA_DV, 1), rel_bias=rel_bias,
        wo1=w_out[l][:SSD_WIDTH].astype(BF16), wo2=w_out[l][SSD_WIDTH:].astype(BF16),
        nfw=norm_ffn_w[l].reshape(1, D_MODEL),
        wu=w_up[l].astype(BF16), ffn_cw=ffn_conv_w[l], ffn_cb=ffn_conv_b[l].reshape(1, 2 * D_FF),
        wd=w_down[l].astype(BF16), fw=final_norm_w.reshape(1, D_MODEL),
    )


def _state_to_kernel(h):
    return h.reshape(h.shape[0], SSD_GROUPS, GROUP_W, SSD_STATE)


def _state_from_kernel(h):
    return h.reshape(h.shape[0], SSD_HEADS, SSD_HEADDIM, SSD_STATE)


def _run_group(x, mod, past_k, past_v, ssm_h0, ssm_conv_hist, ffn_conv_hist, p, *, tm, ssd_rows, bq, bk):
    b, t, d = x.shape
    past = 0 if past_k is None else past_k.shape[1]
    chunk = min(CHUNK, t)
    tp = max(t, SUPER)
    if tp != t:
        x = jnp.pad(x, ((0, 0), (0, tp - t), (0, 0)))
        tm = ssd_rows = bq = tp
    mod3 = mod.reshape(b, 6, d)

    assert tm == bq
    zs, xc, dt, k, v, kb, qt, vt, conv_new = _inproj_call(
        x, mod3, p["norm_mix_w"], p["w_cat"], p["w_t"], ssm_conv_hist.astype(F32), p["cw"], p["cbias"],
        tm=tm, real=min(t, tm))

    y_ssd, h_t = _ssd_call(zs, xc, dt, _state_to_kernel(ssm_h0.astype(F32)),
                           p["dtb"], p["alog"], p["dsk"], p["ssm_nw"],
                           chunk=chunk, rows=ssd_rows, real=min(t, ssd_rows))

    if past == 0:
        assert bq == bk and t % bq == 0
        y_att = _attn_call(p["scal"], qt, kb, vt, _bias_tables(p["rel_bias"], bq, bk), p["subw"], bq=bq, bk=bk, noff=0,
                           out_scale=1.0 - p["lam_init"])
    else:
        assert past % bk == 0 and past >= bk and bq == tp <= bk
        bias0 = _bias_rows(p["rel_bias"], past, t, past - bk, bk, past + t)
        bias1 = _bias_rows(p["rel_bias"], past, t, past, bq, past + t)
        y_att = _attn_cached_call(p["scal"], qt, past_k, past_v, kb, vt, bias0, bias1, p["subw"].reshape(1, DA_DV),
                                  tq=t, bk=bk, out_scale=1.0 - p["lam_init"])

    y, ffn_new = _ffn_call(x, y_ssd, y_att, mod3, p["wo1"], p["wo2"], p["nfw"], p["wu"], p["ffn_cw"], p["ffn_cb"],
                           p["wd"], p["fw"], ffn_conv_hist.astype(F32), tm=tm, real=min(t, tm))
    k = k.reshape(b, tp, DA_HEADS, 2 * DA_DK)
    v = v.reshape(b, tp, DA_HEADS, DA_DV)
    return (y[:, :t], k[:, :t], v[:, :t],
            _state_from_kernel(h_t), conv_new, ffn_new)


def kernel(x_prompt, x_sample, c_prompt, c_sample, cache_k, cache_v, state_ssm, state_ssm_conv, state_ffn_conv, w_ada, b_ada, norm_mix_w, w_in, ssm_conv_w, ssm_conv_b, ssm_dt_bias, ssm_a_log, ssm_d, ssm_norm_w, lambda_q1, lambda_k1, lambda_q2, lambda_k2, attn_subln_w, rel_bias, w_out, norm_ffn_w, w_up, ffn_conv_w, ffn_conv_b, w_down, final_norm_w):
    bp, bs = x_prompt.shape[0], x_sample.shape[0]
    dt = x_prompt.dtype
    p = _pack_params(norm_mix_w, w_in, ssm_conv_w, ssm_conv_b, ssm_dt_bias, ssm_a_log, ssm_d, ssm_norm_w,
                     lambda_q1, lambda_k1, lambda_q2, lambda_k2, attn_subln_w, rel_bias, w_out,
                     norm_ffn_w, w_up, ffn_conv_w, ffn_conv_b, w_down, final_norm_w, 0)
    c_all = jnp.concatenate([c_prompt, c_sample], axis=0)
    npad = -c_all.shape[0] % SUBLANES
    c_all = jnp.pad(c_all, ((0, npad), (0, 0)))
    mod = _mod_call(c_all, w_ada[0], b_ada[0].reshape(1, -1))

    zeros = lambda *s: jnp.zeros(s, dt)
    out_p = _run_group(x_prompt, mod[:bp], None, None,
                       zeros(bp, SSD_HEADS, SSD_HEADDIM, SSD_STATE), zeros(bp, SSD_CONV - 1, SSD_CONV_DIM),
                       zeros(bp, FFN_CONV - 1, 2 * D_FF), p, tm=512, ssd_rows=256, bq=512, bk=512)
    out_s = _run_group(x_sample, mod[bp:bp + bs], cache_k[0], cache_v[0], state_ssm[0], state_ssm_conv[0],
                       state_ffn_conv[0], p, tm=SUPER, ssd_rows=SUPER, bq=SUPER, bk=512)
    y_p, k_p, v_p, h_p, c_p, f_p = out_p
    y_s, k_s, v_s, h_s, c_s, f_s = out_s
    return (y_p, y_s, k_p[None], v_p[None], h_p[None], c_p[None], f_p[None],
            k_s[None], v_s[None], h_s[None], c_s[None], f_s[None])
```

```python
import functools
import math

import numpy as np
import jax
import jax.numpy as jnp
from jax import lax
from jax.experimental import pallas as pl
from jax.experimental.pallas import tpu as pltpu

F32 = jnp.float32
BF16 = jnp.bfloat16
HIGHEST = lax.Precision.HIGHEST

D_MODEL = 1024
CHUNK = 64
SSD_WIDTH = 512
SSD_HEADDIM = 64
SSD_HEADS = 8
SSD_GROUPS = 2
SSD_HPG = 4
SSD_STATE = 128
SSD_CONV = 4
SSD_CONV_DIM = SSD_WIDTH + 2 * SSD_GROUPS * SSD_STATE
GROUP_W = SSD_HPG * SSD_HEADDIM
DA_WIDTH = 512
DA_DK = 64
DA_DV = 128
DA_HEADS = 4
REL_BUCKETS = 32
REL_MAX_DIST = 128
D_FF = 2816
FFN_CONV = 3
EPS = 1e-6
IN_SPLITS = (512, 1536, 1544, 2056, 2568)
LANES = 128
SUBLANES = 8
SUPER = 128
SSD_NSEQ = 4
ATT_KT = 64
FFN_CN = 256
FFN_NC = D_FF // FFN_CN
NEG = -1e30
VMEM_LIMIT = 56 * 1024 * 1024

PZ, PX, PDT, PK, PV, PEND = 0, 512, 1536, 1664, 2176, 2688
LOG2E = math.log2(math.e)


def _silu(x):
    return x / (1.0 + jnp.exp(-x))


def _softplus(x):
    return jnp.maximum(x, 0.0) + jnp.log1p(jnp.exp(-jnp.abs(x)))


def _split3(x):
    hi = x.astype(BF16)
    r1 = x - hi.astype(F32)
    mid = r1.astype(BF16)
    lo = (r1 - mid.astype(F32)).astype(BF16)
    return hi, mid, lo


def _rms(x, w):
    return x * lax.rsqrt(jnp.mean(x * x, axis=-1, keepdims=True) + EPS) * w


def _const_spec(shape):
    nd = len(shape)
    return pl.BlockSpec(shape, lambda *_: (0,) * nd)


def _mod_kernel(c_ref, w_ref, b_ref, o_ref):
    a = _silu(c_ref[...]).astype(BF16)
    o_ref[...] = jnp.dot(a, w_ref[...].astype(BF16), preferred_element_type=F32) + b_ref[...]


def _mod_call(c, w_ada, b_ada):
    n, d = c.shape
    nout = w_ada.shape[1]
    tn = 1024
    return pl.pallas_call(
        _mod_kernel,
        grid=(nout // tn,),
        in_specs=[pl.BlockSpec((n, d), lambda j: (0, 0)),
                  pl.BlockSpec((d, tn), lambda j: (0, j)),
                  pl.BlockSpec((1, tn), lambda j: (0, j))],
        out_specs=pl.BlockSpec((n, tn), lambda j: (0, j)),
        out_shape=jax.ShapeDtypeStruct((n, nout), F32),
        name="mod",
    )(c, w_ada, b_ada)


def _inproj_kernel(x_ref, mod_ref, nw_ref, w_ref, wt_ref, hist_ref, cw_ref, cbias_ref,
                   zs_ref, xc_ref, dt_ref, k_ref, v_ref, kb_ref, qt_ref, vt_ref, cout_ref, cbuf, hb_scr, zbuf,
                   *, tm, real):
    t = pl.program_id(1)
    nconv = SSD_CONV - 1

    @pl.when(t == 0)
    def _init():
        cbuf[0:SUBLANES, :] = jnp.zeros((SUBLANES, SSD_CONV_DIM), F32)
        cbuf[SUBLANES - nconv:SUBLANES, :] = hist_ref[...]

    h = _rms(x_ref[...], nw_ref[...]) * (1.0 + mod_ref[1:2, :]) + mod_ref[0:1, :]
    hb_scr[...] = h.astype(BF16)

    def proj(a, b):
        return jnp.dot(hb_scr[...], w_ref[:, a:b], preferred_element_type=F32)

    def proj_t(a, b):
        return lax.dot_general(wt_ref[a:b, :], hb_scr[...], (((1,), (1,)), ((), ())), preferred_element_type=F32)

    cbuf[SUBLANES:SUBLANES + tm, :] = proj(PX, PDT)
    zbuf[...] = proj(PZ, PX)
    dt_ref[...] = proj(PDT, PK)
    k = proj(PK, PV)
    v = proj(PV, PEND)
    for hd in range(DA_HEADS):
        dst = pl.ds(hd, tm, stride=DA_HEADS)
        k_ref[dst, :] = k[:, hd * DA_DV:(hd + 1) * DA_DV]
        v_ref[dst, :] = v[:, hd * DA_DV:(hd + 1) * DA_DV]
    kb_ref[...] = k.astype(BF16)
    qt_ref[...] = proj_t(0, DA_WIDTH).astype(BF16)
    for hd in range(DA_HEADS):
        vt_ref[hd * DA_DV:(hd + 1) * DA_DV, :] = v[:, hd * DA_DV:(hd + 1) * DA_DV].T.astype(BF16)
    conv = cbias_ref[...]
    for j in range(SSD_CONV):
        off = SUBLANES - nconv + j
        conv = conv + cw_ref[j:j + 1, :] * cbuf[off:off + tm, :]
    tail = cbuf[SUBLANES - nconv + real:SUBLANES + real, :]
    cout_ref[...] = tail
    cbuf[SUBLANES - nconv:SUBLANES, :] = tail
    xc_ref[...] = _silu(conv).astype(BF16)
    zs_ref[...] = _silu(zbuf[...]).astype(BF16)


def _inproj_call(x, mod3, norm_w, w_cat, w_t, hist, cw, cbias, *, tm, real):
    b, t, d = x.shape
    nt = t // tm

    def row(width):
        return pl.BlockSpec((None, tm, width), lambda i, j: (i, j, 0))

    def out(width, dtype):
        return jax.ShapeDtypeStruct((b, t, width), dtype)

    tspec = pl.BlockSpec((None, None, DA_WIDTH, tm), lambda i, j: (i, j, 0, 0))
    tshape = jax.ShapeDtypeStruct((b, nt, DA_WIDTH, tm), BF16)
    hist_spec = pl.BlockSpec((None, SSD_CONV - 1, SSD_CONV_DIM), lambda i, j: (i, 0, 0))
    hspec = pl.BlockSpec((None, tm * DA_HEADS, DA_DV), lambda i, j: (i, j, 0))
    hshape = jax.ShapeDtypeStruct((b, t * DA_HEADS, DA_DV), F32)
    return pl.pallas_call(
        functools.partial(_inproj_kernel, tm=tm, real=real),
        grid=(b, nt),
        in_specs=[row(d),
                  pl.BlockSpec((None, 6, d), lambda i, j: (i, 0, 0)),
                  _const_spec((1, d)),
                  _const_spec((d, PEND)),
                  _const_spec((DA_WIDTH, d)),
                  hist_spec, _const_spec((SSD_CONV, SSD_CONV_DIM)), _const_spec((1, SSD_CONV_DIM))],
        out_specs=[row(512), row(1024), row(LANES), hspec, hspec, row(512), tspec, tspec, hist_spec],
        out_shape=[out(512, BF16), out(1024, BF16), out(LANES, F32),
                   hshape, hshape, out(512, BF16), tshape, tshape,
                   jax.ShapeDtypeStruct((b, SSD_CONV - 1, SSD_CONV_DIM), F32)],
        scratch_shapes=[pltpu.VMEM((tm + SUBLANES, SSD_CONV_DIM), F32), pltpu.VMEM((tm, d), BF16),
                        pltpu.VMEM((tm, SSD_WIDTH), F32)],
        compiler_params=pltpu.CompilerParams(dimension_semantics=("parallel", "arbitrary"),
                                             vmem_limit_bytes=VMEM_LIMIT),
        name="inproj",
    )(x, mod3, norm_w, w_cat, w_t, hist, cw, cbias)


def _ssd_kernel(zs_ref, xc_ref, dt_ref, h0_ref, dtb_ref, alog_ref, dsk_ref, nw_ref, tri_ref, e_ref,
                y_ref, hout_ref, h_scr, ybuf, *, nseq, chunk, rows, real):
    t = pl.program_id(1)
    seqs = range(nseq)

    @pl.when(t == 0)
    def _init():
        for i in seqs:
            for g in range(SSD_GROUPS):
                h_scr[i, g] = h0_ref[i, g].T

    li = lax.broadcasted_iota(jnp.int32, (SUPER, SUPER), 0)
    si = lax.broadcasted_iota(jnp.int32, (SUPER, SUPER), 1)
    cshift = chunk.bit_length() - 1
    mask2 = ((li >> cshift) == (si >> cshift)) & (si <= li)
    lane_g = lax.broadcasted_iota(jnp.int32, (SUPER, GROUP_W), 1) >> (SSD_HEADDIM.bit_length() - 1)

    pre = []
    for i in seqs:
        xs = xc_ref[i, :, 0:SSD_WIDTH].astype(F32)
        dtv = _softplus(dt_ref[i] + dtb_ref[...])
        da = dtv * (-jnp.exp(alog_ref[...]))
        acs = jnp.dot(tri_ref[...], jnp.concatenate(_split3(da), axis=0), preferred_element_type=F32)
        dt_x = jnp.dot(jnp.concatenate(_split3(dtv), axis=1), e_ref[...], preferred_element_type=F32)
        acs_x = jnp.dot(jnp.concatenate(_split3(acs), axis=1), e_ref[...], preferred_element_type=F32)
        pre.append((xs, acs, acs_x, jnp.exp(acs_x), xs * dt_x))

    for sb in range(rows // SUPER):
        o = sb * SUPER
        nreal = (min(real, o + SUPER) - o) // chunk
        acs2 = [pre[i][1][o:o + SUPER, :] for i in seqs]
        acs_t = [a.T for a in acs2]
        chains = [(g, i) for g in range(SSD_GROUPS) for i in seqs]

        def gsl(g):
            return slice(g * GROUP_W, (g + 1) * GROUP_W)

        cmb, cb2, bm_t = {}, {}, {}
        for g, i in chains:
            bcol = SSD_WIDTH + g * SSD_STATE
            ccol = SSD_WIDTH + (SSD_GROUPS + g) * SSD_STATE
            bmb = xc_ref[i, o:o + SUPER, bcol:bcol + SSD_STATE]
            cmb[g, i] = xc_ref[i, o:o + SUPER, ccol:ccol + SSD_STATE]
            cb2[g, i] = lax.dot_general(cmb[g, i], bmb, (((1,), (1,)), ((), ())), preferred_element_type=F32)
            bm_t[g, i] = bmb.astype(F32).T.astype(BF16)
        for g, i in chains:
            ms = []
            for rr in range(SSD_HPG):
                r = g * SSD_HPG + rr
                seg = acs2[i][:, r:r + 1] - acs_t[i][r:r + 1, :]
                dec = jnp.where(mask2, jnp.exp(jnp.where(mask2, seg, 0.0)), 0.0)
                ms.append((cb2[g, i] * dec).astype(BF16))
            full = jnp.dot(jnp.concatenate(ms, axis=0), pre[i][4][o:o + SUPER, gsl(g)].astype(BF16),
                           preferred_element_type=F32)
            ydiag = full[0:SUPER]
            for rr in range(1, SSD_HPG):
                ydiag = jnp.where(lane_g == rr, full[rr * SUPER:(rr + 1) * SUPER], ydiag)
            ybuf[i, o:o + SUPER, gsl(g)] = ydiag
        for j in range(nreal):
            a0, a1 = o + j * chunk, o + (j + 1) * chunk
            h_t = {c: h_scr[c[1], c[0]] for c in chains}
            yoff = {(g, i): jnp.dot(cmb[g, i][j * chunk:(j + 1) * chunk, :], h_t[g, i].astype(BF16),
                                    preferred_element_type=F32) for g, i in chains}
            st = {}
            for g, i in chains:
                _, _, acs_x, eacs_x, xd = pre[i]
                ybuf[i, a0:a1, gsl(g)] = ybuf[i, a0:a1, gsl(g)] + yoff[g, i] * eacs_x[a0:a1, gsl(g)]
                dte = jnp.exp(acs_x[a1 - 1:a1, gsl(g)] - acs_x[a0:a1, gsl(g)])
                xw = (xd[a0:a1, gsl(g)] * dte).astype(BF16)
                pieces = []
                if j > 0:
                    pieces.append(jnp.zeros((j * chunk, GROUP_W), BF16))
                pieces.append(xw)
                if (j + 1) * chunk < SUPER:
                    pieces.append(jnp.zeros((SUPER - (j + 1) * chunk, GROUP_W), BF16))
                xw2 = jnp.concatenate(pieces, axis=0) if len(pieces) > 1 else xw
                st[g, i] = jnp.dot(bm_t[g, i], xw2, preferred_element_type=F32)
            for g, i in chains:
                h_scr[i, g] = h_t[g, i] * pre[i][3][a1 - 1:a1, gsl(g)] + st[g, i]

    for i in seqs:
        y = (ybuf[i] + dsk_ref[...] * pre[i][0]) * zs_ref[i].astype(F32)
        for g in range(SSD_GROUPS):
            gs = slice(g * GROUP_W, (g + 1) * GROUP_W)
            y_ref[i, :, gs] = _rms(y[:, gs], nw_ref[:, gs]).astype(BF16)

    @pl.when(t == pl.num_programs(1) - 1)
    def _fin():
        for i in seqs:
            for g in range(SSD_GROUPS):
                hout_ref[i, g] = h_scr[i, g].T


def _ssd_call(zs, xc, dt, h0_t, dtb, alog, dsk, nw, *, chunk, rows, real):
    b, t, _ = zs.shape
    ii = np.arange(rows)
    tri = ((ii[:, None] // chunk == ii[None, :] // chunk) & (ii[None, :] <= ii[:, None])).astype(np.float32)
    e = np.zeros((LANES, SSD_WIDTH), np.float32)
    for r in range(SSD_HEADS):
        e[r, r * SSD_HEADDIM:(r + 1) * SSD_HEADDIM] = 1.0

    nseq = math.gcd(b, SSD_NSEQ)

    def row(width):
        return pl.BlockSpec((nseq, rows, width), lambda i, j: (i, j, 0))

    state_spec = pl.BlockSpec((nseq, SSD_GROUPS, GROUP_W, SSD_STATE), lambda i, j: (i, 0, 0, 0))
    kern = functools.partial(_ssd_kernel, nseq=nseq, chunk=chunk, rows=rows, real=real)
    return pl.pallas_call(
        kern,
        grid=(b // nseq, t // rows),
        in_specs=[row(SSD_WIDTH), row(SSD_CONV_DIM), row(LANES), state_spec,
                  _const_spec((1, LANES)), _const_spec((1, LANES)),
                  _const_spec((1, SSD_WIDTH)), _const_spec((1, SSD_WIDTH)),
                  _const_spec((rows, 3 * rows)), _const_spec((3 * LANES, SSD_WIDTH))],
        out_specs=[row(SSD_WIDTH), state_spec],
        out_shape=[jax.ShapeDtypeStruct((b, t, SSD_WIDTH), BF16),
                   jax.ShapeDtypeStruct((b, SSD_GROUPS, GROUP_W, SSD_STATE), F32)],
        scratch_shapes=[pltpu.VMEM((nseq, SSD_GROUPS, SSD_STATE, GROUP_W), F32),
                        pltpu.VMEM((nseq, rows, SSD_WIDTH), F32)],
        compiler_params=pltpu.CompilerParams(dimension_semantics=("parallel", "arbitrary"),
                                             vmem_limit_bytes=VMEM_LIMIT),
        name="ssd",
    )(zs, xc, dt, h0_t, dtb, alog, dsk, nw,
      jnp.asarray(np.tile(tri, (1, 3)), BF16), jnp.asarray(np.tile(e, (3, 1)), BF16))


def _attn_kernel(scal_ref, qt_ref, k_ref, vt_ref, btab_ref, subw_ref, o_ref,
                 m_scr, l_scr, acc_scr, sa_scr, sb_scr, sc_scr, pa_scr, pb_scr, alpha_scr, qz_scr, bias_scr,
                 *, bq, bk, noff, out_scale):
    h = pl.program_id(0)
    qi = pl.program_id(2)
    kn0 = qi + (noff - 1)
    lam = scal_ref[0]
    cfar = scal_ref[1 + h]

    @pl.when((pl.program_id(1) == 0) & (qi == 0))
    def _build_bias_tiles():
        width = bq + bk
        kj = lax.broadcasted_iota(jnp.int32, (bk, bq), 0)
        qj = lax.broadcasted_iota(jnp.int32, (bk, bq), 1)
        cshift = CHUNK.bit_length() - 1
        for d in range(2):
            skew = pltpu.roll(jnp.broadcast_to(btab_ref[d], (bk, width)), 0, 1, stride=1, stride_axis=0)
            tile = skew[:, bk:width]
            if d == 1:
                tile = jnp.where((kj >> cshift) <= (qj >> cshift), tile, NEG)
            bias_scr[d] = tile

    zero = jnp.zeros((DA_DK, bq), BF16)
    qz_scr[0, 0:DA_DK, :] = qt_ref[0:DA_DK, :]
    qz_scr[0, DA_DK:DA_DV, :] = zero
    qz_scr[1, 0:DA_DK, :] = zero
    qz_scr[1, DA_DK:DA_DV, :] = qt_ref[DA_DK:DA_DV, :]

    m_scr[...] = jnp.full(m_scr.shape, NEG, F32)
    l_scr[...] = jnp.zeros(l_scr.shape, F32)
    acc_scr[...] = jnp.zeros(acc_scr.shape, F32)

    nsub = bk // ATT_KT

    def fold(x):
        return x.reshape(ATT_KT // SUBLANES, SUBLANES, bq)

    def scores(s_buf, first, count):
        for mm in range(2):
            for e in range(count):
                start = pl.multiple_of((first + e) * bk, bk)
                s_buf[mm, e] = jnp.dot(k_ref[pl.ds(start, bk), :], qz_scr[mm], preferred_element_type=F32)

    def softmax(s_buf, p_buf, entries):
        alphas = []
        for mm in range(2):
            cand = None
            for e, (near, shift) in enumerate(entries):
                mx = None
                for t in range(nsub):
                    rows = slice(t * ATT_KT, (t + 1) * ATT_KT)
                    s = s_buf[mm, e, rows, :]
                    if near is not None:
                        s = s + bias_scr[near, rows, :]
                    pm = jnp.max(fold(s), axis=0)
                    mx = pm if mx is None else jnp.maximum(mx, pm)
                mx = jnp.max(mx, axis=0, keepdims=True) + shift
                cand = mx if cand is None else jnp.maximum(cand, mx)
            m_old = m_scr[mm]
            m_new = jnp.maximum(m_old, cand)
            ls = None
            for e, (near, shift) in enumerate(entries):
                off = m_new - shift
                for t in range(nsub):
                    rows = slice(t * ATT_KT, (t + 1) * ATT_KT)
                    s = s_buf[mm, e, rows, :]
                    if near is not None:
                        s = s + bias_scr[near, rows, :]
                    p = jnp.exp2(s - off)
                    p_buf[mm, e * bk + t * ATT_KT:e * bk + (t + 1) * ATT_KT, :] = p.astype(BF16)
                    ps = jnp.sum(fold(p), axis=0)
                    ls = ps if ls is None else ls + ps
            alpha = jnp.exp2(m_old - m_new)
            l_scr[mm] = alpha * l_scr[mm] + jnp.sum(ls, axis=0, keepdims=True)
            m_scr[mm] = m_new
            alphas.append(alpha)
        return alphas

    def pv(p_buf, first, n):
        vts = [vt_ref[first + e] for e in range(n)]
        vt = jnp.concatenate(vts, axis=1) if n > 1 else vts[0]
        return [jnp.dot(vt, p_buf[mm, 0:n * bk, :], preferred_element_type=F32) for mm in range(2)]

    def accumulate(alphas, pvs):
        for mm in range(2):
            acc_scr[mm] = alphas[mm] * acc_scr[mm] + pvs[mm]

    def softmax_pv(s_buf, first, entries):
        alphas = softmax(s_buf, pa_scr, entries)
        accumulate(alphas, pv(pa_scr, first, len(entries)))

    nfar = jnp.maximum(kn0, 0)
    odd = nfar % 2
    far = (None, cfar)

    near_pair = [(0, 0.0), (1, 0.0)]

    @pl.when(kn0 < 0)
    def _only_first():
        scores(sc_scr, 0, 1)
        softmax_pv(sc_scr, 0, [(1, 0.0)])

    @pl.when(kn0 >= 0)
    def _groups():
        @pl.when(odd == 1)
        def _single():
            scores(sc_scr, 0, 1)
            scores(sa_scr, 1, 2)
            softmax_pv(sc_scr, 0, [far])

        @pl.when(odd == 0)
        def _first_pair():
            scores(sa_scr, 0, 2)

        def far_pair(s_cur, s_next, cur):
            scores(s_next, cur + 2, 2)
            softmax_pv(s_cur, cur, [far, far])

        def far_step(s_cur, p_cur, s_next, p_prev, cur):
            scores(s_next, cur + 2, 2)
            pending = None if p_prev is None else pv(p_prev, cur - 2, 2)
            alphas = softmax(s_cur, p_cur, [far, far])
            if pending is not None:
                accumulate([alpha_scr[0], alpha_scr[1]], pending)
            for mm in range(2):
                alpha_scr[mm] = alphas[mm]

        npairs = nfar // 2
        niter = npairs // 2

        @pl.when(niter >= 1)
        def _far_loop():
            far_step(sa_scr, pa_scr, sb_scr, None, odd)
            far_step(sb_scr, pb_scr, sa_scr, pa_scr, odd + 2)

            def far_body(j, carry):
                cur = odd + 4 * j
                far_step(sa_scr, pa_scr, sb_scr, pb_scr, cur)
                far_step(sb_scr, pb_scr, sa_scr, pa_scr, cur + 2)
                return carry

            lax.fori_loop(1, niter, far_body, 0)
            accumulate([alpha_scr[0], alpha_scr[1]], pv(pb_scr, odd + 4 * niter - 2, 2))

        @pl.when(npairs % 2 == 1)
        def _tail_b():
            far_pair(sa_scr, sb_scr, kn0 - 2)
            softmax_pv(sb_scr, kn0, near_pair)

        @pl.when(npairs % 2 == 0)
        def _tail_a():
            softmax_pv(sa_scr, kn0, near_pair)

    o = acc_scr[0] * (1.0 / l_scr[0]) - acc_scr[1] * (lam / l_scr[1])
    o = o * lax.rsqrt(jnp.mean(o * o, axis=0, keepdims=True) + EPS) * (subw_ref[...] * out_scale)
    o_ref[...] = o.T.astype(BF16)


def _attn_call(scal, qt, kb, vt, btab, subw, *, bq, bk, noff, out_scale):
    b, nq = qt.shape[:2]
    tk = kb.shape[1]
    nkb = vt.shape[1]
    kern = functools.partial(_attn_kernel, bq=bq, bk=bk, noff=noff, out_scale=out_scale)
    return pl.pallas_call(
        kern,
        grid=(DA_HEADS, b, nq),
        in_specs=[pl.BlockSpec(memory_space=pltpu.SMEM),
                  pl.BlockSpec((None, None, DA_DV, bq), lambda h, i, j: (i, j, h, 0)),
                  pl.BlockSpec((None, tk, DA_DV), lambda h, i, j: (i, 0, h)),
                  pl.BlockSpec((None, nkb, DA_DV, bk), lambda h, i, j: (i, 0, h, 0)),
                  pl.BlockSpec((None, 2, 1, bq + bk), lambda h, i, j: (h, 0, 0, 0)),
                  pl.BlockSpec((DA_DV, 1), lambda h, i, j: (0, 0))],
        out_specs=pl.BlockSpec((None, bq, DA_DV), lambda h, i, j: (i, j, h)),
        out_shape=jax.ShapeDtypeStruct((b, nq * bq, DA_WIDTH), BF16),
        scratch_shapes=[pltpu.VMEM((2, 1, bq), F32), pltpu.VMEM((2, 1, bq), F32),
                        pltpu.VMEM((2, DA_DV, bq), F32),
                        pltpu.VMEM((2, 2, bk, bq), F32), pltpu.VMEM((2, 2, bk, bq), F32),
                        pltpu.VMEM((2, 1, bk, bq), F32),
                        pltpu.VMEM((2, 2 * bk, bq), BF16), pltpu.VMEM((2, 2 * bk, bq), BF16),
                        pltpu.VMEM((2, 1, bq), F32),
                        pltpu.VMEM((2, DA_DV, bq), BF16),
                        pltpu.VMEM((2, bk, bq), F32)],
        compiler_params=pltpu.CompilerParams(dimension_semantics=("arbitrary", "arbitrary", "arbitrary"),
                                             vmem_limit_bytes=VMEM_LIMIT),
        name="attn",
    )(scal, qt, kb, vt, btab, subw)


def _attn_cached_kernel(scal_ref, qt_ref, kc_ref, vc_ref, kn_ref, vtn_ref, bias0_ref, bias1_ref, subw_ref, o_ref,
                        *, bq, tq, bk, out_scale):
    lam = scal_ref[0]
    past = kc_ref.shape[0] // DA_HEADS
    lane = lax.broadcasted_iota(jnp.int32, (tq, DA_DV), 1)
    nt = (((1,), (1,)), ((), ()))

    for h in range(DA_HEADS):
        hs = slice(h * DA_DV, (h + 1) * DA_DV)
        cfar = scal_ref[1 + h]
        qn = qt_ref[hs, :].astype(F32).T[0:tq, :]
        q2 = jnp.concatenate([jnp.where(lane < DA_DK, qn, 0.0), jnp.where(lane >= DA_DK, qn, 0.0)],
                             axis=0).astype(BF16)
        head_rows = pl.ds(h, past, stride=DA_HEADS)
        s_c = lax.dot_general(q2, kc_ref[head_rows, :].astype(BF16), nt, preferred_element_type=F32)
        s_n = lax.dot_general(q2, kn_ref[:, hs], nt, preferred_element_type=F32)
        b0 = bias0_ref[h]
        b1 = bias1_ref[h]
        s = jnp.concatenate([s_c[:, 0:past - bk] + cfar,
                             s_c[:, past - bk:past] + jnp.concatenate([b0, b0], axis=0),
                             s_n + jnp.concatenate([b1, b1], axis=0)], axis=1)
        p = jnp.exp2(s - jnp.max(s, axis=1, keepdims=True))
        inv = 1.0 / jnp.sum(p, axis=1, keepdims=True)
        pb = p.astype(BF16)
        acc = (jnp.dot(pb[:, 0:past], vc_ref[head_rows, :].astype(BF16), preferred_element_type=F32)
               + jnp.dot(pb[:, past:past + bq], vtn_ref[hs, :].astype(F32).T.astype(BF16),
                         preferred_element_type=F32))
        o = acc[0:tq, :] * inv[0:tq] - acc[tq:2 * tq, :] * (lam * inv[tq:2 * tq])
        o_ref[0:tq, hs] = (_rms(o, subw_ref[...]) * out_scale).astype(BF16)
        o_ref[tq:bq, hs] = jnp.zeros((bq - tq, DA_DV), BF16)


def _attn_cached_call(scal, qt, cache_k, cache_v, kb, vt, bias0, bias1, subw, *, tq, bk, out_scale):
    b, past = cache_k.shape[:2]
    bq = qt.shape[-1]
    rows = past * DA_HEADS
    kern = functools.partial(_attn_cached_kernel, bq=bq, tq=tq, bk=bk, out_scale=out_scale)
    cache_spec = pl.BlockSpec((None, rows, DA_DV), lambda i: (i, 0, 0))
    return pl.pallas_call(
        kern,
        grid=(b,),
        in_specs=[pl.BlockSpec(memory_space=pltpu.SMEM),
                  pl.BlockSpec((None, None, DA_WIDTH, bq), lambda i: (i, 0, 0, 0)),
                  cache_spec, cache_spec,
                  pl.BlockSpec((None, bq, DA_WIDTH), lambda i: (i, 0, 0)),
                  pl.BlockSpec((None, None, DA_WIDTH, bq), lambda i: (i, 0, 0, 0)),
                  _const_spec((DA_HEADS, tq, bk)), _const_spec((DA_HEADS, tq, bq)),
                  _const_spec((1, DA_DV))],
        out_specs=pl.BlockSpec((None, bq, DA_WIDTH), lambda i: (i, 0, 0)),
        out_shape=jax.ShapeDtypeStruct((b, bq, DA_WIDTH), BF16),
        compiler_params=pltpu.CompilerParams(dimension_semantics=("parallel",), vmem_limit_bytes=VMEM_LIMIT),
        name="attn_cached",
    )(scal, qt, cache_k.reshape(b, rows, DA_DV), cache_v.reshape(b, rows, DA_DV), kb, vt, bias0, bias1, subw)


def _rel_bucket(rel):
    nb = REL_BUCKETS // 2
    max_exact = nb // 2
    n = jnp.abs(rel)
    nf = jnp.maximum(n, 1).astype(jnp.float32)
    large = max_exact + (jnp.log(nf / max_exact) / math.log(REL_MAX_DIST / max_exact)
                         * (nb - max_exact)).astype(jnp.int32)
    large = jnp.minimum(large, nb - 1)
    return jnp.where(rel > 0, nb, 0) + jnp.where(n < max_exact, n, large)


def _bias_tables(rel_bias, bq, bk):
    tabs = []
    for d in range(2):
        offs = (d - 1) * bk + bk - np.arange(bq + bk)
        tabs.append(rel_bias[_rel_bucket(jnp.asarray(offs, jnp.int32))].astype(F32).T * LOG2E)
    return jnp.stack(tabs, axis=1)[:, :, None, :]


def _bias_rows(rel_bias, qpos0, tq, kpos0, nkeys, tk_real):
    offs = (kpos0 - qpos0) + np.arange(-(tq - 1), nkeys)
    table = rel_bias[_rel_bucket(jnp.asarray(offs, jnp.int32))].astype(F32).T * LOG2E
    rows = jnp.stack([table[:, tq - 1 - i:tq - 1 - i + nkeys] for i in range(tq)], axis=1)
    qpos = qpos0 + np.arange(tq)
    kpos = kpos0 + np.arange(nkeys)
    vis = (kpos[None, :] // CHUNK <= qpos[:, None] // CHUNK) & (kpos[None, :] < tk_real)
    return jnp.where(jnp.asarray(vis)[None], rows, NEG)


def _ffn_kernel(x_ref, ys_ref, ya_ref, mod_ref, wo1_ref, wo2_ref, nfw_ref, wu_ref, cw_ref, cb_ref, wd_ref, fw_ref,
                hist_ref, y_ref, tail_ref, tail_scr, buf_a, buf_b, x1_scr, h2_scr, act_scr, *, tm, real):
    t = pl.program_id(1)
    nh = FFN_CONV - 1
    lo = SUBLANES - nh

    @pl.when(t == 0)
    def _init():
        tail_scr[lo:SUBLANES, :] = hist_ref[...]

    mix = (jnp.dot(ys_ref[...], wo1_ref[...], preferred_element_type=F32)
           + jnp.dot(ya_ref[...], wo2_ref[...], preferred_element_type=F32))
    x1 = x_ref[...] + mod_ref[2:3, :] * mix
    x1_scr[...] = x1
    h2 = _rms(x1, nfw_ref[...]) * (1.0 + mod_ref[4:5, :]) + mod_ref[3:4, :]
    h2_scr[...] = h2.astype(BF16)

    def cols(j, half):
        return slice(half * D_FF + j * FFN_CN, half * D_FF + (j + 1) * FFN_CN)

    def up(j, buf):
        for half in range(2):
            buf[half, SUBLANES:SUBLANES + tm, :] = jnp.dot(h2_scr[...], wu_ref[:, cols(j, half)],
                                                            preferred_element_type=F32)

    def conv(j, buf, half):
        cs = cols(j, half)
        buf[half, lo:SUBLANES, :] = tail_scr[lo:SUBLANES, cs]
        c = cb_ref[:, cs]
        for i in range(FFN_CONV):
            c = c + cw_ref[i:i + 1, cs] * buf[half, lo + i:lo + i + tm, :]
        tail_scr[lo:SUBLANES, cs] = buf[half, lo + real:SUBLANES + real, :]
        return c

    bufs = (buf_a, buf_b)
    up(0, bufs[0])
    for j in range(FFN_NC):
        if j + 1 < FFN_NC:
            up(j + 1, bufs[(j + 1) % 2])
        cv = conv(j, bufs[j % 2], 0)
        cg = conv(j, bufs[j % 2], 1)
        act_scr[:, j * FFN_CN:(j + 1) * FFN_CN] = (_silu(cg) * cv).astype(BF16)

    f = jnp.dot(act_scr[...], wd_ref[...], preferred_element_type=F32)
    x2 = x1_scr[...] + mod_ref[5:6, :] * f
    y_ref[...] = _rms(x2, fw_ref[...])
    tail_ref[...] = tail_scr[lo:SUBLANES, :]


def _ffn_call(x, ys, ya, mod3, wo1, wo2, nfw, wu, cw, cb, wd, fw, hist, *, tm, real):
    b, t, d = x.shape
    nh = FFN_CONV - 1

    def row(width):
        return pl.BlockSpec((None, tm, width), lambda i, j: (i, j, 0))

    hist_spec = pl.BlockSpec((None, nh, 2 * D_FF), lambda i, j: (i, 0, 0))
    kern = functools.partial(_ffn_kernel, tm=tm, real=real)
    return pl.pallas_call(
        kern,
        grid=(b, t // tm),
        in_specs=[row(d), row(SSD_WIDTH), row(DA_WIDTH),
                  pl.BlockSpec((None, 6, d), lambda i, j: (i, 0, 0)),
                  _const_spec((SSD_WIDTH, d)), _const_spec((DA_WIDTH, d)), _const_spec((1, d)),
                  _const_spec((d, 2 * D_FF)), _const_spec((FFN_CONV, 2 * D_FF)), _const_spec((1, 2 * D_FF)),
                  _const_spec((D_FF, d)), _const_spec((1, d)),
                  hist_spec],
        out_specs=[row(d), hist_spec],
        out_shape=[jax.ShapeDtypeStruct((b, t, d), F32),
                   jax.ShapeDtypeStruct((b, nh, 2 * D_FF), F32)],
        scratch_shapes=[pltpu.VMEM((SUBLANES, 2 * D_FF), F32),
                        pltpu.VMEM((2, tm + SUBLANES, FFN_CN), F32), pltpu.VMEM((2, tm + SUBLANES, FFN_CN), F32),
                        pltpu.VMEM((tm, d), F32), pltpu.VMEM((tm, d), BF16), pltpu.VMEM((tm, D_FF), BF16)],
        compiler_params=pltpu.CompilerParams(dimension_semantics=("parallel", "arbitrary"),
                                             vmem_limit_bytes=VMEM_LIMIT),
        name="ffn",
    )(x, ys, ya, mod3, wo1, wo2, nfw, wu, cw, cb, wd, fw, hist)


def _pack_params(norm_mix_w, w_in, ssm_conv_w, ssm_conv_b, ssm_dt_bias, ssm_a_log, ssm_d, ssm_norm_w,
                 lambda_q1, lambda_k1, lambda_q2, lambda_k2, attn_subln_w, rel_bias, w_out,
                 norm_ffn_w, w_up, ffn_conv_w, ffn_conv_b, w_down, final_norm_w, layer):
    l = layer
    wz, wx, wdt, wq, wk, wv = jnp.split(w_in[l], IN_SPLITS, axis=-1)
    wdt = jnp.pad(wdt, ((0, 0), (0, LANES - SSD_HEADS)))
    w_cat = jnp.concatenate([wz, wx, wdt, wk, wv], axis=-1).astype(BF16)
    w_t = (wq * (DA_DK ** -0.5 * LOG2E)).T.astype(BF16)

    def pad_heads(v):
        return jnp.pad(v.astype(F32), (0, LANES - SSD_HEADS)).reshape(1, LANES)

    lam_init = 0.8 - 0.6 * math.exp(-0.3 * l)
    lam = (jnp.exp(jnp.sum(lambda_q1[l].astype(F32) * lambda_k1[l].astype(F32)))
           - jnp.exp(jnp.sum(lambda_q2[l].astype(F32) * lambda_k2[l].astype(F32))) + lam_init)
    far_bias = rel_bias[REL_BUCKETS // 2 - 1].astype(F32)
    return dict(
        norm_mix_w=norm_mix_w[l].reshape(1, D_MODEL), w_cat=w_cat, w_t=w_t,
        cw=ssm_conv_w[l], cbias=ssm_conv_b[l].reshape(1, SSD_CONV_DIM),
        dtb=pad_heads(ssm_dt_bias[l]), alog=pad_heads(ssm_a_log[l]),
        dsk=jnp.repeat(ssm_d[l].astype(F32), SSD_HEADDIM).reshape(1, SSD_WIDTH),
        ssm_nw=ssm_norm_w[l].reshape(1, SSD_WIDTH),
        scal=jnp.concatenate([lam.reshape(1), far_bias * LOG2E]).astype(F32), lam_init=lam_init,
        subw=attn_subln_w[l].reshape(DA_DV, 1), rel_bias=rel_bias,
        wo1=w_out[l][:SSD_WIDTH].astype(BF16), wo2=w_out[l][SSD_WIDTH:].astype(BF16),
        nfw=norm_ffn_w[l].reshape(1, D_MODEL),
        wu=w_up[l].astype(BF16), ffn_cw=ffn_conv_w[l], ffn_cb=ffn_conv_b[l].reshape(1, 2 * D_FF),
        wd=w_down[l].astype(BF16), fw=final_norm_w.reshape(1, D_MODEL),
    )


def _state_to_kernel(h):
    return h.reshape(h.shape[0], SSD_GROUPS, GROUP_W, SSD_STATE)


def _state_from_kernel(h):
    return h.reshape(h.shape[0], SSD_HEADS, SSD_HEADDIM, SSD_STATE)


def _run_group(x, mod, past_k, past_v, ssm_h0, ssm_conv_hist, ffn_conv_hist, p, *, tm, ssd_rows, bq, bk):
    b, t, d = x.shape
    past = 0 if past_k is None else past_k.shape[1]
    chunk = min(CHUNK, t)
    tp = max(t, SUPER)
    if tp != t:
        x = jnp.pad(x, ((0, 0), (0, tp - t), (0, 0)))
        tm = ssd_rows = bq = tp
    mod3 = mod.reshape(b, 6, d)

    assert tm == bq
    zs, xc, dt, k, v, kb, qt, vt, conv_new = _inproj_call(
        x, mod3, p["norm_mix_w"], p["w_cat"], p["w_t"], ssm_conv_hist.astype(F32), p["cw"], p["cbias"],
        tm=tm, real=min(t, tm))

    y_ssd, h_t = _ssd_call(zs, xc, dt, _state_to_kernel(ssm_h0.astype(F32)),
                           p["dtb"], p["alog"], p["dsk"], p["ssm_nw"],
                           chunk=chunk, rows=ssd_rows, real=min(t, ssd_rows))

    if past == 0:
        assert bq == bk and t % bq == 0
        y_att = _attn_call(p["scal"], qt, kb, vt, _bias_tables(p["rel_bias"], bq, bk), p["subw"], bq=bq, bk=bk, noff=0,
                           out_scale=1.0 - p["lam_init"])
    else:
        assert past % bk == 0 and past >= bk and bq == tp <= bk
        bias0 = _bias_rows(p["rel_bias"], past, t, past - bk, bk, past + t)
        bias1 = _bias_rows(p["rel_bias"], past, t, past, bq, past + t)
        y_att = _attn_cached_call(p["scal"], qt, past_k, past_v, kb, vt, bias0, bias1, p["subw"].reshape(1, DA_DV),
                                  tq=t, bk=bk, out_scale=1.0 - p["lam_init"])

    y, ffn_new = _ffn_call(x, y_ssd, y_att, mod3, p["wo1"], p["wo2"], p["nfw"], p["wu"], p["ffn_cw"], p["ffn_cb"],
                           p["wd"], p["fw"], ffn_conv_hist.astype(F32), tm=tm, real=min(t, tm))
    k = k.reshape(b, tp, DA_HEADS, 2 * DA_DK)
    v = v.reshape(b, tp, DA_HEADS, DA_DV)
    return (y[:, :t], k[:, :t], v[:, :t],
            _state_from_kernel(h_t), conv_new, ffn_new)


def kernel(x_prompt, x_sample, c_prompt, c_sample, cache_k, cache_v, state_ssm, state_ssm_conv, state_ffn_conv, w_ada, b_ada, norm_mix_w, w_in, ssm_conv_w, ssm_conv_b, ssm_dt_bias, ssm_a_log, ssm_d, ssm_norm_w, lambda_q1, lambda_k1, lambda_q2, lambda_k2, attn_subln_w, rel_bias, w_out, norm_ffn_w, w_up, ffn_conv_w, ffn_conv_b, w_down, final_norm_w):
    bp, bs = x_prompt.shape[0], x_sample.shape[0]
    dt = x_prompt.dtype
    p = _pack_params(norm_mix_w, w_in, ssm_conv_w, ssm_conv_b, ssm_dt_bias, ssm_a_log, ssm_d, ssm_norm_w,
                     lambda_q1, lambda_k1, lambda_q2, lambda_k2, attn_subln_w, rel_bias, w_out,
                     norm_ffn_w, w_up, ffn_conv_w, ffn_conv_b, w_down, final_norm_w, 0)
    c_all = jnp.concatenate([c_prompt, c_sample], axis=0)
    npad = -c_all.shape[0] % SUBLANES
    c_all = jnp.pad(c_all, ((0, npad), (0, 0)))
    mod = _mod_call(c_all, w_ada[0], b_ada[0].reshape(1, -1))

    zeros = lambda *s: jnp.zeros(s, dt)
    out_p = _run_group(x_prompt, mod[:bp], None, None,
                       zeros(bp, SSD_HEADS, SSD_HEADDIM, SSD_STATE), zeros(bp, SSD_CONV - 1, SSD_CONV_DIM),
                       zeros(bp, FFN_CONV - 1, 2 * D_FF), p, tm=512, ssd_rows=256, bq=512, bk=512)
    out_s = _run_group(x_sample, mod[bp:bp + bs], cache_k[0], cache_v[0], state_ssm[0], state_ssm_conv[0],
                       state_ffn_conv[0], p, tm=SUPER, ssd_rows=SUPER, bq=SUPER, bk=512)
    y_p, k_p, v_p, h_p, c_p, f_p = out_p
    y_s, k_s, v_s, h_s, c_s, f_s = out_s
    return (y_p, y_s, k_p[None], v_p[None], h_p[None], c_p[None], f_p[None],
            k_s[None], v_s[None], h_s[None], c_s[None], f_s[None])
```

```python
import functools
import math

import numpy as np
import jax
import jax.numpy as jnp
from jax import lax
from jax.experimental import pallas as pl
from jax.experimental.pallas import tpu as pltpu

F32 = jnp.float32
BF16 = jnp.bfloat16
HIGHEST = lax.Precision.HIGHEST

D_MODEL = 1024
CHUNK = 64
SSD_WIDTH = 512
SSD_HEADDIM = 64
SSD_HEADS = 8
SSD_GROUPS = 2
SSD_HPG = 4
SSD_STATE = 128
SSD_CONV = 4
SSD_CONV_DIM = SSD_WIDTH + 2 * SSD_GROUPS * SSD_STATE
GROUP_W = SSD_HPG * SSD_HEADDIM
DA_WIDTH = 512
DA_DK = 64
DA_DV = 128
DA_HEADS = 4
REL_BUCKETS = 32
REL_MAX_DIST = 128
D_FF = 2816
FFN_CONV = 3
EPS = 1e-6
IN_SPLITS = (512, 1536, 1544, 2056, 2568)
LANES = 128
SUBLANES = 8
TILE_ROWS = 512
SUPER = 128
SSD_NSEQ = 4
ATT_KT = 64
FFN_CN = 256
FFN_NC = D_FF // FFN_CN
NEG = -1e30
VMEM_LIMIT = 56 * 1024 * 1024

PZ, PX, PDT, PK, PV, PEND = 0, 512, 1536, 1664, 2176, 2688
LOG2E = math.log2(math.e)


def _silu(x):
    return x / (1.0 + jnp.exp(-x))


def _softplus(x):
    return jnp.maximum(x, 0.0) + jnp.log1p(jnp.exp(-jnp.abs(x)))


def _split3(x):
    hi = x.astype(BF16)
    r1 = x - hi.astype(F32)
    mid = r1.astype(BF16)
    lo = (r1 - mid.astype(F32)).astype(BF16)
    return hi, mid, lo


def _rms(x, w):
    return x * lax.rsqrt(jnp.mean(x * x, axis=-1, keepdims=True) + EPS) * w


def _const_spec(shape):
    nd = len(shape)
    return pl.BlockSpec(shape, lambda *_: (0,) * nd)


def _mod_kernel(c_ref, w_ref, b_ref, o_ref):
    a = _silu(c_ref[...]).astype(BF16)
    o_ref[...] = jnp.dot(a, w_ref[...].astype(BF16), preferred_element_type=F32) + b_ref[...]


def _mod_call(c, w_ada, b_ada):
    n, d = c.shape
    nout = w_ada.shape[1]
    tn = 1024
    return pl.pallas_call(
        _mod_kernel,
        grid=(nout // tn,),
        in_specs=[pl.BlockSpec((n, d), lambda j: (0, 0)),
                  pl.BlockSpec((d, tn), lambda j: (0, j)),
                  pl.BlockSpec((1, tn), lambda j: (0, j))],
        out_specs=pl.BlockSpec((n, tn), lambda j: (0, j)),
        out_shape=jax.ShapeDtypeStruct((n, nout), F32),
        name="mod",
    )(c, w_ada, b_ada)


def _inproj_kernel(x_ref, mod_ref, nw_ref, w_ref, wt_ref, hist_ref, cw_ref, cbias_ref,
                   zs_ref, xc_ref, dt_ref, k_ref, v_ref, kb_ref, qt_ref, vt_ref, cout_ref, cbuf, hb_scr, zbuf,
                   *, tm, real):
    t = pl.program_id(1)
    nconv = SSD_CONV - 1

    @pl.when(t == 0)
    def _init():
        cbuf[0:SUBLANES, :] = jnp.zeros((SUBLANES, SSD_CONV_DIM), F32)
        cbuf[SUBLANES - nconv:SUBLANES, :] = hist_ref[...]

    h = _rms(x_ref[...], nw_ref[...]) * (1.0 + mod_ref[1:2, :]) + mod_ref[0:1, :]
    hb_scr[...] = h.astype(BF16)

    def proj(a, b):
        return jnp.dot(hb_scr[...], w_ref[:, a:b], preferred_element_type=F32)

    def proj_t(a, b):
        return lax.dot_general(wt_ref[a:b, :], hb_scr[...], (((1,), (1,)), ((), ())), preferred_element_type=F32)

    cbuf[SUBLANES:SUBLANES + tm, :] = proj(PX, PDT)
    zbuf[...] = proj(PZ, PX)
    dt_ref[...] = proj(PDT, PK)
    k = proj(PK, PV)
    v = proj(PV, PEND)
    for hd in range(DA_HEADS):
        dst = pl.ds(hd, tm, stride=DA_HEADS)
        k_ref[dst, :] = k[:, hd * DA_DV:(hd + 1) * DA_DV]
        v_ref[dst, :] = v[:, hd * DA_DV:(hd + 1) * DA_DV]
    kb_ref[...] = k.astype(BF16)
    qt_ref[...] = proj_t(0, DA_WIDTH).astype(BF16)
    for hd in range(DA_HEADS):
        vt_ref[hd * DA_DV:(hd + 1) * DA_DV, :] = v[:, hd * DA_DV:(hd + 1) * DA_DV].T.astype(BF16)
    conv = cbias_ref[...]
    for j in range(SSD_CONV):
        off = SUBLANES - nconv + j
        conv = conv + cw_ref[j:j + 1, :] * cbuf[off:off + tm, :]
    tail = cbuf[SUBLANES - nconv + real:SUBLANES + real, :]
    cout_ref[...] = tail
    cbuf[SUBLANES - nconv:SUBLANES, :] = tail
    xc_ref[...] = _silu(conv).astype(BF16)
    zs_ref[...] = _silu(zbuf[...]).astype(BF16)


def _inproj_call(x, mod3, norm_w, w_cat, w_t, hist, cw, cbias, *, tm, real):
    b, t, d = x.shape
    nt = t // tm

    def row(width):
        return pl.BlockSpec((None, tm, width), lambda i, j: (i, j, 0))

    def out(width, dtype):
        return jax.ShapeDtypeStruct((b, t, width), dtype)

    tspec = pl.BlockSpec((None, None, DA_WIDTH, tm), lambda i, j: (i, j, 0, 0))
    tshape = jax.ShapeDtypeStruct((b, nt, DA_WIDTH, tm), BF16)
    hist_spec = pl.BlockSpec((None, SSD_CONV - 1, SSD_CONV_DIM), lambda i, j: (i, 0, 0))
    hspec = pl.BlockSpec((None, tm * DA_HEADS, DA_DV), lambda i, j: (i, j, 0))
    hshape = jax.ShapeDtypeStruct((b, t * DA_HEADS, DA_DV), F32)
    return pl.pallas_call(
        functools.partial(_inproj_kernel, tm=tm, real=real),
        grid=(b, nt),
        in_specs=[row(d),
                  pl.BlockSpec((None, 6, d), lambda i, j: (i, 0, 0)),
                  _const_spec((1, d)),
                  _const_spec((d, PEND)),
                  _const_spec((DA_WIDTH, d)),
                  hist_spec, _const_spec((SSD_CONV, SSD_CONV_DIM)), _const_spec((1, SSD_CONV_DIM))],
        out_specs=[row(512), row(1024), row(LANES), hspec, hspec, row(512), tspec, tspec, hist_spec],
        out_shape=[out(512, BF16), out(1024, BF16), out(LANES, F32),
                   hshape, hshape, out(512, BF16), tshape, tshape,
                   jax.ShapeDtypeStruct((b, SSD_CONV - 1, SSD_CONV_DIM), F32)],
        scratch_shapes=[pltpu.VMEM((tm + SUBLANES, SSD_CONV_DIM), F32), pltpu.VMEM((tm, d), BF16),
                        pltpu.VMEM((tm, SSD_WIDTH), F32)],
        compiler_params=pltpu.CompilerParams(dimension_semantics=("parallel", "arbitrary"),
                                             vmem_limit_bytes=VMEM_LIMIT),
        name="inproj",
    )(x, mod3, norm_w, w_cat, w_t, hist, cw, cbias)


def _ssd_kernel(zs_ref, xc_ref, dt_ref, h0_ref, dtb_ref, alog_ref, dsk_ref, nw_ref, tri_ref, e_ref,
                y_ref, hout_ref, h_scr, ybuf, *, nseq, chunk, rows, real):
    t = pl.program_id(1)
    seqs = range(nseq)

    @pl.when(t == 0)
    def _init():
        for i in seqs:
            for g in range(SSD_GROUPS):
                h_scr[i, g] = h0_ref[i, g].T

    li = lax.broadcasted_iota(jnp.int32, (SUPER, SUPER), 0)
    si = lax.broadcasted_iota(jnp.int32, (SUPER, SUPER), 1)
    cshift = chunk.bit_length() - 1
    mask2 = ((li >> cshift) == (si >> cshift)) & (si <= li)
    lane_g = lax.broadcasted_iota(jnp.int32, (SUPER, GROUP_W), 1) >> (SSD_HEADDIM.bit_length() - 1)

    pre = []
    for i in seqs:
        xs = xc_ref[i, :, 0:SSD_WIDTH].astype(F32)
        dtv = _softplus(dt_ref[i] + dtb_ref[...])
        da = dtv * (-jnp.exp(alog_ref[...]))
        acs = jnp.dot(tri_ref[...], jnp.concatenate(_split3(da), axis=0), preferred_element_type=F32)
        dt_x = jnp.dot(jnp.concatenate(_split3(dtv), axis=1), e_ref[...], preferred_element_type=F32)
        acs_x = jnp.dot(jnp.concatenate(_split3(acs), axis=1), e_ref[...], preferred_element_type=F32)
        pre.append((xs, acs, acs_x, jnp.exp(acs_x), xs * dt_x))

    for sb in range(rows // SUPER):
        o = sb * SUPER
        nreal = (min(real, o + SUPER) - o) // chunk
        acs2 = [pre[i][1][o:o + SUPER, :] for i in seqs]
        acs_t = [a.T for a in acs2]
        chains = [(g, i) for g in range(SSD_GROUPS) for i in seqs]

        def gsl(g):
            return slice(g * GROUP_W, (g + 1) * GROUP_W)

        cmb, cb2, bm_t = {}, {}, {}
        for g, i in chains:
            bcol = SSD_WIDTH + g * SSD_STATE
            ccol = SSD_WIDTH + (SSD_GROUPS + g) * SSD_STATE
            bmb = xc_ref[i, o:o + SUPER, bcol:bcol + SSD_STATE]
            cmb[g, i] = xc_ref[i, o:o + SUPER, ccol:ccol + SSD_STATE]
            cb2[g, i] = lax.dot_general(cmb[g, i], bmb, (((1,), (1,)), ((), ())), preferred_element_type=F32)
            bm_t[g, i] = bmb.astype(F32).T.astype(BF16)
        for g, i in chains:
            ms = []
            for rr in range(SSD_HPG):
                r = g * SSD_HPG + rr
                seg = acs2[i][:, r:r + 1] - acs_t[i][r:r + 1, :]
                dec = jnp.where(mask2, jnp.exp(jnp.where(mask2, seg, 0.0)), 0.0)
                ms.append((cb2[g, i] * dec).astype(BF16))
            full = jnp.dot(jnp.concatenate(ms, axis=0), pre[i][4][o:o + SUPER, gsl(g)].astype(BF16),
                           preferred_element_type=F32)
            ydiag = full[0:SUPER]
            for rr in range(1, SSD_HPG):
                ydiag = jnp.where(lane_g == rr, full[rr * SUPER:(rr + 1) * SUPER], ydiag)
            ybuf[i, o:o + SUPER, gsl(g)] = ydiag
        for j in range(nreal):
            a0, a1 = o + j * chunk, o + (j + 1) * chunk
            h_t = {c: h_scr[c[1], c[0]] for c in chains}
            yoff = {(g, i): jnp.dot(cmb[g, i][j * chunk:(j + 1) * chunk, :], h_t[g, i].astype(BF16),
                                    preferred_element_type=F32) for g, i in chains}
            st = {}
            for g, i in chains:
                _, _, acs_x, eacs_x, xd = pre[i]
                ybuf[i, a0:a1, gsl(g)] = ybuf[i, a0:a1, gsl(g)] + yoff[g, i] * eacs_x[a0:a1, gsl(g)]
                dte = jnp.exp(acs_x[a1 - 1:a1, gsl(g)] - acs_x[a0:a1, gsl(g)])
                xw = (xd[a0:a1, gsl(g)] * dte).astype(BF16)
                pieces = []
                if j > 0:
                    pieces.append(jnp.zeros((j * chunk, GROUP_W), BF16))
                pieces.append(xw)
                if (j + 1) * chunk < SUPER:
                    pieces.append(jnp.zeros((SUPER - (j + 1) * chunk, GROUP_W), BF16))
                xw2 = jnp.concatenate(pieces, axis=0) if len(pieces) > 1 else xw
                st[g, i] = jnp.dot(bm_t[g, i], xw2, preferred_element_type=F32)
            for g, i in chains:
                h_scr[i, g] = h_t[g, i] * pre[i][3][a1 - 1:a1, gsl(g)] + st[g, i]

    for i in seqs:
        y = (ybuf[i] + dsk_ref[...] * pre[i][0]) * zs_ref[i].astype(F32)
        for g in range(SSD_GROUPS):
            gs = slice(g * GROUP_W, (g + 1) * GROUP_W)
            y_ref[i, :, gs] = _rms(y[:, gs], nw_ref[:, gs]).astype(BF16)

    @pl.when(t == pl.num_programs(1) - 1)
    def _fin():
        for i in seqs:
            for g in range(SSD_GROUPS):
                hout_ref[i, g] = h_scr[i, g].T


def _ssd_call(zs, xc, dt, h0_t, dtb, alog, dsk, nw, *, chunk, rows, real):
    b, t, _ = zs.shape
    ii = np.arange(rows)
    tri = ((ii[:, None] // chunk == ii[None, :] // chunk) & (ii[None, :] <= ii[:, None])).astype(np.float32)
    e = np.zeros((LANES, SSD_WIDTH), np.float32)
    for r in range(SSD_HEADS):
        e[r, r * SSD_HEADDIM:(r + 1) * SSD_HEADDIM] = 1.0

    nseq = math.gcd(b, SSD_NSEQ)

    def row(width):
        return pl.BlockSpec((nseq, rows, width), lambda i, j: (i, j, 0))

    state_spec = pl.BlockSpec((nseq, SSD_GROUPS, GROUP_W, SSD_STATE), lambda i, j: (i, 0, 0, 0))
    kern = functools.partial(_ssd_kernel, nseq=nseq, chunk=chunk, rows=rows, real=real)
    return pl.pallas_call(
        kern,
        grid=(b // nseq, t // rows),
        in_specs=[row(SSD_WIDTH), row(SSD_CONV_DIM), row(LANES), state_spec,
                  _const_spec((1, LANES)), _const_spec((1, LANES)),
                  _const_spec((1, SSD_WIDTH)), _const_spec((1, SSD_WIDTH)),
                  _const_spec((rows, 3 * rows)), _const_spec((3 * LANES, SSD_WIDTH))],
        out_specs=[row(SSD_WIDTH), state_spec],
        out_shape=[jax.ShapeDtypeStruct((b, t, SSD_WIDTH), BF16),
                   jax.ShapeDtypeStruct((b, SSD_GROUPS, GROUP_W, SSD_STATE), F32)],
        scratch_shapes=[pltpu.VMEM((nseq, SSD_GROUPS, SSD_STATE, GROUP_W), F32),
                        pltpu.VMEM((nseq, rows, SSD_WIDTH), F32)],
        compiler_params=pltpu.CompilerParams(dimension_semantics=("parallel", "arbitrary"),
                                             vmem_limit_bytes=VMEM_LIMIT),
        name="ssd",
    )(zs, xc, dt, h0_t, dtb, alog, dsk, nw,
      jnp.asarray(np.tile(tri, (1, 3)), BF16), jnp.asarray(np.tile(e, (3, 1)), BF16))


def _attn_kernel(scal_ref, qt_ref, k_ref, vt_ref, btab_ref, subw_ref, o_ref,
                 m_scr, l_scr, acc_scr, sa_scr, sb_scr, sc_scr, pa_scr, pb_scr, alpha_scr, qz_scr, bias_scr,
                 *, bq, bk, noff, out_scale):
    h = pl.program_id(0)
    qi = pl.program_id(2)
    kn0 = qi + (noff - 1)
    lam = scal_ref[0]
    cfar = scal_ref[1 + h]

    @pl.when((pl.program_id(1) == 0) & (qi == 0))
    def _build_bias_tiles():
        width = bq + bk
        kj = lax.broadcasted_iota(jnp.int32, (bk, bq), 0)
        qj = lax.broadcasted_iota(jnp.int32, (bk, bq), 1)
        cshift = CHUNK.bit_length() - 1
        for d in range(2):
            skew = pltpu.roll(jnp.broadcast_to(btab_ref[d], (bk, width)), 0, 1, stride=1, stride_axis=0)
            tile = skew[:, bk:width]
            if d == 1:
                tile = jnp.where((kj >> cshift) <= (qj >> cshift), tile, NEG)
            bias_scr[d] = tile

    zero = jnp.zeros((DA_DK, bq), BF16)
    qz_scr[0, 0:DA_DK, :] = qt_ref[0:DA_DK, :]
    qz_scr[0, DA_DK:DA_DV, :] = zero
    qz_scr[1, 0:DA_DK, :] = zero
    qz_scr[1, DA_DK:DA_DV, :] = qt_ref[DA_DK:DA_DV, :]

    m_scr[...] = jnp.full(m_scr.shape, NEG, F32)
    l_scr[...] = jnp.zeros(l_scr.shape, F32)
    acc_scr[...] = jnp.zeros(acc_scr.shape, F32)

    nsub = bk // ATT_KT

    def fold(x):
        return x.reshape(ATT_KT // SUBLANES, SUBLANES, bq)

    def scores(s_buf, first, count):
        for mm in range(2):
            for e in range(count):
                start = pl.multiple_of((first + e) * bk, bk)
                s_buf[mm, e] = jnp.dot(k_ref[pl.ds(start, bk), :], qz_scr[mm], preferred_element_type=F32)

    def softmax(s_buf, p_buf, entries):
        alphas = []
        for mm in range(2):
            cand = None
            for e, (near, shift) in enumerate(entries):
                mx = None
                for t in range(nsub):
                    rows = slice(t * ATT_KT, (t + 1) * ATT_KT)
                    s = s_buf[mm, e, rows, :]
                    if near is not None:
                        s = s + bias_scr[near, rows, :]
                    pm = jnp.max(fold(s), axis=0)
                    mx = pm if mx is None else jnp.maximum(mx, pm)
                mx = jnp.max(mx, axis=0, keepdims=True) + shift
                cand = mx if cand is None else jnp.maximum(cand, mx)
            m_old = m_scr[mm]
            m_new = jnp.maximum(m_old, cand)
            ls = None
            for e, (near, shift) in enumerate(entries):
                off = m_new - shift
                for t in range(nsub):
                    rows = slice(t * ATT_KT, (t + 1) * ATT_KT)
                    s = s_buf[mm, e, rows, :]
                    if near is not None:
                        s = s + bias_scr[near, rows, :]
                    p = jnp.exp2(s - off)
                    p_buf[mm, e * bk + t * ATT_KT:e * bk + (t + 1) * ATT_KT, :] = p.astype(BF16)
                    ps = jnp.sum(fold(p), axis=0)
                    ls = ps if ls is None else ls + ps
            alpha = jnp.exp2(m_old - m_new)
            l_scr[mm] = alpha * l_scr[mm] + jnp.sum(ls, axis=0, keepdims=True)
            m_scr[mm] = m_new
            alphas.append(alpha)
        return alphas

    def pv(p_buf, first, n):
        vts = [vt_ref[first + e] for e in range(n)]
        vt = jnp.concatenate(vts, axis=1) if n > 1 else vts[0]
        return [jnp.dot(vt, p_buf[mm, 0:n * bk, :], preferred_element_type=F32) for mm in range(2)]

    def accumulate(alphas, pvs):
        for mm in range(2):
            acc_scr[mm] = alphas[mm] * acc_scr[mm] + pvs[mm]

    def softmax_pv(s_buf, first, entries):
        alphas = softmax(s_buf, pa_scr, entries)
        accumulate(alphas, pv(pa_scr, first, len(entries)))

    nfar = jnp.maximum(kn0, 0)
    odd = nfar % 2
    far = (None, cfar)

    near_pair = [(0, 0.0), (1, 0.0)]

    @pl.when(kn0 < 0)
    def _only_first():
        scores(sc_scr, 0, 1)
        softmax_pv(sc_scr, 0, [(1, 0.0)])

    @pl.when(kn0 >= 0)
    def _groups():
        @pl.when(odd == 1)
        def _single():
            scores(sc_scr, 0, 1)
            scores(sa_scr, 1, 2)
            softmax_pv(sc_scr, 0, [far])

        @pl.when(odd == 0)
        def _first_pair():
            scores(sa_scr, 0, 2)

        def far_pair(s_cur, s_next, cur):
            scores(s_next, cur + 2, 2)
            softmax_pv(s_cur, cur, [far, far])

        def far_step(s_cur, p_cur, s_next, p_prev, cur):
            scores(s_next, cur + 2, 2)
            pending = None if p_prev is None else pv(p_prev, cur - 2, 2)
            alphas = softmax(s_cur, p_cur, [far, far])
            if pending is not None:
                accumulate([alpha_scr[0], alpha_scr[1]], pending)
            for mm in range(2):
                alpha_scr[mm] = alphas[mm]

        npairs = nfar // 2
        niter = npairs // 2

        @pl.when(niter >= 1)
        def _far_loop():
            far_step(sa_scr, pa_scr, sb_scr, None, odd)
            far_step(sb_scr, pb_scr, sa_scr, pa_scr, odd + 2)

            def far_body(j, carry):
                cur = odd + 4 * j
                far_step(sa_scr, pa_scr, sb_scr, pb_scr, cur)
                far_step(sb_scr, pb_scr, sa_scr, pa_scr, cur + 2)
                return carry

            lax.fori_loop(1, niter, far_body, 0)
            accumulate([alpha_scr[0], alpha_scr[1]], pv(pb_scr, odd + 4 * niter - 2, 2))

        @pl.when(npairs % 2 == 1)
        def _tail_b():
            far_pair(sa_scr, sb_scr, kn0 - 2)
            softmax_pv(sb_scr, kn0, near_pair)

        @pl.when(npairs % 2 == 0)
        def _tail_a():
            softmax_pv(sa_scr, kn0, near_pair)

    o = acc_scr[0] * (1.0 / l_scr[0]) - acc_scr[1] * (lam / l_scr[1])
    o = o * lax.rsqrt(jnp.mean(o * o, axis=0, keepdims=True) + EPS) * (subw_ref[...] * out_scale)
    o_ref[...] = o.T.astype(BF16)


def _attn_call(scal, qt, kb, vt, btab, subw, *, bq, bk, noff, out_scale):
    b, nq = qt.shape[:2]
    tk = kb.shape[1]
    nkb = vt.shape[1]
    kern = functools.partial(_attn_kernel, bq=bq, bk=bk, noff=noff, out_scale=out_scale)
    return pl.pallas_call(
        kern,
        grid=(DA_HEADS, b, nq),
        in_specs=[pl.BlockSpec(memory_space=pltpu.SMEM),
                  pl.BlockSpec((None, None, DA_DV, bq), lambda h, i, j: (i, j, h, 0)),
                  pl.BlockSpec((None, tk, DA_DV), lambda h, i, j: (i, 0, h)),
                  pl.BlockSpec((None, nkb, DA_DV, bk), lambda h, i, j: (i, 0, h, 0)),
                  pl.BlockSpec((None, 2, 1, bq + bk), lambda h, i, j: (h, 0, 0, 0)),
                  pl.BlockSpec((DA_DV, 1), lambda h, i, j: (0, 0))],
        out_specs=pl.BlockSpec((None, bq, DA_DV), lambda h, i, j: (i, j, h)),
        out_shape=jax.ShapeDtypeStruct((b, nq * bq, DA_WIDTH), BF16),
        scratch_shapes=[pltpu.VMEM((2, 1, bq), F32), pltpu.VMEM((2, 1, bq), F32),
                        pltpu.VMEM((2, DA_DV, bq), F32),
                        pltpu.VMEM((2, 2, bk, bq), F32), pltpu.VMEM((2, 2, bk, bq), F32),
                        pltpu.VMEM((2, 1, bk, bq), F32),
                        pltpu.VMEM((2, 2 * bk, bq), BF16), pltpu.VMEM((2, 2 * bk, bq), BF16),
                        pltpu.VMEM((2, 1, bq), F32),
                        pltpu.VMEM((2, DA_DV, bq), BF16),
                        pltpu.VMEM((2, bk, bq), F32)],
        compiler_params=pltpu.CompilerParams(dimension_semantics=("arbitrary", "arbitrary", "arbitrary"),
                                             vmem_limit_bytes=VMEM_LIMIT),
        name="attn",
    )(scal, qt, kb, vt, btab, subw)


def _attn_cached_kernel(scal_ref, qt_ref, kc_ref, vc_ref, kn_ref, vtn_ref, bias0_ref, bias1_ref, subw_ref, o_ref,
                        *, bq, tq, bk, out_scale):
    lam = scal_ref[0]
    past = kc_ref.shape[0] // DA_HEADS
    lane = lax.broadcasted_iota(jnp.int32, (tq, DA_DV), 1)
    nt = (((1,), (1,)), ((), ()))

    for h in range(DA_HEADS):
        hs = slice(h * DA_DV, (h + 1) * DA_DV)
        cfar = scal_ref[1 + h]
        qn = qt_ref[hs, :].astype(F32).T[0:tq, :]
        q2 = jnp.concatenate([jnp.where(lane < DA_DK, qn, 0.0), jnp.where(lane >= DA_DK, qn, 0.0)],
                             axis=0).astype(BF16)
        head_rows = pl.ds(h, past, stride=DA_HEADS)
        s_c = lax.dot_general(q2, kc_ref[head_rows, :].astype(BF16), nt, preferred_element_type=F32)
        s_n = lax.dot_general(q2, kn_ref[:, hs], nt, preferred_element_type=F32)
        b0 = bias0_ref[h]
        b1 = bias1_ref[h]
        s = jnp.concatenate([s_c[:, 0:past - bk] + cfar,
                             s_c[:, past - bk:past] + jnp.concatenate([b0, b0], axis=0),
                             s_n + jnp.concatenate([b1, b1], axis=0)], axis=1)
        p = jnp.exp2(s - jnp.max(s, axis=1, keepdims=True))
        inv = 1.0 / jnp.sum(p, axis=1, keepdims=True)
        pb = p.astype(BF16)
        acc = (jnp.dot(pb[:, 0:past], vc_ref[head_rows, :].astype(BF16), preferred_element_type=F32)
               + jnp.dot(pb[:, past:past + bq], vtn_ref[hs, :].astype(F32).T.astype(BF16),
                         preferred_element_type=F32))
        o = acc[0:tq, :] * inv[0:tq] - acc[tq:2 * tq, :] * (lam * inv[tq:2 * tq])
        o_ref[0:tq, hs] = (_rms(o, subw_ref[...]) * out_scale).astype(BF16)
        o_ref[tq:bq, hs] = jnp.zeros((bq - tq, DA_DV), BF16)


def _attn_cached_call(scal, qt, cache_k, cache_v, kb, vt, bias0, bias1, subw, *, tq, bk, out_scale):
    b, past = cache_k.shape[:2]
    bq = qt.shape[-1]
    rows = past * DA_HEADS
    kern = functools.partial(_attn_cached_kernel, bq=bq, tq=tq, bk=bk, out_scale=out_scale)
    cache_spec = pl.BlockSpec((None, rows, DA_DV), lambda i: (i, 0, 0))
    return pl.pallas_call(
        kern,
        grid=(b,),
        in_specs=[pl.BlockSpec(memory_space=pltpu.SMEM),
                  pl.BlockSpec((None, None, DA_WIDTH, bq), lambda i: (i, 0, 0, 0)),
                  cache_spec, cache_spec,
                  pl.BlockSpec((None, bq, DA_WIDTH), lambda i: (i, 0, 0)),
                  pl.BlockSpec((None, None, DA_WIDTH, bq), lambda i: (i, 0, 0, 0)),
                  _const_spec((DA_HEADS, tq, bk)), _const_spec((DA_HEADS, tq, bq)),
                  _const_spec((1, DA_DV))],
        out_specs=pl.BlockSpec((None, bq, DA_WIDTH), lambda i: (i, 0, 0)),
        out_shape=jax.ShapeDtypeStruct((b, bq, DA_WIDTH), BF16),
        compiler_params=pltpu.CompilerParams(dimension_semantics=("parallel",), vmem_limit_bytes=VMEM_LIMIT),
        name="attn_cached",
    )(scal, qt, cache_k.reshape(b, rows, DA_DV), cache_v.reshape(b, rows, DA_DV), kb, vt, bias0, bias1, subw)


def _rel_bucket(rel):
    nb = REL_BUCKETS // 2
    max_exact = nb // 2
    n = jnp.abs(rel)
    nf = jnp.maximum(n, 1).astype(jnp.float32)
    large = max_exact + (jnp.log(nf / max_exact) / math.log(REL_MAX_DIST / max_exact)
                         * (nb - max_exact)).astype(jnp.int32)
    large = jnp.minimum(large, nb - 1)
    return jnp.where(rel > 0, nb, 0) + jnp.where(n < max_exact, n, large)


def _bias_tables(rel_bias, bq, bk):
    tabs = []
    for d in range(2):
        offs = (d - 1) * bk + bk - np.arange(bq + bk)
        tabs.append(rel_bias[_rel_bucket(jnp.asarray(offs, jnp.int32))].astype(F32).T * LOG2E)
    return jnp.stack(tabs, axis=1)[:, :, None, :]


def _bias_rows(rel_bias, qpos0, tq, kpos0, nkeys, tk_real):
    offs = (kpos0 - qpos0) + np.arange(-(tq - 1), nkeys)
    table = rel_bias[_rel_bucket(jnp.asarray(offs, jnp.int32))].astype(F32).T * LOG2E
    rows = jnp.stack([table[:, tq - 1 - i:tq - 1 - i + nkeys] for i in range(tq)], axis=1)
    qpos = qpos0 + np.arange(tq)
    kpos = kpos0 + np.arange(nkeys)
    vis = (kpos[None, :] // CHUNK <= qpos[:, None] // CHUNK) & (kpos[None, :] < tk_real)
    return jnp.where(jnp.asarray(vis)[None], rows, NEG)


def _ffn_kernel(x_ref, ys_ref, ya_ref, mod_ref, wo1_ref, wo2_ref, nfw_ref, wu_ref, cw_ref, cb_ref, wd_ref, fw_ref,
                hist_ref, y_ref, tail_ref, tail_scr, buf_a, buf_b, x1_scr, h2_scr, act_scr, *, nseq, tm):
    t = pl.program_id(1)
    nh = FFN_CONV - 1
    lo = SUBLANES - nh
    seqs = range(nseq)

    def rows(i):
        return slice(i * tm, (i + 1) * tm)

    def stacked(ref):
        return jnp.concatenate([ref[i] for i in seqs], axis=0) if nseq > 1 else ref[0]

    @pl.when(t == 0)
    def _init():
        for i in seqs:
            tail_scr[i, lo:SUBLANES, :] = hist_ref[i]

    mix = (jnp.dot(stacked(ys_ref), wo1_ref[...], preferred_element_type=F32)
           + jnp.dot(stacked(ya_ref), wo2_ref[...], preferred_element_type=F32))
    for i in seqs:
        x1 = x_ref[i] + mod_ref[i, 2:3, :] * mix[rows(i)]
        x1_scr[rows(i), :] = x1
        h2 = _rms(x1, nfw_ref[...]) * (1.0 + mod_ref[i, 4:5, :]) + mod_ref[i, 3:4, :]
        h2_scr[rows(i), :] = h2.astype(BF16)

    def cols(j, half):
        return slice(half * D_FF + j * FFN_CN, half * D_FF + (j + 1) * FFN_CN)

    def up(j, buf):
        for half in range(2):
            u = jnp.dot(h2_scr[...], wu_ref[:, cols(j, half)], preferred_element_type=F32)
            for i in seqs:
                buf[half, i, SUBLANES:SUBLANES + tm, :] = u[rows(i)]

    def conv(j, buf, half, i):
        cs = cols(j, half)
        buf[half, i, lo:SUBLANES, :] = tail_scr[i, lo:SUBLANES, cs]
        c = cb_ref[:, cs]
        for k in range(FFN_CONV):
            c = c + cw_ref[k:k + 1, cs] * buf[half, i, lo + k:lo + k + tm, :]
        tail_scr[i, lo:SUBLANES, cs] = buf[half, i, lo + tm:SUBLANES + tm, :]
        return c

    bufs = (buf_a, buf_b)
    up(0, bufs[0])
    for j in range(FFN_NC):
        if j + 1 < FFN_NC:
            up(j + 1, bufs[(j + 1) % 2])
        for i in seqs:
            cv = conv(j, bufs[j % 2], 0, i)
            cg = conv(j, bufs[j % 2], 1, i)
            act_scr[rows(i), j * FFN_CN:(j + 1) * FFN_CN] = (_silu(cg) * cv).astype(BF16)

    f = jnp.dot(act_scr[...], wd_ref[...], preferred_element_type=F32)
    for i in seqs:
        x2 = x1_scr[rows(i), :] + mod_ref[i, 5:6, :] * f[rows(i)]
        y_ref[i] = _rms(x2, fw_ref[...])
        tail_ref[i] = tail_scr[i, lo:SUBLANES, :]


def _ffn_call(x, ys, ya, mod3, wo1, wo2, nfw, wu, cw, cb, wd, fw, hist, *, nseq, tm):
    b, t, d = x.shape
    nh = FFN_CONV - 1
    m = nseq * tm

    def row(width):
        return pl.BlockSpec((nseq, tm, width), lambda i, j: (i, j, 0))

    hist_spec = pl.BlockSpec((nseq, nh, 2 * D_FF), lambda i, j: (i, 0, 0))
    kern = functools.partial(_ffn_kernel, nseq=nseq, tm=tm)
    return pl.pallas_call(
        kern,
        grid=(b // nseq, t // tm),
        in_specs=[row(d), row(SSD_WIDTH), row(DA_WIDTH),
                  pl.BlockSpec((nseq, 6, d), lambda i, j: (i, 0, 0)),
                  _const_spec((SSD_WIDTH, d)), _const_spec((DA_WIDTH, d)), _const_spec((1, d)),
                  _const_spec((d, 2 * D_FF)), _const_spec((FFN_CONV, 2 * D_FF)), _const_spec((1, 2 * D_FF)),
                  _const_spec((D_FF, d)), _const_spec((1, d)),
                  hist_spec],
        out_specs=[row(d), hist_spec],
        out_shape=[jax.ShapeDtypeStruct((b, t, d), F32),
                   jax.ShapeDtypeStruct((b, nh, 2 * D_FF), F32)],
        scratch_shapes=[pltpu.VMEM((nseq, SUBLANES, 2 * D_FF), F32),
                        pltpu.VMEM((2, nseq, tm + SUBLANES, FFN_CN), F32),
                        pltpu.VMEM((2, nseq, tm + SUBLANES, FFN_CN), F32),
                        pltpu.VMEM((m, d), F32), pltpu.VMEM((m, d), BF16), pltpu.VMEM((m, D_FF), BF16)],
        compiler_params=pltpu.CompilerParams(dimension_semantics=("parallel", "arbitrary"),
                                             vmem_limit_bytes=VMEM_LIMIT),
        name="ffn",
    )(x, ys, ya, mod3, wo1, wo2, nfw, wu, cw, cb, wd, fw, hist)


def _pack_params(norm_mix_w, w_in, ssm_conv_w, ssm_conv_b, ssm_dt_bias, ssm_a_log, ssm_d, ssm_norm_w,
                 lambda_q1, lambda_k1, lambda_q2, lambda_k2, attn_subln_w, rel_bias, w_out,
                 norm_ffn_w, w_up, ffn_conv_w, ffn_conv_b, w_down, final_norm_w, layer):
    l = layer
    wz, wx, wdt, wq, wk, wv = jnp.split(w_in[l], IN_SPLITS, axis=-1)
    wdt = jnp.pad(wdt, ((0, 0), (0, LANES - SSD_HEADS)))
    w_cat = jnp.concatenate([wz, wx, wdt, wk, wv], axis=-1).astype(BF16)
    w_t = (wq * (DA_DK ** -0.5 * LOG2E)).T.astype(BF16)

    def pad_heads(v):
        return jnp.pad(v.astype(F32), (0, LANES - SSD_HEADS)).reshape(1, LANES)

    lam_init = 0.8 - 0.6 * math.exp(-0.3 * l)
    lam = (jnp.exp(jnp.sum(lambda_q1[l].astype(F32) * lambda_k1[l].astype(F32)))
           - jnp.exp(jnp.sum(lambda_q2[l].astype(F32) * lambda_k2[l].astype(F32))) + lam_init)
    far_bias = rel_bias[REL_BUCKETS // 2 - 1].astype(F32)
    return dict(
        norm_mix_w=norm_mix_w[l].reshape(1, D_MODEL), w_cat=w_cat, w_t=w_t,
        cw=ssm_conv_w[l], cbias=ssm_conv_b[l].reshape(1, SSD_CONV_DIM),
        dtb=pad_heads(ssm_dt_bias[l]), alog=pad_heads(ssm_a_log[l]),
        dsk=jnp.repeat(ssm_d[l].astype(F32), SSD_HEADDIM).reshape(1, SSD_WIDTH),
        ssm_nw=ssm_norm_w[l].reshape(1, SSD_WIDTH),
        scal=jnp.concatenate([lam.reshape(1), far_bias * LOG2E]).astype(F32), lam_init=lam_init,
        subw=attn_subln_w[l].reshape(DA_DV, 1), rel_bias=rel_bias,
        wo1=w_out[l][:SSD_WIDTH].astype(BF16), wo2=w_out[l][SSD_WIDTH:].astype(BF16),
        nfw=norm_ffn_w[l].reshape(1, D_MODEL),
        wu=w_up[l].astype(BF16), ffn_cw=ffn_conv_w[l], ffn_cb=ffn_conv_b[l].reshape(1, 2 * D_FF),
        wd=w_down[l].astype(BF16), fw=final_norm_w.reshape(1, D_MODEL),
    )


def _state_to_kernel(h):
    return h.reshape(h.shape[0], SSD_GROUPS, GROUP_W, SSD_STATE)


def _state_from_kernel(h):
    return h.reshape(h.shape[0], SSD_HEADS, SSD_HEADDIM, SSD_STATE)


def _run_group(x, mod, past_k, past_v, ssm_h0, ssm_conv_hist, ffn_conv_hist, p, *, tm, ssd_rows, bq, bk):
    b, t, d = x.shape
    past = 0 if past_k is None else past_k.shape[1]
    chunk = min(CHUNK, t)
    tp = max(t, SUPER)
    if tp != t:
        x = jnp.pad(x, ((0, 0), (0, tp - t), (0, 0)))
        tm = ssd_rows = bq = tp
    mod3 = mod.reshape(b, 6, d)

    assert tm == bq
    zs, xc, dt, k, v, kb, qt, vt, conv_new = _inproj_call(
        x, mod3, p["norm_mix_w"], p["w_cat"], p["w_t"], ssm_conv_hist.astype(F32), p["cw"], p["cbias"],
        tm=tm, real=min(t, tm))

    y_ssd, h_t = _ssd_call(zs, xc, dt, _state_to_kernel(ssm_h0.astype(F32)),
                           p["dtb"], p["alog"], p["dsk"], p["ssm_nw"],
                           chunk=chunk, rows=ssd_rows, real=min(t, ssd_rows))

    if past == 0:
        assert bq == bk and t % bq == 0
        y_att = _attn_call(p["scal"], qt, kb, vt, _bias_tables(p["rel_bias"], bq, bk), p["subw"], bq=bq, bk=bk, noff=0,
                           out_scale=1.0 - p["lam_init"])
    else:
        assert past % bk == 0 and past >= bk and bq == tp <= bk
        bias0 = _bias_rows(p["rel_bias"], past, t, past - bk, bk, past + t)
        bias1 = _bias_rows(p["rel_bias"], past, t, past, bq, past + t)
        y_att = _attn_cached_call(p["scal"], qt, past_k, past_v, kb, vt, bias0, bias1, p["subw"].reshape(1, DA_DV),
                                  tq=t, bk=bk, out_scale=1.0 - p["lam_init"])

    tf = min(t, tm)
    nseq = math.gcd(b, max(1, TILE_ROWS // tf))
    y, ffn_new = _ffn_call(x[:, :t], y_ssd[:, :t], y_att[:, :t], mod3, p["wo1"], p["wo2"], p["nfw"], p["wu"],
                           p["ffn_cw"], p["ffn_cb"], p["wd"], p["fw"], ffn_conv_hist.astype(F32),
                           nseq=nseq, tm=tf)
    k = k.reshape(b, tp, DA_HEADS, 2 * DA_DK)
    v = v.reshape(b, tp, DA_HEADS, DA_DV)
    return (y, k[:, :t], v[:, :t],
            _state_from_kernel(h_t), conv_new, ffn_new)


def kernel(x_prompt, x_sample, c_prompt, c_sample, cache_k, cache_v, state_ssm, state_ssm_conv, state_ffn_conv, w_ada, b_ada, norm_mix_w, w_in, ssm_conv_w, ssm_conv_b, ssm_dt_bias, ssm_a_log, ssm_d, ssm_norm_w, lambda_q1, lambda_k1, lambda_q2, lambda_k2, attn_subln_w, rel_bias, w_out, norm_ffn_w, w_up, ffn_conv_w, ffn_conv_b, w_down, final_norm_w):
    bp, bs = x_prompt.shape[0], x_sample.shape[0]
    dt = x_prompt.dtype
    p = _pack_params(norm_mix_w, w_in, ssm_conv_w, ssm_conv_b, ssm_dt_bias, ssm_a_log, ssm_d, ssm_norm_w,
                     lambda_q1, lambda_k1, lambda_q2, lambda_k2, attn_subln_w, rel_bias, w_out,
                     norm_ffn_w, w_up, ffn_conv_w, ffn_conv_b, w_down, final_norm_w, 0)
    c_all = jnp.concatenate([c_prompt, c_sample], axis=0)
    npad = -c_all.shape[0] % SUBLANES
    c_all = jnp.pad(c_all, ((0, npad), (0, 0)))
    mod = _mod_call(c_all, w_ada[0], b_ada[0].reshape(1, -1))

    zeros = lambda *s: jnp.zeros(s, dt)
    out_p = _run_group(x_prompt, mod[:bp], None, None,
                       zeros(bp, SSD_HEADS, SSD_HEADDIM, SSD_STATE), zeros(bp, SSD_CONV - 1, SSD_CONV_DIM),
                       zeros(bp, FFN_CONV - 1, 2 * D_FF), p, tm=512, ssd_rows=256, bq=512, bk=512)
    out_s = _run_group(x_sample, mod[bp:bp + bs], cache_k[0], cache_v[0], state_ssm[0], state_ssm_conv[0],
                       state_ffn_conv[0], p, tm=SUPER, ssd_rows=SUPER, bq=SUPER, bk=512)
    y_p, k_p, v_p, h_p, c_p, f_p = out_p
    y_s, k_s, v_s, h_s, c_s, f_s = out_s
    return (y_p, y_s, k_p[None], v_p[None], h_p[None], c_p[None], f_p[None],
            k_s[None], v_s[None], h_s[None], c_s[None], f_s[None])
```

```python
import functools
import math

import numpy as np
import jax
import jax.numpy as jnp
from jax import lax
from jax.experimental import pallas as pl
from jax.experimental.pallas import tpu as pltpu

F32 = jnp.float32
BF16 = jnp.bfloat16
HIGHEST = lax.Precision.HIGHEST

D_MODEL = 1024
CHUNK = 64
SSD_WIDTH = 512
SSD_HEADDIM = 64
SSD_HEADS = 8
SSD_GROUPS = 2
SSD_HPG = 4
SSD_STATE = 128
SSD_CONV = 4
SSD_CONV_DIM = SSD_WIDTH + 2 * SSD_GROUPS * SSD_STATE
GROUP_W = SSD_HPG * SSD_HEADDIM
DA_WIDTH = 512
DA_DK = 64
DA_DV = 128
DA_HEADS = 4
REL_BUCKETS = 32
REL_MAX_DIST = 128
D_FF = 2816
FFN_CONV = 3
EPS = 1e-6
IN_SPLITS = (512, 1536, 1544, 2056, 2568)
LANES = 128
SUBLANES = 8
TILE_ROWS = 512
SUPER = 128
SSD_NSEQ = 4
ATT_KT = 64
FFN_CN = 256
FFN_NC = D_FF // FFN_CN
NEG = -1e30
VMEM_LIMIT = 56 * 1024 * 1024

PZ, PX, PDT, PK, PV, PEND = 0, 512, 1536, 1664, 2176, 2688
LOG2E = math.log2(math.e)


def _silu(x):
    return x / (1.0 + jnp.exp(-x))


def _softplus(x):
    return jnp.maximum(x, 0.0) + jnp.log1p(jnp.exp(-jnp.abs(x)))


def _split3(x):
    hi = x.astype(BF16)
    r1 = x - hi.astype(F32)
    mid = r1.astype(BF16)
    lo = (r1 - mid.astype(F32)).astype(BF16)
    return hi, mid, lo


def _rms(x, w):
    return x * lax.rsqrt(jnp.mean(x * x, axis=-1, keepdims=True) + EPS) * w


def _const_spec(shape):
    nd = len(shape)
    return pl.BlockSpec(shape, lambda *_: (0,) * nd)


def _mod_kernel(c_ref, w_ref, b_ref, o_ref):
    a = _silu(c_ref[...]).astype(BF16)
    o_ref[...] = jnp.dot(a, w_ref[...].astype(BF16), preferred_element_type=F32) + b_ref[...]


def _mod_call(c, w_ada, b_ada):
    n, d = c.shape
    nout = w_ada.shape[1]
    tn = 1024
    return pl.pallas_call(
        _mod_kernel,
        grid=(nout // tn,),
        in_specs=[pl.BlockSpec((n, d), lambda j: (0, 0)),
                  pl.BlockSpec((d, tn), lambda j: (0, j)),
                  pl.BlockSpec((1, tn), lambda j: (0, j))],
        out_specs=pl.BlockSpec((n, tn), lambda j: (0, j)),
        out_shape=jax.ShapeDtypeStruct((n, nout), F32),
        name="mod",
    )(c, w_ada, b_ada)


def _inproj_kernel(x_ref, mod_ref, nw_ref, w_ref, wt_ref, hist_ref, cw_ref, cbias_ref,
                   zs_ref, xc_ref, dt_ref, k_ref, v_ref, kb_ref, qt_ref, vt_ref, cout_ref, cbuf, hb_scr, zbuf,
                   *, tm, real):
    t = pl.program_id(1)
    nconv = SSD_CONV - 1

    @pl.when(t == 0)
    def _init():
        cbuf[0:SUBLANES, :] = jnp.zeros((SUBLANES, SSD_CONV_DIM), F32)
        cbuf[SUBLANES - nconv:SUBLANES, :] = hist_ref[...]

    h = _rms(x_ref[...], nw_ref[...]) * (1.0 + mod_ref[1:2, :]) + mod_ref[0:1, :]
    hb_scr[...] = h.astype(BF16)

    def proj(a, b):
        return jnp.dot(hb_scr[...], w_ref[:, a:b], preferred_element_type=F32)

    def proj_t(a, b):
        return lax.dot_general(wt_ref[a:b, :], hb_scr[...], (((1,), (1,)), ((), ())), preferred_element_type=F32)

    cbuf[SUBLANES:SUBLANES + tm, :] = proj(PX, PDT)
    zbuf[...] = proj(PZ, PX)
    dt_ref[...] = proj(PDT, PK)
    k = proj(PK, PV)
    v = proj(PV, PEND)
    for hd in range(DA_HEADS):
        dst = pl.ds(hd, tm, stride=DA_HEADS)
        k_ref[dst, :] = k[:, hd * DA_DV:(hd + 1) * DA_DV]
        v_ref[dst, :] = v[:, hd * DA_DV:(hd + 1) * DA_DV]
    kb_ref[...] = k.astype(BF16)
    qt_ref[...] = proj_t(0, DA_WIDTH).astype(BF16)
    for hd in range(DA_HEADS):
        vt_ref[hd * DA_DV:(hd + 1) * DA_DV, :] = v[:, hd * DA_DV:(hd + 1) * DA_DV].T.astype(BF16)
    conv = cbias_ref[...]
    for j in range(SSD_CONV):
        off = SUBLANES - nconv + j
        conv = conv + cw_ref[j:j + 1, :] * cbuf[off:off + tm, :]
    tail = cbuf[SUBLANES - nconv + real:SUBLANES + real, :]
    cout_ref[...] = tail
    cbuf[SUBLANES - nconv:SUBLANES, :] = tail
    xc_ref[...] = _silu(conv).astype(BF16)
    zs_ref[...] = _silu(zbuf[...]).astype(BF16)


def _inproj_call(x, mod3, norm_w, w_cat, w_t, hist, cw, cbias, *, tm, real):
    b, t, d = x.shape
    nt = t // tm

    def row(width):
        return pl.BlockSpec((None, tm, width), lambda i, j: (i, j, 0))

    def out(width, dtype):
        return jax.ShapeDtypeStruct((b, t, width), dtype)

    tspec = pl.BlockSpec((None, None, DA_WIDTH, tm), lambda i, j: (i, j, 0, 0))
    tshape = jax.ShapeDtypeStruct((b, nt, DA_WIDTH, tm), BF16)
    hist_spec = pl.BlockSpec((None, SSD_CONV - 1, SSD_CONV_DIM), lambda i, j: (i, 0, 0))
    hspec = pl.BlockSpec((None, tm * DA_HEADS, DA_DV), lambda i, j: (i, j, 0))
    hshape = jax.ShapeDtypeStruct((b, t * DA_HEADS, DA_DV), F32)
    return pl.pallas_call(
        functools.partial(_inproj_kernel, tm=tm, real=real),
        grid=(b, nt),
        in_specs=[row(d),
                  pl.BlockSpec((None, 6, d), lambda i, j: (i, 0, 0)),
                  _const_spec((1, d)),
                  _const_spec((d, PEND)),
                  _const_spec((DA_WIDTH, d)),
                  hist_spec, _const_spec((SSD_CONV, SSD_CONV_DIM)), _const_spec((1, SSD_CONV_DIM))],
        out_specs=[row(512), row(1024), row(LANES), hspec, hspec, row(512), tspec, tspec, hist_spec],
        out_shape=[out(512, BF16), out(1024, BF16), out(LANES, F32),
                   hshape, hshape, out(512, BF16), tshape, tshape,
                   jax.ShapeDtypeStruct((b, SSD_CONV - 1, SSD_CONV_DIM), F32)],
        scratch_shapes=[pltpu.VMEM((tm + SUBLANES, SSD_CONV_DIM), F32), pltpu.VMEM((tm, d), BF16),
                        pltpu.VMEM((tm, SSD_WIDTH), F32)],
        compiler_params=pltpu.CompilerParams(dimension_semantics=("parallel", "arbitrary"),
                                             vmem_limit_bytes=VMEM_LIMIT),
        name="inproj",
    )(x, mod3, norm_w, w_cat, w_t, hist, cw, cbias)


def _ssd_kernel(zs_ref, xc_ref, dt_ref, h0_ref, dtb_ref, alog_ref, dsk_ref, nw_ref, tri_ref, e_ref,
                y_ref, hout_ref, h_scr, ybuf, *, nseq, chunk, rows, real):
    t = pl.program_id(1)
    seqs = range(nseq)

    @pl.when(t == 0)
    def _init():
        for i in seqs:
            for g in range(SSD_GROUPS):
                h_scr[i, g] = h0_ref[i, g].T

    li = lax.broadcasted_iota(jnp.int32, (SUPER, SUPER), 0)
    si = lax.broadcasted_iota(jnp.int32, (SUPER, SUPER), 1)
    cshift = chunk.bit_length() - 1
    mask2 = ((li >> cshift) == (si >> cshift)) & (si <= li)
    lane_g = lax.broadcasted_iota(jnp.int32, (SUPER, GROUP_W), 1) >> (SSD_HEADDIM.bit_length() - 1)

    pre = []
    for i in seqs:
        xs = xc_ref[i, :, 0:SSD_WIDTH].astype(F32)
        dtv = _softplus(dt_ref[i] + dtb_ref[...])
        da = dtv * (-jnp.exp(alog_ref[...]))
        acs = jnp.dot(tri_ref[...], jnp.concatenate(_split3(da), axis=0), preferred_element_type=F32)
        dt_x = jnp.dot(jnp.concatenate(_split3(dtv), axis=1), e_ref[...], preferred_element_type=F32)
        acs_x = jnp.dot(jnp.concatenate(_split3(acs), axis=1), e_ref[...], preferred_element_type=F32)
        pre.append((xs, acs, acs_x, jnp.exp(acs_x), xs * dt_x))

    for sb in range(rows // SUPER):
        o = sb * SUPER
        nreal = (min(real, o + SUPER) - o) // chunk
        acs2 = [pre[i][1][o:o + SUPER, :] for i in seqs]
        acs_t = [a.T for a in acs2]
        chains = [(g, i) for g in range(SSD_GROUPS) for i in seqs]

        def gsl(g):
            return slice(g * GROUP_W, (g + 1) * GROUP_W)

        cmb, cb2, bm_t = {}, {}, {}
        for g, i in chains:
            bcol = SSD_WIDTH + g * SSD_STATE
            ccol = SSD_WIDTH + (SSD_GROUPS + g) * SSD_STATE
            bmb = xc_ref[i, o:o + SUPER, bcol:bcol + SSD_STATE]
            cmb[g, i] = xc_ref[i, o:o + SUPER, ccol:ccol + SSD_STATE]
            cb2[g, i] = lax.dot_general(cmb[g, i], bmb, (((1,), (1,)), ((), ())), preferred_element_type=F32)
            bm_t[g, i] = bmb.astype(F32).T.astype(BF16)
        for g, i in chains:
            ms = []
            for rr in range(SSD_HPG):
                r = g * SSD_HPG + rr
                seg = acs2[i][:, r:r + 1] - acs_t[i][r:r + 1, :]
                dec = jnp.where(mask2, jnp.exp(jnp.where(mask2, seg, 0.0)), 0.0)
                ms.append((cb2[g, i] * dec).astype(BF16))
            full = jnp.dot(jnp.concatenate(ms, axis=0), pre[i][4][o:o + SUPER, gsl(g)].astype(BF16),
                           preferred_element_type=F32)
            ydiag = full[0:SUPER]
            for rr in range(1, SSD_HPG):
                ydiag = jnp.where(lane_g == rr, full[rr * SUPER:(rr + 1) * SUPER], ydiag)
            ybuf[i, o:o + SUPER, gsl(g)] = ydiag
        for j in range(nreal):
            a0, a1 = o + j * chunk, o + (j + 1) * chunk
            h_t = {c: h_scr[c[1], c[0]] for c in chains}
            yoff = {(g, i): jnp.dot(cmb[g, i][j * chunk:(j + 1) * chunk, :], h_t[g, i].astype(BF16),
                                    preferred_element_type=F32) for g, i in chains}
            st = {}
            for g, i in chains:
                _, _, acs_x, eacs_x, xd = pre[i]
                ybuf[i, a0:a1, gsl(g)] = ybuf[i, a0:a1, gsl(g)] + yoff[g, i] * eacs_x[a0:a1, gsl(g)]
                dte = jnp.exp(acs_x[a1 - 1:a1, gsl(g)] - acs_x[a0:a1, gsl(g)])
                xw = (xd[a0:a1, gsl(g)] * dte).astype(BF16)
                pieces = []
                if j > 0:
                    pieces.append(jnp.zeros((j * chunk, GROUP_W), BF16))
                pieces.append(xw)
                if (j + 1) * chunk < SUPER:
                    pieces.append(jnp.zeros((SUPER - (j + 1) * chunk, GROUP_W), BF16))
                xw2 = jnp.concatenate(pieces, axis=0) if len(pieces) > 1 else xw
                st[g, i] = jnp.dot(bm_t[g, i], xw2, preferred_element_type=F32)
            for g, i in chains:
                h_scr[i, g] = h_t[g, i] * pre[i][3][a1 - 1:a1, gsl(g)] + st[g, i]

    for i in seqs:
        y = (ybuf[i] + dsk_ref[...] * pre[i][0]) * zs_ref[i].astype(F32)
        for g in range(SSD_GROUPS):
            gs = slice(g * GROUP_W, (g + 1) * GROUP_W)
            y_ref[i, :, gs] = _rms(y[:, gs], nw_ref[:, gs]).astype(BF16)

    @pl.when(t == pl.num_programs(1) - 1)
    def _fin():
        for i in seqs:
            for g in range(SSD_GROUPS):
                hout_ref[i, g] = h_scr[i, g].T


def _ssd_call(zs, xc, dt, h0_t, dtb, alog, dsk, nw, *, chunk, rows, real):
    b, t, _ = zs.shape
    ii = np.arange(rows)
    tri = ((ii[:, None] // chunk == ii[None, :] // chunk) & (ii[None, :] <= ii[:, None])).astype(np.float32)
    e = np.zeros((LANES, SSD_WIDTH), np.float32)
    for r in range(SSD_HEADS):
        e[r, r * SSD_HEADDIM:(r + 1) * SSD_HEADDIM] = 1.0

    nseq = math.gcd(b, SSD_NSEQ)

    def row(width):
        return pl.BlockSpec((nseq, rows, width), lambda i, j: (i, j, 0))

    state_spec = pl.BlockSpec((nseq, SSD_GROUPS, GROUP_W, SSD_STATE), lambda i, j: (i, 0, 0, 0))
    kern = functools.partial(_ssd_kernel, nseq=nseq, chunk=chunk, rows=rows, real=real)
    return pl.pallas_call(
        kern,
        grid=(b // nseq, t // rows),
        in_specs=[row(SSD_WIDTH), row(SSD_CONV_DIM), row(LANES), state_spec,
                  _const_spec((1, LANES)), _const_spec((1, LANES)),
                  _const_spec((1, SSD_WIDTH)), _const_spec((1, SSD_WIDTH)),
                  _const_spec((rows, 3 * rows)), _const_spec((3 * LANES, SSD_WIDTH))],
        out_specs=[row(SSD_WIDTH), state_spec],
        out_shape=[jax.ShapeDtypeStruct((b, t, SSD_WIDTH), BF16),
                   jax.ShapeDtypeStruct((b, SSD_GROUPS, GROUP_W, SSD_STATE), F32)],
        scratch_shapes=[pltpu.VMEM((nseq, SSD_GROUPS, SSD_STATE, GROUP_W), F32),
                        pltpu.VMEM((nseq, rows, SSD_WIDTH), F32)],
        compiler_params=pltpu.CompilerParams(dimension_semantics=("parallel", "arbitrary"),
                                             vmem_limit_bytes=VMEM_LIMIT),
        name="ssd",
    )(zs, xc, dt, h0_t, dtb, alog, dsk, nw,
      jnp.asarray(np.tile(tri, (1, 3)), BF16), jnp.asarray(np.tile(e, (3, 1)), BF16))


def _attn_kernel(scal_ref, qt_ref, k_ref, vt_ref, btab_ref, subw_ref, o_ref,
                 m_scr, l_scr, acc_scr, sa_scr, sb_scr, sc_scr, pa_scr, pb_scr, alpha_scr, qz_scr, bias_scr,
                 *, bq, bk, noff, out_scale):
    h = pl.program_id(0)
    qi = pl.program_id(2)
    kn0 = qi + (noff - 1)
    lam = scal_ref[0]
    cfar = scal_ref[1 + h]

    @pl.when((pl.program_id(1) == 0) & (qi == 0))
    def _build_bias_tiles():
        width = bq + bk
        kj = lax.broadcasted_iota(jnp.int32, (bk, bq), 0)
        qj = lax.broadcasted_iota(jnp.int32, (bk, bq), 1)
        cshift = CHUNK.bit_length() - 1
        for d in range(2):
            skew = pltpu.roll(jnp.broadcast_to(btab_ref[d], (bk, width)), 0, 1, stride=1, stride_axis=0)
            tile = skew[:, bk:width]
            if d == 1:
                tile = jnp.where((kj >> cshift) <= (qj >> cshift), tile, NEG)
            bias_scr[d] = tile

    zero = jnp.zeros((DA_DK, bq), BF16)
    qz_scr[0, 0:DA_DK, :] = qt_ref[0:DA_DK, :]
    qz_scr[0, DA_DK:DA_DV, :] = zero
    qz_scr[1, 0:DA_DK, :] = zero
    qz_scr[1, DA_DK:DA_DV, :] = qt_ref[DA_DK:DA_DV, :]

    m_scr[...] = jnp.full(m_scr.shape, NEG, F32)
    l_scr[...] = jnp.zeros(l_scr.shape, F32)
    acc_scr[...] = jnp.zeros(acc_scr.shape, F32)

    nsub = bk // ATT_KT

    def fold(x):
        return x.reshape(ATT_KT // SUBLANES, SUBLANES, bq)

    def scores(s_buf, first, count):
        for mm in range(2):
            for e in range(count):
                start = pl.multiple_of((first + e) * bk, bk)
                s_buf[mm, e] = jnp.dot(k_ref[pl.ds(start, bk), :], qz_scr[mm], preferred_element_type=F32)

    def softmax(s_buf, p_buf, entries):
        alphas = []
        for mm in range(2):
            cand = None
            for e, (near, shift) in enumerate(entries):
                mx = None
                for t in range(nsub):
                    rows = slice(t * ATT_KT, (t + 1) * ATT_KT)
                    s = s_buf[mm, e, rows, :]
                    if near is not None:
                        s = s + bias_scr[near, rows, :]
                    pm = jnp.max(fold(s), axis=0)
                    mx = pm if mx is None else jnp.maximum(mx, pm)
                mx = jnp.max(mx, axis=0, keepdims=True) + shift
                cand = mx if cand is None else jnp.maximum(cand, mx)
            m_old = m_scr[mm]
            m_new = jnp.maximum(m_old, cand)
            ls = None
            for e, (near, shift) in enumerate(entries):
                off = m_new - shift
                for t in range(nsub):
                    rows = slice(t * ATT_KT, (t + 1) * ATT_KT)
                    s = s_buf[mm, e, rows, :]
                    if near is not None:
                        s = s + bias_scr[near, rows, :]
                    p = jnp.exp2(s - off)
                    p_buf[mm, e * bk + t * ATT_KT:e * bk + (t + 1) * ATT_KT, :] = p.astype(BF16)
                    ps = jnp.sum(fold(p), axis=0)
                    ls = ps if ls is None else ls + ps
            alpha = jnp.exp2(m_old - m_new)
            l_scr[mm] = alpha * l_scr[mm] + jnp.sum(ls, axis=0, keepdims=True)
            m_scr[mm] = m_new
            alphas.append(alpha)
        return alphas

    def pv(p_buf, first, n):
        vts = [vt_ref[first + e] for e in range(n)]
        vt = jnp.concatenate(vts, axis=1) if n > 1 else vts[0]
        return [jnp.dot(vt, p_buf[mm, 0:n * bk, :], preferred_element_type=F32) for mm in range(2)]

    def accumulate(alphas, pvs):
        for mm in range(2):
            acc_scr[mm] = alphas[mm] * acc_scr[mm] + pvs[mm]

    def softmax_pv(s_buf, first, entries):
        alphas = softmax(s_buf, pa_scr, entries)
        accumulate(alphas, pv(pa_scr, first, len(entries)))

    nfar = jnp.maximum(kn0, 0)
    odd = nfar % 2
    far = (None, cfar)

    near_pair = [(0, 0.0), (1, 0.0)]

    @pl.when(kn0 < 0)
    def _only_first():
        scores(sc_scr, 0, 1)
        softmax_pv(sc_scr, 0, [(1, 0.0)])

    @pl.when(kn0 >= 0)
    def _groups():
        @pl.when(odd == 1)
        def _single():
            scores(sc_scr, 0, 1)
            scores(sa_scr, 1, 2)
            softmax_pv(sc_scr, 0, [far])

        @pl.when(odd == 0)
        def _first_pair():
            scores(sa_scr, 0, 2)

        def far_pair(s_cur, s_next, cur):
            scores(s_next, cur + 2, 2)
            softmax_pv(s_cur, cur, [far, far])

        def far_step(s_cur, p_cur, s_next, p_prev, cur):
            scores(s_next, cur + 2, 2)
            pending = None if p_prev is None else pv(p_prev, cur - 2, 2)
            alphas = softmax(s_cur, p_cur, [far, far])
            if pending is not None:
                accumulate([alpha_scr[0], alpha_scr[1]], pending)
            for mm in range(2):
                alpha_scr[mm] = alphas[mm]

        npairs = nfar // 2
        niter = npairs // 2

        @pl.when(niter >= 1)
        def _far_loop():
            far_step(sa_scr, pa_scr, sb_scr, None, odd)
            far_step(sb_scr, pb_scr, sa_scr, pa_scr, odd + 2)

            def far_body(j, carry):
                cur = odd + 4 * j
                far_step(sa_scr, pa_scr, sb_scr, pb_scr, cur)
                far_step(sb_scr, pb_scr, sa_scr, pa_scr, cur + 2)
                return carry

            lax.fori_loop(1, niter, far_body, 0)
            accumulate([alpha_scr[0], alpha_scr[1]], pv(pb_scr, odd + 4 * niter - 2, 2))

        @pl.when(npairs % 2 == 1)
        def _tail_b():
            far_pair(sa_scr, sb_scr, kn0 - 2)
            softmax_pv(sb_scr, kn0, near_pair)

        @pl.when(npairs % 2 == 0)
        def _tail_a():
            softmax_pv(sa_scr, kn0, near_pair)

    o = acc_scr[0] * (1.0 / l_scr[0]) - acc_scr[1] * (lam / l_scr[1])
    o = o * lax.rsqrt(jnp.mean(o * o, axis=0, keepdims=True) + EPS) * (subw_ref[...] * out_scale)
    o_ref[...] = o.T.astype(BF16)


def _attn_call(scal, qt, kb, vt, btab, subw, *, bq, bk, noff, out_scale):
    b, nq = qt.shape[:2]
    tk = kb.shape[1]
    nkb = vt.shape[1]
    kern = functools.partial(_attn_kernel, bq=bq, bk=bk, noff=noff, out_scale=out_scale)
    return pl.pallas_call(
        kern,
        grid=(DA_HEADS, b, nq),
        in_specs=[pl.BlockSpec(memory_space=pltpu.SMEM),
                  pl.BlockSpec((None, None, DA_DV, bq), lambda h, i, j: (i, j, h, 0)),
                  pl.BlockSpec((None, tk, DA_DV), lambda h, i, j: (i, 0, h)),
                  pl.BlockSpec((None, nkb, DA_DV, bk), lambda h, i, j: (i, 0, h, 0)),
                  pl.BlockSpec((None, 2, 1, bq + bk), lambda h, i, j: (h, 0, 0, 0)),
                  pl.BlockSpec((DA_DV, 1), lambda h, i, j: (0, 0))],
        out_specs=pl.BlockSpec((None, bq, DA_DV), lambda h, i, j: (i, j, h)),
        out_shape=jax.ShapeDtypeStruct((b, nq * bq, DA_WIDTH), BF16),
        scratch_shapes=[pltpu.VMEM((2, 1, bq), F32), pltpu.VMEM((2, 1, bq), F32),
                        pltpu.VMEM((2, DA_DV, bq), F32),
                        pltpu.VMEM((2, 2, bk, bq), F32), pltpu.VMEM((2, 2, bk, bq), F32),
                        pltpu.VMEM((2, 1, bk, bq), F32),
                        pltpu.VMEM((2, 2 * bk, bq), BF16), pltpu.VMEM((2, 2 * bk, bq), BF16),
                        pltpu.VMEM((2, 1, bq), F32),
                        pltpu.VMEM((2, DA_DV, bq), BF16),
                        pltpu.VMEM((2, bk, bq), F32)],
        compiler_params=pltpu.CompilerParams(dimension_semantics=("arbitrary", "arbitrary", "arbitrary"),
                                             vmem_limit_bytes=VMEM_LIMIT),
        name="attn",
    )(scal, qt, kb, vt, btab, subw)


def _attn_cached_kernel(scal_ref, qt_ref, kc_ref, vc_ref, kn_ref, vtn_ref, bias0_ref, bias1_ref, subw_ref, o_ref,
                        *, bq, tq, bk, out_scale):
    lam = scal_ref[0]
    past = kc_ref.shape[0] // DA_HEADS
    lane = lax.broadcasted_iota(jnp.int32, (tq, DA_DV), 1)
    nt = (((1,), (1,)), ((), ()))

    for h in range(DA_HEADS):
        hs = slice(h * DA_DV, (h + 1) * DA_DV)
        cfar = scal_ref[1 + h]
        qn = qt_ref[hs, :].astype(F32).T[0:tq, :]
        q2 = jnp.concatenate([jnp.where(lane < DA_DK, qn, 0.0), jnp.where(lane >= DA_DK, qn, 0.0)],
                             axis=0).astype(BF16)
        head_rows = pl.ds(h, past, stride=DA_HEADS)
        s_c = lax.dot_general(q2, kc_ref[head_rows, :].astype(BF16), nt, preferred_element_type=F32)
        s_n = lax.dot_general(q2, kn_ref[:, hs], nt, preferred_element_type=F32)
        b0 = bias0_ref[h]
        b1 = bias1_ref[h]
        s = jnp.concatenate([s_c[:, 0:past - bk] + cfar,
                             s_c[:, past - bk:past] + jnp.concatenate([b0, b0], axis=0),
                             s_n + jnp.concatenate([b1, b1], axis=0)], axis=1)
        p = jnp.exp2(s - jnp.max(s, axis=1, keepdims=True))
        inv = 1.0 / jnp.sum(p, axis=1, keepdims=True)
        pb = p.astype(BF16)
        acc = (jnp.dot(pb[:, 0:past], vc_ref[head_rows, :].astype(BF16), preferred_element_type=F32)
               + jnp.dot(pb[:, past:past + bq], vtn_ref[hs, :].astype(F32).T.astype(BF16),
                         preferred_element_type=F32))
        o = acc[0:tq, :] * inv[0:tq] - acc[tq:2 * tq, :] * (lam * inv[tq:2 * tq])
        o_ref[0:tq, hs] = (_rms(o, subw_ref[...]) * out_scale).astype(BF16)
        o_ref[tq:bq, hs] = jnp.zeros((bq - tq, DA_DV), BF16)


def _attn_cached_call(scal, qt, cache_k, cache_v, kb, vt, bias0, bias1, subw, *, tq, bk, out_scale):
    b, past = cache_k.shape[:2]
    bq = qt.shape[-1]
    rows = past * DA_HEADS
    kern = functools.partial(_attn_cached_kernel, bq=bq, tq=tq, bk=bk, out_scale=out_scale)
    cache_spec = pl.BlockSpec((None, rows, DA_DV), lambda i: (i, 0, 0))
    return pl.pallas_call(
        kern,
        grid=(b,),
        in_specs=[pl.BlockSpec(memory_space=pltpu.SMEM),
                  pl.BlockSpec((None, None, DA_WIDTH, bq), lambda i: (i, 0, 0, 0)),
                  cache_spec, cache_spec,
                  pl.BlockSpec((None, bq, DA_WIDTH), lambda i: (i, 0, 0)),
                  pl.BlockSpec((None, None, DA_WIDTH, bq), lambda i: (i, 0, 0, 0)),
                  _const_spec((DA_HEADS, tq, bk)), _const_spec((DA_HEADS, tq, bq)),
                  _const_spec((1, DA_DV))],
        out_specs=pl.BlockSpec((None, bq, DA_WIDTH), lambda i: (i, 0, 0)),
        out_shape=jax.ShapeDtypeStruct((b, bq, DA_WIDTH), BF16),
        compiler_params=pltpu.CompilerParams(dimension_semantics=("parallel",), vmem_limit_bytes=VMEM_LIMIT),
        name="attn_cached",
    )(scal, qt, cache_k.reshape(b, rows, DA_DV), cache_v.reshape(b, rows, DA_DV), kb, vt, bias0, bias1, subw)


def _rel_bucket(rel):
    nb = REL_BUCKETS // 2
    max_exact = nb // 2
    n = jnp.abs(rel)
    nf = jnp.maximum(n, 1).astype(jnp.float32)
    large = max_exact + (jnp.log(nf / max_exact) / math.log(REL_MAX_DIST / max_exact)
                         * (nb - max_exact)).astype(jnp.int32)
    large = jnp.minimum(large, nb - 1)
    return jnp.where(rel > 0, nb, 0) + jnp.where(n < max_exact, n, large)


def _bias_tables(rel_bias, bq, bk):
    tabs = []
    for d in range(2):
        offs = (d - 1) * bk + bk - np.arange(bq + bk)
        tabs.append(rel_bias[_rel_bucket(jnp.asarray(offs, jnp.int32))].astype(F32).T * LOG2E)
    return jnp.stack(tabs, axis=1)[:, :, None, :]


def _bias_rows(rel_bias, qpos0, tq, kpos0, nkeys, tk_real):
    offs = (kpos0 - qpos0) + np.arange(-(tq - 1), nkeys)
    table = rel_bias[_rel_bucket(jnp.asarray(offs, jnp.int32))].astype(F32).T * LOG2E
    rows = jnp.stack([table[:, tq - 1 - i:tq - 1 - i + nkeys] for i in range(tq)], axis=1)
    qpos = qpos0 + np.arange(tq)
    kpos = kpos0 + np.arange(nkeys)
    vis = (kpos[None, :] // CHUNK <= qpos[:, None] // CHUNK) & (kpos[None, :] < tk_real)
    return jnp.where(jnp.asarray(vis)[None], rows, NEG)


def _ffn_kernel(x_ref, ys_ref, ya_ref, mod_ref, wo1_ref, wo2_ref, nfw_ref, wu_ref, cw_ref, cb_ref, wd_ref, fw_ref,
                hist_ref, y_ref, tail_ref, tail_scr, buf_a, buf_b, x1_scr, h2_scr, act_scr, *, nseq, tm):
    t = pl.program_id(1)
    nh = FFN_CONV - 1
    lo = SUBLANES - nh
    seqs = range(nseq)

    def rows(i):
        return slice(i * tm, (i + 1) * tm)

    def stacked(ref):
        return jnp.concatenate([ref[i] for i in seqs], axis=0) if nseq > 1 else ref[0]

    @pl.when(t == 0)
    def _init():
        for i in seqs:
            tail_scr[i, lo:SUBLANES, :] = hist_ref[i]

    mix = (jnp.dot(stacked(ys_ref), wo1_ref[...], preferred_element_type=F32)
           + jnp.dot(stacked(ya_ref), wo2_ref[...], preferred_element_type=F32))
    for i in seqs:
        x1 = x_ref[i] + mod_ref[i, 2:3, :] * mix[rows(i)]
        x1_scr[rows(i), :] = x1
        h2 = _rms(x1, nfw_ref[...]) * (1.0 + mod_ref[i, 4:5, :]) + mod_ref[i, 3:4, :]
        h2_scr[rows(i), :] = h2.astype(BF16)

    def cols(j, half):
        return slice(half * D_FF + j * FFN_CN, half * D_FF + (j + 1) * FFN_CN)

    def up(j, buf):
        for half in range(2):
            u = jnp.dot(h2_scr[...], wu_ref[:, cols(j, half)], preferred_element_type=F32)
            for i in seqs:
                buf[half, i, SUBLANES:SUBLANES + tm, :] = u[rows(i)]

    def conv(j, buf, half, i):
        cs = cols(j, half)
        buf[half, i, lo:SUBLANES, :] = tail_scr[i, lo:SUBLANES, cs]
        c = cb_ref[:, cs]
        for k in range(FFN_CONV):
            c = c + cw_ref[k:k + 1, cs] * buf[half, i, lo + k:lo + k + tm, :]
        tail_scr[i, lo:SUBLANES, cs] = buf[half, i, lo + tm:SUBLANES + tm, :]
        return c

    bufs = (buf_a, buf_b)

    @pl.when(t >= 0)
    def _chunks():
        up(0, bufs[0])
        for j in range(FFN_NC):
            if j + 1 < FFN_NC:
                up(j + 1, bufs[(j + 1) % 2])
            for i in seqs:
                cv = conv(j, bufs[j % 2], 0, i)
                cg = conv(j, bufs[j % 2], 1, i)
                act_scr[rows(i), j * FFN_CN:(j + 1) * FFN_CN] = (_silu(cg) * cv).astype(BF16)

    f = jnp.dot(act_scr[...], wd_ref[...], preferred_element_type=F32)
    for i in seqs:
        x2 = x1_scr[rows(i), :] + mod_ref[i, 5:6, :] * f[rows(i)]
        y_ref[i] = _rms(x2, fw_ref[...])
        tail_ref[i] = tail_scr[i, lo:SUBLANES, :]


def _ffn_call(x, ys, ya, mod3, wo1, wo2, nfw, wu, cw, cb, wd, fw, hist, *, nseq, tm):
    b, t, d = x.shape
    nh = FFN_CONV - 1
    m = nseq * tm

    def row(width):
        return pl.BlockSpec((nseq, tm, width), lambda i, j: (i, j, 0))

    hist_spec = pl.BlockSpec((nseq, nh, 2 * D_FF), lambda i, j: (i, 0, 0))
    kern = functools.partial(_ffn_kernel, nseq=nseq, tm=tm)
    return pl.pallas_call(
        kern,
        grid=(b // nseq, t // tm),
        in_specs=[row(d), row(SSD_WIDTH), row(DA_WIDTH),
                  pl.BlockSpec((nseq, 6, d), lambda i, j: (i, 0, 0)),
                  _const_spec((SSD_WIDTH, d)), _const_spec((DA_WIDTH, d)), _const_spec((1, d)),
                  _const_spec((d, 2 * D_FF)), _const_spec((FFN_CONV, 2 * D_FF)), _const_spec((1, 2 * D_FF)),
                  _const_spec((D_FF, d)), _const_spec((1, d)),
                  hist_spec],
        out_specs=[row(d), hist_spec],
        out_shape=[jax.ShapeDtypeStruct((b, t, d), F32),
                   jax.ShapeDtypeStruct((b, nh, 2 * D_FF), F32)],
        scratch_shapes=[pltpu.VMEM((nseq, SUBLANES, 2 * D_FF), F32),
                        pltpu.VMEM((2, nseq, tm + SUBLANES, FFN_CN), F32),
                        pltpu.VMEM((2, nseq, tm + SUBLANES, FFN_CN), F32),
                        pltpu.VMEM((m, d), F32), pltpu.VMEM((m, d), BF16), pltpu.VMEM((m, D_FF), BF16)],
        compiler_params=pltpu.CompilerParams(dimension_semantics=("parallel", "arbitrary"),
                                             vmem_limit_bytes=VMEM_LIMIT),
        name="ffn",
    )(x, ys, ya, mod3, wo1, wo2, nfw, wu, cw, cb, wd, fw, hist)


def _pack_params(norm_mix_w, w_in, ssm_conv_w, ssm_conv_b, ssm_dt_bias, ssm_a_log, ssm_d, ssm_norm_w,
                 lambda_q1, lambda_k1, lambda_q2, lambda_k2, attn_subln_w, rel_bias, w_out,
                 norm_ffn_w, w_up, ffn_conv_w, ffn_conv_b, w_down, final_norm_w, layer):
    l = layer
    wz, wx, wdt, wq, wk, wv = jnp.split(w_in[l], IN_SPLITS, axis=-1)
    wdt = jnp.pad(wdt, ((0, 0), (0, LANES - SSD_HEADS)))
    w_cat = jnp.concatenate([wz, wx, wdt, wk, wv], axis=-1).astype(BF16)
    w_t = (wq * (DA_DK ** -0.5 * LOG2E)).T.astype(BF16)

    def pad_heads(v):
        return jnp.pad(v.astype(F32), (0, LANES - SSD_HEADS)).reshape(1, LANES)

    lam_init = 0.8 - 0.6 * math.exp(-0.3 * l)
    lam = (jnp.exp(jnp.sum(lambda_q1[l].astype(F32) * lambda_k1[l].astype(F32)))
           - jnp.exp(jnp.sum(lambda_q2[l].astype(F32) * lambda_k2[l].astype(F32))) + lam_init)
    far_bias = rel_bias[REL_BUCKETS // 2 - 1].astype(F32)
    return dict(
        norm_mix_w=norm_mix_w[l].reshape(1, D_MODEL), w_cat=w_cat, w_t=w_t,
        cw=ssm_conv_w[l], cbias=ssm_conv_b[l].reshape(1, SSD_CONV_DIM),
        dtb=pad_heads(ssm_dt_bias[l]), alog=pad_heads(ssm_a_log[l]),
        dsk=jnp.repeat(ssm_d[l].astype(F32), SSD_HEADDIM).reshape(1, SSD_WIDTH),
        ssm_nw=ssm_norm_w[l].reshape(1, SSD_WIDTH),
        scal=jnp.concatenate([lam.reshape(1), far_bias * LOG2E]).astype(F32), lam_init=lam_init,
        subw=attn_subln_w[l].reshape(DA_DV, 1), rel_bias=rel_bias,
        wo1=w_out[l][:SSD_WIDTH].astype(BF16), wo2=w_out[l][SSD_WIDTH:].astype(BF16),
        nfw=norm_ffn_w[l].reshape(1, D_MODEL),
        wu=w_up[l].astype(BF16), ffn_cw=ffn_conv_w[l], ffn_cb=ffn_conv_b[l].reshape(1, 2 * D_FF),
        wd=w_down[l].astype(BF16), fw=final_norm_w.reshape(1, D_MODEL),
    )


def _state_to_kernel(h):
    return h.reshape(h.shape[0], SSD_GROUPS, GROUP_W, SSD_STATE)


def _state_from_kernel(h):
    return h.reshape(h.shape[0], SSD_HEADS, SSD_HEADDIM, SSD_STATE)


def _run_group(x, mod, past_k, past_v, ssm_h0, ssm_conv_hist, ffn_conv_hist, p, *, tm, ssd_rows, bq, bk):
    b, t, d = x.shape
    past = 0 if past_k is None else past_k.shape[1]
    chunk = min(CHUNK, t)
    tp = max(t, SUPER)
    if tp != t:
        x = jnp.pad(x, ((0, 0), (0, tp - t), (0, 0)))
        tm = ssd_rows = bq = tp
    mod3 = mod.reshape(b, 6, d)

    assert tm == bq
    zs, xc, dt, k, v, kb, qt, vt, conv_new = _inproj_call(
        x, mod3, p["norm_mix_w"], p["w_cat"], p["w_t"], ssm_conv_hist.astype(F32), p["cw"], p["cbias"],
        tm=tm, real=min(t, tm))

    y_ssd, h_t = _ssd_call(zs, xc, dt, _state_to_kernel(ssm_h0.astype(F32)),
                           p["dtb"], p["alog"], p["dsk"], p["ssm_nw"],
                           chunk=chunk, rows=ssd_rows, real=min(t, ssd_rows))

    if past == 0:
        assert bq == bk and t % bq == 0
        y_att = _attn_call(p["scal"], qt, kb, vt, _bias_tables(p["rel_bias"], bq, bk), p["subw"], bq=bq, bk=bk, noff=0,
                           out_scale=1.0 - p["lam_init"])
    else:
        assert past % bk == 0 and past >= bk and bq == tp <= bk
        bias0 = _bias_rows(p["rel_bias"], past, t, past - bk, bk, past + t)
        bias1 = _bias_rows(p["rel_bias"], past, t, past, bq, past + t)
        y_att = _attn_cached_call(p["scal"], qt, past_k, past_v, kb, vt, bias0, bias1, p["subw"].reshape(1, DA_DV),
                                  tq=t, bk=bk, out_scale=1.0 - p["lam_init"])

    tf = min(t, tm)
    nseq = math.gcd(b, max(1, TILE_ROWS // tf))
    y, ffn_new = _ffn_call(x[:, :t], y_ssd[:, :t], y_att[:, :t], mod3, p["wo1"], p["wo2"], p["nfw"], p["wu"],
                           p["ffn_cw"], p["ffn_cb"], p["wd"], p["fw"], ffn_conv_hist.astype(F32),
                           nseq=nseq, tm=tf)
    k = k.reshape(b, tp, DA_HEADS, 2 * DA_DK)
    v = v.reshape(b, tp, DA_HEADS, DA_DV)
    return (y, k[:, :t], v[:, :t],
            _state_from_kernel(h_t), conv_new, ffn_new)


def kernel(x_prompt, x_sample, c_prompt, c_sample, cache_k, cache_v, state_ssm, state_ssm_conv, state_ffn_conv, w_ada, b_ada, norm_mix_w, w_in, ssm_conv_w, ssm_conv_b, ssm_dt_bias, ssm_a_log, ssm_d, ssm_norm_w, lambda_q1, lambda_k1, lambda_q2, lambda_k2, attn_subln_w, rel_bias, w_out, norm_ffn_w, w_up, ffn_conv_w, ffn_conv_b, w_down, final_norm_w):
    bp, bs = x_prompt.shape[0], x_sample.shape[0]
    dt = x_prompt.dtype
    p = _pack_params(norm_mix_w, w_in, ssm_conv_w, ssm_conv_b, ssm_dt_bias, ssm_a_log, ssm_d, ssm_norm_w,
                     lambda_q1, lambda_k1, lambda_q2, lambda_k2, attn_subln_w, rel_bias, w_out,
                     norm_ffn_w, w_up, ffn_conv_w, ffn_conv_b, w_down, final_norm_w, 0)
    c_all = jnp.concatenate([c_prompt, c_sample], axis=0)
    npad = -c_all.shape[0] % SUBLANES
    c_all = jnp.pad(c_all, ((0, npad), (0, 0)))
    mod = _mod_call(c_all, w_ada[0], b_ada[0].reshape(1, -1))

    zeros = lambda *s: jnp.zeros(s, dt)
    out_p = _run_group(x_prompt, mod[:bp], None, None,
                       zeros(bp, SSD_HEADS, SSD_HEADDIM, SSD_STATE), zeros(bp, SSD_CONV - 1, SSD_CONV_DIM),
                       zeros(bp, FFN_CONV - 1, 2 * D_FF), p, tm=512, ssd_rows=256, bq=512, bk=512)
    out_s = _run_group(x_sample, mod[bp:bp + bs], cache_k[0], cache_v[0], state_ssm[0], state_ssm_conv[0],
                       state_ffn_conv[0], p, tm=SUPER, ssd_rows=SUPER, bq=SUPER, bk=512)
    y_p, k_p, v_p, h_p, c_p, f_p = out_p
    y_s, k_s, v_s, h_s, c_s, f_s = out_s
    return (y_p, y_s, k_p[None], v_p[None], h_p[None], c_p[None], f_p[None],
            k_s[None], v_s[None], h_s[None], c_s[None], f_s[None])
```

```python
import functools
import math

import numpy as np
import jax
import jax.numpy as jnp
from jax import lax
from jax.experimental import pallas as pl
from jax.experimental.pallas import tpu as pltpu

F32 = jnp.float32
BF16 = jnp.bfloat16

D_MODEL = 1024
CHUNK = 64
SSD_WIDTH = 512
SSD_HEADDIM = 64
SSD_HEADS = 8
SSD_GROUPS = 2
SSD_HPG = 4
SSD_STATE = 128
SSD_CONV = 4
SSD_CONV_DIM = SSD_WIDTH + 2 * SSD_GROUPS * SSD_STATE
GROUP_W = SSD_HPG * SSD_HEADDIM
DA_WIDTH = 512
DA_DK = 64
DA_DV = 128
DA_HEADS = 4
REL_BUCKETS = 32
REL_MAX_DIST = 128
D_FF = 2816
FFN_CONV = 3
EPS = 1e-6
IN_SPLITS = (512, 1536, 1544, 2056, 2568)
LANES = 128
SUBLANES = 8
TILE_ROWS = 512
MOD_TN = 1024
SUPER = 128
SSD_ROWS = 256
SSD_NSEQ = 4
ATT_BLOCK = 512
ATT_KT = 64
FFN_CN = 256
FFN_NC = D_FF // FFN_CN
NEG = -1e30
VMEM_LIMIT = 56 * 1024 * 1024

PZ, PX, PDT, PK, PV, PEND = 0, 512, 1536, 1664, 2176, 2688
LOG2E = math.log2(math.e)


def _silu(x):
    return x / (1.0 + jnp.exp(-x))


def _softplus(x):
    return jnp.maximum(x, 0.0) + jnp.log1p(jnp.exp(-jnp.abs(x)))


def _split3(x):
    hi = x.astype(BF16)
    r1 = x - hi.astype(F32)
    mid = r1.astype(BF16)
    lo = (r1 - mid.astype(F32)).astype(BF16)
    return hi, mid, lo


def _rms(x, w):
    return x * lax.rsqrt(jnp.mean(x * x, axis=-1, keepdims=True) + EPS) * w


def _const_spec(shape):
    nd = len(shape)
    return pl.BlockSpec(shape, lambda *_: (0,) * nd)


def _mod_kernel(c_ref, w_ref, b_ref, o_ref):
    a = _silu(c_ref[...]).astype(BF16)
    o_ref[...] = jnp.dot(a, w_ref[...].astype(BF16), preferred_element_type=F32) + b_ref[...]


def _mod_call(c, w_ada, b_ada):
    n, d = c.shape
    nout = w_ada.shape[1]
    tn = MOD_TN
    return pl.pallas_call(
        _mod_kernel,
        grid=(nout // tn,),
        in_specs=[pl.BlockSpec((n, d), lambda j: (0, 0)),
                  pl.BlockSpec((d, tn), lambda j: (0, j)),
                  pl.BlockSpec((1, tn), lambda j: (0, j))],
        out_specs=pl.BlockSpec((n, tn), lambda j: (0, j)),
        out_shape=jax.ShapeDtypeStruct((n, nout), F32),
        name="mod",
    )(c, w_ada, b_ada)


def _inproj_kernel(x_ref, mod_ref, nw_ref, w_ref, wt_ref, hist_ref, cw_ref, cbias_ref,
                   zs_ref, xc_ref, dt_ref, k_ref, v_ref, kb_ref, qt_ref, vt_ref, cout_ref, cbuf, hb_scr, zbuf,
                   *, tm, real):
    t = pl.program_id(1)
    nconv = SSD_CONV - 1

    @pl.when(t == 0)
    def _init():
        cbuf[0:SUBLANES, :] = jnp.zeros((SUBLANES, SSD_CONV_DIM), F32)
        cbuf[SUBLANES - nconv:SUBLANES, :] = hist_ref[...]

    h = _rms(x_ref[...], nw_ref[...]) * (1.0 + mod_ref[1:2, :]) + mod_ref[0:1, :]
    hb_scr[...] = h.astype(BF16)

    def proj(a, b):
        return jnp.dot(hb_scr[...], w_ref[:, a:b], preferred_element_type=F32)

    def proj_t(a, b):
        return lax.dot_general(wt_ref[a:b, :], hb_scr[...], (((1,), (1,)), ((), ())), preferred_element_type=F32)

    cbuf[SUBLANES:SUBLANES + tm, :] = proj(PX, PDT)
    zbuf[...] = proj(PZ, PX)
    dt_ref[...] = proj(PDT, PK)
    k = proj(PK, PV)
    v = proj(PV, PEND)
    for hd in range(DA_HEADS):
        dst = pl.ds(hd, tm, stride=DA_HEADS)
        k_ref[dst, :] = k[:, hd * DA_DV:(hd + 1) * DA_DV]
        v_ref[dst, :] = v[:, hd * DA_DV:(hd + 1) * DA_DV]
    kb_ref[...] = k.astype(BF16)
    qt_ref[...] = proj_t(0, DA_WIDTH).astype(BF16)
    for hd in range(DA_HEADS):
        vt_ref[hd * DA_DV:(hd + 1) * DA_DV, :] = v[:, hd * DA_DV:(hd + 1) * DA_DV].T.astype(BF16)
    conv = cbias_ref[...]
    for j in range(SSD_CONV):
        off = SUBLANES - nconv + j
        conv = conv + cw_ref[j:j + 1, :] * cbuf[off:off + tm, :]
    tail = cbuf[SUBLANES - nconv + real:SUBLANES + real, :]
    cout_ref[...] = tail
    cbuf[SUBLANES - nconv:SUBLANES, :] = tail
    xc_ref[...] = _silu(conv).astype(BF16)
    zs_ref[...] = _silu(zbuf[...]).astype(BF16)


def _inproj_call(x, mod3, norm_w, w_cat, w_t, hist, cw, cbias, *, tm, real):
    b, t, d = x.shape
    nt = t // tm

    def row(width):
        return pl.BlockSpec((None, tm, width), lambda i, j: (i, j, 0))

    def out(width, dtype):
        return jax.ShapeDtypeStruct((b, t, width), dtype)

    tspec = pl.BlockSpec((None, None, DA_WIDTH, tm), lambda i, j: (i, j, 0, 0))
    tshape = jax.ShapeDtypeStruct((b, nt, DA_WIDTH, tm), BF16)
    hist_spec = pl.BlockSpec((None, SSD_CONV - 1, SSD_CONV_DIM), lambda i, j: (i, 0, 0))
    hspec = pl.BlockSpec((None, tm * DA_HEADS, DA_DV), lambda i, j: (i, j, 0))
    hshape = jax.ShapeDtypeStruct((b, t * DA_HEADS, DA_DV), F32)
    return pl.pallas_call(
        functools.partial(_inproj_kernel, tm=tm, real=real),
        grid=(b, nt),
        in_specs=[row(d),
                  pl.BlockSpec((None, 6, d), lambda i, j: (i, 0, 0)),
                  _const_spec((1, d)),
                  _const_spec((d, PEND)),
                  _const_spec((DA_WIDTH, d)),
                  hist_spec, _const_spec((SSD_CONV, SSD_CONV_DIM)), _const_spec((1, SSD_CONV_DIM))],
        out_specs=[row(512), row(1024), row(LANES), hspec, hspec, row(512), tspec, tspec, hist_spec],
        out_shape=[out(512, BF16), out(1024, BF16), out(LANES, F32),
                   hshape, hshape, out(512, BF16), tshape, tshape,
                   jax.ShapeDtypeStruct((b, SSD_CONV - 1, SSD_CONV_DIM), F32)],
        scratch_shapes=[pltpu.VMEM((tm + SUBLANES, SSD_CONV_DIM), F32), pltpu.VMEM((tm, d), BF16),
                        pltpu.VMEM((tm, SSD_WIDTH), F32)],
        compiler_params=pltpu.CompilerParams(dimension_semantics=("parallel", "arbitrary"),
                                             vmem_limit_bytes=VMEM_LIMIT),
        name="inproj",
    )(x, mod3, norm_w, w_cat, w_t, hist, cw, cbias)


def _ssd_kernel(zs_ref, xc_ref, dt_ref, h0_ref, dtb_ref, alog_ref, dsk_ref, nw_ref, tri_ref, e_ref,
                y_ref, hout_ref, h_scr, ybuf, *, nseq, chunk, rows, real):
    t = pl.program_id(1)
    seqs = range(nseq)

    @pl.when(t == 0)
    def _init():
        for i in seqs:
            for g in range(SSD_GROUPS):
                h_scr[i, g] = h0_ref[i, g].T

    li = lax.broadcasted_iota(jnp.int32, (SUPER, SUPER), 0)
    si = lax.broadcasted_iota(jnp.int32, (SUPER, SUPER), 1)
    cshift = chunk.bit_length() - 1
    mask2 = ((li >> cshift) == (si >> cshift)) & (si <= li)
    lane_g = lax.broadcasted_iota(jnp.int32, (SUPER, GROUP_W), 1) >> (SSD_HEADDIM.bit_length() - 1)

    pre = []
    for i in seqs:
        xs = xc_ref[i, :, 0:SSD_WIDTH].astype(F32)
        dtv = _softplus(dt_ref[i] + dtb_ref[...])
        da = dtv * (-jnp.exp(alog_ref[...]))
        acs = jnp.dot(tri_ref[...], jnp.concatenate(_split3(da), axis=0), preferred_element_type=F32)
        dt_x = jnp.dot(jnp.concatenate(_split3(dtv), axis=1), e_ref[...], preferred_element_type=F32)
        acs_x = jnp.dot(jnp.concatenate(_split3(acs), axis=1), e_ref[...], preferred_element_type=F32)
        pre.append((xs, acs, acs_x, jnp.exp(acs_x), xs * dt_x))

    for sb in range(rows // SUPER):
        o = sb * SUPER
        nreal = (min(real, o + SUPER) - o) // chunk
        acs2 = [pre[i][1][o:o + SUPER, :] for i in seqs]
        acs_t = [a.T for a in acs2]
        chains = [(g, i) for g in range(SSD_GROUPS) for i in seqs]

        def gsl(g):
            return slice(g * GROUP_W, (g + 1) * GROUP_W)

        cmb, cb2, bm_t = {}, {}, {}
        for g, i in chains:
            bcol = SSD_WIDTH + g * SSD_STATE
            ccol = SSD_WIDTH + (SSD_GROUPS + g) * SSD_STATE
            bmb = xc_ref[i, o:o + SUPER, bcol:bcol + SSD_STATE]
            cmb[g, i] = xc_ref[i, o:o + SUPER, ccol:ccol + SSD_STATE]
            cb2[g, i] = lax.dot_general(cmb[g, i], bmb, (((1,), (1,)), ((), ())), preferred_element_type=F32)
            bm_t[g, i] = bmb.astype(F32).T.astype(BF16)
        for g, i in chains:
            ms = []
            for rr in range(SSD_HPG):
                r = g * SSD_HPG + rr
                seg = acs2[i][:, r:r + 1] - acs_t[i][r:r + 1, :]
                dec = jnp.where(mask2, jnp.exp(jnp.where(mask2, seg, 0.0)), 0.0)
                ms.append((cb2[g, i] * dec).astype(BF16))
            full = jnp.dot(jnp.concatenate(ms, axis=0), pre[i][4][o:o + SUPER, gsl(g)].astype(BF16),
                           preferred_element_type=F32)
            ydiag = full[0:SUPER]
            for rr in range(1, SSD_HPG):
                ydiag = jnp.where(lane_g == rr, full[rr * SUPER:(rr + 1) * SUPER], ydiag)
            ybuf[i, o:o + SUPER, gsl(g)] = ydiag
        for j in range(nreal):
            a0, a1 = o + j * chunk, o + (j + 1) * chunk
            h_t = {c: h_scr[c[1], c[0]] for c in chains}
            yoff = {(g, i): jnp.dot(cmb[g, i][j * chunk:(j + 1) * chunk, :], h_t[g, i].astype(BF16),
                                    preferred_element_type=F32) for g, i in chains}
            st = {}
            for g, i in chains:
                _, _, acs_x, eacs_x, xd = pre[i]
                ybuf[i, a0:a1, gsl(g)] = ybuf[i, a0:a1, gsl(g)] + yoff[g, i] * eacs_x[a0:a1, gsl(g)]
                dte = jnp.exp(acs_x[a1 - 1:a1, gsl(g)] - acs_x[a0:a1, gsl(g)])
                xw = (xd[a0:a1, gsl(g)] * dte).astype(BF16)
                pieces = []
                if j > 0:
                    pieces.append(jnp.zeros((j * chunk, GROUP_W), BF16))
                pieces.append(xw)
                if (j + 1) * chunk < SUPER:
                    pieces.append(jnp.zeros((SUPER - (j + 1) * chunk, GROUP_W), BF16))
                xw2 = jnp.concatenate(pieces, axis=0) if len(pieces) > 1 else xw
                st[g, i] = jnp.dot(bm_t[g, i], xw2, preferred_element_type=F32)
            for g, i in chains:
                h_scr[i, g] = h_t[g, i] * pre[i][3][a1 - 1:a1, gsl(g)] + st[g, i]

    for i in seqs:
        y = (ybuf[i] + dsk_ref[...] * pre[i][0]) * zs_ref[i].astype(F32)
        for g in range(SSD_GROUPS):
            gs = slice(g * GROUP_W, (g + 1) * GROUP_W)
            y_ref[i, :, gs] = _rms(y[:, gs], nw_ref[:, gs]).astype(BF16)

    @pl.when(t == pl.num_programs(1) - 1)
    def _fin():
        for i in seqs:
            for g in range(SSD_GROUPS):
                hout_ref[i, g] = h_scr[i, g].T


def _ssd_call(zs, xc, dt, h0_t, dtb, alog, dsk, nw, *, chunk, rows, real):
    b, t, _ = zs.shape
    ii = np.arange(rows)
    tri = ((ii[:, None] // chunk == ii[None, :] // chunk) & (ii[None, :] <= ii[:, None])).astype(np.float32)
    e = np.zeros((LANES, SSD_WIDTH), np.float32)
    for r in range(SSD_HEADS):
        e[r, r * SSD_HEADDIM:(r + 1) * SSD_HEADDIM] = 1.0

    nseq = math.gcd(b, SSD_NSEQ)

    def row(width):
        return pl.BlockSpec((nseq, rows, width), lambda i, j: (i, j, 0))

    state_spec = pl.BlockSpec((nseq, SSD_GROUPS, GROUP_W, SSD_STATE), lambda i, j: (i, 0, 0, 0))
    kern = functools.partial(_ssd_kernel, nseq=nseq, chunk=chunk, rows=rows, real=real)
    return pl.pallas_call(
        kern,
        grid=(b // nseq, t // rows),
        in_specs=[row(SSD_WIDTH), row(SSD_CONV_DIM), row(LANES), state_spec,
                  _const_spec((1, LANES)), _const_spec((1, LANES)),
                  _const_spec((1, SSD_WIDTH)), _const_spec((1, SSD_WIDTH)),
                  _const_spec((rows, 3 * rows)), _const_spec((3 * LANES, SSD_WIDTH))],
        out_specs=[row(SSD_WIDTH), state_spec],
        out_shape=[jax.ShapeDtypeStruct((b, t, SSD_WIDTH), BF16),
                   jax.ShapeDtypeStruct((b, SSD_GROUPS, GROUP_W, SSD_STATE), F32)],
        scratch_shapes=[pltpu.VMEM((nseq, SSD_GROUPS, SSD_STATE, GROUP_W), F32),
                        pltpu.VMEM((nseq, rows, SSD_WIDTH), F32)],
        compiler_params=pltpu.CompilerParams(dimension_semantics=("parallel", "arbitrary"),
                                             vmem_limit_bytes=VMEM_LIMIT),
        name="ssd",
    )(zs, xc, dt, h0_t, dtb, alog, dsk, nw,
      jnp.asarray(np.tile(tri, (1, 3)), BF16), jnp.asarray(np.tile(e, (3, 1)), BF16))


def _attn_kernel(scal_ref, qt_ref, k_ref, vt_ref, btab_ref, subw_ref, o_ref,
                 m_scr, l_scr, acc_scr, sa_scr, sb_scr, sc_scr, pa_scr, pb_scr, alpha_scr, qz_scr, bias_scr,
                 *, bq, bk, out_scale):
    h = pl.program_id(0)
    qi = pl.program_id(2)
    kn0 = qi - 1
    lam = scal_ref[0]
    cfar = scal_ref[1 + h]

    @pl.when((pl.program_id(1) == 0) & (qi == 0))
    def _build_bias_tiles():
        width = bq + bk
        kj = lax.broadcasted_iota(jnp.int32, (bk, bq), 0)
        qj = lax.broadcasted_iota(jnp.int32, (bk, bq), 1)
        cshift = CHUNK.bit_length() - 1
        for d in range(2):
            skew = pltpu.roll(jnp.broadcast_to(btab_ref[d], (bk, width)), 0, 1, stride=1, stride_axis=0)
            tile = skew[:, bk:width]
            if d == 1:
                tile = jnp.where((kj >> cshift) <= (qj >> cshift), tile, NEG)
            bias_scr[d] = tile

    zero = jnp.zeros((DA_DK, bq), BF16)
    qz_scr[0, 0:DA_DK, :] = qt_ref[0:DA_DK, :]
    qz_scr[0, DA_DK:DA_DV, :] = zero
    qz_scr[1, 0:DA_DK, :] = zero
    qz_scr[1, DA_DK:DA_DV, :] = qt_ref[DA_DK:DA_DV, :]

    m_scr[...] = jnp.full(m_scr.shape, NEG, F32)
    l_scr[...] = jnp.zeros(l_scr.shape, F32)
    acc_scr[...] = jnp.zeros(acc_scr.shape, F32)

    nsub = bk // ATT_KT

    def fold(x):
        return x.reshape(ATT_KT // SUBLANES, SUBLANES, bq)

    def scores(s_buf, first, count):
        for mm in range(2):
            for e in range(count):
                start = pl.multiple_of((first + e) * bk, bk)
                s_buf[mm, e] = jnp.dot(k_ref[pl.ds(start, bk), :], qz_scr[mm], preferred_element_type=F32)

    def softmax(s_buf, p_buf, entries):
        alphas = []
        for mm in range(2):
            cand = None
            for e, (near, shift) in enumerate(entries):
                mx = None
                for t in range(nsub):
                    rows = slice(t * ATT_KT, (t + 1) * ATT_KT)
                    s = s_buf[mm, e, rows, :]
                    if near is not None:
                        s = s + bias_scr[near, rows, :]
                    pm = jnp.max(fold(s), axis=0)
                    mx = pm if mx is None else jnp.maximum(mx, pm)
                mx = jnp.max(mx, axis=0, keepdims=True) + shift
                cand = mx if cand is None else jnp.maximum(cand, mx)
            m_old = m_scr[mm]
            m_new = jnp.maximum(m_old, cand)
            ls = None
            for e, (near, shift) in enumerate(entries):
                off = m_new - shift
                for t in range(nsub):
                    rows = slice(t * ATT_KT, (t + 1) * ATT_KT)
                    s = s_buf[mm, e, rows, :]
                    if near is not None:
                        s = s + bias_scr[near, rows, :]
                    p = jnp.exp2(s - off)
                    p_buf[mm, e * bk + t * ATT_KT:e * bk + (t + 1) * ATT_KT, :] = p.astype(BF16)
                    ps = jnp.sum(fold(p), axis=0)
                    ls = ps if ls is None else ls + ps
            alpha = jnp.exp2(m_old - m_new)
            l_scr[mm] = alpha * l_scr[mm] + jnp.sum(ls, axis=0, keepdims=True)
            m_scr[mm] = m_new
            alphas.append(alpha)
        return alphas

    def pv(p_buf, first, n):
        vts = [vt_ref[first + e] for e in range(n)]
        vt = jnp.concatenate(vts, axis=1) if n > 1 else vts[0]
        return [jnp.dot(vt, p_buf[mm, 0:n * bk, :], preferred_element_type=F32) for mm in range(2)]

    def accumulate(alphas, pvs):
        for mm in range(2):
            acc_scr[mm] = alphas[mm] * acc_scr[mm] + pvs[mm]

    def softmax_pv(s_buf, first, entries):
        alphas = softmax(s_buf, pa_scr, entries)
        accumulate(alphas, pv(pa_scr, first, len(entries)))

    nfar = jnp.maximum(kn0, 0)
    odd = nfar % 2
    far = (None, cfar)

    near_pair = [(0, 0.0), (1, 0.0)]

    @pl.when(kn0 < 0)
    def _only_first():
        scores(sc_scr, 0, 1)
        softmax_pv(sc_scr, 0, [(1, 0.0)])

    @pl.when(kn0 >= 0)
    def _groups():
        @pl.when(odd == 1)
        def _single():
            scores(sc_scr, 0, 1)
            scores(sa_scr, 1, 2)
            softmax_pv(sc_scr, 0, [far])

        @pl.when(odd == 0)
        def _first_pair():
            scores(sa_scr, 0, 2)

        def far_pair(s_cur, s_next, cur):
            scores(s_next, cur + 2, 2)
            softmax_pv(s_cur, cur, [far, far])

        def far_step(s_cur, p_cur, s_next, p_prev, cur):
            scores(s_next, cur + 2, 2)
            pending = None if p_prev is None else pv(p_prev, cur - 2, 2)
            alphas = softmax(s_cur, p_cur, [far, far])
            if pending is not None:
                accumulate([alpha_scr[0], alpha_scr[1]], pending)
            for mm in range(2):
                alpha_scr[mm] = alphas[mm]

        npairs = nfar // 2
        niter = npairs // 2

        @pl.when(niter >= 1)
        def _far_loop():
            far_step(sa_scr, pa_scr, sb_scr, None, odd)
            far_step(sb_scr, pb_scr, sa_scr, pa_scr, odd + 2)

            def far_body(j, carry):
                cur = odd + 4 * j
                far_step(sa_scr, pa_scr, sb_scr, pb_scr, cur)
                far_step(sb_scr, pb_scr, sa_scr, pa_scr, cur + 2)
                return carry

            lax.fori_loop(1, niter, far_body, 0)
            accumulate([alpha_scr[0], alpha_scr[1]], pv(pb_scr, odd + 4 * niter - 2, 2))

        @pl.when(npairs % 2 == 1)
        def _tail_b():
            far_pair(sa_scr, sb_scr, kn0 - 2)
            softmax_pv(sb_scr, kn0, near_pair)

        @pl.when(npairs % 2 == 0)
        def _tail_a():
            softmax_pv(sa_scr, kn0, near_pair)

    o = acc_scr[0] * (1.0 / l_scr[0]) - acc_scr[1] * (lam / l_scr[1])
    o = o * lax.rsqrt(jnp.mean(o * o, axis=0, keepdims=True) + EPS) * (subw_ref[...] * out_scale)
    o_ref[...] = o.T.astype(BF16)


def _attn_call(scal, qt, kb, vt, btab, subw, *, bq, bk, out_scale):
    b, nq = qt.shape[:2]
    tk = kb.shape[1]
    nkb = vt.shape[1]
    kern = functools.partial(_attn_kernel, bq=bq, bk=bk, out_scale=out_scale)
    return pl.pallas_call(
        kern,
        grid=(DA_HEADS, b, nq),
        in_specs=[pl.BlockSpec(memory_space=pltpu.SMEM),
                  pl.BlockSpec((None, None, DA_DV, bq), lambda h, i, j: (i, j, h, 0)),
                  pl.BlockSpec((None, tk, DA_DV), lambda h, i, j: (i, 0, h)),
                  pl.BlockSpec((None, nkb, DA_DV, bk), lambda h, i, j: (i, 0, h, 0)),
                  pl.BlockSpec((None, 2, 1, bq + bk), lambda h, i, j: (h, 0, 0, 0)),
                  pl.BlockSpec((DA_DV, 1), lambda h, i, j: (0, 0))],
        out_specs=pl.BlockSpec((None, bq, DA_DV), lambda h, i, j: (i, j, h)),
        out_shape=jax.ShapeDtypeStruct((b, nq * bq, DA_WIDTH), BF16),
        scratch_shapes=[pltpu.VMEM((2, 1, bq), F32), pltpu.VMEM((2, 1, bq), F32),
                        pltpu.VMEM((2, DA_DV, bq), F32),
                        pltpu.VMEM((2, 2, bk, bq), F32), pltpu.VMEM((2, 2, bk, bq), F32),
                        pltpu.VMEM((2, 1, bk, bq), F32),
                        pltpu.VMEM((2, 2 * bk, bq), BF16), pltpu.VMEM((2, 2 * bk, bq), BF16),
                        pltpu.VMEM((2, 1, bq), F32),
                        pltpu.VMEM((2, DA_DV, bq), BF16),
                        pltpu.VMEM((2, bk, bq), F32)],
        compiler_params=pltpu.CompilerParams(dimension_semantics=("arbitrary", "arbitrary", "arbitrary"),
                                             vmem_limit_bytes=VMEM_LIMIT),
        name="attn",
    )(scal, qt, kb, vt, btab, subw)


def _attn_cached_kernel(scal_ref, qt_ref, kc_ref, vc_ref, kn_ref, vtn_ref, bias0_ref, bias1_ref, subw_ref, o_ref,
                        *, bq, tq, bk, out_scale):
    lam = scal_ref[0]
    past = kc_ref.shape[0] // DA_HEADS
    lane = lax.broadcasted_iota(jnp.int32, (tq, DA_DV), 1)
    nt = (((1,), (1,)), ((), ()))

    for h in range(DA_HEADS):
        hs = slice(h * DA_DV, (h + 1) * DA_DV)
        cfar = scal_ref[1 + h]
        qn = qt_ref[hs, :].astype(F32).T[0:tq, :]
        q2 = jnp.concatenate([jnp.where(lane < DA_DK, qn, 0.0), jnp.where(lane >= DA_DK, qn, 0.0)],
                             axis=0).astype(BF16)
        head_rows = pl.ds(h, past, stride=DA_HEADS)
        s_c = lax.dot_general(q2, kc_ref[head_rows, :].astype(BF16), nt, preferred_element_type=F32)
        s_n = lax.dot_general(q2, kn_ref[:, hs], nt, preferred_element_type=F32)
        b0 = bias0_ref[h]
        b1 = bias1_ref[h]
        s = jnp.concatenate([s_c[:, 0:past - bk] + cfar,
                             s_c[:, past - bk:past] + jnp.concatenate([b0, b0], axis=0),
                             s_n + jnp.concatenate([b1, b1], axis=0)], axis=1)
        p = jnp.exp2(s - jnp.max(s, axis=1, keepdims=True))
        inv = 1.0 / jnp.sum(p, axis=1, keepdims=True)
        pb = p.astype(BF16)
        acc = (jnp.dot(pb[:, 0:past], vc_ref[head_rows, :].astype(BF16), preferred_element_type=F32)
               + jnp.dot(pb[:, past:past + bq], vtn_ref[hs, :].astype(F32).T.astype(BF16),
                         preferred_element_type=F32))
        o = acc[0:tq, :] * inv[0:tq] - acc[tq:2 * tq, :] * (lam * inv[tq:2 * tq])
        o_ref[0:tq, hs] = (_rms(o, subw_ref[...]) * out_scale).astype(BF16)
        o_ref[tq:bq, hs] = jnp.zeros((bq - tq, DA_DV), BF16)


def _attn_cached_call(scal, qt, cache_k, cache_v, kb, vt, bias0, bias1, subw, *, tq, bk, out_scale):
    b, past = cache_k.shape[:2]
    bq = qt.shape[-1]
    rows = past * DA_HEADS
    kern = functools.partial(_attn_cached_kernel, bq=bq, tq=tq, bk=bk, out_scale=out_scale)
    cache_spec = pl.BlockSpec((None, rows, DA_DV), lambda i: (i, 0, 0))
    return pl.pallas_call(
        kern,
        grid=(b,),
        in_specs=[pl.BlockSpec(memory_space=pltpu.SMEM),
                  pl.BlockSpec((None, None, DA_WIDTH, bq), lambda i: (i, 0, 0, 0)),
                  cache_spec, cache_spec,
                  pl.BlockSpec((None, bq, DA_WIDTH), lambda i: (i, 0, 0)),
                  pl.BlockSpec((None, None, DA_WIDTH, bq), lambda i: (i, 0, 0, 0)),
                  _const_spec((DA_HEADS, tq, bk)), _const_spec((DA_HEADS, tq, bq)),
                  _const_spec((1, DA_DV))],
        out_specs=pl.BlockSpec((None, bq, DA_WIDTH), lambda i: (i, 0, 0)),
        out_shape=jax.ShapeDtypeStruct((b, bq, DA_WIDTH), BF16),
        compiler_params=pltpu.CompilerParams(dimension_semantics=("parallel",), vmem_limit_bytes=VMEM_LIMIT),
        name="attn_cached",
    )(scal, qt, cache_k.reshape(b, rows, DA_DV), cache_v.reshape(b, rows, DA_DV), kb, vt, bias0, bias1, subw)


def _rel_bucket(rel):
    nb = REL_BUCKETS // 2
    max_exact = nb // 2
    n = jnp.abs(rel)
    nf = jnp.maximum(n, 1).astype(jnp.float32)
    large = max_exact + (jnp.log(nf / max_exact) / math.log(REL_MAX_DIST / max_exact)
                         * (nb - max_exact)).astype(jnp.int32)
    large = jnp.minimum(large, nb - 1)
    return jnp.where(rel > 0, nb, 0) + jnp.where(n < max_exact, n, large)


def _bias_tables(rel_bias, bq, bk):
    tabs = []
    for d in range(2):
        offs = (d - 1) * bk + bk - np.arange(bq + bk)
        tabs.append(rel_bias[_rel_bucket(jnp.asarray(offs, jnp.int32))].astype(F32).T * LOG2E)
    return jnp.stack(tabs, axis=1)[:, :, None, :]


def _bias_rows(rel_bias, qpos0, tq, kpos0, nkeys, tk_real):
    offs = (kpos0 - qpos0) + np.arange(-(tq - 1), nkeys)
    table = rel_bias[_rel_bucket(jnp.asarray(offs, jnp.int32))].astype(F32).T * LOG2E
    rows = jnp.stack([table[:, tq - 1 - i:tq - 1 - i + nkeys] for i in range(tq)], axis=1)
    qpos = qpos0 + np.arange(tq)
    kpos = kpos0 + np.arange(nkeys)
    vis = (kpos[None, :] // CHUNK <= qpos[:, None] // CHUNK) & (kpos[None, :] < tk_real)
    return jnp.where(jnp.asarray(vis)[None], rows, NEG)


def _ffn_kernel(x_ref, ys_ref, ya_ref, mod_ref, wo1_ref, wo2_ref, nfw_ref, wu_ref, cw_ref, cb_ref, wd_ref, fw_ref,
                hist_ref, y_ref, tail_ref, tail_scr, buf_a, buf_b, x1_scr, h2_scr, act_scr, *, nseq, tm):
    t = pl.program_id(1)
    nh = FFN_CONV - 1
    lo = SUBLANES - nh
    seqs = range(nseq)

    def rows(i):
        return slice(i * tm, (i + 1) * tm)

    def stacked(ref):
        return jnp.concatenate([ref[i] for i in seqs], axis=0) if nseq > 1 else ref[0]

    @pl.when(t == 0)
    def _init():
        for i in seqs:
            tail_scr[i, lo:SUBLANES, :] = hist_ref[i]

    mix = (jnp.dot(stacked(ys_ref), wo1_ref[...], preferred_element_type=F32)
           + jnp.dot(stacked(ya_ref), wo2_ref[...], preferred_element_type=F32))
    for i in seqs:
        x1 = x_ref[i] + mod_ref[i, 2:3, :] * mix[rows(i)]
        x1_scr[rows(i), :] = x1
        h2 = _rms(x1, nfw_ref[...]) * (1.0 + mod_ref[i, 4:5, :]) + mod_ref[i, 3:4, :]
        h2_scr[rows(i), :] = h2.astype(BF16)

    def cols(j, half):
        return slice(half * D_FF + j * FFN_CN, half * D_FF + (j + 1) * FFN_CN)

    def up(j, buf):
        for half in range(2):
            u = jnp.dot(h2_scr[...], wu_ref[:, cols(j, half)], preferred_element_type=F32)
            for i in seqs:
                buf[half, i, SUBLANES:SUBLANES + tm, :] = u[rows(i)]

    def conv(j, buf, half, i):
        cs = cols(j, half)
        buf[half, i, lo:SUBLANES, :] = tail_scr[i, lo:SUBLANES, cs]
        c = cb_ref[:, cs]
        for k in range(FFN_CONV):
            c = c + cw_ref[k:k + 1, cs] * buf[half, i, lo + k:lo + k + tm, :]
        tail_scr[i, lo:SUBLANES, cs] = buf[half, i, lo + tm:SUBLANES + tm, :]
        return c

    bufs = (buf_a, buf_b)

    @pl.when(t >= 0)
    def _chunks():
        up(0, bufs[0])
        for j in range(FFN_NC):
            if j + 1 < FFN_NC:
                up(j + 1, bufs[(j + 1) % 2])
            for i in seqs:
                cv = conv(j, bufs[j % 2], 0, i)
                cg = conv(j, bufs[j % 2], 1, i)
                act_scr[rows(i), j * FFN_CN:(j + 1) * FFN_CN] = (_silu(cg) * cv).astype(BF16)

    f = jnp.dot(act_scr[...], wd_ref[...], preferred_element_type=F32)
    for i in seqs:
        x2 = x1_scr[rows(i), :] + mod_ref[i, 5:6, :] * f[rows(i)]
        y_ref[i] = _rms(x2, fw_ref[...])
        tail_ref[i] = tail_scr[i, lo:SUBLANES, :]


def _ffn_call(x, ys, ya, mod3, wo1, wo2, nfw, wu, cw, cb, wd, fw, hist, *, nseq, tm):
    b, t, d = x.shape
    nh = FFN_CONV - 1
    m = nseq * tm

    def row(width):
        return pl.BlockSpec((nseq, tm, width), lambda i, j: (i, j, 0))

    hist_spec = pl.BlockSpec((nseq, nh, 2 * D_FF), lambda i, j: (i, 0, 0))
    kern = functools.partial(_ffn_kernel, nseq=nseq, tm=tm)
    return pl.pallas_call(
        kern,
        grid=(b // nseq, t // tm),
        in_specs=[row(d), row(SSD_WIDTH), row(DA_WIDTH),
                  pl.BlockSpec((nseq, 6, d), lambda i, j: (i, 0, 0)),
                  _const_spec((SSD_WIDTH, d)), _const_spec((DA_WIDTH, d)), _const_spec((1, d)),
                  _const_spec((d, 2 * D_FF)), _const_spec((FFN_CONV, 2 * D_FF)), _const_spec((1, 2 * D_FF)),
                  _const_spec((D_FF, d)), _const_spec((1, d)),
                  hist_spec],
        out_specs=[row(d), hist_spec],
        out_shape=[jax.ShapeDtypeStruct((b, t, d), F32),
                   jax.ShapeDtypeStruct((b, nh, 2 * D_FF), F32)],
        scratch_shapes=[pltpu.VMEM((nseq, SUBLANES, 2 * D_FF), F32),
                        pltpu.VMEM((2, nseq, tm + SUBLANES, FFN_CN), F32),
                        pltpu.VMEM((2, nseq, tm + SUBLANES, FFN_CN), F32),
                        pltpu.VMEM((m, d), F32), pltpu.VMEM((m, d), BF16), pltpu.VMEM((m, D_FF), BF16)],
        compiler_params=pltpu.CompilerParams(dimension_semantics=("parallel", "arbitrary"),
                                             vmem_limit_bytes=VMEM_LIMIT),
        name="ffn",
    )(x, ys, ya, mod3, wo1, wo2, nfw, wu, cw, cb, wd, fw, hist)


def _pack_params(norm_mix_w, w_in, ssm_conv_w, ssm_conv_b, ssm_dt_bias, ssm_a_log, ssm_d, ssm_norm_w,
                 lambda_q1, lambda_k1, lambda_q2, lambda_k2, attn_subln_w, rel_bias, w_out,
                 norm_ffn_w, w_up, ffn_conv_w, ffn_conv_b, w_down, final_norm_w, layer):
    l = layer
    wz, wx, wdt, wq, wk, wv = jnp.split(w_in[l], IN_SPLITS, axis=-1)
    wdt = jnp.pad(wdt, ((0, 0), (0, LANES - SSD_HEADS)))
    w_cat = jnp.concatenate([wz, wx, wdt, wk, wv], axis=-1).astype(BF16)
    w_t = (wq * (DA_DK ** -0.5 * LOG2E)).T.astype(BF16)

    def pad_heads(v):
        return jnp.pad(v.astype(F32), (0, LANES - SSD_HEADS)).reshape(1, LANES)

    lam_init = 0.8 - 0.6 * math.exp(-0.3 * l)
    lam = (jnp.exp(jnp.sum(lambda_q1[l].astype(F32) * lambda_k1[l].astype(F32)))
           - jnp.exp(jnp.sum(lambda_q2[l].astype(F32) * lambda_k2[l].astype(F32))) + lam_init)
    far_bias = rel_bias[REL_BUCKETS // 2 - 1].astype(F32)
    return dict(
        norm_mix_w=norm_mix_w[l].reshape(1, D_MODEL), w_cat=w_cat, w_t=w_t,
        cw=ssm_conv_w[l], cbias=ssm_conv_b[l].reshape(1, SSD_CONV_DIM),
        dtb=pad_heads(ssm_dt_bias[l]), alog=pad_heads(ssm_a_log[l]),
        dsk=jnp.repeat(ssm_d[l].astype(F32), SSD_HEADDIM).reshape(1, SSD_WIDTH),
        ssm_nw=ssm_norm_w[l].reshape(1, SSD_WIDTH),
        scal=jnp.concatenate([lam.reshape(1), far_bias * LOG2E]).astype(F32), lam_init=lam_init,
        subw=attn_subln_w[l].reshape(DA_DV, 1), rel_bias=rel_bias,
        wo1=w_out[l][:SSD_WIDTH].astype(BF16), wo2=w_out[l][SSD_WIDTH:].astype(BF16),
        nfw=norm_ffn_w[l].reshape(1, D_MODEL),
        wu=w_up[l].astype(BF16), ffn_cw=ffn_conv_w[l], ffn_cb=ffn_conv_b[l].reshape(1, 2 * D_FF),
        wd=w_down[l].astype(BF16), fw=final_norm_w.reshape(1, D_MODEL),
    )


def _state_to_kernel(h):
    return h.reshape(h.shape[0], SSD_GROUPS, GROUP_W, SSD_STATE)


def _state_from_kernel(h):
    return h.reshape(h.shape[0], SSD_HEADS, SSD_HEADDIM, SSD_STATE)


def _run_group(x, mod, past_k, past_v, ssm_h0, ssm_conv_hist, ffn_conv_hist, p, *, tm, ssd_rows, bq, bk):
    b, t, d = x.shape
    past = 0 if past_k is None else past_k.shape[1]
    chunk = min(CHUNK, t)
    tp = max(t, SUPER)
    if tp != t:
        x = jnp.pad(x, ((0, 0), (0, tp - t), (0, 0)))
        tm = ssd_rows = bq = tp
    mod3 = mod.reshape(b, 6, d)

    assert tm == bq
    zs, xc, dt, k, v, kb, qt, vt, conv_new = _inproj_call(
        x, mod3, p["norm_mix_w"], p["w_cat"], p["w_t"], ssm_conv_hist.astype(F32), p["cw"], p["cbias"],
        tm=tm, real=min(t, tm))

    y_ssd, h_t = _ssd_call(zs, xc, dt, _state_to_kernel(ssm_h0.astype(F32)),
                           p["dtb"], p["alog"], p["dsk"], p["ssm_nw"],
                           chunk=chunk, rows=ssd_rows, real=min(t, ssd_rows))

    if past == 0:
        assert bq == bk and t % bq == 0
        y_att = _attn_call(p["scal"], qt, kb, vt, _bias_tables(p["rel_bias"], bq, bk), p["subw"], bq=bq, bk=bk,
                           out_scale=1.0 - p["lam_init"])
    else:
        assert past % bk == 0 and past >= bk and bq == tp <= bk
        bias0 = _bias_rows(p["rel_bias"], past, t, past - bk, bk, past + t)
        bias1 = _bias_rows(p["rel_bias"], past, t, past, bq, past + t)
        y_att = _attn_cached_call(p["scal"], qt, past_k, past_v, kb, vt, bias0, bias1, p["subw"].reshape(1, DA_DV),
                                  tq=t, bk=bk, out_scale=1.0 - p["lam_init"])

    tf = min(t, tm)
    nseq = math.gcd(b, max(1, TILE_ROWS // tf))
    y, ffn_new = _ffn_call(x[:, :t], y_ssd[:, :t], y_att[:, :t], mod3, p["wo1"], p["wo2"], p["nfw"], p["wu"],
                           p["ffn_cw"], p["ffn_cb"], p["wd"], p["fw"], ffn_conv_hist.astype(F32),
                           nseq=nseq, tm=tf)
    k = k.reshape(b, tp, DA_HEADS, 2 * DA_DK)
    v = v.reshape(b, tp, DA_HEADS, DA_DV)
    return (y, k[:, :t], v[:, :t],
            _state_from_kernel(h_t), conv_new, ffn_new)


def kernel(x_prompt, x_sample, c_prompt, c_sample, cache_k, cache_v, state_ssm, state_ssm_conv, state_ffn_conv, w_ada, b_ada, norm_mix_w, w_in, ssm_conv_w, ssm_conv_b, ssm_dt_bias, ssm_a_log, ssm_d, ssm_norm_w, lambda_q1, lambda_k1, lambda_q2, lambda_k2, attn_subln_w, rel_bias, w_out, norm_ffn_w, w_up, ffn_conv_w, ffn_conv_b, w_down, final_norm_w):
    bp, bs = x_prompt.shape[0], x_sample.shape[0]
    dt = x_prompt.dtype
    p = _pack_params(norm_mix_w, w_in, ssm_conv_w, ssm_conv_b, ssm_dt_bias, ssm_a_log, ssm_d, ssm_norm_w,
                     lambda_q1, lambda_k1, lambda_q2, lambda_k2, attn_subln_w, rel_bias, w_out,
                     norm_ffn_w, w_up, ffn_conv_w, ffn_conv_b, w_down, final_norm_w, 0)
    c_all = jnp.concatenate([c_prompt, c_sample], axis=0)
    npad = -c_all.shape[0] % SUBLANES
    c_all = jnp.pad(c_all, ((0, npad), (0, 0)))
    mod = _mod_call(c_all, w_ada[0], b_ada[0].reshape(1, -1))

    zeros = lambda *s: jnp.zeros(s, dt)
    out_p = _run_group(x_prompt, mod[:bp], None, None,
                       zeros(bp, SSD_HEADS, SSD_HEADDIM, SSD_STATE), zeros(bp, SSD_CONV - 1, SSD_CONV_DIM),
                       zeros(bp, FFN_CONV - 1, 2 * D_FF), p,
                       tm=TILE_ROWS, ssd_rows=SSD_ROWS, bq=ATT_BLOCK, bk=ATT_BLOCK)
    out_s = _run_group(x_sample, mod[bp:bp + bs], cache_k[0], cache_v[0], state_ssm[0], state_ssm_conv[0],
                       state_ffn_conv[0], p, tm=SUPER, ssd_rows=SUPER, bq=SUPER, bk=ATT_BLOCK)
    y_p, k_p, v_p, h_p, c_p, f_p = out_p
    y_s, k_s, v_s, h_s, c_s, f_s = out_s
    return (y_p, y_s, k_p[None], v_p[None], h_p[None], c_p[None], f_p[None],
            k_s[None], v_s[None], h_s[None], c_s[None], f_s[None])
```

```python
import functools
import math

import numpy as np
import jax
import jax.numpy as jnp
from jax import lax
from jax.experimental import pallas as pl
from jax.experimental.pallas import tpu as pltpu

F32 = jnp.float32
BF16 = jnp.bfloat16

D_MODEL = 1024
CHUNK = 64
SSD_WIDTH = 512
SSD_HEADDIM = 64
SSD_HEADS = 8
SSD_GROUPS = 2
SSD_HPG = 4
SSD_STATE = 128
SSD_CONV = 4
SSD_CONV_DIM = SSD_WIDTH + 2 * SSD_GROUPS * SSD_STATE
GROUP_W = SSD_HPG * SSD_HEADDIM
DA_WIDTH = 512
DA_DK = 64
DA_DV = 128
DA_HEADS = 4
REL_BUCKETS = 32
REL_MAX_DIST = 128
D_FF = 2816
FFN_CONV = 3
EPS = 1e-6
IN_SPLITS = (512, 1536, 1544, 2056, 2568)
LANES = 128
SUBLANES = 8
TILE_ROWS = 512
MOD_TN = 1024
SUPER = 128
SSD_ROWS = 256
SSD_NSEQ = 4
ATT_BLOCK = 512
ATT_KT = 64
FFN_CN = 256
FFN_NC = D_FF // FFN_CN
NEG = -1e30
VMEM_LIMIT = 56 * 1024 * 1024

PZ, PX, PDT, PK, PV, PEND = 0, 512, 1536, 1664, 2176, 2688
LOG2E = math.log2(math.e)


def _silu(x):
    return x / (1.0 + jnp.exp(-x))


def _softplus(x):
    return jnp.maximum(x, 0.0) + jnp.log1p(jnp.exp(-jnp.abs(x)))


def _split3(x):
    hi = x.astype(BF16)
    r1 = x - hi.astype(F32)
    mid = r1.astype(BF16)
    lo = (r1 - mid.astype(F32)).astype(BF16)
    return hi, mid, lo


def _rms(x, w):
    return x * lax.rsqrt(jnp.mean(x * x, axis=-1, keepdims=True) + EPS) * w


def _const_spec(shape):
    nd = len(shape)
    return pl.BlockSpec(shape, lambda *_: (0,) * nd)


def _mod_kernel(c_ref, w_ref, b_ref, o_ref):
    a = _silu(c_ref[...]).astype(BF16)
    o_ref[...] = jnp.dot(a, w_ref[...].astype(BF16), preferred_element_type=F32) + b_ref[...]


def _mod_call(c, w_ada, b_ada):
    n, d = c.shape
    nout = w_ada.shape[1]
    tn = MOD_TN
    return pl.pallas_call(
        _mod_kernel,
        grid=(nout // tn,),
        in_specs=[pl.BlockSpec((n, d), lambda j: (0, 0)),
                  pl.BlockSpec((d, tn), lambda j: (0, j)),
                  pl.BlockSpec((1, tn), lambda j: (0, j))],
        out_specs=pl.BlockSpec((n, tn), lambda j: (0, j)),
        out_shape=jax.ShapeDtypeStruct((n, nout), F32),
        name="mod",
    )(c, w_ada, b_ada)


def _inproj_kernel(x_ref, mod_ref, nw_ref, w_ref, wt_ref, hist_ref, cw_ref, cbias_ref,
                   zs_ref, xc_ref, dt_ref, k_ref, v_ref, kb_ref, qt_ref, vt_ref, cout_ref, cbuf, zbuf,
                   *, tm, real):
    t = pl.program_id(1)
    nconv = SSD_CONV - 1

    @pl.when(t == 0)
    def _init():
        cbuf[0:SUBLANES, :] = jnp.zeros((SUBLANES, SSD_CONV_DIM), F32)
        cbuf[SUBLANES - nconv:SUBLANES, :] = hist_ref[...]

    nslab = 4 if tm % (4 * LANES) == 0 and tm >= TILE_ROWS else 1
    hm = tm // nslab
    for sl in range(nslab):
        r0 = sl * hm
        rs = slice(r0, r0 + hm)
        h = _rms(x_ref[rs, :], nw_ref[...]) * (1.0 + mod_ref[1:2, :]) + mod_ref[0:1, :]
        hb = h.astype(BF16)

        def proj(a, b, hb=hb):
            return jnp.dot(hb, w_ref[:, a:b], preferred_element_type=F32)

        cbuf[SUBLANES + r0:SUBLANES + r0 + hm, :] = proj(PX, PDT)
        zbuf[rs, :] = proj(PZ, PX)
        dt_ref[rs, :] = proj(PDT, PK)
        k = proj(PK, PV)
        v = proj(PV, PEND)
        for hd in range(DA_HEADS):
            dst = pl.ds(r0 * DA_HEADS + hd, hm, stride=DA_HEADS)
            k_ref[dst, :] = k[:, hd * DA_DV:(hd + 1) * DA_DV]
            v_ref[dst, :] = v[:, hd * DA_DV:(hd + 1) * DA_DV]
        kb_ref[rs, :] = k.astype(BF16)
        qt_ref[:, rs] = lax.dot_general(wt_ref[...], hb, (((1,), (1,)), ((), ())),
                                        preferred_element_type=F32).astype(BF16)
        for hd in range(DA_HEADS):
            vt_ref[hd * DA_DV:(hd + 1) * DA_DV, rs] = v[:, hd * DA_DV:(hd + 1) * DA_DV].T.astype(BF16)
        conv = cbias_ref[...]
        for j in range(SSD_CONV):
            off = SUBLANES - nconv + j + r0
            conv = conv + cw_ref[j:j + 1, :] * cbuf[off:off + hm, :]
        xc_ref[rs, :] = _silu(conv).astype(BF16)
        zs_ref[rs, :] = _silu(zbuf[rs, :]).astype(BF16)
    tail = cbuf[SUBLANES - nconv + real:SUBLANES + real, :]
    cout_ref[...] = tail
    cbuf[SUBLANES - nconv:SUBLANES, :] = tail


def _inproj_call(x, mod3, norm_w, w_cat, w_t, hist, cw, cbias, *, tm, real):
    b, t, d = x.shape
    nt = t // tm

    def row(width):
        return pl.BlockSpec((None, tm, width), lambda i, j: (i, j, 0))

    def out(width, dtype):
        return jax.ShapeDtypeStruct((b, t, width), dtype)

    tspec = pl.BlockSpec((None, None, DA_WIDTH, tm), lambda i, j: (i, j, 0, 0))
    tshape = jax.ShapeDtypeStruct((b, nt, DA_WIDTH, tm), BF16)
    hist_spec = pl.BlockSpec((None, SSD_CONV - 1, SSD_CONV_DIM), lambda i, j: (i, 0, 0))
    hspec = pl.BlockSpec((None, tm * DA_HEADS, DA_DV), lambda i, j: (i, j, 0))
    hshape = jax.ShapeDtypeStruct((b, t * DA_HEADS, DA_DV), F32)
    return pl.pallas_call(
        functools.partial(_inproj_kernel, tm=tm, real=real),
        grid=(b, nt),
        in_specs=[row(d),
                  pl.BlockSpec((None, 6, d), lambda i, j: (i, 0, 0)),
                  _const_spec((1, d)),
                  _const_spec((d, PEND)),
                  _const_spec((DA_WIDTH, d)),
                  hist_spec, _const_spec((SSD_CONV, SSD_CONV_DIM)), _const_spec((1, SSD_CONV_DIM))],
        out_specs=[row(512), row(1024), row(LANES), hspec, hspec, row(512), tspec, tspec, hist_spec],
        out_shape=[out(512, BF16), out(1024, BF16), out(LANES, F32),
                   hshape, hshape, out(512, BF16), tshape, tshape,
                   jax.ShapeDtypeStruct((b, SSD_CONV - 1, SSD_CONV_DIM), F32)],
        scratch_shapes=[pltpu.VMEM((tm + SUBLANES, SSD_CONV_DIM), F32),
                        pltpu.VMEM((tm, SSD_WIDTH), F32)],
        compiler_params=pltpu.CompilerParams(dimension_semantics=("parallel", "arbitrary"),
                                             vmem_limit_bytes=VMEM_LIMIT),
        name="inproj",
    )(x, mod3, norm_w, w_cat, w_t, hist, cw, cbias)


def _ssd_kernel(zs_ref, xc_ref, dt_ref, h0_ref, dtb_ref, alog_ref, dsk_ref, nw_ref, tri_ref, e_ref,
                y_ref, hout_ref, h_scr, ybuf, *, nseq, chunk, rows, real):
    t = pl.program_id(1)
    seqs = range(nseq)

    @pl.when(t == 0)
    def _init():
        for i in seqs:
            for g in range(SSD_GROUPS):
                h_scr[i, g] = h0_ref[i, g].T

    li = lax.broadcasted_iota(jnp.int32, (SUPER, SUPER), 0)
    si = lax.broadcasted_iota(jnp.int32, (SUPER, SUPER), 1)
    cshift = chunk.bit_length() - 1
    mask2 = ((li >> cshift) == (si >> cshift)) & (si <= li)
    lane_g = lax.broadcasted_iota(jnp.int32, (SUPER, GROUP_W), 1) >> (SSD_HEADDIM.bit_length() - 1)

    pre = []
    for i in seqs:
        xs = xc_ref[i, :, 0:SSD_WIDTH].astype(F32)
        dtv = _softplus(dt_ref[i] + dtb_ref[...])
        da = dtv * (-jnp.exp(alog_ref[...]))
        acs = jnp.dot(tri_ref[...], jnp.concatenate(_split3(da), axis=0), preferred_element_type=F32)
        dt_x = jnp.dot(jnp.concatenate(_split3(dtv), axis=1), e_ref[...], preferred_element_type=F32)
        acs_x = jnp.dot(jnp.concatenate(_split3(acs), axis=1), e_ref[...], preferred_element_type=F32)
        pre.append((xs, acs, acs_x, jnp.exp(acs_x), xs * dt_x))

    for sb in range(rows // SUPER):
        o = sb * SUPER
        nreal = (min(real, o + SUPER) - o) // chunk
        acs2 = [pre[i][1][o:o + SUPER, :] for i in seqs]
        acs_t = [a.T for a in acs2]
        chains = [(g, i) for g in range(SSD_GROUPS) for i in seqs]

        def gsl(g):
            return slice(g * GROUP_W, (g + 1) * GROUP_W)

        cmb, cb2, bm_t = {}, {}, {}
        for g, i in chains:
            bcol = SSD_WIDTH + g * SSD_STATE
            ccol = SSD_WIDTH + (SSD_GROUPS + g) * SSD_STATE
            bmb = xc_ref[i, o:o + SUPER, bcol:bcol + SSD_STATE]
            cmb[g, i] = xc_ref[i, o:o + SUPER, ccol:ccol + SSD_STATE]
            cb2[g, i] = lax.dot_general(cmb[g, i], bmb, (((1,), (1,)), ((), ())), preferred_element_type=F32)
            bm_t[g, i] = bmb.astype(F32).T.astype(BF16)
        for g, i in chains:
            ms = []
            for rr in range(SSD_HPG):
                r = g * SSD_HPG + rr
                seg = acs2[i][:, r:r + 1] - acs_t[i][r:r + 1, :]
                dec = jnp.where(mask2, jnp.exp(jnp.where(mask2, seg, 0.0)), 0.0)
                ms.append((cb2[g, i] * dec).astype(BF16))
            full = jnp.dot(jnp.concatenate(ms, axis=0), pre[i][4][o:o + SUPER, gsl(g)].astype(BF16),
                           preferred_element_type=F32)
            ydiag = full[0:SUPER]
            for rr in range(1, SSD_HPG):
                ydiag = jnp.where(lane_g == rr, full[rr * SUPER:(rr + 1) * SUPER], ydiag)
            ybuf[i, o:o + SUPER, gsl(g)] = ydiag
        for j in range(nreal):
            a0, a1 = o + j * chunk, o + (j + 1) * chunk
            h_t = {c: h_scr[c[1], c[0]] for c in chains}
            yoff = {(g, i): jnp.dot(cmb[g, i][j * chunk:(j + 1) * chunk, :], h_t[g, i].astype(BF16),
                                    preferred_element_type=F32) for g, i in chains}
            st = {}
            for g, i in chains:
                _, _, acs_x, eacs_x, xd = pre[i]
                ybuf[i, a0:a1, gsl(g)] = ybuf[i, a0:a1, gsl(g)] + yoff[g, i] * eacs_x[a0:a1, gsl(g)]
                dte = jnp.exp(acs_x[a1 - 1:a1, gsl(g)] - acs_x[a0:a1, gsl(g)])
                xw = (xd[a0:a1, gsl(g)] * dte).astype(BF16)
                pieces = []
                if j > 0:
                    pieces.append(jnp.zeros((j * chunk, GROUP_W), BF16))
                pieces.append(xw)
                if (j + 1) * chunk < SUPER:
                    pieces.append(jnp.zeros((SUPER - (j + 1) * chunk, GROUP_W), BF16))
                xw2 = jnp.concatenate(pieces, axis=0) if len(pieces) > 1 else xw
                st[g, i] = jnp.dot(bm_t[g, i], xw2, preferred_element_type=F32)
            for g, i in chains:
                h_scr[i, g] = h_t[g, i] * pre[i][3][a1 - 1:a1, gsl(g)] + st[g, i]

    for i in seqs:
        y = (ybuf[i] + dsk_ref[...] * pre[i][0]) * zs_ref[i].astype(F32)
        for g in range(SSD_GROUPS):
            gs = slice(g * GROUP_W, (g + 1) * GROUP_W)
            y_ref[i, :, gs] = _rms(y[:, gs], nw_ref[:, gs]).astype(BF16)

    @pl.when(t == pl.num_programs(1) - 1)
    def _fin():
        for i in seqs:
            for g in range(SSD_GROUPS):
                hout_ref[i, g] = h_scr[i, g].T


def _ssd_call(zs, xc, dt, h0_t, dtb, alog, dsk, nw, *, chunk, rows, real):
    b, t, _ = zs.shape
    ii = np.arange(rows)
    tri = ((ii[:, None] // chunk == ii[None, :] // chunk) & (ii[None, :] <= ii[:, None])).astype(np.float32)
    e = np.zeros((LANES, SSD_WIDTH), np.float32)
    for r in range(SSD_HEADS):
        e[r, r * SSD_HEADDIM:(r + 1) * SSD_HEADDIM] = 1.0

    nseq = math.gcd(b, SSD_NSEQ)

    def row(width):
        return pl.BlockSpec((nseq, rows, width), lambda i, j: (i, j, 0))

    state_spec = pl.BlockSpec((nseq, SSD_GROUPS, GROUP_W, SSD_STATE), lambda i, j: (i, 0, 0, 0))
    kern = functools.partial(_ssd_kernel, nseq=nseq, chunk=chunk, rows=rows, real=real)
    return pl.pallas_call(
        kern,
        grid=(b // nseq, t // rows),
        in_specs=[row(SSD_WIDTH), row(SSD_CONV_DIM), row(LANES), state_spec,
                  _const_spec((1, LANES)), _const_spec((1, LANES)),
                  _const_spec((1, SSD_WIDTH)), _const_spec((1, SSD_WIDTH)),
                  _const_spec((rows, 3 * rows)), _const_spec((3 * LANES, SSD_WIDTH))],
        out_specs=[row(SSD_WIDTH), state_spec],
        out_shape=[jax.ShapeDtypeStruct((b, t, SSD_WIDTH), BF16),
                   jax.ShapeDtypeStruct((b, SSD_GROUPS, GROUP_W, SSD_STATE), F32)],
        scratch_shapes=[pltpu.VMEM((nseq, SSD_GROUPS, SSD_STATE, GROUP_W), F32),
                        pltpu.VMEM((nseq, rows, SSD_WIDTH), F32)],
        compiler_params=pltpu.CompilerParams(dimension_semantics=("parallel", "arbitrary"),
                                             vmem_limit_bytes=VMEM_LIMIT),
        name="ssd",
    )(zs, xc, dt, h0_t, dtb, alog, dsk, nw,
      jnp.asarray(np.tile(tri, (1, 3)), BF16), jnp.asarray(np.tile(e, (3, 1)), BF16))


def _attn_kernel(scal_ref, qt_ref, k_ref, vt_ref, btab_ref, subw_ref, o_ref,
                 m_scr, l_scr, acc_scr, sa_scr, sb_scr, sc_scr, pa_scr, pb_scr, alpha_scr, qz_scr, bias_scr,
                 *, bq, bk, out_scale):
    h = pl.program_id(0)
    qi = pl.program_id(2)
    kn0 = qi - 1
    lam = scal_ref[0]
    cfar = scal_ref[1 + h]

    @pl.when((pl.program_id(1) == 0) & (qi == 0))
    def _build_bias_tiles():
        width = bq + bk
        kj = lax.broadcasted_iota(jnp.int32, (bk, bq), 0)
        qj = lax.broadcasted_iota(jnp.int32, (bk, bq), 1)
        cshift = CHUNK.bit_length() - 1
        for d in range(2):
            skew = pltpu.roll(jnp.broadcast_to(btab_ref[d], (bk, width)), 0, 1, stride=1, stride_axis=0)
            tile = skew[:, bk:width]
            if d == 1:
                tile = jnp.where((kj >> cshift) <= (qj >> cshift), tile, NEG)
            bias_scr[d] = tile

    zero = jnp.zeros((DA_DK, bq), BF16)
    qz_scr[0, 0:DA_DK, :] = qt_ref[0:DA_DK, :]
    qz_scr[0, DA_DK:DA_DV, :] = zero
    qz_scr[1, 0:DA_DK, :] = zero
    qz_scr[1, DA_DK:DA_DV, :] = qt_ref[DA_DK:DA_DV, :]

    m_scr[...] = jnp.full(m_scr.shape, NEG, F32)
    l_scr[...] = jnp.zeros(l_scr.shape, F32)
    acc_scr[...] = jnp.zeros(acc_scr.shape, F32)

    nsub = bk // ATT_KT

    def fold(x):
        return x.reshape(ATT_KT // SUBLANES, SUBLANES, bq)

    def scores(s_buf, first, count):
        for mm in range(2):
            for e in range(count):
                start = pl.multiple_of((first + e) * bk, bk)
                s_buf[mm, e] = jnp.dot(k_ref[pl.ds(start, bk), :], qz_scr[mm], preferred_element_type=F32)

    def softmax(s_buf, p_buf, entries):
        alphas = []
        for mm in range(2):
            cand = None
            for e, (near, shift) in enumerate(entries):
                mx = None
                for t in range(nsub):
                    rows = slice(t * ATT_KT, (t + 1) * ATT_KT)
                    s = s_buf[mm, e, rows, :]
                    if near is not None:
                        s = s + bias_scr[near, rows, :]
                    pm = jnp.max(fold(s), axis=0)
                    mx = pm if mx is None else jnp.maximum(mx, pm)
                mx = jnp.max(mx, axis=0, keepdims=True) + shift
                cand = mx if cand is None else jnp.maximum(cand, mx)
            m_old = m_scr[mm]
            m_new = jnp.maximum(m_old, cand)
            ls = None
            for e, (near, shift) in enumerate(entries):
                off = m_new - shift
                for t in range(nsub):
                    rows = slice(t * ATT_KT, (t + 1) * ATT_KT)
                    s = s_buf[mm, e, rows, :]
                    if near is not None:
                        s = s + bias_scr[near, rows, :]
                    p = jnp.exp2(s - off)
                    p_buf[mm, e * bk + t * ATT_KT:e * bk + (t + 1) * ATT_KT, :] = p.astype(BF16)
                    ps = jnp.sum(fold(p), axis=0)
                    ls = ps if ls is None else ls + ps
            alpha = jnp.exp2(m_old - m_new)
            l_scr[mm] = alpha * l_scr[mm] + jnp.sum(ls, axis=0, keepdims=True)
            m_scr[mm] = m_new
            alphas.append(alpha)
        return alphas

    def pv(p_buf, first, n):
        vts = [vt_ref[first + e] for e in range(n)]
        vt = jnp.concatenate(vts, axis=1) if n > 1 else vts[0]
        return [jnp.dot(vt, p_buf[mm, 0:n * bk, :], preferred_element_type=F32) for mm in range(2)]

    def accumulate(alphas, pvs):
        for mm in range(2):
            acc_scr[mm] = alphas[mm] * acc_scr[mm] + pvs[mm]

    def softmax_pv(s_buf, first, entries):
        alphas = softmax(s_buf, pa_scr, entries)
        accumulate(alphas, pv(pa_scr, first, len(entries)))

    nfar = jnp.maximum(kn0, 0)
    odd = nfar % 2
    far = (None, cfar)

    near_pair = [(0, 0.0), (1, 0.0)]

    @pl.when(kn0 < 0)
    def _only_first():
        scores(sc_scr, 0, 1)
        softmax_pv(sc_scr, 0, [(1, 0.0)])

    @pl.when(kn0 >= 0)
    def _groups():
        @pl.when(odd == 1)
        def _single():
            scores(sc_scr, 0, 1)
            scores(sa_scr, 1, 2)
            softmax_pv(sc_scr, 0, [far])

        @pl.when(odd == 0)
        def _first_pair():
            scores(sa_scr, 0, 2)

        def far_pair(s_cur, s_next, cur):
            scores(s_next, cur + 2, 2)
            softmax_pv(s_cur, cur, [far, far])

        def far_step(s_cur, p_cur, s_next, p_prev, cur):
            scores(s_next, cur + 2, 2)
            pending = None if p_prev is None else pv(p_prev, cur - 2, 2)
            alphas = softmax(s_cur, p_cur, [far, far])
            if pending is not None:
                accumulate([alpha_scr[0], alpha_scr[1]], pending)
            for mm in range(2):
                alpha_scr[mm] = alphas[mm]

        npairs = nfar // 2
        niter = npairs // 2

        @pl.when(niter >= 1)
        def _far_loop():
            far_step(sa_scr, pa_scr, sb_scr, None, odd)
            far_step(sb_scr, pb_scr, sa_scr, pa_scr, odd + 2)

            def far_body(j, carry):
                cur = odd + 4 * j
                far_step(sa_scr, pa_scr, sb_scr, pb_scr, cur)
                far_step(sb_scr, pb_scr, sa_scr, pa_scr, cur + 2)
                return carry

            lax.fori_loop(1, niter, far_body, 0)
            accumulate([alpha_scr[0], alpha_scr[1]], pv(pb_scr, odd + 4 * niter - 2, 2))

        @pl.when(npairs % 2 == 1)
        def _tail_b():
            far_pair(sa_scr, sb_scr, kn0 - 2)
            softmax_pv(sb_scr, kn0, near_pair)

        @pl.when(npairs % 2 == 0)
        def _tail_a():
            softmax_pv(sa_scr, kn0, near_pair)

    o = acc_scr[0] * (1.0 / l_scr[0]) - acc_scr[1] * (lam / l_scr[1])
    o = o * lax.rsqrt(jnp.mean(o * o, axis=0, keepdims=True) + EPS) * (subw_ref[...] * out_scale)
    o_ref[...] = o.T.astype(BF16)


def _attn_call(scal, qt, kb, vt, btab, subw, *, bq, bk, out_scale):
    b, nq = qt.shape[:2]
    tk = kb.shape[1]
    nkb = vt.shape[1]
    kern = functools.partial(_attn_kernel, bq=bq, bk=bk, out_scale=out_scale)
    return pl.pallas_call(
        kern,
        grid=(DA_HEADS, b, nq),
        in_specs=[pl.BlockSpec(memory_space=pltpu.SMEM),
                  pl.BlockSpec((None, None, DA_DV, bq), lambda h, i, j: (i, j, h, 0)),
                  pl.BlockSpec((None, tk, DA_DV), lambda h, i, j: (i, 0, h)),
                  pl.BlockSpec((None, nkb, DA_DV, bk), lambda h, i, j: (i, 0, h, 0)),
                  pl.BlockSpec((None, 2, 1, bq + bk), lambda h, i, j: (h, 0, 0, 0)),
                  pl.BlockSpec((DA_DV, 1), lambda h, i, j: (0, 0))],
        out_specs=pl.BlockSpec((None, bq, DA_DV), lambda h, i, j: (i, j, h)),
        out_shape=jax.ShapeDtypeStruct((b, nq * bq, DA_WIDTH), BF16),
        scratch_shapes=[pltpu.VMEM((2, 1, bq), F32), pltpu.VMEM((2, 1, bq), F32),
                        pltpu.VMEM((2, DA_DV, bq), F32),
                        pltpu.VMEM((2, 2, bk, bq), F32), pltpu.VMEM((2, 2, bk, bq), F32),
                        pltpu.VMEM((2, 1, bk, bq), F32),
                        pltpu.VMEM((2, 2 * bk, bq), BF16), pltpu.VMEM((2, 2 * bk, bq), BF16),
                        pltpu.VMEM((2, 1, bq), F32),
                        pltpu.VMEM((2, DA_DV, bq), BF16),
                        pltpu.VMEM((2, bk, bq), F32)],
        compiler_params=pltpu.CompilerParams(dimension_semantics=("arbitrary", "arbitrary", "arbitrary"),
                                             vmem_limit_bytes=VMEM_LIMIT),
        name="attn",
    )(scal, qt, kb, vt, btab, subw)


def _attn_cached_kernel(scal_ref, qt_ref, kc_ref, vc_ref, kn_ref, vtn_ref, bias0_ref, bias1_ref, subw_ref, o_ref,
                        *, bq, tq, bk, out_scale):
    lam = scal_ref[0]
    past = kc_ref.shape[0] // DA_HEADS
    lane = lax.broadcasted_iota(jnp.int32, (tq, DA_DV), 1)
    nt = (((1,), (1,)), ((), ()))

    for h in range(DA_HEADS):
        hs = slice(h * DA_DV, (h + 1) * DA_DV)
        cfar = scal_ref[1 + h]
        qn = qt_ref[hs, :].astype(F32).T[0:tq, :]
        q2 = jnp.concatenate([jnp.where(lane < DA_DK, qn, 0.0), jnp.where(lane >= DA_DK, qn, 0.0)],
                             axis=0).astype(BF16)
        head_rows = pl.ds(h, past, stride=DA_HEADS)
        s_c = lax.dot_general(q2, kc_ref[head_rows, :].astype(BF16), nt, preferred_element_type=F32)
        s_n = lax.dot_general(q2, kn_ref[:, hs], nt, preferred_element_type=F32)
        b0 = bias0_ref[h]
        b1 = bias1_ref[h]
        s = jnp.concatenate([s_c[:, 0:past - bk] + cfar,
                             s_c[:, past - bk:past] + jnp.concatenate([b0, b0], axis=0),
                             s_n + jnp.concatenate([b1, b1], axis=0)], axis=1)
        p = jnp.exp2(s - jnp.max(s, axis=1, keepdims=True))
        inv = 1.0 / jnp.sum(p, axis=1, keepdims=True)
        pb = p.astype(BF16)
        acc = (jnp.dot(pb[:, 0:past], vc_ref[head_rows, :].astype(BF16), preferred_element_type=F32)
               + jnp.dot(pb[:, past:past + bq], vtn_ref[hs, :].astype(F32).T.astype(BF16),
                         preferred_element_type=F32))
        o = acc[0:tq, :] * inv[0:tq] - acc[tq:2 * tq, :] * (lam * inv[tq:2 * tq])
        o_ref[0:tq, hs] = (_rms(o, subw_ref[...]) * out_scale).astype(BF16)
        o_ref[tq:bq, hs] = jnp.zeros((bq - tq, DA_DV), BF16)


def _attn_cached_call(scal, qt, cache_k, cache_v, kb, vt, bias0, bias1, subw, *, tq, bk, out_scale):
    b, past = cache_k.shape[:2]
    bq = qt.shape[-1]
    rows = past * DA_HEADS
    kern = functools.partial(_attn_cached_kernel, bq=bq, tq=tq, bk=bk, out_scale=out_scale)
    cache_spec = pl.BlockSpec((None, rows, DA_DV), lambda i: (i, 0, 0))
    return pl.pallas_call(
        kern,
        grid=(b,),
        in_specs=[pl.BlockSpec(memory_space=pltpu.SMEM),
                  pl.BlockSpec((None, None, DA_WIDTH, bq), lambda i: (i, 0, 0, 0)),
                  cache_spec, cache_spec,
                  pl.BlockSpec((None, bq, DA_WIDTH), lambda i: (i, 0, 0)),
                  pl.BlockSpec((None, None, DA_WIDTH, bq), lambda i: (i, 0, 0, 0)),
                  _const_spec((DA_HEADS, tq, bk)), _const_spec((DA_HEADS, tq, bq)),
                  _const_spec((1, DA_DV))],
        out_specs=pl.BlockSpec((None, bq, DA_WIDTH), lambda i: (i, 0, 0)),
        out_shape=jax.ShapeDtypeStruct((b, bq, DA_WIDTH), BF16),
        compiler_params=pltpu.CompilerParams(dimension_semantics=("parallel",), vmem_limit_bytes=VMEM_LIMIT),
        name="attn_cached",
    )(scal, qt, cache_k.reshape(b, rows, DA_DV), cache_v.reshape(b, rows, DA_DV), kb, vt, bias0, bias1, subw)


def _rel_bucket(rel):
    nb = REL_BUCKETS // 2
    max_exact = nb // 2
    n = jnp.abs(rel)
    nf = jnp.maximum(n, 1).astype(jnp.float32)
    large = max_exact + (jnp.log(nf / max_exact) / math.log(REL_MAX_DIST / max_exact)
                         * (nb - max_exact)).astype(jnp.int32)
    large = jnp.minimum(large, nb - 1)
    return jnp.where(rel > 0, nb, 0) + jnp.where(n < max_exact, n, large)


def _bias_tables(rel_bias, bq, bk):
    tabs = []
    for d in range(2):
        offs = (d - 1) * bk + bk - np.arange(bq + bk)
        tabs.append(rel_bias[_rel_bucket(jnp.asarray(offs, jnp.int32))].astype(F32).T * LOG2E)
    return jnp.stack(tabs, axis=1)[:, :, None, :]


def _bias_rows(rel_bias, qpos0, tq, kpos0, nkeys, tk_real):
    offs = (kpos0 - qpos0) + np.arange(-(tq - 1), nkeys)
    table = rel_bias[_rel_bucket(jnp.asarray(offs, jnp.int32))].astype(F32).T * LOG2E
    rows = jnp.stack([table[:, tq - 1 - i:tq - 1 - i + nkeys] for i in range(tq)], axis=1)
    qpos = qpos0 + np.arange(tq)
    kpos = kpos0 + np.arange(nkeys)
    vis = (kpos[None, :] // CHUNK <= qpos[:, None] // CHUNK) & (kpos[None, :] < tk_real)
    return jnp.where(jnp.asarray(vis)[None], rows, NEG)


def _ffn_kernel(x_ref, ys_ref, ya_ref, mod_ref, wo1_ref, wo2_ref, nfw_ref, wu_ref, cw_ref, cb_ref, wd_ref, fw_ref,
                hist_ref, y_ref, tail_ref, tail_scr, buf_a, buf_b, x1_scr, h2_scr, act_scr, *, nseq, tm):
    t = pl.program_id(1)
    nh = FFN_CONV - 1
    lo = SUBLANES - nh
    seqs = range(nseq)

    def rows(i):
        return slice(i * tm, (i + 1) * tm)

    def stacked(ref):
        return jnp.concatenate([ref[i] for i in seqs], axis=0) if nseq > 1 else ref[0]

    @pl.when(t == 0)
    def _init():
        for i in seqs:
            tail_scr[i, lo:SUBLANES, :] = hist_ref[i]

    mix = (jnp.dot(stacked(ys_ref), wo1_ref[...], preferred_element_type=F32)
           + jnp.dot(stacked(ya_ref), wo2_ref[...], preferred_element_type=F32))
    for i in seqs:
        x1 = x_ref[i] + mod_ref[i, 2:3, :] * mix[rows(i)]
        x1_scr[rows(i), :] = x1
        h2 = _rms(x1, nfw_ref[...]) * (1.0 + mod_ref[i, 4:5, :]) + mod_ref[i, 3:4, :]
        h2_scr[rows(i), :] = h2.astype(BF16)

    def cols(j, half):
        return slice(half * D_FF + j * FFN_CN, half * D_FF + (j + 1) * FFN_CN)

    def up(j, buf):
        for half in range(2):
            u = jnp.dot(h2_scr[...], wu_ref[:, cols(j, half)], preferred_element_type=F32)
            for i in seqs:
                buf[half, i, SUBLANES:SUBLANES + tm, :] = u[rows(i)]

    def conv(j, buf, half, i):
        cs = cols(j, half)
        buf[half, i, lo:SUBLANES, :] = tail_scr[i, lo:SUBLANES, cs]
        c = cb_ref[:, cs]
        for k in range(FFN_CONV):
            c = c + cw_ref[k:k + 1, cs] * buf[half, i, lo + k:lo + k + tm, :]
        tail_scr[i, lo:SUBLANES, cs] = buf[half, i, lo + tm:SUBLANES + tm, :]
        return c

    bufs = (buf_a, buf_b)

    @pl.when(t >= 0)
    def _chunks():
        up(0, bufs[0])
        for j in range(FFN_NC):
            if j + 1 < FFN_NC:
                up(j + 1, bufs[(j + 1) % 2])
            for i in seqs:
                cv = conv(j, bufs[j % 2], 0, i)
                cg = conv(j, bufs[j % 2], 1, i)
                act_scr[rows(i), j * FFN_CN:(j + 1) * FFN_CN] = (_silu(cg) * cv).astype(BF16)

    f = jnp.dot(act_scr[...], wd_ref[...], preferred_element_type=F32)
    for i in seqs:
        x2 = x1_scr[rows(i), :] + mod_ref[i, 5:6, :] * f[rows(i)]
        y_ref[i] = _rms(x2, fw_ref[...])
        tail_ref[i] = tail_scr[i, lo:SUBLANES, :]


def _ffn_call(x, ys, ya, mod3, wo1, wo2, nfw, wu, cw, cb, wd, fw, hist, *, nseq, tm):
    b, t, d = x.shape
    nh = FFN_CONV - 1
    m = nseq * tm

    def row(width):
        return pl.BlockSpec((nseq, tm, width), lambda i, j: (i, j, 0))

    hist_spec = pl.BlockSpec((nseq, nh, 2 * D_FF), lambda i, j: (i, 0, 0))
    kern = functools.partial(_ffn_kernel, nseq=nseq, tm=tm)
    return pl.pallas_call(
        kern,
        grid=(b // nseq, t // tm),
        in_specs=[row(d), row(SSD_WIDTH), row(DA_WIDTH),
                  pl.BlockSpec((nseq, 6, d), lambda i, j: (i, 0, 0)),
                  _const_spec((SSD_WIDTH, d)), _const_spec((DA_WIDTH, d)), _const_spec((1, d)),
                  _const_spec((d, 2 * D_FF)), _const_spec((FFN_CONV, 2 * D_FF)), _const_spec((1, 2 * D_FF)),
                  _const_spec((D_FF, d)), _const_spec((1, d)),
                  hist_spec],
        out_specs=[row(d), hist_spec],
        out_shape=[jax.ShapeDtypeStruct((b, t, d), F32),
                   jax.ShapeDtypeStruct((b, nh, 2 * D_FF), F32)],
        scratch_shapes=[pltpu.VMEM((nseq, SUBLANES, 2 * D_FF), F32),
                        pltpu.VMEM((2, nseq, tm + SUBLANES, FFN_CN), F32),
                        pltpu.VMEM((2, nseq, tm + SUBLANES, FFN_CN), F32),
                        pltpu.VMEM((m, d), F32), pltpu.VMEM((m, d), BF16), pltpu.VMEM((m, D_FF), BF16)],
        compiler_params=pltpu.CompilerParams(dimension_semantics=("parallel", "arbitrary"),
                                             vmem_limit_bytes=VMEM_LIMIT),
        name="ffn",
    )(x, ys, ya, mod3, wo1, wo2, nfw, wu, cw, cb, wd, fw, hist)


def _pack_params(norm_mix_w, w_in, ssm_conv_w, ssm_conv_b, ssm_dt_bias, ssm_a_log, ssm_d, ssm_norm_w,
                 lambda_q1, lambda_k1, lambda_q2, lambda_k2, attn_subln_w, rel_bias, w_out,
                 norm_ffn_w, w_up, ffn_conv_w, ffn_conv_b, w_down, final_norm_w, layer):
    l = layer
    wz, wx, wdt, wq, wk, wv = jnp.split(w_in[l], IN_SPLITS, axis=-1)
    wdt = jnp.pad(wdt, ((0, 0), (0, LANES - SSD_HEADS)))
    w_cat = jnp.concatenate([wz, wx, wdt, wk, wv], axis=-1).astype(BF16)
    w_t = (wq * (DA_DK ** -0.5 * LOG2E)).T.astype(BF16)

    def pad_heads(v):
        return jnp.pad(v.astype(F32), (0, LANES - SSD_HEADS)).reshape(1, LANES)

    lam_init = 0.8 - 0.6 * math.exp(-0.3 * l)
    lam = (jnp.exp(jnp.sum(lambda_q1[l].astype(F32) * lambda_k1[l].astype(F32)))
           - jnp.exp(jnp.sum(lambda_q2[l].astype(F32) * lambda_k2[l].astype(F32))) + lam_init)
    far_bias = rel_bias[REL_BUCKETS // 2 - 1].astype(F32)
    return dict(
        norm_mix_w=norm_mix_w[l].reshape(1, D_MODEL), w_cat=w_cat, w_t=w_t,
        cw=ssm_conv_w[l], cbias=ssm_conv_b[l].reshape(1, SSD_CONV_DIM),
        dtb=pad_heads(ssm_dt_bias[l]), alog=pad_heads(ssm_a_log[l]),
        dsk=jnp.repeat(ssm_d[l].astype(F32), SSD_HEADDIM).reshape(1, SSD_WIDTH),
        ssm_nw=ssm_norm_w[l].reshape(1, SSD_WIDTH),
        scal=jnp.concatenate([lam.reshape(1), far_bias * LOG2E]).astype(F32), lam_init=lam_init,
        subw=attn_subln_w[l].reshape(DA_DV, 1), rel_bias=rel_bias,
        wo1=w_out[l][:SSD_WIDTH].astype(BF16), wo2=w_out[l][SSD_WIDTH:].astype(BF16),
        nfw=norm_ffn_w[l].reshape(1, D_MODEL),
        wu=w_up[l].astype(BF16), ffn_cw=ffn_conv_w[l], ffn_cb=ffn_conv_b[l].reshape(1, 2 * D_FF),
        wd=w_down[l].astype(BF16), fw=final_norm_w.reshape(1, D_MODEL),
    )


def _state_to_kernel(h):
    return h.reshape(h.shape[0], SSD_GROUPS, GROUP_W, SSD_STATE)


def _state_from_kernel(h):
    return h.reshape(h.shape[0], SSD_HEADS, SSD_HEADDIM, SSD_STATE)


def _run_group(x, mod, past_k, past_v, ssm_h0, ssm_conv_hist, ffn_conv_hist, p, *, tm, ssd_rows, bq, bk):
    b, t, d = x.shape
    past = 0 if past_k is None else past_k.shape[1]
    chunk = min(CHUNK, t)
    tp = max(t, SUPER)
    if tp != t:
        x = jnp.pad(x, ((0, 0), (0, tp - t), (0, 0)))
        tm = ssd_rows = bq = tp
    mod3 = mod.reshape(b, 6, d)

    assert tm == bq
    zs, xc, dt, k, v, kb, qt, vt, conv_new = _inproj_call(
        x, mod3, p["norm_mix_w"], p["w_cat"], p["w_t"], ssm_conv_hist.astype(F32), p["cw"], p["cbias"],
        tm=tm, real=min(t, tm))

    y_ssd, h_t = _ssd_call(zs, xc, dt, _state_to_kernel(ssm_h0.astype(F32)),
                           p["dtb"], p["alog"], p["dsk"], p["ssm_nw"],
                           chunk=chunk, rows=ssd_rows, real=min(t, ssd_rows))

    if past == 0:
        assert bq == bk and t % bq == 0
        y_att = _attn_call(p["scal"], qt, kb, vt, _bias_tables(p["rel_bias"], bq, bk), p["subw"], bq=bq, bk=bk,
                           out_scale=1.0 - p["lam_init"])
    else:
        assert past % bk == 0 and past >= bk and bq == tp <= bk
        bias0 = _bias_rows(p["rel_bias"], past, t, past - bk, bk, past + t)
        bias1 = _bias_rows(p["rel_bias"], past, t, past, bq, past + t)
        y_att = _attn_cached_call(p["scal"], qt, past_k, past_v, kb, vt, bias0, bias1, p["subw"].reshape(1, DA_DV),
                                  tq=t, bk=bk, out_scale=1.0 - p["lam_init"])

    tf = min(t, tm)
    nseq = math.gcd(b, max(1, TILE_ROWS // tf))
    y, ffn_new = _ffn_call(x[:, :t], y_ssd[:, :t], y_att[:, :t], mod3, p["wo1"], p["wo2"], p["nfw"], p["wu"],
                           p["ffn_cw"], p["ffn_cb"], p["wd"], p["fw"], ffn_conv_hist.astype(F32),
                           nseq=nseq, tm=tf)
    k = k.reshape(b, tp, DA_HEADS, 2 * DA_DK)
    v = v.reshape(b, tp, DA_HEADS, DA_DV)
    return (y, k[:, :t], v[:, :t],
            _state_from_kernel(h_t), conv_new, ffn_new)


def kernel(x_prompt, x_sample, c_prompt, c_sample, cache_k, cache_v, state_ssm, state_ssm_conv, state_ffn_conv, w_ada, b_ada, norm_mix_w, w_in, ssm_conv_w, ssm_conv_b, ssm_dt_bias, ssm_a_log, ssm_d, ssm_norm_w, lambda_q1, lambda_k1, lambda_q2, lambda_k2, attn_subln_w, rel_bias, w_out, norm_ffn_w, w_up, ffn_conv_w, ffn_conv_b, w_down, final_norm_w):
    bp, bs = x_prompt.shape[0], x_sample.shape[0]
    dt = x_prompt.dtype
    p = _pack_params(norm_mix_w, w_in, ssm_conv_w, ssm_conv_b, ssm_dt_bias, ssm_a_log, ssm_d, ssm_norm_w,
                     lambda_q1, lambda_k1, lambda_q2, lambda_k2, attn_subln_w, rel_bias, w_out,
                     norm_ffn_w, w_up, ffn_conv_w, ffn_conv_b, w_down, final_norm_w, 0)
    c_all = jnp.concatenate([c_prompt, c_sample], axis=0)
    npad = -c_all.shape[0] % SUBLANES
    c_all = jnp.pad(c_all, ((0, npad), (0, 0)))
    mod = _mod_call(c_all, w_ada[0], b_ada[0].reshape(1, -1))

    zeros = lambda *s: jnp.zeros(s, dt)
    out_p = _run_group(x_prompt, mod[:bp], None, None,
                       zeros(bp, SSD_HEADS, SSD_HEADDIM, SSD_STATE), zeros(bp, SSD_CONV - 1, SSD_CONV_DIM),
                       zeros(bp, FFN_CONV - 1, 2 * D_FF), p,
                       tm=TILE_ROWS, ssd_rows=SSD_ROWS, bq=ATT_BLOCK, bk=ATT_BLOCK)
    out_s = _run_group(x_sample, mod[bp:bp + bs], cache_k[0], cache_v[0], state_ssm[0], state_ssm_conv[0],
                       state_ffn_conv[0], p, tm=SUPER, ssd_rows=SUPER, bq=SUPER, bk=ATT_BLOCK)
    y_p, k_p, v_p, h_p, c_p, f_p = out_p
    y_s, k_s, v_s, h_s, c_s, f_s = out_s
    return (y_p, y_s, k_p[None], v_p[None], h_p[None], c_p[None], f_p[None],
            k_s[None], v_s[None], h_s[None], c_s[None], f_s[None])
```

```python
import functools
import math

import numpy as np
import jax
import jax.numpy as jnp
from jax import lax
from jax.experimental import pallas as pl
from jax.experimental.pallas import tpu as pltpu

F32 = jnp.float32
BF16 = jnp.bfloat16

D_MODEL = 1024
CHUNK = 64
SSD_WIDTH = 512
SSD_HEADDIM = 64
SSD_HEADS = 8
SSD_GROUPS = 2
SSD_HPG = 4
SSD_STATE = 128
SSD_CONV = 4
SSD_CONV_DIM = SSD_WIDTH + 2 * SSD_GROUPS * SSD_STATE
GROUP_W = SSD_HPG * SSD_HEADDIM
DA_WIDTH = 512
DA_DK = 64
DA_DV = 128
DA_HEADS = 4
REL_BUCKETS = 32
REL_MAX_DIST = 128
D_FF = 2816
FFN_CONV = 3
EPS = 1e-6
IN_SPLITS = (512, 1536, 1544, 2056, 2568)
LANES = 128
SUBLANES = 8
TILE_ROWS = 512
MOD_TN = 1024
SUPER = 128
SSD_ROWS = 256
SSD_NSEQ = 4
ATT_BLOCK = 512
ATT_KT = 64
FFN_CN = 256
FFN_NC = D_FF // FFN_CN
NEG = -1e30
VMEM_LIMIT = 56 * 1024 * 1024

PZ, PX, PDT, PK, PV, PEND = 0, 512, 1536, 1664, 2176, 2688
LOG2E = math.log2(math.e)


def _silu(x):
    return x / (1.0 + jnp.exp(-x))


def _softplus(x):
    return jnp.maximum(x, 0.0) + jnp.log1p(jnp.exp(-jnp.abs(x)))


def _split3(x):
    hi = x.astype(BF16)
    r1 = x - hi.astype(F32)
    mid = r1.astype(BF16)
    lo = (r1 - mid.astype(F32)).astype(BF16)
    return hi, mid, lo


def _rms(x, w):
    return x * lax.rsqrt(jnp.mean(x * x, axis=-1, keepdims=True) + EPS) * w


def _const_spec(shape):
    nd = len(shape)
    return pl.BlockSpec(shape, lambda *_: (0,) * nd)


def _mod_kernel(c_ref, w_ref, b_ref, o_ref):
    a = _silu(c_ref[...]).astype(BF16)
    o_ref[...] = jnp.dot(a, w_ref[...].astype(BF16), preferred_element_type=F32) + b_ref[...]


def _mod_call(c, w_ada, b_ada):
    n, d = c.shape
    nout = w_ada.shape[1]
    tn = MOD_TN
    return pl.pallas_call(
        _mod_kernel,
        grid=(nout // tn,),
        in_specs=[pl.BlockSpec((n, d), lambda j: (0, 0)),
                  pl.BlockSpec((d, tn), lambda j: (0, j)),
                  pl.BlockSpec((1, tn), lambda j: (0, j))],
        out_specs=pl.BlockSpec((n, tn), lambda j: (0, j)),
        out_shape=jax.ShapeDtypeStruct((n, nout), F32),
        name="mod",
    )(c, w_ada, b_ada)


def _inproj_kernel(x_ref, mod_ref, nw_ref, w_ref, wt_ref, hist_ref, cw_ref, cbias_ref,
                   zs_ref, xc_ref, dt_ref, k_ref, v_ref, kb_ref, qt_ref, vt_ref, cout_ref, cbuf, hb_scr, zbuf,
                   *, tm, real):
    t = pl.program_id(1)
    nconv = SSD_CONV - 1

    @pl.when(t == 0)
    def _init():
        cbuf[0:SUBLANES, :] = jnp.zeros((SUBLANES, SSD_CONV_DIM), F32)
        cbuf[SUBLANES - nconv:SUBLANES, :] = hist_ref[...]

    h = _rms(x_ref[...], nw_ref[...]) * (1.0 + mod_ref[1:2, :]) + mod_ref[0:1, :]
    hb_scr[...] = h.astype(BF16)

    def proj(a, b):
        return jnp.dot(hb_scr[...], w_ref[:, a:b], preferred_element_type=F32)

    def proj_t(a, b):
        return lax.dot_general(wt_ref[a:b, :], hb_scr[...], (((1,), (1,)), ((), ())), preferred_element_type=F32)

    cbuf[SUBLANES:SUBLANES + tm, :] = proj(PX, PDT)
    zbuf[...] = proj(PZ, PX)
    dt_ref[...] = proj(PDT, PK)
    k = proj(PK, PV)
    v = proj(PV, PEND)
    for hd in range(DA_HEADS):
        dst = pl.ds(hd, tm, stride=DA_HEADS)
        k_ref[dst, :] = k[:, hd * DA_DV:(hd + 1) * DA_DV]
        v_ref[dst, :] = v[:, hd * DA_DV:(hd + 1) * DA_DV]
    kb_ref[...] = k.astype(BF16)
    qt_ref[...] = proj_t(0, DA_WIDTH).astype(BF16)
    for hd in range(DA_HEADS):
        vt_ref[hd * DA_DV:(hd + 1) * DA_DV, :] = v[:, hd * DA_DV:(hd + 1) * DA_DV].T.astype(BF16)
    conv = cbias_ref[...]
    for j in range(SSD_CONV):
        off = SUBLANES - nconv + j
        conv = conv + cw_ref[j:j + 1, :] * cbuf[off:off + tm, :]
    tail = cbuf[SUBLANES - nconv + real:SUBLANES + real, :]
    cout_ref[...] = tail
    cbuf[SUBLANES - nconv:SUBLANES, :] = tail
    xc_ref[...] = _silu(conv).astype(BF16)
    zs_ref[...] = _silu(zbuf[...]).astype(BF16)


def _inproj_call(x, mod3, norm_w, w_cat, w_t, hist, cw, cbias, *, tm, real):
    b, t, d = x.shape
    nt = t // tm

    def row(width):
        return pl.BlockSpec((None, tm, width), lambda i, j: (i, j, 0))

    def out(width, dtype):
        return jax.ShapeDtypeStruct((b, t, width), dtype)

    tspec = pl.BlockSpec((None, None, DA_WIDTH, tm), lambda i, j: (i, j, 0, 0))
    tshape = jax.ShapeDtypeStruct((b, nt, DA_WIDTH, tm), BF16)
    hist_spec = pl.BlockSpec((None, SSD_CONV - 1, SSD_CONV_DIM), lambda i, j: (i, 0, 0))
    hspec = pl.BlockSpec((None, tm * DA_HEADS, DA_DV), lambda i, j: (i, j, 0))
    hshape = jax.ShapeDtypeStruct((b, t * DA_HEADS, DA_DV), F32)
    return pl.pallas_call(
        functools.partial(_inproj_kernel, tm=tm, real=real),
        grid=(b, nt),
        in_specs=[row(d),
                  pl.BlockSpec((None, 6, d), lambda i, j: (i, 0, 0)),
                  _const_spec((1, d)),
                  _const_spec((d, PEND)),
                  _const_spec((DA_WIDTH, d)),
                  hist_spec, _const_spec((SSD_CONV, SSD_CONV_DIM)), _const_spec((1, SSD_CONV_DIM))],
        out_specs=[row(512), row(1024), row(LANES), hspec, hspec, row(512), tspec, tspec, hist_spec],
        out_shape=[out(512, BF16), out(1024, BF16), out(LANES, F32),
                   hshape, hshape, out(512, BF16), tshape, tshape,
                   jax.ShapeDtypeStruct((b, SSD_CONV - 1, SSD_CONV_DIM), F32)],
        scratch_shapes=[pltpu.VMEM((tm + SUBLANES, SSD_CONV_DIM), F32), pltpu.VMEM((tm, d), BF16),
                        pltpu.VMEM((tm, SSD_WIDTH), F32)],
        compiler_params=pltpu.CompilerParams(dimension_semantics=("parallel", "arbitrary"),
                                             vmem_limit_bytes=VMEM_LIMIT),
        name="inproj",
    )(x, mod3, norm_w, w_cat, w_t, hist, cw, cbias)


def _ssd_kernel(zs_ref, xc_ref, dt_ref, h0_ref, dtb_ref, alog_ref, dsk_ref, nw_ref, tri_ref, e_ref,
                y_ref, hout_ref, h_scr, ybuf, *, nseq, chunk, rows, real):
    t = pl.program_id(1)
    seqs = range(nseq)

    @pl.when(t == 0)
    def _init():
        for i in seqs:
            for g in range(SSD_GROUPS):
                h_scr[i, g] = h0_ref[i, g].T

    li = lax.broadcasted_iota(jnp.int32, (SUPER, SUPER), 0)
    si = lax.broadcasted_iota(jnp.int32, (SUPER, SUPER), 1)
    cshift = chunk.bit_length() - 1
    mask2 = ((li >> cshift) == (si >> cshift)) & (si <= li)
    lane_g = lax.broadcasted_iota(jnp.int32, (SUPER, GROUP_W), 1) >> (SSD_HEADDIM.bit_length() - 1)

    pre = []
    for i in seqs:
        xs = xc_ref[i, :, 0:SSD_WIDTH].astype(F32)
        dtv = _softplus(dt_ref[i] + dtb_ref[...])
        da = dtv * (-jnp.exp(alog_ref[...]))
        acs = jnp.dot(tri_ref[...], jnp.concatenate(_split3(da), axis=0), preferred_element_type=F32)
        dt_x = jnp.dot(jnp.concatenate(_split3(dtv), axis=1), e_ref[...], preferred_element_type=F32)
        acs_x = jnp.dot(jnp.concatenate(_split3(acs), axis=1), e_ref[...], preferred_element_type=F32)
        pre.append((xs, acs, acs_x, jnp.exp(acs_x), xs * dt_x))

    for sb in range(rows // SUPER):
        o = sb * SUPER
        nreal = (min(real, o + SUPER) - o) // chunk
        acs2 = [pre[i][1][o:o + SUPER, :] for i in seqs]
        acs_t = [a.T for a in acs2]
        chains = [(g, i) for g in range(SSD_GROUPS) for i in seqs]

        def gsl(g):
            return slice(g * GROUP_W, (g + 1) * GROUP_W)

        cmb, cb2, bm_t = {}, {}, {}
        for g, i in chains:
            bcol = SSD_WIDTH + g * SSD_STATE
            ccol = SSD_WIDTH + (SSD_GROUPS + g) * SSD_STATE
            bmb = xc_ref[i, o:o + SUPER, bcol:bcol + SSD_STATE]
            cmb[g, i] = xc_ref[i, o:o + SUPER, ccol:ccol + SSD_STATE]
            cb2[g, i] = lax.dot_general(cmb[g, i], bmb, (((1,), (1,)), ((), ())), preferred_element_type=F32)
            bm_t[g, i] = bmb.astype(F32).T.astype(BF16)
        for g, i in chains:
            ms = []
            for rr in range(SSD_HPG):
                r = g * SSD_HPG + rr
                seg = acs2[i][:, r:r + 1] - acs_t[i][r:r + 1, :]
                dec = jnp.where(mask2, jnp.exp(jnp.where(mask2, seg, 0.0)), 0.0)
                ms.append((cb2[g, i] * dec).astype(BF16))
            full = jnp.dot(jnp.concatenate(ms, axis=0), pre[i][4][o:o + SUPER, gsl(g)].astype(BF16),
                           preferred_element_type=F32)
            ydiag = full[0:SUPER]
            for rr in range(1, SSD_HPG):
                ydiag = jnp.where(lane_g == rr, full[rr * SUPER:(rr + 1) * SUPER], ydiag)
            ybuf[i, o:o + SUPER, gsl(g)] = ydiag
        for j in range(nreal):
            a0, a1 = o + j * chunk, o + (j + 1) * chunk
            h_t = {c: h_scr[c[1], c[0]] for c in chains}
            yoff = {(g, i): jnp.dot(cmb[g, i][j * chunk:(j + 1) * chunk, :], h_t[g, i].astype(BF16),
                                    preferred_element_type=F32) for g, i in chains}
            st = {}
            for g, i in chains:
                _, _, acs_x, eacs_x, xd = pre[i]
                ybuf[i, a0:a1, gsl(g)] = ybuf[i, a0:a1, gsl(g)] + yoff[g, i] * eacs_x[a0:a1, gsl(g)]
                dte = jnp.exp(acs_x[a1 - 1:a1, gsl(g)] - acs_x[a0:a1, gsl(g)])
                xw = (xd[a0:a1, gsl(g)] * dte).astype(BF16)
                pieces = []
                if j > 0:
                    pieces.append(jnp.zeros((j * chunk, GROUP_W), BF16))
                pieces.append(xw)
                if (j + 1) * chunk < SUPER:
                    pieces.append(jnp.zeros((SUPER - (j + 1) * chunk, GROUP_W), BF16))
                xw2 = jnp.concatenate(pieces, axis=0) if len(pieces) > 1 else xw
                st[g, i] = jnp.dot(bm_t[g, i], xw2, preferred_element_type=F32)
            for g, i in chains:
                h_scr[i, g] = h_t[g, i] * pre[i][3][a1 - 1:a1, gsl(g)] + st[g, i]

    for i in seqs:
        y = (ybuf[i] + dsk_ref[...] * pre[i][0]) * zs_ref[i].astype(F32)
        for g in range(SSD_GROUPS):
            gs = slice(g * GROUP_W, (g + 1) * GROUP_W)
            y_ref[i, :, gs] = _rms(y[:, gs], nw_ref[:, gs]).astype(BF16)

    @pl.when(t == pl.num_programs(1) - 1)
    def _fin():
        for i in seqs:
            for g in range(SSD_GROUPS):
                hout_ref[i, g] = h_scr[i, g].T


def _ssd_call(zs, xc, dt, h0_t, dtb, alog, dsk, nw, *, chunk, rows, real):
    b, t, _ = zs.shape
    ii = np.arange(rows)
    tri = ((ii[:, None] // chunk == ii[None, :] // chunk) & (ii[None, :] <= ii[:, None])).astype(np.float32)
    e = np.zeros((LANES, SSD_WIDTH), np.float32)
    for r in range(SSD_HEADS):
        e[r, r * SSD_HEADDIM:(r + 1) * SSD_HEADDIM] = 1.0

    nseq = math.gcd(b, SSD_NSEQ)

    def row(width):
        return pl.BlockSpec((nseq, rows, width), lambda i, j: (i, j, 0))

    state_spec = pl.BlockSpec((nseq, SSD_GROUPS, GROUP_W, SSD_STATE), lambda i, j: (i, 0, 0, 0))
    kern = functools.partial(_ssd_kernel, nseq=nseq, chunk=chunk, rows=rows, real=real)
    return pl.pallas_call(
        kern,
        grid=(b // nseq, t // rows),
        in_specs=[row(SSD_WIDTH), row(SSD_CONV_DIM), row(LANES), state_spec,
                  _const_spec((1, LANES)), _const_spec((1, LANES)),
                  _const_spec((1, SSD_WIDTH)), _const_spec((1, SSD_WIDTH)),
                  _const_spec((rows, 3 * rows)), _const_spec((3 * LANES, SSD_WIDTH))],
        out_specs=[row(SSD_WIDTH), state_spec],
        out_shape=[jax.ShapeDtypeStruct((b, t, SSD_WIDTH), BF16),
                   jax.ShapeDtypeStruct((b, SSD_GROUPS, GROUP_W, SSD_STATE), F32)],
        scratch_shapes=[pltpu.VMEM((nseq, SSD_GROUPS, SSD_STATE, GROUP_W), F32),
                        pltpu.VMEM((nseq, rows, SSD_WIDTH), F32)],
        compiler_params=pltpu.CompilerParams(dimension_semantics=("parallel", "arbitrary"),
                                             vmem_limit_bytes=VMEM_LIMIT),
        name="ssd",
    )(zs, xc, dt, h0_t, dtb, alog, dsk, nw,
      jnp.asarray(np.tile(tri, (1, 3)), BF16), jnp.asarray(np.tile(e, (3, 1)), BF16))


def _attn_kernel(scal_ref, qt_ref, k_ref, vt_ref, btab_ref, subw_ref, o_ref,
                 m_scr, l_scr, acc_scr, sa_scr, sb_scr, sc_scr, pa_scr, pb_scr, alpha_scr, qz_scr, bias_scr,
                 *, bq, bk, out_scale):
    h = pl.program_id(0)
    qi = pl.program_id(2)
    kn0 = qi - 1
    lam = scal_ref[0]
    cfar = scal_ref[1 + h]

    @pl.when((pl.program_id(1) == 0) & (qi == 0))
    def _build_bias_tiles():
        width = bq + bk
        kj = lax.broadcasted_iota(jnp.int32, (bk, bq), 0)
        qj = lax.broadcasted_iota(jnp.int32, (bk, bq), 1)
        cshift = CHUNK.bit_length() - 1
        for d in range(2):
            skew = pltpu.roll(jnp.broadcast_to(btab_ref[d], (bk, width)), 0, 1, stride=1, stride_axis=0)
            tile = skew[:, bk:width]
            if d == 1:
                tile = jnp.where((kj >> cshift) <= (qj >> cshift), tile, NEG)
            bias_scr[d] = tile

    zero = jnp.zeros((DA_DK, bq), BF16)
    qz_scr[0, 0:DA_DK, :] = qt_ref[0:DA_DK, :]
    qz_scr[0, DA_DK:DA_DV, :] = zero
    qz_scr[1, 0:DA_DK, :] = zero
    qz_scr[1, DA_DK:DA_DV, :] = qt_ref[DA_DK:DA_DV, :]

    m_scr[...] = jnp.full(m_scr.shape, NEG, F32)
    l_scr[...] = jnp.zeros(l_scr.shape, F32)
    acc_scr[...] = jnp.zeros(acc_scr.shape, F32)

    nsub = bk // ATT_KT

    def fold(x):
        return x.reshape(ATT_KT // SUBLANES, SUBLANES, x.shape[1])

    def scores(s_buf, first, count):
        for mm in range(2):
            for e in range(count):
                start = pl.multiple_of((first + e) * bk, bk)
                s_buf[mm, e] = jnp.dot(k_ref[pl.ds(start, bk), :], qz_scr[mm], preferred_element_type=F32)

    def first_visible(near, t):
        if near != 1:
            return 0
        return (t * ATT_KT // CHUNK) * CHUNK // LANES * LANES

    def merge(acc, part, c0, op):
        if acc is None:
            return part
        if c0 == 0:
            return op(acc, part)
        return jnp.concatenate([acc[:, :c0], op(acc[:, c0:], part)], axis=1)

    def softmax(s_buf, p_buf, entries):
        alphas = []
        for mm in range(2):
            cand = None
            for e, (near, shift) in enumerate(entries):
                mx = None
                for t in range(nsub):
                    rows = slice(t * ATT_KT, (t + 1) * ATT_KT)
                    c0 = first_visible(near, t)
                    s = s_buf[mm, e, rows, c0:]
                    if near is not None:
                        s = s + bias_scr[near, rows, c0:]
                    mx = merge(mx, jnp.max(fold(s), axis=0), c0, jnp.maximum)
                mx = jnp.max(mx, axis=0, keepdims=True) + shift
                cand = mx if cand is None else jnp.maximum(cand, mx)
            m_old = m_scr[mm]
            m_new = jnp.maximum(m_old, cand)
            ls = None
            for e, (near, shift) in enumerate(entries):
                off = m_new - shift
                for t in range(nsub):
                    rows = slice(t * ATT_KT, (t + 1) * ATT_KT)
                    prow = slice(e * bk + t * ATT_KT, e * bk + (t + 1) * ATT_KT)
                    c0 = first_visible(near, t)
                    s = s_buf[mm, e, rows, c0:]
                    if near is not None:
                        s = s + bias_scr[near, rows, c0:]
                    p = jnp.exp2(s - off[:, c0:])
                    if c0:
                        p_buf[mm, prow, 0:c0] = jnp.zeros((ATT_KT, c0), BF16)
                    p_buf[mm, prow, c0:] = p.astype(BF16)
                    ls = merge(ls, jnp.sum(fold(p), axis=0), c0, jnp.add)
            alpha = jnp.exp2(m_old - m_new)
            l_scr[mm] = alpha * l_scr[mm] + jnp.sum(ls, axis=0, keepdims=True)
            m_scr[mm] = m_new
            alphas.append(alpha)
        return alphas

    def pv(p_buf, first, n):
        vts = [vt_ref[first + e] for e in range(n)]
        vt = jnp.concatenate(vts, axis=1) if n > 1 else vts[0]
        return [jnp.dot(vt, p_buf[mm, 0:n * bk, :], preferred_element_type=F32) for mm in range(2)]

    def accumulate(alphas, pvs):
        for mm in range(2):
            acc_scr[mm] = alphas[mm] * acc_scr[mm] + pvs[mm]

    def softmax_pv(s_buf, first, entries):
        alphas = softmax(s_buf, pa_scr, entries)
        accumulate(alphas, pv(pa_scr, first, len(entries)))

    nfar = jnp.maximum(kn0, 0)
    odd = nfar % 2
    far = (None, cfar)

    near_pair = [(0, 0.0), (1, 0.0)]

    @pl.when(kn0 < 0)
    def _only_first():
        scores(sc_scr, 0, 1)
        softmax_pv(sc_scr, 0, [(1, 0.0)])

    @pl.when(kn0 >= 0)
    def _groups():
        @pl.when(odd == 1)
        def _single():
            scores(sc_scr, 0, 1)
            scores(sa_scr, 1, 2)
            softmax_pv(sc_scr, 0, [far])

        @pl.when(odd == 0)
        def _first_pair():
            scores(sa_scr, 0, 2)

        def far_pair(s_cur, s_next, cur):
            scores(s_next, cur + 2, 2)
            softmax_pv(s_cur, cur, [far, far])

        def far_step(s_cur, p_cur, s_next, p_prev, cur):
            scores(s_next, cur + 2, 2)
            pending = None if p_prev is None else pv(p_prev, cur - 2, 2)
            alphas = softmax(s_cur, p_cur, [far, far])
            if pending is not None:
                accumulate([alpha_scr[0], alpha_scr[1]], pending)
            for mm in range(2):
                alpha_scr[mm] = alphas[mm]

        npairs = nfar // 2
        niter = npairs // 2

        @pl.when(niter >= 1)
        def _far_loop():
            far_step(sa_scr, pa_scr, sb_scr, None, odd)
            far_step(sb_scr, pb_scr, sa_scr, pa_scr, odd + 2)

            def far_body(j, carry):
                cur = odd + 4 * j
                far_step(sa_scr, pa_scr, sb_scr, pb_scr, cur)
                far_step(sb_scr, pb_scr, sa_scr, pa_scr, cur + 2)
                return carry

            lax.fori_loop(1, niter, far_body, 0)
            accumulate([alpha_scr[0], alpha_scr[1]], pv(pb_scr, odd + 4 * niter - 2, 2))

        @pl.when(npairs % 2 == 1)
        def _tail_b():
            far_pair(sa_scr, sb_scr, kn0 - 2)
            softmax_pv(sb_scr, kn0, near_pair)

        @pl.when(npairs % 2 == 0)
        def _tail_a():
            softmax_pv(sa_scr, kn0, near_pair)

    o = acc_scr[0] * (1.0 / l_scr[0]) - acc_scr[1] * (lam / l_scr[1])
    o = o * lax.rsqrt(jnp.mean(o * o, axis=0, keepdims=True) + EPS) * (subw_ref[...] * out_scale)
    o_ref[...] = o.T.astype(BF16)


def _attn_call(scal, qt, kb, vt, btab, subw, *, bq, bk, out_scale):
    b, nq = qt.shape[:2]
    tk = kb.shape[1]
    nkb = vt.shape[1]
    kern = functools.partial(_attn_kernel, bq=bq, bk=bk, out_scale=out_scale)
    return pl.pallas_call(
        kern,
        grid=(DA_HEADS, b, nq),
        in_specs=[pl.BlockSpec(memory_space=pltpu.SMEM),
                  pl.BlockSpec((None, None, DA_DV, bq), lambda h, i, j: (i, j, h, 0)),
                  pl.BlockSpec((None, tk, DA_DV), lambda h, i, j: (i, 0, h)),
                  pl.BlockSpec((None, nkb, DA_DV, bk), lambda h, i, j: (i, 0, h, 0)),
                  pl.BlockSpec((None, 2, 1, bq + bk), lambda h, i, j: (h, 0, 0, 0)),
                  pl.BlockSpec((DA_DV, 1), lambda h, i, j: (0, 0))],
        out_specs=pl.BlockSpec((None, bq, DA_DV), lambda h, i, j: (i, j, h)),
        out_shape=jax.ShapeDtypeStruct((b, nq * bq, DA_WIDTH), BF16),
        scratch_shapes=[pltpu.VMEM((2, 1, bq), F32), pltpu.VMEM((2, 1, bq), F32),
                        pltpu.VMEM((2, DA_DV, bq), F32),
                        pltpu.VMEM((2, 2, bk, bq), F32), pltpu.VMEM((2, 2, bk, bq), F32),
                        pltpu.VMEM((2, 1, bk, bq), F32),
                        pltpu.VMEM((2, 2 * bk, bq), BF16), pltpu.VMEM((2, 2 * bk, bq), BF16),
                        pltpu.VMEM((2, 1, bq), F32),
                        pltpu.VMEM((2, DA_DV, bq), BF16),
                        pltpu.VMEM((2, bk, bq), F32)],
        compiler_params=pltpu.CompilerParams(dimension_semantics=("arbitrary", "arbitrary", "arbitrary"),
                                             vmem_limit_bytes=VMEM_LIMIT),
        name="attn",
    )(scal, qt, kb, vt, btab, subw)


def _attn_cached_kernel(scal_ref, qt_ref, kc_ref, vc_ref, kn_ref, vtn_ref, bias0_ref, bias1_ref, subw_ref, o_ref,
                        *, bq, tq, bk, out_scale):
    lam = scal_ref[0]
    past = kc_ref.shape[0] // DA_HEADS
    lane = lax.broadcasted_iota(jnp.int32, (tq, DA_DV), 1)
    nt = (((1,), (1,)), ((), ()))

    for h in range(DA_HEADS):
        hs = slice(h * DA_DV, (h + 1) * DA_DV)
        cfar = scal_ref[1 + h]
        qn = qt_ref[hs, :].astype(F32).T[0:tq, :]
        q2 = jnp.concatenate([jnp.where(lane < DA_DK, qn, 0.0), jnp.where(lane >= DA_DK, qn, 0.0)],
                             axis=0).astype(BF16)
        head_rows = pl.ds(h, past, stride=DA_HEADS)
        s_c = lax.dot_general(q2, kc_ref[head_rows, :].astype(BF16), nt, preferred_element_type=F32)
        s_n = lax.dot_general(q2, kn_ref[:, hs], nt, preferred_element_type=F32)
        b0 = bias0_ref[h]
        b1 = bias1_ref[h]
        s = jnp.concatenate([s_c[:, 0:past - bk] + cfar,
                             s_c[:, past - bk:past] + jnp.concatenate([b0, b0], axis=0),
                             s_n + jnp.concatenate([b1, b1], axis=0)], axis=1)
        p = jnp.exp2(s - jnp.max(s, axis=1, keepdims=True))
        inv = 1.0 / jnp.sum(p, axis=1, keepdims=True)
        pb = p.astype(BF16)
        acc = (jnp.dot(pb[:, 0:past], vc_ref[head_rows, :].astype(BF16), preferred_element_type=F32)
               + jnp.dot(pb[:, past:past + bq], vtn_ref[hs, :].astype(F32).T.astype(BF16),
                         preferred_element_type=F32))
        o = acc[0:tq, :] * inv[0:tq] - acc[tq:2 * tq, :] * (lam * inv[tq:2 * tq])
        o_ref[0:tq, hs] = (_rms(o, subw_ref[...]) * out_scale).astype(BF16)
        o_ref[tq:bq, hs] = jnp.zeros((bq - tq, DA_DV), BF16)


def _attn_cached_call(scal, qt, cache_k, cache_v, kb, vt, bias0, bias1, subw, *, tq, bk, out_scale):
    b, past = cache_k.shape[:2]
    bq = qt.shape[-1]
    rows = past * DA_HEADS
    kern = functools.partial(_attn_cached_kernel, bq=bq, tq=tq, bk=bk, out_scale=out_scale)
    cache_spec = pl.BlockSpec((None, rows, DA_DV), lambda i: (i, 0, 0))
    return pl.pallas_call(
        kern,
        grid=(b,),
        in_specs=[pl.BlockSpec(memory_space=pltpu.SMEM),
                  pl.BlockSpec((None, None, DA_WIDTH, bq), lambda i: (i, 0, 0, 0)),
                  cache_spec, cache_spec,
                  pl.BlockSpec((None, bq, DA_WIDTH), lambda i: (i, 0, 0)),
                  pl.BlockSpec((None, None, DA_WIDTH, bq), lambda i: (i, 0, 0, 0)),
                  _const_spec((DA_HEADS, tq, bk)), _const_spec((DA_HEADS, tq, bq)),
                  _const_spec((1, DA_DV))],
        out_specs=pl.BlockSpec((None, bq, DA_WIDTH), lambda i: (i, 0, 0)),
        out_shape=jax.ShapeDtypeStruct((b, bq, DA_WIDTH), BF16),
        compiler_params=pltpu.CompilerParams(dimension_semantics=("parallel",), vmem_limit_bytes=VMEM_LIMIT),
        name="attn_cached",
    )(scal, qt, cache_k.reshape(b, rows, DA_DV), cache_v.reshape(b, rows, DA_DV), kb, vt, bias0, bias1, subw)


def _rel_bucket(rel):
    nb = REL_BUCKETS // 2
    max_exact = nb // 2
    n = jnp.abs(rel)
    nf = jnp.maximum(n, 1).astype(jnp.float32)
    large = max_exact + (jnp.log(nf / max_exact) / math.log(REL_MAX_DIST / max_exact)
                         * (nb - max_exact)).astype(jnp.int32)
    large = jnp.minimum(large, nb - 1)
    return jnp.where(rel > 0, nb, 0) + jnp.where(n < max_exact, n, large)


def _bias_tables(rel_bias, bq, bk):
    tabs = []
    for d in range(2):
        offs = (d - 1) * bk + bk - np.arange(bq + bk)
        tabs.append(rel_bias[_rel_bucket(jnp.asarray(offs, jnp.int32))].astype(F32).T * LOG2E)
    return jnp.stack(tabs, axis=1)[:, :, None, :]


def _bias_rows(rel_bias, qpos0, tq, kpos0, nkeys, tk_real):
    offs = (kpos0 - qpos0) + np.arange(-(tq - 1), nkeys)
    table = rel_bias[_rel_bucket(jnp.asarray(offs, jnp.int32))].astype(F32).T * LOG2E
    rows = jnp.stack([table[:, tq - 1 - i:tq - 1 - i + nkeys] for i in range(tq)], axis=1)
    qpos = qpos0 + np.arange(tq)
    kpos = kpos0 + np.arange(nkeys)
    vis = (kpos[None, :] // CHUNK <= qpos[:, None] // CHUNK) & (kpos[None, :] < tk_real)
    return jnp.where(jnp.asarray(vis)[None], rows, NEG)


def _ffn_kernel(x_ref, ys_ref, ya_ref, mod_ref, wo1_ref, wo2_ref, nfw_ref, wu_ref, cw_ref, cb_ref, wd_ref, fw_ref,
                hist_ref, y_ref, tail_ref, tail_scr, buf_a, buf_b, x1_scr, h2_scr, act_scr, *, nseq, tm):
    t = pl.program_id(1)
    nh = FFN_CONV - 1
    lo = SUBLANES - nh
    seqs = range(nseq)

    def rows(i):
        return slice(i * tm, (i + 1) * tm)

    def stacked(ref):
        return jnp.concatenate([ref[i] for i in seqs], axis=0) if nseq > 1 else ref[0]

    @pl.when(t == 0)
    def _init():
        for i in seqs:
            tail_scr[i, lo:SUBLANES, :] = hist_ref[i]

    mix = (jnp.dot(stacked(ys_ref), wo1_ref[...], preferred_element_type=F32)
           + jnp.dot(stacked(ya_ref), wo2_ref[...], preferred_element_type=F32))
    for i in seqs:
        x1 = x_ref[i] + mod_ref[i, 2:3, :] * mix[rows(i)]
        x1_scr[rows(i), :] = x1
        h2 = _rms(x1, nfw_ref[...]) * (1.0 + mod_ref[i, 4:5, :]) + mod_ref[i, 3:4, :]
        h2_scr[rows(i), :] = h2.astype(BF16)

    def cols(j, half):
        return slice(half * D_FF + j * FFN_CN, half * D_FF + (j + 1) * FFN_CN)

    def up(j, buf):
        for half in range(2):
            u = jnp.dot(h2_scr[...], wu_ref[:, cols(j, half)], preferred_element_type=F32)
            for i in seqs:
                buf[half, i, SUBLANES:SUBLANES + tm, :] = u[rows(i)]

    def conv(j, buf, half, i):
        cs = cols(j, half)
        buf[half, i, lo:SUBLANES, :] = tail_scr[i, lo:SUBLANES, cs]
        c = cb_ref[:, cs]
        for k in range(FFN_CONV):
            c = c + cw_ref[k:k + 1, cs] * buf[half, i, lo + k:lo + k + tm, :]
        tail_scr[i, lo:SUBLANES, cs] = buf[half, i, lo + tm:SUBLANES + tm, :]
        return c

    bufs = (buf_a, buf_b)

    @pl.when(t >= 0)
    def _chunks():
        up(0, bufs[0])
        for j in range(FFN_NC):
            if j + 1 < FFN_NC:
                up(j + 1, bufs[(j + 1) % 2])
            for i in seqs:
                cv = conv(j, bufs[j % 2], 0, i)
                cg = conv(j, bufs[j % 2], 1, i)
                act_scr[rows(i), j * FFN_CN:(j + 1) * FFN_CN] = (_silu(cg) * cv).astype(BF16)

    f = jnp.dot(act_scr[...], wd_ref[...], preferred_element_type=F32)
    for i in seqs:
        x2 = x1_scr[rows(i), :] + mod_ref[i, 5:6, :] * f[rows(i)]
        y_ref[i] = _rms(x2, fw_ref[...])
        tail_ref[i] = tail_scr[i, lo:SUBLANES, :]


def _ffn_call(x, ys, ya, mod3, wo1, wo2, nfw, wu, cw, cb, wd, fw, hist, *, nseq, tm):
    b, t, d = x.shape
    nh = FFN_CONV - 1
    m = nseq * tm

    def row(width):
        return pl.BlockSpec((nseq, tm, width), lambda i, j: (i, j, 0))

    hist_spec = pl.BlockSpec((nseq, nh, 2 * D_FF), lambda i, j: (i, 0, 0))
    kern = functools.partial(_ffn_kernel, nseq=nseq, tm=tm)
    return pl.pallas_call(
        kern,
        grid=(b // nseq, t // tm),
        in_specs=[row(d), row(SSD_WIDTH), row(DA_WIDTH),
                  pl.BlockSpec((nseq, 6, d), lambda i, j: (i, 0, 0)),
                  _const_spec((SSD_WIDTH, d)), _const_spec((DA_WIDTH, d)), _const_spec((1, d)),
                  _const_spec((d, 2 * D_FF)), _const_spec((FFN_CONV, 2 * D_FF)), _const_spec((1, 2 * D_FF)),
                  _const_spec((D_FF, d)), _const_spec((1, d)),
                  hist_spec],
        out_specs=[row(d), hist_spec],
        out_shape=[jax.ShapeDtypeStruct((b, t, d), F32),
                   jax.ShapeDtypeStruct((b, nh, 2 * D_FF), F32)],
        scratch_shapes=[pltpu.VMEM((nseq, SUBLANES, 2 * D_FF), F32),
                        pltpu.VMEM((2, nseq, tm + SUBLANES, FFN_CN), F32),
                        pltpu.VMEM((2, nseq, tm + SUBLANES, FFN_CN), F32),
                        pltpu.VMEM((m, d), F32), pltpu.VMEM((m, d), BF16), pltpu.VMEM((m, D_FF), BF16)],
        compiler_params=pltpu.CompilerParams(dimension_semantics=("parallel", "arbitrary"),
                                             vmem_limit_bytes=VMEM_LIMIT),
        name="ffn",
    )(x, ys, ya, mod3, wo1, wo2, nfw, wu, cw, cb, wd, fw, hist)


def _pack_params(norm_mix_w, w_in, ssm_conv_w, ssm_conv_b, ssm_dt_bias, ssm_a_log, ssm_d, ssm_norm_w,
                 lambda_q1, lambda_k1, lambda_q2, lambda_k2, attn_subln_w, rel_bias, w_out,
                 norm_ffn_w, w_up, ffn_conv_w, ffn_conv_b, w_down, final_norm_w, layer):
    l = layer
    wz, wx, wdt, wq, wk, wv = jnp.split(w_in[l], IN_SPLITS, axis=-1)
    wdt = jnp.pad(wdt, ((0, 0), (0, LANES - SSD_HEADS)))
    w_cat = jnp.concatenate([wz, wx, wdt, wk, wv], axis=-1).astype(BF16)
    w_t = (wq * (DA_DK ** -0.5 * LOG2E)).T.astype(BF16)

    def pad_heads(v):
        return jnp.pad(v.astype(F32), (0, LANES - SSD_HEADS)).reshape(1, LANES)

    lam_init = 0.8 - 0.6 * math.exp(-0.3 * l)
    lam = (jnp.exp(jnp.sum(lambda_q1[l].astype(F32) * lambda_k1[l].astype(F32)))
           - jnp.exp(jnp.sum(lambda_q2[l].astype(F32) * lambda_k2[l].astype(F32))) + lam_init)
    far_bias = rel_bias[REL_BUCKETS // 2 - 1].astype(F32)
    return dict(
        norm_mix_w=norm_mix_w[l].reshape(1, D_MODEL), w_cat=w_cat, w_t=w_t,
        cw=ssm_conv_w[l], cbias=ssm_conv_b[l].reshape(1, SSD_CONV_DIM),
        dtb=pad_heads(ssm_dt_bias[l]), alog=pad_heads(ssm_a_log[l]),
        dsk=jnp.repeat(ssm_d[l].astype(F32), SSD_HEADDIM).reshape(1, SSD_WIDTH),
        ssm_nw=ssm_norm_w[l].reshape(1, SSD_WIDTH),
        scal=jnp.concatenate([lam.reshape(1), far_bias * LOG2E]).astype(F32), lam_init=lam_init,
        subw=attn_subln_w[l].reshape(DA_DV, 1), rel_bias=rel_bias,
        wo1=w_out[l][:SSD_WIDTH].astype(BF16), wo2=w_out[l][SSD_WIDTH:].astype(BF16),
        nfw=norm_ffn_w[l].reshape(1, D_MODEL),
        wu=w_up[l].astype(BF16), ffn_cw=ffn_conv_w[l], ffn_cb=ffn_conv_b[l].reshape(1, 2 * D_FF),
        wd=w_down[l].astype(BF16), fw=final_norm_w.reshape(1, D_MODEL),
    )


def _state_to_kernel(h):
    return h.reshape(h.shape[0], SSD_GROUPS, GROUP_W, SSD_STATE)


def _state_from_kernel(h):
    return h.reshape(h.shape[0], SSD_HEADS, SSD_HEADDIM, SSD_STATE)


def _run_group(x, mod, past_k, past_v, ssm_h0, ssm_conv_hist, ffn_conv_hist, p, *, tm, ssd_rows, bq, bk):
    b, t, d = x.shape
    past = 0 if past_k is None else past_k.shape[1]
    chunk = min(CHUNK, t)
    tp = max(t, SUPER)
    if tp != t:
        x = jnp.pad(x, ((0, 0), (0, tp - t), (0, 0)))
        tm = ssd_rows = bq = tp
    mod3 = mod.reshape(b, 6, d)

    assert tm == bq
    zs, xc, dt, k, v, kb, qt, vt, conv_new = _inproj_call(
        x, mod3, p["norm_mix_w"], p["w_cat"], p["w_t"], ssm_conv_hist.astype(F32), p["cw"], p["cbias"],
        tm=tm, real=min(t, tm))

    y_ssd, h_t = _ssd_call(zs, xc, dt, _state_to_kernel(ssm_h0.astype(F32)),
                           p["dtb"], p["alog"], p["dsk"], p["ssm_nw"],
                           chunk=chunk, rows=ssd_rows, real=min(t, ssd_rows))

    if past == 0:
        assert bq == bk and t % bq == 0
        y_att = _attn_call(p["scal"], qt, kb, vt, _bias_tables(p["rel_bias"], bq, bk), p["subw"], bq=bq, bk=bk,
                           out_scale=1.0 - p["lam_init"])
    else:
        assert past % bk == 0 and past >= bk and bq == tp <= bk
        bias0 = _bias_rows(p["rel_bias"], past, t, past - bk, bk, past + t)
        bias1 = _bias_rows(p["rel_bias"], past, t, past, bq, past + t)
        y_att = _attn_cached_call(p["scal"], qt, past_k, past_v, kb, vt, bias0, bias1, p["subw"].reshape(1, DA_DV),
                                  tq=t, bk=bk, out_scale=1.0 - p["lam_init"])

    tf = min(t, tm)
    nseq = math.gcd(b, max(1, TILE_ROWS // tf))
    y, ffn_new = _ffn_call(x[:, :t], y_ssd[:, :t], y_att[:, :t], mod3, p["wo1"], p["wo2"], p["nfw"], p["wu"],
                           p["ffn_cw"], p["ffn_cb"], p["wd"], p["fw"], ffn_conv_hist.astype(F32),
                           nseq=nseq, tm=tf)
    k = k.reshape(b, tp, DA_HEADS, 2 * DA_DK)
    v = v.reshape(b, tp, DA_HEADS, DA_DV)
    return (y, k[:, :t], v[:, :t],
            _state_from_kernel(h_t), conv_new, ffn_new)


def kernel(x_prompt, x_sample, c_prompt, c_sample, cache_k, cache_v, state_ssm, state_ssm_conv, state_ffn_conv, w_ada, b_ada, norm_mix_w, w_in, ssm_conv_w, ssm_conv_b, ssm_dt_bias, ssm_a_log, ssm_d, ssm_norm_w, lambda_q1, lambda_k1, lambda_q2, lambda_k2, attn_subln_w, rel_bias, w_out, norm_ffn_w, w_up, ffn_conv_w, ffn_conv_b, w_down, final_norm_w):
    bp, bs = x_prompt.shape[0], x_sample.shape[0]
    dt = x_prompt.dtype
    p = _pack_params(norm_mix_w, w_in, ssm_conv_w, ssm_conv_b, ssm_dt_bias, ssm_a_log, ssm_d, ssm_norm_w,
                     lambda_q1, lambda_k1, lambda_q2, lambda_k2, attn_subln_w, rel_bias, w_out,
                     norm_ffn_w, w_up, ffn_conv_w, ffn_conv_b, w_down, final_norm_w, 0)
    c_all = jnp.concatenate([c_prompt, c_sample], axis=0)
    npad = -c_all.shape[0] % SUBLANES
    c_all = jnp.pad(c_all, ((0, npad), (0, 0)))
    mod = _mod_call(c_all, w_ada[0], b_ada[0].reshape(1, -1))

    zeros = lambda *s: jnp.zeros(s, dt)
    out_p = _run_group(x_prompt, mod[:bp], None, None,
                       zeros(bp, SSD_HEADS, SSD_HEADDIM, SSD_STATE), zeros(bp, SSD_CONV - 1, SSD_CONV_DIM),
                       zeros(bp, FFN_CONV - 1, 2 * D_FF), p,
                       tm=TILE_ROWS, ssd_rows=SSD_ROWS, bq=ATT_BLOCK, bk=ATT_BLOCK)
    out_s = _run_group(x_sample, mod[bp:bp + bs], cache_k[0], cache_v[0], state_ssm[0], state_ssm_conv[0],
                       state_ffn_conv[0], p, tm=SUPER, ssd_rows=SUPER, bq=SUPER, bk=ATT_BLOCK)
    y_p, k_p, v_p, h_p, c_p, f_p = out_p
    y_s, k_s, v_s, h_s, c_s, f_s = out_s
    return (y_p, y_s, k_p[None], v_p[None], h_p[None], c_p[None], f_p[None],
            k_s[None], v_s[None], h_s[None], c_s[None], f_s[None])
```

```python
import functools
import math

import numpy as np
import jax
import jax.numpy as jnp
from jax import lax
from jax.experimental import pallas as pl
from jax.experimental.pallas import tpu as pltpu

F32 = jnp.float32
BF16 = jnp.bfloat16

D_MODEL = 1024
CHUNK = 64
SSD_WIDTH = 512
SSD_HEADDIM = 64
SSD_HEADS = 8
SSD_GROUPS = 2
SSD_HPG = 4
SSD_STATE = 128
SSD_CONV = 4
SSD_CONV_DIM = SSD_WIDTH + 2 * SSD_GROUPS * SSD_STATE
GROUP_W = SSD_HPG * SSD_HEADDIM
DA_WIDTH = 512
DA_DK = 64
DA_DV = 128
DA_HEADS = 4
REL_BUCKETS = 32
REL_MAX_DIST = 128
D_FF = 2816
FFN_CONV = 3
EPS = 1e-6
IN_SPLITS = (512, 1536, 1544, 2056, 2568)
LANES = 128
SUBLANES = 8
TILE_ROWS = 512
MOD_TN = 1024
SUPER = 128
SSD_ROWS = 256
SSD_NSEQ = 4
ATT_BLOCK = 512
ATT_KT = 64
FFN_CN = 256
FFN_NC = D_FF // FFN_CN
NEG = -1e30
VMEM_LIMIT = 56 * 1024 * 1024

PZ, PX, PDT, PK, PV, PEND = 0, 512, 1536, 1664, 2176, 2688
LOG2E = math.log2(math.e)


def _silu(x):
    return x / (1.0 + jnp.exp(-x))


def _softplus(x):
    return jnp.maximum(x, 0.0) + jnp.log1p(jnp.exp(-jnp.abs(x)))


def _split3(x):
    hi = x.astype(BF16)
    r1 = x - hi.astype(F32)
    mid = r1.astype(BF16)
    lo = (r1 - mid.astype(F32)).astype(BF16)
    return hi, mid, lo


def _rms(x, w):
    return x * lax.rsqrt(jnp.mean(x * x, axis=-1, keepdims=True) + EPS) * w


def _const_spec(shape):
    nd = len(shape)
    return pl.BlockSpec(shape, lambda *_: (0,) * nd)


def _mod_kernel(c_ref, w_ref, b_ref, o_ref):
    a = _silu(c_ref[...]).astype(BF16)
    o_ref[...] = jnp.dot(a, w_ref[...].astype(BF16), preferred_element_type=F32) + b_ref[...]


def _mod_call(c, w_ada, b_ada):
    n, d = c.shape
    nout = w_ada.shape[1]
    tn = MOD_TN
    return pl.pallas_call(
        _mod_kernel,
        grid=(nout // tn,),
        in_specs=[pl.BlockSpec((n, d), lambda j: (0, 0)),
                  pl.BlockSpec((d, tn), lambda j: (0, j)),
                  pl.BlockSpec((1, tn), lambda j: (0, j))],
        out_specs=pl.BlockSpec((n, tn), lambda j: (0, j)),
        out_shape=jax.ShapeDtypeStruct((n, nout), F32),
        name="mod",
    )(c, w_ada, b_ada)


def _inproj_kernel(x_ref, mod_ref, nw_ref, w_ref, wt_ref, hist_ref, cw_ref, cbias_ref,
                   zs_ref, xc_ref, dt_ref, k_ref, v_ref, kb_ref, qt_ref, vt_ref, cout_ref, cbuf, hb_scr, zbuf,
                   *, tm, real):
    t = pl.program_id(1)
    nconv = SSD_CONV - 1

    @pl.when(t == 0)
    def _init():
        cbuf[0:SUBLANES, :] = jnp.zeros((SUBLANES, SSD_CONV_DIM), F32)
        cbuf[SUBLANES - nconv:SUBLANES, :] = hist_ref[...]

    h = _rms(x_ref[...], nw_ref[...]) * (1.0 + mod_ref[1:2, :]) + mod_ref[0:1, :]
    hb_scr[...] = h.astype(BF16)

    def proj(a, b):
        return jnp.dot(hb_scr[...], w_ref[:, a:b], preferred_element_type=F32)

    def proj_t(a, b):
        return lax.dot_general(wt_ref[a:b, :], hb_scr[...], (((1,), (1,)), ((), ())), preferred_element_type=F32)

    cbuf[SUBLANES:SUBLANES + tm, :] = proj(PX, PDT)
    zbuf[...] = proj(PZ, PX)
    dt_ref[...] = proj(PDT, PK)
    k = proj(PK, PV)
    v = proj(PV, PEND)
    for hd in range(DA_HEADS):
        dst = pl.ds(hd, tm, stride=DA_HEADS)
        k_ref[dst, :] = k[:, hd * DA_DV:(hd + 1) * DA_DV]
        v_ref[dst, :] = v[:, hd * DA_DV:(hd + 1) * DA_DV]
    kb_ref[...] = k.astype(BF16)
    qt_ref[...] = proj_t(0, DA_WIDTH).astype(BF16)
    for hd in range(DA_HEADS):
        vt_ref[hd * DA_DV:(hd + 1) * DA_DV, :] = v[:, hd * DA_DV:(hd + 1) * DA_DV].T.astype(BF16)
    conv = cbias_ref[...]
    for j in range(SSD_CONV):
        off = SUBLANES - nconv + j
        conv = conv + cw_ref[j:j + 1, :] * cbuf[off:off + tm, :]
    tail = cbuf[SUBLANES - nconv + real:SUBLANES + real, :]
    cout_ref[...] = tail
    cbuf[SUBLANES - nconv:SUBLANES, :] = tail
    xc_ref[...] = _silu(conv).astype(BF16)
    zs_ref[...] = _silu(zbuf[...]).astype(BF16)


def _inproj_call(x, mod3, norm_w, w_cat, w_t, hist, cw, cbias, *, tm, real):
    b, t, d = x.shape
    nt = t // tm

    def row(width):
        return pl.BlockSpec((None, tm, width), lambda i, j: (i, j, 0))

    def out(width, dtype):
        return jax.ShapeDtypeStruct((b, t, width), dtype)

    tspec = pl.BlockSpec((None, None, DA_WIDTH, tm), lambda i, j: (i, j, 0, 0))
    tshape = jax.ShapeDtypeStruct((b, nt, DA_WIDTH, tm), BF16)
    hist_spec = pl.BlockSpec((None, SSD_CONV - 1, SSD_CONV_DIM), lambda i, j: (i, 0, 0))
    hspec = pl.BlockSpec((None, tm * DA_HEADS, DA_DV), lambda i, j: (i, j, 0))
    hshape = jax.ShapeDtypeStruct((b, t * DA_HEADS, DA_DV), F32)
    return pl.pallas_call(
        functools.partial(_inproj_kernel, tm=tm, real=real),
        grid=(b, nt),
        in_specs=[row(d),
                  pl.BlockSpec((None, 6, d), lambda i, j: (i, 0, 0)),
                  _const_spec((1, d)),
                  _const_spec((d, PEND)),
                  _const_spec((DA_WIDTH, d)),
                  hist_spec, _const_spec((SSD_CONV, SSD_CONV_DIM)), _const_spec((1, SSD_CONV_DIM))],
        out_specs=[row(512), row(1024), row(LANES), hspec, hspec, row(512), tspec, tspec, hist_spec],
        out_shape=[out(512, BF16), out(1024, BF16), out(LANES, F32),
                   hshape, hshape, out(512, BF16), tshape, tshape,
                   jax.ShapeDtypeStruct((b, SSD_CONV - 1, SSD_CONV_DIM), F32)],
        scratch_shapes=[pltpu.VMEM((tm + SUBLANES, SSD_CONV_DIM), F32), pltpu.VMEM((tm, d), BF16),
                        pltpu.VMEM((tm, SSD_WIDTH), F32)],
        compiler_params=pltpu.CompilerParams(dimension_semantics=("parallel", "arbitrary"),
                                             vmem_limit_bytes=VMEM_LIMIT),
        name="inproj",
    )(x, mod3, norm_w, w_cat, w_t, hist, cw, cbias)


def _ssd_kernel(zs_ref, xc_ref, dt_ref, h0_ref, dtb_ref, alog_ref, dsk_ref, nw_ref, tri_ref, e_ref,
                y_ref, hout_ref, h_scr, ybuf, *, nseq, chunk, rows, real):
    t = pl.program_id(1)
    seqs = range(nseq)

    @pl.when(t == 0)
    def _init():
        for i in seqs:
            for g in range(SSD_GROUPS):
                h_scr[i, g] = h0_ref[i, g].T

    li = lax.broadcasted_iota(jnp.int32, (SUPER, SUPER), 0)
    si = lax.broadcasted_iota(jnp.int32, (SUPER, SUPER), 1)
    cshift = chunk.bit_length() - 1
    mask2 = ((li >> cshift) == (si >> cshift)) & (si <= li)
    lane_g = lax.broadcasted_iota(jnp.int32, (SUPER, GROUP_W), 1) >> (SSD_HEADDIM.bit_length() - 1)

    pre = []
    for i in seqs:
        xs = xc_ref[i, :, 0:SSD_WIDTH].astype(F32)
        dtv = _softplus(dt_ref[i] + dtb_ref[...])
        da = dtv * (-jnp.exp(alog_ref[...]))
        acs = jnp.dot(tri_ref[...], jnp.concatenate(_split3(da), axis=0), preferred_element_type=F32)
        dt_x = jnp.dot(jnp.concatenate(_split3(dtv), axis=1), e_ref[...], preferred_element_type=F32)
        acs_x = jnp.dot(jnp.concatenate(_split3(acs), axis=1), e_ref[...], preferred_element_type=F32)
        pre.append((xs, acs, acs_x, jnp.exp(acs_x), xs * dt_x))

    for sb in range(rows // SUPER):
        o = sb * SUPER
        nreal = (min(real, o + SUPER) - o) // chunk
        acs2 = [pre[i][1][o:o + SUPER, :] for i in seqs]
        acs_t = [a.T for a in acs2]
        chains = [(g, i) for g in range(SSD_GROUPS) for i in seqs]

        def gsl(g):
            return slice(g * GROUP_W, (g + 1) * GROUP_W)

        cmb, cb2, bm_t = {}, {}, {}
        for g, i in chains:
            bcol = SSD_WIDTH + g * SSD_STATE
            ccol = SSD_WIDTH + (SSD_GROUPS + g) * SSD_STATE
            bmb = xc_ref[i, o:o + SUPER, bcol:bcol + SSD_STATE]
            cmb[g, i] = xc_ref[i, o:o + SUPER, ccol:ccol + SSD_STATE]
            cb2[g, i] = lax.dot_general(cmb[g, i], bmb, (((1,), (1,)), ((), ())), preferred_element_type=F32)
            bm_t[g, i] = bmb.astype(F32).T.astype(BF16)
        for g, i in chains:
            ms = []
            for rr in range(SSD_HPG):
                r = g * SSD_HPG + rr
                seg = acs2[i][:, r:r + 1] - acs_t[i][r:r + 1, :]
                dec = jnp.where(mask2, jnp.exp(jnp.where(mask2, seg, 0.0)), 0.0)
                ms.append((cb2[g, i] * dec).astype(BF16))
            full = jnp.dot(jnp.concatenate(ms, axis=0), pre[i][4][o:o + SUPER, gsl(g)].astype(BF16),
                           preferred_element_type=F32)
            ydiag = full[0:SUPER]
            for rr in range(1, SSD_HPG):
                ydiag = jnp.where(lane_g == rr, full[rr * SUPER:(rr + 1) * SUPER], ydiag)
            ybuf[i, o:o + SUPER, gsl(g)] = ydiag
        for j in range(nreal):
            a0, a1 = o + j * chunk, o + (j + 1) * chunk
            h_t = {c: h_scr[c[1], c[0]] for c in chains}
            yoff = {(g, i): jnp.dot(cmb[g, i][j * chunk:(j + 1) * chunk, :], h_t[g, i].astype(BF16),
                                    preferred_element_type=F32) for g, i in chains}
            st = {}
            for g, i in chains:
                _, _, acs_x, eacs_x, xd = pre[i]
                ybuf[i, a0:a1, gsl(g)] = ybuf[i, a0:a1, gsl(g)] + yoff[g, i] * eacs_x[a0:a1, gsl(g)]
                dte = jnp.exp(acs_x[a1 - 1:a1, gsl(g)] - acs_x[a0:a1, gsl(g)])
                xw = (xd[a0:a1, gsl(g)] * dte).astype(BF16)
                pieces = []
                if j > 0:
                    pieces.append(jnp.zeros((j * chunk, GROUP_W), BF16))
                pieces.append(xw)
                if (j + 1) * chunk < SUPER:
                    pieces.append(jnp.zeros((SUPER - (j + 1) * chunk, GROUP_W), BF16))
                xw2 = jnp.concatenate(pieces, axis=0) if len(pieces) > 1 else xw
                st[g, i] = jnp.dot(bm_t[g, i], xw2, preferred_element_type=F32)
            for g, i in chains:
                h_scr[i, g] = h_t[g, i] * pre[i][3][a1 - 1:a1, gsl(g)] + st[g, i]

    for i in seqs:
        y = (ybuf[i] + dsk_ref[...] * pre[i][0]) * zs_ref[i].astype(F32)
        for g in range(SSD_GROUPS):
            gs = slice(g * GROUP_W, (g + 1) * GROUP_W)
            y_ref[i, :, gs] = _rms(y[:, gs], nw_ref[:, gs]).astype(BF16)

    @pl.when(t == pl.num_programs(1) - 1)
    def _fin():
        for i in seqs:
            for g in range(SSD_GROUPS):
                hout_ref[i, g] = h_scr[i, g].T


def _ssd_call(zs, xc, dt, h0_t, dtb, alog, dsk, nw, *, chunk, rows, real):
    b, t, _ = zs.shape
    ii = np.arange(rows)
    tri = ((ii[:, None] // chunk == ii[None, :] // chunk) & (ii[None, :] <= ii[:, None])).astype(np.float32)
    e = np.zeros((LANES, SSD_WIDTH), np.float32)
    for r in range(SSD_HEADS):
        e[r, r * SSD_HEADDIM:(r + 1) * SSD_HEADDIM] = 1.0

    nseq = math.gcd(b, SSD_NSEQ)

    def row(width):
        return pl.BlockSpec((nseq, rows, width), lambda i, j: (i, j, 0))

    state_spec = pl.BlockSpec((nseq, SSD_GROUPS, GROUP_W, SSD_STATE), lambda i, j: (i, 0, 0, 0))
    kern = functools.partial(_ssd_kernel, nseq=nseq, chunk=chunk, rows=rows, real=real)
    return pl.pallas_call(
        kern,
        grid=(b // nseq, t // rows),
        in_specs=[row(SSD_WIDTH), row(SSD_CONV_DIM), row(LANES), state_spec,
                  _const_spec((1, LANES)), _const_spec((1, LANES)),
                  _const_spec((1, SSD_WIDTH)), _const_spec((1, SSD_WIDTH)),
                  _const_spec((rows, 3 * rows)), _const_spec((3 * LANES, SSD_WIDTH))],
        out_specs=[row(SSD_WIDTH), state_spec],
        out_shape=[jax.ShapeDtypeStruct((b, t, SSD_WIDTH), BF16),
                   jax.ShapeDtypeStruct((b, SSD_GROUPS, GROUP_W, SSD_STATE), F32)],
        scratch_shapes=[pltpu.VMEM((nseq, SSD_GROUPS, SSD_STATE, GROUP_W), F32),
                        pltpu.VMEM((nseq, rows, SSD_WIDTH), F32)],
        compiler_params=pltpu.CompilerParams(dimension_semantics=("parallel", "arbitrary"),
                                             vmem_limit_bytes=VMEM_LIMIT),
        name="ssd",
    )(zs, xc, dt, h0_t, dtb, alog, dsk, nw,
      jnp.asarray(np.tile(tri, (1, 3)), BF16), jnp.asarray(np.tile(e, (3, 1)), BF16))


def _attn_kernel(scal_ref, qt_ref, k_ref, vt_ref, btab_ref, subw_ref, o_ref,
                 m_scr, l_scr, acc_scr, sa_scr, sb_scr, sc_scr, pa_scr, pb_scr, alpha_scr, qz_scr, bias_scr,
                 *, bq, bk, out_scale):
    h = pl.program_id(0)
    qi = pl.program_id(2)
    kn0 = qi - 1
    lam = scal_ref[0]
    cfar = scal_ref[1 + h]

    @pl.when((pl.program_id(1) == 0) & (qi == 0))
    def _build_bias_tiles():
        width = bq + bk
        kj = lax.broadcasted_iota(jnp.int32, (bk, bq), 0)
        qj = lax.broadcasted_iota(jnp.int32, (bk, bq), 1)
        cshift = CHUNK.bit_length() - 1
        for d in range(2):
            skew = pltpu.roll(jnp.broadcast_to(btab_ref[d], (bk, width)), 0, 1, stride=1, stride_axis=0)
            tile = skew[:, bk:width]
            if d == 1:
                tile = jnp.where((kj >> cshift) <= (qj >> cshift), tile, NEG)
            bias_scr[d] = tile

    zero = jnp.zeros((DA_DK, bq), BF16)
    qz_scr[0, 0:DA_DK, :] = qt_ref[0:DA_DK, :]
    qz_scr[0, DA_DK:DA_DV, :] = zero
    qz_scr[1, 0:DA_DK, :] = zero
    qz_scr[1, DA_DK:DA_DV, :] = qt_ref[DA_DK:DA_DV, :]

    m_scr[...] = jnp.full(m_scr.shape, NEG, F32)
    l_scr[...] = jnp.zeros(l_scr.shape, F32)
    acc_scr[...] = jnp.zeros(acc_scr.shape, F32)

    nsub = bk // ATT_KT

    def fold(x):
        return x.reshape(ATT_KT // SUBLANES, SUBLANES, x.shape[1])

    def scores(s_buf, first, count, own_last=False):
        half = bk // 2
        for mm in range(2):
            for e in range(count):
                start = pl.multiple_of((first + e) * bk, bk)
                if own_last and e == count - 1 and half % LANES == 0:
                    s_buf[mm, e, 0:half, :] = jnp.dot(k_ref[pl.ds(start, half), :], qz_scr[mm],
                                                      preferred_element_type=F32)
                    s_buf[mm, e, half:bk, half:bq] = jnp.dot(k_ref[pl.ds(start + half, half), :],
                                                             qz_scr[mm, :, half:bq], preferred_element_type=F32)
                else:
                    s_buf[mm, e] = jnp.dot(k_ref[pl.ds(start, bk), :], qz_scr[mm], preferred_element_type=F32)

    def first_visible(near, t):
        if near != 1:
            return 0
        return (t * ATT_KT // CHUNK) * CHUNK // LANES * LANES

    def merge(acc, part, c0, op):
        if acc is None:
            return part
        if c0 == 0:
            return op(acc, part)
        return jnp.concatenate([acc[:, :c0], op(acc[:, c0:], part)], axis=1)

    def softmax(s_buf, p_buf, entries):
        alphas = []
        for mm in range(2):
            cand = None
            for e, (near, shift) in enumerate(entries):
                mx = None
                for t in range(nsub):
                    rows = slice(t * ATT_KT, (t + 1) * ATT_KT)
                    c0 = first_visible(near, t)
                    s = s_buf[mm, e, rows, c0:]
                    if near is not None:
                        s = s + bias_scr[near, rows, c0:]
                    mx = merge(mx, jnp.max(fold(s), axis=0), c0, jnp.maximum)
                mx = jnp.max(mx, axis=0, keepdims=True) + shift
                cand = mx if cand is None else jnp.maximum(cand, mx)
            m_old = m_scr[mm]
            m_new = jnp.maximum(m_old, cand)
            ls = None
            for e, (near, shift) in enumerate(entries):
                off = m_new - shift
                for t in range(nsub):
                    rows = slice(t * ATT_KT, (t + 1) * ATT_KT)
                    prow = slice(e * bk + t * ATT_KT, e * bk + (t + 1) * ATT_KT)
                    c0 = first_visible(near, t)
                    s = s_buf[mm, e, rows, c0:]
                    if near is not None:
                        s = s + bias_scr[near, rows, c0:]
                    p = jnp.exp2(s - off[:, c0:])
                    if c0:
                        p_buf[mm, prow, 0:c0] = jnp.zeros((ATT_KT, c0), BF16)
                    p_buf[mm, prow, c0:] = p.astype(BF16)
                    ls = merge(ls, jnp.sum(fold(p), axis=0), c0, jnp.add)
            alpha = jnp.exp2(m_old - m_new)
            l_scr[mm] = alpha * l_scr[mm] + jnp.sum(ls, axis=0, keepdims=True)
            m_scr[mm] = m_new
            alphas.append(alpha)
        return alphas

    def pv(p_buf, first, n):
        vts = [vt_ref[first + e] for e in range(n)]
        vt = jnp.concatenate(vts, axis=1) if n > 1 else vts[0]
        return [jnp.dot(vt, p_buf[mm, 0:n * bk, :], preferred_element_type=F32) for mm in range(2)]

    def accumulate(alphas, pvs):
        for mm in range(2):
            acc_scr[mm] = alphas[mm] * acc_scr[mm] + pvs[mm]

    def softmax_pv(s_buf, first, entries):
        alphas = softmax(s_buf, pa_scr, entries)
        accumulate(alphas, pv(pa_scr, first, len(entries)))

    nfar = jnp.maximum(kn0, 0)
    odd = nfar % 2
    far = (None, cfar)

    near_pair = [(0, 0.0), (1, 0.0)]

    @pl.when(kn0 < 0)
    def _only_first():
        scores(sc_scr, 0, 1, own_last=True)
        softmax_pv(sc_scr, 0, [(1, 0.0)])

    @pl.when(kn0 >= 0)
    def _groups():
        @pl.when(odd == 1)
        def _single():
            scores(sc_scr, 0, 1)
            scores(sa_scr, 1, 2)
            softmax_pv(sc_scr, 0, [far])

        @pl.when(odd == 0)
        def _first_pair():
            scores(sa_scr, 0, 2)

        def far_step(s_cur, p_cur, s_next, p_prev, cur):
            scores(s_next, cur + 2, 2)
            pending = None if p_prev is None else pv(p_prev, cur - 2, 2)
            alphas = softmax(s_cur, p_cur, [far, far])
            if pending is not None:
                accumulate([alpha_scr[0], alpha_scr[1]], pending)
            for mm in range(2):
                alpha_scr[mm] = alphas[mm]

        npairs = nfar // 2
        niter = npairs // 2

        @pl.when(niter >= 1)
        def _far_loop():
            far_step(sa_scr, pa_scr, sb_scr, None, odd)
            far_step(sb_scr, pb_scr, sa_scr, pa_scr, odd + 2)

            def far_body(j, carry):
                cur = odd + 4 * j
                far_step(sa_scr, pa_scr, sb_scr, pb_scr, cur)
                far_step(sb_scr, pb_scr, sa_scr, pa_scr, cur + 2)
                return carry

            lax.fori_loop(1, niter, far_body, 0)
            accumulate([alpha_scr[0], alpha_scr[1]], pv(pb_scr, odd + 4 * niter - 2, 2))

        @pl.when(npairs % 2 == 1)
        def _tail_b():
            scores(sb_scr, kn0, 2, own_last=True)
            softmax_pv(sa_scr, kn0 - 2, [far, far])
            softmax_pv(sb_scr, kn0, near_pair)

        @pl.when(npairs % 2 == 0)
        def _tail_a():
            softmax_pv(sa_scr, kn0, near_pair)

    o = acc_scr[0] * (1.0 / l_scr[0]) - acc_scr[1] * (lam / l_scr[1])
    o = o * lax.rsqrt(jnp.mean(o * o, axis=0, keepdims=True) + EPS) * (subw_ref[...] * out_scale)
    o_ref[...] = o.T.astype(BF16)


def _attn_call(scal, qt, kb, vt, btab, subw, *, bq, bk, out_scale):
    b, nq = qt.shape[:2]
    tk = kb.shape[1]
    nkb = vt.shape[1]
    kern = functools.partial(_attn_kernel, bq=bq, bk=bk, out_scale=out_scale)
    return pl.pallas_call(
        kern,
        grid=(DA_HEADS, b, nq),
        in_specs=[pl.BlockSpec(memory_space=pltpu.SMEM),
                  pl.BlockSpec((None, None, DA_DV, bq), lambda h, i, j: (i, j, h, 0)),
                  pl.BlockSpec((None, tk, DA_DV), lambda h, i, j: (i, 0, h)),
                  pl.BlockSpec((None, nkb, DA_DV, bk), lambda h, i, j: (i, 0, h, 0)),
                  pl.BlockSpec((None, 2, 1, bq + bk), lambda h, i, j: (h, 0, 0, 0)),
                  pl.BlockSpec((DA_DV, 1), lambda h, i, j: (0, 0))],
        out_specs=pl.BlockSpec((None, bq, DA_DV), lambda h, i, j: (i, j, h)),
        out_shape=jax.ShapeDtypeStruct((b, nq * bq, DA_WIDTH), BF16),
        scratch_shapes=[pltpu.VMEM((2, 1, bq), F32), pltpu.VMEM((2, 1, bq), F32),
                        pltpu.VMEM((2, DA_DV, bq), F32),
                        pltpu.VMEM((2, 2, bk, bq), F32), pltpu.VMEM((2, 2, bk, bq), F32),
                        pltpu.VMEM((2, 1, bk, bq), F32),
                        pltpu.VMEM((2, 2 * bk, bq), BF16), pltpu.VMEM((2, 2 * bk, bq), BF16),
                        pltpu.VMEM((2, 1, bq), F32),
                        pltpu.VMEM((2, DA_DV, bq), BF16),
                        pltpu.VMEM((2, bk, bq), F32)],
        compiler_params=pltpu.CompilerParams(dimension_semantics=("arbitrary", "arbitrary", "arbitrary"),
                                             vmem_limit_bytes=VMEM_LIMIT),
        name="attn",
    )(scal, qt, kb, vt, btab, subw)


def _attn_cached_kernel(scal_ref, qt_ref, kc_ref, vc_ref, kn_ref, vtn_ref, bias0_ref, bias1_ref, subw_ref, o_ref,
                        *, bq, tq, bk, out_scale):
    lam = scal_ref[0]
    past = kc_ref.shape[0] // DA_HEADS
    lane = lax.broadcasted_iota(jnp.int32, (tq, DA_DV), 1)
    nt = (((1,), (1,)), ((), ()))

    for h in range(DA_HEADS):
        hs = slice(h * DA_DV, (h + 1) * DA_DV)
        cfar = scal_ref[1 + h]
        qn = qt_ref[hs, :].astype(F32).T[0:tq, :]
        q2 = jnp.concatenate([jnp.where(lane < DA_DK, qn, 0.0), jnp.where(lane >= DA_DK, qn, 0.0)],
                             axis=0).astype(BF16)
        head_rows = pl.ds(h, past, stride=DA_HEADS)
        s_c = lax.dot_general(q2, kc_ref[head_rows, :].astype(BF16), nt, preferred_element_type=F32)
        s_n = lax.dot_general(q2, kn_ref[:, hs], nt, preferred_element_type=F32)
        b0 = bias0_ref[h]
        b1 = bias1_ref[h]
        s = jnp.concatenate([s_c[:, 0:past - bk] + cfar,
                             s_c[:, past - bk:past] + jnp.concatenate([b0, b0], axis=0),
                             s_n + jnp.concatenate([b1, b1], axis=0)], axis=1)
        p = jnp.exp2(s - jnp.max(s, axis=1, keepdims=True))
        inv = 1.0 / jnp.sum(p, axis=1, keepdims=True)
        pb = p.astype(BF16)
        acc = (jnp.dot(pb[:, 0:past], vc_ref[head_rows, :].astype(BF16), preferred_element_type=F32)
               + jnp.dot(pb[:, past:past + bq], vtn_ref[hs, :].astype(F32).T.astype(BF16),
                         preferred_element_type=F32))
        o = acc[0:tq, :] * inv[0:tq] - acc[tq:2 * tq, :] * (lam * inv[tq:2 * tq])
        o_ref[0:tq, hs] = (_rms(o, subw_ref[...]) * out_scale).astype(BF16)
        o_ref[tq:bq, hs] = jnp.zeros((bq - tq, DA_DV), BF16)


def _attn_cached_call(scal, qt, cache_k, cache_v, kb, vt, bias0, bias1, subw, *, tq, bk, out_scale):
    b, past = cache_k.shape[:2]
    bq = qt.shape[-1]
    rows = past * DA_HEADS
    kern = functools.partial(_attn_cached_kernel, bq=bq, tq=tq, bk=bk, out_scale=out_scale)
    cache_spec = pl.BlockSpec((None, rows, DA_DV), lambda i: (i, 0, 0))
    return pl.pallas_call(
        kern,
        grid=(b,),
        in_specs=[pl.BlockSpec(memory_space=pltpu.SMEM),
                  pl.BlockSpec((None, None, DA_WIDTH, bq), lambda i: (i, 0, 0, 0)),
                  cache_spec, cache_spec,
                  pl.BlockSpec((None, bq, DA_WIDTH), lambda i: (i, 0, 0)),
                  pl.BlockSpec((None, None, DA_WIDTH, bq), lambda i: (i, 0, 0, 0)),
                  _const_spec((DA_HEADS, tq, bk)), _const_spec((DA_HEADS, tq, bq)),
                  _const_spec((1, DA_DV))],
        out_specs=pl.BlockSpec((None, bq, DA_WIDTH), lambda i: (i, 0, 0)),
        out_shape=jax.ShapeDtypeStruct((b, bq, DA_WIDTH), BF16),
        compiler_params=pltpu.CompilerParams(dimension_semantics=("parallel",), vmem_limit_bytes=VMEM_LIMIT),
        name="attn_cached",
    )(scal, qt, cache_k.reshape(b, rows, DA_DV), cache_v.reshape(b, rows, DA_DV), kb, vt, bias0, bias1, subw)


def _rel_bucket(rel):
    nb = REL_BUCKETS // 2
    max_exact = nb // 2
    n = jnp.abs(rel)
    nf = jnp.maximum(n, 1).astype(jnp.float32)
    large = max_exact + (jnp.log(nf / max_exact) / math.log(REL_MAX_DIST / max_exact)
                         * (nb - max_exact)).astype(jnp.int32)
    large = jnp.minimum(large, nb - 1)
    return jnp.where(rel > 0, nb, 0) + jnp.where(n < max_exact, n, large)


def _bias_tables(rel_bias, bq, bk):
    tabs = []
    for d in range(2):
        offs = (d - 1) * bk + bk - np.arange(bq + bk)
        tabs.append(rel_bias[_rel_bucket(jnp.asarray(offs, jnp.int32))].astype(F32).T * LOG2E)
    return jnp.stack(tabs, axis=1)[:, :, None, :]


def _bias_rows(rel_bias, qpos0, tq, kpos0, nkeys, tk_real):
    offs = (kpos0 - qpos0) + np.arange(-(tq - 1), nkeys)
    table = rel_bias[_rel_bucket(jnp.asarray(offs, jnp.int32))].astype(F32).T * LOG2E
    rows = jnp.stack([table[:, tq - 1 - i:tq - 1 - i + nkeys] for i in range(tq)], axis=1)
    qpos = qpos0 + np.arange(tq)
    kpos = kpos0 + np.arange(nkeys)
    vis = (kpos[None, :] // CHUNK <= qpos[:, None] // CHUNK) & (kpos[None, :] < tk_real)
    return jnp.where(jnp.asarray(vis)[None], rows, NEG)


def _ffn_kernel(x_ref, ys_ref, ya_ref, mod_ref, wo1_ref, wo2_ref, nfw_ref, wu_ref, cw_ref, cb_ref, wd_ref, fw_ref,
                hist_ref, y_ref, tail_ref, tail_scr, buf_a, buf_b, x1_scr, h2_scr, act_scr, *, nseq, tm):
    t = pl.program_id(1)
    nh = FFN_CONV - 1
    lo = SUBLANES - nh
    seqs = range(nseq)

    def rows(i):
        return slice(i * tm, (i + 1) * tm)

    def stacked(ref):
        return jnp.concatenate([ref[i] for i in seqs], axis=0) if nseq > 1 else ref[0]

    @pl.when(t == 0)
    def _init():
        for i in seqs:
            tail_scr[i, lo:SUBLANES, :] = hist_ref[i]

    mix = (jnp.dot(stacked(ys_ref), wo1_ref[...], preferred_element_type=F32)
           + jnp.dot(stacked(ya_ref), wo2_ref[...], preferred_element_type=F32))
    for i in seqs:
        x1 = x_ref[i] + mod_ref[i, 2:3, :] * mix[rows(i)]
        x1_scr[rows(i), :] = x1
        h2 = _rms(x1, nfw_ref[...]) * (1.0 + mod_ref[i, 4:5, :]) + mod_ref[i, 3:4, :]
        h2_scr[rows(i), :] = h2.astype(BF16)

    def cols(j, half):
        return slice(half * D_FF + j * FFN_CN, half * D_FF + (j + 1) * FFN_CN)

    def up(j, buf):
        for half in range(2):
            u = jnp.dot(h2_scr[...], wu_ref[:, cols(j, half)], preferred_element_type=F32)
            for i in seqs:
                buf[half, i, SUBLANES:SUBLANES + tm, :] = u[rows(i)]

    def conv(j, buf, half, i):
        cs = cols(j, half)
        buf[half, i, lo:SUBLANES, :] = tail_scr[i, lo:SUBLANES, cs]
        c = cb_ref[:, cs]
        for k in range(FFN_CONV):
            c = c + cw_ref[k:k + 1, cs] * buf[half, i, lo + k:lo + k + tm, :]
        tail_scr[i, lo:SUBLANES, cs] = buf[half, i, lo + tm:SUBLANES + tm, :]
        return c

    bufs = (buf_a, buf_b)

    @pl.when(t >= 0)
    def _chunks():
        up(0, bufs[0])
        for j in range(FFN_NC):
            if j + 1 < FFN_NC:
                up(j + 1, bufs[(j + 1) % 2])
            for i in seqs:
                cv = conv(j, bufs[j % 2], 0, i)
                cg = conv(j, bufs[j % 2], 1, i)
                act_scr[rows(i), j * FFN_CN:(j + 1) * FFN_CN] = (_silu(cg) * cv).astype(BF16)

    f = jnp.dot(act_scr[...], wd_ref[...], preferred_element_type=F32)
    for i in seqs:
        x2 = x1_scr[rows(i), :] + mod_ref[i, 5:6, :] * f[rows(i)]
        y_ref[i] = _rms(x2, fw_ref[...])
        tail_ref[i] = tail_scr[i, lo:SUBLANES, :]


def _ffn_call(x, ys, ya, mod3, wo1, wo2, nfw, wu, cw, cb, wd, fw, hist, *, nseq, tm):
    b, t, d = x.shape
    nh = FFN_CONV - 1
    m = nseq * tm

    def row(width):
        return pl.BlockSpec((nseq, tm, width), lambda i, j: (i, j, 0))

    hist_spec = pl.BlockSpec((nseq, nh, 2 * D_FF), lambda i, j: (i, 0, 0))
    kern = functools.partial(_ffn_kernel, nseq=nseq, tm=tm)
    return pl.pallas_call(
        kern,
        grid=(b // nseq, t // tm),
        in_specs=[row(d), row(SSD_WIDTH), row(DA_WIDTH),
                  pl.BlockSpec((nseq, 6, d), lambda i, j: (i, 0, 0)),
                  _const_spec((SSD_WIDTH, d)), _const_spec((DA_WIDTH, d)), _const_spec((1, d)),
                  _const_spec((d, 2 * D_FF)), _const_spec((FFN_CONV, 2 * D_FF)), _const_spec((1, 2 * D_FF)),
                  _const_spec((D_FF, d)), _const_spec((1, d)),
                  hist_spec],
        out_specs=[row(d), hist_spec],
        out_shape=[jax.ShapeDtypeStruct((b, t, d), F32),
                   jax.ShapeDtypeStruct((b, nh, 2 * D_FF), F32)],
        scratch_shapes=[pltpu.VMEM((nseq, SUBLANES, 2 * D_FF), F32),
                        pltpu.VMEM((2, nseq, tm + SUBLANES, FFN_CN), F32),
                        pltpu.VMEM((2, nseq, tm + SUBLANES, FFN_CN), F32),
                        pltpu.VMEM((m, d), F32), pltpu.VMEM((m, d), BF16), pltpu.VMEM((m, D_FF), BF16)],
        compiler_params=pltpu.CompilerParams(dimension_semantics=("parallel", "arbitrary"),
                                             vmem_limit_bytes=VMEM_LIMIT),
        name="ffn",
    )(x, ys, ya, mod3, wo1, wo2, nfw, wu, cw, cb, wd, fw, hist)


def _pack_params(norm_mix_w, w_in, ssm_conv_w, ssm_conv_b, ssm_dt_bias, ssm_a_log, ssm_d, ssm_norm_w,
                 lambda_q1, lambda_k1, lambda_q2, lambda_k2, attn_subln_w, rel_bias, w_out,
                 norm_ffn_w, w_up, ffn_conv_w, ffn_conv_b, w_down, final_norm_w, layer):
    l = layer
    wz, wx, wdt, wq, wk, wv = jnp.split(w_in[l], IN_SPLITS, axis=-1)
    wdt = jnp.pad(wdt, ((0, 0), (0, LANES - SSD_HEADS)))
    w_cat = jnp.concatenate([wz, wx, wdt, wk, wv], axis=-1).astype(BF16)
    w_t = (wq * (DA_DK ** -0.5 * LOG2E)).T.astype(BF16)

    def pad_heads(v):
        return jnp.pad(v.astype(F32), (0, LANES - SSD_HEADS)).reshape(1, LANES)

    lam_init = 0.8 - 0.6 * math.exp(-0.3 * l)
    lam = (jnp.exp(jnp.sum(lambda_q1[l].astype(F32) * lambda_k1[l].astype(F32)))
           - jnp.exp(jnp.sum(lambda_q2[l].astype(F32) * lambda_k2[l].astype(F32))) + lam_init)
    far_bias = rel_bias[REL_BUCKETS // 2 - 1].astype(F32)
    return dict(
        norm_mix_w=norm_mix_w[l].reshape(1, D_MODEL), w_cat=w_cat, w_t=w_t,
        cw=ssm_conv_w[l], cbias=ssm_conv_b[l].reshape(1, SSD_CONV_DIM),
        dtb=pad_heads(ssm_dt_bias[l]), alog=pad_heads(ssm_a_log[l]),
        dsk=jnp.repeat(ssm_d[l].astype(F32), SSD_HEADDIM).reshape(1, SSD_WIDTH),
        ssm_nw=ssm_norm_w[l].reshape(1, SSD_WIDTH),
        scal=jnp.concatenate([lam.reshape(1), far_bias * LOG2E]).astype(F32), lam_init=lam_init,
        subw=attn_subln_w[l].reshape(DA_DV, 1), rel_bias=rel_bias,
        wo1=w_out[l][:SSD_WIDTH].astype(BF16), wo2=w_out[l][SSD_WIDTH:].astype(BF16),
        nfw=norm_ffn_w[l].reshape(1, D_MODEL),
        wu=w_up[l].astype(BF16), ffn_cw=ffn_conv_w[l], ffn_cb=ffn_conv_b[l].reshape(1, 2 * D_FF),
        wd=w_down[l].astype(BF16), fw=final_norm_w.reshape(1, D_MODEL),
    )


def _state_to_kernel(h):
    return h.reshape(h.shape[0], SSD_GROUPS, GROUP_W, SSD_STATE)


def _state_from_kernel(h):
    return h.reshape(h.shape[0], SSD_HEADS, SSD_HEADDIM, SSD_STATE)


def _run_group(x, mod, past_k, past_v, ssm_h0, ssm_conv_hist, ffn_conv_hist, p, *, tm, ssd_rows, bq, bk):
    b, t, d = x.shape
    past = 0 if past_k is None else past_k.shape[1]
    chunk = min(CHUNK, t)
    tp = max(t, SUPER)
    if tp != t:
        x = jnp.pad(x, ((0, 0), (0, tp - t), (0, 0)))
        tm = ssd_rows = bq = tp
    mod3 = mod.reshape(b, 6, d)

    assert tm == bq
    zs, xc, dt, k, v, kb, qt, vt, conv_new = _inproj_call(
        x, mod3, p["norm_mix_w"], p["w_cat"], p["w_t"], ssm_conv_hist.astype(F32), p["cw"], p["cbias"],
        tm=tm, real=min(t, tm))

    y_ssd, h_t = _ssd_call(zs, xc, dt, _state_to_kernel(ssm_h0.astype(F32)),
                           p["dtb"], p["alog"], p["dsk"], p["ssm_nw"],
                           chunk=chunk, rows=ssd_rows, real=min(t, ssd_rows))

    if past == 0:
        assert bq == bk and t % bq == 0
        y_att = _attn_call(p["scal"], qt, kb, vt, _bias_tables(p["rel_bias"], bq, bk), p["subw"], bq=bq, bk=bk,
                           out_scale=1.0 - p["lam_init"])
    else:
        assert past % bk == 0 and past >= bk and bq == tp <= bk
        bias0 = _bias_rows(p["rel_bias"], past, t, past - bk, bk, past + t)
        bias1 = _bias_rows(p["rel_bias"], past, t, past, bq, past + t)
        y_att = _attn_cached_call(p["scal"], qt, past_k, past_v, kb, vt, bias0, bias1, p["subw"].reshape(1, DA_DV),
                                  tq=t, bk=bk, out_scale=1.0 - p["lam_init"])

    tf = min(t, tm)
    nseq = math.gcd(b, max(1, TILE_ROWS // tf))
    y, ffn_new = _ffn_call(x[:, :t], y_ssd[:, :t], y_att[:, :t], mod3, p["wo1"], p["wo2"], p["nfw"], p["wu"],
                           p["ffn_cw"], p["ffn_cb"], p["wd"], p["fw"], ffn_conv_hist.astype(F32),
                           nseq=nseq, tm=tf)
    k = k.reshape(b, tp, DA_HEADS, 2 * DA_DK)
    v = v.reshape(b, tp, DA_HEADS, DA_DV)
    return (y, k[:, :t], v[:, :t],
            _state_from_kernel(h_t), conv_new, ffn_new)


def kernel(x_prompt, x_sample, c_prompt, c_sample, cache_k, cache_v, state_ssm, state_ssm_conv, state_ffn_conv, w_ada, b_ada, norm_mix_w, w_in, ssm_conv_w, ssm_conv_b, ssm_dt_bias, ssm_a_log, ssm_d, ssm_norm_w, lambda_q1, lambda_k1, lambda_q2, lambda_k2, attn_subln_w, rel_bias, w_out, norm_ffn_w, w_up, ffn_conv_w, ffn_conv_b, w_down, final_norm_w):
    bp, bs = x_prompt.shape[0], x_sample.shape[0]
    dt = x_prompt.dtype
    p = _pack_params(norm_mix_w, w_in, ssm_conv_w, ssm_conv_b, ssm_dt_bias, ssm_a_log, ssm_d, ssm_norm_w,
                     lambda_q1, lambda_k1, lambda_q2, lambda_k2, attn_subln_w, rel_bias, w_out,
                     norm_ffn_w, w_up, ffn_conv_w, ffn_conv_b, w_down, final_norm_w, 0)
    c_all = jnp.concatenate([c_prompt, c_sample], axis=0)
    npad = -c_all.shape[0] % SUBLANES
    c_all = jnp.pad(c_all, ((0, npad), (0, 0)))
    mod = _mod_call(c_all, w_ada[0], b_ada[0].reshape(1, -1))

    zeros = lambda *s: jnp.zeros(s, dt)
    out_p = _run_group(x_prompt, mod[:bp], None, None,
                       zeros(bp, SSD_HEADS, SSD_HEADDIM, SSD_STATE), zeros(bp, SSD_CONV - 1, SSD_CONV_DIM),
                       zeros(bp, FFN_CONV - 1, 2 * D_FF), p,
                       tm=TILE_ROWS, ssd_rows=SSD_ROWS, bq=ATT_BLOCK, bk=ATT_BLOCK)
    out_s = _run_group(x_sample, mod[bp:bp + bs], cache_k[0], cache_v[0], state_ssm[0], state_ssm_conv[0],
                       state_ffn_conv[0], p, tm=SUPER, ssd_rows=SUPER, bq=SUPER, bk=ATT_BLOCK)
    y_p, k_p, v_p, h_p, c_p, f_p = out_p
    y_s, k_s, v_s, h_s, c_s, f_s = out_s
    return (y_p, y_s, k_p[None], v_p[None], h_p[None], c_p[None], f_p[None],
            k_s[None], v_s[None], h_s[None], c_s[None], f_s[None])
```

```python
import functools
import math

import numpy as np
import jax
import jax.numpy as jnp
from jax import lax
from jax.experimental import pallas as pl
from jax.experimental.pallas import tpu as pltpu

F32 = jnp.float32
BF16 = jnp.bfloat16

D_MODEL = 1024
CHUNK = 64
SSD_WIDTH = 512
SSD_HEADDIM = 64
SSD_HEADS = 8
SSD_GROUPS = 2
SSD_HPG = 4
SSD_STATE = 128
SSD_CONV = 4
SSD_CONV_DIM = SSD_WIDTH + 2 * SSD_GROUPS * SSD_STATE
GROUP_W = SSD_HPG * SSD_HEADDIM
DA_WIDTH = 512
DA_DK = 64
DA_DV = 128
DA_HEADS = 4
REL_BUCKETS = 32
REL_MAX_DIST = 128
D_FF = 2816
FFN_CONV = 3
EPS = 1e-6
IN_SPLITS = (512, 1536, 1544, 2056, 2568)
LANES = 128
SUBLANES = 8
TILE_ROWS = 512
MOD_TN = 1024
SUPER = 128
SSD_ROWS = 256
SSD_NSEQ = 4
ATT_BLOCK = 512
ATT_KT = 64
FFN_CN = 256
FFN_NC = D_FF // FFN_CN
NEG = -1e30
VMEM_LIMIT = 56 * 1024 * 1024

PZ, PX, PDT, PK, PV, PEND = 0, 512, 1536, 1664, 2176, 2688
LOG2E = math.log2(math.e)


def _silu(x):
    return x / (1.0 + jnp.exp(-x))


def _softplus(x):
    return jnp.maximum(x, 0.0) + jnp.log1p(jnp.exp(-jnp.abs(x)))


def _split3(x):
    hi = x.astype(BF16)
    r1 = x - hi.astype(F32)
    mid = r1.astype(BF16)
    lo = (r1 - mid.astype(F32)).astype(BF16)
    return hi, mid, lo


def _rms(x, w):
    return x * lax.rsqrt(jnp.mean(x * x, axis=-1, keepdims=True) + EPS) * w


def _const_spec(shape):
    nd = len(shape)
    return pl.BlockSpec(shape, lambda *_: (0,) * nd)


def _mod_kernel(c_ref, w_ref, b_ref, o_ref):
    a = _silu(c_ref[...]).astype(BF16)
    o_ref[...] = jnp.dot(a, w_ref[...].astype(BF16), preferred_element_type=F32) + b_ref[...]


def _mod_call(c, w_ada, b_ada):
    n, d = c.shape
    nout = w_ada.shape[1]
    tn = MOD_TN
    return pl.pallas_call(
        _mod_kernel,
        grid=(nout // tn,),
        in_specs=[pl.BlockSpec((n, d), lambda j: (0, 0)),
                  pl.BlockSpec((d, tn), lambda j: (0, j)),
                  pl.BlockSpec((1, tn), lambda j: (0, j))],
        out_specs=pl.BlockSpec((n, tn), lambda j: (0, j)),
        out_shape=jax.ShapeDtypeStruct((n, nout), F32),
        name="mod",
    )(c, w_ada, b_ada)


def _inproj_kernel(x_ref, mod_ref, nw_ref, w_ref, wt_ref, hist_ref, cw_ref, cbias_ref,
                   zs_ref, xc_ref, dt_ref, k_ref, v_ref, kb_ref, qt_ref, vt_ref, cout_ref, cbuf, hb_scr, zbuf,
                   *, tm, real):
    t = pl.program_id(1)
    nconv = SSD_CONV - 1

    @pl.when(t == 0)
    def _init():
        cbuf[0:SUBLANES, :] = jnp.zeros((SUBLANES, SSD_CONV_DIM), F32)
        cbuf[SUBLANES - nconv:SUBLANES, :] = hist_ref[...]

    h = _rms(x_ref[...], nw_ref[...]) * (1.0 + mod_ref[1:2, :]) + mod_ref[0:1, :]
    hb_scr[...] = h.astype(BF16)

    def proj(a, b):
        return jnp.dot(hb_scr[...], w_ref[:, a:b], preferred_element_type=F32)

    def proj_t(a, b):
        return lax.dot_general(wt_ref[a:b, :], hb_scr[...], (((1,), (1,)), ((), ())), preferred_element_type=F32)

    cbuf[SUBLANES:SUBLANES + tm, :] = proj(PX, PDT)
    zbuf[...] = proj(PZ, PX)
    dt_ref[...] = proj(PDT, PK)
    k = proj(PK, PV)
    v = proj(PV, PEND)
    for hd in range(DA_HEADS):
        dst = pl.ds(hd, tm, stride=DA_HEADS)
        k_ref[dst, :] = k[:, hd * DA_DV:(hd + 1) * DA_DV]
        v_ref[dst, :] = v[:, hd * DA_DV:(hd + 1) * DA_DV]
    kb_ref[...] = k.astype(BF16)
    qt_ref[...] = proj_t(0, DA_WIDTH).astype(BF16)
    for hd in range(DA_HEADS):
        vt_ref[hd * DA_DV:(hd + 1) * DA_DV, :] = v[:, hd * DA_DV:(hd + 1) * DA_DV].T.astype(BF16)
    conv = cbias_ref[...]
    for j in range(SSD_CONV):
        off = SUBLANES - nconv + j
        conv = conv + cw_ref[j:j + 1, :] * cbuf[off:off + tm, :]
    tail = cbuf[SUBLANES - nconv + real:SUBLANES + real, :]
    cout_ref[...] = tail
    cbuf[SUBLANES - nconv:SUBLANES, :] = tail
    xc_ref[...] = _silu(conv).astype(BF16)
    zs_ref[...] = _silu(zbuf[...]).astype(BF16)


def _inproj_call(x, mod3, norm_w, w_cat, w_t, hist, cw, cbias, *, tm, real):
    b, t, d = x.shape
    nt = t // tm

    def row(width):
        return pl.BlockSpec((None, tm, width), lambda i, j: (i, j, 0))

    def out(width, dtype):
        return jax.ShapeDtypeStruct((b, t, width), dtype)

    tspec = pl.BlockSpec((None, None, DA_WIDTH, tm), lambda i, j: (i, j, 0, 0))
    tshape = jax.ShapeDtypeStruct((b, nt, DA_WIDTH, tm), BF16)
    hist_spec = pl.BlockSpec((None, SSD_CONV - 1, SSD_CONV_DIM), lambda i, j: (i, 0, 0))
    hspec = pl.BlockSpec((None, tm * DA_HEADS, DA_DV), lambda i, j: (i, j, 0))
    hshape = jax.ShapeDtypeStruct((b, t * DA_HEADS, DA_DV), F32)
    return pl.pallas_call(
        functools.partial(_inproj_kernel, tm=tm, real=real),
        grid=(b, nt),
        in_specs=[row(d),
                  pl.BlockSpec((None, 6, d), lambda i, j: (i, 0, 0)),
                  _const_spec((1, d)),
                  _const_spec((d, PEND)),
                  _const_spec((DA_WIDTH, d)),
                  hist_spec, _const_spec((SSD_CONV, SSD_CONV_DIM)), _const_spec((1, SSD_CONV_DIM))],
        out_specs=[row(512), row(1024), row(LANES), hspec, hspec, row(512), tspec, tspec, hist_spec],
        out_shape=[out(512, BF16), out(1024, BF16), out(LANES, F32),
                   hshape, hshape, out(512, BF16), tshape, tshape,
                   jax.ShapeDtypeStruct((b, SSD_CONV - 1, SSD_CONV_DIM), F32)],
        scratch_shapes=[pltpu.VMEM((tm + SUBLANES, SSD_CONV_DIM), F32), pltpu.VMEM((tm, d), BF16),
                        pltpu.VMEM((tm, SSD_WIDTH), F32)],
        compiler_params=pltpu.CompilerParams(dimension_semantics=("parallel", "arbitrary"),
                                             vmem_limit_bytes=VMEM_LIMIT),
        name="inproj",
    )(x, mod3, norm_w, w_cat, w_t, hist, cw, cbias)


def _ssd_kernel(zs_ref, xc_ref, dt_ref, h0_ref, dtb_ref, alog_ref, dsk_ref, nw_ref, tri_ref, e_ref,
                y_ref, hout_ref, h_scr, ybuf, *, nseq, chunk, rows, real):
    t = pl.program_id(1)
    seqs = range(nseq)

    @pl.when(t == 0)
    def _init():
        for i in seqs:
            for g in range(SSD_GROUPS):
                h_scr[i, g] = h0_ref[i, g].T

    li = lax.broadcasted_iota(jnp.int32, (SUPER, SUPER), 0)
    si = lax.broadcasted_iota(jnp.int32, (SUPER, SUPER), 1)
    cshift = chunk.bit_length() - 1
    mask2 = ((li >> cshift) == (si >> cshift)) & (si <= li)
    lane_g = lax.broadcasted_iota(jnp.int32, (SUPER, GROUP_W), 1) >> (SSD_HEADDIM.bit_length() - 1)

    pre = []
    for i in seqs:
        xs = xc_ref[i, :, 0:SSD_WIDTH].astype(F32)
        dtv = _softplus(dt_ref[i] + dtb_ref[...])
        da = dtv * (-jnp.exp(alog_ref[...]))
        acs = jnp.dot(tri_ref[...], jnp.concatenate(_split3(da), axis=0), preferred_element_type=F32)
        dt_x = jnp.dot(jnp.concatenate(_split3(dtv), axis=1), e_ref[...], preferred_element_type=F32)
        acs_x = jnp.dot(jnp.concatenate(_split3(acs), axis=1), e_ref[...], preferred_element_type=F32)
        pre.append((None, acs, acs_x, jnp.exp(acs_x), xs * dt_x))

    for sb in range(rows // SUPER):
        o = sb * SUPER
        nreal = (min(real, o + SUPER) - o) // chunk
        acs2 = [pre[i][1][o:o + SUPER, :] for i in seqs]
        acs_t = [a.T for a in acs2]
        chains = [(g, i) for g in range(SSD_GROUPS) for i in seqs]

        def gsl(g):
            return slice(g * GROUP_W, (g + 1) * GROUP_W)

        cmb, cb2, bm_t = {}, {}, {}
        for g, i in chains:
            bcol = SSD_WIDTH + g * SSD_STATE
            ccol = SSD_WIDTH + (SSD_GROUPS + g) * SSD_STATE
            bmb = xc_ref[i, o:o + SUPER, bcol:bcol + SSD_STATE]
            cmb[g, i] = xc_ref[i, o:o + SUPER, ccol:ccol + SSD_STATE]
            cb2[g, i] = lax.dot_general(cmb[g, i], bmb, (((1,), (1,)), ((), ())), preferred_element_type=F32)
            bm_t[g, i] = bmb.astype(F32).T.astype(BF16)
        for g, i in chains:
            ms = []
            for rr in range(SSD_HPG):
                r = g * SSD_HPG + rr
                seg = acs2[i][:, r:r + 1] - acs_t[i][r:r + 1, :]
                dec = jnp.where(mask2, jnp.exp(jnp.where(mask2, seg, 0.0)), 0.0)
                ms.append((cb2[g, i] * dec).astype(BF16))
            full = jnp.dot(jnp.concatenate(ms, axis=0), pre[i][4][o:o + SUPER, gsl(g)].astype(BF16),
                           preferred_element_type=F32)
            ydiag = full[0:SUPER]
            for rr in range(1, SSD_HPG):
                ydiag = jnp.where(lane_g == rr, full[rr * SUPER:(rr + 1) * SUPER], ydiag)
            ybuf[i, o:o + SUPER, gsl(g)] = ydiag
        for j in range(nreal):
            a0, a1 = o + j * chunk, o + (j + 1) * chunk
            h_t = {c: h_scr[c[1], c[0]] for c in chains}
            yoff = {(g, i): jnp.dot(cmb[g, i][j * chunk:(j + 1) * chunk, :], h_t[g, i].astype(BF16),
                                    preferred_element_type=F32) for g, i in chains}
            st = {}
            for g, i in chains:
                _, _, acs_x, eacs_x, xd = pre[i]
                ybuf[i, a0:a1, gsl(g)] = ybuf[i, a0:a1, gsl(g)] + yoff[g, i] * eacs_x[a0:a1, gsl(g)]
                dte = jnp.exp(acs_x[a1 - 1:a1, gsl(g)] - acs_x[a0:a1, gsl(g)])
                xw = (xd[a0:a1, gsl(g)] * dte).astype(BF16)
                pieces = []
                if j > 0:
                    pieces.append(jnp.zeros((j * chunk, GROUP_W), BF16))
                pieces.append(xw)
                if (j + 1) * chunk < SUPER:
                    pieces.append(jnp.zeros((SUPER - (j + 1) * chunk, GROUP_W), BF16))
                xw2 = jnp.concatenate(pieces, axis=0) if len(pieces) > 1 else xw
                st[g, i] = jnp.dot(bm_t[g, i], xw2, preferred_element_type=F32)
            for g, i in chains:
                h_scr[i, g] = h_t[g, i] * pre[i][3][a1 - 1:a1, gsl(g)] + st[g, i]

    for i in seqs:
        y = (ybuf[i] + dsk_ref[...] * xc_ref[i, :, 0:SSD_WIDTH].astype(F32)) * zs_ref[i].astype(F32)
        for g in range(SSD_GROUPS):
            gs = slice(g * GROUP_W, (g + 1) * GROUP_W)
            y_ref[i, :, gs] = _rms(y[:, gs], nw_ref[:, gs]).astype(BF16)

    @pl.when(t == pl.num_programs(1) - 1)
    def _fin():
        for i in seqs:
            for g in range(SSD_GROUPS):
                hout_ref[i, g] = h_scr[i, g].T


def _ssd_call(zs, xc, dt, h0_t, dtb, alog, dsk, nw, *, chunk, rows, real):
    b, t, _ = zs.shape
    ii = np.arange(rows)
    tri = ((ii[:, None] // chunk == ii[None, :] // chunk) & (ii[None, :] <= ii[:, None])).astype(np.float32)
    e = np.zeros((LANES, SSD_WIDTH), np.float32)
    for r in range(SSD_HEADS):
        e[r, r * SSD_HEADDIM:(r + 1) * SSD_HEADDIM] = 1.0

    nseq = math.gcd(b, SSD_NSEQ)

    def row(width):
        return pl.BlockSpec((nseq, rows, width), lambda i, j: (i, j, 0))

    state_spec = pl.BlockSpec((nseq, SSD_GROUPS, GROUP_W, SSD_STATE), lambda i, j: (i, 0, 0, 0))
    kern = functools.partial(_ssd_kernel, nseq=nseq, chunk=chunk, rows=rows, real=real)
    return pl.pallas_call(
        kern,
        grid=(b // nseq, t // rows),
        in_specs=[row(SSD_WIDTH), row(SSD_CONV_DIM), row(LANES), state_spec,
                  _const_spec((1, LANES)), _const_spec((1, LANES)),
                  _const_spec((1, SSD_WIDTH)), _const_spec((1, SSD_WIDTH)),
                  _const_spec((rows, 3 * rows)), _const_spec((3 * LANES, SSD_WIDTH))],
        out_specs=[row(SSD_WIDTH), state_spec],
        out_shape=[jax.ShapeDtypeStruct((b, t, SSD_WIDTH), BF16),
                   jax.ShapeDtypeStruct((b, SSD_GROUPS, GROUP_W, SSD_STATE), F32)],
        scratch_shapes=[pltpu.VMEM((nseq, SSD_GROUPS, SSD_STATE, GROUP_W), F32),
                        pltpu.VMEM((nseq, rows, SSD_WIDTH), F32)],
        compiler_params=pltpu.CompilerParams(dimension_semantics=("parallel", "arbitrary"),
                                             vmem_limit_bytes=VMEM_LIMIT),
        name="ssd",
    )(zs, xc, dt, h0_t, dtb, alog, dsk, nw,
      jnp.asarray(np.tile(tri, (1, 3)), BF16), jnp.asarray(np.tile(e, (3, 1)), BF16))


def _attn_kernel(scal_ref, qt_ref, k_ref, vt_ref, btab_ref, subw_ref, o_ref,
                 m_scr, l_scr, acc_scr, sa_scr, sb_scr, sc_scr, pa_scr, pb_scr, alpha_scr, qz_scr, bias_scr,
                 *, bq, bk, out_scale):
    h = pl.program_id(0)
    qi = pl.program_id(2)
    kn0 = qi - 1
    lam = scal_ref[0]
    cfar = scal_ref[1 + h]

    @pl.when((pl.program_id(1) == 0) & (qi == 0))
    def _build_bias_tiles():
        width = bq + bk
        kj = lax.broadcasted_iota(jnp.int32, (bk, bq), 0)
        qj = lax.broadcasted_iota(jnp.int32, (bk, bq), 1)
        cshift = CHUNK.bit_length() - 1
        for d in range(2):
            skew = pltpu.roll(jnp.broadcast_to(btab_ref[d], (bk, width)), 0, 1, stride=1, stride_axis=0)
            tile = skew[:, bk:width]
            if d == 1:
                tile = jnp.where((kj >> cshift) <= (qj >> cshift), tile, NEG)
            bias_scr[d] = tile

    zero = jnp.zeros((DA_DK, bq), BF16)
    qz_scr[0, 0:DA_DK, :] = qt_ref[0:DA_DK, :]
    qz_scr[0, DA_DK:DA_DV, :] = zero
    qz_scr[1, 0:DA_DK, :] = zero
    qz_scr[1, DA_DK:DA_DV, :] = qt_ref[DA_DK:DA_DV, :]

    m_scr[...] = jnp.full(m_scr.shape, NEG, F32)
    l_scr[...] = jnp.zeros(l_scr.shape, F32)
    acc_scr[...] = jnp.zeros(acc_scr.shape, F32)

    nsub = bk // ATT_KT

    def fold(x):
        return x.reshape(ATT_KT // SUBLANES, SUBLANES, x.shape[1])

    def scores(s_buf, first, count, own_last=False):
        half = bk // 2
        for mm in range(2):
            for e in range(count):
                start = pl.multiple_of((first + e) * bk, bk)
                if own_last and e == count - 1 and half % LANES == 0:
                    s_buf[mm, e, 0:half, :] = jnp.dot(k_ref[pl.ds(start, half), :], qz_scr[mm],
                                                      preferred_element_type=F32)
                    s_buf[mm, e, half:bk, half:bq] = jnp.dot(k_ref[pl.ds(start + half, half), :],
                                                             qz_scr[mm, :, half:bq], preferred_element_type=F32)
                else:
                    s_buf[mm, e] = jnp.dot(k_ref[pl.ds(start, bk), :], qz_scr[mm], preferred_element_type=F32)

    def first_visible(near, t):
        if near != 1:
            return 0
        return (t * ATT_KT // CHUNK) * CHUNK // LANES * LANES

    def merge(acc, part, c0, op):
        if acc is None:
            return part
        if c0 == 0:
            return op(acc, part)
        return jnp.concatenate([acc[:, :c0], op(acc[:, c0:], part)], axis=1)

    def softmax(s_buf, p_buf, entries):
        alphas = []
        for mm in range(2):
            cand = None
            for e, (near, shift) in enumerate(entries):
                mx = None
                for t in range(nsub):
                    rows = slice(t * ATT_KT, (t + 1) * ATT_KT)
                    c0 = first_visible(near, t)
                    s = s_buf[mm, e, rows, c0:]
                    if near is not None:
                        s = s + bias_scr[near, rows, c0:]
                    mx = merge(mx, jnp.max(fold(s), axis=0), c0, jnp.maximum)
                mx = jnp.max(mx, axis=0, keepdims=True) + shift
                cand = mx if cand is None else jnp.maximum(cand, mx)
            m_old = m_scr[mm]
            m_new = jnp.maximum(m_old, cand)
            ls = None
            for e, (near, shift) in enumerate(entries):
                off = m_new - shift
                for t in range(nsub):
                    rows = slice(t * ATT_KT, (t + 1) * ATT_KT)
                    prow = slice(e * bk + t * ATT_KT, e * bk + (t + 1) * ATT_KT)
                    c0 = first_visible(near, t)
                    s = s_buf[mm, e, rows, c0:]
                    if near is not None:
                        s = s + bias_scr[near, rows, c0:]
                    p = jnp.exp2(s - off[:, c0:])
                    if c0:
                        p_buf[mm, prow, 0:c0] = jnp.zeros((ATT_KT, c0), BF16)
                    p_buf[mm, prow, c0:] = p.astype(BF16)
                    ls = merge(ls, jnp.sum(fold(p), axis=0), c0, jnp.add)
            alpha = jnp.exp2(m_old - m_new)
            l_scr[mm] = alpha * l_scr[mm] + jnp.sum(ls, axis=0, keepdims=True)
            m_scr[mm] = m_new
            alphas.append(alpha)
        return alphas

    def pv(p_buf, first, n):
        vts = [vt_ref[first + e] for e in range(n)]
        vt = jnp.concatenate(vts, axis=1) if n > 1 else vts[0]
        return [jnp.dot(vt, p_buf[mm, 0:n * bk, :], preferred_element_type=F32) for mm in range(2)]

    def accumulate(alphas, pvs):
        for mm in range(2):
            acc_scr[mm] = alphas[mm] * acc_scr[mm] + pvs[mm]

    def softmax_pv(s_buf, first, entries):
        alphas = softmax(s_buf, pa_scr, entries)
        accumulate(alphas, pv(pa_scr, first, len(entries)))

    nfar = jnp.maximum(kn0, 0)
    odd = nfar % 2
    far = (None, cfar)

    near_pair = [(0, 0.0), (1, 0.0)]

    @pl.when(kn0 < 0)
    def _only_first():
        scores(sc_scr, 0, 1, own_last=True)
        softmax_pv(sc_scr, 0, [(1, 0.0)])

    @pl.when(kn0 >= 0)
    def _groups():
        @pl.when(odd == 1)
        def _single():
            scores(sc_scr, 0, 1)
            scores(sa_scr, 1, 2)
            softmax_pv(sc_scr, 0, [far])

        @pl.when(odd == 0)
        def _first_pair():
            scores(sa_scr, 0, 2)

        def far_step(s_cur, p_cur, s_next, p_prev, cur):
            scores(s_next, cur + 2, 2)
            pending = None if p_prev is None else pv(p_prev, cur - 2, 2)
            alphas = softmax(s_cur, p_cur, [far, far])
            if pending is not None:
                accumulate([alpha_scr[0], alpha_scr[1]], pending)
            for mm in range(2):
                alpha_scr[mm] = alphas[mm]

        npairs = nfar // 2
        niter = npairs // 2

        @pl.when(niter >= 1)
        def _far_loop():
            far_step(sa_scr, pa_scr, sb_scr, None, odd)
            far_step(sb_scr, pb_scr, sa_scr, pa_scr, odd + 2)

            def far_body(j, carry):
                cur = odd + 4 * j
                far_step(sa_scr, pa_scr, sb_scr, pb_scr, cur)
                far_step(sb_scr, pb_scr, sa_scr, pa_scr, cur + 2)
                return carry

            lax.fori_loop(1, niter, far_body, 0)
            accumulate([alpha_scr[0], alpha_scr[1]], pv(pb_scr, odd + 4 * niter - 2, 2))

        @pl.when(npairs % 2 == 1)
        def _tail_b():
            scores(sb_scr, kn0, 2, own_last=True)
            softmax_pv(sa_scr, kn0 - 2, [far, far])
            softmax_pv(sb_scr, kn0, near_pair)

        @pl.when(npairs % 2 == 0)
        def _tail_a():
            softmax_pv(sa_scr, kn0, near_pair)

    o = acc_scr[0] * (1.0 / l_scr[0]) - acc_scr[1] * (lam / l_scr[1])
    o = o * lax.rsqrt(jnp.mean(o * o, axis=0, keepdims=True) + EPS) * (subw_ref[...] * out_scale)
    o_ref[...] = o.T.astype(BF16)


def _attn_call(scal, qt, kb, vt, btab, subw, *, bq, bk, out_scale):
    b, nq = qt.shape[:2]
    tk = kb.shape[1]
    nkb = vt.shape[1]
    kern = functools.partial(_attn_kernel, bq=bq, bk=bk, out_scale=out_scale)
    return pl.pallas_call(
        kern,
        grid=(DA_HEADS, b, nq),
        in_specs=[pl.BlockSpec(memory_space=pltpu.SMEM),
                  pl.BlockSpec((None, None, DA_DV, bq), lambda h, i, j: (i, j, h, 0)),
                  pl.BlockSpec((None, tk, DA_DV), lambda h, i, j: (i, 0, h)),
                  pl.BlockSpec((None, nkb, DA_DV, bk), lambda h, i, j: (i, 0, h, 0)),
                  pl.BlockSpec((None, 2, 1, bq + bk), lambda h, i, j: (h, 0, 0, 0)),
                  pl.BlockSpec((DA_DV, 1), lambda h, i, j: (0, 0))],
        out_specs=pl.BlockSpec((None, bq, DA_DV), lambda h, i, j: (i, j, h)),
        out_shape=jax.ShapeDtypeStruct((b, nq * bq, DA_WIDTH), BF16),
        scratch_shapes=[pltpu.VMEM((2, 1, bq), F32), pltpu.VMEM((2, 1, bq), F32),
                        pltpu.VMEM((2, DA_DV, bq), F32),
                        pltpu.VMEM((2, 2, bk, bq), F32), pltpu.VMEM((2, 2, bk, bq), F32),
                        pltpu.VMEM((2, 1, bk, bq), F32),
                        pltpu.VMEM((2, 2 * bk, bq), BF16), pltpu.VMEM((2, 2 * bk, bq), BF16),
                        pltpu.VMEM((2, 1, bq), F32),
                        pltpu.VMEM((2, DA_DV, bq), BF16),
                        pltpu.VMEM((2, bk, bq), F32)],
        compiler_params=pltpu.CompilerParams(dimension_semantics=("arbitrary", "arbitrary", "arbitrary"),
                                             vmem_limit_bytes=VMEM_LIMIT),
        name="attn",
    )(scal, qt, kb, vt, btab, subw)


def _attn_cached_kernel(scal_ref, qt_ref, kc_ref, vc_ref, kn_ref, vtn_ref, bias0_ref, bias1_ref, subw_ref, o_ref,
                        *, bq, tq, bk, out_scale):
    lam = scal_ref[0]
    past = kc_ref.shape[0] // DA_HEADS
    lane = lax.broadcasted_iota(jnp.int32, (tq, DA_DV), 1)
    nt = (((1,), (1,)), ((), ()))

    for h in range(DA_HEADS):
        hs = slice(h * DA_DV, (h + 1) * DA_DV)
        cfar = scal_ref[1 + h]
        qn = qt_ref[hs, :].astype(F32).T[0:tq, :]
        q2 = jnp.concatenate([jnp.where(lane < DA_DK, qn, 0.0), jnp.where(lane >= DA_DK, qn, 0.0)],
                             axis=0).astype(BF16)
        head_rows = pl.ds(h, past, stride=DA_HEADS)
        s_c = lax.dot_general(q2, kc_ref[head_rows, :].astype(BF16), nt, preferred_element_type=F32)
        s_n = lax.dot_general(q2, kn_ref[:, hs], nt, preferred_element_type=F32)
        b0 = bias0_ref[h]
        b1 = bias1_ref[h]
        s = jnp.concatenate([s_c[:, 0:past - bk] + cfar,
                             s_c[:, past - bk:past] + jnp.concatenate([b0, b0], axis=0),
                             s_n + jnp.concatenate([b1, b1], axis=0)], axis=1)
        p = jnp.exp2(s - jnp.max(s, axis=1, keepdims=True))
        inv = 1.0 / jnp.sum(p, axis=1, keepdims=True)
        pb = p.astype(BF16)
        acc = (jnp.dot(pb[:, 0:past], vc_ref[head_rows, :].astype(BF16), preferred_element_type=F32)
               + jnp.dot(pb[:, past:past + bq], vtn_ref[hs, :].astype(F32).T.astype(BF16),
                         preferred_element_type=F32))
        o = acc[0:tq, :] * inv[0:tq] - acc[tq:2 * tq, :] * (lam * inv[tq:2 * tq])
        o_ref[0:tq, hs] = (_rms(o, subw_ref[...]) * out_scale).astype(BF16)
        o_ref[tq:bq, hs] = jnp.zeros((bq - tq, DA_DV), BF16)


def _attn_cached_call(scal, qt, cache_k, cache_v, kb, vt, bias0, bias1, subw, *, tq, bk, out_scale):
    b, past = cache_k.shape[:2]
    bq = qt.shape[-1]
    rows = past * DA_HEADS
    kern = functools.partial(_attn_cached_kernel, bq=bq, tq=tq, bk=bk, out_scale=out_scale)
    cache_spec = pl.BlockSpec((None, rows, DA_DV), lambda i: (i, 0, 0))
    return pl.pallas_call(
        kern,
        grid=(b,),
        in_specs=[pl.BlockSpec(memory_space=pltpu.SMEM),
                  pl.BlockSpec((None, None, DA_WIDTH, bq), lambda i: (i, 0, 0, 0)),
                  cache_spec, cache_spec,
                  pl.BlockSpec((None, bq, DA_WIDTH), lambda i: (i, 0, 0)),
                  pl.BlockSpec((None, None, DA_WIDTH, bq), lambda i: (i, 0, 0, 0)),
                  _const_spec((DA_HEADS, tq, bk)), _const_spec((DA_HEADS, tq, bq)),
                  _const_spec((1, DA_DV))],
        out_specs=pl.BlockSpec((None, bq, DA_WIDTH), lambda i: (i, 0, 0)),
        out_shape=jax.ShapeDtypeStruct((b, bq, DA_WIDTH), BF16),
        compiler_params=pltpu.CompilerParams(dimension_semantics=("parallel",), vmem_limit_bytes=VMEM_LIMIT),
        name="attn_cached",
    )(scal, qt, cache_k.reshape(b, rows, DA_DV), cache_v.reshape(b, rows, DA_DV), kb, vt, bias0, bias1, subw)


def _rel_bucket(rel):
    nb = REL_BUCKETS // 2
    max_exact = nb // 2
    n = jnp.abs(rel)
    nf = jnp.maximum(n, 1).astype(jnp.float32)
    large = max_exact + (jnp.log(nf / max_exact) / math.log(REL_MAX_DIST / max_exact)
                         * (nb - max_exact)).astype(jnp.int32)
    large = jnp.minimum(large, nb - 1)
    return jnp.where(rel > 0, nb, 0) + jnp.where(n < max_exact, n, large)


def _bias_tables(rel_bias, bq, bk):
    tabs = []
    for d in range(2):
        offs = (d - 1) * bk + bk - np.arange(bq + bk)
        tabs.append(rel_bias[_rel_bucket(jnp.asarray(offs, jnp.int32))].astype(F32).T * LOG2E)
    return jnp.stack(tabs, axis=1)[:, :, None, :]


def _bias_rows(rel_bias, qpos0, tq, kpos0, nkeys, tk_real):
    offs = (kpos0 - qpos0) + np.arange(-(tq - 1), nkeys)
    table = rel_bias[_rel_bucket(jnp.asarray(offs, jnp.int32))].astype(F32).T * LOG2E
    rows = jnp.stack([table[:, tq - 1 - i:tq - 1 - i + nkeys] for i in range(tq)], axis=1)
    qpos = qpos0 + np.arange(tq)
    kpos = kpos0 + np.arange(nkeys)
    vis = (kpos[None, :] // CHUNK <= qpos[:, None] // CHUNK) & (kpos[None, :] < tk_real)
    return jnp.where(jnp.asarray(vis)[None], rows, NEG)


def _ffn_kernel(x_ref, ys_ref, ya_ref, mod_ref, wo1_ref, wo2_ref, nfw_ref, wu_ref, cw_ref, cb_ref, wd_ref, fw_ref,
                hist_ref, y_ref, tail_ref, tail_scr, buf_a, buf_b, x1_scr, h2_scr, act_scr, *, nseq, tm):
    t = pl.program_id(1)
    nh = FFN_CONV - 1
    lo = SUBLANES - nh
    seqs = range(nseq)

    def rows(i):
        return slice(i * tm, (i + 1) * tm)

    def stacked(ref):
        return jnp.concatenate([ref[i] for i in seqs], axis=0) if nseq > 1 else ref[0]

    @pl.when(t == 0)
    def _init():
        for i in seqs:
            tail_scr[i, lo:SUBLANES, :] = hist_ref[i]

    mix = (jnp.dot(stacked(ys_ref), wo1_ref[...], preferred_element_type=F32)
           + jnp.dot(stacked(ya_ref), wo2_ref[...], preferred_element_type=F32))
    for i in seqs:
        x1 = x_ref[i] + mod_ref[i, 2:3, :] * mix[rows(i)]
        x1_scr[rows(i), :] = x1
        h2 = _rms(x1, nfw_ref[...]) * (1.0 + mod_ref[i, 4:5, :]) + mod_ref[i, 3:4, :]
        h2_scr[rows(i), :] = h2.astype(BF16)

    def cols(j, half):
        return slice(half * D_FF + j * FFN_CN, half * D_FF + (j + 1) * FFN_CN)

    def up(j, buf):
        for half in range(2):
            u = jnp.dot(h2_scr[...], wu_ref[:, cols(j, half)], preferred_element_type=F32)
            for i in seqs:
                buf[half, i, SUBLANES:SUBLANES + tm, :] = u[rows(i)]

    def conv(j, buf, half, i):
        cs = cols(j, half)
        buf[half, i, lo:SUBLANES, :] = tail_scr[i, lo:SUBLANES, cs]
        c = cb_ref[:, cs]
        for k in range(FFN_CONV):
            c = c + cw_ref[k:k + 1, cs] * buf[half, i, lo + k:lo + k + tm, :]
        tail_scr[i, lo:SUBLANES, cs] = buf[half, i, lo + tm:SUBLANES + tm, :]
        return c

    bufs = (buf_a, buf_b)

    @pl.when(t >= 0)
    def _chunks():
        up(0, bufs[0])
        for j in range(FFN_NC):
            if j + 1 < FFN_NC:
                up(j + 1, bufs[(j + 1) % 2])
            for i in seqs:
                cv = conv(j, bufs[j % 2], 0, i)
                cg = conv(j, bufs[j % 2], 1, i)
                act_scr[rows(i), j * FFN_CN:(j + 1) * FFN_CN] = (_silu(cg) * cv).astype(BF16)

    f = jnp.dot(act_scr[...], wd_ref[...], preferred_element_type=F32)
    for i in seqs:
        x2 = x1_scr[rows(i), :] + mod_ref[i, 5:6, :] * f[rows(i)]
        y_ref[i] = _rms(x2, fw_ref[...])
        tail_ref[i] = tail_scr[i, lo:SUBLANES, :]


def _ffn_call(x, ys, ya, mod3, wo1, wo2, nfw, wu, cw, cb, wd, fw, hist, *, nseq, tm):
    b, t, d = x.shape
    nh = FFN_CONV - 1
    m = nseq * tm

    def row(width):
        return pl.BlockSpec((nseq, tm, width), lambda i, j: (i, j, 0))

    hist_spec = pl.BlockSpec((nseq, nh, 2 * D_FF), lambda i, j: (i, 0, 0))
    kern = functools.partial(_ffn_kernel, nseq=nseq, tm=tm)
    return pl.pallas_call(
        kern,
        grid=(b // nseq, t // tm),
        in_specs=[row(d), row(SSD_WIDTH), row(DA_WIDTH),
                  pl.BlockSpec((nseq, 6, d), lambda i, j: (i, 0, 0)),
                  _const_spec((SSD_WIDTH, d)), _const_spec((DA_WIDTH, d)), _const_spec((1, d)),
                  _const_spec((d, 2 * D_FF)), _const_spec((FFN_CONV, 2 * D_FF)), _const_spec((1, 2 * D_FF)),
                  _const_spec((D_FF, d)), _const_spec((1, d)),
                  hist_spec],
        out_specs=[row(d), hist_spec],
        out_shape=[jax.ShapeDtypeStruct((b, t, d), F32),
                   jax.ShapeDtypeStruct((b, nh, 2 * D_FF), F32)],
        scratch_shapes=[pltpu.VMEM((nseq, SUBLANES, 2 * D_FF), F32),
                        pltpu.VMEM((2, nseq, tm + SUBLANES, FFN_CN), F32),
                        pltpu.VMEM((2, nseq, tm + SUBLANES, FFN_CN), F32),
                        pltpu.VMEM((m, d), F32), pltpu.VMEM((m, d), BF16), pltpu.VMEM((m, D_FF), BF16)],
        compiler_params=pltpu.CompilerParams(dimension_semantics=("parallel", "arbitrary"),
                                             vmem_limit_bytes=VMEM_LIMIT),
        name="ffn",
    )(x, ys, ya, mod3, wo1, wo2, nfw, wu, cw, cb, wd, fw, hist)


def _pack_params(norm_mix_w, w_in, ssm_conv_w, ssm_conv_b, ssm_dt_bias, ssm_a_log, ssm_d, ssm_norm_w,
                 lambda_q1, lambda_k1, lambda_q2, lambda_k2, attn_subln_w, rel_bias, w_out,
                 norm_ffn_w, w_up, ffn_conv_w, ffn_conv_b, w_down, final_norm_w, layer):
    l = layer
    wz, wx, wdt, wq, wk, wv = jnp.split(w_in[l], IN_SPLITS, axis=-1)
    wdt = jnp.pad(wdt, ((0, 0), (0, LANES - SSD_HEADS)))
    w_cat = jnp.concatenate([wz, wx, wdt, wk, wv], axis=-1).astype(BF16)
    w_t = (wq * (DA_DK ** -0.5 * LOG2E)).T.astype(BF16)

    def pad_heads(v):
        return jnp.pad(v.astype(F32), (0, LANES - SSD_HEADS)).reshape(1, LANES)

    lam_init = 0.8 - 0.6 * math.exp(-0.3 * l)
    lam = (jnp.exp(jnp.sum(lambda_q1[l].astype(F32) * lambda_k1[l].astype(F32)))
           - jnp.exp(jnp.sum(lambda_q2[l].astype(F32) * lambda_k2[l].astype(F32))) + lam_init)
    far_bias = rel_bias[REL_BUCKETS // 2 - 1].astype(F32)
    return dict(
        norm_mix_w=norm_mix_w[l].reshape(1, D_MODEL), w_cat=w_cat, w_t=w_t,
        cw=ssm_conv_w[l], cbias=ssm_conv_b[l].reshape(1, SSD_CONV_DIM),
        dtb=pad_heads(ssm_dt_bias[l]), alog=pad_heads(ssm_a_log[l]),
        dsk=jnp.repeat(ssm_d[l].astype(F32), SSD_HEADDIM).reshape(1, SSD_WIDTH),
        ssm_nw=ssm_norm_w[l].reshape(1, SSD_WIDTH),
        scal=jnp.concatenate([lam.reshape(1), far_bias * LOG2E]).astype(F32), lam_init=lam_init,
        subw=attn_subln_w[l].reshape(DA_DV, 1), rel_bias=rel_bias,
        wo1=w_out[l][:SSD_WIDTH].astype(BF16), wo2=w_out[l][SSD_WIDTH:].astype(BF16),
        nfw=norm_ffn_w[l].reshape(1, D_MODEL),
        wu=w_up[l].astype(BF16), ffn_cw=ffn_conv_w[l], ffn_cb=ffn_conv_b[l].reshape(1, 2 * D_FF),
        wd=w_down[l].astype(BF16), fw=final_norm_w.reshape(1, D_MODEL),
    )


def _state_to_kernel(h):
    return h.reshape(h.shape[0], SSD_GROUPS, GROUP_W, SSD_STATE)


def _state_from_kernel(h):
    return h.reshape(h.shape[0], SSD_HEADS, SSD_HEADDIM, SSD_STATE)


def _run_group(x, mod, past_k, past_v, ssm_h0, ssm_conv_hist, ffn_conv_hist, p, *, tm, ssd_rows, bq, bk):
    b, t, d = x.shape
    past = 0 if past_k is None else past_k.shape[1]
    chunk = min(CHUNK, t)
    tp = max(t, SUPER)
    if tp != t:
        x = jnp.pad(x, ((0, 0), (0, tp - t), (0, 0)))
        tm = ssd_rows = bq = tp
    mod3 = mod.reshape(b, 6, d)

    assert tm == bq
    zs, xc, dt, k, v, kb, qt, vt, conv_new = _inproj_call(
        x, mod3, p["norm_mix_w"], p["w_cat"], p["w_t"], ssm_conv_hist.astype(F32), p["cw"], p["cbias"],
        tm=tm, real=min(t, tm))

    y_ssd, h_t = _ssd_call(zs, xc, dt, _state_to_kernel(ssm_h0.astype(F32)),
                           p["dtb"], p["alog"], p["dsk"], p["ssm_nw"],
                           chunk=chunk, rows=ssd_rows, real=min(t, ssd_rows))

    if past == 0:
        assert bq == bk and t % bq == 0
        y_att = _attn_call(p["scal"], qt, kb, vt, _bias_tables(p["rel_bias"], bq, bk), p["subw"], bq=bq, bk=bk,
                           out_scale=1.0 - p["lam_init"])
    else:
        assert past % bk == 0 and past >= bk and bq == tp <= bk
        bias0 = _bias_rows(p["rel_bias"], past, t, past - bk, bk, past + t)
        bias1 = _bias_rows(p["rel_bias"], past, t, past, bq, past + t)
        y_att = _attn_cached_call(p["scal"], qt, past_k, past_v, kb, vt, bias0, bias1, p["subw"].reshape(1, DA_DV),
                                  tq=t, bk=bk, out_scale=1.0 - p["lam_init"])

    tf = min(t, tm)
    nseq = math.gcd(b, max(1, TILE_ROWS // tf))
    y, ffn_new = _ffn_call(x[:, :t], y_ssd[:, :t], y_att[:, :t], mod3, p["wo1"], p["wo2"], p["nfw"], p["wu"],
                           p["ffn_cw"], p["ffn_cb"], p["wd"], p["fw"], ffn_conv_hist.astype(F32),
                           nseq=nseq, tm=tf)
    k = k.reshape(b, tp, DA_HEADS, 2 * DA_DK)
    v = v.reshape(b, tp, DA_HEADS, DA_DV)
    return (y, k[:, :t], v[:, :t],
            _state_from_kernel(h_t), conv_new, ffn_new)


def kernel(x_prompt, x_sample, c_prompt, c_sample, cache_k, cache_v, state_ssm, state_ssm_conv, state_ffn_conv, w_ada, b_ada, norm_mix_w, w_in, ssm_conv_w, ssm_conv_b, ssm_dt_bias, ssm_a_log, ssm_d, ssm_norm_w, lambda_q1, lambda_k1, lambda_q2, lambda_k2, attn_subln_w, rel_bias, w_out, norm_ffn_w, w_up, ffn_conv_w, ffn_conv_b, w_down, final_norm_w):
    bp, bs = x_prompt.shape[0], x_sample.shape[0]
    dt = x_prompt.dtype
    p = _pack_params(norm_mix_w, w_in, ssm_conv_w, ssm_conv_b, ssm_dt_bias, ssm_a_log, ssm_d, ssm_norm_w,
                     lambda_q1, lambda_k1, lambda_q2, lambda_k2, attn_subln_w, rel_bias, w_out,
                     norm_ffn_w, w_up, ffn_conv_w, ffn_conv_b, w_down, final_norm_w, 0)
    c_all = jnp.concatenate([c_prompt, c_sample], axis=0)
    npad = -c_all.shape[0] % SUBLANES
    c_all = jnp.pad(c_all, ((0, npad), (0, 0)))
    mod = _mod_call(c_all, w_ada[0], b_ada[0].reshape(1, -1))

    zeros = lambda *s: jnp.zeros(s, dt)
    out_p = _run_group(x_prompt, mod[:bp], None, None,
                       zeros(bp, SSD_HEADS, SSD_HEADDIM, SSD_STATE), zeros(bp, SSD_CONV - 1, SSD_CONV_DIM),
                       zeros(bp, FFN_CONV - 1, 2 * D_FF), p,
                       tm=TILE_ROWS, ssd_rows=SSD_ROWS, bq=ATT_BLOCK, bk=ATT_BLOCK)
    out_s = _run_group(x_sample, mod[bp:bp + bs], cache_k[0], cache_v[0], state_ssm[0], state_ssm_conv[0],
                       state_ffn_conv[0], p, tm=SUPER, ssd_rows=SUPER, bq=SUPER, bk=ATT_BLOCK)
    y_p, k_p, v_p, h_p, c_p, f_p = out_p
    y_s, k_s, v_s, h_s, c_s, f_s = out_s
    return (y_p, y_s, k_p[None], v_p[None], h_p[None], c_p[None], f_p[None],
            k_s[None], v_s[None], h_s[None], c_s[None], f_s[None])
```
